```python
import math
import jax, jax.numpy as jnp
from jax import lax
import numpy as np

D_MODEL = 1024
BATCH = 8
SEQ = 8192
DEPTH = 1

HEAD_DIM = 64
N_ATT_HEADS = 12
ATT_WIDTH = N_ATT_HEADS * HEAD_DIM
DILATED_PATTERNS = ((128, 1), (512, 4), (2048, 16))
ATT_BLOCK = 128
SSM_EXPAND = 2
SSM_INNER = SSM_EXPAND * D_MODEL
SSM_HEAD_DIM = 64
SSM_HEADS = SSM_INNER // SSM_HEAD_DIM
SSM_GROUPS = 8
SSM_STATE = 128
SSM_CONV = 4
SSM_CHUNK = 128
CONV_DIM = SSM_INNER + 2 * SSM_GROUPS * SSM_STATE
FFN_HIDDEN = 4 * D_MODEL
N_BRANCHES = 2
IN_SPLITS = (ATT_WIDTH, ATT_WIDTH, ATT_WIDTH, SSM_INNER, CONV_DIM, SSM_HEADS, N_BRANCHES * D_MODEL)
IN_PROJ_WIDTH = sum(IN_SPLITS)
RMS_EPS = 1e-6

kernel_name = "hybrid_dilated_attn_mamba2_gated_block"


def rmsnorm(x, w):
    x32 = x.astype(jnp.float32)
    y = x32 * lax.rsqrt(jnp.mean(x32 * x32, axis=-1, keepdims=True) + RMS_EPS)
    return (y * w.astype(jnp.float32)).astype(x.dtype)


def alibi_slopes(n):
    def pow2(m):
        start = 2.0 ** (-8.0 / m)
        return [start ** (i + 1) for i in range(m)]
    if (n & (n - 1)) == 0:
        s = pow2(n)
    else:
        c = 2 ** int(math.floor(math.log2(n)))
        s = pow2(c) + pow2(2 * c)[0::2][: n - c]
    return jnp.asarray(np.array(s, dtype=np.float32))


def dilated_window_attention(q, k, v, slopes, window, dilation):
    b, s, h, dh = q.shape
    band = window // dilation
    blk = ATT_BLOCK
    span = dilation * blk
    s_pad = -(-s // span) * span
    pad = ((0, 0), (0, s_pad - s), (0, 0), (0, 0))
    q, k, v = [jnp.pad(t, pad) for t in (q, k, v)]
    nb = s_pad // span
    qb = q.reshape(b, nb, blk, dilation, h, dh)
    kb = k.reshape(b, nb, blk, dilation, h, dh)
    vb = v.reshape(b, nb, blk, dilation, h, dh)

    def with_prev(t):
        prev = jnp.concatenate([jnp.zeros_like(t[:, :1]), t[:, :-1]], axis=1)
        return jnp.concatenate([prev, t], axis=2)

    kk, vv = with_prev(kb), with_prev(vb)
    scores = jnp.einsum('bnirhd,bnjrhd->bnrhij', qb, kk,
                        preferred_element_type=jnp.float32) * (dh ** -0.5)
    i_idx = jnp.arange(blk)[:, None]
    j_idx = jnp.arange(2 * blk)[None, :]
    dist = blk + i_idx - j_idx
    n_idx = jnp.arange(nb)[:, None, None]
    valid = (dist >= 0) & (dist <= band) & ((n_idx > 0) | (j_idx >= blk))
    bias = -slopes[:, None, None] * (dist * dilation).astype(jnp.float32)
    scores = scores + bias[None, None, None]
    scores = jnp.where(valid[None, :, None, None], scores, -jnp.inf)
    m = jnp.max(scores, axis=-1, keepdims=True)
    p = jnp.exp(scores - m)
    l = jnp.sum(p, axis=-1)
    o = jnp.einsum('bnrhij,bnjrhd->bnirhd', p, vv.astype(jnp.float32))
    l_t = jnp.transpose(l, (0, 1, 4, 2, 3))
    m_t = jnp.transpose(m[..., 0], (0, 1, 4, 2, 3))
    o = o / l_t[..., None]
    o = o.reshape(b, s_pad, h, dh)[:, :s]
    return o, m_t.reshape(b, s_pad, h)[:, :s], l_t.reshape(b, s_pad, h)[:, :s]


def ssd_chunked(x, dt, a, bmat, cmat):
    b, s, g, hg, p = x.shape
    n = bmat.shape[-1]
    ch = SSM_CHUNK
    s_pad = -(-s // ch) * ch
    padlen = s_pad - s
    x = jnp.pad(x.astype(jnp.float32), ((0, 0), (0, padlen), (0, 0), (0, 0), (0, 0)))
    dt = jnp.pad(dt, ((0, 0), (0, padlen), (0, 0), (0, 0)))
    bmat = jnp.pad(bmat.astype(jnp.float32), ((0, 0), (0, padlen), (0, 0), (0, 0)))
    cmat = jnp.pad(cmat.astype(jnp.float32), ((0, 0), (0, padlen), (0, 0), (0, 0)))
    c = s_pad // ch
    xd = (x * dt[..., None]).reshape(b, c, ch, g, hg, p)
    la = jnp.moveaxis((dt * a).reshape(b, c, ch, g, hg), 2, -1)
    bc = bmat.reshape(b, c, ch, g, n)
    cc = cmat.reshape(b, c, ch, g, n)
    a_cs = jnp.cumsum(la, axis=-1)
    tril = jnp.tril(jnp.ones((ch, ch), dtype=bool))
    decay = jnp.exp(jnp.where(tril, a_cs[..., :, None] - a_cs[..., None, :], -jnp.inf))
    cb = jnp.einsum('bclgn,bcsgn->bcgls', cc, bc)
    y_diag = jnp.einsum('bcghls,bcsghp->bclghp', cb[:, :, :, None] * decay, xd)
    decay_states = jnp.exp(a_cs[..., -1:] - a_cs)
    states = jnp.einsum('bclgn,bcghl,bclghp->bcghpn', bc, decay_states, xd)
    chunk_decay = jnp.exp(a_cs[..., -1])

    def step(hstate, inp):
        st, dec = inp
        return hstate * dec[..., None, None] + st, hstate

    init = jnp.zeros((b, g, hg, p, n), jnp.float32)
    _, prev = lax.scan(step, init, (jnp.moveaxis(states, 1, 0), jnp.moveaxis(chunk_decay, 1, 0)))
    prev = jnp.moveaxis(prev, 0, 1)
    y_off = jnp.einsum('bclgn,bcghpn,bcghl->bclghp', cc, prev, jnp.exp(a_cs))
    return (y_diag + y_off).reshape(b, s_pad, g, hg, p)[:, :s]


def mamba2_mixer(z, xbc, dt_raw, conv_w, conv_b, dt_bias, a_log, d_skip, norm_w):
    b, s, _ = z.shape
    hg = SSM_HEADS // SSM_GROUPS
    xbc = lax.conv_general_dilated(xbc, conv_w[:, None, :], window_strides=(1,),
                                   padding=[(SSM_CONV - 1, 0)],
                                   dimension_numbers=('NWC', 'WIO', 'NWC'),
                                   feature_group_count=CONV_DIM) + conv_b
    xbc = jax.nn.silu(xbc)
    xs, bm, cm = jnp.split(xbc, [SSM_INNER, SSM_INNER + SSM_GROUPS * SSM_STATE], axis=-1)
    xh = xs.reshape(b, s, SSM_GROUPS, hg, SSM_HEAD_DIM)
    bm = bm.reshape(b, s, SSM_GROUPS, SSM_STATE)
    cm = cm.reshape(b, s, SSM_GROUPS, SSM_STATE)
    dt = jax.nn.softplus((dt_raw + dt_bias).astype(jnp.float32)).reshape(b, s, SSM_GROUPS, hg)
    a = -jnp.exp(a_log.astype(jnp.float32)).reshape(SSM_GROUPS, hg)
    y = ssd_chunked(xh, dt, a, bm, cm)
    y = y + d_skip.astype(jnp.float32).reshape(SSM_GROUPS, hg)[:, :, None] * xh.astype(jnp.float32)
    y = y.reshape(b, s, SSM_INNER) * jax.nn.silu(z.astype(jnp.float32))
    yg = y.reshape(b, s, SSM_GROUPS, SSM_INNER // SSM_GROUPS)
    yg = yg * lax.rsqrt(jnp.mean(yg * yg, axis=-1, keepdims=True) + RMS_EPS)
    y = yg.reshape(b, s, SSM_INNER) * norm_w.astype(jnp.float32)
    return y.astype(z.dtype)


def _fwd_setup_inputs(seed: int = 0) -> dict:
    key = jax.random.key(seed)
    ks = jax.random.split(key, 20)
    f32 = jnp.float32

    def nrm(k, shape, scale):
        return jax.random.normal(k, shape, f32) * scale

    def gain(k, width):
        return 1.0 + 0.05 * jax.random.normal(k, (DEPTH, width), f32)

    dt0 = jnp.exp(jax.random.uniform(ks[7], (DEPTH, SSM_HEADS), f32,
                                     math.log(1e-3), math.log(1e-1)))
    return {
        "x": jax.random.normal(ks[0], (BATCH, SEQ, D_MODEL), f32),
        "norm_mix_pre_w": gain(ks[1], D_MODEL),
        "w_in": nrm(ks[2], (DEPTH, D_MODEL, IN_PROJ_WIDTH), D_MODEL ** -0.5),
        "b_gate": nrm(ks[3], (DEPTH, N_BRANCHES * D_MODEL), 0.02),
        "conv_w": nrm(ks[4], (DEPTH, SSM_CONV, CONV_DIM), SSM_CONV ** -0.5),
        "conv_b": nrm(ks[5], (DEPTH, CONV_DIM), 0.02),
        "dt_bias": dt0 + jnp.log(-jnp.expm1(-dt0)),
        "a_log": jnp.log(jax.random.uniform(ks[8], (DEPTH, SSM_HEADS), f32, 1.0, 16.0)),
        "d_skip": 1.0 + 0.1 * jax.random.normal(ks[9], (DEPTH, SSM_HEADS), f32),
        "ssm_norm_w": gain(ks[10], SSM_INNER),
        "w_att_proj": nrm(ks[11], (DEPTH, ATT_WIDTH, D_MODEL), ATT_WIDTH ** -0.5),
        "w_ssm_proj": nrm(ks[12], (DEPTH, SSM_INNER, D_MODEL), SSM_INNER ** -0.5),
        "w_out": nrm(ks[13], (DEPTH, D_MODEL, D_MODEL), D_MODEL ** -0.5),
        "norm_mix_post_w": gain(ks[14], D_MODEL),
        "norm_ffn_pre_w": gain(ks[15], D_MODEL),
        "w_up": nrm(ks[16], (DEPTH, D_MODEL, FFN_HIDDEN), D_MODEL ** -0.5),
        "w_down": nrm(ks[17], (DEPTH, FFN_HIDDEN, D_MODEL), FFN_HIDDEN ** -0.5),
        "norm_ffn_post_w": gain(ks[18], D_MODEL),
    }


def _fwd_reference(x, norm_mix_pre_w, w_in, b_gate, conv_w, conv_b, dt_bias, a_log, d_skip,
              ssm_norm_w, w_att_proj, w_ssm_proj, w_out, norm_mix_post_w, norm_ffn_pre_w,
              w_up, w_down, norm_ffn_post_w):
    b, s, _ = x.shape
    slopes = alibi_slopes(N_ATT_HEADS)
    offsets = [int(o) for o in np.cumsum(IN_SPLITS)[:-1]]
    h = x
    for layer in range(DEPTH):
        u = rmsnorm(h, norm_mix_pre_w[layer])
        proj = u @ w_in[layer]
        q, k, v, z, xbc, dt_raw, gate_logits = jnp.split(proj, offsets, axis=-1)
        q = q.reshape(b, s, N_ATT_HEADS, HEAD_DIM)
        k = k.reshape(b, s, N_ATT_HEADS, HEAD_DIM)
        v = v.reshape(b, s, N_ATT_HEADS, HEAD_DIM)
        outs, maxes, dens = [], [], []
        for window, dilation in DILATED_PATTERNS:
            o_g, m_g, l_g = dilated_window_attention(q, k, v, slopes, window, dilation)
            outs.append(o_g)
            maxes.append(m_g)
            dens.append(l_g)
        m_all = jnp.stack(maxes)
        wts = jnp.exp(m_all - jnp.max(m_all, axis=0, keepdims=True)) * jnp.stack(dens)
        att = jnp.sum(wts[..., None] * jnp.stack(outs), axis=0) / jnp.sum(wts, axis=0)[..., None]
        att = att.reshape(b, s, ATT_WIDTH).astype(x.dtype) @ w_att_proj[layer]
        ssm = mamba2_mixer(z, xbc, dt_raw, conv_w[layer], conv_b[layer], dt_bias[layer],
                           a_log[layer], d_skip[layer], ssm_norm_w[layer])
        ssm = ssm @ w_ssm_proj[layer]
        gates = jax.nn.sigmoid(gate_logits + b_gate[layer])
        g_att, g_ssm = jnp.split(gates, 2, axis=-1)
        mixed = (g_att * att + g_ssm * ssm) @ w_out[layer]
        h = h + rmsnorm(mixed, norm_mix_post_w[layer])
        f = rmsnorm(h, norm_ffn_pre_w[layer])
        f = jnp.square(jax.nn.relu(f @ w_up[layer])) @ w_down[layer]
        h = h + rmsnorm(f, norm_ffn_post_w[layer])
    return h


import jax as _jax
import jax.numpy as _jnp

TWIN_FORMAT = 'train_step'
FWD_PARAMS = ['x', 'norm_mix_pre_w', 'w_in', 'b_gate', 'conv_w', 'conv_b', 'dt_bias', 'a_log', 'd_skip', 'ssm_norm_w', 'w_att_proj', 'w_ssm_proj', 'w_out', 'norm_mix_post_w', 'norm_ffn_pre_w', 'w_up', 'w_down', 'norm_ffn_post_w']
TWIN_WEIGHTS = ['norm_mix_pre_w', 'w_in', 'b_gate', 'conv_w', 'conv_b', 'dt_bias', 'a_log', 'd_skip', 'ssm_norm_w', 'w_att_proj', 'w_ssm_proj', 'w_out', 'norm_mix_post_w', 'norm_ffn_pre_w', 'w_up', 'w_down', 'norm_ffn_post_w']
TWIN_DIFF_INPUT = 'x'
TWIN_INPUTS = ['x', 'norm_mix_pre_w', 'w_in', 'b_gate', 'conv_w', 'conv_b', 'dt_bias', 'a_log', 'd_skip', 'ssm_norm_w', 'w_att_proj', 'w_ssm_proj', 'w_out', 'norm_mix_post_w', 'norm_ffn_pre_w', 'w_up', 'w_down', 'norm_ffn_post_w', 'loss_target', 'm_norm_mix_pre_w', 'm_w_in', 'm_b_gate', 'm_conv_w', 'm_conv_b', 'm_dt_bias', 'm_a_log', 'm_d_skip', 'm_ssm_norm_w', 'm_w_att_proj', 'm_w_ssm_proj', 'm_w_out', 'm_norm_mix_post_w', 'm_norm_ffn_pre_w', 'm_w_up', 'm_w_down', 'm_norm_ffn_post_w', 'v_norm_mix_pre_w', 'v_w_in', 'v_b_gate', 'v_conv_w', 'v_conv_b', 'v_dt_bias', 'v_a_log', 'v_d_skip', 'v_ssm_norm_w', 'v_w_att_proj', 'v_w_ssm_proj', 'v_w_out', 'v_norm_mix_post_w', 'v_norm_ffn_pre_w', 'v_w_up', 'v_w_down', 'v_norm_ffn_post_w']
TWIN_OUTPUTS = ['loss', 'grad_x', 'grad_norm_mix_pre_w', 'grad_w_in', 'grad_b_gate', 'grad_conv_w', 'grad_conv_b', 'grad_dt_bias', 'grad_a_log', 'grad_d_skip', 'grad_ssm_norm_w', 'grad_w_att_proj', 'grad_w_ssm_proj', 'grad_w_out', 'grad_norm_mix_post_w', 'grad_norm_ffn_pre_w', 'grad_w_up', 'grad_w_down', 'grad_norm_ffn_post_w', 'delta_norm_mix_pre_w', 'delta_w_in', 'delta_b_gate', 'delta_conv_w', 'delta_conv_b', 'delta_dt_bias', 'delta_a_log', 'delta_d_skip', 'delta_ssm_norm_w', 'delta_w_att_proj', 'delta_w_ssm_proj', 'delta_w_out', 'delta_norm_mix_post_w', 'delta_norm_ffn_pre_w', 'delta_w_up', 'delta_w_down', 'delta_norm_ffn_post_w', 'new_m_norm_mix_pre_w', 'new_m_w_in', 'new_m_b_gate', 'new_m_conv_w', 'new_m_conv_b', 'new_m_dt_bias', 'new_m_a_log', 'new_m_d_skip', 'new_m_ssm_norm_w', 'new_m_w_att_proj', 'new_m_w_ssm_proj', 'new_m_w_out', 'new_m_norm_mix_post_w', 'new_m_norm_ffn_pre_w', 'new_m_w_up', 'new_m_w_down', 'new_m_norm_ffn_post_w', 'new_v_norm_mix_pre_w', 'new_v_w_in', 'new_v_b_gate', 'new_v_conv_w', 'new_v_conv_b', 'new_v_dt_bias', 'new_v_a_log', 'new_v_d_skip', 'new_v_ssm_norm_w', 'new_v_w_att_proj', 'new_v_w_ssm_proj', 'new_v_w_out', 'new_v_norm_mix_post_w', 'new_v_norm_ffn_pre_w', 'new_v_w_up', 'new_v_w_down', 'new_v_norm_ffn_post_w']
TWIN_LEAF_KINDS = {'loss': 'loss', 'grad_x': 'grad_x', 'grad_norm_mix_pre_w': 'grad_w', 'grad_w_in': 'grad_w', 'grad_b_gate': 'grad_w', 'grad_conv_w': 'grad_w', 'grad_conv_b': 'grad_w', 'grad_dt_bias': 'grad_w', 'grad_a_log': 'grad_w', 'grad_d_skip': 'grad_w', 'grad_ssm_norm_w': 'grad_w', 'grad_w_att_proj': 'grad_w', 'grad_w_ssm_proj': 'grad_w', 'grad_w_out': 'grad_w', 'grad_norm_mix_post_w': 'grad_w', 'grad_norm_ffn_pre_w': 'grad_w', 'grad_w_up': 'grad_w', 'grad_w_down': 'grad_w', 'grad_norm_ffn_post_w': 'grad_w', 'delta_norm_mix_pre_w': 'delta_w', 'delta_w_in': 'delta_w', 'delta_b_gate': 'delta_w', 'delta_conv_w': 'delta_w', 'delta_conv_b': 'delta_w', 'delta_dt_bias': 'delta_w', 'delta_a_log': 'delta_w', 'delta_d_skip': 'delta_w', 'delta_ssm_norm_w': 'delta_w', 'delta_w_att_proj': 'delta_w', 'delta_w_ssm_proj': 'delta_w', 'delta_w_out': 'delta_w', 'delta_norm_mix_post_w': 'delta_w', 'delta_norm_ffn_pre_w': 'delta_w', 'delta_w_up': 'delta_w', 'delta_w_down': 'delta_w', 'delta_norm_ffn_post_w': 'delta_w', 'new_m_norm_mix_pre_w': 'new_m', 'new_m_w_in': 'new_m', 'new_m_b_gate': 'new_m', 'new_m_conv_w': 'new_m', 'new_m_conv_b': 'new_m', 'new_m_dt_bias': 'new_m', 'new_m_a_log': 'new_m', 'new_m_d_skip': 'new_m', 'new_m_ssm_norm_w': 'new_m', 'new_m_w_att_proj': 'new_m', 'new_m_w_ssm_proj': 'new_m', 'new_m_w_out': 'new_m', 'new_m_norm_mix_post_w': 'new_m', 'new_m_norm_ffn_pre_w': 'new_m', 'new_m_w_up': 'new_m', 'new_m_w_down': 'new_m', 'new_m_norm_ffn_post_w': 'new_m', 'new_v_norm_mix_pre_w': 'new_v', 'new_v_w_in': 'new_v', 'new_v_b_gate': 'new_v', 'new_v_conv_w': 'new_v', 'new_v_conv_b': 'new_v', 'new_v_dt_bias': 'new_v', 'new_v_a_log': 'new_v', 'new_v_d_skip': 'new_v', 'new_v_ssm_norm_w': 'new_v', 'new_v_w_att_proj': 'new_v', 'new_v_w_ssm_proj': 'new_v', 'new_v_w_out': 'new_v', 'new_v_norm_mix_post_w': 'new_v', 'new_v_norm_ffn_pre_w': 'new_v', 'new_v_w_up': 'new_v', 'new_v_w_down': 'new_v', 'new_v_norm_ffn_post_w': 'new_v'}


def _forward(args):
    return _fwd_reference(*[args[k] for k in FWD_PARAMS])


def _output_shape():
    def fwd():
        inp = _fwd_setup_inputs(0)
        return _fwd_reference(*[inp[k] for k in FWD_PARAMS])
    out = _jax.eval_shape(fwd)
    return out.shape, out.dtype

N_MICROBATCH = 1
ADAM_LR = 0.001
ADAM_B1 = 0.9
ADAM_B2 = 0.999
ADAM_EPS = 1e-08
ADAM_WD = 0.01
ADAM_STEP = 10
PER_EXAMPLE_BATCH_AXIS = {'x': 0, 'loss_target': 0}
SHARED_INPUTS = []
_WEIGHT_DTYPES = {'norm_mix_pre_w': _jnp.float32, 'w_in': _jnp.float32, 'b_gate': _jnp.float32, 'conv_w': _jnp.float32, 'conv_b': _jnp.float32, 'dt_bias': _jnp.float32, 'a_log': _jnp.float32, 'd_skip': _jnp.float32, 'ssm_norm_w': _jnp.float32, 'w_att_proj': _jnp.float32, 'w_ssm_proj': _jnp.float32, 'w_out': _jnp.float32, 'norm_mix_post_w': _jnp.float32, 'norm_ffn_pre_w': _jnp.float32, 'w_up': _jnp.float32, 'w_down': _jnp.float32, 'norm_ffn_post_w': _jnp.float32}
MOMENT_SCALE = {'norm_mix_pre_w': 1.224560e+00, 'w_in': 3.535636e-01, 'b_gate': 1.364060e+00, 'conv_w': 1.278314e+00, 'conv_b': 5.043083e+00, 'dt_bias': 7.203962e-01, 'a_log': 8.280541e+00, 'd_skip': 5.624496e+00, 'ssm_norm_w': 2.939272e+00, 'w_att_proj': 5.130359e-01, 'w_ssm_proj': 4.395643e+00, 'w_out': 4.676690e+00, 'norm_mix_post_w': 6.412086e+01, 'norm_ffn_pre_w': 1.876671e+00, 'w_up': 9.705135e-01, 'w_down': 5.031481e+00, 'norm_ffn_post_w': 6.594628e+01}


def _to_microbatches(a, axis):
    t = _jnp.moveaxis(a, axis, 0)
    t = t.reshape((N_MICROBATCH, t.shape[0] // N_MICROBATCH) + t.shape[1:])
    return _jnp.moveaxis(t, 1, axis + 1)


def setup_inputs(seed: int = 0) -> dict:
    inp = _fwd_setup_inputs(seed)
    key = _jax.random.fold_in(_jax.random.key(seed), 7919)
    shape, _ = _output_shape()
    out = dict(inp)
    out["loss_target"] = _jax.random.normal(_jax.random.fold_in(key, 0), shape, _jnp.float32)
    for i, name in enumerate(TWIN_WEIGHTS):
        w = inp[name].astype(_jnp.float32)
        if MOMENT_SCALE is None:
            s = _jnp.sqrt(_jnp.mean(_jnp.square(w)) + 1e-30)
        else:
            s = MOMENT_SCALE[name]
        km, kv = _jax.random.split(_jax.random.fold_in(key, i + 1))
        out[name] = w
        out["m_" + name] = s * _jax.random.normal(km, w.shape, _jnp.float32)
        out["v_" + name] = (s * s) * _jax.random.uniform(kv, w.shape, _jnp.float32, 0.5, 1.5)
    if N_MICROBATCH > 1:
        for name, axis in PER_EXAMPLE_BATCH_AXIS.items():
            out[name] = _to_microbatches(out[name], axis)
    return {'x': out['x'], 'norm_mix_pre_w': out['norm_mix_pre_w'], 'w_in': out['w_in'], 'b_gate': out['b_gate'], 'conv_w': out['conv_w'], 'conv_b': out['conv_b'], 'dt_bias': out['dt_bias'], 'a_log': out['a_log'], 'd_skip': out['d_skip'], 'ssm_norm_w': out['ssm_norm_w'], 'w_att_proj': out['w_att_proj'], 'w_ssm_proj': out['w_ssm_proj'], 'w_out': out['w_out'], 'norm_mix_post_w': out['norm_mix_post_w'], 'norm_ffn_pre_w': out['norm_ffn_pre_w'], 'w_up': out['w_up'], 'w_down': out['w_down'], 'norm_ffn_post_w': out['norm_ffn_post_w'], 'loss_target': out['loss_target'], 'm_norm_mix_pre_w': out['m_norm_mix_pre_w'], 'm_w_in': out['m_w_in'], 'm_b_gate': out['m_b_gate'], 'm_conv_w': out['m_conv_w'], 'm_conv_b': out['m_conv_b'], 'm_dt_bias': out['m_dt_bias'], 'm_a_log': out['m_a_log'], 'm_d_skip': out['m_d_skip'], 'm_ssm_norm_w': out['m_ssm_norm_w'], 'm_w_att_proj': out['m_w_att_proj'], 'm_w_ssm_proj': out['m_w_ssm_proj'], 'm_w_out': out['m_w_out'], 'm_norm_mix_post_w': out['m_norm_mix_post_w'], 'm_norm_ffn_pre_w': out['m_norm_ffn_pre_w'], 'm_w_up': out['m_w_up'], 'm_w_down': out['m_w_down'], 'm_norm_ffn_post_w': out['m_norm_ffn_post_w'], 'v_norm_mix_pre_w': out['v_norm_mix_pre_w'], 'v_w_in': out['v_w_in'], 'v_b_gate': out['v_b_gate'], 'v_conv_w': out['v_conv_w'], 'v_conv_b': out['v_conv_b'], 'v_dt_bias': out['v_dt_bias'], 'v_a_log': out['v_a_log'], 'v_d_skip': out['v_d_skip'], 'v_ssm_norm_w': out['v_ssm_norm_w'], 'v_w_att_proj': out['v_w_att_proj'], 'v_w_ssm_proj': out['v_w_ssm_proj'], 'v_w_out': out['v_w_out'], 'v_norm_mix_post_w': out['v_norm_mix_post_w'], 'v_norm_ffn_pre_w': out['v_norm_ffn_pre_w'], 'v_w_up': out['v_w_up'], 'v_w_down': out['v_w_down'], 'v_norm_ffn_post_w': out['v_norm_ffn_post_w']}


def _loss(weights, diff, rest, loss_target):
    with _jax.named_scope("forward"):
        args = {**rest, TWIN_DIFF_INPUT: diff, **{k: w.astype(_WEIGHT_DTYPES[k]) for k, w in weights.items()}}
        y = _forward(args)
    with _jax.named_scope("loss_head"):
        err = _jnp.square(y.astype(_jnp.float32) - loss_target)
        return 0.5 * _jnp.sum(_jnp.mean(err, axis=-1)) if err.ndim else 0.5 * err


def _adamw(w, g, m, v):
    m = ADAM_B1 * m + (1.0 - ADAM_B1) * g
    v = ADAM_B2 * v + (1.0 - ADAM_B2) * _jnp.square(g)
    m_hat = m / (1.0 - ADAM_B1 ** ADAM_STEP)
    v_hat = v / (1.0 - ADAM_B2 ** ADAM_STEP)
    delta = -ADAM_LR * (m_hat / (_jnp.sqrt(v_hat) + ADAM_EPS) + ADAM_WD * w)
    return delta, m, v


def reference(x, norm_mix_pre_w, w_in, b_gate, conv_w, conv_b, dt_bias, a_log, d_skip, ssm_norm_w, w_att_proj, w_ssm_proj, w_out, norm_mix_post_w, norm_ffn_pre_w, w_up, w_down, norm_ffn_post_w, loss_target, m_norm_mix_pre_w, m_w_in, m_b_gate, m_conv_w, m_conv_b, m_dt_bias, m_a_log, m_d_skip, m_ssm_norm_w, m_w_att_proj, m_w_ssm_proj, m_w_out, m_norm_mix_post_w, m_norm_ffn_pre_w, m_w_up, m_w_down, m_norm_ffn_post_w, v_norm_mix_pre_w, v_w_in, v_b_gate, v_conv_w, v_conv_b, v_dt_bias, v_a_log, v_d_skip, v_ssm_norm_w, v_w_att_proj, v_w_ssm_proj, v_w_out, v_norm_mix_post_w, v_norm_ffn_pre_w, v_w_up, v_w_down, v_norm_ffn_post_w):
    given = dict(x=x, norm_mix_pre_w=norm_mix_pre_w, w_in=w_in, b_gate=b_gate, conv_w=conv_w, conv_b=conv_b, dt_bias=dt_bias, a_log=a_log, d_skip=d_skip, ssm_norm_w=ssm_norm_w, w_att_proj=w_att_proj, w_ssm_proj=w_ssm_proj, w_out=w_out, norm_mix_post_w=norm_mix_post_w, norm_ffn_pre_w=norm_ffn_pre_w, w_up=w_up, w_down=w_down, norm_ffn_post_w=norm_ffn_post_w, loss_target=loss_target, m_norm_mix_pre_w=m_norm_mix_pre_w, m_w_in=m_w_in, m_b_gate=m_b_gate, m_conv_w=m_conv_w, m_conv_b=m_conv_b, m_dt_bias=m_dt_bias, m_a_log=m_a_log, m_d_skip=m_d_skip, m_ssm_norm_w=m_ssm_norm_w, m_w_att_proj=m_w_att_proj, m_w_ssm_proj=m_w_ssm_proj, m_w_out=m_w_out, m_norm_mix_post_w=m_norm_mix_post_w, m_norm_ffn_pre_w=m_norm_ffn_pre_w, m_w_up=m_w_up, m_w_down=m_w_down, m_norm_ffn_post_w=m_norm_ffn_post_w, v_norm_mix_pre_w=v_norm_mix_pre_w, v_w_in=v_w_in, v_b_gate=v_b_gate, v_conv_w=v_conv_w, v_conv_b=v_conv_b, v_dt_bias=v_dt_bias, v_a_log=v_a_log, v_d_skip=v_d_skip, v_ssm_norm_w=v_ssm_norm_w, v_w_att_proj=v_w_att_proj, v_w_ssm_proj=v_w_ssm_proj, v_w_out=v_w_out, v_norm_mix_post_w=v_norm_mix_post_w, v_norm_ffn_pre_w=v_norm_ffn_pre_w, v_w_up=v_w_up, v_w_down=v_w_down, v_norm_ffn_post_w=v_norm_ffn_post_w)
    weights = {n: given[n] for n in TWIN_WEIGHTS}
    shared = {n: given[n] for n in SHARED_INPUTS}
    per_example = {n: given[n] for n in ['x']}
    grad_fn = _jax.value_and_grad(_loss, argnums=(0, 1))

    def one_microbatch(ex, loss_target):
        ex = dict(ex)
        diff = ex.pop(TWIN_DIFF_INPUT)
        return grad_fn(weights, diff, {**shared, **ex}, loss_target)

    if N_MICROBATCH == 1:
        loss, (grad_w, grad_x) = one_microbatch(per_example, given["loss_target"])
    else:
        def body(carry, xs):
            loss_sum, grad_sum = carry
            l_k, (gw_k, gx_k) = one_microbatch(xs[0], xs[1])
            with _jax.named_scope("update"):
                return (loss_sum + l_k, _jax.tree.map(_jnp.add, grad_sum, gw_k)), gx_k

        init = (_jnp.zeros((), _jnp.float32), _jax.tree.map(_jnp.zeros_like, weights))
        (loss, grad_w), grad_x = _jax.lax.scan(body, init, (per_example, given["loss_target"]))
    with _jax.named_scope("update"):
        delta_w, new_m, new_v = {}, {}, {}
        for n in TWIN_WEIGHTS:
            delta_w[n], new_m[n], new_v[n] = _adamw(weights[n], grad_w[n], given["m_" + n], given["v_" + n])
    return (loss, grad_x, *[grad_w[n] for n in TWIN_WEIGHTS], *[delta_w[n] for n in TWIN_WEIGHTS],
            *[new_m[n] for n in TWIN_WEIGHTS], *[new_v[n] for n in TWIN_WEIGHTS])
```

```python
import functools
import math

import jax
import jax.numpy as jnp
import numpy as np
from jax import lax
from jax.experimental import pallas as pl
from jax.experimental.pallas import tpu as pltpu

F32 = jnp.float32
BF16 = jnp.bfloat16

D_MODEL = 1024
HEAD_DIM = 64
N_ATT_HEADS = 12
ATT_WIDTH = N_ATT_HEADS * HEAD_DIM
DILATED_PATTERNS = ((128, 1), (512, 4), (2048, 16))
ATT_BLOCK = 128
SSM_INNER = 2048
SSM_HEAD_DIM = 64
SSM_HEADS = 32
SSM_GROUPS = 8
SSM_STATE = 128
SSM_CHUNK = 128
CONV_DIM = 4096
SSM_CONV = 4
FFN_HIDDEN = 4096
RMS_EPS = 1e-6
N_DEV = 8

ADAM_LR = 0.001
ADAM_B1 = 0.9
ADAM_B2 = 0.999
ADAM_EPS = 1e-08
ADAM_WD = 0.01
ADAM_STEP = 10

LANE = 128
OFF_Z, OFF_GL, OFF_XBC, OFF_QKV, OFF_DT = 0, 2048, 4096, 8192, 10496
PROJ_W = 10752
PROJ_BLOCKS = PROJ_W // LANE
VMEM_LIMIT = 52 * 1024 * 1024
NEG = -1e30

HI = lax.Precision.HIGHEST
NT_DIMS = (((1,), (1,)), ((), ()))
TN_DIMS = (((0,), (0,)), ((), ()))
S = jax.ShapeDtypeStruct


def _params(sem):
    return pltpu.CompilerParams(dimension_semantics=sem, vmem_limit_bytes=VMEM_LIMIT)


def _matmul(a, b, *, mode, out_dtype, name, tm, tn, tk, epilogue=None, extra=None):
    if mode == "nn":
        (m, k), n = a.shape, b.shape[1]
        a_spec = pl.BlockSpec((tm, tk), lambda i, j, kk: (i, kk))
        b_spec = pl.BlockSpec((tk, tn), lambda i, j, kk: (kk, j))
        dims = (((1,), (0,)), ((), ()))
    elif mode == "nt":
        (m, k), n = a.shape, b.shape[0]
        a_spec = pl.BlockSpec((tm, tk), lambda i, j, kk: (i, kk))
        b_spec = pl.BlockSpec((tn, tk), lambda i, j, kk: (j, kk))
        dims = NT_DIMS
    else:
        (k, m), n = a.shape, b.shape[1]
        a_spec = pl.BlockSpec((tk, tm), lambda i, j, kk: (kk, i))
        b_spec = pl.BlockSpec((tk, tn), lambda i, j, kk: (kk, j))
        dims = TN_DIMS
    assert m % tm == 0 and n % tn == 0 and k % tk == 0, (name, m, n, k)
    nk = k // tk
    o_spec = pl.BlockSpec((tm, tn), lambda i, j, kk: (i, j))
    in_specs, args = [a_spec, b_spec], [a, b]
    if epilogue == "relu2":
        out_shape = (S((m, n), BF16), S((m, n), BF16))
        out_specs = (o_spec, o_spec)
    else:
        out_shape, out_specs = S((m, n), out_dtype), o_spec
    if epilogue == "relu2_bwd":
        in_specs.append(o_spec)
        args.append(extra)

    def finish(acc, refs):
        if epilogue == "relu2":
            r = jnp.maximum(acc, 0.0)
            refs[0][...] = (r * r).astype(BF16)
            refs[1][...] = acc.astype(BF16)
        elif epilogue == "relu2_bwd":
            up = refs[0][...].astype(F32)
            refs[1][...] = (acc * (2.0 * jnp.maximum(up, 0.0))).astype(out_dtype)
        else:
            refs[0][...] = acc.astype(out_dtype)

    def body(a_ref, b_ref, *rest):
        part = lax.dot_general(a_ref[...].astype(BF16), b_ref[...].astype(BF16), dims, preferred_element_type=F32)
        if nk == 1:
            finish(part, rest)
            return
        acc_ref = rest[-1]
        kk = pl.program_id(2)

        @pl.when(kk == 0)
        def _():
            acc_ref[...] = part

        @pl.when(kk > 0)
        def _():
            acc_ref[...] += part

        @pl.when(kk == nk - 1)
        def _():
            finish(acc_ref[...], rest[:-1])

    scratch = [] if nk == 1 else [pltpu.VMEM((tm, tn), F32)]
    return pl.pallas_call(
        body, grid=(m // tm, n // tn, nk), in_specs=in_specs, out_specs=out_specs, out_shape=out_shape,
        scratch_shapes=scratch, name=name, compiler_params=_params(("parallel", "parallel", "arbitrary")),
    )(*args)


def _rowcall(body, name, n_rows, tr, ins, outs, scratch=()):
    res = pl.pallas_call(
        body, grid=(n_rows // tr,),
        in_specs=[pl.BlockSpec(bs, im) for _, bs, im in ins],
        out_specs=[pl.BlockSpec(bs, im) for _, _, bs, im in outs],
        out_shape=[S(sh, dt) for sh, dt, _, _ in outs],
        scratch_shapes=list(scratch), name=name, compiler_params=_params(("arbitrary",)),
    )(*[a for a, _, _ in ins])
    return res


def _rows(arr, tr, width=None, cb=0):
    width = arr.shape[1] if width is None else width
    return (arr, (tr, width), lambda i, cb=cb: (i, cb))


def _whole(arr):
    nd = arr.ndim
    return (arr, arr.shape, lambda i, nd=nd: (0,) * nd)


def _orow(n_rows, width, dtype, tr):
    return ((n_rows, width), dtype, (tr, width), lambda i: (i, 0))


def _oacc(width):
    return ((1, width), F32, (1, width), lambda i: (0, 0))


def _accumulate(ref, value):
    first = pl.program_id(0) == 0

    @pl.when(first)
    def _():
        ref[...] = value

    @pl.when(jnp.logical_not(first))
    def _():
        ref[...] += value


def _colsum(v):
    return jnp.sum(v, axis=0, keepdims=True)


def _rms_fwd(x, w):
    r = lax.rsqrt(jnp.mean(x * x, axis=-1, keepdims=True) + RMS_EPS)
    return x * r * w


def _rms_bwd(gy, x, w):
    r = lax.rsqrt(jnp.mean(x * x, axis=-1, keepdims=True) + RMS_EPS)
    xn = x * r
    gxn = gy * w
    gx = r * (gxn - xn * jnp.mean(gxn * xn, axis=-1, keepdims=True))
    return gx, _colsum(gy * xn)


def _sigmoid(x):
    return 1.0 / (1.0 + jnp.exp(-x))


def _head_expand(n_heads_pad, n_heads, width):
    h = lax.broadcasted_iota(jnp.int32, (n_heads_pad, n_heads * width), 0)
    c = lax.broadcasted_iota(jnp.int32, (n_heads_pad, n_heads * width), 1)
    return (c // width == h).astype(F32)


def _head_reduce(n_heads, width, n_heads_pad):
    c = lax.broadcasted_iota(jnp.int32, (n_heads * width, n_heads_pad), 0)
    h = lax.broadcasted_iota(jnp.int32, (n_heads * width, n_heads_pad), 1)
    return (c // width == h).astype(F32)


def _block_ones(n, width):
    r = lax.broadcasted_iota(jnp.int32, (n, n), 0)
    c = lax.broadcasted_iota(jnp.int32, (n, n), 1)
    return (r // width == c // width).astype(F32)


def _pre_norm(x, w_pre, tr=512):
    t = x.shape[0]

    def body(x_ref, w_ref, u_ref):
        u_ref[...] = _rms_fwd(x_ref[...], w_ref[...]).astype(BF16)

    return _rowcall(body, "pre_norm", t, tr, [_rows(x, tr), _whole(w_pre)], [_orow(t, D_MODEL, BF16, tr)])[0]


def _conv_fwd(proj, conv_w, conv_b, tr=256):
    t = proj.shape[0]
    cb = OFF_XBC // CONV_DIM
    halo = (proj, (8, CONV_DIM), lambda i: (jnp.maximum(i * (tr // 8) - 1, 0), cb))

    def body(cur_ref, prev_ref, w_ref, b_ref, o_ref, ext):
        ext[pl.ds(0, 8), :] = jnp.where(pl.program_id(0) > 0, prev_ref[...], 0.0)
        ext[pl.ds(8, tr), :] = cur_ref[...]
        acc = b_ref[...] + w_ref[3:4, :] * cur_ref[...]
        for k in range(SSM_CONV - 1):
            acc = acc + w_ref[k:k + 1, :] * ext[pl.ds(8 - 3 + k, tr), :]
        o_ref[...] = acc * _sigmoid(acc)

    return _rowcall(body, "conv_fwd", t, tr, [_rows(proj, tr, CONV_DIM, cb), halo, _whole(conv_w), _whole(conv_b)],
                    [_orow(t, CONV_DIM, F32, tr)], scratch=[pltpu.VMEM((tr + 8, CONV_DIM), F32)])[0]


def _dt_fwd(proj, dt_bias_pad, tr=512):
    t = proj.shape[0]

    def body(raw_ref, b_ref, dtx_ref, dtt_ref):
        v = raw_ref[...] + b_ref[...]
        dt = jnp.maximum(v, 0.0) + jnp.log1p(jnp.exp(-jnp.abs(v)))
        dtx_ref[...] = jnp.dot(dt, _head_expand(LANE, SSM_HEADS, SSM_HEAD_DIM), precision=HI, preferred_element_type=F32)
        dtt_ref[...] = dt.T

    return _rowcall(body, "dt_fwd", t, tr, [_rows(proj, tr, LANE, OFF_DT // LANE), _whole(dt_bias_pad)],
                    [_orow(t, SSM_INNER, F32, tr), ((LANE, t), F32, (LANE, tr), lambda i: (0, i))])


def _gate_norm_fwd(y_ssd, xa, proj, dskip_x, norm_w, tr=256):
    t = y_ssd.shape[0]
    gw = SSM_INNER // SSM_GROUPS

    def body(y_ref, xs_ref, z_ref, d_ref, w_ref, o_ref):
        z = z_ref[...]
        y3 = (y_ref[...] + d_ref[...] * xs_ref[...]) * (z * _sigmoid(z))
        for g in range(SSM_GROUPS):
            sl = slice(g * gw, (g + 1) * gw)
            o_ref[:, sl] = _rms_fwd(y3[:, sl], w_ref[:, sl]).astype(BF16)

    return _rowcall(body, "gate_norm_fwd", t, tr,
                    [_rows(y_ssd, tr), _rows(xa, tr, SSM_INNER, 0), _rows(proj, tr, SSM_INNER, OFF_Z // SSM_INNER), _whole(dskip_x), _whole(norm_w)],
                    [_orow(t, SSM_INNER, BF16, tr)])[0]


def _gating_fwd(proj, b_gate, att_p, ssm_p, tr=512):
    t = proj.shape[0]

    def body(gl_ref, b_ref, a_ref, s_ref, o_ref):
        gates = _sigmoid(gl_ref[...] + b_ref[...])
        o_ref[...] = (gates[:, :D_MODEL] * a_ref[...] + gates[:, D_MODEL:] * s_ref[...]).astype(BF16)

    return _rowcall(body, "gating_fwd", t, tr, [_rows(proj, tr, 2 * D_MODEL, OFF_GL // (2 * D_MODEL)), _whole(b_gate), _rows(att_p, tr), _rows(ssm_p, tr)],
                    [_orow(t, D_MODEL, BF16, tr)])[0]


def _mix_post_ffn_pre(x, mixed, w_post, w_fpre, tr=512):
    t = x.shape[0]

    def body(x_ref, m_ref, wp_ref, wf_ref, h1_ref, f_ref):
        h1 = x_ref[...] + _rms_fwd(m_ref[...], wp_ref[...])
        h1_ref[...] = h1
        f_ref[...] = _rms_fwd(h1, wf_ref[...]).astype(BF16)

    return _rowcall(body, "mix_post_ffn_pre", t, tr, [_rows(x, tr), _rows(mixed, tr), _whole(w_post), _whole(w_fpre)],
                    [_orow(t, D_MODEL, F32, tr), _orow(t, D_MODEL, BF16, tr)])


def _loss_and_ffn_post_bwd(h1, dn, w_fpost, target, tr=512):
    t = h1.shape[0]

    def body(h1_ref, dn_ref, w_ref, tg_ref, loss_ref, gh2_ref, gdn_ref, gw_ref):
        dn = dn_ref[...]
        w = w_ref[...]
        err = h1_ref[...] + _rms_fwd(dn, w) - tg_ref[...]
        _accumulate(loss_ref, jnp.zeros((1, LANE), F32) + 0.5 * jnp.sum(jnp.mean(err * err, axis=-1, keepdims=True)))
        gh2 = err * (1.0 / D_MODEL)
        gh2_ref[...] = gh2
        gdn, gw = _rms_bwd(gh2, dn, w)
        gdn_ref[...] = gdn.astype(BF16)
        _accumulate(gw_ref, gw)

    return _rowcall(body, "loss_ffn_post_bwd", t, tr, [_rows(h1, tr), _rows(dn, tr), _whole(w_fpost), _rows(target, tr)],
                    [_oacc(LANE), _orow(t, D_MODEL, F32, tr), _orow(t, D_MODEL, BF16, tr), _oacc(D_MODEL)])


def _ffn_pre_mix_post_bwd(g_h2, g_f, h1, w_fpre, mixed, w_post, tr=512):
    t = h1.shape[0]

    def body(gh2_ref, gf_ref, h1_ref, wf_ref, m_ref, wp_ref, gh1_ref, gm_ref, gwf_ref, gwp_ref):
        gx, gwf = _rms_bwd(gf_ref[...], h1_ref[...], wf_ref[...])
        gh1 = gh2_ref[...] + gx
        gh1_ref[...] = gh1
        gm, gwp = _rms_bwd(gh1, m_ref[...], wp_ref[...])
        gm_ref[...] = gm.astype(BF16)
        _accumulate(gwf_ref, gwf)
        _accumulate(gwp_ref, gwp)

    return _rowcall(body, "ffn_pre_mix_post_bwd", t, tr,
                    [_rows(g_h2, tr), _rows(g_f, tr), _rows(h1, tr), _whole(w_fpre), _rows(mixed, tr), _whole(w_post)],
                    [_orow(t, D_MODEL, F32, tr), _orow(t, D_MODEL, BF16, tr), _oacc(D_MODEL), _oacc(D_MODEL)])


def _gating_bwd(g_mixin, proj, b_gate, att_p, ssm_p, tr=512):
    t = proj.shape[0]

    def body(gm_ref, gl_ref, b_ref, a_ref, s_ref, ga_ref, gs_ref, ggl_ref, gb_ref):
        gates = _sigmoid(gl_ref[...] + b_ref[...])
        gm = gm_ref[...]
        g_att, g_ssm = gates[:, :D_MODEL], gates[:, D_MODEL:]
        ga_ref[...] = (gm * g_att).astype(BF16)
        gs_ref[...] = (gm * g_ssm).astype(BF16)
        ggl_a = gm * a_ref[...] * g_att * (1.0 - g_att)
        ggl_s = gm * s_ref[...] * g_ssm * (1.0 - g_ssm)
        ggl_ref[:, :D_MODEL] = ggl_a.astype(BF16)
        ggl_ref[:, D_MODEL:] = ggl_s.astype(BF16)
        _accumulate(gb_ref.at[:, :D_MODEL], _colsum(ggl_a))
        _accumulate(gb_ref.at[:, D_MODEL:], _colsum(ggl_s))

    return _rowcall(body, "gating_bwd", t, tr,
                    [_rows(g_mixin, tr), _rows(proj, tr, 2 * D_MODEL, OFF_GL // (2 * D_MODEL)), _whole(b_gate), _rows(att_p, tr), _rows(ssm_p, tr)],
                    [_orow(t, D_MODEL, BF16, tr), _orow(t, D_MODEL, BF16, tr), _orow(t, 2 * D_MODEL, BF16, tr), _oacc(2 * D_MODEL)])


def _gate_norm_bwd(g_y4, y_ssd, xa, proj, dskip_x, norm_w, tr=256):
    t = y_ssd.shape[0]
    gw = SSM_INNER // SSM_GROUPS

    def body(g_ref, y_ref, xs_ref, z_ref, d_ref, w_ref, gy2_ref, gz_ref, gnw_ref, gdx_ref, gd_ref):
        z = z_ref[...]
        xs = xs_ref[...]
        sg = _sigmoid(z)
        sz = z * sg
        y2 = y_ref[...] + d_ref[...] * xs
        y3 = y2 * sz
        g4 = g_ref[...]
        for g in range(SSM_GROUPS):
            sl = slice(g * gw, (g + 1) * gw)
            gy3, gnw = _rms_bwd(g4[:, sl], y3[:, sl], w_ref[:, sl])
            _accumulate(gnw_ref.at[:, sl], gnw)
            gy2 = gy3 * sz[:, sl]
            gy2_ref[:, sl] = gy2
            gz_ref[:, sl] = (gy3 * y2[:, sl] * (sg[:, sl] * (1.0 + z[:, sl] * (1.0 - sg[:, sl])))).astype(BF16)
            _accumulate(gdx_ref.at[:, sl], _colsum(gy2 * xs[:, sl]))
        tot = jnp.broadcast_to(gdx_ref[...], (8, SSM_INNER))
        gd_ref[...] = jnp.dot(tot, _head_reduce(SSM_HEADS, SSM_HEAD_DIM, LANE), precision=HI, preferred_element_type=F32)[0:1, :]

    return _rowcall(body, "gate_norm_bwd", t, tr,
                    [_rows(g_y4, tr), _rows(y_ssd, tr), _rows(xa, tr, SSM_INNER, 0), _rows(proj, tr, SSM_INNER, OFF_Z // SSM_INNER), _whole(dskip_x), _whole(norm_w)],
                    [_orow(t, SSM_INNER, F32, tr), _orow(t, SSM_INNER, BF16, tr), _oacc(SSM_INNER), _oacc(SSM_INNER), _oacc(LANE)])


def _dt_bwd(g_dtx, ga_rows, proj, dt_bias_pad, tr=512):
    t = proj.shape[0]

    def body(g_ref, ga_ref, raw_ref, b_ref, o_ref, gb_ref, gal_ref):
        red = _head_reduce(SSM_HEADS, SSM_HEAD_DIM, LANE)
        gdt = jnp.dot(g_ref[...], red, precision=HI, preferred_element_type=F32)
        graw = gdt * _sigmoid(raw_ref[...] + b_ref[...])
        o_ref[...] = graw.astype(BF16)
        _accumulate(gb_ref, _colsum(graw))
        tot = jnp.broadcast_to(_colsum(ga_ref[...]), (8, SSM_INNER))
        gal_ref[...] = jnp.dot(tot, red, precision=HI, preferred_element_type=F32)[0:1, :]

    return _rowcall(body, "dt_bwd", t, tr, [_rows(g_dtx, tr), _whole(ga_rows), _rows(proj, tr, LANE, OFF_DT // LANE), _whole(dt_bias_pad)],
                    [_orow(t, LANE, BF16, tr), _oacc(LANE), _oacc(LANE)])


def _conv_bwd_act(g_xs, g_b, g_c, proj, conv_w, conv_b, tr=256):
    t = proj.shape[0]
    cb = OFF_XBC // CONV_DIM
    halo = (proj, (8, CONV_DIM), lambda i: (jnp.maximum(i * (tr // 8) - 1, 0), cb))
    nb, nc = SSM_INNER, SSM_INNER + SSM_GROUPS * SSM_STATE

    def body(gxs_ref, gb_ref, gc_ref, cur_ref, prev_ref, w_ref, b_ref, o_ref, gcb_ref, gw0, gw1, gw2, gw3, ext):
        ext[pl.ds(0, 8), :] = jnp.where(pl.program_id(0) > 0, prev_ref[...], 0.0)
        ext[pl.ds(8, tr), :] = cur_ref[...]
        acc = b_ref[...] + w_ref[3:4, :] * cur_ref[...]
        for k in range(SSM_CONV - 1):
            acc = acc + w_ref[k:k + 1, :] * ext[pl.ds(8 - 3 + k, tr), :]
        sg = _sigmoid(acc)
        dsilu = sg * (1.0 + acc * (1.0 - sg))
        o_ref[:, :nb] = gxs_ref[...] * dsilu[:, :nb]
        o_ref[:, nb:nc] = gb_ref[...] * dsilu[:, nb:nc]
        o_ref[:, nc:] = gc_ref[...] * dsilu[:, nc:]
        gxc = o_ref[...]
        _accumulate(gcb_ref, _colsum(gxc))
        for k, gw in enumerate((gw0, gw1, gw2, gw3)):
            _accumulate(gw, _colsum(gxc * ext[pl.ds(8 - 3 + k, tr), :]))

    return _rowcall(body, "conv_bwd_act", t, tr,
                    [_rows(g_xs, tr), _rows(g_b, tr), _rows(g_c, tr), _rows(proj, tr, CONV_DIM, cb), halo, _whole(conv_w), _whole(conv_b)],
                    [_orow(t, CONV_DIM, F32, tr)] + [_oacc(CONV_DIM)] * 5, scratch=[pltpu.VMEM((tr + 8, CONV_DIM), F32)])


def _conv_bwd_in(g_xc, conv_w, tr=256):
    t = g_xc.shape[0]
    n_blk = t // tr
    halo = (g_xc, (8, CONV_DIM), lambda i: (jnp.minimum((i + 1) * (tr // 8), t // 8 - 1), 0))

    def body(cur_ref, nxt_ref, w_ref, o_ref, ext):
        ext[pl.ds(0, tr), :] = cur_ref[...]
        ext[pl.ds(tr, 8), :] = jnp.where(pl.program_id(0) < n_blk - 1, nxt_ref[...], 0.0)
        acc = w_ref[3:4, :] * cur_ref[...]
        for k in range(SSM_CONV - 1):
            acc = acc + w_ref[k:k + 1, :] * ext[pl.ds(3 - k, tr), :]
        o_ref[...] = acc.astype(BF16)

    return _rowcall(body, "conv_bwd_in", t, tr, [_rows(g_xc, tr), halo, _whole(conv_w)], [_orow(t, CONV_DIM, BF16, tr)],
                    scratch=[pltpu.VMEM((tr + 8, CONV_DIM), F32)])[0]


def _pre_norm_bwd(g_h1, g_u, x, w_pre, tr=512):
    t = x.shape[0]

    def body(gh_ref, gu_ref, x_ref, w_ref, gx_ref, gw_ref):
        gx, gw = _rms_bwd(gu_ref[...], x_ref[...], w_ref[...])
        gx_ref[...] = gh_ref[...] + gx
        _accumulate(gw_ref, gw)

    return _rowcall(body, "pre_norm_bwd", t, tr, [_rows(g_h1, tr), _rows(g_u, tr), _rows(x, tr), _whole(w_pre)],
                    [_orow(t, D_MODEL, F32, tr), _oacc(D_MODEL)])


def _alibi_slopes(n):
    def pow2(m):
        start = 2.0 ** (-8.0 / m)
        return [start ** (i + 1) for i in range(m)]
    if (n & (n - 1)) == 0:
        s = pow2(n)
    else:
        c = 2 ** int(math.floor(math.log2(n)))
        s = pow2(c) + pow2(2 * c)[0::2][: n - c]
    return np.array(s, dtype=np.float32)


def _slope_rows():
    s = _alibi_slopes(N_ATT_HEADS).reshape(N_ATT_HEADS // 2, 2)
    return jnp.asarray(np.broadcast_to(np.repeat(s, HEAD_DIM, axis=1)[:, None, :], (N_ATT_HEADS // 2, 8, LANE)).copy())


def _att_geometry(t, dil):
    rows = t // dil
    bq = min(512, rows)
    return rows, bq, bq // ATT_BLOCK, rows // bq


def _lane_half():
    return lax.broadcasted_iota(jnp.int32, (ATT_BLOCK, LANE), 1) // HEAD_DIM


def _att_scores_mask(dil, first):
    iq = lax.broadcasted_iota(jnp.int32, (ATT_BLOCK, 2 * ATT_BLOCK), 0)
    jk = lax.broadcasted_iota(jnp.int32, (ATT_BLOCK, 2 * ATT_BLOCK), 1)
    dist = ATT_BLOCK + iq - jk
    valid = (dist >= 0) & (dist <= ATT_BLOCK) & (jnp.logical_not(first) | (jk >= ATT_BLOCK))
    return (dist * dil).astype(F32), valid


def _att_fwd(proj, dil, slopes):
    t = proj.shape[0]
    rows, bq, nsub, nblk = _att_geometry(t, dil)
    pv = proj.reshape(rows, dil * PROJ_W)
    qb = OFF_QKV // LANE

    def spec(off, prev=False):
        if prev:
            return pl.BlockSpec((ATT_BLOCK, LANE), lambda r, hp, i: (jnp.maximum(i * nsub - 1, 0), r * PROJ_BLOCKS + qb + off + hp))
        return pl.BlockSpec((bq, LANE), lambda r, hp, i: (i, r * PROJ_BLOCKS + qb + off + hp))

    o_spec = pl.BlockSpec((bq, LANE), lambda r, hp, i: (i, r * (ATT_WIDTH // LANE) + hp))

    def body(q_ref, kc_ref, kp_ref, vc_ref, vp_ref, sl_ref, o_ref, lse_ref):
        i = pl.program_id(2)
        half = _lane_half()
        for sub in range(nsub):
            rs = slice(sub * ATT_BLOCK, (sub + 1) * ATT_BLOCK)
            q = (q_ref[rs, :] * (HEAD_DIM ** -0.5)).astype(BF16)
            if sub == 0:
                kk = jnp.concatenate([kp_ref[...], kc_ref[rs, :]], axis=0).astype(BF16)
                vv = jnp.concatenate([vp_ref[...], vc_ref[rs, :]], axis=0).astype(BF16)
                first = i == 0
            else:
                ks = slice((sub - 1) * ATT_BLOCK, (sub + 1) * ATT_BLOCK)
                kk, vv = kc_ref[ks, :].astype(BF16), vc_ref[ks, :].astype(BF16)
                first = jnp.bool_(False)
            dist, valid = _att_scores_mask(dil, first)
            outs, lses = [], []
            for e in range(2):
                qe = jnp.where(half == e, q, jnp.zeros_like(q))
                s = lax.dot_general(qe, kk, NT_DIMS, preferred_element_type=F32)
                s = jnp.where(valid, s - sl_ref[0:1, e * HEAD_DIM:e * HEAD_DIM + 1] * dist, NEG)
                m = jnp.max(s, axis=-1, keepdims=True)
                p = jnp.exp(s - m)
                l = jnp.sum(p, axis=-1, keepdims=True)
                outs.append(jnp.dot(p.astype(BF16), vv, preferred_element_type=F32) / l)
                lses.append(m + jnp.log(l))
            o_ref[rs, :] = jnp.where(half == 0, outs[0], outs[1]).astype(BF16)
            lse_ref[rs, :] = jnp.where(half == 0, lses[0], lses[1])

    o, lse = pl.pallas_call(
        body, grid=(dil, N_ATT_HEADS // 2, nblk),
        in_specs=[spec(0), spec(6), spec(6, True), spec(12), spec(12, True), pl.BlockSpec((None, 8, LANE), lambda r, hp, i: (hp, 0, 0))],
        out_specs=[o_spec, o_spec], out_shape=[S((rows, dil * ATT_WIDTH), BF16), S((rows, dil * ATT_WIDTH), F32)],
        name=f"att_fwd_d{dil}", compiler_params=_params(("parallel", "parallel", "arbitrary")),
    )(pv, pv, pv, pv, pv, slopes)
    return o.reshape(t, ATT_WIDTH), lse.reshape(t, ATT_WIDTH)


def _att_combine(outs, lses, tr=512):
    t = outs[0].shape[0]

    def body(o0, o1, o2, l0, l1, l2, att_ref, lse_ref):
        ls = [l0[...], l1[...], l2[...]]
        m = jnp.maximum(jnp.maximum(ls[0], ls[1]), ls[2])
        ws = [jnp.exp(l - m) for l in ls]
        tot = ws[0] + ws[1] + ws[2]
        num = ws[0] * o0[...].astype(F32) + ws[1] * o1[...].astype(F32) + ws[2] * o2[...].astype(F32)
        att_ref[...] = (num / tot).astype(BF16)
        lse_ref[...] = m + jnp.log(tot)

    return _rowcall(body, "att_combine", t, tr, [_rows(a, tr) for a in list(outs) + list(lses)],
                    [_orow(t, ATT_WIDTH, BF16, tr), _orow(t, ATT_WIDTH, F32, tr)])


def _att_delta(g_att, att, tr=512):
    t = att.shape[0]

    def body(g_ref, a_ref, o_ref):
        prod = g_ref[...] * a_ref[...].astype(F32)
        o_ref[...] = jnp.dot(prod, _block_ones(ATT_WIDTH, HEAD_DIM), precision=HI, preferred_element_type=F32)

    return _rowcall(body, "att_delta", t, tr, [_rows(g_att, tr), _rows(att, tr)], [_orow(t, ATT_WIDTH, F32, tr)])[0]


def _att_bwd_q(proj, g_att, lse, delta, dil, slopes):
    t = proj.shape[0]
    rows, bq, nsub, nblk = _att_geometry(t, dil)
    pv = proj.reshape(rows, dil * PROJ_W)
    qb = OFF_QKV // LANE
    aw = ATT_WIDTH // LANE

    def spec(off, prev=False):
        if prev:
            return pl.BlockSpec((ATT_BLOCK, LANE), lambda r, hp, i: (jnp.maximum(i * nsub - 1, 0), r * PROJ_BLOCKS + qb + off + hp))
        return pl.BlockSpec((bq, LANE), lambda r, hp, i: (i, r * PROJ_BLOCKS + qb + off + hp))

    o_spec = pl.BlockSpec((bq, LANE), lambda r, hp, i: (i, r * aw + hp))

    def body(q_ref, kc_ref, kp_ref, vc_ref, vp_ref, do_ref, lse_ref, dl_ref, sl_ref, dq_ref):
        i = pl.program_id(2)
        half = _lane_half()
        for sub in range(nsub):
            rs = slice(sub * ATT_BLOCK, (sub + 1) * ATT_BLOCK)
            q = (q_ref[rs, :] * (HEAD_DIM ** -0.5)).astype(BF16)
            do = do_ref[rs, :].astype(BF16)
            if sub == 0:
                kk = jnp.concatenate([kp_ref[...], kc_ref[rs, :]], axis=0).astype(BF16)
                vv = jnp.concatenate([vp_ref[...], vc_ref[rs, :]], axis=0).astype(BF16)
                first = i == 0
            else:
                ks = slice((sub - 1) * ATT_BLOCK, (sub + 1) * ATT_BLOCK)
                kk, vv = kc_ref[ks, :].astype(BF16), vc_ref[ks, :].astype(BF16)
                first = jnp.bool_(False)
            dist, valid = _att_scores_mask(dil, first)
            dqs = []
            for e in range(2):
                c = e * HEAD_DIM
                qe = jnp.where(half == e, q, jnp.zeros_like(q))
                doe = jnp.where(half == e, do, jnp.zeros_like(do))
                s = lax.dot_general(qe, kk, NT_DIMS, preferred_element_type=F32)
                s = jnp.where(valid, s - sl_ref[0:1, c:c + 1] * dist, NEG)
                p = jnp.exp(s - lse_ref[rs, c:c + 1])
                dp = lax.dot_general(doe, vv, NT_DIMS, preferred_element_type=F32)
                ds = p * (dp - dl_ref[rs, c:c + 1])
                dqs.append(jnp.dot(ds.astype(BF16), kk, preferred_element_type=F32))
            dq_ref[rs, :] = jnp.where(half == 0, dqs[0], dqs[1]) * (HEAD_DIM ** -0.5)

    gv, lv, dv = (a.reshape(rows, dil * ATT_WIDTH) for a in (g_att, lse, delta))
    dq = pl.pallas_call(
        body, grid=(dil, N_ATT_HEADS // 2, nblk),
        in_specs=[spec(0), spec(6), spec(6, True), spec(12), spec(12, True), o_spec, o_spec, o_spec,
                  pl.BlockSpec((None, 8, LANE), lambda r, hp, i: (hp, 0, 0))],
        out_specs=o_spec, out_shape=S((rows, dil * ATT_WIDTH), F32),
        name=f"att_bwd_q_d{dil}", compiler_params=_params(("parallel", "parallel", "arbitrary")),
    )(pv, pv, pv, pv, pv, gv, lv, dv, slopes)
    return dq.reshape(t, ATT_WIDTH)


def _att_bwd_kv(proj, g_att, lse, delta, dil, slopes):
    t = proj.shape[0]
    rows, bq, nsub, nblk = _att_geometry(t, dil)
    pv = proj.reshape(rows, dil * PROJ_W)
    qb = OFF_QKV // LANE
    aw = ATT_WIDTH // LANE
    n128 = rows // ATT_BLOCK

    def pspec(off, nxt=False):
        if nxt:
            return pl.BlockSpec((ATT_BLOCK, LANE), lambda r, hp, i: (jnp.minimum((i + 1) * nsub, n128 - 1), r * PROJ_BLOCKS + qb + off + hp))
        return pl.BlockSpec((bq, LANE), lambda r, hp, i: (i, r * PROJ_BLOCKS + qb + off + hp))

    def aspec(nxt=False):
        if nxt:
            return pl.BlockSpec((ATT_BLOCK, LANE), lambda r, hp, i: (jnp.minimum((i + 1) * nsub, n128 - 1), r * aw + hp))
        return pl.BlockSpec((bq, LANE), lambda r, hp, i: (i, r * aw + hp))

    def body(qc_ref, qn_ref, k_ref, v_ref, doc_ref, don_ref, lsec_ref, lsen_ref, dlc_ref, dln_ref, sl_ref, dk_ref, dv_ref):
        i = pl.program_id(2)
        half = _lane_half()
        jk = lax.broadcasted_iota(jnp.int32, (ATT_BLOCK, 2 * ATT_BLOCK), 0)
        cq = lax.broadcasted_iota(jnp.int32, (ATT_BLOCK, 2 * ATT_BLOCK), 1)
        dist_i = cq - jk
        dist = (dist_i * dil).astype(F32)
        for sub in range(nsub):
            rs = slice(sub * ATT_BLOCK, (sub + 1) * ATT_BLOCK)

            def both(cur_ref, nxt_ref):
                if sub < nsub - 1:
                    return cur_ref[sub * ATT_BLOCK:(sub + 2) * ATT_BLOCK, :]
                return jnp.concatenate([cur_ref[rs, :], nxt_ref[...]], axis=0)

            has_next = jnp.bool_(True) if sub < nsub - 1 else i < nblk - 1
            valid = (dist_i >= 0) & (dist_i <= ATT_BLOCK) & (has_next | (cq < ATT_BLOCK))
            qq = (both(qc_ref, qn_ref) * (HEAD_DIM ** -0.5)).astype(BF16)
            doo = both(doc_ref, don_ref).astype(BF16)
            lse_t = both(lsec_ref, lsen_ref).T
            dl_t = both(dlc_ref, dln_ref).T
            k = k_ref[rs, :].astype(BF16)
            v = v_ref[rs, :].astype(BF16)
            dks, dvs = [], []
            for e in range(2):
                c = e * HEAD_DIM
                ke = jnp.where(half == e, k, jnp.zeros_like(k))
                ve = jnp.where(half == e, v, jnp.zeros_like(v))
                st = lax.dot_general(ke, qq, NT_DIMS, preferred_element_type=F32)
                st = jnp.where(valid, st - sl_ref[0:1, c:c + 1] * dist, NEG)
                pt = jnp.exp(st - lse_t[c:c + 1, :])
                dpt = lax.dot_general(ve, doo, NT_DIMS, preferred_element_type=F32)
                dst = pt * (dpt - dl_t[c:c + 1, :])
                dks.append(jnp.dot(dst.astype(BF16), qq, preferred_element_type=F32))
                dvs.append(jnp.dot(pt.astype(BF16), doo, preferred_element_type=F32))
            dk_ref[rs, :] = jnp.where(half == 0, dks[0], dks[1])
            dv_ref[rs, :] = jnp.where(half == 0, dvs[0], dvs[1])

    gv, lv, dv = (a.reshape(rows, dil * ATT_WIDTH) for a in (g_att, lse, delta))
    dk, dvv = pl.pallas_call(
        body, grid=(dil, N_ATT_HEADS // 2, nblk),
        in_specs=[pspec(0), pspec(0, True), pspec(6), pspec(12), aspec(), aspec(True), aspec(), aspec(True), aspec(), aspec(True),
                  pl.BlockSpec((None, 8, LANE), lambda r, hp, i: (hp, 0, 0))],
        out_specs=[aspec(), aspec()], out_shape=[S((rows, dil * ATT_WIDTH), F32)] * 2,
        name=f"att_bwd_kv_d{dil}", compiler_params=_params(("parallel", "parallel", "arbitrary")),
    )(pv, pv, pv, pv, gv, gv, lv, lv, dv, dv, slopes)
    return dk.reshape(t, ATT_WIDTH), dvv.reshape(t, ATT_WIDTH)


def _att_grad_sum(dqs, dks, dvs, tr=512):
    t = dqs[0].shape[0]

    def body(*refs):
        o_ref = refs[-1]
        for n in range(3):
            tot = refs[3 * n][...] + refs[3 * n + 1][...] + refs[3 * n + 2][...]
            o_ref[:, n * ATT_WIDTH:(n + 1) * ATT_WIDTH] = tot.astype(BF16)

    return _rowcall(body, "att_grad_sum", t, tr, [_rows(a, tr) for a in list(dqs) + list(dks) + list(dvs)],
                    [_orow(t, 3 * ATT_WIDTH, BF16, tr)])[0]


def _ssd_common(xs, dtx, alx, dtt, alg, b, c):
    ch = SSM_CHUNK
    row = lax.broadcasted_iota(jnp.int32, (ch, ch), 0)
    col = lax.broadcasted_iota(jnp.int32, (ch, ch), 1)
    tril = (col <= row)
    la = dtx * (-jnp.exp(alx))
    cs = jnp.dot(tril.astype(F32), la, precision=HI, preferred_element_type=F32)
    la_t = dtt * (-jnp.exp(alg))
    cs_t = jnp.dot(la_t, (row <= col).astype(F32), precision=HI, preferred_element_type=F32)
    cs_last = cs[ch - 1:ch, :]
    return dict(tril=tril, row=row, col=col, la=la, cs=cs, cs_t=cs_t, cs_last=cs_last,
                e=jnp.exp(cs), w=jnp.exp(cs_last - cs), xd=xs * dtx)


def _decay_col(cs_t, heads_per_group):
    r = lax.broadcasted_iota(jnp.int32, (heads_per_group * SSM_HEAD_DIM, SSM_STATE), 0) // SSM_HEAD_DIM
    out = jnp.zeros((heads_per_group * SSM_HEAD_DIM, SSM_STATE), F32)
    for j in range(heads_per_group):
        out = jnp.where(r == j, jnp.exp(cs_t[j:j + 1, SSM_CHUNK - 1:SSM_CHUNK]), out)
    return out


def _ssd_specs(t):
    hg = SSM_HEADS // SSM_GROUPS
    gw = hg * SSM_HEAD_DIM
    nb0 = SSM_INNER // SSM_STATE
    return hg, gw, nb0


def _ssd_fwd(xa, dtx, dtt_g, alog_x, alog_g):
    t = xa.shape[0]
    nch = t // SSM_CHUNK
    hg, gw, nb0 = _ssd_specs(t)
    ch = SSM_CHUNK

    def body(xs_ref, b_ref, c_ref, dtx_ref, dtt_ref, alx_ref, alg_ref, y_ref, st_ref, h_scr):
        cc, g = pl.program_id(0), pl.program_id(1)

        @pl.when(cc == 0)
        def _():
            h_scr[g] = jnp.zeros((gw, SSM_STATE), F32)

        q = _ssd_common(xs_ref[...], dtx_ref[...], alx_ref[...], dtt_ref[...], alg_ref[...], b_ref[...], c_ref[...])
        bb, cb = b_ref[...].astype(BF16), c_ref[...].astype(BF16)
        cbm = lax.dot_general(cb, bb, NT_DIMS, preferred_element_type=F32)
        h = h_scr[g]
        st_ref[...] = h
        xd16 = q["xd"].astype(BF16)
        y = lax.dot_general(cb, h.astype(BF16), NT_DIMS, preferred_element_type=F32) * q["e"]
        lane_head = lax.broadcasted_iota(jnp.int32, (ch, gw), 1) // SSM_HEAD_DIM
        for j in range(hg):
            diff = q["cs"][:, j * SSM_HEAD_DIM:j * SSM_HEAD_DIM + 1] - q["cs_t"][j:j + 1, :]
            gmat = cbm * jnp.exp(jnp.where(q["tril"], diff, NEG))
            yj = jnp.dot(gmat.astype(BF16), xd16, preferred_element_type=F32)
            y = y + jnp.where(lane_head == j, yj, 0.0)
        y_ref[...] = y
        s_new = lax.dot_general((q["xd"] * q["w"]).astype(BF16), bb, TN_DIMS, preferred_element_type=F32)
        h_scr[g] = _decay_col(q["cs_t"], hg) * h + s_new

    return pl.pallas_call(
        body, grid=(nch, SSM_GROUPS),
        in_specs=[pl.BlockSpec((ch, gw), lambda cc, g: (cc, g)),
                  pl.BlockSpec((ch, SSM_STATE), lambda cc, g: (cc, nb0 + g)),
                  pl.BlockSpec((ch, SSM_STATE), lambda cc, g: (cc, nb0 + SSM_GROUPS + g)),
                  pl.BlockSpec((ch, gw), lambda cc, g: (cc, g)),
                  pl.BlockSpec((None, 8, ch), lambda cc, g: (g, 0, cc)),
                  pl.BlockSpec((1, gw), lambda cc, g: (0, g)),
                  pl.BlockSpec((None, 8, ch), lambda cc, g: (g, 0, 0))],
        out_specs=[pl.BlockSpec((ch, gw), lambda cc, g: (cc, g)),
                   pl.BlockSpec((None, None, gw, SSM_STATE), lambda cc, g: (cc, g, 0, 0))],
        out_shape=[S((t, SSM_INNER), F32), S((nch, SSM_GROUPS, gw, SSM_STATE), F32)],
        scratch_shapes=[pltpu.VMEM((SSM_GROUPS, gw, SSM_STATE), F32)],
        name="ssd_fwd", compiler_params=_params(("arbitrary", "arbitrary")),
    )(xa, xa, xa, dtx, dtt_g, alog_x, alog_g)


def _ssd_bwd(xa, dtx, dtt_g, alog_x, alog_g, g_y, states, dskip_x):
    t = xa.shape[0]
    nch = t // SSM_CHUNK
    hg, gw, nb0 = _ssd_specs(t)
    ch = SSM_CHUNK

    def rc(cc):
        return nch - 1 - cc

    def body(xs_ref, b_ref, c_ref, dtx_ref, dtt_ref, alx_ref, alg_ref, gy_ref, st_ref, dsk_ref,
             gxs_ref, gb_ref, gc_ref, gdt_ref, ga_ref, gh_scr):
        cc, g = pl.program_id(0), pl.program_id(1)

        @pl.when(cc == 0)
        def _():
            gh_scr[g] = jnp.zeros((gw, SSM_STATE), F32)

        xs, dtx = xs_ref[...], dtx_ref[...]
        q = _ssd_common(xs, dtx, alx_ref[...], dtt_ref[...], alg_ref[...], b_ref[...], c_ref[...])
        cs, cs_t, e, w, xd = q["cs"], q["cs_t"], q["e"], q["w"], q["xd"]
        bb, cb = b_ref[...].astype(BF16), c_ref[...].astype(BF16)
        gy = gy_ref[...]
        gy16, xd16 = gy.astype(BF16), xd.astype(BF16)
        h = st_ref[...]
        h16 = h.astype(BF16)
        ghn = gh_scr[g]
        ghn16 = ghn.astype(BF16)
        seg = _block_ones(gw, SSM_HEAD_DIM)
        cbm = lax.dot_general(cb, bb, NT_DIMS, preferred_element_type=F32)
        cbt = lax.dot_general(bb, cb, NT_DIMS, preferred_element_type=F32)

        gye16 = (gy * e).astype(BF16)
        chm = lax.dot_general(cb, h16, NT_DIMS, preferred_element_type=F32)
        g_c = jnp.dot(gye16, h16, preferred_element_type=F32)
        gh_off = lax.dot_general(gye16, cb, TN_DIMS, preferred_element_type=F32)
        g_e = jnp.dot(gy * chm, seg, precision=HI, preferred_element_type=F32)
        bgs = lax.dot_general(bb, ghn16, NT_DIMS, preferred_element_type=F32)
        g_xd = w * bgs
        g_w = jnp.dot(xd * bgs, seg, precision=HI, preferred_element_type=F32)
        g_b = jnp.dot((xd * w).astype(BF16), ghn16, preferred_element_type=F32)
        decay = _decay_col(cs_t, hg)
        gh_scr[g] = decay * ghn + gh_off
        rsum = jnp.sum(ghn * h, axis=1, keepdims=True)
        lane_head = lax.broadcasted_iota(jnp.int32, (ch, gw), 1) // SSM_HEAD_DIM
        lane_head1 = lax.broadcasted_iota(jnp.int32, (1, gw), 1) // SSM_HEAD_DIM
        g_el = jnp.zeros((1, gw), F32)
        g_cs = g_e * e - g_w * w
        upper = q["row"] <= q["col"]
        for j in range(hg):
            g_el = jnp.where(lane_head1 == j, jnp.sum(rsum[j * SSM_HEAD_DIM:(j + 1) * SSM_HEAD_DIM, :], axis=0, keepdims=True), g_el)
            csc = cs[:, j * SSM_HEAD_DIM:j * SSM_HEAD_DIM + 1]
            csr = cs_t[j:j + 1, :]
            lm = jnp.exp(jnp.where(q["tril"], csc - csr, NEG))
            lmt = jnp.exp(jnp.where(upper, csr - csc, NEG))
            gyj = jnp.where(lane_head == j, gy16, jnp.zeros_like(gy16))
            xdj = jnp.where(lane_head == j, xd16, jnp.zeros_like(xd16))
            gg = lax.dot_general(gyj, xd16, NT_DIMS, preferred_element_type=F32)
            ggt = lax.dot_general(xdj, gy16, NT_DIMS, preferred_element_type=F32)
            gcb, gcbt = gg * lm, ggt * lmt
            g_c = g_c + jnp.dot(gcb.astype(BF16), bb, preferred_element_type=F32)
            g_b = g_b + jnp.dot(gcbt.astype(BF16), cb, preferred_element_type=F32)
            gxdj = jnp.dot((cbt * lmt).astype(BF16), gy16, preferred_element_type=F32)
            g_xd = g_xd + jnp.where(lane_head == j, gxdj, 0.0)
            d_cs = jnp.sum(gcb * cbm, axis=1, keepdims=True) - jnp.sum(gcbt * cbt, axis=1, keepdims=True)
            g_cs = g_cs + jnp.where(lane_head == j, d_cs, 0.0)
        extra = _colsum(g_w * w) + g_el * jnp.exp(q["cs_last"])
        g_cs = g_cs + jnp.where(lax.broadcasted_iota(jnp.int32, (ch, gw), 0) == ch - 1, extra, 0.0)
        g_la = jnp.dot(upper.astype(F32), g_cs, precision=HI, preferred_element_type=F32)
        a_x = -jnp.exp(alx_ref[...])
        gdt_ref[...] = g_xd * xs + g_la * a_x * (1.0 / SSM_HEAD_DIM)
        ga_row = _colsum(g_la * q["la"]) * (1.0 / SSM_HEAD_DIM)
        ga_ref[...] = jnp.where(lax.broadcasted_iota(jnp.int32, (8, gw), 0) == 0, ga_row, 0.0)
        gxs_ref[...] = g_xd * dtx + gy * dsk_ref[...]
        gb_ref[...] = g_b
        gc_ref[...] = g_c

    return pl.pallas_call(
        body, grid=(nch, SSM_GROUPS),
        in_specs=[pl.BlockSpec((ch, gw), lambda cc, g: (rc(cc), g)),
                  pl.BlockSpec((ch, SSM_STATE), lambda cc, g: (rc(cc), nb0 + g)),
                  pl.BlockSpec((ch, SSM_STATE), lambda cc, g: (rc(cc), nb0 + SSM_GROUPS + g)),
                  pl.BlockSpec((ch, gw), lambda cc, g: (rc(cc), g)),
                  pl.BlockSpec((None, 8, ch), lambda cc, g: (g, 0, rc(cc))),
                  pl.BlockSpec((1, gw), lambda cc, g: (0, g)),
                  pl.BlockSpec((None, 8, ch), lambda cc, g: (g, 0, 0)),
                  pl.BlockSpec((ch, gw), lambda cc, g: (rc(cc), g)),
                  pl.BlockSpec((None, None, gw, SSM_STATE), lambda cc, g: (rc(cc), g, 0, 0)),
                  pl.BlockSpec((1, gw), lambda cc, g: (0, g))],
        out_specs=[pl.BlockSpec((ch, gw), lambda cc, g: (rc(cc), g)),
                   pl.BlockSpec((ch, SSM_STATE), lambda cc, g: (rc(cc), g)),
                   pl.BlockSpec((ch, SSM_STATE), lambda cc, g: (rc(cc), g)),
                   pl.BlockSpec((ch, gw), lambda cc, g: (rc(cc), g)),
                   pl.BlockSpec((8, gw), lambda cc, g: (rc(cc), g))],
        out_shape=[S((t, SSM_INNER), F32), S((t, SSM_GROUPS * SSM_STATE), F32), S((t, SSM_GROUPS * SSM_STATE), F32),
                   S((t, SSM_INNER), F32), S((nch * 8, SSM_INNER), F32)],
        scratch_shapes=[pltpu.VMEM((SSM_GROUPS, gw, SSM_STATE), F32)],
        name="ssd_bwd", compiler_params=_params(("arbitrary", "arbitrary")),
    )(xa, xa, xa, dtx, dtt_g, alog_x, alog_g, g_y, states, dskip_x)


def _local_step(x, target, w_pre, w_in_r, b_gate, conv_w, conv_b, dt_bias, a_log, d_skip, ssm_norm_w,
                w_att, w_ssm, w_out, w_post, w_fpre, w_up, w_down, w_fpost):
    t = x.shape[0]
    mm = functools.partial(_matmul, tm=512)
    slopes = _slope_rows()
    hg = SSM_HEADS // SSM_GROUPS
    dt_bias_pad = jnp.pad(dt_bias, ((0, 0), (0, LANE - SSM_HEADS)))
    alog_x = jnp.repeat(a_log, SSM_HEAD_DIM, axis=1)
    alog_g = jnp.broadcast_to(jnp.pad(a_log.reshape(SSM_GROUPS, hg), ((0, 0), (0, 8 - hg)))[:, :, None], (SSM_GROUPS, 8, SSM_CHUNK))
    dskip_x = jnp.repeat(d_skip, SSM_HEAD_DIM, axis=1)

    u = _pre_norm(x, w_pre)
    proj = mm(u, w_in_r, mode="nn", out_dtype=F32, name="in_proj", tn=1792, tk=D_MODEL)
    fwd = [_att_fwd(proj, dil, slopes) for _, dil in DILATED_PATTERNS]
    att, lse = _att_combine([o for o, _ in fwd], [l for _, l in fwd])
    xa = _conv_fwd(proj, conv_w, conv_b)
    dtx, dtt = _dt_fwd(proj, dt_bias_pad)
    dtt_g = jnp.pad(dtt[:SSM_HEADS].reshape(SSM_GROUPS, hg, t), ((0, 0), (0, 8 - hg), (0, 0)))
    y_ssd, states = _ssd_fwd(xa, dtx, dtt_g, alog_x, alog_g)
    y4 = _gate_norm_fwd(y_ssd, xa, proj, dskip_x, ssm_norm_w)
    att_p = mm(att, w_att, mode="nn", out_dtype=F32, name="att_proj", tn=D_MODEL, tk=ATT_WIDTH)
    ssm_p = mm(y4, w_ssm, mode="nn", out_dtype=F32, name="ssm_proj", tn=D_MODEL, tk=SSM_INNER)
    mixin = _gating_fwd(proj, b_gate, att_p, ssm_p)
    mixed = mm(mixin, w_out, mode="nn", out_dtype=F32, name="out_proj", tn=D_MODEL, tk=D_MODEL)
    h1, f = _mix_post_ffn_pre(x, mixed, w_post, w_fpre)
    act, up = mm(f, w_up, mode="nn", out_dtype=BF16, name="ffn_up", tn=2048, tk=D_MODEL, epilogue="relu2")
    dn = mm(act, w_down, mode="nn", out_dtype=F32, name="ffn_down", tn=D_MODEL, tk=FFN_HIDDEN)
    loss, g_h2, g_dn, gw_fpost = _loss_and_ffn_post_bwd(h1, dn, w_fpost, target)

    g_up = mm(g_dn, w_down, mode="nt", out_dtype=BF16, name="ffn_down_bwd_x", tn=2048, tk=D_MODEL, epilogue="relu2_bwd", extra=up)
    gw_down = _matmul(act, g_dn, mode="tn", out_dtype=BF16, name="ffn_down_bwd_w", tm=1024, tn=D_MODEL, tk=512)
    g_f = mm(g_up, w_up, mode="nt", out_dtype=F32, name="ffn_up_bwd_x", tn=D_MODEL, tk=FFN_HIDDEN)
    gw_up = _matmul(f, g_up, mode="tn", out_dtype=BF16, name="ffn_up_bwd_w", tm=D_MODEL, tn=2048, tk=512)
    g_h1, g_mixed, gw_fpre, gw_post = _ffn_pre_mix_post_bwd(g_h2, g_f, h1, w_fpre, mixed, w_post)
    g_mixin = mm(g_mixed, w_out, mode="nt", out_dtype=F32, name="out_proj_bwd_x", tn=D_MODEL, tk=D_MODEL)
    gw_out = _matmul(mixin, g_mixed, mode="tn", out_dtype=BF16, name="out_proj_bwd_w", tm=D_MODEL, tn=D_MODEL, tk=512)
    g_att_p, g_ssm_p, g_gl, g_b_gate = _gating_bwd(g_mixin, proj, b_gate, att_p, ssm_p)
    g_att = mm(g_att_p, w_att, mode="nt", out_dtype=F32, name="att_proj_bwd_x", tn=ATT_WIDTH, tk=D_MODEL)
    gw_att = _matmul(att, g_att_p, mode="tn", out_dtype=BF16, name="att_proj_bwd_w", tm=ATT_WIDTH, tn=D_MODEL, tk=512)
    g_y4 = mm(g_ssm_p, w_ssm, mode="nt", out_dtype=F32, name="ssm_proj_bwd_x", tn=SSM_INNER, tk=D_MODEL)
    gw_ssm = _matmul(y4, g_ssm_p, mode="tn", out_dtype=BF16, name="ssm_proj_bwd_w", tm=1024, tn=D_MODEL, tk=512)
    g_y2, g_z, g_norm_w, _, g_d_skip = _gate_norm_bwd(g_y4, y_ssd, xa, proj, dskip_x, ssm_norm_w)
    g_xs, g_bm, g_cm, g_dtx, ga_rows = _ssd_bwd(xa, dtx, dtt_g, alog_x, alog_g, g_y2, states, dskip_x)
    g_dt_raw, g_dt_bias, g_a_log = _dt_bwd(g_dtx, ga_rows, proj, dt_bias_pad)
    g_xc, g_conv_b, gcw0, gcw1, gcw2, gcw3 = _conv_bwd_act(g_xs, g_bm, g_cm, proj, conv_w, conv_b)
    g_xbc = _conv_bwd_in(g_xc, conv_w)
    delta = _att_delta(g_att, att)
    dqs, dks, dvs = [], [], []
    for _, dil in DILATED_PATTERNS:
        dqs.append(_att_bwd_q(proj, g_att, lse, delta, dil, slopes))
        dk, dv = _att_bwd_kv(proj, g_att, lse, delta, dil, slopes)
        dks.append(dk)
        dvs.append(dv)
    g_qkv = _att_grad_sum(dqs, dks, dvs)
    g_proj = jnp.concatenate([g_z, g_gl, g_xbc, g_qkv, g_dt_raw, jnp.zeros((t, PROJ_W - OFF_DT - LANE), BF16)], axis=1)
    g_u = mm(g_proj, w_in_r, mode="nt", out_dtype=F32, name="in_proj_bwd_x", tn=D_MODEL, tk=1792)
    gw_in_r = _matmul(u, g_proj, mode="tn", out_dtype=BF16, name="in_proj_bwd_w", tm=D_MODEL, tn=1792, tk=512)
    g_x, gw_pre = _pre_norm_bwd(g_h1, g_u, x, w_pre)

    grads = dict(
        norm_mix_pre_w=gw_pre, w_in_r=gw_in_r, b_gate=g_b_gate, conv_w=jnp.concatenate([gcw0, gcw1, gcw2, gcw3], axis=0),
        conv_b=g_conv_b, dt_bias=g_dt_bias[:, :SSM_HEADS], a_log=g_a_log[:, :SSM_HEADS], d_skip=g_d_skip[:, :SSM_HEADS],
        ssm_norm_w=g_norm_w, w_att_proj=gw_att, w_ssm_proj=gw_ssm, w_out=gw_out, norm_mix_post_w=gw_post,
        norm_ffn_pre_w=gw_fpre, w_up=gw_up, w_down=gw_down, norm_ffn_post_w=gw_fpost)
    return loss, g_x, grads


def _mesh_pos():
    return lax.axis_index("x"), lax.axis_index("y"), lax.axis_index("c")


def _all_gather_rows(shard):
    m_per, n = shard.shape

    def body(x_ref, out_ref, send_sems, recv_sems, local_sem):
        x, y, c = _mesh_pos()
        me, sibling = (x, y, c), (x, y, 1 - c)
        chips = [(1 - x, y), (x, 1 - y), (1 - x, 1 - y)]

        def rows(px, py, pc):
            return out_ref.at[pl.ds((4 * px + 2 * py + pc) * m_per, m_per), :]

        def copy(k, block, to, src=None):
            return pltpu.make_async_remote_copy(
                src_ref=rows(*block) if src is None else src, dst_ref=rows(*block),
                send_sem=send_sems.at[k], recv_sem=recv_sems.at[k], device_id=to, device_id_type=pl.DeviceIdType.MESH)

        mine = pltpu.make_async_copy(x_ref, rows(*me), local_sem)
        mine.start()
        first = [copy(0, me, sibling, src=x_ref)]
        first += [copy(1 + j, me, (*chip, c), src=x_ref) for j, chip in enumerate(chips)]
        for cp in first:
            cp.start()
        passed = [copy(4 + j, (*chip, c), sibling) for j, chip in enumerate(chips)]
        for j, chip in enumerate(chips):
            copy(1 + j, (*chip, c), me).wait_recv()
            passed[j].start()
        copy(0, sibling, me).wait_recv()
        for j, chip in enumerate(chips):
            copy(4 + j, (*chip, 1 - c), me).wait_recv()
        for cp in first + passed:
            cp.wait_send()
        mine.wait()

    return pl.pallas_call(
        body, out_shape=S((N_DEV * m_per, n), shard.dtype),
        in_specs=[pl.BlockSpec(memory_space=pltpu.HBM)], out_specs=pl.BlockSpec(memory_space=pltpu.HBM),
        scratch_shapes=[pltpu.SemaphoreType.DMA((7,)), pltpu.SemaphoreType.DMA((7,)), pltpu.SemaphoreType.DMA],
        name="weights_all_gather",
    )(shard)


def _exchange_grads(slabs, small):
    _, r_big, n = slabs.shape
    r_small = small.shape[0]

    def body(slab_ref, small_ref, recv_ref, gsm_ref, send_sems, recv_sems, local_sems):
        x, y, c = _mesh_pos()
        me = 4 * x + 2 * y + c

        def peer(k):
            px = 1 - x if k & 4 else x
            py = 1 - y if k & 2 else y
            pc = 1 - c if k & 1 else c
            return (px, py, pc), 4 * px + 2 * py + pc

        def big(k):
            to, lin = peer(k)
            return pltpu.make_async_remote_copy(
                src_ref=slab_ref.at[lin], dst_ref=recv_ref.at[me], send_sem=send_sems.at[k - 1], recv_sem=recv_sems.at[k - 1],
                device_id=to, device_id_type=pl.DeviceIdType.MESH)

        def sml(k):
            to, _ = peer(k)
            return pltpu.make_async_remote_copy(
                src_ref=small_ref, dst_ref=gsm_ref.at[me], send_sem=send_sems.at[6 + k], recv_sem=recv_sems.at[6 + k],
                device_id=to, device_id_type=pl.DeviceIdType.MESH)

        own_big = pltpu.make_async_copy(slab_ref.at[me], recv_ref.at[me], local_sems.at[0])
        own_small = pltpu.make_async_copy(small_ref, gsm_ref.at[me], local_sems.at[1])
        own_big.start()
        own_small.start()
        sends = [sml(k) for k in range(1, N_DEV)] + [big(k) for k in range(1, N_DEV)]
        for cp in sends:
            cp.start()
        for k in range(1, N_DEV):
            _, lin = peer(k)
            pltpu.make_async_remote_copy(
                src_ref=small_ref, dst_ref=gsm_ref.at[lin], send_sem=send_sems.at[6 + k], recv_sem=recv_sems.at[6 + k],
                device_id=peer(k)[0], device_id_type=pl.DeviceIdType.MESH).wait_recv()
            pltpu.make_async_remote_copy(
                src_ref=slab_ref.at[lin], dst_ref=recv_ref.at[lin], send_sem=send_sems.at[k - 1], recv_sem=recv_sems.at[k - 1],
                device_id=peer(k)[0], device_id_type=pl.DeviceIdType.MESH).wait_recv()
        for cp in sends:
            cp.wait_send()
        own_big.wait()
        own_small.wait()

    hbm = pl.BlockSpec(memory_space=pltpu.HBM)
    return pl.pallas_call(
        body, out_shape=[S((N_DEV, r_big, n), slabs.dtype), S((N_DEV, r_small, n), small.dtype)],
        in_specs=[hbm, hbm], out_specs=[hbm, hbm],
        scratch_shapes=[pltpu.SemaphoreType.DMA((14,)), pltpu.SemaphoreType.DMA((14,)), pltpu.SemaphoreType.DMA((2,))],
        name="grad_exchange",
    )(slabs, small)


def _adamw(w, m, v, slabs, name, tr):
    r = w.shape[0]
    c1 = 1.0 - ADAM_B1 ** ADAM_STEP
    c2 = 1.0 - ADAM_B2 ** ADAM_STEP

    def body(w_ref, m_ref, v_ref, s_ref, g_ref, d_ref, nm_ref, nv_ref):
        g = s_ref[0].astype(F32)
        for d in range(1, N_DEV):
            g = g + s_ref[d].astype(F32)
        nm = ADAM_B1 * m_ref[...] + (1.0 - ADAM_B1) * g
        nv = ADAM_B2 * v_ref[...] + (1.0 - ADAM_B2) * (g * g)
        g_ref[...] = g
        nm_ref[...] = nm
        nv_ref[...] = nv
        d_ref[...] = -ADAM_LR * ((nm / c1) / (jnp.sqrt(nv / c2) + ADAM_EPS) + ADAM_WD * w_ref[...])

    blk = pl.BlockSpec((tr, LANE), lambda i: (i, 0))
    return pl.pallas_call(
        body, grid=(r // tr,), in_specs=[blk, blk, blk, pl.BlockSpec((N_DEV, tr, LANE), lambda i: (0, i, 0))],
        out_specs=[blk] * 4, out_shape=[S((r, LANE), F32)] * 4, name=name, compiler_params=_params(("parallel",)),
    )(w, m, v, slabs)


BIG = ("w_in", "conv_w", "w_att_proj", "w_ssm_proj", "w_out", "w_up", "w_down")
SMALL = ("norm_mix_pre_w", "b_gate", "conv_b", "dt_bias", "a_log", "d_skip", "ssm_norm_w", "norm_mix_post_w",
         "norm_ffn_pre_w", "norm_ffn_post_w")
ORDER = ("norm_mix_pre_w", "w_in", "b_gate", "conv_w", "conv_b", "dt_bias", "a_log", "d_skip", "ssm_norm_w", "w_att_proj",
         "w_ssm_proj", "w_out", "norm_mix_post_w", "norm_ffn_pre_w", "w_up", "w_down", "norm_ffn_post_w")
COL_SHARDED = ("w_in", "conv_w", "w_att_proj", "w_up")


def _pack(parts, rows_multiple):
    flat = jnp.concatenate([p.reshape(-1) for p in parts])
    pad = (-flat.shape[0]) % (rows_multiple * LANE)
    return jnp.pad(flat, (0, pad)).reshape(-1, LANE)


def _unpack(flat2d, shapes):
    flat, out, off = flat2d.reshape(-1), [], 0
    for sh in shapes:
        n = int(np.prod(sh))
        out.append(flat[off:off + n].reshape(sh))
        off += n
    return out


def _reorder_in_proj(w):
    qkv, z, xbc = w[:, :2304], w[:, 2304:4352], w[:, 4352:8448]
    dt, gate = w[:, 8448:8480], w[:, 8480:10528]
    return jnp.concatenate([z, gate, xbc, qkv, dt, jnp.zeros((w.shape[0], PROJ_W - 10528), w.dtype)], axis=1)


def _restore_in_proj(wr):
    return jnp.concatenate([wr[:, OFF_QKV:OFF_QKV + 2304], wr[:, OFF_Z:OFF_Z + 2048], wr[:, OFF_XBC:OFF_XBC + 4096],
                            wr[:, OFF_DT:OFF_DT + 32], wr[:, OFF_GL:OFF_GL + 2048]], axis=1)


def _gathered_full(name, g):
    if name in COL_SHARDED:
        return jnp.moveaxis(g, 0, 1).reshape(g.shape[1], N_DEV * g.shape[2])
    return g.reshape(N_DEV * g.shape[1], g.shape[2])


def _split_for_devices(name, full):
    if name in COL_SHARDED:
        r, cdim = full.shape
        return jnp.moveaxis(full.reshape(r, N_DEV, cdim // N_DEV), 1, 0)
    return full.reshape(N_DEV, full.shape[0] // N_DEV, full.shape[1])


def kernel(x, norm_mix_pre_w, w_in, b_gate, conv_w, conv_b, dt_bias, a_log, d_skip, ssm_norm_w, w_att_proj, w_ssm_proj, w_out, norm_mix_post_w, norm_ffn_pre_w, w_up, w_down, norm_ffn_post_w, loss_target, m_norm_mix_pre_w, m_w_in, m_b_gate, m_conv_w, m_conv_b, m_dt_bias, m_a_log, m_d_skip, m_ssm_norm_w, m_w_att_proj, m_w_ssm_proj, m_w_out, m_norm_mix_post_w, m_norm_ffn_pre_w, m_w_up, m_w_down, m_norm_ffn_post_w, v_norm_mix_pre_w, v_w_in, v_b_gate, v_conv_w, v_conv_b, v_dt_bias, v_a_log, v_d_skip, v_ssm_norm_w, v_w_att_proj, v_w_ssm_proj, v_w_out, v_norm_mix_post_w, v_norm_ffn_pre_w, v_w_up, v_w_down, v_norm_ffn_post_w):
    w = dict(norm_mix_pre_w=norm_mix_pre_w, w_in=w_in, b_gate=b_gate, conv_w=conv_w, conv_b=conv_b, dt_bias=dt_bias, a_log=a_log,
             d_skip=d_skip, ssm_norm_w=ssm_norm_w, w_att_proj=w_att_proj, w_ssm_proj=w_ssm_proj, w_out=w_out,
             norm_mix_post_w=norm_mix_post_w, norm_ffn_pre_w=norm_ffn_pre_w, w_up=w_up, w_down=w_down, norm_ffn_post_w=norm_ffn_post_w)
    m = dict(norm_mix_pre_w=m_norm_mix_pre_w, w_in=m_w_in, b_gate=m_b_gate, conv_w=m_conv_w, conv_b=m_conv_b, dt_bias=m_dt_bias,
             a_log=m_a_log, d_skip=m_d_skip, ssm_norm_w=m_ssm_norm_w, w_att_proj=m_w_att_proj, w_ssm_proj=m_w_ssm_proj, w_out=m_w_out,
             norm_mix_post_w=m_norm_mix_post_w, norm_ffn_pre_w=m_norm_ffn_pre_w, w_up=m_w_up, w_down=m_w_down, norm_ffn_post_w=m_norm_ffn_post_w)
    v = dict(norm_mix_pre_w=v_norm_mix_pre_w, w_in=v_w_in, b_gate=v_b_gate, conv_w=v_conv_w, conv_b=v_conv_b, dt_bias=v_dt_bias,
             a_log=v_a_log, d_skip=v_d_skip, ssm_norm_w=v_ssm_norm_w, w_att_proj=v_w_att_proj, w_ssm_proj=v_w_ssm_proj, w_out=v_w_out,
             norm_mix_post_w=v_norm_mix_post_w, norm_ffn_pre_w=v_norm_ffn_pre_w, w_up=v_w_up, w_down=v_w_down, norm_ffn_post_w=v_norm_ffn_post_w)
    shard_shapes = {n: w[n].shape[1:] for n in ORDER}

    def to_wire(n):
        a = w[n][0]
        return lax.bitcast_convert_type(a, BF16).reshape(-1) if n == "conv_w" else a.astype(BF16).reshape(-1)

    wire_shapes = [(shard_shapes[n][0], shard_shapes[n][1] * 2) if n == "conv_w" else shard_shapes[n] for n in BIG]
    shard = _pack([to_wire(n) for n in BIG], 16)
    gathered = _all_gather_rows(shard).reshape(N_DEV, -1)
    pieces = [jnp.stack(col) for col in zip(*[_unpack(gathered[d], wire_shapes) for d in range(N_DEV)])]
    full = {}
    for n, g in zip(BIG, pieces):
        if n == "conv_w":
            g = lax.bitcast_convert_type(g.reshape(N_DEV, SSM_CONV, -1, 2), F32)
        full[n] = _gathered_full(n, g)

    loss, g_x, grads = _local_step(
        x[0], loss_target[0], w["norm_mix_pre_w"], _reorder_in_proj(full["w_in"]), w["b_gate"], full["conv_w"], w["conv_b"],
        w["dt_bias"], w["a_log"], w["d_skip"], w["ssm_norm_w"], full["w_att_proj"], full["w_ssm_proj"], full["w_out"],
        w["norm_mix_post_w"], w["norm_ffn_pre_w"], full["w_up"], full["w_down"], w["norm_ffn_post_w"])
    grads["w_in"] = _restore_in_proj(grads.pop("w_in_r"))

    per_dev = [_split_for_devices(n, grads[n].astype(BF16)) for n in BIG]
    slabs = jnp.stack([_pack([p[d] for p in per_dev], 16) for d in range(N_DEV)])
    small = _pack([grads[n].astype(F32) for n in SMALL], 8)
    recv, small_all = _exchange_grads(slabs, small)

    big_shapes = [shard_shapes[n] for n in BIG]
    small_shapes = [shard_shapes[n] for n in SMALL]
    big_out = _adamw(*[_pack([d_[n][0] for n in BIG], 16) for d_ in (w, m, v)], recv, "adamw_sharded", recv.shape[1] // 17)
    small_out = _adamw(*[_pack([d_[n][0] for n in SMALL], 8) for d_ in (w, m, v)], small_all, "adamw_replicated", small_all.shape[1])
    res = []
    for big_flat, small_flat in zip(big_out, small_out):
        vals = dict(zip(BIG, _unpack(big_flat, big_shapes)))
        vals.update(zip(SMALL, _unpack(small_flat, small_shapes)))
        res.append([vals[n][None] for n in ORDER])
    g_out, d_out, m_out, v_out = res
    total = lax.psum(loss[0, 0], ("x", "y", "c"))
    return (total, g_x[None], *g_out, *d_out, *m_out, *v_out)
```

```python
import functools
import math

import jax
import jax.numpy as jnp
import numpy as np
from jax import lax
from jax.experimental import pallas as pl
from jax.experimental.pallas import tpu as pltpu

F32 = jnp.float32
BF16 = jnp.bfloat16

D_MODEL = 1024
HEAD_DIM = 64
N_ATT_HEADS = 12
ATT_WIDTH = N_ATT_HEADS * HEAD_DIM
DILATED_PATTERNS = ((128, 1), (512, 4), (2048, 16))
ATT_BLOCK = 128
SSM_INNER = 2048
SSM_HEAD_DIM = 64
SSM_HEADS = 32
SSM_GROUPS = 8
SSM_STATE = 128
SSM_CHUNK = 128
CONV_DIM = 4096
SSM_CONV = 4
FFN_HIDDEN = 4096
RMS_EPS = 1e-6
N_DEV = 8

ADAM_LR = 0.001
ADAM_B1 = 0.9
ADAM_B2 = 0.999
ADAM_EPS = 1e-08
ADAM_WD = 0.01
ADAM_STEP = 10

LANE = 128
OFF_Z, OFF_GL, OFF_XBC, OFF_QKV, OFF_DT = 0, 2048, 4096, 8192, 10496
PROJ_W = 10752
PROJ_BLOCKS = PROJ_W // LANE
VMEM_LIMIT = 52 * 1024 * 1024
NEG = -1e30

HI = lax.Precision.HIGHEST
NT_DIMS = (((1,), (1,)), ((), ()))
TN_DIMS = (((0,), (0,)), ((), ()))
S = jax.ShapeDtypeStruct


def _params(sem):
    return pltpu.CompilerParams(dimension_semantics=sem, vmem_limit_bytes=VMEM_LIMIT)


def _matmul(a, b, *, mode, out_dtype, name, tm, tn, tk, epilogue=None, extra=None, stacked=False):
    if mode == "nn":
        m, k = a.shape
        n = b.shape[0] * b.shape[2] if stacked else b.shape[1]
        a_spec = pl.BlockSpec((tm, tk), lambda i, j, kk: (i, kk))
        b_spec = pl.BlockSpec((None, tk, tn), lambda i, j, kk: (j, kk, 0)) if stacked else pl.BlockSpec((tk, tn), lambda i, j, kk: (kk, j))
        dims = (((1,), (0,)), ((), ()))
    elif mode == "nt":
        m, k = a.shape
        n = b.shape[1] if stacked else b.shape[0]
        a_spec = pl.BlockSpec((tm, tk), lambda i, j, kk: (i, kk))
        b_spec = pl.BlockSpec((None, tn, tk), lambda i, j, kk: (kk, j, 0)) if stacked else pl.BlockSpec((tn, tk), lambda i, j, kk: (j, kk))
        dims = NT_DIMS
    else:
        (k, m), n = a.shape, b.shape[1]
        a_spec = pl.BlockSpec((tk, tm), lambda i, j, kk: (kk, i))
        b_spec = pl.BlockSpec((tk, tn), lambda i, j, kk: (kk, j))
        dims = TN_DIMS
    assert m % tm == 0 and n % tn == 0 and k % tk == 0, (name, m, n, k)
    if stacked:
        assert (tk if mode == "nt" else tn) * N_DEV == (k if mode == "nt" else n), name
    nk = k // tk
    o_spec = pl.BlockSpec((tm, tn), lambda i, j, kk: (i, j))
    in_specs, args = [a_spec, b_spec], [a, b]
    if epilogue == "relu2":
        out_shape = (S((m, n), BF16), S((m, n), BF16))
        out_specs = (o_spec, o_spec)
    elif stacked and mode == "tn":
        out_shape, out_specs = S((N_DEV, m, tn), out_dtype), pl.BlockSpec((None, tm, tn), lambda i, j, kk: (j, i, 0))
    else:
        out_shape, out_specs = S((m, n), out_dtype), o_spec
    if epilogue == "relu2_bwd":
        in_specs.append(o_spec)
        args.append(extra)

    def finish(acc, refs):
        if epilogue == "relu2":
            r = jnp.maximum(acc, 0.0)
            refs[0][...] = (r * r).astype(BF16)
            refs[1][...] = acc.astype(BF16)
        elif epilogue == "relu2_bwd":
            up = refs[0][...].astype(F32)
            refs[1][...] = (acc * (2.0 * jnp.maximum(up, 0.0))).astype(out_dtype)
        else:
            refs[0][...] = acc.astype(out_dtype)

    def body(a_ref, b_ref, *rest):
        part = lax.dot_general(a_ref[...].astype(BF16), b_ref[...].astype(BF16), dims, preferred_element_type=F32)
        if nk == 1:
            finish(part, rest)
            return
        acc_ref = rest[-1]
        kk = pl.program_id(2)

        @pl.when(kk == 0)
        def _():
            acc_ref[...] = part

        @pl.when(kk > 0)
        def _():
            acc_ref[...] += part

        @pl.when(kk == nk - 1)
        def _():
            finish(acc_ref[...], rest[:-1])

    scratch = [] if nk == 1 else [pltpu.VMEM((tm, tn), F32)]
    return pl.pallas_call(
        body, grid=(m // tm, n // tn, nk), in_specs=in_specs, out_specs=out_specs, out_shape=out_shape,
        scratch_shapes=scratch, name=name, compiler_params=_params(("parallel", "parallel", "arbitrary")),
    )(*args)


def _rowcall(body, name, n_rows, tr, ins, outs, scratch=()):
    res = pl.pallas_call(
        body, grid=(n_rows // tr,),
        in_specs=[pl.BlockSpec(bs, im) for _, bs, im in ins],
        out_specs=[pl.BlockSpec(bs, im) for _, _, bs, im in outs],
        out_shape=[S(sh, dt) for sh, dt, _, _ in outs],
        scratch_shapes=list(scratch), name=name, compiler_params=_params(("arbitrary",)),
    )(*[a for a, _, _ in ins])
    return res


def _rows(arr, tr, width=None, cb=0):
    width = arr.shape[1] if width is None else width
    return (arr, (tr, width), lambda i, cb=cb: (i, cb))


def _whole(arr):
    nd = arr.ndim
    return (arr, arr.shape, lambda i, nd=nd: (0,) * nd)


def _orow(n_rows, width, dtype, tr):
    return ((n_rows, width), dtype, (tr, width), lambda i: (i, 0))


def _oacc(width):
    return ((1, width), F32, (1, width), lambda i: (0, 0))


def _accumulate(ref, value):
    first = pl.program_id(0) == 0

    @pl.when(first)
    def _():
        ref[...] = value

    @pl.when(jnp.logical_not(first))
    def _():
        ref[...] += value


def _colsum(v):
    return jnp.sum(v, axis=0, keepdims=True)


def _rms_fwd(x, w):
    r = lax.rsqrt(jnp.mean(x * x, axis=-1, keepdims=True) + RMS_EPS)
    return x * r * w


def _rms_bwd(gy, x, w):
    r = lax.rsqrt(jnp.mean(x * x, axis=-1, keepdims=True) + RMS_EPS)
    xn = x * r
    gxn = gy * w
    gx = r * (gxn - xn * jnp.mean(gxn * xn, axis=-1, keepdims=True))
    return gx, _colsum(gy * xn)


def _sigmoid(x):
    return 1.0 / (1.0 + jnp.exp(-x))


def _head_expand(n_heads_pad, n_heads, width):
    h = lax.broadcasted_iota(jnp.int32, (n_heads_pad, n_heads * width), 0)
    c = lax.broadcasted_iota(jnp.int32, (n_heads_pad, n_heads * width), 1)
    return (c // width == h).astype(F32)


def _head_reduce(n_heads, width, n_heads_pad):
    c = lax.broadcasted_iota(jnp.int32, (n_heads * width, n_heads_pad), 0)
    h = lax.broadcasted_iota(jnp.int32, (n_heads * width, n_heads_pad), 1)
    return (c // width == h).astype(F32)


def _block_ones(n, width):
    r = lax.broadcasted_iota(jnp.int32, (n, n), 0)
    c = lax.broadcasted_iota(jnp.int32, (n, n), 1)
    return (r // width == c // width).astype(F32)


def _pre_norm(x, w_pre, tr=512):
    t = x.shape[0]

    def body(x_ref, w_ref, u_ref):
        u_ref[...] = _rms_fwd(x_ref[...], w_ref[...]).astype(BF16)

    return _rowcall(body, "pre_norm", t, tr, [_rows(x, tr), _whole(w_pre)], [_orow(t, D_MODEL, BF16, tr)])[0]


def _conv_fwd(proj, conv_w, conv_b, tr=256):
    t = proj.shape[0]
    cb = OFF_XBC // CONV_DIM
    halo = (proj, (8, CONV_DIM), lambda i: (jnp.maximum(i * (tr // 8) - 1, 0), cb))

    def body(cur_ref, prev_ref, w_ref, b_ref, o_ref, ext):
        ext[pl.ds(0, 8), :] = jnp.where(pl.program_id(0) > 0, prev_ref[...], 0.0)
        ext[pl.ds(8, tr), :] = cur_ref[...]
        acc = b_ref[...] + w_ref[3:4, :] * cur_ref[...]
        for k in range(SSM_CONV - 1):
            acc = acc + w_ref[k:k + 1, :] * ext[pl.ds(8 - 3 + k, tr), :]
        o_ref[...] = acc * _sigmoid(acc)

    return _rowcall(body, "conv_fwd", t, tr, [_rows(proj, tr, CONV_DIM, cb), halo, _whole(conv_w), _whole(conv_b)],
                    [_orow(t, CONV_DIM, F32, tr)], scratch=[pltpu.VMEM((tr + 8, CONV_DIM), F32)])[0]


def _dt_fwd(proj, dt_bias_pad, tr=512):
    t = proj.shape[0]

    def body(raw_ref, b_ref, dtx_ref, dtt_ref):
        v = raw_ref[...] + b_ref[...]
        dt = jnp.maximum(v, 0.0) + jnp.log1p(jnp.exp(-jnp.abs(v)))
        dtx_ref[...] = jnp.dot(dt, _head_expand(LANE, SSM_HEADS, SSM_HEAD_DIM), precision=HI, preferred_element_type=F32)
        dtt_ref[...] = dt.T

    return _rowcall(body, "dt_fwd", t, tr, [_rows(proj, tr, LANE, OFF_DT // LANE), _whole(dt_bias_pad)],
                    [_orow(t, SSM_INNER, F32, tr), ((LANE, t), F32, (LANE, tr), lambda i: (0, i))])


def _gate_norm_fwd(y_ssd, xa, proj, dskip_x, norm_w, tr=256):
    t = y_ssd.shape[0]
    gw = SSM_INNER // SSM_GROUPS

    def body(y_ref, xs_ref, z_ref, d_ref, w_ref, o_ref):
        z = z_ref[...]
        y3 = (y_ref[...] + d_ref[...] * xs_ref[...]) * (z * _sigmoid(z))
        for g in range(SSM_GROUPS):
            sl = slice(g * gw, (g + 1) * gw)
            o_ref[:, sl] = _rms_fwd(y3[:, sl], w_ref[:, sl]).astype(BF16)

    return _rowcall(body, "gate_norm_fwd", t, tr,
                    [_rows(y_ssd, tr), _rows(xa, tr, SSM_INNER, 0), _rows(proj, tr, SSM_INNER, OFF_Z // SSM_INNER), _whole(dskip_x), _whole(norm_w)],
                    [_orow(t, SSM_INNER, BF16, tr)])[0]


def _gating_fwd(proj, b_gate, att_p, ssm_p, tr=512):
    t = proj.shape[0]

    def body(gl_ref, b_ref, a_ref, s_ref, o_ref):
        gates = _sigmoid(gl_ref[...] + b_ref[...])
        o_ref[...] = (gates[:, :D_MODEL] * a_ref[...] + gates[:, D_MODEL:] * s_ref[...]).astype(BF16)

    return _rowcall(body, "gating_fwd", t, tr, [_rows(proj, tr, 2 * D_MODEL, OFF_GL // (2 * D_MODEL)), _whole(b_gate), _rows(att_p, tr), _rows(ssm_p, tr)],
                    [_orow(t, D_MODEL, BF16, tr)])[0]


def _mix_post_ffn_pre(x, mixed, w_post, w_fpre, tr=512):
    t = x.shape[0]

    def body(x_ref, m_ref, wp_ref, wf_ref, h1_ref, f_ref):
        h1 = x_ref[...] + _rms_fwd(m_ref[...], wp_ref[...])
        h1_ref[...] = h1
        f_ref[...] = _rms_fwd(h1, wf_ref[...]).astype(BF16)

    return _rowcall(body, "mix_post_ffn_pre", t, tr, [_rows(x, tr), _rows(mixed, tr), _whole(w_post), _whole(w_fpre)],
                    [_orow(t, D_MODEL, F32, tr), _orow(t, D_MODEL, BF16, tr)])


def _loss_and_ffn_post_bwd(h1, dn, w_fpost, target, tr=512):
    t = h1.shape[0]

    def body(h1_ref, dn_ref, w_ref, tg_ref, loss_ref, gh2_ref, gdn_ref, gw_ref):
        dn = dn_ref[...]
        w = w_ref[...]
        err = h1_ref[...] + _rms_fwd(dn, w) - tg_ref[...]
        _accumulate(loss_ref, jnp.zeros((1, LANE), F32) + 0.5 * jnp.sum(jnp.mean(err * err, axis=-1, keepdims=True)))
        gh2 = err * (1.0 / D_MODEL)
        gh2_ref[...] = gh2
        gdn, gw = _rms_bwd(gh2, dn, w)
        gdn_ref[...] = gdn.astype(BF16)
        _accumulate(gw_ref, gw)

    return _rowcall(body, "loss_ffn_post_bwd", t, tr, [_rows(h1, tr), _rows(dn, tr), _whole(w_fpost), _rows(target, tr)],
                    [_oacc(LANE), _orow(t, D_MODEL, F32, tr), _orow(t, D_MODEL, BF16, tr), _oacc(D_MODEL)])


def _ffn_pre_mix_post_bwd(g_h2, g_f, h1, w_fpre, mixed, w_post, tr=512):
    t = h1.shape[0]

    def body(gh2_ref, gf_ref, h1_ref, wf_ref, m_ref, wp_ref, gh1_ref, gm_ref, gwf_ref, gwp_ref):
        gx, gwf = _rms_bwd(gf_ref[...], h1_ref[...], wf_ref[...])
        gh1 = gh2_ref[...] + gx
        gh1_ref[...] = gh1
        gm, gwp = _rms_bwd(gh1, m_ref[...], wp_ref[...])
        gm_ref[...] = gm.astype(BF16)
        _accumulate(gwf_ref, gwf)
        _accumulate(gwp_ref, gwp)

    return _rowcall(body, "ffn_pre_mix_post_bwd", t, tr,
                    [_rows(g_h2, tr), _rows(g_f, tr), _rows(h1, tr), _whole(w_fpre), _rows(mixed, tr), _whole(w_post)],
                    [_orow(t, D_MODEL, F32, tr), _orow(t, D_MODEL, BF16, tr), _oacc(D_MODEL), _oacc(D_MODEL)])


def _gating_bwd(g_mixin, proj, b_gate, att_p, ssm_p, tr=512):
    t = proj.shape[0]

    def body(gm_ref, gl_ref, b_ref, a_ref, s_ref, ga_ref, gs_ref, ggl_ref, gb_ref):
        gates = _sigmoid(gl_ref[...] + b_ref[...])
        gm = gm_ref[...]
        g_att, g_ssm = gates[:, :D_MODEL], gates[:, D_MODEL:]
        ga_ref[...] = (gm * g_att).astype(BF16)
        gs_ref[...] = (gm * g_ssm).astype(BF16)
        ggl_a = gm * a_ref[...] * g_att * (1.0 - g_att)
        ggl_s = gm * s_ref[...] * g_ssm * (1.0 - g_ssm)
        ggl_ref[:, :D_MODEL] = ggl_a.astype(BF16)
        ggl_ref[:, D_MODEL:] = ggl_s.astype(BF16)
        _accumulate(gb_ref.at[:, :D_MODEL], _colsum(ggl_a))
        _accumulate(gb_ref.at[:, D_MODEL:], _colsum(ggl_s))

    return _rowcall(body, "gating_bwd", t, tr,
                    [_rows(g_mixin, tr), _rows(proj, tr, 2 * D_MODEL, OFF_GL // (2 * D_MODEL)), _whole(b_gate), _rows(att_p, tr), _rows(ssm_p, tr)],
                    [_orow(t, D_MODEL, BF16, tr), _orow(t, D_MODEL, BF16, tr), _orow(t, 2 * D_MODEL, BF16, tr), _oacc(2 * D_MODEL)])


def _gate_norm_bwd(g_y4, y_ssd, xa, proj, dskip_x, norm_w, tr=256):
    t = y_ssd.shape[0]
    gw = SSM_INNER // SSM_GROUPS

    def body(g_ref, y_ref, xs_ref, z_ref, d_ref, w_ref, gy2_ref, gz_ref, gnw_ref, gdx_ref, gd_ref):
        z = z_ref[...]
        xs = xs_ref[...]
        sg = _sigmoid(z)
        sz = z * sg
        y2 = y_ref[...] + d_ref[...] * xs
        y3 = y2 * sz
        g4 = g_ref[...]
        for g in range(SSM_GROUPS):
            sl = slice(g * gw, (g + 1) * gw)
            gy3, gnw = _rms_bwd(g4[:, sl], y3[:, sl], w_ref[:, sl])
            _accumulate(gnw_ref.at[:, sl], gnw)
            gy2 = gy3 * sz[:, sl]
            gy2_ref[:, sl] = gy2
            gz_ref[:, sl] = (gy3 * y2[:, sl] * (sg[:, sl] * (1.0 + z[:, sl] * (1.0 - sg[:, sl])))).astype(BF16)
            _accumulate(gdx_ref.at[:, sl], _colsum(gy2 * xs[:, sl]))
        tot = jnp.broadcast_to(gdx_ref[...], (8, SSM_INNER))
        gd_ref[...] = jnp.dot(tot, _head_reduce(SSM_HEADS, SSM_HEAD_DIM, LANE), precision=HI, preferred_element_type=F32)[0:1, :]

    return _rowcall(body, "gate_norm_bwd", t, tr,
                    [_rows(g_y4, tr), _rows(y_ssd, tr), _rows(xa, tr, SSM_INNER, 0), _rows(proj, tr, SSM_INNER, OFF_Z // SSM_INNER), _whole(dskip_x), _whole(norm_w)],
                    [_orow(t, SSM_INNER, F32, tr), _orow(t, SSM_INNER, BF16, tr), _oacc(SSM_INNER), _oacc(SSM_INNER), _oacc(LANE)])


def _dt_bwd(g_dtx, ga_rows, proj, dt_bias_pad, tr=512):
    t = proj.shape[0]

    def body(g_ref, ga_ref, raw_ref, b_ref, o_ref, gb_ref, gal_ref):
        red = _head_reduce(SSM_HEADS, SSM_HEAD_DIM, LANE)
        gdt = jnp.dot(g_ref[...], red, precision=HI, preferred_element_type=F32)
        graw = gdt * _sigmoid(raw_ref[...] + b_ref[...])
        o_ref[...] = graw.astype(BF16)
        _accumulate(gb_ref, _colsum(graw))
        tot = jnp.broadcast_to(_colsum(ga_ref[...]), (8, SSM_INNER))
        gal_ref[...] = jnp.dot(tot, red, precision=HI, preferred_element_type=F32)[0:1, :]

    return _rowcall(body, "dt_bwd", t, tr, [_rows(g_dtx, tr), _whole(ga_rows), _rows(proj, tr, LANE, OFF_DT // LANE), _whole(dt_bias_pad)],
                    [_orow(t, LANE, BF16, tr), _oacc(LANE), _oacc(LANE)])


def _conv_bwd_act(g_xs, g_b, g_c, proj, conv_w, conv_b, tr=256):
    t = proj.shape[0]
    cb = OFF_XBC // CONV_DIM
    halo = (proj, (8, CONV_DIM), lambda i: (jnp.maximum(i * (tr // 8) - 1, 0), cb))
    nb, nc = SSM_INNER, SSM_INNER + SSM_GROUPS * SSM_STATE

    def body(gxs_ref, gb_ref, gc_ref, cur_ref, prev_ref, w_ref, b_ref, o_ref, gcb_ref, gw0, gw1, gw2, gw3, ext):
        ext[pl.ds(0, 8), :] = jnp.where(pl.program_id(0) > 0, prev_ref[...], 0.0)
        ext[pl.ds(8, tr), :] = cur_ref[...]
        acc = b_ref[...] + w_ref[3:4, :] * cur_ref[...]
        for k in range(SSM_CONV - 1):
            acc = acc + w_ref[k:k + 1, :] * ext[pl.ds(8 - 3 + k, tr), :]
        sg = _sigmoid(acc)
        dsilu = sg * (1.0 + acc * (1.0 - sg))
        o_ref[:, :nb] = gxs_ref[...] * dsilu[:, :nb]
        o_ref[:, nb:nc] = gb_ref[...] * dsilu[:, nb:nc]
        o_ref[:, nc:] = gc_ref[...] * dsilu[:, nc:]
        gxc = o_ref[...]
        _accumulate(gcb_ref, _colsum(gxc))
        for k, gw in enumerate((gw0, gw1, gw2, gw3)):
            _accumulate(gw, _colsum(gxc * ext[pl.ds(8 - 3 + k, tr), :]))

    return _rowcall(body, "conv_bwd_act", t, tr,
                    [_rows(g_xs, tr), _rows(g_b, tr), _rows(g_c, tr), _rows(proj, tr, CONV_DIM, cb), halo, _whole(conv_w), _whole(conv_b)],
                    [_orow(t, CONV_DIM, F32, tr)] + [_oacc(CONV_DIM)] * 5, scratch=[pltpu.VMEM((tr + 8, CONV_DIM), F32)])


def _conv_bwd_in(g_xc, conv_w, tr=256):
    t = g_xc.shape[0]
    n_blk = t // tr
    halo = (g_xc, (8, CONV_DIM), lambda i: (jnp.minimum((i + 1) * (tr // 8), t // 8 - 1), 0))

    def body(cur_ref, nxt_ref, w_ref, o_ref, ext):
        ext[pl.ds(0, tr), :] = cur_ref[...]
        ext[pl.ds(tr, 8), :] = jnp.where(pl.program_id(0) < n_blk - 1, nxt_ref[...], 0.0)
        acc = w_ref[3:4, :] * cur_ref[...]
        for k in range(SSM_CONV - 1):
            acc = acc + w_ref[k:k + 1, :] * ext[pl.ds(3 - k, tr), :]
        o_ref[...] = acc.astype(BF16)

    return _rowcall(body, "conv_bwd_in", t, tr, [_rows(g_xc, tr), halo, _whole(conv_w)], [_orow(t, CONV_DIM, BF16, tr)],
                    scratch=[pltpu.VMEM((tr + 8, CONV_DIM), F32)])[0]


def _pre_norm_bwd(g_h1, g_u, x, w_pre, tr=512):
    t = x.shape[0]

    def body(gh_ref, gu_ref, x_ref, w_ref, gx_ref, gw_ref):
        gx, gw = _rms_bwd(gu_ref[...], x_ref[...], w_ref[...])
        gx_ref[...] = gh_ref[...] + gx
        _accumulate(gw_ref, gw)

    return _rowcall(body, "pre_norm_bwd", t, tr, [_rows(g_h1, tr), _rows(g_u, tr), _rows(x, tr), _whole(w_pre)],
                    [_orow(t, D_MODEL, F32, tr), _oacc(D_MODEL)])


def _alibi_slopes(n):
    def pow2(m):
        start = 2.0 ** (-8.0 / m)
        return [start ** (i + 1) for i in range(m)]
    if (n & (n - 1)) == 0:
        s = pow2(n)
    else:
        c = 2 ** int(math.floor(math.log2(n)))
        s = pow2(c) + pow2(2 * c)[0::2][: n - c]
    return np.array(s, dtype=np.float32)


def _slope_rows():
    s = _alibi_slopes(N_ATT_HEADS).reshape(N_ATT_HEADS // 2, 2)
    return jnp.asarray(np.broadcast_to(np.repeat(s, HEAD_DIM, axis=1)[:, None, :], (N_ATT_HEADS // 2, 8, LANE)).copy())


ATT_MAX_BLOCK_ROWS = 2048


def _att_geometry(t, dil):
    rows = t // dil
    bq = min(512, rows, ATT_MAX_BLOCK_ROWS // dil)
    return bq // ATT_BLOCK, rows // bq, bq * dil, ATT_BLOCK * dil


def _residue_rows(r, dil, first_block, n_blocks=1):
    if dil == 1:
        return pl.ds(first_block * ATT_BLOCK, n_blocks * ATT_BLOCK)
    return pl.ds(r + first_block * ATT_BLOCK * dil, n_blocks * ATT_BLOCK, stride=dil)


def _lane_half():
    return lax.broadcasted_iota(jnp.int32, (ATT_BLOCK, LANE), 1) // HEAD_DIM


def _att_scores_mask(dil, first):
    iq = lax.broadcasted_iota(jnp.int32, (ATT_BLOCK, 2 * ATT_BLOCK), 0)
    jk = lax.broadcasted_iota(jnp.int32, (ATT_BLOCK, 2 * ATT_BLOCK), 1)
    dist = ATT_BLOCK + iq - jk
    valid = (dist >= 0) & (dist <= ATT_BLOCK) & (jnp.logical_not(first) | (jk >= ATT_BLOCK))
    return (dist * dil).astype(F32), valid


def _att_fwd(proj, dil, slopes):
    t = proj.shape[0]
    nsub, nblk, rb, pb = _att_geometry(t, dil)
    qb = OFF_QKV // LANE

    def spec(off, prev=False):
        if prev:
            return pl.BlockSpec((pb, LANE), lambda hp, i, r: (jnp.maximum(i * nsub - 1, 0), qb + off + hp))
        return pl.BlockSpec((rb, LANE), lambda hp, i, r: (i, qb + off + hp))

    o_spec = pl.BlockSpec((rb, LANE), lambda hp, i, r: (i, hp))

    def body(q_ref, kc_ref, kp_ref, vc_ref, vp_ref, sl_ref, o_ref, lse_ref):
        i, r = pl.program_id(1), pl.program_id(2)
        half = _lane_half()
        for sub in range(nsub):
            rs = _residue_rows(r, dil, sub)
            q = (q_ref[rs, :] * (HEAD_DIM ** -0.5)).astype(BF16)
            if sub == 0:
                r0 = _residue_rows(r, dil, 0)
                kk = jnp.concatenate([kp_ref[r0, :], kc_ref[rs, :]], axis=0).astype(BF16)
                vv = jnp.concatenate([vp_ref[r0, :], vc_ref[rs, :]], axis=0).astype(BF16)
                first = i == 0
            else:
                ks = _residue_rows(r, dil, sub - 1, 2)
                kk, vv = kc_ref[ks, :].astype(BF16), vc_ref[ks, :].astype(BF16)
                first = jnp.bool_(False)
            dist, valid = _att_scores_mask(dil, first)
            outs, lses = [], []
            for e in range(2):
                qe = jnp.where(half == e, q, jnp.zeros_like(q))
                s = lax.dot_general(qe, kk, NT_DIMS, preferred_element_type=F32)
                s = jnp.where(valid, s - sl_ref[0:1, e * HEAD_DIM:e * HEAD_DIM + 1] * dist, NEG)
                m = jnp.max(s, axis=-1, keepdims=True)
                p = jnp.exp(s - m)
                l = jnp.sum(p, axis=-1, keepdims=True)
                outs.append(jnp.dot(p.astype(BF16), vv, preferred_element_type=F32) / l)
                lses.append(m + jnp.log(l))
            o_ref[rs, :] = jnp.where(half == 0, outs[0], outs[1])
            lse_ref[rs, :] = jnp.where(half == 0, lses[0], lses[1])

    return pl.pallas_call(
        body, grid=(N_ATT_HEADS // 2, nblk, dil),
        in_specs=[spec(0), spec(6), spec(6, True), spec(12), spec(12, True), pl.BlockSpec((None, 8, LANE), lambda hp, i, r: (hp, 0, 0))],
        out_specs=[o_spec, o_spec], out_shape=[S((t, ATT_WIDTH), F32), S((t, ATT_WIDTH), F32)],
        name=f"att_fwd_d{dil}", compiler_params=_params(("parallel", "parallel", "arbitrary")),
    )(proj, proj, proj, proj, proj, slopes)


def _att_combine(outs, lses, tr=512):
    t = outs[0].shape[0]

    def body(o0, o1, o2, l0, l1, l2, att_ref, lse_ref):
        ls = [l0[...], l1[...], l2[...]]
        m = jnp.maximum(jnp.maximum(ls[0], ls[1]), ls[2])
        ws = [jnp.exp(l - m) for l in ls]
        tot = ws[0] + ws[1] + ws[2]
        num = ws[0] * o0[...].astype(F32) + ws[1] * o1[...].astype(F32) + ws[2] * o2[...].astype(F32)
        att_ref[...] = (num / tot).astype(BF16)
        lse_ref[...] = m + jnp.log(tot)

    return _rowcall(body, "att_combine", t, tr, [_rows(a, tr) for a in list(outs) + list(lses)],
                    [_orow(t, ATT_WIDTH, BF16, tr), _orow(t, ATT_WIDTH, F32, tr)])


def _att_delta(g_att, att, tr=512):
    t = att.shape[0]

    def body(g_ref, a_ref, o_ref):
        prod = g_ref[...] * a_ref[...].astype(F32)
        o_ref[...] = jnp.dot(prod, _block_ones(ATT_WIDTH, HEAD_DIM), precision=HI, preferred_element_type=F32)

    return _rowcall(body, "att_delta", t, tr, [_rows(g_att, tr), _rows(att, tr)], [_orow(t, ATT_WIDTH, F32, tr)])[0]


def _att_bwd_q(proj, g_att, lse, delta, dil, slopes):
    t = proj.shape[0]
    nsub, nblk, rb, pb = _att_geometry(t, dil)
    qb = OFF_QKV // LANE

    def spec(off, prev=False):
        if prev:
            return pl.BlockSpec((pb, LANE), lambda hp, i, r: (jnp.maximum(i * nsub - 1, 0), qb + off + hp))
        return pl.BlockSpec((rb, LANE), lambda hp, i, r: (i, qb + off + hp))

    o_spec = pl.BlockSpec((rb, LANE), lambda hp, i, r: (i, hp))

    def body(q_ref, kc_ref, kp_ref, vc_ref, vp_ref, do_ref, lse_ref, dl_ref, sl_ref, dq_ref):
        i, r = pl.program_id(1), pl.program_id(2)
        half = _lane_half()
        for sub in range(nsub):
            rs = _residue_rows(r, dil, sub)
            q = (q_ref[rs, :] * (HEAD_DIM ** -0.5)).astype(BF16)
            do = do_ref[rs, :].astype(BF16)
            lse_q, dl_q = lse_ref[rs, :], dl_ref[rs, :]
            if sub == 0:
                r0 = _residue_rows(r, dil, 0)
                kk = jnp.concatenate([kp_ref[r0, :], kc_ref[rs, :]], axis=0).astype(BF16)
                vv = jnp.concatenate([vp_ref[r0, :], vc_ref[rs, :]], axis=0).astype(BF16)
                first = i == 0
            else:
                ks = _residue_rows(r, dil, sub - 1, 2)
                kk, vv = kc_ref[ks, :].astype(BF16), vc_ref[ks, :].astype(BF16)
                first = jnp.bool_(False)
            dist, valid = _att_scores_mask(dil, first)
            dqs = []
            for e in range(2):
                c = e * HEAD_DIM
                qe = jnp.where(half == e, q, jnp.zeros_like(q))
                doe = jnp.where(half == e, do, jnp.zeros_like(do))
                s = lax.dot_general(qe, kk, NT_DIMS, preferred_element_type=F32)
                s = jnp.where(valid, s - sl_ref[0:1, c:c + 1] * dist, NEG)
                p = jnp.exp(s - lse_q[:, c:c + 1])
                dp = lax.dot_general(doe, vv, NT_DIMS, preferred_element_type=F32)
                ds = p * (dp - dl_q[:, c:c + 1])
                dqs.append(jnp.dot(ds.astype(BF16), kk, preferred_element_type=F32))
            dq_ref[rs, :] = jnp.where(half == 0, dqs[0], dqs[1]) * (HEAD_DIM ** -0.5)

    return pl.pallas_call(
        body, grid=(N_ATT_HEADS // 2, nblk, dil),
        in_specs=[spec(0), spec(6), spec(6, True), spec(12), spec(12, True), o_spec, o_spec, o_spec,
                  pl.BlockSpec((None, 8, LANE), lambda hp, i, r: (hp, 0, 0))],
        out_specs=o_spec, out_shape=S((t, ATT_WIDTH), F32),
        name=f"att_bwd_q_d{dil}", compiler_params=_params(("parallel", "parallel", "arbitrary")),
    )(proj, proj, proj, proj, proj, g_att, lse, delta, slopes)


def _att_bwd_kv(proj, g_att, lse, delta, dil, slopes):
    t = proj.shape[0]
    nsub, nblk, rb, pb = _att_geometry(t, dil)
    qb = OFF_QKV // LANE
    n_pb = t // pb

    def pspec(off, nxt=False):
        if nxt:
            return pl.BlockSpec((pb, LANE), lambda hp, i, r: (jnp.minimum((i + 1) * nsub, n_pb - 1), qb + off + hp))
        return pl.BlockSpec((rb, LANE), lambda hp, i, r: (i, qb + off + hp))

    def aspec(nxt=False):
        if nxt:
            return pl.BlockSpec((pb, LANE), lambda hp, i, r: (jnp.minimum((i + 1) * nsub, n_pb - 1), hp))
        return pl.BlockSpec((rb, LANE), lambda hp, i, r: (i, hp))

    def body(qc_ref, qn_ref, k_ref, v_ref, doc_ref, don_ref, lsec_ref, lsen_ref, dlc_ref, dln_ref, sl_ref, dk_ref, dv_ref):
        i, r = pl.program_id(1), pl.program_id(2)
        half = _lane_half()
        jk = lax.broadcasted_iota(jnp.int32, (ATT_BLOCK, 2 * ATT_BLOCK), 0)
        cq = lax.broadcasted_iota(jnp.int32, (ATT_BLOCK, 2 * ATT_BLOCK), 1)
        dist_i = cq - jk
        dist = (dist_i * dil).astype(F32)
        for sub in range(nsub):
            rs = _residue_rows(r, dil, sub)

            def both(cur_ref, nxt_ref):
                if sub < nsub - 1:
                    return cur_ref[_residue_rows(r, dil, sub, 2), :]
                return jnp.concatenate([cur_ref[rs, :], nxt_ref[_residue_rows(r, dil, 0), :]], axis=0)

            has_next = jnp.bool_(True) if sub < nsub - 1 else i < nblk - 1
            valid = (dist_i >= 0) & (dist_i <= ATT_BLOCK) & (has_next | (cq < ATT_BLOCK))
            qq = (both(qc_ref, qn_ref) * (HEAD_DIM ** -0.5)).astype(BF16)
            doo = both(doc_ref, don_ref).astype(BF16)
            lse_t = both(lsec_ref, lsen_ref).T
            dl_t = both(dlc_ref, dln_ref).T
            k = k_ref[rs, :].astype(BF16)
            v = v_ref[rs, :].astype(BF16)
            dks, dvs = [], []
            for e in range(2):
                c = e * HEAD_DIM
                ke = jnp.where(half == e, k, jnp.zeros_like(k))
                ve = jnp.where(half == e, v, jnp.zeros_like(v))
                st = lax.dot_general(ke, qq, NT_DIMS, preferred_element_type=F32)
                st = jnp.where(valid, st - sl_ref[0:1, c:c + 1] * dist, NEG)
                pt = jnp.exp(st - lse_t[c:c + 1, :])
                dpt = lax.dot_general(ve, doo, NT_DIMS, preferred_element_type=F32)
                dst = pt * (dpt - dl_t[c:c + 1, :])
                dks.append(jnp.dot(dst.astype(BF16), qq, preferred_element_type=F32))
                dvs.append(jnp.dot(pt.astype(BF16), doo, preferred_element_type=F32))
            dk_ref[rs, :] = jnp.where(half == 0, dks[0], dks[1])
            dv_ref[rs, :] = jnp.where(half == 0, dvs[0], dvs[1])

    return pl.pallas_call(
        body, grid=(N_ATT_HEADS // 2, nblk, dil),
        in_specs=[pspec(0), pspec(0, True), pspec(6), pspec(12), aspec(), aspec(True), aspec(), aspec(True), aspec(), aspec(True),
                  pl.BlockSpec((None, 8, LANE), lambda hp, i, r: (hp, 0, 0))],
        out_specs=[aspec(), aspec()], out_shape=[S((t, ATT_WIDTH), F32)] * 2,
        name=f"att_bwd_kv_d{dil}", compiler_params=_params(("parallel", "parallel", "arbitrary")),
    )(proj, proj, proj, proj, g_att, g_att, lse, lse, delta, delta, slopes)


def _att_grad_sum(dqs, dks, dvs, tr=512):
    t = dqs[0].shape[0]

    def body(*refs):
        o_ref = refs[-1]
        for n in range(3):
            tot = refs[3 * n][...] + refs[3 * n + 1][...] + refs[3 * n + 2][...]
            o_ref[:, n * ATT_WIDTH:(n + 1) * ATT_WIDTH] = tot.astype(BF16)

    return _rowcall(body, "att_grad_sum", t, tr, [_rows(a, tr) for a in list(dqs) + list(dks) + list(dvs)],
                    [_orow(t, 3 * ATT_WIDTH, BF16, tr)])[0]


def _ssd_common(xs, dtx, alx, dtt, alg, b, c):
    ch = SSM_CHUNK
    row = lax.broadcasted_iota(jnp.int32, (ch, ch), 0)
    col = lax.broadcasted_iota(jnp.int32, (ch, ch), 1)
    tril = (col <= row)
    la = dtx * (-jnp.exp(alx))
    cs = jnp.dot(tril.astype(F32), la, precision=HI, preferred_element_type=F32)
    la_t = dtt * (-jnp.exp(alg))
    cs_t = jnp.dot(la_t, (row <= col).astype(F32), precision=HI, preferred_element_type=F32)
    cs_last = cs[ch - 1:ch, :]
    return dict(tril=tril, row=row, col=col, la=la, cs=cs, cs_t=cs_t, cs_last=cs_last,
                e=jnp.exp(cs), w=jnp.exp(cs_last - cs), xd=xs * dtx)


def _decay_col(cs_t, heads_per_group):
    r = lax.broadcasted_iota(jnp.int32, (heads_per_group * SSM_HEAD_DIM, SSM_STATE), 0) // SSM_HEAD_DIM
    out = jnp.zeros((heads_per_group * SSM_HEAD_DIM, SSM_STATE), F32)
    for j in range(heads_per_group):
        out = jnp.where(r == j, jnp.exp(cs_t[j:j + 1, SSM_CHUNK - 1:SSM_CHUNK]), out)
    return out


def _ssd_specs(t):
    hg = SSM_HEADS // SSM_GROUPS
    gw = hg * SSM_HEAD_DIM
    nb0 = SSM_INNER // SSM_STATE
    return hg, gw, nb0


def _ssd_fwd(xa, dtx, dtt_g, alog_x, alog_g):
    t = xa.shape[0]
    nch = t // SSM_CHUNK
    hg, gw, nb0 = _ssd_specs(t)
    ch = SSM_CHUNK

    def body(xs_ref, b_ref, c_ref, dtx_ref, dtt_ref, alx_ref, alg_ref, y_ref, st_ref, h_scr):
        cc, g = pl.program_id(0), pl.program_id(1)

        @pl.when(cc == 0)
        def _():
            h_scr[g] = jnp.zeros((gw, SSM_STATE), F32)

        q = _ssd_common(xs_ref[...], dtx_ref[...], alx_ref[...], dtt_ref[...], alg_ref[...], b_ref[...], c_ref[...])
        bb, cb = b_ref[...].astype(BF16), c_ref[...].astype(BF16)
        cbm = lax.dot_general(cb, bb, NT_DIMS, preferred_element_type=F32)
        h = h_scr[g]
        st_ref[...] = h
        xd16 = q["xd"].astype(BF16)
        y = lax.dot_general(cb, h.astype(BF16), NT_DIMS, preferred_element_type=F32) * q["e"]
        lane_head = lax.broadcasted_iota(jnp.int32, (ch, gw), 1) // SSM_HEAD_DIM
        for j in range(hg):
            diff = q["cs"][:, j * SSM_HEAD_DIM:j * SSM_HEAD_DIM + 1] - q["cs_t"][j:j + 1, :]
            gmat = cbm * jnp.exp(jnp.where(q["tril"], diff, NEG))
            yj = jnp.dot(gmat.astype(BF16), xd16, preferred_element_type=F32)
            y = y + jnp.where(lane_head == j, yj, 0.0)
        y_ref[...] = y
        s_new = lax.dot_general((q["xd"] * q["w"]).astype(BF16), bb, TN_DIMS, preferred_element_type=F32)
        h_scr[g] = _decay_col(q["cs_t"], hg) * h + s_new

    return pl.pallas_call(
        body, grid=(nch, SSM_GROUPS),
        in_specs=[pl.BlockSpec((ch, gw), lambda cc, g: (cc, g)),
                  pl.BlockSpec((ch, SSM_STATE), lambda cc, g: (cc, nb0 + g)),
                  pl.BlockSpec((ch, SSM_STATE), lambda cc, g: (cc, nb0 + SSM_GROUPS + g)),
                  pl.BlockSpec((ch, gw), lambda cc, g: (cc, g)),
                  pl.BlockSpec((None, 8, ch), lambda cc, g: (g, 0, cc)),
                  pl.BlockSpec((1, gw), lambda cc, g: (0, g)),
                  pl.BlockSpec((None, 8, ch), lambda cc, g: (g, 0, 0))],
        out_specs=[pl.BlockSpec((ch, gw), lambda cc, g: (cc, g)),
                   pl.BlockSpec((None, None, gw, SSM_STATE), lambda cc, g: (cc, g, 0, 0))],
        out_shape=[S((t, SSM_INNER), F32), S((nch, SSM_GROUPS, gw, SSM_STATE), F32)],
        scratch_shapes=[pltpu.VMEM((SSM_GROUPS, gw, SSM_STATE), F32)],
        name="ssd_fwd", compiler_params=_params(("arbitrary", "arbitrary")),
    )(xa, xa, xa, dtx, dtt_g, alog_x, alog_g)


def _ssd_bwd(xa, dtx, dtt_g, alog_x, alog_g, g_y, states, dskip_x):
    t = xa.shape[0]
    nch = t // SSM_CHUNK
    hg, gw, nb0 = _ssd_specs(t)
    ch = SSM_CHUNK

    def rc(cc):
        return nch - 1 - cc

    def body(xs_ref, b_ref, c_ref, dtx_ref, dtt_ref, alx_ref, alg_ref, gy_ref, st_ref, dsk_ref,
             gxs_ref, gb_ref, gc_ref, gdt_ref, ga_ref, gh_scr):
        cc, g = pl.program_id(0), pl.program_id(1)

        @pl.when(cc == 0)
        def _():
            gh_scr[g] = jnp.zeros((gw, SSM_STATE), F32)

        xs, dtx = xs_ref[...], dtx_ref[...]
        q = _ssd_common(xs, dtx, alx_ref[...], dtt_ref[...], alg_ref[...], b_ref[...], c_ref[...])
        cs, cs_t, e, w, xd = q["cs"], q["cs_t"], q["e"], q["w"], q["xd"]
        bb, cb = b_ref[...].astype(BF16), c_ref[...].astype(BF16)
        gy = gy_ref[...]
        gy16, xd16 = gy.astype(BF16), xd.astype(BF16)
        h = st_ref[...]
        h16 = h.astype(BF16)
        ghn = gh_scr[g]
        ghn16 = ghn.astype(BF16)
        seg = _block_ones(gw, SSM_HEAD_DIM)
        cbm = lax.dot_general(cb, bb, NT_DIMS, preferred_element_type=F32)
        cbt = lax.dot_general(bb, cb, NT_DIMS, preferred_element_type=F32)

        gye16 = (gy * e).astype(BF16)
        chm = lax.dot_general(cb, h16, NT_DIMS, preferred_element_type=F32)
        g_c = jnp.dot(gye16, h16, preferred_element_type=F32)
        gh_off = lax.dot_general(gye16, cb, TN_DIMS, preferred_element_type=F32)
        g_e = jnp.dot(gy * chm, seg, precision=HI, preferred_element_type=F32)
        bgs = lax.dot_general(bb, ghn16, NT_DIMS, preferred_element_type=F32)
        g_xd = w * bgs
        g_w = jnp.dot(xd * bgs, seg, precision=HI, preferred_element_type=F32)
        g_b = jnp.dot((xd * w).astype(BF16), ghn16, preferred_element_type=F32)
        decay = _decay_col(cs_t, hg)
        gh_scr[g] = decay * ghn + gh_off
        rsum = jnp.sum(ghn * h, axis=1, keepdims=True)
        lane_head = lax.broadcasted_iota(jnp.int32, (ch, gw), 1) // SSM_HEAD_DIM
        lane_head1 = lax.broadcasted_iota(jnp.int32, (1, gw), 1) // SSM_HEAD_DIM
        g_el = jnp.zeros((1, gw), F32)
        g_cs = g_e * e - g_w * w
        upper = q["row"] <= q["col"]
        for j in range(hg):
            g_el = jnp.where(lane_head1 == j, jnp.sum(rsum[j * SSM_HEAD_DIM:(j + 1) * SSM_HEAD_DIM, :], axis=0, keepdims=True), g_el)
            csc = cs[:, j * SSM_HEAD_DIM:j * SSM_HEAD_DIM + 1]
            csr = cs_t[j:j + 1, :]
            lm = jnp.exp(jnp.where(q["tril"], csc - csr, NEG))
            lmt = jnp.exp(jnp.where(upper, csr - csc, NEG))
            gyj = jnp.where(lane_head == j, gy16, jnp.zeros_like(gy16))
            xdj = jnp.where(lane_head == j, xd16, jnp.zeros_like(xd16))
            gg = lax.dot_general(gyj, xd16, NT_DIMS, preferred_element_type=F32)
            ggt = lax.dot_general(xdj, gy16, NT_DIMS, preferred_element_type=F32)
            gcb, gcbt = gg * lm, ggt * lmt
            g_c = g_c + jnp.dot(gcb.astype(BF16), bb, preferred_element_type=F32)
            g_b = g_b + jnp.dot(gcbt.astype(BF16), cb, preferred_element_type=F32)
            gxdj = jnp.dot((cbt * lmt).astype(BF16), gy16, preferred_element_type=F32)
            g_xd = g_xd + jnp.where(lane_head == j, gxdj, 0.0)
            d_cs = jnp.sum(gcb * cbm, axis=1, keepdims=True) - jnp.sum(gcbt * cbt, axis=1, keepdims=True)
            g_cs = g_cs + jnp.where(lane_head == j, d_cs, 0.0)
        extra = _colsum(g_w * w) + g_el * jnp.exp(q["cs_last"])
        g_cs = g_cs + jnp.where(lax.broadcasted_iota(jnp.int32, (ch, gw), 0) == ch - 1, extra, 0.0)
        g_la = jnp.dot(upper.astype(F32), g_cs, precision=HI, preferred_element_type=F32)
        a_x = -jnp.exp(alx_ref[...])
        gdt_ref[...] = g_xd * xs + g_la * a_x * (1.0 / SSM_HEAD_DIM)
        ga_row = _colsum(g_la * q["la"]) * (1.0 / SSM_HEAD_DIM)
        ga_ref[...] = jnp.where(lax.broadcasted_iota(jnp.int32, (8, gw), 0) == 0, ga_row, 0.0)
        gxs_ref[...] = g_xd * dtx + gy * dsk_ref[...]
        gb_ref[...] = g_b
        gc_ref[...] = g_c

    return pl.pallas_call(
        body, grid=(nch, SSM_GROUPS),
        in_specs=[pl.BlockSpec((ch, gw), lambda cc, g: (rc(cc), g)),
                  pl.BlockSpec((ch, SSM_STATE), lambda cc, g: (rc(cc), nb0 + g)),
                  pl.BlockSpec((ch, SSM_STATE), lambda cc, g: (rc(cc), nb0 + SSM_GROUPS + g)),
                  pl.BlockSpec((ch, gw), lambda cc, g: (rc(cc), g)),
                  pl.BlockSpec((None, 8, ch), lambda cc, g: (g, 0, rc(cc))),
                  pl.BlockSpec((1, gw), lambda cc, g: (0, g)),
                  pl.BlockSpec((None, 8, ch), lambda cc, g: (g, 0, 0)),
                  pl.BlockSpec((ch, gw), lambda cc, g: (rc(cc), g)),
                  pl.BlockSpec((None, None, gw, SSM_STATE), lambda cc, g: (rc(cc), g, 0, 0)),
                  pl.BlockSpec((1, gw), lambda cc, g: (0, g))],
        out_specs=[pl.BlockSpec((ch, gw), lambda cc, g: (rc(cc), g)),
                   pl.BlockSpec((ch, SSM_STATE), lambda cc, g: (rc(cc), g)),
                   pl.BlockSpec((ch, SSM_STATE), lambda cc, g: (rc(cc), g)),
                   pl.BlockSpec((ch, gw), lambda cc, g: (rc(cc), g)),
                   pl.BlockSpec((8, gw), lambda cc, g: (rc(cc), g))],
        out_shape=[S((t, SSM_INNER), F32), S((t, SSM_GROUPS * SSM_STATE), F32), S((t, SSM_GROUPS * SSM_STATE), F32),
                   S((t, SSM_INNER), F32), S((nch * 8, SSM_INNER), F32)],
        scratch_shapes=[pltpu.VMEM((SSM_GROUPS, gw, SSM_STATE), F32)],
        name="ssd_bwd", compiler_params=_params(("arbitrary", "arbitrary")),
    )(xa, xa, xa, dtx, dtt_g, alog_x, alog_g, g_y, states, dskip_x)


def _local_step(x, target, w_pre, w_in_r, b_gate, conv_w, conv_b, dt_bias, a_log, d_skip, ssm_norm_w,
                w_att, w_ssm, w_out, w_post, w_fpre, w_up, w_down, w_fpost):
    t = x.shape[0]
    mm = functools.partial(_matmul, tm=512)
    slopes = _slope_rows()
    hg = SSM_HEADS // SSM_GROUPS
    dt_bias_pad = jnp.pad(dt_bias, ((0, 0), (0, LANE - SSM_HEADS)))
    alog_x = jnp.repeat(a_log, SSM_HEAD_DIM, axis=1)
    alog_g = jnp.broadcast_to(jnp.pad(a_log.reshape(SSM_GROUPS, hg), ((0, 0), (0, 8 - hg)))[:, :, None], (SSM_GROUPS, 8, SSM_CHUNK))
    dskip_x = jnp.repeat(d_skip, SSM_HEAD_DIM, axis=1)

    u = _pre_norm(x, w_pre)
    proj = mm(u, w_in_r, mode="nn", out_dtype=F32, name="in_proj", tn=1792, tk=D_MODEL)
    fwd = [_att_fwd(proj, dil, slopes) for _, dil in DILATED_PATTERNS]
    att, lse = _att_combine([o for o, _ in fwd], [l for _, l in fwd])
    xa = _conv_fwd(proj, conv_w, conv_b)
    dtx, dtt = _dt_fwd(proj, dt_bias_pad)
    dtt_g = jnp.pad(dtt[:SSM_HEADS].reshape(SSM_GROUPS, hg, t), ((0, 0), (0, 8 - hg), (0, 0)))
    y_ssd, states = _ssd_fwd(xa, dtx, dtt_g, alog_x, alog_g)
    y4 = _gate_norm_fwd(y_ssd, xa, proj, dskip_x, ssm_norm_w)
    att_p = mm(att, w_att, mode="nn", out_dtype=F32, name="att_proj", tn=D_MODEL // N_DEV, tk=ATT_WIDTH, stacked=True)
    ssm_p = mm(y4, w_ssm, mode="nn", out_dtype=F32, name="ssm_proj", tn=D_MODEL, tk=SSM_INNER)
    mixin = _gating_fwd(proj, b_gate, att_p, ssm_p)
    mixed = mm(mixin, w_out, mode="nn", out_dtype=F32, name="out_proj", tn=D_MODEL, tk=D_MODEL)
    h1, f = _mix_post_ffn_pre(x, mixed, w_post, w_fpre)
    act, up = mm(f, w_up, mode="nn", out_dtype=BF16, name="ffn_up", tn=FFN_HIDDEN // N_DEV, tk=D_MODEL, epilogue="relu2", stacked=True)
    dn = mm(act, w_down, mode="nn", out_dtype=F32, name="ffn_down", tn=D_MODEL, tk=FFN_HIDDEN)
    loss, g_h2, g_dn, gw_fpost = _loss_and_ffn_post_bwd(h1, dn, w_fpost, target)

    g_up = mm(g_dn, w_down, mode="nt", out_dtype=BF16, name="ffn_down_bwd_x", tn=2048, tk=D_MODEL, epilogue="relu2_bwd", extra=up)
    gw_down = _matmul(act, g_dn, mode="tn", out_dtype=BF16, name="ffn_down_bwd_w", tm=1024, tn=D_MODEL, tk=512)
    g_f = mm(g_up, w_up, mode="nt", out_dtype=F32, name="ffn_up_bwd_x", tn=D_MODEL, tk=FFN_HIDDEN // N_DEV, stacked=True)
    gw_up = _matmul(f, g_up, mode="tn", out_dtype=BF16, name="ffn_up_bwd_w", tm=D_MODEL, tn=FFN_HIDDEN // N_DEV, tk=512, stacked=True)
    g_h1, g_mixed, gw_fpre, gw_post = _ffn_pre_mix_post_bwd(g_h2, g_f, h1, w_fpre, mixed, w_post)
    g_mixin = mm(g_mixed, w_out, mode="nt", out_dtype=F32, name="out_proj_bwd_x", tn=D_MODEL, tk=D_MODEL)
    gw_out = _matmul(mixin, g_mixed, mode="tn", out_dtype=BF16, name="out_proj_bwd_w", tm=D_MODEL, tn=D_MODEL, tk=512)
    g_att_p, g_ssm_p, g_gl, g_b_gate = _gating_bwd(g_mixin, proj, b_gate, att_p, ssm_p)
    g_att = mm(g_att_p, w_att, mode="nt", out_dtype=F32, name="att_proj_bwd_x", tn=ATT_WIDTH, tk=D_MODEL // N_DEV, stacked=True)
    gw_att = _matmul(att, g_att_p, mode="tn", out_dtype=BF16, name="att_proj_bwd_w", tm=ATT_WIDTH, tn=D_MODEL // N_DEV, tk=512, stacked=True)
    g_y4 = mm(g_ssm_p, w_ssm, mode="nt", out_dtype=F32, name="ssm_proj_bwd_x", tn=SSM_INNER, tk=D_MODEL)
    gw_ssm = _matmul(y4, g_ssm_p, mode="tn", out_dtype=BF16, name="ssm_proj_bwd_w", tm=1024, tn=D_MODEL, tk=512)
    g_y2, g_z, g_norm_w, _, g_d_skip = _gate_norm_bwd(g_y4, y_ssd, xa, proj, dskip_x, ssm_norm_w)
    g_xs, g_bm, g_cm, g_dtx, ga_rows = _ssd_bwd(xa, dtx, dtt_g, alog_x, alog_g, g_y2, states, dskip_x)
    g_dt_raw, g_dt_bias, g_a_log = _dt_bwd(g_dtx, ga_rows, proj, dt_bias_pad)
    g_xc, g_conv_b, gcw0, gcw1, gcw2, gcw3 = _conv_bwd_act(g_xs, g_bm, g_cm, proj, conv_w, conv_b)
    g_xbc = _conv_bwd_in(g_xc, conv_w)
    delta = _att_delta(g_att, att)
    dqs, dks, dvs = [], [], []
    for _, dil in DILATED_PATTERNS:
        dqs.append(_att_bwd_q(proj, g_att, lse, delta, dil, slopes))
        dk, dv = _att_bwd_kv(proj, g_att, lse, delta, dil, slopes)
        dks.append(dk)
        dvs.append(dv)
    g_qkv = _att_grad_sum(dqs, dks, dvs)
    g_proj = jnp.concatenate([g_z, g_gl, g_xbc, g_qkv, g_dt_raw, jnp.zeros((t, PROJ_W - OFF_DT - LANE), BF16)], axis=1)
    g_u = mm(g_proj, w_in_r, mode="nt", out_dtype=F32, name="in_proj_bwd_x", tn=D_MODEL, tk=1792)
    gw_in_r = _matmul(u, g_proj, mode="tn", out_dtype=BF16, name="in_proj_bwd_w", tm=D_MODEL, tn=1792, tk=512)
    g_x, gw_pre = _pre_norm_bwd(g_h1, g_u, x, w_pre)

    grads = dict(
        norm_mix_pre_w=gw_pre, w_in_r=gw_in_r, b_gate=g_b_gate, conv_w=jnp.concatenate([gcw0, gcw1, gcw2, gcw3], axis=0),
        conv_b=g_conv_b, dt_bias=g_dt_bias[:, :SSM_HEADS], a_log=g_a_log[:, :SSM_HEADS], d_skip=g_d_skip[:, :SSM_HEADS],
        ssm_norm_w=g_norm_w, w_att_proj=gw_att, w_ssm_proj=gw_ssm, w_out=gw_out, norm_mix_post_w=gw_post,
        norm_ffn_pre_w=gw_fpre, w_up=gw_up, w_down=gw_down, norm_ffn_post_w=gw_fpost)
    return loss, g_x, grads


def _mesh_pos():
    return lax.axis_index("x"), lax.axis_index("y"), lax.axis_index("c")


def _all_gather(shards):
    n = len(shards)

    def body(*refs):
        x_refs, o_refs = refs[:n], refs[n:2 * n]
        send_sems, recv_sems, local_sems = refs[2 * n:]
        x, y, c = _mesh_pos()
        me, sibling = (x, y, c), (x, y, 1 - c)
        chips = [(1 - x, y), (x, 1 - y), (1 - x, 1 - y)]

        def copy(a, k, block, to, src=None):
            dst = o_refs[a].at[4 * block[0] + 2 * block[1] + block[2]]
            return pltpu.make_async_remote_copy(
                src_ref=dst if src is None else src, dst_ref=dst, send_sem=send_sems.at[7 * a + k], recv_sem=recv_sems.at[7 * a + k],
                device_id=to, device_id_type=pl.DeviceIdType.MESH)

        mine = [pltpu.make_async_copy(x_refs[a], o_refs[a].at[4 * x + 2 * y + c], local_sems.at[a]) for a in range(n)]
        for cp in mine:
            cp.start()
        first = []
        for a in range(n):
            first.append(copy(a, 0, me, sibling, src=x_refs[a]))
            first += [copy(a, 1 + j, me, (*chip, c), src=x_refs[a]) for j, chip in enumerate(chips)]
        for cp in first:
            cp.start()
        passed = []
        for j, chip in enumerate(chips):
            for a in range(n):
                copy(a, 1 + j, (*chip, c), me).wait_recv()
                passed.append(copy(a, 4 + j, (*chip, c), sibling))
                passed[-1].start()
        for a in range(n):
            copy(a, 0, sibling, me).wait_recv()
            for j, chip in enumerate(chips):
                copy(a, 4 + j, (*chip, 1 - c), me).wait_recv()
        for cp in first + passed:
            cp.wait_send()
        for cp in mine:
            cp.wait()

    hbm = pl.BlockSpec(memory_space=pltpu.HBM)
    return pl.pallas_call(
        body, out_shape=[S((N_DEV,) + s.shape, s.dtype) for s in shards],
        in_specs=[hbm] * n, out_specs=[hbm] * n,
        scratch_shapes=[pltpu.SemaphoreType.DMA((7 * n,)), pltpu.SemaphoreType.DMA((7 * n,)), pltpu.SemaphoreType.DMA((n,))],
        name="weights_all_gather",
    )(*shards)


def _exchange_grads(slab_arrays, small):
    n = len(slab_arrays)
    r_small = small.shape[0]

    def body(*refs):
        slab_refs, small_ref = refs[:n], refs[n]
        recv_refs, gsm_ref = refs[n + 1:2 * n + 1], refs[2 * n + 1]
        send_sems, recv_sems, local_sems = refs[2 * n + 2:]
        x, y, c = _mesh_pos()
        me = 4 * x + 2 * y + c

        def peer(k):
            px = 1 - x if k & 4 else x
            py = 1 - y if k & 2 else y
            pc = 1 - c if k & 1 else c
            return (px, py, pc), 4 * px + 2 * py + pc

        def copy(a, k, sending):
            to, lin = peer(k)
            sem = 7 * a + k - 1
            if a == n:
                src, dst = small_ref, gsm_ref.at[me if sending else lin]
            else:
                src, dst = slab_refs[a].at[lin], recv_refs[a].at[me if sending else lin]
            return pltpu.make_async_remote_copy(src_ref=src, dst_ref=dst, send_sem=send_sems.at[sem], recv_sem=recv_sems.at[sem],
                                                device_id=to, device_id_type=pl.DeviceIdType.MESH)

        own = [pltpu.make_async_copy(slab_refs[a].at[me], recv_refs[a].at[me], local_sems.at[a]) for a in range(n)]
        own.append(pltpu.make_async_copy(small_ref, gsm_ref.at[me], local_sems.at[n]))
        for cp in own:
            cp.start()
        order = [n] + list(range(n))
        sends = [copy(a, k, True) for a in order for k in range(1, N_DEV)]
        for cp in sends:
            cp.start()
        for a in order:
            for k in range(1, N_DEV):
                copy(a, k, False).wait_recv()
        for cp in sends:
            cp.wait_send()
        for cp in own:
            cp.wait()

    hbm = pl.BlockSpec(memory_space=pltpu.HBM)
    n_sem = 7 * (n + 1)
    res = pl.pallas_call(
        body, out_shape=[S(a.shape, a.dtype) for a in slab_arrays] + [S((N_DEV, r_small, LANE), small.dtype)],
        in_specs=[hbm] * (n + 1), out_specs=[hbm] * (n + 1),
        scratch_shapes=[pltpu.SemaphoreType.DMA((n_sem,)), pltpu.SemaphoreType.DMA((n_sem,)), pltpu.SemaphoreType.DMA((n + 1,))],
        name="grad_exchange",
    )(*slab_arrays, small)
    return res[:n], res[n]


def _adamw(w, m, v, slabs, name, tr):
    r, cols = w.shape
    c1 = 1.0 - ADAM_B1 ** ADAM_STEP
    c2 = 1.0 - ADAM_B2 ** ADAM_STEP

    def body(w_ref, m_ref, v_ref, s_ref, g_ref, d_ref, nm_ref, nv_ref):
        g = s_ref[0].astype(F32)
        for d in range(1, N_DEV):
            g = g + s_ref[d].astype(F32)
        nm = ADAM_B1 * m_ref[...] + (1.0 - ADAM_B1) * g
        nv = ADAM_B2 * v_ref[...] + (1.0 - ADAM_B2) * (g * g)
        g_ref[...] = g
        nm_ref[...] = nm
        nv_ref[...] = nv
        d_ref[...] = -ADAM_LR * ((nm / c1) / (jnp.sqrt(nv / c2) + ADAM_EPS) + ADAM_WD * w_ref[...])

    assert r % tr == 0, name
    blk = pl.BlockSpec((tr, cols), lambda i: (i, 0))
    return pl.pallas_call(
        body, grid=(r // tr,), in_specs=[blk, blk, blk, pl.BlockSpec((N_DEV, tr, cols), lambda i: (0, i, 0))],
        out_specs=[blk] * 4, out_shape=[S((r, cols), F32)] * 4, name=name, compiler_params=_params(("parallel",)),
    )(w, m, v, slabs)


BIG = ("w_in", "w_att_proj", "w_up", "w_ssm_proj", "w_out", "w_down", "conv_w")
ADAMW_ROWS = dict(w_in=256, w_att_proj=768, w_up=512, w_ssm_proj=256, w_out=128, w_down=256, conv_w=4)
SMALL = ("norm_mix_pre_w", "b_gate", "conv_b", "dt_bias", "a_log", "d_skip", "ssm_norm_w", "norm_mix_post_w",
         "norm_ffn_pre_w", "norm_ffn_post_w")
ORDER = ("norm_mix_pre_w", "w_in", "b_gate", "conv_w", "conv_b", "dt_bias", "a_log", "d_skip", "ssm_norm_w", "w_att_proj",
         "w_ssm_proj", "w_out", "norm_mix_post_w", "norm_ffn_pre_w", "w_up", "w_down", "norm_ffn_post_w")
ROW_SHARDED = ("w_ssm_proj", "w_out", "w_down")
IN_PROJ_W = 10528
IN_SHARD_W = IN_PROJ_W // N_DEV
IN_SEGMENTS = ((2304, 4352), (8480, 10528), (4352, 8448), (0, 2304), (8448, 8480))


def _pack(parts, rows_multiple):
    flat = jnp.concatenate([p.reshape(-1) for p in parts])
    pad = (-flat.shape[0]) % (rows_multiple * LANE)
    return jnp.pad(flat, (0, pad)).reshape(-1, LANE)


def _unpack(flat2d, shapes):
    flat, out, off = flat2d.reshape(-1), [], 0
    for sh in shapes:
        n = int(np.prod(sh))
        out.append(flat[off:off + n].reshape(sh))
        off += n
    return out


def _reorder_in_proj(w):
    qkv, z, xbc = w[:, :2304], w[:, 2304:4352], w[:, 4352:8448]
    dt, gate = w[:, 8448:8480], w[:, 8480:10528]
    return jnp.concatenate([z, gate, xbc, qkv, dt, jnp.zeros((w.shape[0], PROJ_W - 10528), w.dtype)], axis=1)


def _restore_in_proj(wr):
    return jnp.concatenate([wr[:, OFF_QKV:OFF_QKV + 2304], wr[:, OFF_Z:OFF_Z + 2048], wr[:, OFF_XBC:OFF_XBC + 4096],
                            wr[:, OFF_DT:OFF_DT + 32], wr[:, OFF_GL:OFF_GL + 2048]], axis=1)


def _assemble_in_proj(g):
    pieces = []
    for lo, hi in IN_SEGMENTS:
        while lo < hi:
            d = lo // IN_SHARD_W
            end = min(hi, (d + 1) * IN_SHARD_W)
            pieces.append(g[d][:, lo - d * IN_SHARD_W:end - d * IN_SHARD_W])
            lo = end
    pieces.append(jnp.zeros((g.shape[1], PROJ_W - IN_PROJ_W), g.dtype))
    return jnp.concatenate(pieces, axis=1)


def _in_proj_slabs(wr):
    orig = _restore_in_proj(wr)
    return jnp.stack([orig[:, d * IN_SHARD_W:(d + 1) * IN_SHARD_W] for d in range(N_DEV)])


def kernel(x, norm_mix_pre_w, w_in, b_gate, conv_w, conv_b, dt_bias, a_log, d_skip, ssm_norm_w, w_att_proj, w_ssm_proj, w_out, norm_mix_post_w, norm_ffn_pre_w, w_up, w_down, norm_ffn_post_w, loss_target, m_norm_mix_pre_w, m_w_in, m_b_gate, m_conv_w, m_conv_b, m_dt_bias, m_a_log, m_d_skip, m_ssm_norm_w, m_w_att_proj, m_w_ssm_proj, m_w_out, m_norm_mix_post_w, m_norm_ffn_pre_w, m_w_up, m_w_down, m_norm_ffn_post_w, v_norm_mix_pre_w, v_w_in, v_b_gate, v_conv_w, v_conv_b, v_dt_bias, v_a_log, v_d_skip, v_ssm_norm_w, v_w_att_proj, v_w_ssm_proj, v_w_out, v_norm_mix_post_w, v_norm_ffn_pre_w, v_w_up, v_w_down, v_norm_ffn_post_w):
    w = dict(norm_mix_pre_w=norm_mix_pre_w, w_in=w_in, b_gate=b_gate, conv_w=conv_w, conv_b=conv_b, dt_bias=dt_bias, a_log=a_log,
             d_skip=d_skip, ssm_norm_w=ssm_norm_w, w_att_proj=w_att_proj, w_ssm_proj=w_ssm_proj, w_out=w_out,
             norm_mix_post_w=norm_mix_post_w, norm_ffn_pre_w=norm_ffn_pre_w, w_up=w_up, w_down=w_down, norm_ffn_post_w=norm_ffn_post_w)
    m = dict(norm_mix_pre_w=m_norm_mix_pre_w, w_in=m_w_in, b_gate=m_b_gate, conv_w=m_conv_w, conv_b=m_conv_b, dt_bias=m_dt_bias,
             a_log=m_a_log, d_skip=m_d_skip, ssm_norm_w=m_ssm_norm_w, w_att_proj=m_w_att_proj, w_ssm_proj=m_w_ssm_proj, w_out=m_w_out,
             norm_mix_post_w=m_norm_mix_post_w, norm_ffn_pre_w=m_norm_ffn_pre_w, w_up=m_w_up, w_down=m_w_down, norm_ffn_post_w=m_norm_ffn_post_w)
    v = dict(norm_mix_pre_w=v_norm_mix_pre_w, w_in=v_w_in, b_gate=v_b_gate, conv_w=v_conv_w, conv_b=v_conv_b, dt_bias=v_dt_bias,
             a_log=v_a_log, d_skip=v_d_skip, ssm_norm_w=v_ssm_norm_w, w_att_proj=v_w_att_proj, w_ssm_proj=v_w_ssm_proj, w_out=v_w_out,
             norm_mix_post_w=v_norm_mix_post_w, norm_ffn_pre_w=v_norm_ffn_pre_w, w_up=v_w_up, w_down=v_w_down, norm_ffn_post_w=v_norm_ffn_post_w)
    shard_shapes = {n: w[n].shape[1:] for n in ORDER}

    gathered = _all_gather([w[n][0] if n == "conv_w" else w[n][0].astype(BF16) for n in BIG])
    full = dict(zip(BIG, gathered))
    for n in ROW_SHARDED:
        full[n] = full[n].reshape(-1, full[n].shape[2])
    conv_full = jnp.moveaxis(full["conv_w"], 0, 1).reshape(SSM_CONV, CONV_DIM)

    loss, g_x, grads = _local_step(
        x[0], loss_target[0], w["norm_mix_pre_w"], _assemble_in_proj(full["w_in"]), w["b_gate"], conv_full, w["conv_b"],
        w["dt_bias"], w["a_log"], w["d_skip"], w["ssm_norm_w"], full["w_att_proj"], full["w_ssm_proj"], full["w_out"],
        w["norm_mix_post_w"], w["norm_ffn_pre_w"], full["w_up"], full["w_down"], w["norm_ffn_post_w"])

    slabs = dict(w_in=_in_proj_slabs(grads["w_in_r"]), w_att_proj=grads["w_att_proj"], w_up=grads["w_up"],
                 conv_w=jnp.moveaxis(grads["conv_w"].reshape(SSM_CONV, N_DEV, -1), 1, 0))
    for n in ROW_SHARDED:
        slabs[n] = grads[n].reshape(N_DEV, -1, grads[n].shape[1])
    small = _pack([grads[n].astype(F32) for n in SMALL], 8)
    recv, small_all = _exchange_grads([slabs[n] for n in BIG], small)

    small_shapes = [shard_shapes[n] for n in SMALL]
    small_out = _adamw(*[_pack([d_[n][0] for n in SMALL], 8) for d_ in (w, m, v)], small_all, "adamw_replicated", small_all.shape[1])
    big_out = {n: _adamw(w[n][0], m[n][0], v[n][0], r, "adamw_" + n, ADAMW_ROWS[n]) for n, r in zip(BIG, recv)}
    res = []
    for which, small_flat in enumerate(small_out):
        vals = {n: big_out[n][which] for n in BIG}
        vals.update(zip(SMALL, _unpack(small_flat, small_shapes)))
        res.append([vals[n][None] for n in ORDER])
    g_out, d_out, m_out, v_out = res
    total = lax.psum(loss[0, 0], ("x", "y", "c"))
    return (total, g_x[None], *g_out, *d_out, *m_out, *v_out)
```

```python
import functools
import math

import jax
import jax.numpy as jnp
import numpy as np
from jax import lax
from jax.experimental import pallas as pl
from jax.experimental.pallas import tpu as pltpu

F32 = jnp.float32
BF16 = jnp.bfloat16

D_MODEL = 1024
HEAD_DIM = 64
N_ATT_HEADS = 12
ATT_WIDTH = N_ATT_HEADS * HEAD_DIM
DILATED_PATTERNS = ((128, 1), (512, 4), (2048, 16))
ATT_BLOCK = 128
SSM_INNER = 2048
SSM_HEAD_DIM = 64
SSM_HEADS = 32
SSM_GROUPS = 8
SSM_STATE = 128
SSM_CHUNK = 128
CONV_DIM = 4096
SSM_CONV = 4
FFN_HIDDEN = 4096
RMS_EPS = 1e-6
N_DEV = 8

ADAM_LR = 0.001
ADAM_B1 = 0.9
ADAM_B2 = 0.999
ADAM_EPS = 1e-08
ADAM_WD = 0.01
ADAM_STEP = 10

LANE = 128
OFF_Z, OFF_GL, OFF_XBC, OFF_QKV, OFF_DT = 0, 2048, 4096, 8192, 10496
PROJ_W = 10752
PROJ_BLOCKS = PROJ_W // LANE
VMEM_LIMIT = 52 * 1024 * 1024
NEG = -1e30

HI = lax.Precision.HIGHEST
NT_DIMS = (((1,), (1,)), ((), ()))
TN_DIMS = (((0,), (0,)), ((), ()))
S = jax.ShapeDtypeStruct


def _params(sem):
    return pltpu.CompilerParams(dimension_semantics=sem, vmem_limit_bytes=VMEM_LIMIT)


def _matmul(a, b, *, mode, out_dtype, name, tm, tn, tk, epilogue=None, extra=None, stacked=False):
    if mode == "nn":
        m, k = a.shape
        n = b.shape[0] * b.shape[2] if stacked else b.shape[1]
        a_spec = pl.BlockSpec((tm, tk), lambda i, j, kk: (i, kk))
        b_spec = pl.BlockSpec((None, tk, tn), lambda i, j, kk: (j, kk, 0)) if stacked else pl.BlockSpec((tk, tn), lambda i, j, kk: (kk, j))
        dims = (((1,), (0,)), ((), ()))
    elif mode == "nt":
        m, k = a.shape
        n = b.shape[1] if stacked else b.shape[0]
        a_spec = pl.BlockSpec((tm, tk), lambda i, j, kk: (i, kk))
        b_spec = pl.BlockSpec((None, tn, tk), lambda i, j, kk: (kk, j, 0)) if stacked else pl.BlockSpec((tn, tk), lambda i, j, kk: (j, kk))
        dims = NT_DIMS
    else:
        (k, m), n = a.shape, b.shape[1]
        a_spec = pl.BlockSpec((tk, tm), lambda i, j, kk: (kk, i))
        b_spec = pl.BlockSpec((tk, tn), lambda i, j, kk: (kk, j))
        dims = TN_DIMS
    assert m % tm == 0 and n % tn == 0 and k % tk == 0, (name, m, n, k)
    if stacked:
        assert (tk if mode == "nt" else tn) * N_DEV == (k if mode == "nt" else n), name
    nk = k // tk
    o_spec = pl.BlockSpec((tm, tn), lambda i, j, kk: (i, j))
    in_specs, args = [a_spec, b_spec], [a, b]
    if epilogue == "relu2":
        out_shape = (S((m, n), BF16), S((m, n), BF16))
        out_specs = (o_spec, o_spec)
    elif stacked and mode == "tn":
        out_shape, out_specs = S((N_DEV, m, tn), out_dtype), pl.BlockSpec((None, tm, tn), lambda i, j, kk: (j, i, 0))
    else:
        out_shape, out_specs = S((m, n), out_dtype), o_spec
    if epilogue == "relu2_bwd":
        in_specs.append(o_spec)
        args.append(extra)

    def finish(acc, refs):
        if epilogue == "relu2":
            r = jnp.maximum(acc, 0.0)
            refs[0][...] = (r * r).astype(BF16)
            refs[1][...] = acc.astype(BF16)
        elif epilogue == "relu2_bwd":
            up = refs[0][...].astype(F32)
            refs[1][...] = (acc * (2.0 * jnp.maximum(up, 0.0))).astype(out_dtype)
        else:
            refs[0][...] = acc.astype(out_dtype)

    def body(a_ref, b_ref, *rest):
        part = lax.dot_general(a_ref[...].astype(BF16), b_ref[...].astype(BF16), dims, preferred_element_type=F32)
        if nk == 1:
            finish(part, rest)
            return
        acc_ref = rest[-1]
        kk = pl.program_id(2)

        @pl.when(kk == 0)
        def _():
            acc_ref[...] = part

        @pl.when(kk > 0)
        def _():
            acc_ref[...] += part

        @pl.when(kk == nk - 1)
        def _():
            finish(acc_ref[...], rest[:-1])

    scratch = [] if nk == 1 else [pltpu.VMEM((tm, tn), F32)]
    return pl.pallas_call(
        body, grid=(m // tm, n // tn, nk), in_specs=in_specs, out_specs=out_specs, out_shape=out_shape,
        scratch_shapes=scratch, name=name, compiler_params=_params(("parallel", "parallel", "arbitrary")),
    )(*args)


def _rowcall(body, name, n_rows, tr, ins, outs, scratch=()):
    res = pl.pallas_call(
        body, grid=(n_rows // tr,),
        in_specs=[pl.BlockSpec(bs, im) for _, bs, im in ins],
        out_specs=[pl.BlockSpec(bs, im) for _, _, bs, im in outs],
        out_shape=[S(sh, dt) for sh, dt, _, _ in outs],
        scratch_shapes=list(scratch), name=name, compiler_params=_params(("arbitrary",)),
    )(*[a for a, _, _ in ins])
    return res


def _rows(arr, tr, width=None, cb=0):
    width = arr.shape[1] if width is None else width
    return (arr, (tr, width), lambda i, cb=cb: (i, cb))


def _whole(arr):
    nd = arr.ndim
    return (arr, arr.shape, lambda i, nd=nd: (0,) * nd)


def _orow(n_rows, width, dtype, tr):
    return ((n_rows, width), dtype, (tr, width), lambda i: (i, 0))


def _oacc(width):
    return ((1, width), F32, (1, width), lambda i: (0, 0))


def _accumulate(ref, value):
    first = pl.program_id(0) == 0

    @pl.when(first)
    def _():
        ref[...] = value

    @pl.when(jnp.logical_not(first))
    def _():
        ref[...] += value


def _colsum(v):
    return jnp.sum(v, axis=0, keepdims=True)


def _rms_fwd(x, w):
    r = lax.rsqrt(jnp.mean(x * x, axis=-1, keepdims=True) + RMS_EPS)
    return x * r * w


def _rms_bwd(gy, x, w):
    r = lax.rsqrt(jnp.mean(x * x, axis=-1, keepdims=True) + RMS_EPS)
    xn = x * r
    gxn = gy * w
    gx = r * (gxn - xn * jnp.mean(gxn * xn, axis=-1, keepdims=True))
    return gx, _colsum(gy * xn)


def _sigmoid(x):
    return 1.0 / (1.0 + jnp.exp(-x))


def _head_expand(n_heads_pad, n_heads, width):
    h = lax.broadcasted_iota(jnp.int32, (n_heads_pad, n_heads * width), 0)
    c = lax.broadcasted_iota(jnp.int32, (n_heads_pad, n_heads * width), 1)
    return (c // width == h).astype(F32)


def _head_reduce(n_heads, width, n_heads_pad):
    c = lax.broadcasted_iota(jnp.int32, (n_heads * width, n_heads_pad), 0)
    h = lax.broadcasted_iota(jnp.int32, (n_heads * width, n_heads_pad), 1)
    return (c // width == h).astype(F32)


def _block_ones(n, width):
    r = lax.broadcasted_iota(jnp.int32, (n, n), 0)
    c = lax.broadcasted_iota(jnp.int32, (n, n), 1)
    return (r // width == c // width).astype(F32)


def _pre_norm(x, w_pre, tr=512):
    t = x.shape[0]

    def body(x_ref, w_ref, u_ref):
        u_ref[...] = _rms_fwd(x_ref[...], w_ref[...]).astype(BF16)

    return _rowcall(body, "pre_norm", t, tr, [_rows(x, tr), _whole(w_pre)], [_orow(t, D_MODEL, BF16, tr)])[0]


def _conv_fwd(proj, conv_w, conv_b, tr=256):
    t = proj.shape[0]
    cb = OFF_XBC // CONV_DIM
    halo = (proj, (8, CONV_DIM), lambda i: (jnp.maximum(i * (tr // 8) - 1, 0), cb))

    def body(cur_ref, prev_ref, w_ref, b_ref, o_ref, ext):
        ext[pl.ds(0, 8), :] = jnp.where(pl.program_id(0) > 0, prev_ref[...], 0.0)
        ext[pl.ds(8, tr), :] = cur_ref[...]
        acc = b_ref[...] + w_ref[3:4, :] * cur_ref[...]
        for k in range(SSM_CONV - 1):
            acc = acc + w_ref[k:k + 1, :] * ext[pl.ds(8 - 3 + k, tr), :]
        o_ref[...] = acc * _sigmoid(acc)

    return _rowcall(body, "conv_fwd", t, tr, [_rows(proj, tr, CONV_DIM, cb), halo, _whole(conv_w), _whole(conv_b)],
                    [_orow(t, CONV_DIM, F32, tr)], scratch=[pltpu.VMEM((tr + 8, CONV_DIM), F32)])[0]


def _dt_fwd(proj, dt_bias_pad, alog_pad, tr=512):
    t = proj.shape[0]

    def body(raw_ref, b_ref, al_ref, dtx_ref, csx_ref, cst_ref):
        v = raw_ref[...] + b_ref[...]
        dt = jnp.maximum(v, 0.0) + jnp.log1p(jnp.exp(-jnp.abs(v)))
        expand = _head_expand(LANE, SSM_HEADS, SSM_HEAD_DIM)
        dtx_ref[...] = jnp.dot(dt, expand, precision=HI, preferred_element_type=F32)
        la = dt * (-jnp.exp(al_ref[...]))
        row = lax.broadcasted_iota(jnp.int32, (SSM_CHUNK, SSM_CHUNK), 0)
        col = lax.broadcasted_iota(jnp.int32, (SSM_CHUNK, SSM_CHUNK), 1)
        tril = (col <= row).astype(F32)
        cs = jnp.concatenate([jnp.dot(tril, la[k * SSM_CHUNK:(k + 1) * SSM_CHUNK, :], precision=HI, preferred_element_type=F32)
                              for k in range(tr // SSM_CHUNK)], axis=0)
        csx_ref[...] = jnp.dot(cs, expand, precision=HI, preferred_element_type=F32)
        cst_ref[...] = cs.T

    return _rowcall(body, "dt_fwd", t, tr, [_rows(proj, tr, LANE, OFF_DT // LANE), _whole(dt_bias_pad), _whole(alog_pad)],
                    [_orow(t, SSM_INNER, F32, tr), _orow(t, SSM_INNER, F32, tr), ((LANE, t), F32, (LANE, tr), lambda i: (0, i))])


def _gate_norm_fwd(y_ssd, xa, proj, dskip_x, norm_w, tr=256):
    t = y_ssd.shape[0]
    gw = SSM_INNER // SSM_GROUPS

    def body(y_ref, xs_ref, z_ref, d_ref, w_ref, o_ref):
        z = z_ref[...]
        y3 = (y_ref[...] + d_ref[...] * xs_ref[...]) * (z * _sigmoid(z))
        for g in range(SSM_GROUPS):
            sl = slice(g * gw, (g + 1) * gw)
            o_ref[:, sl] = _rms_fwd(y3[:, sl], w_ref[:, sl]).astype(BF16)

    return _rowcall(body, "gate_norm_fwd", t, tr,
                    [_rows(y_ssd, tr), _rows(xa, tr, SSM_INNER, 0), _rows(proj, tr, SSM_INNER, OFF_Z // SSM_INNER), _whole(dskip_x), _whole(norm_w)],
                    [_orow(t, SSM_INNER, BF16, tr)])[0]


def _gating_fwd(proj, b_gate, att_p, ssm_p, tr=512):
    t = proj.shape[0]

    def body(gl_ref, b_ref, a_ref, s_ref, o_ref):
        gates = _sigmoid(gl_ref[...] + b_ref[...])
        o_ref[...] = (gates[:, :D_MODEL] * a_ref[...] + gates[:, D_MODEL:] * s_ref[...]).astype(BF16)

    return _rowcall(body, "gating_fwd", t, tr, [_rows(proj, tr, 2 * D_MODEL, OFF_GL // (2 * D_MODEL)), _whole(b_gate), _rows(att_p, tr), _rows(ssm_p, tr)],
                    [_orow(t, D_MODEL, BF16, tr)])[0]


def _mix_post_ffn_pre(x, mixed, w_post, w_fpre, tr=512):
    t = x.shape[0]

    def body(x_ref, m_ref, wp_ref, wf_ref, h1_ref, f_ref):
        h1 = x_ref[...] + _rms_fwd(m_ref[...], wp_ref[...])
        h1_ref[...] = h1
        f_ref[...] = _rms_fwd(h1, wf_ref[...]).astype(BF16)

    return _rowcall(body, "mix_post_ffn_pre", t, tr, [_rows(x, tr), _rows(mixed, tr), _whole(w_post), _whole(w_fpre)],
                    [_orow(t, D_MODEL, F32, tr), _orow(t, D_MODEL, BF16, tr)])


def _loss_and_ffn_post_bwd(h1, dn, w_fpost, target, tr=512):
    t = h1.shape[0]

    def body(h1_ref, dn_ref, w_ref, tg_ref, loss_ref, gh2_ref, gdn_ref, gw_ref):
        dn = dn_ref[...]
        w = w_ref[...]
        err = h1_ref[...] + _rms_fwd(dn, w) - tg_ref[...]
        _accumulate(loss_ref, jnp.zeros((1, LANE), F32) + 0.5 * jnp.sum(jnp.mean(err * err, axis=-1, keepdims=True)))
        gh2 = err * (1.0 / D_MODEL)
        gh2_ref[...] = gh2
        gdn, gw = _rms_bwd(gh2, dn, w)
        gdn_ref[...] = gdn.astype(BF16)
        _accumulate(gw_ref, gw)

    return _rowcall(body, "loss_ffn_post_bwd", t, tr, [_rows(h1, tr), _rows(dn, tr), _whole(w_fpost), _rows(target, tr)],
                    [_oacc(LANE), _orow(t, D_MODEL, F32, tr), _orow(t, D_MODEL, BF16, tr), _oacc(D_MODEL)])


def _ffn_pre_mix_post_bwd(g_h2, g_f, h1, w_fpre, mixed, w_post, tr=512):
    t = h1.shape[0]

    def body(gh2_ref, gf_ref, h1_ref, wf_ref, m_ref, wp_ref, gh1_ref, gm_ref, gwf_ref, gwp_ref):
        gx, gwf = _rms_bwd(gf_ref[...], h1_ref[...], wf_ref[...])
        gh1 = gh2_ref[...] + gx
        gh1_ref[...] = gh1
        gm, gwp = _rms_bwd(gh1, m_ref[...], wp_ref[...])
        gm_ref[...] = gm.astype(BF16)
        _accumulate(gwf_ref, gwf)
        _accumulate(gwp_ref, gwp)

    return _rowcall(body, "ffn_pre_mix_post_bwd", t, tr,
                    [_rows(g_h2, tr), _rows(g_f, tr), _rows(h1, tr), _whole(w_fpre), _rows(mixed, tr), _whole(w_post)],
                    [_orow(t, D_MODEL, F32, tr), _orow(t, D_MODEL, BF16, tr), _oacc(D_MODEL), _oacc(D_MODEL)])


def _gating_bwd(g_mixin, proj, b_gate, att_p, ssm_p, tr=512):
    t = proj.shape[0]

    def body(gm_ref, gl_ref, b_ref, a_ref, s_ref, ga_ref, gs_ref, ggl_ref, gb_ref):
        gates = _sigmoid(gl_ref[...] + b_ref[...])
        gm = gm_ref[...]
        g_att, g_ssm = gates[:, :D_MODEL], gates[:, D_MODEL:]
        ga_ref[...] = (gm * g_att).astype(BF16)
        gs_ref[...] = (gm * g_ssm).astype(BF16)
        ggl_a = gm * a_ref[...] * g_att * (1.0 - g_att)
        ggl_s = gm * s_ref[...] * g_ssm * (1.0 - g_ssm)
        ggl_ref[:, :D_MODEL] = ggl_a.astype(BF16)
        ggl_ref[:, D_MODEL:] = ggl_s.astype(BF16)
        _accumulate(gb_ref.at[:, :D_MODEL], _colsum(ggl_a))
        _accumulate(gb_ref.at[:, D_MODEL:], _colsum(ggl_s))

    return _rowcall(body, "gating_bwd", t, tr,
                    [_rows(g_mixin, tr), _rows(proj, tr, 2 * D_MODEL, OFF_GL // (2 * D_MODEL)), _whole(b_gate), _rows(att_p, tr), _rows(ssm_p, tr)],
                    [_orow(t, D_MODEL, BF16, tr), _orow(t, D_MODEL, BF16, tr), _orow(t, 2 * D_MODEL, BF16, tr), _oacc(2 * D_MODEL)])


def _gate_norm_bwd(g_y4, y_ssd, xa, proj, dskip_x, norm_w, tr=256):
    t = y_ssd.shape[0]
    gw = SSM_INNER // SSM_GROUPS

    def body(g_ref, y_ref, xs_ref, z_ref, d_ref, w_ref, gy2_ref, gz_ref, gnw_ref, gdx_ref, gd_ref):
        z = z_ref[...]
        xs = xs_ref[...]
        sg = _sigmoid(z)
        sz = z * sg
        y2 = y_ref[...] + d_ref[...] * xs
        y3 = y2 * sz
        g4 = g_ref[...]
        for g in range(SSM_GROUPS):
            sl = slice(g * gw, (g + 1) * gw)
            gy3, gnw = _rms_bwd(g4[:, sl], y3[:, sl], w_ref[:, sl])
            _accumulate(gnw_ref.at[:, sl], gnw)
            gy2 = gy3 * sz[:, sl]
            gy2_ref[:, sl] = gy2
            gz_ref[:, sl] = (gy3 * y2[:, sl] * (sg[:, sl] * (1.0 + z[:, sl] * (1.0 - sg[:, sl])))).astype(BF16)
            _accumulate(gdx_ref.at[:, sl], _colsum(gy2 * xs[:, sl]))
        tot = jnp.broadcast_to(gdx_ref[...], (8, SSM_INNER))
        gd_ref[...] = jnp.dot(tot, _head_reduce(SSM_HEADS, SSM_HEAD_DIM, LANE), precision=HI, preferred_element_type=F32)[0:1, :]

    return _rowcall(body, "gate_norm_bwd", t, tr,
                    [_rows(g_y4, tr), _rows(y_ssd, tr), _rows(xa, tr, SSM_INNER, 0), _rows(proj, tr, SSM_INNER, OFF_Z // SSM_INNER), _whole(dskip_x), _whole(norm_w)],
                    [_orow(t, SSM_INNER, F32, tr), _orow(t, SSM_INNER, BF16, tr), _oacc(SSM_INNER), _oacc(SSM_INNER), _oacc(LANE)])


def _dt_bwd(g_dtx, ga_rows, proj, dt_bias_pad, tr=512):
    t = proj.shape[0]

    def body(g_ref, ga_ref, raw_ref, b_ref, o_ref, gb_ref, gal_ref):
        red = _head_reduce(SSM_HEADS, SSM_HEAD_DIM, LANE)
        gdt = jnp.dot(g_ref[...], red, precision=HI, preferred_element_type=F32)
        graw = gdt * _sigmoid(raw_ref[...] + b_ref[...])
        o_ref[...] = graw.astype(BF16)
        _accumulate(gb_ref, _colsum(graw))
        tot = jnp.broadcast_to(_colsum(ga_ref[...]), (8, SSM_INNER))
        gal_ref[...] = jnp.dot(tot, red, precision=HI, preferred_element_type=F32)[0:1, :]

    return _rowcall(body, "dt_bwd", t, tr, [_rows(g_dtx, tr), _whole(ga_rows), _rows(proj, tr, LANE, OFF_DT // LANE), _whole(dt_bias_pad)],
                    [_orow(t, LANE, BF16, tr), _oacc(LANE), _oacc(LANE)])


def _conv_bwd_act(g_xs, g_b, g_c, proj, conv_w, conv_b, tr=256):
    t = proj.shape[0]
    cb = OFF_XBC // CONV_DIM
    halo = (proj, (8, CONV_DIM), lambda i: (jnp.maximum(i * (tr // 8) - 1, 0), cb))
    nb, nc = SSM_INNER, SSM_INNER + SSM_GROUPS * SSM_STATE

    def body(gxs_ref, gb_ref, gc_ref, cur_ref, prev_ref, w_ref, b_ref, o_ref, gcb_ref, gw0, gw1, gw2, gw3, ext):
        ext[pl.ds(0, 8), :] = jnp.where(pl.program_id(0) > 0, prev_ref[...], 0.0)
        ext[pl.ds(8, tr), :] = cur_ref[...]
        acc = b_ref[...] + w_ref[3:4, :] * cur_ref[...]
        for k in range(SSM_CONV - 1):
            acc = acc + w_ref[k:k + 1, :] * ext[pl.ds(8 - 3 + k, tr), :]
        sg = _sigmoid(acc)
        dsilu = sg * (1.0 + acc * (1.0 - sg))
        o_ref[:, :nb] = gxs_ref[...] * dsilu[:, :nb]
        o_ref[:, nb:nc] = gb_ref[...] * dsilu[:, nb:nc]
        o_ref[:, nc:] = gc_ref[...] * dsilu[:, nc:]
        gxc = o_ref[...]
        _accumulate(gcb_ref, _colsum(gxc))
        for k, gw in enumerate((gw0, gw1, gw2, gw3)):
            _accumulate(gw, _colsum(gxc * ext[pl.ds(8 - 3 + k, tr), :]))

    return _rowcall(body, "conv_bwd_act", t, tr,
                    [_rows(g_xs, tr), _rows(g_b, tr), _rows(g_c, tr), _rows(proj, tr, CONV_DIM, cb), halo, _whole(conv_w), _whole(conv_b)],
                    [_orow(t, CONV_DIM, F32, tr)] + [_oacc(CONV_DIM)] * 5, scratch=[pltpu.VMEM((tr + 8, CONV_DIM), F32)])


def _conv_bwd_in(g_xc, conv_w, tr=256):
    t = g_xc.shape[0]
    n_blk = t // tr
    halo = (g_xc, (8, CONV_DIM), lambda i: (jnp.minimum((i + 1) * (tr // 8), t // 8 - 1), 0))

    def body(cur_ref, nxt_ref, w_ref, o_ref, ext):
        ext[pl.ds(0, tr), :] = cur_ref[...]
        ext[pl.ds(tr, 8), :] = jnp.where(pl.program_id(0) < n_blk - 1, nxt_ref[...], 0.0)
        acc = w_ref[3:4, :] * cur_ref[...]
        for k in range(SSM_CONV - 1):
            acc = acc + w_ref[k:k + 1, :] * ext[pl.ds(3 - k, tr), :]
        o_ref[...] = acc.astype(BF16)

    return _rowcall(body, "conv_bwd_in", t, tr, [_rows(g_xc, tr), halo, _whole(conv_w)], [_orow(t, CONV_DIM, BF16, tr)],
                    scratch=[pltpu.VMEM((tr + 8, CONV_DIM), F32)])[0]


def _pre_norm_bwd(g_h1, g_u, x, w_pre, tr=512):
    t = x.shape[0]

    def body(gh_ref, gu_ref, x_ref, w_ref, gx_ref, gw_ref):
        gx, gw = _rms_bwd(gu_ref[...], x_ref[...], w_ref[...])
        gx_ref[...] = gh_ref[...] + gx
        _accumulate(gw_ref, gw)

    return _rowcall(body, "pre_norm_bwd", t, tr, [_rows(g_h1, tr), _rows(g_u, tr), _rows(x, tr), _whole(w_pre)],
                    [_orow(t, D_MODEL, F32, tr), _oacc(D_MODEL)])


def _alibi_slopes(n):
    def pow2(m):
        start = 2.0 ** (-8.0 / m)
        return [start ** (i + 1) for i in range(m)]
    if (n & (n - 1)) == 0:
        s = pow2(n)
    else:
        c = 2 ** int(math.floor(math.log2(n)))
        s = pow2(c) + pow2(2 * c)[0::2][: n - c]
    return np.array(s, dtype=np.float32)


def _slope_rows():
    s = _alibi_slopes(N_ATT_HEADS).reshape(N_ATT_HEADS // 2, 2)
    return jnp.asarray(np.broadcast_to(np.repeat(s, HEAD_DIM, axis=1)[:, None, :], (N_ATT_HEADS // 2, 8, LANE)).copy())


ATT_MAX_BLOCK_ROWS = 2048


RESIDUE_MAJOR_FROM = 16


class _AttLayout:
    def __init__(self, t, dil):
        self.t, self.dil = t, dil
        self.rows = t // dil
        self.residue_major = dil >= RESIDUE_MAJOR_FROM
        if self.residue_major:
            bq, self.stride = min(512, self.rows), 1
        else:
            bq, self.stride = min(512, self.rows, ATT_MAX_BLOCK_ROWS // dil), dil
        self.nsub = bq // ATT_BLOCK
        self.nblk = self.rows // bq
        self.rb = bq * self.stride
        self.pb = ATT_BLOCK * self.stride
        self.n_pb = self.rows * self.stride // self.pb

    def qkv(self, proj):
        if self.residue_major:
            qkv = proj[:, OFF_QKV:OFF_QKV + 3 * ATT_WIDTH]
            return qkv.reshape(self.rows, self.dil * 3 * ATT_WIDTH), 3 * ATT_WIDTH // LANE, 0
        return proj, 0, OFF_QKV // LANE

    def act(self, a):
        return a.reshape(self.rows, self.dil * ATT_WIDTH) if self.residue_major else a

    def act_shape(self):
        return (self.rows, self.dil * ATT_WIDTH) if self.residue_major else (self.t, ATT_WIDTH)

    def col(self, r, band, c):
        return r * band + c if self.residue_major else c


def _residue_rows(r, stride, first_block, n_blocks=1):
    if stride == 1:
        return pl.ds(first_block * ATT_BLOCK, n_blocks * ATT_BLOCK)
    return pl.ds(r + first_block * ATT_BLOCK * stride, n_blocks * ATT_BLOCK, stride=stride)


def _lane_half():
    return lax.broadcasted_iota(jnp.int32, (ATT_BLOCK, LANE), 1) // HEAD_DIM


def _att_scores_mask(dil, first):
    iq = lax.broadcasted_iota(jnp.int32, (ATT_BLOCK, 2 * ATT_BLOCK), 0)
    jk = lax.broadcasted_iota(jnp.int32, (ATT_BLOCK, 2 * ATT_BLOCK), 1)
    dist = ATT_BLOCK + iq - jk
    valid = (dist >= 0) & (dist <= ATT_BLOCK) & (jnp.logical_not(first) | (jk >= ATT_BLOCK))
    return (dist * dil).astype(F32), valid


def _att_fwd(proj, dil, slopes):
    t = proj.shape[0]
    lay = _AttLayout(t, dil)
    nsub, nblk, rb, pb = lay.nsub, lay.nblk, lay.rb, lay.pb
    src, band, qb = lay.qkv(proj)
    aw = ATT_WIDTH // LANE

    def spec(off, prev=False):
        if prev:
            return pl.BlockSpec((pb, LANE), lambda hp, i, r: (jnp.maximum(i * nsub - 1, 0), lay.col(r, band, qb + off + hp)))
        return pl.BlockSpec((rb, LANE), lambda hp, i, r: (i, lay.col(r, band, qb + off + hp)))

    o_spec = pl.BlockSpec((rb, LANE), lambda hp, i, r: (i, lay.col(r, aw, hp)))

    def body(q_ref, kc_ref, kp_ref, vc_ref, vp_ref, sl_ref, o_ref, lse_ref):
        i, r = pl.program_id(1), pl.program_id(2)
        half = _lane_half()
        for sub in range(nsub):
            rs = _residue_rows(r, lay.stride,sub)
            q = (q_ref[rs, :] * (HEAD_DIM ** -0.5)).astype(BF16)
            if sub == 0:
                r0 = _residue_rows(r, lay.stride,0)
                kk = jnp.concatenate([kp_ref[r0, :], kc_ref[rs, :]], axis=0).astype(BF16)
                vv = jnp.concatenate([vp_ref[r0, :], vc_ref[rs, :]], axis=0).astype(BF16)
                first = i == 0
            else:
                ks = _residue_rows(r, lay.stride,sub - 1, 2)
                kk, vv = kc_ref[ks, :].astype(BF16), vc_ref[ks, :].astype(BF16)
                first = jnp.bool_(False)
            dist, valid = _att_scores_mask(dil, first)
            outs, lses = [], []
            for e in range(2):
                qe = jnp.where(half == e, q, jnp.zeros_like(q))
                s = lax.dot_general(qe, kk, NT_DIMS, preferred_element_type=F32)
                s = jnp.where(valid, s - sl_ref[0:1, e * HEAD_DIM:e * HEAD_DIM + 1] * dist, NEG)
                m = jnp.max(s, axis=-1, keepdims=True)
                p = jnp.exp(s - m)
                l = jnp.sum(p, axis=-1, keepdims=True)
                outs.append(jnp.dot(p.astype(BF16), vv, preferred_element_type=F32) / l)
                lses.append(m + jnp.log(l))
            o_ref[rs, :] = jnp.where(half == 0, outs[0], outs[1])
            lse_ref[rs, :] = jnp.where(half == 0, lses[0], lses[1])

    o, lse = pl.pallas_call(
        body, grid=(N_ATT_HEADS // 2, nblk, dil),
        in_specs=[spec(0), spec(6), spec(6, True), spec(12), spec(12, True), pl.BlockSpec((None, 8, LANE), lambda hp, i, r: (hp, 0, 0))],
        out_specs=[o_spec, o_spec], out_shape=[S(lay.act_shape(), F32)] * 2,
        name=f"att_fwd_d{dil}", compiler_params=_params(("parallel", "parallel", "arbitrary")),
    )(src, src, src, src, src, slopes)
    return o.reshape(t, ATT_WIDTH), lse.reshape(t, ATT_WIDTH)


def _att_combine(outs, lses, tr=512):
    t = outs[0].shape[0]

    def body(o0, o1, o2, l0, l1, l2, att_ref, lse_ref):
        ls = [l0[...], l1[...], l2[...]]
        m = jnp.maximum(jnp.maximum(ls[0], ls[1]), ls[2])
        ws = [jnp.exp(l - m) for l in ls]
        tot = ws[0] + ws[1] + ws[2]
        num = ws[0] * o0[...].astype(F32) + ws[1] * o1[...].astype(F32) + ws[2] * o2[...].astype(F32)
        att_ref[...] = (num / tot).astype(BF16)
        lse_ref[...] = m + jnp.log(tot)

    return _rowcall(body, "att_combine", t, tr, [_rows(a, tr) for a in list(outs) + list(lses)],
                    [_orow(t, ATT_WIDTH, BF16, tr), _orow(t, ATT_WIDTH, F32, tr)])


def _att_delta(g_att, att, tr=512):
    t = att.shape[0]

    def body(g_ref, a_ref, o_ref):
        prod = g_ref[...] * a_ref[...].astype(F32)
        o_ref[...] = jnp.dot(prod, _block_ones(ATT_WIDTH, HEAD_DIM), precision=HI, preferred_element_type=F32)

    return _rowcall(body, "att_delta", t, tr, [_rows(g_att, tr), _rows(att, tr)], [_orow(t, ATT_WIDTH, F32, tr)])[0]


def _att_bwd_q(proj, g_att, lse, delta, dil, slopes):
    t = proj.shape[0]
    lay = _AttLayout(t, dil)
    nsub, nblk, rb, pb = lay.nsub, lay.nblk, lay.rb, lay.pb
    src, band, qb = lay.qkv(proj)
    aw = ATT_WIDTH // LANE

    def spec(off, prev=False):
        if prev:
            return pl.BlockSpec((pb, LANE), lambda hp, i, r: (jnp.maximum(i * nsub - 1, 0), lay.col(r, band, qb + off + hp)))
        return pl.BlockSpec((rb, LANE), lambda hp, i, r: (i, lay.col(r, band, qb + off + hp)))

    o_spec = pl.BlockSpec((rb, LANE), lambda hp, i, r: (i, lay.col(r, aw, hp)))

    def body(q_ref, kc_ref, kp_ref, vc_ref, vp_ref, do_ref, lse_ref, dl_ref, sl_ref, dq_ref):
        i, r = pl.program_id(1), pl.program_id(2)
        half = _lane_half()
        for sub in range(nsub):
            rs = _residue_rows(r, lay.stride,sub)
            q = (q_ref[rs, :] * (HEAD_DIM ** -0.5)).astype(BF16)
            do = do_ref[rs, :].astype(BF16)
            lse_q, dl_q = lse_ref[rs, :], dl_ref[rs, :]
            if sub == 0:
                r0 = _residue_rows(r, lay.stride,0)
                kk = jnp.concatenate([kp_ref[r0, :], kc_ref[rs, :]], axis=0).astype(BF16)
                vv = jnp.concatenate([vp_ref[r0, :], vc_ref[rs, :]], axis=0).astype(BF16)
                first = i == 0
            else:
                ks = _residue_rows(r, lay.stride,sub - 1, 2)
                kk, vv = kc_ref[ks, :].astype(BF16), vc_ref[ks, :].astype(BF16)
                first = jnp.bool_(False)
            dist, valid = _att_scores_mask(dil, first)
            dqs = []
            for e in range(2):
                c = e * HEAD_DIM
                qe = jnp.where(half == e, q, jnp.zeros_like(q))
                doe = jnp.where(half == e, do, jnp.zeros_like(do))
                s = lax.dot_general(qe, kk, NT_DIMS, preferred_element_type=F32)
                s = jnp.where(valid, s - sl_ref[0:1, c:c + 1] * dist, NEG)
                p = jnp.exp(s - lse_q[:, c:c + 1])
                dp = lax.dot_general(doe, vv, NT_DIMS, preferred_element_type=F32)
                ds = p * (dp - dl_q[:, c:c + 1])
                dqs.append(jnp.dot(ds.astype(BF16), kk, preferred_element_type=F32))
            dq_ref[rs, :] = jnp.where(half == 0, dqs[0], dqs[1]) * (HEAD_DIM ** -0.5)

    return pl.pallas_call(
        body, grid=(N_ATT_HEADS // 2, nblk, dil),
        in_specs=[spec(0), spec(6), spec(6, True), spec(12), spec(12, True), o_spec, o_spec, o_spec,
                  pl.BlockSpec((None, 8, LANE), lambda hp, i, r: (hp, 0, 0))],
        out_specs=o_spec, out_shape=S(lay.act_shape(), F32),
        name=f"att_bwd_q_d{dil}", compiler_params=_params(("parallel", "parallel", "arbitrary")),
    )(src, src, src, src, src, lay.act(g_att), lay.act(lse), lay.act(delta), slopes).reshape(t, ATT_WIDTH)


def _att_bwd_kv(proj, g_att, lse, delta, dil, slopes):
    t = proj.shape[0]
    lay = _AttLayout(t, dil)
    nsub, nblk, rb, pb, n_pb = lay.nsub, lay.nblk, lay.rb, lay.pb, lay.n_pb
    src, band, qb = lay.qkv(proj)
    aw = ATT_WIDTH // LANE

    def pspec(off, nxt=False):
        if nxt:
            return pl.BlockSpec((pb, LANE), lambda hp, i, r: (jnp.minimum((i + 1) * nsub, n_pb - 1), lay.col(r, band, qb + off + hp)))
        return pl.BlockSpec((rb, LANE), lambda hp, i, r: (i, lay.col(r, band, qb + off + hp)))

    def aspec(nxt=False):
        if nxt:
            return pl.BlockSpec((pb, LANE), lambda hp, i, r: (jnp.minimum((i + 1) * nsub, n_pb - 1), lay.col(r, aw, hp)))
        return pl.BlockSpec((rb, LANE), lambda hp, i, r: (i, lay.col(r, aw, hp)))

    def body(qc_ref, qn_ref, k_ref, v_ref, doc_ref, don_ref, lsec_ref, lsen_ref, dlc_ref, dln_ref, sl_ref, dk_ref, dv_ref):
        i, r = pl.program_id(1), pl.program_id(2)
        half = _lane_half()
        jk = lax.broadcasted_iota(jnp.int32, (ATT_BLOCK, 2 * ATT_BLOCK), 0)
        cq = lax.broadcasted_iota(jnp.int32, (ATT_BLOCK, 2 * ATT_BLOCK), 1)
        dist_i = cq - jk
        dist = (dist_i * dil).astype(F32)
        for sub in range(nsub):
            rs = _residue_rows(r, lay.stride,sub)

            def both(cur_ref, nxt_ref):
                if sub < nsub - 1:
                    return cur_ref[_residue_rows(r, lay.stride,sub, 2), :]
                return jnp.concatenate([cur_ref[rs, :], nxt_ref[_residue_rows(r, lay.stride,0), :]], axis=0)

            has_next = jnp.bool_(True) if sub < nsub - 1 else i < nblk - 1
            valid = (dist_i >= 0) & (dist_i <= ATT_BLOCK) & (has_next | (cq < ATT_BLOCK))
            qq = (both(qc_ref, qn_ref) * (HEAD_DIM ** -0.5)).astype(BF16)
            doo = both(doc_ref, don_ref).astype(BF16)
            lse_t = both(lsec_ref, lsen_ref).T
            dl_t = both(dlc_ref, dln_ref).T
            k = k_ref[rs, :].astype(BF16)
            v = v_ref[rs, :].astype(BF16)
            dks, dvs = [], []
            for e in range(2):
                c = e * HEAD_DIM
                ke = jnp.where(half == e, k, jnp.zeros_like(k))
                ve = jnp.where(half == e, v, jnp.zeros_like(v))
                st = lax.dot_general(ke, qq, NT_DIMS, preferred_element_type=F32)
                st = jnp.where(valid, st - sl_ref[0:1, c:c + 1] * dist, NEG)
                pt = jnp.exp(st - lse_t[c:c + 1, :])
                dpt = lax.dot_general(ve, doo, NT_DIMS, preferred_element_type=F32)
                dst = pt * (dpt - dl_t[c:c + 1, :])
                dks.append(jnp.dot(dst.astype(BF16), qq, preferred_element_type=F32))
                dvs.append(jnp.dot(pt.astype(BF16), doo, preferred_element_type=F32))
            dk_ref[rs, :] = jnp.where(half == 0, dks[0], dks[1])
            dv_ref[rs, :] = jnp.where(half == 0, dvs[0], dvs[1])

    gv, lv, dlv = lay.act(g_att), lay.act(lse), lay.act(delta)
    dk, dv = pl.pallas_call(
        body, grid=(N_ATT_HEADS // 2, nblk, dil),
        in_specs=[pspec(0), pspec(0, True), pspec(6), pspec(12), aspec(), aspec(True), aspec(), aspec(True), aspec(), aspec(True),
                  pl.BlockSpec((None, 8, LANE), lambda hp, i, r: (hp, 0, 0))],
        out_specs=[aspec(), aspec()], out_shape=[S(lay.act_shape(), F32)] * 2,
        name=f"att_bwd_kv_d{dil}", compiler_params=_params(("parallel", "parallel", "arbitrary")),
    )(src, src, src, src, gv, gv, lv, lv, dlv, dlv, slopes)
    return dk.reshape(t, ATT_WIDTH), dv.reshape(t, ATT_WIDTH)


def _att_grad_sum(dqs, dks, dvs, tr=512):
    t = dqs[0].shape[0]

    def body(*refs):
        o_ref = refs[-1]
        for n in range(3):
            tot = refs[3 * n][...] + refs[3 * n + 1][...] + refs[3 * n + 2][...]
            o_ref[:, n * ATT_WIDTH:(n + 1) * ATT_WIDTH] = tot.astype(BF16)

    return _rowcall(body, "att_grad_sum", t, tr, [_rows(a, tr) for a in list(dqs) + list(dks) + list(dvs)],
                    [_orow(t, 3 * ATT_WIDTH, BF16, tr)])[0]


def _ssd_common(xs, dtx, cs, cs_t):
    ch = SSM_CHUNK
    row = lax.broadcasted_iota(jnp.int32, (ch, ch), 0)
    col = lax.broadcasted_iota(jnp.int32, (ch, ch), 1)
    cs_last = cs[ch - 1:ch, :]
    return dict(tril=col <= row, row=row, col=col, cs=cs, cs_t=cs_t, cs_last=cs_last,
                e=jnp.exp(cs), w=jnp.exp(cs_last - cs), xd=xs * dtx)


def _dot_split(a, b, split):
    ops = [a, b]
    x = ops[split]
    hi = x.astype(BF16)
    lo = (x - hi.astype(F32)).astype(BF16)
    other = ops[1 - split].astype(BF16)
    if split == 1:
        return jnp.dot(other, hi, preferred_element_type=F32) + jnp.dot(other, lo, preferred_element_type=F32)
    return jnp.dot(hi, other, preferred_element_type=F32) + jnp.dot(lo, other, preferred_element_type=F32)


def _decay_col(cs_t, heads_per_group):
    r = lax.broadcasted_iota(jnp.int32, (heads_per_group * SSM_HEAD_DIM, SSM_STATE), 0) // SSM_HEAD_DIM
    out = jnp.zeros((heads_per_group * SSM_HEAD_DIM, SSM_STATE), F32)
    for j in range(heads_per_group):
        out = jnp.where(r == j, jnp.exp(cs_t[j:j + 1, SSM_CHUNK - 1:SSM_CHUNK]), out)
    return out


def _ssd_specs(t):
    hg = SSM_HEADS // SSM_GROUPS
    gw = hg * SSM_HEAD_DIM
    nb0 = SSM_INNER // SSM_STATE
    return hg, gw, nb0


def _ssd_fwd(xa, dtx, csx, cst_g):
    t = xa.shape[0]
    nch = t // SSM_CHUNK
    hg, gw, nb0 = _ssd_specs(t)
    ch = SSM_CHUNK

    def body(xs_ref, b_ref, c_ref, dtx_ref, cs_ref, cst_ref, y_ref, st_ref, h_scr):
        cc, g = pl.program_id(0), pl.program_id(1)

        @pl.when(cc == 0)
        def _():
            h_scr[g] = jnp.zeros((gw, SSM_STATE), F32)

        q = _ssd_common(xs_ref[...], dtx_ref[...], cs_ref[...], cst_ref[...])
        bb, cb = b_ref[...].astype(BF16), c_ref[...].astype(BF16)
        cbm = lax.dot_general(cb, bb, NT_DIMS, preferred_element_type=F32)
        h = h_scr[g]
        st_ref[...] = h
        xd16 = q["xd"].astype(BF16)
        y = lax.dot_general(cb, h.astype(BF16), NT_DIMS, preferred_element_type=F32) * q["e"]
        lane_head = lax.broadcasted_iota(jnp.int32, (ch, gw), 1) // SSM_HEAD_DIM
        for j in range(hg):
            diff = q["cs"][:, j * SSM_HEAD_DIM:j * SSM_HEAD_DIM + 1] - q["cs_t"][j:j + 1, :]
            gmat = cbm * jnp.exp(jnp.where(q["tril"], diff, NEG))
            yj = jnp.dot(gmat.astype(BF16), xd16, preferred_element_type=F32)
            y = y + jnp.where(lane_head == j, yj, 0.0)
        y_ref[...] = y
        s_new = lax.dot_general((q["xd"] * q["w"]).astype(BF16), bb, TN_DIMS, preferred_element_type=F32)
        h_scr[g] = _decay_col(q["cs_t"], hg) * h + s_new

    return pl.pallas_call(
        body, grid=(nch, SSM_GROUPS),
        in_specs=[pl.BlockSpec((ch, gw), lambda cc, g: (cc, g)),
                  pl.BlockSpec((ch, SSM_STATE), lambda cc, g: (cc, nb0 + g)),
                  pl.BlockSpec((ch, SSM_STATE), lambda cc, g: (cc, nb0 + SSM_GROUPS + g)),
                  pl.BlockSpec((ch, gw), lambda cc, g: (cc, g)),
                  pl.BlockSpec((ch, gw), lambda cc, g: (cc, g)),
                  pl.BlockSpec((None, 8, ch), lambda cc, g: (g, 0, cc))],
        out_specs=[pl.BlockSpec((ch, gw), lambda cc, g: (cc, g)),
                   pl.BlockSpec((None, None, gw, SSM_STATE), lambda cc, g: (cc, g, 0, 0))],
        out_shape=[S((t, SSM_INNER), F32), S((nch, SSM_GROUPS, gw, SSM_STATE), F32)],
        scratch_shapes=[pltpu.VMEM((SSM_GROUPS, gw, SSM_STATE), F32)],
        name="ssd_fwd", compiler_params=_params(("arbitrary", "arbitrary")),
    )(xa, xa, xa, dtx, csx, cst_g)


def _ssd_bwd(xa, dtx, csx, cst_g, alog_x, g_y, states, dskip_x):
    t = xa.shape[0]
    nch = t // SSM_CHUNK
    hg, gw, nb0 = _ssd_specs(t)
    ch = SSM_CHUNK

    def rc(cc):
        return nch - 1 - cc

    def body(xs_ref, b_ref, c_ref, dtx_ref, cs_ref, cst_ref, alx_ref, gy_ref, st_ref, dsk_ref,
             gxs_ref, gb_ref, gc_ref, gdt_ref, ga_ref, gh_scr):
        cc, g = pl.program_id(0), pl.program_id(1)

        @pl.when(cc == 0)
        def _():
            gh_scr[g] = jnp.zeros((gw, SSM_STATE), F32)

        xs, dtx = xs_ref[...], dtx_ref[...]
        q = _ssd_common(xs, dtx, cs_ref[...], cst_ref[...])
        cs, cs_t, e, w, xd = q["cs"], q["cs_t"], q["e"], q["w"], q["xd"]
        bb, cb = b_ref[...].astype(BF16), c_ref[...].astype(BF16)
        gy = gy_ref[...]
        gy16, xd16 = gy.astype(BF16), xd.astype(BF16)
        h = st_ref[...]
        h16 = h.astype(BF16)
        ghn = gh_scr[g]
        ghn16 = ghn.astype(BF16)
        seg = _block_ones(gw, SSM_HEAD_DIM)
        cbm = lax.dot_general(cb, bb, NT_DIMS, preferred_element_type=F32)
        cbt = lax.dot_general(bb, cb, NT_DIMS, preferred_element_type=F32)

        gye16 = (gy * e).astype(BF16)
        chm = lax.dot_general(cb, h16, NT_DIMS, preferred_element_type=F32)
        g_c = jnp.dot(gye16, h16, preferred_element_type=F32)
        gh_off = lax.dot_general(gye16, cb, TN_DIMS, preferred_element_type=F32)
        g_e = _dot_split(gy * chm, seg, 0)
        bgs = lax.dot_general(bb, ghn16, NT_DIMS, preferred_element_type=F32)
        g_xd = w * bgs
        g_w = _dot_split(xd * bgs, seg, 0)
        g_b = jnp.dot((xd * w).astype(BF16), ghn16, preferred_element_type=F32)
        decay = _decay_col(cs_t, hg)
        gh_scr[g] = decay * ghn + gh_off
        rsum = jnp.sum(ghn * h, axis=1, keepdims=True)
        lane_head = lax.broadcasted_iota(jnp.int32, (ch, gw), 1) // SSM_HEAD_DIM
        lane_head1 = lax.broadcasted_iota(jnp.int32, (1, gw), 1) // SSM_HEAD_DIM
        g_el = jnp.zeros((1, gw), F32)
        g_cs = g_e * e - g_w * w
        upper = q["row"] <= q["col"]
        for j in range(hg):
            g_el = jnp.where(lane_head1 == j, jnp.sum(rsum[j * SSM_HEAD_DIM:(j + 1) * SSM_HEAD_DIM, :], axis=0, keepdims=True), g_el)
            csc = cs[:, j * SSM_HEAD_DIM:j * SSM_HEAD_DIM + 1]
            csr = cs_t[j:j + 1, :]
            lm = jnp.exp(jnp.where(q["tril"], csc - csr, NEG))
            lmt = jnp.exp(jnp.where(upper, csr - csc, NEG))
            gyj = jnp.where(lane_head == j, gy16, jnp.zeros_like(gy16))
            xdj = jnp.where(lane_head == j, xd16, jnp.zeros_like(xd16))
            gg = lax.dot_general(gyj, xd16, NT_DIMS, preferred_element_type=F32)
            ggt = lax.dot_general(xdj, gy16, NT_DIMS, preferred_element_type=F32)
            gcb, gcbt = gg * lm, ggt * lmt
            g_c = g_c + jnp.dot(gcb.astype(BF16), bb, preferred_element_type=F32)
            g_b = g_b + jnp.dot(gcbt.astype(BF16), cb, preferred_element_type=F32)
            gxdj = jnp.dot((cbt * lmt).astype(BF16), gy16, preferred_element_type=F32)
            g_xd = g_xd + jnp.where(lane_head == j, gxdj, 0.0)
            d_cs = jnp.sum(gcb * cbm, axis=1, keepdims=True) - jnp.sum(gcbt * cbt, axis=1, keepdims=True)
            g_cs = g_cs + jnp.where(lane_head == j, d_cs, 0.0)
        extra = _colsum(g_w * w) + g_el * jnp.exp(q["cs_last"])
        g_cs = g_cs + jnp.where(lax.broadcasted_iota(jnp.int32, (ch, gw), 0) == ch - 1, extra, 0.0)
        g_la = _dot_split(upper, g_cs, 1)
        a_x = -jnp.exp(alx_ref[...])
        gdt_ref[...] = g_xd * xs + g_la * a_x * (1.0 / SSM_HEAD_DIM)
        ga_row = _colsum(g_la * (dtx * a_x)) * (1.0 / SSM_HEAD_DIM)
        ga_ref[...] = jnp.where(lax.broadcasted_iota(jnp.int32, (8, gw), 0) == 0, ga_row, 0.0)
        gxs_ref[...] = g_xd * dtx + gy * dsk_ref[...]
        gb_ref[...] = g_b
        gc_ref[...] = g_c

    return pl.pallas_call(
        body, grid=(nch, SSM_GROUPS),
        in_specs=[pl.BlockSpec((ch, gw), lambda cc, g: (rc(cc), g)),
                  pl.BlockSpec((ch, SSM_STATE), lambda cc, g: (rc(cc), nb0 + g)),
                  pl.BlockSpec((ch, SSM_STATE), lambda cc, g: (rc(cc), nb0 + SSM_GROUPS + g)),
                  pl.BlockSpec((ch, gw), lambda cc, g: (rc(cc), g)),
                  pl.BlockSpec((ch, gw), lambda cc, g: (rc(cc), g)),
                  pl.BlockSpec((None, 8, ch), lambda cc, g: (g, 0, rc(cc))),
                  pl.BlockSpec((1, gw), lambda cc, g: (0, g)),
                  pl.BlockSpec((ch, gw), lambda cc, g: (rc(cc), g)),
                  pl.BlockSpec((None, None, gw, SSM_STATE), lambda cc, g: (rc(cc), g, 0, 0)),
                  pl.BlockSpec((1, gw), lambda cc, g: (0, g))],
        out_specs=[pl.BlockSpec((ch, gw), lambda cc, g: (rc(cc), g)),
                   pl.BlockSpec((ch, SSM_STATE), lambda cc, g: (rc(cc), g)),
                   pl.BlockSpec((ch, SSM_STATE), lambda cc, g: (rc(cc), g)),
                   pl.BlockSpec((ch, gw), lambda cc, g: (rc(cc), g)),
                   pl.BlockSpec((8, gw), lambda cc, g: (rc(cc), g))],
        out_shape=[S((t, SSM_INNER), F32), S((t, SSM_GROUPS * SSM_STATE), F32), S((t, SSM_GROUPS * SSM_STATE), F32),
                   S((t, SSM_INNER), F32), S((nch * 8, SSM_INNER), F32)],
        scratch_shapes=[pltpu.VMEM((SSM_GROUPS, gw, SSM_STATE), F32)],
        name="ssd_bwd", compiler_params=_params(("arbitrary", "arbitrary")),
    )(xa, xa, xa, dtx, csx, cst_g, alog_x, g_y, states, dskip_x)


def _local_step(x, target, w_pre, w_in_r, b_gate, conv_w, conv_b, dt_bias, a_log, d_skip, ssm_norm_w,
                w_att, w_ssm, w_out, w_post, w_fpre, w_up, w_down, w_fpost):
    t = x.shape[0]
    mm = functools.partial(_matmul, tm=512)
    slopes = _slope_rows()
    hg = SSM_HEADS // SSM_GROUPS
    dt_bias_pad = jnp.pad(dt_bias, ((0, 0), (0, LANE - SSM_HEADS)))
    alog_x = jnp.repeat(a_log, SSM_HEAD_DIM, axis=1)
    alog_pad = jnp.pad(a_log, ((0, 0), (0, LANE - SSM_HEADS)))
    dskip_x = jnp.repeat(d_skip, SSM_HEAD_DIM, axis=1)

    u = _pre_norm(x, w_pre)
    proj = mm(u, w_in_r, mode="nn", out_dtype=F32, name="in_proj", tn=1792, tk=D_MODEL)
    fwd = [_att_fwd(proj, dil, slopes) for _, dil in DILATED_PATTERNS]
    att, lse = _att_combine([o for o, _ in fwd], [l for _, l in fwd])
    xa = _conv_fwd(proj, conv_w, conv_b)
    dtx, csx, cst = _dt_fwd(proj, dt_bias_pad, alog_pad)
    cst_g = jnp.pad(cst[:SSM_HEADS].reshape(SSM_GROUPS, hg, t), ((0, 0), (0, 8 - hg), (0, 0)))
    y_ssd, states = _ssd_fwd(xa, dtx, csx, cst_g)
    y4 = _gate_norm_fwd(y_ssd, xa, proj, dskip_x, ssm_norm_w)
    att_p = mm(att, w_att, mode="nn", out_dtype=F32, name="att_proj", tn=D_MODEL, tk=ATT_WIDTH)
    ssm_p = mm(y4, w_ssm, mode="nn", out_dtype=F32, name="ssm_proj", tn=D_MODEL, tk=SSM_INNER)
    mixin = _gating_fwd(proj, b_gate, att_p, ssm_p)
    mixed = mm(mixin, w_out, mode="nn", out_dtype=F32, name="out_proj", tn=D_MODEL, tk=D_MODEL)
    h1, f = _mix_post_ffn_pre(x, mixed, w_post, w_fpre)
    act, up = _matmul(f, w_up, mode="nn", out_dtype=BF16, name="ffn_up", tm=2048, tn=FFN_HIDDEN // N_DEV, tk=D_MODEL, epilogue="relu2", stacked=True)
    dn = mm(act, w_down, mode="nn", out_dtype=F32, name="ffn_down", tn=D_MODEL, tk=FFN_HIDDEN)
    loss, g_h2, g_dn, gw_fpost = _loss_and_ffn_post_bwd(h1, dn, w_fpost, target)

    g_up = mm(g_dn, w_down, mode="nt", out_dtype=BF16, name="ffn_down_bwd_x", tn=2048, tk=D_MODEL, epilogue="relu2_bwd", extra=up)
    gw_down = _matmul(act, g_dn, mode="tn", out_dtype=BF16, name="ffn_down_bwd_w", tm=1024, tn=D_MODEL, tk=512)
    g_f = _matmul(g_up, w_up, mode="nt", out_dtype=F32, name="ffn_up_bwd_x", tm=2048, tn=D_MODEL, tk=FFN_HIDDEN // N_DEV, stacked=True)
    gw_up = _matmul(f, g_up, mode="tn", out_dtype=BF16, name="ffn_up_bwd_w", tm=D_MODEL, tn=FFN_HIDDEN // N_DEV, tk=2048, stacked=True)
    g_h1, g_mixed, gw_fpre, gw_post = _ffn_pre_mix_post_bwd(g_h2, g_f, h1, w_fpre, mixed, w_post)
    g_mixin = mm(g_mixed, w_out, mode="nt", out_dtype=F32, name="out_proj_bwd_x", tn=D_MODEL, tk=D_MODEL)
    gw_out = _matmul(mixin, g_mixed, mode="tn", out_dtype=BF16, name="out_proj_bwd_w", tm=D_MODEL, tn=D_MODEL, tk=512)
    g_att_p, g_ssm_p, g_gl, g_b_gate = _gating_bwd(g_mixin, proj, b_gate, att_p, ssm_p)
    g_att = mm(g_att_p, w_att, mode="nt", out_dtype=F32, name="att_proj_bwd_x", tn=ATT_WIDTH, tk=D_MODEL)
    gw_att = _matmul(att, g_att_p, mode="tn", out_dtype=BF16, name="att_proj_bwd_w", tm=ATT_WIDTH, tn=D_MODEL, tk=512)
    g_y4 = mm(g_ssm_p, w_ssm, mode="nt", out_dtype=F32, name="ssm_proj_bwd_x", tn=SSM_INNER, tk=D_MODEL)
    gw_ssm = _matmul(y4, g_ssm_p, mode="tn", out_dtype=BF16, name="ssm_proj_bwd_w", tm=1024, tn=D_MODEL, tk=512)
    g_y2, g_z, g_norm_w, _, g_d_skip = _gate_norm_bwd(g_y4, y_ssd, xa, proj, dskip_x, ssm_norm_w)
    g_xs, g_bm, g_cm, g_dtx, ga_rows = _ssd_bwd(xa, dtx, csx, cst_g, alog_x, g_y2, states, dskip_x)
    g_dt_raw, g_dt_bias, g_a_log = _dt_bwd(g_dtx, ga_rows, proj, dt_bias_pad)
    g_xc, g_conv_b, gcw0, gcw1, gcw2, gcw3 = _conv_bwd_act(g_xs, g_bm, g_cm, proj, conv_w, conv_b)
    g_xbc = _conv_bwd_in(g_xc, conv_w)
    delta = _att_delta(g_att, att)
    dqs, dks, dvs = [], [], []
    for _, dil in DILATED_PATTERNS:
        dqs.append(_att_bwd_q(proj, g_att, lse, delta, dil, slopes))
        dk, dv = _att_bwd_kv(proj, g_att, lse, delta, dil, slopes)
        dks.append(dk)
        dvs.append(dv)
    g_qkv = _att_grad_sum(dqs, dks, dvs)
    g_proj = jnp.concatenate([g_z, g_gl, g_xbc, g_qkv, g_dt_raw, jnp.zeros((t, PROJ_W - OFF_DT - LANE), BF16)], axis=1)
    g_u = mm(g_proj, w_in_r, mode="nt", out_dtype=F32, name="in_proj_bwd_x", tn=D_MODEL, tk=1792)
    gw_in_r = _matmul(u, g_proj, mode="tn", out_dtype=BF16, name="in_proj_bwd_w", tm=D_MODEL, tn=1792, tk=512)
    g_x, gw_pre = _pre_norm_bwd(g_h1, g_u, x, w_pre)

    grads = dict(
        norm_mix_pre_w=gw_pre, w_in_r=gw_in_r, b_gate=g_b_gate, conv_w=jnp.concatenate([gcw0, gcw1, gcw2, gcw3], axis=0),
        conv_b=g_conv_b, dt_bias=g_dt_bias[:, :SSM_HEADS], a_log=g_a_log[:, :SSM_HEADS], d_skip=g_d_skip[:, :SSM_HEADS],
        ssm_norm_w=g_norm_w, w_att_proj=gw_att, w_ssm_proj=gw_ssm, w_out=gw_out, norm_mix_post_w=gw_post,
        norm_ffn_pre_w=gw_fpre, w_up=gw_up, w_down=gw_down, norm_ffn_post_w=gw_fpost)
    return loss, g_x, grads


def _mesh_pos():
    return lax.axis_index("x"), lax.axis_index("y"), lax.axis_index("c")


def _all_gather(shards):
    n = len(shards)

    def body(*refs):
        x_refs, o_refs = refs[:n], refs[n:2 * n]
        send_sems, recv_sems, local_sems = refs[2 * n:]
        x, y, c = _mesh_pos()
        me, sibling = (x, y, c), (x, y, 1 - c)
        chips = [(1 - x, y), (x, 1 - y), (1 - x, 1 - y)]

        def copy(a, k, block, to, src=None):
            dst = o_refs[a].at[4 * block[0] + 2 * block[1] + block[2]]
            return pltpu.make_async_remote_copy(
                src_ref=dst if src is None else src, dst_ref=dst, send_sem=send_sems.at[7 * a + k], recv_sem=recv_sems.at[7 * a + k],
                device_id=to, device_id_type=pl.DeviceIdType.MESH)

        mine = [pltpu.make_async_copy(x_refs[a], o_refs[a].at[4 * x + 2 * y + c], local_sems.at[a]) for a in range(n)]
        for cp in mine:
            cp.start()
        first = []
        for a in range(n):
            first.append(copy(a, 0, me, sibling, src=x_refs[a]))
            first += [copy(a, 1 + j, me, (*chip, c), src=x_refs[a]) for j, chip in enumerate(chips)]
        for cp in first:
            cp.start()
        passed = []
        for j, chip in enumerate(chips):
            for a in range(n):
                copy(a, 1 + j, (*chip, c), me).wait_recv()
                passed.append(copy(a, 4 + j, (*chip, c), sibling))
                passed[-1].start()
        for a in range(n):
            copy(a, 0, sibling, me).wait_recv()
            for j, chip in enumerate(chips):
                copy(a, 4 + j, (*chip, 1 - c), me).wait_recv()
        for cp in first + passed:
            cp.wait_send()
        for cp in mine:
            cp.wait()

    hbm = pl.BlockSpec(memory_space=pltpu.HBM)
    return pl.pallas_call(
        body, out_shape=[S((N_DEV,) + s.shape, s.dtype) for s in shards],
        in_specs=[hbm] * n, out_specs=[hbm] * n,
        scratch_shapes=[pltpu.SemaphoreType.DMA((7 * n,)), pltpu.SemaphoreType.DMA((7 * n,)), pltpu.SemaphoreType.DMA((n,))],
        name="weights_all_gather",
    )(*shards)


def _exchange_grads(slab_arrays, small):
    n = len(slab_arrays)
    r_small = small.shape[0]

    def body(*refs):
        slab_refs, small_ref = refs[:n], refs[n]
        recv_refs, gsm_ref = refs[n + 1:2 * n + 1], refs[2 * n + 1]
        send_sems, recv_sems, local_sems = refs[2 * n + 2:]
        x, y, c = _mesh_pos()
        me = 4 * x + 2 * y + c

        def peer(k):
            px = 1 - x if k & 4 else x
            py = 1 - y if k & 2 else y
            pc = 1 - c if k & 1 else c
            return (px, py, pc), 4 * px + 2 * py + pc

        def copy(a, k, sending):
            to, lin = peer(k)
            sem = 7 * a + k - 1
            if a == n:
                src, dst = small_ref, gsm_ref.at[me if sending else lin]
            else:
                src, dst = slab_refs[a].at[lin], recv_refs[a].at[me if sending else lin]
            return pltpu.make_async_remote_copy(src_ref=src, dst_ref=dst, send_sem=send_sems.at[sem], recv_sem=recv_sems.at[sem],
                                                device_id=to, device_id_type=pl.DeviceIdType.MESH)

        own = [pltpu.make_async_copy(slab_refs[a].at[me], recv_refs[a].at[me], local_sems.at[a]) for a in range(n)]
        own.append(pltpu.make_async_copy(small_ref, gsm_ref.at[me], local_sems.at[n]))
        for cp in own:
            cp.start()
        order = [n] + list(range(n))
        sends = [copy(a, k, True) for a in order for k in range(1, N_DEV)]
        for cp in sends:
            cp.start()
        for a in order:
            for k in range(1, N_DEV):
                copy(a, k, False).wait_recv()
        for cp in sends:
            cp.wait_send()
        for cp in own:
            cp.wait()

    hbm = pl.BlockSpec(memory_space=pltpu.HBM)
    n_sem = 7 * (n + 1)
    res = pl.pallas_call(
        body, out_shape=[S(a.shape, a.dtype) for a in slab_arrays] + [S((N_DEV, r_small, LANE), small.dtype)],
        in_specs=[hbm] * (n + 1), out_specs=[hbm] * (n + 1),
        scratch_shapes=[pltpu.SemaphoreType.DMA((n_sem,)), pltpu.SemaphoreType.DMA((n_sem,)), pltpu.SemaphoreType.DMA((n + 1,))],
        name="grad_exchange",
    )(*slab_arrays, small)
    return res[:n], res[n]


def _adamw(w, m, v, slabs, name, tr):
    r, cols = w.shape
    c1 = 1.0 - ADAM_B1 ** ADAM_STEP
    c2 = 1.0 - ADAM_B2 ** ADAM_STEP

    def body(w_ref, m_ref, v_ref, s_ref, g_ref, d_ref, nm_ref, nv_ref):
        g = s_ref[0].astype(F32)
        for d in range(1, N_DEV):
            g = g + s_ref[d].astype(F32)
        nm = ADAM_B1 * m_ref[...] + (1.0 - ADAM_B1) * g
        nv = ADAM_B2 * v_ref[...] + (1.0 - ADAM_B2) * (g * g)
        g_ref[...] = g
        nm_ref[...] = nm
        nv_ref[...] = nv
        d_ref[...] = -ADAM_LR * ((nm / c1) / (jnp.sqrt(nv / c2) + ADAM_EPS) + ADAM_WD * w_ref[...])

    assert r % tr == 0, name
    blk = pl.BlockSpec((tr, cols), lambda i: (i, 0))
    return pl.pallas_call(
        body, grid=(r // tr,), in_specs=[blk, blk, blk, pl.BlockSpec((N_DEV, tr, cols), lambda i: (0, i, 0))],
        out_specs=[blk] * 4, out_shape=[S((r, cols), F32)] * 4, name=name, compiler_params=_params(("parallel",)),
    )(w, m, v, slabs)


BIG = ("w_in", "w_att_proj", "w_up", "w_ssm_proj", "w_out", "w_down", "conv_w")
ADAMW_ROWS = dict(w_in=256, w_att_proj=768, w_up=512, w_ssm_proj=256, w_out=128, w_down=256, conv_w=4)
SMALL = ("norm_mix_pre_w", "b_gate", "conv_b", "dt_bias", "a_log", "d_skip", "ssm_norm_w", "norm_mix_post_w",
         "norm_ffn_pre_w", "norm_ffn_post_w")
ORDER = ("norm_mix_pre_w", "w_in", "b_gate", "conv_w", "conv_b", "dt_bias", "a_log", "d_skip", "ssm_norm_w", "w_att_proj",
         "w_ssm_proj", "w_out", "norm_mix_post_w", "norm_ffn_pre_w", "w_up", "w_down", "norm_ffn_post_w")
ROW_SHARDED = ("w_ssm_proj", "w_out", "w_down")
IN_PROJ_W = 10528
IN_SHARD_W = IN_PROJ_W // N_DEV
IN_SEGMENTS = ((2304, 4352), (8480, 10528), (4352, 8448), (0, 2304), (8448, 8480))


def _pack(parts, rows_multiple):
    flat = jnp.concatenate([p.reshape(-1) for p in parts])
    pad = (-flat.shape[0]) % (rows_multiple * LANE)
    return jnp.pad(flat, (0, pad)).reshape(-1, LANE)


def _unpack(flat2d, shapes):
    flat, out, off = flat2d.reshape(-1), [], 0
    for sh in shapes:
        n = int(np.prod(sh))
        out.append(flat[off:off + n].reshape(sh))
        off += n
    return out


def _reorder_in_proj(w):
    qkv, z, xbc = w[:, :2304], w[:, 2304:4352], w[:, 4352:8448]
    dt, gate = w[:, 8448:8480], w[:, 8480:10528]
    return jnp.concatenate([z, gate, xbc, qkv, dt, jnp.zeros((w.shape[0], PROJ_W - 10528), w.dtype)], axis=1)


def _restore_in_proj(wr):
    return jnp.concatenate([wr[:, OFF_QKV:OFF_QKV + 2304], wr[:, OFF_Z:OFF_Z + 2048], wr[:, OFF_XBC:OFF_XBC + 4096],
                            wr[:, OFF_DT:OFF_DT + 32], wr[:, OFF_GL:OFF_GL + 2048]], axis=1)


def _assemble_in_proj(g):
    pieces = []
    for lo, hi in IN_SEGMENTS:
        while lo < hi:
            d = lo // IN_SHARD_W
            end = min(hi, (d + 1) * IN_SHARD_W)
            pieces.append(g[d][:, lo - d * IN_SHARD_W:end - d * IN_SHARD_W])
            lo = end
    pieces.append(jnp.zeros((g.shape[1], PROJ_W - IN_PROJ_W), g.dtype))
    return jnp.concatenate(pieces, axis=1)


def _in_proj_slabs(wr):
    orig = _restore_in_proj(wr)
    return jnp.stack([orig[:, d * IN_SHARD_W:(d + 1) * IN_SHARD_W] for d in range(N_DEV)])


def kernel(x, norm_mix_pre_w, w_in, b_gate, conv_w, conv_b, dt_bias, a_log, d_skip, ssm_norm_w, w_att_proj, w_ssm_proj, w_out, norm_mix_post_w, norm_ffn_pre_w, w_up, w_down, norm_ffn_post_w, loss_target, m_norm_mix_pre_w, m_w_in, m_b_gate, m_conv_w, m_conv_b, m_dt_bias, m_a_log, m_d_skip, m_ssm_norm_w, m_w_att_proj, m_w_ssm_proj, m_w_out, m_norm_mix_post_w, m_norm_ffn_pre_w, m_w_up, m_w_down, m_norm_ffn_post_w, v_norm_mix_pre_w, v_w_in, v_b_gate, v_conv_w, v_conv_b, v_dt_bias, v_a_log, v_d_skip, v_ssm_norm_w, v_w_att_proj, v_w_ssm_proj, v_w_out, v_norm_mix_post_w, v_norm_ffn_pre_w, v_w_up, v_w_down, v_norm_ffn_post_w):
    w = dict(norm_mix_pre_w=norm_mix_pre_w, w_in=w_in, b_gate=b_gate, conv_w=conv_w, conv_b=conv_b, dt_bias=dt_bias, a_log=a_log,
             d_skip=d_skip, ssm_norm_w=ssm_norm_w, w_att_proj=w_att_proj, w_ssm_proj=w_ssm_proj, w_out=w_out,
             norm_mix_post_w=norm_mix_post_w, norm_ffn_pre_w=norm_ffn_pre_w, w_up=w_up, w_down=w_down, norm_ffn_post_w=norm_ffn_post_w)
    m = dict(norm_mix_pre_w=m_norm_mix_pre_w, w_in=m_w_in, b_gate=m_b_gate, conv_w=m_conv_w, conv_b=m_conv_b, dt_bias=m_dt_bias,
             a_log=m_a_log, d_skip=m_d_skip, ssm_norm_w=m_ssm_norm_w, w_att_proj=m_w_att_proj, w_ssm_proj=m_w_ssm_proj, w_out=m_w_out,
             norm_mix_post_w=m_norm_mix_post_w, norm_ffn_pre_w=m_norm_ffn_pre_w, w_up=m_w_up, w_down=m_w_down, norm_ffn_post_w=m_norm_ffn_post_w)
    v = dict(norm_mix_pre_w=v_norm_mix_pre_w, w_in=v_w_in, b_gate=v_b_gate, conv_w=v_conv_w, conv_b=v_conv_b, dt_bias=v_dt_bias,
             a_log=v_a_log, d_skip=v_d_skip, ssm_norm_w=v_ssm_norm_w, w_att_proj=v_w_att_proj, w_ssm_proj=v_w_ssm_proj, w_out=v_w_out,
             norm_mix_post_w=v_norm_mix_post_w, norm_ffn_pre_w=v_norm_ffn_pre_w, w_up=v_w_up, w_down=v_w_down, norm_ffn_post_w=v_norm_ffn_post_w)
    shard_shapes = {n: w[n].shape[1:] for n in ORDER}

    gathered = _all_gather([w[n][0] if n == "conv_w" else w[n][0].astype(BF16) for n in BIG])
    full = dict(zip(BIG, gathered))
    for n in ROW_SHARDED:
        full[n] = full[n].reshape(-1, full[n].shape[2])
    conv_full = jnp.moveaxis(full["conv_w"], 0, 1).reshape(SSM_CONV, CONV_DIM)
    full["w_att_proj"] = jnp.moveaxis(full["w_att_proj"], 0, 1).reshape(ATT_WIDTH, D_MODEL)

    loss, g_x, grads = _local_step(
        x[0], loss_target[0], w["norm_mix_pre_w"], _assemble_in_proj(full["w_in"]), w["b_gate"], conv_full, w["conv_b"],
        w["dt_bias"], w["a_log"], w["d_skip"], w["ssm_norm_w"], full["w_att_proj"], full["w_ssm_proj"], full["w_out"],
        w["norm_mix_post_w"], w["norm_ffn_pre_w"], full["w_up"], full["w_down"], w["norm_ffn_post_w"])

    slabs = dict(w_in=_in_proj_slabs(grads["w_in_r"]), w_up=grads["w_up"],
                 w_att_proj=jnp.moveaxis(grads["w_att_proj"].reshape(ATT_WIDTH, N_DEV, -1), 1, 0),
                 conv_w=jnp.moveaxis(grads["conv_w"].reshape(SSM_CONV, N_DEV, -1), 1, 0))
    for n in ROW_SHARDED:
        slabs[n] = grads[n].reshape(N_DEV, -1, grads[n].shape[1])
    small = _pack([grads[n].astype(F32) for n in SMALL], 8)
    recv, small_all = _exchange_grads([slabs[n] for n in BIG], small)

    small_shapes = [shard_shapes[n] for n in SMALL]
    small_out = _adamw(*[_pack([d_[n][0] for n in SMALL], 8) for d_ in (w, m, v)], small_all, "adamw_replicated", small_all.shape[1])
    big_out = {n: _adamw(w[n][0], m[n][0], v[n][0], r, "adamw_" + n, ADAMW_ROWS[n]) for n, r in zip(BIG, recv)}
    res = []
    for which, small_flat in enumerate(small_out):
        vals = {n: big_out[n][which] for n in BIG}
        vals.update(zip(SMALL, _unpack(small_flat, small_shapes)))
        res.append([vals[n][None] for n in ORDER])
    g_out, d_out, m_out, v_out = res
    total = lax.psum(loss[0, 0], ("x", "y", "c"))
    return (total, g_x[None], *g_out, *d_out, *m_out, *v_out)
```

```python
import functools
import math

import jax
import jax.numpy as jnp
import numpy as np
from jax import lax
from jax.experimental import pallas as pl
from jax.experimental.pallas import tpu as pltpu

F32 = jnp.float32
BF16 = jnp.bfloat16

D_MODEL = 1024
HEAD_DIM = 64
N_ATT_HEADS = 12
ATT_WIDTH = N_ATT_HEADS * HEAD_DIM
DILATED_PATTERNS = ((128, 1), (512, 4), (2048, 16))
ATT_BLOCK = 128
SSM_INNER = 2048
SSM_HEAD_DIM = 64
SSM_HEADS = 32
SSM_GROUPS = 8
SSM_STATE = 128
SSM_CHUNK = 128
CONV_DIM = 4096
SSM_CONV = 4
FFN_HIDDEN = 4096
RMS_EPS = 1e-6
N_DEV = 8

ADAM_LR = 0.001
ADAM_B1 = 0.9
ADAM_B2 = 0.999
ADAM_EPS = 1e-08
ADAM_WD = 0.01
ADAM_STEP = 10

LANE = 128
OFF_Z, OFF_GL, OFF_XBC, OFF_QKV, OFF_DT = 0, 2048, 4096, 8192, 10496
PROJ_W = 10752
PROJ_BLOCKS = PROJ_W // LANE
VMEM_LIMIT = 52 * 1024 * 1024
NEG = -1e30

HI = lax.Precision.HIGHEST
NT_DIMS = (((1,), (1,)), ((), ()))
TN_DIMS = (((0,), (0,)), ((), ()))
S = jax.ShapeDtypeStruct


def _params(sem):
    return pltpu.CompilerParams(dimension_semantics=sem, vmem_limit_bytes=VMEM_LIMIT)


def _matmul(a, b, *, mode, out_dtype, name, tm, tn, tk, epilogue=None, extra=None, stacked=False, after=None):
    if mode == "nn":
        m, k = a.shape
        n = b.shape[0] * b.shape[2] if stacked else b.shape[1]
        a_spec = pl.BlockSpec((tm, tk), lambda i, j, kk: (i, kk))
        b_spec = pl.BlockSpec((None, tk, tn), lambda i, j, kk: (j, kk, 0)) if stacked else pl.BlockSpec((tk, tn), lambda i, j, kk: (kk, j))
        dims = (((1,), (0,)), ((), ()))
    elif mode == "nt":
        m, k = a.shape
        n = b.shape[1] if stacked else b.shape[0]
        a_spec = pl.BlockSpec((tm, tk), lambda i, j, kk: (i, kk))
        b_spec = pl.BlockSpec((None, tn, tk), lambda i, j, kk: (kk, j, 0)) if stacked else pl.BlockSpec((tn, tk), lambda i, j, kk: (j, kk))
        dims = NT_DIMS
    else:
        (k, m), n = a.shape, b.shape[1]
        a_spec = pl.BlockSpec((tk, tm), lambda i, j, kk: (kk, i))
        b_spec = pl.BlockSpec((tk, tn), lambda i, j, kk: (kk, j))
        dims = TN_DIMS
    assert m % tm == 0 and n % tn == 0 and k % tk == 0, (name, m, n, k)
    if stacked:
        assert (tk if mode == "nt" else tn) * N_DEV == (k if mode == "nt" else n), name
    nk = k // tk
    o_spec = pl.BlockSpec((tm, tn), lambda i, j, kk: (i, j))
    in_specs, args = [a_spec, b_spec], [a, b]
    if epilogue == "relu2":
        out_shape = (S((m, n), BF16), S((m, n), BF16))
        out_specs = (o_spec, o_spec)
    elif stacked and mode == "tn":
        out_shape, out_specs = S((N_DEV, m, tn), out_dtype), pl.BlockSpec((None, tm, tn), lambda i, j, kk: (j, i, 0))
    else:
        out_shape, out_specs = S((m, n), out_dtype), o_spec
    if epilogue == "relu2_bwd":
        in_specs.append(o_spec)
        args.append(extra)
    n_in = len(args)
    if after is not None:
        in_specs.append(pl.BlockSpec(after.shape, lambda i, j, kk: (0,) * after.ndim))
        args.append(after)

    def finish(acc, refs):
        if epilogue == "relu2":
            r = jnp.maximum(acc, 0.0)
            refs[0][...] = (r * r).astype(BF16)
            refs[1][...] = acc.astype(BF16)
        elif epilogue == "relu2_bwd":
            up = refs[0][...].astype(F32)
            refs[1][...] = (acc * (2.0 * jnp.maximum(up, 0.0))).astype(out_dtype)
        else:
            refs[0][...] = acc.astype(out_dtype)

    def body(a_ref, b_ref, *rest):
        rest = rest[:n_in - 2] + rest[len(args) - 2:]
        part = lax.dot_general(a_ref[...].astype(BF16), b_ref[...].astype(BF16), dims, preferred_element_type=F32)
        if nk == 1:
            finish(part, rest)
            return
        acc_ref = rest[-1]
        kk = pl.program_id(2)

        @pl.when(kk == 0)
        def _():
            acc_ref[...] = part

        @pl.when(kk > 0)
        def _():
            acc_ref[...] += part

        @pl.when(kk == nk - 1)
        def _():
            finish(acc_ref[...], rest[:-1])

    scratch = [] if nk == 1 else [pltpu.VMEM((tm, tn), F32)]
    return pl.pallas_call(
        body, grid=(m // tm, n // tn, nk), in_specs=in_specs, out_specs=out_specs, out_shape=out_shape,
        scratch_shapes=scratch, name=name, compiler_params=_params(("parallel", "parallel", "arbitrary")),
    )(*args)


def _rowcall(body, name, n_rows, tr, ins, outs, scratch=()):
    res = pl.pallas_call(
        body, grid=(n_rows // tr,),
        in_specs=[pl.BlockSpec(bs, im) for _, bs, im in ins],
        out_specs=[pl.BlockSpec(bs, im) for _, _, bs, im in outs],
        out_shape=[S(sh, dt) for sh, dt, _, _ in outs],
        scratch_shapes=list(scratch), name=name, compiler_params=_params(("arbitrary",)),
    )(*[a for a, _, _ in ins])
    return res


def _rows(arr, tr, width=None, cb=0):
    width = arr.shape[1] if width is None else width
    return (arr, (tr, width), lambda i, cb=cb: (i, cb))


def _whole(arr):
    nd = arr.ndim
    return (arr, arr.shape, lambda i, nd=nd: (0,) * nd)


def _orow(n_rows, width, dtype, tr):
    return ((n_rows, width), dtype, (tr, width), lambda i: (i, 0))


def _oacc(width):
    return ((1, width), F32, (1, width), lambda i: (0, 0))


def _accumulate(ref, value):
    first = pl.program_id(0) == 0

    @pl.when(first)
    def _():
        ref[...] = value

    @pl.when(jnp.logical_not(first))
    def _():
        ref[...] += value


def _colsum(v):
    return jnp.sum(v, axis=0, keepdims=True)


def _rms_fwd(x, w):
    r = lax.rsqrt(jnp.mean(x * x, axis=-1, keepdims=True) + RMS_EPS)
    return x * r * w


def _rms_bwd(gy, x, w):
    r = lax.rsqrt(jnp.mean(x * x, axis=-1, keepdims=True) + RMS_EPS)
    xn = x * r
    gxn = gy * w
    gx = r * (gxn - xn * jnp.mean(gxn * xn, axis=-1, keepdims=True))
    return gx, _colsum(gy * xn)


def _sigmoid(x):
    return 1.0 / (1.0 + jnp.exp(-x))


def _head_expand(n_heads_pad, n_heads, width):
    h = lax.broadcasted_iota(jnp.int32, (n_heads_pad, n_heads * width), 0)
    c = lax.broadcasted_iota(jnp.int32, (n_heads_pad, n_heads * width), 1)
    return (c // width == h).astype(F32)


def _head_reduce(n_heads, width, n_heads_pad):
    c = lax.broadcasted_iota(jnp.int32, (n_heads * width, n_heads_pad), 0)
    h = lax.broadcasted_iota(jnp.int32, (n_heads * width, n_heads_pad), 1)
    return (c // width == h).astype(F32)


def _block_ones(n, width):
    r = lax.broadcasted_iota(jnp.int32, (n, n), 0)
    c = lax.broadcasted_iota(jnp.int32, (n, n), 1)
    return (r // width == c // width).astype(F32)


def _pre_norm(x, w_pre, tr=512):
    t = x.shape[0]

    def body(x_ref, w_ref, u_ref):
        u_ref[...] = _rms_fwd(x_ref[...], w_ref[...]).astype(BF16)

    return _rowcall(body, "pre_norm", t, tr, [_rows(x, tr), _whole(w_pre)], [_orow(t, D_MODEL, BF16, tr)])[0]


def _conv_fwd(proj, conv_w, conv_b, tr=256):
    t = proj.shape[0]
    cb = OFF_XBC // CONV_DIM
    halo = (proj, (8, CONV_DIM), lambda i: (jnp.maximum(i * (tr // 8) - 1, 0), cb))

    def body(cur_ref, prev_ref, w_ref, b_ref, o_ref, ext):
        ext[pl.ds(0, 8), :] = jnp.where(pl.program_id(0) > 0, prev_ref[...], 0.0)
        ext[pl.ds(8, tr), :] = cur_ref[...]
        acc = b_ref[...] + w_ref[3:4, :] * cur_ref[...]
        for k in range(SSM_CONV - 1):
            acc = acc + w_ref[k:k + 1, :] * ext[pl.ds(8 - 3 + k, tr), :]
        o_ref[...] = acc * _sigmoid(acc)

    return _rowcall(body, "conv_fwd", t, tr, [_rows(proj, tr, CONV_DIM, cb), halo, _whole(conv_w), _whole(conv_b)],
                    [_orow(t, CONV_DIM, F32, tr)], scratch=[pltpu.VMEM((tr + 8, CONV_DIM), F32)])[0]


def _dt_fwd(proj, dt_bias_pad, alog_pad, tr=512):
    t = proj.shape[0]

    def body(raw_ref, b_ref, al_ref, dtx_ref, csx_ref, cst_ref):
        v = raw_ref[...] + b_ref[...]
        dt = jnp.maximum(v, 0.0) + jnp.log1p(jnp.exp(-jnp.abs(v)))
        expand = _head_expand(LANE, SSM_HEADS, SSM_HEAD_DIM)
        dtx_ref[...] = jnp.dot(dt, expand, precision=HI, preferred_element_type=F32)
        la = dt * (-jnp.exp(al_ref[...]))
        row = lax.broadcasted_iota(jnp.int32, (SSM_CHUNK, SSM_CHUNK), 0)
        col = lax.broadcasted_iota(jnp.int32, (SSM_CHUNK, SSM_CHUNK), 1)
        tril = (col <= row).astype(F32)
        cs = jnp.concatenate([jnp.dot(tril, la[k * SSM_CHUNK:(k + 1) * SSM_CHUNK, :], precision=HI, preferred_element_type=F32)
                              for k in range(tr // SSM_CHUNK)], axis=0)
        csx_ref[...] = jnp.dot(cs, expand, precision=HI, preferred_element_type=F32)
        cst_ref[...] = cs.T

    return _rowcall(body, "dt_fwd", t, tr, [_rows(proj, tr, LANE, OFF_DT // LANE), _whole(dt_bias_pad), _whole(alog_pad)],
                    [_orow(t, SSM_INNER, F32, tr), _orow(t, SSM_INNER, F32, tr), ((LANE, t), F32, (LANE, tr), lambda i: (0, i))])


def _gate_norm_fwd(y_ssd, xa, proj, dskip_x, norm_w, tr=256):
    t = y_ssd.shape[0]
    gw = SSM_INNER // SSM_GROUPS

    def body(y_ref, xs_ref, z_ref, d_ref, w_ref, o_ref):
        z = z_ref[...]
        y3 = (y_ref[...] + d_ref[...] * xs_ref[...]) * (z * _sigmoid(z))
        for g in range(SSM_GROUPS):
            sl = slice(g * gw, (g + 1) * gw)
            o_ref[:, sl] = _rms_fwd(y3[:, sl], w_ref[:, sl]).astype(BF16)

    return _rowcall(body, "gate_norm_fwd", t, tr,
                    [_rows(y_ssd, tr), _rows(xa, tr, SSM_INNER, 0), _rows(proj, tr, SSM_INNER, OFF_Z // SSM_INNER), _whole(dskip_x), _whole(norm_w)],
                    [_orow(t, SSM_INNER, BF16, tr)])[0]


def _gating_fwd(proj, b_gate, att_p, ssm_p, tr=512):
    t = proj.shape[0]

    def body(gl_ref, b_ref, a_ref, s_ref, o_ref):
        gates = _sigmoid(gl_ref[...] + b_ref[...])
        o_ref[...] = (gates[:, :D_MODEL] * a_ref[...] + gates[:, D_MODEL:] * s_ref[...]).astype(BF16)

    return _rowcall(body, "gating_fwd", t, tr, [_rows(proj, tr, 2 * D_MODEL, OFF_GL // (2 * D_MODEL)), _whole(b_gate), _rows(att_p, tr), _rows(ssm_p, tr)],
                    [_orow(t, D_MODEL, BF16, tr)])[0]


def _mix_post_ffn_pre(x, mixed, w_post, w_fpre, tr=512):
    t = x.shape[0]

    def body(x_ref, m_ref, wp_ref, wf_ref, h1_ref, f_ref):
        h1 = x_ref[...] + _rms_fwd(m_ref[...], wp_ref[...])
        h1_ref[...] = h1
        f_ref[...] = _rms_fwd(h1, wf_ref[...]).astype(BF16)

    return _rowcall(body, "mix_post_ffn_pre", t, tr, [_rows(x, tr), _rows(mixed, tr), _whole(w_post), _whole(w_fpre)],
                    [_orow(t, D_MODEL, F32, tr), _orow(t, D_MODEL, BF16, tr)])


def _loss_and_ffn_post_bwd(h1, dn, w_fpost, target, tr=512):
    t = h1.shape[0]

    def body(h1_ref, dn_ref, w_ref, tg_ref, loss_ref, gh2_ref, gdn_ref, gw_ref):
        dn = dn_ref[...]
        w = w_ref[...]
        err = h1_ref[...] + _rms_fwd(dn, w) - tg_ref[...]
        _accumulate(loss_ref, jnp.zeros((1, LANE), F32) + 0.5 * jnp.sum(jnp.mean(err * err, axis=-1, keepdims=True)))
        gh2 = err * (1.0 / D_MODEL)
        gh2_ref[...] = gh2
        gdn, gw = _rms_bwd(gh2, dn, w)
        gdn_ref[...] = gdn.astype(BF16)
        _accumulate(gw_ref, gw)

    return _rowcall(body, "loss_ffn_post_bwd", t, tr, [_rows(h1, tr), _rows(dn, tr), _whole(w_fpost), _rows(target, tr)],
                    [_oacc(LANE), _orow(t, D_MODEL, F32, tr), _orow(t, D_MODEL, BF16, tr), _oacc(D_MODEL)])


def _ffn_pre_mix_post_bwd(g_h2, g_f, h1, w_fpre, mixed, w_post, tr=512):
    t = h1.shape[0]

    def body(gh2_ref, gf_ref, h1_ref, wf_ref, m_ref, wp_ref, gh1_ref, gm_ref, gwf_ref, gwp_ref):
        gx, gwf = _rms_bwd(gf_ref[...], h1_ref[...], wf_ref[...])
        gh1 = gh2_ref[...] + gx
        gh1_ref[...] = gh1
        gm, gwp = _rms_bwd(gh1, m_ref[...], wp_ref[...])
        gm_ref[...] = gm.astype(BF16)
        _accumulate(gwf_ref, gwf)
        _accumulate(gwp_ref, gwp)

    return _rowcall(body, "ffn_pre_mix_post_bwd", t, tr,
                    [_rows(g_h2, tr), _rows(g_f, tr), _rows(h1, tr), _whole(w_fpre), _rows(mixed, tr), _whole(w_post)],
                    [_orow(t, D_MODEL, F32, tr), _orow(t, D_MODEL, BF16, tr), _oacc(D_MODEL), _oacc(D_MODEL)])


def _gating_bwd(g_mixin, proj, b_gate, att_p, ssm_p, tr=512):
    t = proj.shape[0]

    def body(gm_ref, gl_ref, b_ref, a_ref, s_ref, ga_ref, gs_ref, ggl_ref, gb_ref):
        gates = _sigmoid(gl_ref[...] + b_ref[...])
        gm = gm_ref[...]
        g_att, g_ssm = gates[:, :D_MODEL], gates[:, D_MODEL:]
        ga_ref[...] = (gm * g_att).astype(BF16)
        gs_ref[...] = (gm * g_ssm).astype(BF16)
        ggl_a = gm * a_ref[...] * g_att * (1.0 - g_att)
        ggl_s = gm * s_ref[...] * g_ssm * (1.0 - g_ssm)
        ggl_ref[:, :D_MODEL] = ggl_a.astype(BF16)
        ggl_ref[:, D_MODEL:] = ggl_s.astype(BF16)
        _accumulate(gb_ref.at[:, :D_MODEL], _colsum(ggl_a))
        _accumulate(gb_ref.at[:, D_MODEL:], _colsum(ggl_s))

    return _rowcall(body, "gating_bwd", t, tr,
                    [_rows(g_mixin, tr), _rows(proj, tr, 2 * D_MODEL, OFF_GL // (2 * D_MODEL)), _whole(b_gate), _rows(att_p, tr), _rows(ssm_p, tr)],
                    [_orow(t, D_MODEL, BF16, tr), _orow(t, D_MODEL, BF16, tr), _orow(t, 2 * D_MODEL, BF16, tr), _oacc(2 * D_MODEL)])


def _gate_norm_bwd(g_y4, y_ssd, xa, proj, dskip_x, norm_w, tr=256):
    t = y_ssd.shape[0]
    gw = SSM_INNER // SSM_GROUPS

    def body(g_ref, y_ref, xs_ref, z_ref, d_ref, w_ref, gy2_ref, gz_ref, gnw_ref, gdx_ref, gd_ref):
        z = z_ref[...]
        xs = xs_ref[...]
        sg = _sigmoid(z)
        sz = z * sg
        y2 = y_ref[...] + d_ref[...] * xs
        y3 = y2 * sz
        g4 = g_ref[...]
        for g in range(SSM_GROUPS):
            sl = slice(g * gw, (g + 1) * gw)
            gy3, gnw = _rms_bwd(g4[:, sl], y3[:, sl], w_ref[:, sl])
            _accumulate(gnw_ref.at[:, sl], gnw)
            gy2 = gy3 * sz[:, sl]
            gy2_ref[:, sl] = gy2
            gz_ref[:, sl] = (gy3 * y2[:, sl] * (sg[:, sl] * (1.0 + z[:, sl] * (1.0 - sg[:, sl])))).astype(BF16)
            _accumulate(gdx_ref.at[:, sl], _colsum(gy2 * xs[:, sl]))
        tot = jnp.broadcast_to(gdx_ref[...], (8, SSM_INNER))
        gd_ref[...] = jnp.dot(tot, _head_reduce(SSM_HEADS, SSM_HEAD_DIM, LANE), precision=HI, preferred_element_type=F32)[0:1, :]

    return _rowcall(body, "gate_norm_bwd", t, tr,
                    [_rows(g_y4, tr), _rows(y_ssd, tr), _rows(xa, tr, SSM_INNER, 0), _rows(proj, tr, SSM_INNER, OFF_Z // SSM_INNER), _whole(dskip_x), _whole(norm_w)],
                    [_orow(t, SSM_INNER, F32, tr), _orow(t, SSM_INNER, BF16, tr), _oacc(SSM_INNER), _oacc(SSM_INNER), _oacc(LANE)])


def _dt_bwd(g_dtx, ga_rows, proj, dt_bias_pad, tr=512):
    t = proj.shape[0]

    def body(g_ref, ga_ref, raw_ref, b_ref, o_ref, gb_ref, gal_ref):
        red = _head_reduce(SSM_HEADS, SSM_HEAD_DIM, LANE)
        gdt = jnp.dot(g_ref[...], red, precision=HI, preferred_element_type=F32)
        graw = gdt * _sigmoid(raw_ref[...] + b_ref[...])
        o_ref[...] = graw.astype(BF16)
        _accumulate(gb_ref, _colsum(graw))
        tot = jnp.broadcast_to(_colsum(ga_ref[...]), (8, SSM_INNER))
        gal_ref[...] = jnp.dot(tot, red, precision=HI, preferred_element_type=F32)[0:1, :]

    return _rowcall(body, "dt_bwd", t, tr, [_rows(g_dtx, tr), _whole(ga_rows), _rows(proj, tr, LANE, OFF_DT // LANE), _whole(dt_bias_pad)],
                    [_orow(t, LANE, BF16, tr), _oacc(LANE), _oacc(LANE)])


def _conv_bwd_act(g_xs, g_b, g_c, proj, conv_w, conv_b, tr=256):
    t = proj.shape[0]
    cb = OFF_XBC // CONV_DIM
    halo = (proj, (8, CONV_DIM), lambda i: (jnp.maximum(i * (tr // 8) - 1, 0), cb))
    nb, nc = SSM_INNER, SSM_INNER + SSM_GROUPS * SSM_STATE

    def body(gxs_ref, gb_ref, gc_ref, cur_ref, prev_ref, w_ref, b_ref, o_ref, gcb_ref, gw0, gw1, gw2, gw3, ext):
        ext[pl.ds(0, 8), :] = jnp.where(pl.program_id(0) > 0, prev_ref[...], 0.0)
        ext[pl.ds(8, tr), :] = cur_ref[...]
        acc = b_ref[...] + w_ref[3:4, :] * cur_ref[...]
        for k in range(SSM_CONV - 1):
            acc = acc + w_ref[k:k + 1, :] * ext[pl.ds(8 - 3 + k, tr), :]
        sg = _sigmoid(acc)
        dsilu = sg * (1.0 + acc * (1.0 - sg))
        o_ref[:, :nb] = gxs_ref[...] * dsilu[:, :nb]
        o_ref[:, nb:nc] = gb_ref[...] * dsilu[:, nb:nc]
        o_ref[:, nc:] = gc_ref[...] * dsilu[:, nc:]
        gxc = o_ref[...]
        _accumulate(gcb_ref, _colsum(gxc))
        for k, gw in enumerate((gw0, gw1, gw2, gw3)):
            _accumulate(gw, _colsum(gxc * ext[pl.ds(8 - 3 + k, tr), :]))

    return _rowcall(body, "conv_bwd_act", t, tr,
                    [_rows(g_xs, tr), _rows(g_b, tr), _rows(g_c, tr), _rows(proj, tr, CONV_DIM, cb), halo, _whole(conv_w), _whole(conv_b)],
                    [_orow(t, CONV_DIM, F32, tr)] + [_oacc(CONV_DIM)] * 5, scratch=[pltpu.VMEM((tr + 8, CONV_DIM), F32)])


def _conv_bwd_in(g_xc, conv_w, tr=256):
    t = g_xc.shape[0]
    n_blk = t // tr
    halo = (g_xc, (8, CONV_DIM), lambda i: (jnp.minimum((i + 1) * (tr // 8), t // 8 - 1), 0))

    def body(cur_ref, nxt_ref, w_ref, o_ref, ext):
        ext[pl.ds(0, tr), :] = cur_ref[...]
        ext[pl.ds(tr, 8), :] = jnp.where(pl.program_id(0) < n_blk - 1, nxt_ref[...], 0.0)
        acc = w_ref[3:4, :] * cur_ref[...]
        for k in range(SSM_CONV - 1):
            acc = acc + w_ref[k:k + 1, :] * ext[pl.ds(3 - k, tr), :]
        o_ref[...] = acc.astype(BF16)

    return _rowcall(body, "conv_bwd_in", t, tr, [_rows(g_xc, tr), halo, _whole(conv_w)], [_orow(t, CONV_DIM, BF16, tr)],
                    scratch=[pltpu.VMEM((tr + 8, CONV_DIM), F32)])[0]


def _pre_norm_bwd(g_h1, g_u, x, w_pre, tr=512):
    t = x.shape[0]

    def body(gh_ref, gu_ref, x_ref, w_ref, gx_ref, gw_ref):
        gx, gw = _rms_bwd(gu_ref[...], x_ref[...], w_ref[...])
        gx_ref[...] = gh_ref[...] + gx
        _accumulate(gw_ref, gw)

    return _rowcall(body, "pre_norm_bwd", t, tr, [_rows(g_h1, tr), _rows(g_u, tr), _rows(x, tr), _whole(w_pre)],
                    [_orow(t, D_MODEL, F32, tr), _oacc(D_MODEL)])


def _alibi_slopes(n):
    def pow2(m):
        start = 2.0 ** (-8.0 / m)
        return [start ** (i + 1) for i in range(m)]
    if (n & (n - 1)) == 0:
        s = pow2(n)
    else:
        c = 2 ** int(math.floor(math.log2(n)))
        s = pow2(c) + pow2(2 * c)[0::2][: n - c]
    return np.array(s, dtype=np.float32)


def _slope_rows():
    s = _alibi_slopes(N_ATT_HEADS).reshape(N_ATT_HEADS // 2, 2)
    return jnp.asarray(np.broadcast_to(np.repeat(s, HEAD_DIM, axis=1)[:, None, :], (N_ATT_HEADS // 2, 8, LANE)).copy())


ATT_MAX_BLOCK_ROWS = 2048


RESIDUE_MAJOR_FROM = 16


class _AttLayout:
    def __init__(self, t, dil):
        self.t, self.dil = t, dil
        self.rows = t // dil
        self.residue_major = dil >= RESIDUE_MAJOR_FROM
        if self.residue_major:
            bq, self.stride = min(512, self.rows), 1
        else:
            bq, self.stride = min(512, self.rows, ATT_MAX_BLOCK_ROWS // dil), dil
        self.nsub = bq // ATT_BLOCK
        self.nblk = self.rows // bq
        self.rb = bq * self.stride
        self.pb = ATT_BLOCK * self.stride
        self.n_pb = self.rows * self.stride // self.pb

    def qkv(self, proj):
        if self.residue_major:
            qkv = proj[:, OFF_QKV:OFF_QKV + 3 * ATT_WIDTH]
            return qkv.reshape(self.rows, self.dil * 3 * ATT_WIDTH), 3 * ATT_WIDTH // LANE, 0
        return proj, 0, OFF_QKV // LANE

    def act(self, a):
        return a.reshape(self.rows, self.dil * ATT_WIDTH) if self.residue_major else a

    def act_shape(self):
        return (self.rows, self.dil * ATT_WIDTH) if self.residue_major else (self.t, ATT_WIDTH)

    def col(self, r, band, c):
        return r * band + c if self.residue_major else c


def _residue_rows(r, stride, first_block, n_blocks=1):
    if stride == 1:
        return pl.ds(first_block * ATT_BLOCK, n_blocks * ATT_BLOCK)
    return pl.ds(r + first_block * ATT_BLOCK * stride, n_blocks * ATT_BLOCK, stride=stride)


def _lane_half():
    return lax.broadcasted_iota(jnp.int32, (ATT_BLOCK, LANE), 1) // HEAD_DIM


def _att_scores_mask(dil, first):
    iq = lax.broadcasted_iota(jnp.int32, (ATT_BLOCK, 2 * ATT_BLOCK), 0)
    jk = lax.broadcasted_iota(jnp.int32, (ATT_BLOCK, 2 * ATT_BLOCK), 1)
    dist = ATT_BLOCK + iq - jk
    valid = (dist >= 0) & (dist <= ATT_BLOCK) & (jnp.logical_not(first) | (jk >= ATT_BLOCK))
    return (dist * dil).astype(F32), valid


def _att_fwd(proj, dil, slopes):
    t = proj.shape[0]
    lay = _AttLayout(t, dil)
    nsub, nblk, rb, pb = lay.nsub, lay.nblk, lay.rb, lay.pb
    src, band, qb = lay.qkv(proj)
    aw = ATT_WIDTH // LANE

    def spec(off, prev=False):
        if prev:
            return pl.BlockSpec((pb, LANE), lambda hp, i, r: (jnp.maximum(i * nsub - 1, 0), lay.col(r, band, qb + off + hp)))
        return pl.BlockSpec((rb, LANE), lambda hp, i, r: (i, lay.col(r, band, qb + off + hp)))

    o_spec = pl.BlockSpec((rb, LANE), lambda hp, i, r: (i, lay.col(r, aw, hp)))

    def body(q_ref, kc_ref, kp_ref, vc_ref, vp_ref, sl_ref, o_ref, lse_ref):
        i, r = pl.program_id(1), pl.program_id(2)
        half = _lane_half()
        for sub in range(nsub):
            rs = _residue_rows(r, lay.stride,sub)
            q = (q_ref[rs, :] * (HEAD_DIM ** -0.5)).astype(BF16)
            if sub == 0:
                r0 = _residue_rows(r, lay.stride,0)
                kk = jnp.concatenate([kp_ref[r0, :], kc_ref[rs, :]], axis=0).astype(BF16)
                vv = jnp.concatenate([vp_ref[r0, :], vc_ref[rs, :]], axis=0).astype(BF16)
                first = i == 0
            else:
                ks = _residue_rows(r, lay.stride,sub - 1, 2)
                kk, vv = kc_ref[ks, :].astype(BF16), vc_ref[ks, :].astype(BF16)
                first = jnp.bool_(False)
            dist, valid = _att_scores_mask(dil, first)
            outs, lses = [], []
            for e in range(2):
                qe = jnp.where(half == e, q, jnp.zeros_like(q))
                s = lax.dot_general(qe, kk, NT_DIMS, preferred_element_type=F32)
                s = jnp.where(valid, s - sl_ref[0:1, e * HEAD_DIM:e * HEAD_DIM + 1] * dist, NEG)
                m = jnp.max(s, axis=-1, keepdims=True)
                p = jnp.exp(s - m)
                l = jnp.sum(p, axis=-1, keepdims=True)
                outs.append(jnp.dot(p.astype(BF16), vv, preferred_element_type=F32) / l)
                lses.append(m + jnp.log(l))
            o_ref[rs, :] = jnp.where(half == 0, outs[0], outs[1])
            lse_ref[rs, :] = jnp.where(half == 0, lses[0], lses[1])

    o, lse = pl.pallas_call(
        body, grid=(N_ATT_HEADS // 2, nblk, dil),
        in_specs=[spec(0), spec(6), spec(6, True), spec(12), spec(12, True), pl.BlockSpec((None, 8, LANE), lambda hp, i, r: (hp, 0, 0))],
        out_specs=[o_spec, o_spec], out_shape=[S(lay.act_shape(), F32)] * 2,
        name=f"att_fwd_d{dil}", compiler_params=_params(("parallel", "parallel", "arbitrary")),
    )(src, src, src, src, src, slopes)
    return o.reshape(t, ATT_WIDTH), lse.reshape(t, ATT_WIDTH)


def _att_combine(outs, lses, tr=512):
    t = outs[0].shape[0]

    def body(o0, o1, o2, l0, l1, l2, att_ref, lse_ref):
        ls = [l0[...], l1[...], l2[...]]
        m = jnp.maximum(jnp.maximum(ls[0], ls[1]), ls[2])
        ws = [jnp.exp(l - m) for l in ls]
        tot = ws[0] + ws[1] + ws[2]
        num = ws[0] * o0[...].astype(F32) + ws[1] * o1[...].astype(F32) + ws[2] * o2[...].astype(F32)
        att_ref[...] = (num / tot).astype(BF16)
        lse_ref[...] = m + jnp.log(tot)

    return _rowcall(body, "att_combine", t, tr, [_rows(a, tr) for a in list(outs) + list(lses)],
                    [_orow(t, ATT_WIDTH, BF16, tr), _orow(t, ATT_WIDTH, F32, tr)])


def _att_delta(g_att, att, tr=512):
    t = att.shape[0]

    def body(g_ref, a_ref, o_ref):
        prod = g_ref[...] * a_ref[...].astype(F32)
        o_ref[...] = jnp.dot(prod, _block_ones(ATT_WIDTH, HEAD_DIM), precision=HI, preferred_element_type=F32)

    return _rowcall(body, "att_delta", t, tr, [_rows(g_att, tr), _rows(att, tr)], [_orow(t, ATT_WIDTH, F32, tr)])[0]


def _att_bwd_q(proj, g_att, lse, delta, dil, slopes):
    t = proj.shape[0]
    lay = _AttLayout(t, dil)
    nsub, nblk, rb, pb = lay.nsub, lay.nblk, lay.rb, lay.pb
    src, band, qb = lay.qkv(proj)
    aw = ATT_WIDTH // LANE

    def spec(off, prev=False):
        if prev:
            return pl.BlockSpec((pb, LANE), lambda hp, i, r: (jnp.maximum(i * nsub - 1, 0), lay.col(r, band, qb + off + hp)))
        return pl.BlockSpec((rb, LANE), lambda hp, i, r: (i, lay.col(r, band, qb + off + hp)))

    o_spec = pl.BlockSpec((rb, LANE), lambda hp, i, r: (i, lay.col(r, aw, hp)))

    def body(q_ref, kc_ref, kp_ref, vc_ref, vp_ref, do_ref, lse_ref, dl_ref, sl_ref, dq_ref):
        i, r = pl.program_id(1), pl.program_id(2)
        half = _lane_half()
        for sub in range(nsub):
            rs = _residue_rows(r, lay.stride,sub)
            q = (q_ref[rs, :] * (HEAD_DIM ** -0.5)).astype(BF16)
            do = do_ref[rs, :].astype(BF16)
            lse_q, dl_q = lse_ref[rs, :], dl_ref[rs, :]
            if sub == 0:
                r0 = _residue_rows(r, lay.stride,0)
                kk = jnp.concatenate([kp_ref[r0, :], kc_ref[rs, :]], axis=0).astype(BF16)
                vv = jnp.concatenate([vp_ref[r0, :], vc_ref[rs, :]], axis=0).astype(BF16)
                first = i == 0
            else:
                ks = _residue_rows(r, lay.stride,sub - 1, 2)
                kk, vv = kc_ref[ks, :].astype(BF16), vc_ref[ks, :].astype(BF16)
                first = jnp.bool_(False)
            dist, valid = _att_scores_mask(dil, first)
            dqs = []
            for e in range(2):
                c = e * HEAD_DIM
                qe = jnp.where(half == e, q, jnp.zeros_like(q))
                doe = jnp.where(half == e, do, jnp.zeros_like(do))
                s = lax.dot_general(qe, kk, NT_DIMS, preferred_element_type=F32)
                s = jnp.where(valid, s - sl_ref[0:1, c:c + 1] * dist, NEG)
                p = jnp.exp(s - lse_q[:, c:c + 1])
                dp = lax.dot_general(doe, vv, NT_DIMS, preferred_element_type=F32)
                ds = p * (dp - dl_q[:, c:c + 1])
                dqs.append(jnp.dot(ds.astype(BF16), kk, preferred_element_type=F32))
            dq_ref[rs, :] = jnp.where(half == 0, dqs[0], dqs[1]) * (HEAD_DIM ** -0.5)

    return pl.pallas_call(
        body, grid=(N_ATT_HEADS // 2, nblk, dil),
        in_specs=[spec(0), spec(6), spec(6, True), spec(12), spec(12, True), o_spec, o_spec, o_spec,
                  pl.BlockSpec((None, 8, LANE), lambda hp, i, r: (hp, 0, 0))],
        out_specs=o_spec, out_shape=S(lay.act_shape(), F32),
        name=f"att_bwd_q_d{dil}", compiler_params=_params(("parallel", "parallel", "arbitrary")),
    )(src, src, src, src, src, lay.act(g_att), lay.act(lse), lay.act(delta), slopes).reshape(t, ATT_WIDTH)


def _att_bwd_kv(proj, g_att, lse, delta, dil, slopes):
    t = proj.shape[0]
    lay = _AttLayout(t, dil)
    nsub, nblk, rb, pb, n_pb = lay.nsub, lay.nblk, lay.rb, lay.pb, lay.n_pb
    src, band, qb = lay.qkv(proj)
    aw = ATT_WIDTH // LANE

    def pspec(off, nxt=False):
        if nxt:
            return pl.BlockSpec((pb, LANE), lambda hp, i, r: (jnp.minimum((i + 1) * nsub, n_pb - 1), lay.col(r, band, qb + off + hp)))
        return pl.BlockSpec((rb, LANE), lambda hp, i, r: (i, lay.col(r, band, qb + off + hp)))

    def aspec(nxt=False):
        if nxt:
            return pl.BlockSpec((pb, LANE), lambda hp, i, r: (jnp.minimum((i + 1) * nsub, n_pb - 1), lay.col(r, aw, hp)))
        return pl.BlockSpec((rb, LANE), lambda hp, i, r: (i, lay.col(r, aw, hp)))

    def body(qc_ref, qn_ref, k_ref, v_ref, doc_ref, don_ref, lsec_ref, lsen_ref, dlc_ref, dln_ref, sl_ref, dk_ref, dv_ref):
        i, r = pl.program_id(1), pl.program_id(2)
        half = _lane_half()
        jk = lax.broadcasted_iota(jnp.int32, (ATT_BLOCK, 2 * ATT_BLOCK), 0)
        cq = lax.broadcasted_iota(jnp.int32, (ATT_BLOCK, 2 * ATT_BLOCK), 1)
        dist_i = cq - jk
        dist = (dist_i * dil).astype(F32)
        for sub in range(nsub):
            rs = _residue_rows(r, lay.stride,sub)

            def both(cur_ref, nxt_ref):
                if sub < nsub - 1:
                    return cur_ref[_residue_rows(r, lay.stride,sub, 2), :]
                return jnp.concatenate([cur_ref[rs, :], nxt_ref[_residue_rows(r, lay.stride,0), :]], axis=0)

            has_next = jnp.bool_(True) if sub < nsub - 1 else i < nblk - 1
            valid = (dist_i >= 0) & (dist_i <= ATT_BLOCK) & (has_next | (cq < ATT_BLOCK))
            qq = (both(qc_ref, qn_ref) * (HEAD_DIM ** -0.5)).astype(BF16)
            doo = both(doc_ref, don_ref).astype(BF16)
            lse_t = both(lsec_ref, lsen_ref).T
            dl_t = both(dlc_ref, dln_ref).T
            k = k_ref[rs, :].astype(BF16)
            v = v_ref[rs, :].astype(BF16)
            dks, dvs = [], []
            for e in range(2):
                c = e * HEAD_DIM
                ke = jnp.where(half == e, k, jnp.zeros_like(k))
                ve = jnp.where(half == e, v, jnp.zeros_like(v))
                st = lax.dot_general(ke, qq, NT_DIMS, preferred_element_type=F32)
                st = jnp.where(valid, st - sl_ref[0:1, c:c + 1] * dist, NEG)
                pt = jnp.exp(st - lse_t[c:c + 1, :])
                dpt = lax.dot_general(ve, doo, NT_DIMS, preferred_element_type=F32)
                dst = pt * (dpt - dl_t[c:c + 1, :])
                dks.append(jnp.dot(dst.astype(BF16), qq, preferred_element_type=F32))
                dvs.append(jnp.dot(pt.astype(BF16), doo, preferred_element_type=F32))
            dk_ref[rs, :] = jnp.where(half == 0, dks[0], dks[1])
            dv_ref[rs, :] = jnp.where(half == 0, dvs[0], dvs[1])

    gv, lv, dlv = lay.act(g_att), lay.act(lse), lay.act(delta)
    dk, dv = pl.pallas_call(
        body, grid=(N_ATT_HEADS // 2, nblk, dil),
        in_specs=[pspec(0), pspec(0, True), pspec(6), pspec(12), aspec(), aspec(True), aspec(), aspec(True), aspec(), aspec(True),
                  pl.BlockSpec((None, 8, LANE), lambda hp, i, r: (hp, 0, 0))],
        out_specs=[aspec(), aspec()], out_shape=[S(lay.act_shape(), F32)] * 2,
        name=f"att_bwd_kv_d{dil}", compiler_params=_params(("parallel", "parallel", "arbitrary")),
    )(src, src, src, src, gv, gv, lv, lv, dlv, dlv, slopes)
    return dk.reshape(t, ATT_WIDTH), dv.reshape(t, ATT_WIDTH)


def _att_grad_sum(dqs, dks, dvs, tr=512):
    t = dqs[0].shape[0]

    def body(*refs):
        o_ref = refs[-1]
        for n in range(3):
            tot = refs[3 * n][...] + refs[3 * n + 1][...] + refs[3 * n + 2][...]
            o_ref[:, n * ATT_WIDTH:(n + 1) * ATT_WIDTH] = tot.astype(BF16)

    return _rowcall(body, "att_grad_sum", t, tr, [_rows(a, tr) for a in list(dqs) + list(dks) + list(dvs)],
                    [_orow(t, 3 * ATT_WIDTH, BF16, tr)])[0]


def _ssd_common(xs, dtx, cs, cs_t):
    ch = SSM_CHUNK
    row = lax.broadcasted_iota(jnp.int32, (ch, ch), 0)
    col = lax.broadcasted_iota(jnp.int32, (ch, ch), 1)
    cs_last = cs[ch - 1:ch, :]
    return dict(tril=col <= row, row=row, col=col, cs=cs, cs_t=cs_t, cs_last=cs_last,
                e=jnp.exp(cs), w=jnp.exp(cs_last - cs), xd=xs * dtx)


def _dot_split(a, b, split):
    ops = [a, b]
    x = ops[split]
    hi = x.astype(BF16)
    lo = (x - hi.astype(F32)).astype(BF16)
    other = ops[1 - split].astype(BF16)
    if split == 1:
        return jnp.dot(other, hi, preferred_element_type=F32) + jnp.dot(other, lo, preferred_element_type=F32)
    return jnp.dot(hi, other, preferred_element_type=F32) + jnp.dot(lo, other, preferred_element_type=F32)


def _decay_col(cs_t, heads_per_group):
    r = lax.broadcasted_iota(jnp.int32, (heads_per_group * SSM_HEAD_DIM, SSM_STATE), 0) // SSM_HEAD_DIM
    out = jnp.zeros((heads_per_group * SSM_HEAD_DIM, SSM_STATE), F32)
    for j in range(heads_per_group):
        out = jnp.where(r == j, jnp.exp(cs_t[j:j + 1, SSM_CHUNK - 1:SSM_CHUNK]), out)
    return out


def _ssd_specs(t):
    hg = SSM_HEADS // SSM_GROUPS
    gw = hg * SSM_HEAD_DIM
    nb0 = SSM_INNER // SSM_STATE
    return hg, gw, nb0


def _ssd_fwd(xa, dtx, csx, cst_g):
    t = xa.shape[0]
    nch = t // SSM_CHUNK
    hg, gw, nb0 = _ssd_specs(t)
    ch = SSM_CHUNK

    def body(xs_ref, b_ref, c_ref, dtx_ref, cs_ref, cst_ref, y_ref, st_ref, h_scr):
        cc, g = pl.program_id(0), pl.program_id(1)

        @pl.when(cc == 0)
        def _():
            h_scr[g] = jnp.zeros((gw, SSM_STATE), F32)

        q = _ssd_common(xs_ref[...], dtx_ref[...], cs_ref[...], cst_ref[...])
        bb, cb = b_ref[...].astype(BF16), c_ref[...].astype(BF16)
        cbm = lax.dot_general(cb, bb, NT_DIMS, preferred_element_type=F32)
        h = h_scr[g]
        st_ref[...] = h
        xd16 = q["xd"].astype(BF16)
        y = lax.dot_general(cb, h.astype(BF16), NT_DIMS, preferred_element_type=F32) * q["e"]
        lane_head = lax.broadcasted_iota(jnp.int32, (ch, gw), 1) // SSM_HEAD_DIM
        for j in range(hg):
            diff = q["cs"][:, j * SSM_HEAD_DIM:j * SSM_HEAD_DIM + 1] - q["cs_t"][j:j + 1, :]
            gmat = cbm * jnp.exp(jnp.where(q["tril"], diff, NEG))
            yj = jnp.dot(gmat.astype(BF16), xd16, preferred_element_type=F32)
            y = y + jnp.where(lane_head == j, yj, 0.0)
        y_ref[...] = y
        s_new = lax.dot_general((q["xd"] * q["w"]).astype(BF16), bb, TN_DIMS, preferred_element_type=F32)
        h_scr[g] = _decay_col(q["cs_t"], hg) * h + s_new

    return pl.pallas_call(
        body, grid=(nch, SSM_GROUPS),
        in_specs=[pl.BlockSpec((ch, gw), lambda cc, g: (cc, g)),
                  pl.BlockSpec((ch, SSM_STATE), lambda cc, g: (cc, nb0 + g)),
                  pl.BlockSpec((ch, SSM_STATE), lambda cc, g: (cc, nb0 + SSM_GROUPS + g)),
                  pl.BlockSpec((ch, gw), lambda cc, g: (cc, g)),
                  pl.BlockSpec((ch, gw), lambda cc, g: (cc, g)),
                  pl.BlockSpec((None, 8, ch), lambda cc, g: (g, 0, cc))],
        out_specs=[pl.BlockSpec((ch, gw), lambda cc, g: (cc, g)),
                   pl.BlockSpec((None, None, gw, SSM_STATE), lambda cc, g: (cc, g, 0, 0))],
        out_shape=[S((t, SSM_INNER), F32), S((nch, SSM_GROUPS, gw, SSM_STATE), F32)],
        scratch_shapes=[pltpu.VMEM((SSM_GROUPS, gw, SSM_STATE), F32)],
        name="ssd_fwd", compiler_params=_params(("arbitrary", "arbitrary")),
    )(xa, xa, xa, dtx, csx, cst_g)


def _ssd_bwd(xa, dtx, csx, cst_g, alog_x, g_y, states, dskip_x):
    t = xa.shape[0]
    nch = t // SSM_CHUNK
    hg, gw, nb0 = _ssd_specs(t)
    ch = SSM_CHUNK

    def rc(cc):
        return nch - 1 - cc

    def body(xs_ref, b_ref, c_ref, dtx_ref, cs_ref, cst_ref, alx_ref, gy_ref, st_ref, dsk_ref,
             gxs_ref, gb_ref, gc_ref, gdt_ref, ga_ref, gh_scr):
        cc, g = pl.program_id(0), pl.program_id(1)

        @pl.when(cc == 0)
        def _():
            gh_scr[g] = jnp.zeros((gw, SSM_STATE), F32)

        xs, dtx = xs_ref[...], dtx_ref[...]
        q = _ssd_common(xs, dtx, cs_ref[...], cst_ref[...])
        cs, cs_t, e, w, xd = q["cs"], q["cs_t"], q["e"], q["w"], q["xd"]
        bb, cb = b_ref[...].astype(BF16), c_ref[...].astype(BF16)
        gy = gy_ref[...]
        gy16, xd16 = gy.astype(BF16), xd.astype(BF16)
        h = st_ref[...]
        h16 = h.astype(BF16)
        ghn = gh_scr[g]
        ghn16 = ghn.astype(BF16)
        seg = _block_ones(gw, SSM_HEAD_DIM)
        cbm = lax.dot_general(cb, bb, NT_DIMS, preferred_element_type=F32)
        cbt = lax.dot_general(bb, cb, NT_DIMS, preferred_element_type=F32)

        gye16 = (gy * e).astype(BF16)
        chm = lax.dot_general(cb, h16, NT_DIMS, preferred_element_type=F32)
        g_c = jnp.dot(gye16, h16, preferred_element_type=F32)
        gh_off = lax.dot_general(gye16, cb, TN_DIMS, preferred_element_type=F32)
        g_e = _dot_split(gy * chm, seg, 0)
        bgs = lax.dot_general(bb, ghn16, NT_DIMS, preferred_element_type=F32)
        g_xd = w * bgs
        g_w = _dot_split(xd * bgs, seg, 0)
        g_b = jnp.dot((xd * w).astype(BF16), ghn16, preferred_element_type=F32)
        decay = _decay_col(cs_t, hg)
        gh_scr[g] = decay * ghn + gh_off
        rsum = jnp.sum(ghn * h, axis=1, keepdims=True)
        lane_head = lax.broadcasted_iota(jnp.int32, (ch, gw), 1) // SSM_HEAD_DIM
        lane_head1 = lax.broadcasted_iota(jnp.int32, (1, gw), 1) // SSM_HEAD_DIM
        g_el = jnp.zeros((1, gw), F32)
        g_cs = g_e * e - g_w * w
        upper = q["row"] <= q["col"]
        for j in range(hg):
            g_el = jnp.where(lane_head1 == j, jnp.sum(rsum[j * SSM_HEAD_DIM:(j + 1) * SSM_HEAD_DIM, :], axis=0, keepdims=True), g_el)
            csc = cs[:, j * SSM_HEAD_DIM:j * SSM_HEAD_DIM + 1]
            csr = cs_t[j:j + 1, :]
            lm = jnp.exp(jnp.where(q["tril"], csc - csr, NEG))
            lmt = jnp.exp(jnp.where(upper, csr - csc, NEG))
            gyj = jnp.where(lane_head == j, gy16, jnp.zeros_like(gy16))
            xdj = jnp.where(lane_head == j, xd16, jnp.zeros_like(xd16))
            gg = lax.dot_general(gyj, xd16, NT_DIMS, preferred_element_type=F32)
            ggt = lax.dot_general(xdj, gy16, NT_DIMS, preferred_element_type=F32)
            gcb, gcbt = gg * lm, ggt * lmt
            g_c = g_c + jnp.dot(gcb.astype(BF16), bb, preferred_element_type=F32)
            g_b = g_b + jnp.dot(gcbt.astype(BF16), cb, preferred_element_type=F32)
            gxdj = jnp.dot((cbt * lmt).astype(BF16), gy16, preferred_element_type=F32)
            g_xd = g_xd + jnp.where(lane_head == j, gxdj, 0.0)
            d_cs = jnp.sum(gcb * cbm, axis=1, keepdims=True) - jnp.sum(gcbt * cbt, axis=1, keepdims=True)
            g_cs = g_cs + jnp.where(lane_head == j, d_cs, 0.0)
        extra = _colsum(g_w * w) + g_el * jnp.exp(q["cs_last"])
        g_cs = g_cs + jnp.where(lax.broadcasted_iota(jnp.int32, (ch, gw), 0) == ch - 1, extra, 0.0)
        g_la = _dot_split(upper, g_cs, 1)
        a_x = -jnp.exp(alx_ref[...])
        gdt_ref[...] = g_xd * xs + g_la * a_x * (1.0 / SSM_HEAD_DIM)
        ga_row = _colsum(g_la * (dtx * a_x)) * (1.0 / SSM_HEAD_DIM)
        ga_ref[...] = jnp.where(lax.broadcasted_iota(jnp.int32, (8, gw), 0) == 0, ga_row, 0.0)
        gxs_ref[...] = g_xd * dtx + gy * dsk_ref[...]
        gb_ref[...] = g_b
        gc_ref[...] = g_c

    return pl.pallas_call(
        body, grid=(nch, SSM_GROUPS),
        in_specs=[pl.BlockSpec((ch, gw), lambda cc, g: (rc(cc), g)),
                  pl.BlockSpec((ch, SSM_STATE), lambda cc, g: (rc(cc), nb0 + g)),
                  pl.BlockSpec((ch, SSM_STATE), lambda cc, g: (rc(cc), nb0 + SSM_GROUPS + g)),
                  pl.BlockSpec((ch, gw), lambda cc, g: (rc(cc), g)),
                  pl.BlockSpec((ch, gw), lambda cc, g: (rc(cc), g)),
                  pl.BlockSpec((None, 8, ch), lambda cc, g: (g, 0, rc(cc))),
                  pl.BlockSpec((1, gw), lambda cc, g: (0, g)),
                  pl.BlockSpec((ch, gw), lambda cc, g: (rc(cc), g)),
                  pl.BlockSpec((None, None, gw, SSM_STATE), lambda cc, g: (rc(cc), g, 0, 0)),
                  pl.BlockSpec((1, gw), lambda cc, g: (0, g))],
        out_specs=[pl.BlockSpec((ch, gw), lambda cc, g: (rc(cc), g)),
                   pl.BlockSpec((ch, SSM_STATE), lambda cc, g: (rc(cc), g)),
                   pl.BlockSpec((ch, SSM_STATE), lambda cc, g: (rc(cc), g)),
                   pl.BlockSpec((ch, gw), lambda cc, g: (rc(cc), g)),
                   pl.BlockSpec((8, gw), lambda cc, g: (rc(cc), g))],
        out_shape=[S((t, SSM_INNER), F32), S((t, SSM_GROUPS * SSM_STATE), F32), S((t, SSM_GROUPS * SSM_STATE), F32),
                   S((t, SSM_INNER), F32), S((nch * 8, SSM_INNER), F32)],
        scratch_shapes=[pltpu.VMEM((SSM_GROUPS, gw, SSM_STATE), F32)],
        name="ssd_bwd", compiler_params=_params(("arbitrary", "arbitrary")),
    )(xa, xa, xa, dtx, csx, cst_g, alog_x, g_y, states, dskip_x)


def _local_step(x, target, w_pre, w_in_r, b_gate, conv_w, conv_b, dt_bias, a_log, d_skip, ssm_norm_w,
                late_weights, w_post, w_fpre, w_fpost, on_mid_grads, on_in_proj_grads):
    t = x.shape[0]
    mm = functools.partial(_matmul, tm=512)
    slopes = _slope_rows()
    hg = SSM_HEADS // SSM_GROUPS
    dt_bias_pad = jnp.pad(dt_bias, ((0, 0), (0, LANE - SSM_HEADS)))
    alog_x = jnp.repeat(a_log, SSM_HEAD_DIM, axis=1)
    alog_pad = jnp.pad(a_log, ((0, 0), (0, LANE - SSM_HEADS)))
    dskip_x = jnp.repeat(d_skip, SSM_HEAD_DIM, axis=1)

    u = _pre_norm(x, w_pre)
    proj = mm(u, w_in_r, mode="nn", out_dtype=F32, name="in_proj", tn=1792, tk=D_MODEL)
    fwd = [_att_fwd(proj, dil, slopes) for _, dil in DILATED_PATTERNS]
    att, lse = _att_combine([o for o, _ in fwd], [l for _, l in fwd])
    xa = _conv_fwd(proj, conv_w, conv_b)
    dtx, csx, cst = _dt_fwd(proj, dt_bias_pad, alog_pad)
    cst_g = jnp.pad(cst[:SSM_HEADS].reshape(SSM_GROUPS, hg, t), ((0, 0), (0, 8 - hg), (0, 0)))
    y_ssd, states = _ssd_fwd(xa, dtx, csx, cst_g)
    y4 = _gate_norm_fwd(y_ssd, xa, proj, dskip_x, ssm_norm_w)
    w_att, w_ssm, w_out, w_up, w_down = late_weights(y4)
    att_p = mm(att, w_att, mode="nn", out_dtype=F32, name="att_proj", tn=D_MODEL, tk=ATT_WIDTH)
    ssm_p = mm(y4, w_ssm, mode="nn", out_dtype=F32, name="ssm_proj", tn=D_MODEL, tk=SSM_INNER)
    mixin = _gating_fwd(proj, b_gate, att_p, ssm_p)
    mixed = mm(mixin, w_out, mode="nn", out_dtype=F32, name="out_proj", tn=D_MODEL, tk=D_MODEL)
    h1, f = _mix_post_ffn_pre(x, mixed, w_post, w_fpre)
    act, up = _matmul(f, w_up, mode="nn", out_dtype=BF16, name="ffn_up", tm=2048, tn=FFN_HIDDEN // N_DEV, tk=D_MODEL, epilogue="relu2", stacked=True)
    dn = mm(act, w_down, mode="nn", out_dtype=F32, name="ffn_down", tn=D_MODEL, tk=FFN_HIDDEN)
    loss, g_h2, g_dn, gw_fpost = _loss_and_ffn_post_bwd(h1, dn, w_fpost, target)

    g_up = mm(g_dn, w_down, mode="nt", out_dtype=BF16, name="ffn_down_bwd_x", tn=2048, tk=D_MODEL, epilogue="relu2_bwd", extra=up)
    gw_down = _matmul(act, g_dn, mode="tn", out_dtype=BF16, name="ffn_down_bwd_w", tm=1024, tn=D_MODEL, tk=512)
    g_f = _matmul(g_up, w_up, mode="nt", out_dtype=F32, name="ffn_up_bwd_x", tm=2048, tn=D_MODEL, tk=FFN_HIDDEN // N_DEV, stacked=True)
    gw_up = _matmul(f, g_up, mode="tn", out_dtype=BF16, name="ffn_up_bwd_w", tm=D_MODEL, tn=FFN_HIDDEN // N_DEV, tk=2048, stacked=True)
    g_h1, g_mixed, gw_fpre, gw_post = _ffn_pre_mix_post_bwd(g_h2, g_f, h1, w_fpre, mixed, w_post)
    g_mixin = mm(g_mixed, w_out, mode="nt", out_dtype=F32, name="out_proj_bwd_x", tn=D_MODEL, tk=D_MODEL)
    gw_out = _matmul(mixin, g_mixed, mode="tn", out_dtype=BF16, name="out_proj_bwd_w", tm=D_MODEL, tn=D_MODEL, tk=512)
    g_att_p, g_ssm_p, g_gl, g_b_gate = _gating_bwd(g_mixin, proj, b_gate, att_p, ssm_p)
    g_att = mm(g_att_p, w_att, mode="nt", out_dtype=F32, name="att_proj_bwd_x", tn=ATT_WIDTH, tk=D_MODEL)
    gw_att = _matmul(att, g_att_p, mode="tn", out_dtype=BF16, name="att_proj_bwd_w", tm=ATT_WIDTH, tn=D_MODEL, tk=512)
    g_y4 = mm(g_ssm_p, w_ssm, mode="nt", out_dtype=F32, name="ssm_proj_bwd_x", tn=SSM_INNER, tk=D_MODEL)
    gw_ssm = _matmul(y4, g_ssm_p, mode="tn", out_dtype=BF16, name="ssm_proj_bwd_w", tm=1024, tn=D_MODEL, tk=512)
    token = on_mid_grads(dict(w_att_proj=gw_att, w_ssm_proj=gw_ssm, w_out=gw_out, w_up=gw_up, w_down=gw_down))
    if token is not None:
        ssm_norm_w = ssm_norm_w + jnp.tile(token[0:1, :], (1, SSM_INNER // LANE))
    g_y2, g_z, g_norm_w, _, g_d_skip = _gate_norm_bwd(g_y4, y_ssd, xa, proj, dskip_x, ssm_norm_w)
    g_xs, g_bm, g_cm, g_dtx, ga_rows = _ssd_bwd(xa, dtx, csx, cst_g, alog_x, g_y2, states, dskip_x)
    g_dt_raw, g_dt_bias, g_a_log = _dt_bwd(g_dtx, ga_rows, proj, dt_bias_pad)
    g_xc, g_conv_b, gcw0, gcw1, gcw2, gcw3 = _conv_bwd_act(g_xs, g_bm, g_cm, proj, conv_w, conv_b)
    g_xbc = _conv_bwd_in(g_xc, conv_w)
    delta = _att_delta(g_att, att)
    dqs, dks, dvs = [], [], []
    for _, dil in DILATED_PATTERNS:
        dqs.append(_att_bwd_q(proj, g_att, lse, delta, dil, slopes))
        dk, dv = _att_bwd_kv(proj, g_att, lse, delta, dil, slopes)
        dks.append(dk)
        dvs.append(dv)
    g_qkv = _att_grad_sum(dqs, dks, dvs)
    g_proj = jnp.concatenate([g_z, g_gl, g_xbc, g_qkv, g_dt_raw, jnp.zeros((t, PROJ_W - OFF_DT - LANE), BF16)], axis=1)
    gw_in_r = _matmul(u, g_proj, mode="tn", out_dtype=BF16, name="in_proj_bwd_w", tm=D_MODEL, tn=1792, tk=512)
    token = on_in_proj_grads(gw_in_r, jnp.concatenate([gcw0, gcw1, gcw2, gcw3], axis=0))
    g_u = mm(g_proj, w_in_r, mode="nt", out_dtype=F32, name="in_proj_bwd_x", tn=D_MODEL, tk=1792, after=token)
    g_x, gw_pre = _pre_norm_bwd(g_h1, g_u, x, w_pre)

    grads = dict(
        norm_mix_pre_w=gw_pre, b_gate=g_b_gate, conv_b=g_conv_b, dt_bias=g_dt_bias[:, :SSM_HEADS], a_log=g_a_log[:, :SSM_HEADS],
        d_skip=g_d_skip[:, :SSM_HEADS], ssm_norm_w=g_norm_w, norm_mix_post_w=gw_post, norm_ffn_pre_w=gw_fpre, norm_ffn_post_w=gw_fpost)
    return loss, g_x, grads


def _mesh_pos():
    return lax.axis_index("x"), lax.axis_index("y"), lax.axis_index("c")


def _all_gather(shards):
    n = len(shards)

    def body(*refs):
        x_refs, o_refs = refs[:n], refs[n:2 * n]
        send_sems, recv_sems, local_sems = refs[2 * n:]
        x, y, c = _mesh_pos()
        me, sibling = (x, y, c), (x, y, 1 - c)
        chips = [(1 - x, y), (x, 1 - y), (1 - x, 1 - y)]

        def copy(a, k, block, to, src=None):
            dst = o_refs[a].at[4 * block[0] + 2 * block[1] + block[2]]
            return pltpu.make_async_remote_copy(
                src_ref=dst if src is None else src, dst_ref=dst, send_sem=send_sems.at[7 * a + k], recv_sem=recv_sems.at[7 * a + k],
                device_id=to, device_id_type=pl.DeviceIdType.MESH)

        mine = [pltpu.make_async_copy(x_refs[a], o_refs[a].at[4 * x + 2 * y + c], local_sems.at[a]) for a in range(n)]
        for cp in mine:
            cp.start()
        first = []
        for a in range(n):
            first.append(copy(a, 0, me, sibling, src=x_refs[a]))
            first += [copy(a, 1 + j, me, (*chip, c), src=x_refs[a]) for j, chip in enumerate(chips)]
        for cp in first:
            cp.start()
        passed = []
        for j, chip in enumerate(chips):
            for a in range(n):
                copy(a, 1 + j, (*chip, c), me).wait_recv()
                passed.append(copy(a, 4 + j, (*chip, c), sibling))
                passed[-1].start()
        for a in range(n):
            copy(a, 0, sibling, me).wait_recv()
            for j, chip in enumerate(chips):
                copy(a, 4 + j, (*chip, 1 - c), me).wait_recv()
        for cp in first + passed:
            cp.wait_send()
        for cp in mine:
            cp.wait()

    hbm = pl.BlockSpec(memory_space=pltpu.HBM)
    return pl.pallas_call(
        body, out_shape=[S((N_DEV,) + s.shape, s.dtype) for s in shards],
        in_specs=[hbm] * n, out_specs=[hbm] * n,
        scratch_shapes=[pltpu.SemaphoreType.DMA((7 * n,)), pltpu.SemaphoreType.DMA((7 * n,)), pltpu.SemaphoreType.DMA((n,))],
        name="weights_all_gather",
    )(*shards)


def _exchange_grads(slab_arrays, small):
    n = len(slab_arrays)
    r_small = small.shape[0]

    def body(*refs):
        slab_refs, small_ref = refs[:n], refs[n]
        recv_refs, gsm_ref = refs[n + 1:2 * n + 1], refs[2 * n + 1]
        send_sems, recv_sems, local_sems = refs[2 * n + 2:]
        x, y, c = _mesh_pos()
        me = 4 * x + 2 * y + c

        def peer(k):
            px = 1 - x if k & 4 else x
            py = 1 - y if k & 2 else y
            pc = 1 - c if k & 1 else c
            return (px, py, pc), 4 * px + 2 * py + pc

        def copy(a, k, sending):
            to, lin = peer(k)
            sem = 7 * a + k - 1
            if a == n:
                src, dst = small_ref, gsm_ref.at[me if sending else lin]
            else:
                src, dst = slab_refs[a].at[lin], recv_refs[a].at[me if sending else lin]
            return pltpu.make_async_remote_copy(src_ref=src, dst_ref=dst, send_sem=send_sems.at[sem], recv_sem=recv_sems.at[sem],
                                                device_id=to, device_id_type=pl.DeviceIdType.MESH)

        own = [pltpu.make_async_copy(slab_refs[a].at[me], recv_refs[a].at[me], local_sems.at[a]) for a in range(n)]
        own.append(pltpu.make_async_copy(small_ref, gsm_ref.at[me], local_sems.at[n]))
        for cp in own:
            cp.start()
        order = [n] + list(range(n))
        sends = [copy(a, k, True) for a in order for k in range(1, N_DEV)]
        for cp in sends:
            cp.start()
        for a in order:
            for k in range(1, N_DEV):
                copy(a, k, False).wait_recv()
        for cp in sends:
            cp.wait_send()
        for cp in own:
            cp.wait()

    hbm = pl.BlockSpec(memory_space=pltpu.HBM)
    n_sem = 7 * (n + 1)
    res = pl.pallas_call(
        body, out_shape=[S(a.shape, a.dtype) for a in slab_arrays] + [S((N_DEV, r_small, LANE), small.dtype)],
        in_specs=[hbm] * (n + 1), out_specs=[hbm] * (n + 1),
        scratch_shapes=[pltpu.SemaphoreType.DMA((n_sem,)), pltpu.SemaphoreType.DMA((n_sem,)), pltpu.SemaphoreType.DMA((n + 1,))],
        name="grad_exchange",
    )(*slab_arrays, small)
    return res[:n], res[n]


def _peer_of(k, x, y, c):
    px = 1 - x if k & 4 else x
    py = 1 - y if k & 2 else y
    pc = 1 - c if k & 1 else c
    return (px, py, pc), 4 * px + 2 * py + pc


def _split_copies(src_refs, land_refs, send_sems, recv_sems, per_peer):
    x, y, c = _mesh_pos()
    me = 4 * x + 2 * y + c
    sends, recvs = [], []
    for a, (src, land) in enumerate(zip(src_refs, land_refs)):
        for k in range(1, N_DEV):
            to, lin = _peer_of(k, x, y, c)
            sem = 7 * a + k - 1
            piece = src.at[lin] if per_peer else src
            for slot, out in ((me, sends), (lin, recvs)):
                out.append(pltpu.make_async_remote_copy(
                    src_ref=piece, dst_ref=land.at[slot], send_sem=send_sems.at[sem], recv_sem=recv_sems.at[sem],
                    device_id=to, device_id_type=pl.DeviceIdType.MESH))
    return sends, recvs


def _remote_start(srcs, per_peer, name):
    n = len(srcs)
    lands = [lax.empty((N_DEV,) + (s.shape[1:] if per_peer else s.shape), s.dtype) for s in srcs]

    def body(*refs):
        src_refs, land_refs = refs[:n], refs[n:2 * n]
        send_sems, recv_sems = refs[2 * n], refs[2 * n + 1]
        token = refs[-1]
        sends, _ = _split_copies(src_refs, land_refs, send_sems, recv_sems, per_peer)
        for cp in sends:
            cp.start()
        token[...] = jnp.zeros_like(token)

    hbm = pl.BlockSpec(memory_space=pltpu.HBM)
    sem = pl.BlockSpec(memory_space=pltpu.SEMAPHORE)
    res = pl.pallas_call(
        body, name=name,
        out_shape=(pltpu.SemaphoreType.DMA((7 * n,)), pltpu.SemaphoreType.DMA((7 * n,)),
                   *[pltpu.HBM(a.shape, a.dtype) for a in srcs + lands], S((8, LANE), F32)),
        in_specs=[hbm] * (2 * n), out_specs=(sem, sem, *[hbm] * (2 * n), pl.BlockSpec(memory_space=pltpu.VMEM)),
        input_output_aliases={i: 2 + i for i in range(2 * n)},
        compiler_params=pltpu.CompilerParams(has_side_effects=pltpu.SideEffectType.DATAFLOW_SIDE_EFFECTING),
    )(*[pltpu.with_memory_space_constraint(a, pltpu.HBM) for a in srcs + lands])
    return dict(sems=res[:2], srcs=list(res[2:2 + n]), lands=list(res[2 + n:2 + 2 * n]), per_peer=per_peer), res[-1]


def _remote_wait(handle, after, name):
    n = len(handle["srcs"])
    per_peer = handle["per_peer"]

    def body(*refs):
        src_refs, land_refs = refs[:n], refs[n:2 * n]
        send_sems, recv_sems = refs[2 * n], refs[2 * n + 1]
        sends, recvs = _split_copies(src_refs, land_refs, send_sems, recv_sems, per_peer)
        for cp in sends:
            cp.wait_send()
        for cp in recvs:
            cp.wait_recv()

    hbm = pl.BlockSpec(memory_space=pltpu.HBM)
    sem = pl.BlockSpec(memory_space=pltpu.SEMAPHORE)
    arrays = handle["srcs"] + handle["lands"]
    res = pl.pallas_call(
        body, name=name, out_shape=tuple(pltpu.HBM(a.shape, a.dtype) for a in arrays),
        in_specs=[hbm] * (2 * n) + [sem, sem, pl.BlockSpec(memory_space=pl.ANY)], out_specs=tuple([hbm] * (2 * n)),
        input_output_aliases={i: i for i in range(2 * n)},
        compiler_params=pltpu.CompilerParams(has_side_effects=pltpu.SideEffectType.DATAFLOW_SIDE_EFFECTING),
    )(*arrays, *handle["sems"], after)
    return list(res[n:])


def _with_own(lands, own, me):
    return [lax.dynamic_update_index_in_dim(land, o.astype(land.dtype), me, 0) for land, o in zip(lands, own)]


def _adamw(w, m, v, slabs, name, tr):
    r, cols = w.shape
    c1 = 1.0 - ADAM_B1 ** ADAM_STEP
    c2 = 1.0 - ADAM_B2 ** ADAM_STEP

    def body(w_ref, m_ref, v_ref, s_ref, g_ref, d_ref, nm_ref, nv_ref):
        g = s_ref[0].astype(F32)
        for d in range(1, N_DEV):
            g = g + s_ref[d].astype(F32)
        nm = ADAM_B1 * m_ref[...] + (1.0 - ADAM_B1) * g
        nv = ADAM_B2 * v_ref[...] + (1.0 - ADAM_B2) * (g * g)
        g_ref[...] = g
        nm_ref[...] = nm
        nv_ref[...] = nv
        d_ref[...] = -ADAM_LR * ((nm / c1) / (jnp.sqrt(nv / c2) + ADAM_EPS) + ADAM_WD * w_ref[...])

    assert r % tr == 0, name
    blk = pl.BlockSpec((tr, cols), lambda i: (i, 0))
    return pl.pallas_call(
        body, grid=(r // tr,), in_specs=[blk, blk, blk, pl.BlockSpec((N_DEV, tr, cols), lambda i: (0, i, 0))],
        out_specs=[blk] * 4, out_shape=[S((r, cols), F32)] * 4, name=name, compiler_params=_params(("parallel",)),
    )(w, m, v, slabs)


BIG = ("w_in", "w_att_proj", "w_up", "w_ssm_proj", "w_out", "w_down", "conv_w")
ADAMW_ROWS = dict(w_in=256, w_att_proj=768, w_up=512, w_ssm_proj=256, w_out=128, w_down=256, conv_w=4)
SMALL = ("norm_mix_pre_w", "b_gate", "conv_b", "dt_bias", "a_log", "d_skip", "ssm_norm_w", "norm_mix_post_w",
         "norm_ffn_pre_w", "norm_ffn_post_w")
ORDER = ("norm_mix_pre_w", "w_in", "b_gate", "conv_w", "conv_b", "dt_bias", "a_log", "d_skip", "ssm_norm_w", "w_att_proj",
         "w_ssm_proj", "w_out", "norm_mix_post_w", "norm_ffn_pre_w", "w_up", "w_down", "norm_ffn_post_w")
ROW_SHARDED = ("w_ssm_proj", "w_out", "w_down")
LATE = ("w_att_proj", "w_ssm_proj", "w_out", "w_up", "w_down")
IN_PROJ_W = 10528
IN_SHARD_W = IN_PROJ_W // N_DEV
IN_SEGMENTS = ((2304, 4352), (8480, 10528), (4352, 8448), (0, 2304), (8448, 8480))


def _pack(parts, rows_multiple):
    flat = jnp.concatenate([p.reshape(-1) for p in parts])
    pad = (-flat.shape[0]) % (rows_multiple * LANE)
    return jnp.pad(flat, (0, pad)).reshape(-1, LANE)


def _unpack(flat2d, shapes):
    flat, out, off = flat2d.reshape(-1), [], 0
    for sh in shapes:
        n = int(np.prod(sh))
        out.append(flat[off:off + n].reshape(sh))
        off += n
    return out


def _reorder_in_proj(w):
    qkv, z, xbc = w[:, :2304], w[:, 2304:4352], w[:, 4352:8448]
    dt, gate = w[:, 8448:8480], w[:, 8480:10528]
    return jnp.concatenate([z, gate, xbc, qkv, dt, jnp.zeros((w.shape[0], PROJ_W - 10528), w.dtype)], axis=1)


def _restore_in_proj(wr):
    return jnp.concatenate([wr[:, OFF_QKV:OFF_QKV + 2304], wr[:, OFF_Z:OFF_Z + 2048], wr[:, OFF_XBC:OFF_XBC + 4096],
                            wr[:, OFF_DT:OFF_DT + 32], wr[:, OFF_GL:OFF_GL + 2048]], axis=1)


def _assemble_in_proj(g):
    pieces = []
    for lo, hi in IN_SEGMENTS:
        while lo < hi:
            d = lo // IN_SHARD_W
            end = min(hi, (d + 1) * IN_SHARD_W)
            pieces.append(g[d][:, lo - d * IN_SHARD_W:end - d * IN_SHARD_W])
            lo = end
    pieces.append(jnp.zeros((g.shape[1], PROJ_W - IN_PROJ_W), g.dtype))
    return jnp.concatenate(pieces, axis=1)


def _in_proj_slabs(wr):
    orig = _restore_in_proj(wr)
    return jnp.stack([orig[:, d * IN_SHARD_W:(d + 1) * IN_SHARD_W] for d in range(N_DEV)])


def kernel(x, norm_mix_pre_w, w_in, b_gate, conv_w, conv_b, dt_bias, a_log, d_skip, ssm_norm_w, w_att_proj, w_ssm_proj, w_out, norm_mix_post_w, norm_ffn_pre_w, w_up, w_down, norm_ffn_post_w, loss_target, m_norm_mix_pre_w, m_w_in, m_b_gate, m_conv_w, m_conv_b, m_dt_bias, m_a_log, m_d_skip, m_ssm_norm_w, m_w_att_proj, m_w_ssm_proj, m_w_out, m_norm_mix_post_w, m_norm_ffn_pre_w, m_w_up, m_w_down, m_norm_ffn_post_w, v_norm_mix_pre_w, v_w_in, v_b_gate, v_conv_w, v_conv_b, v_dt_bias, v_a_log, v_d_skip, v_ssm_norm_w, v_w_att_proj, v_w_ssm_proj, v_w_out, v_norm_mix_post_w, v_norm_ffn_pre_w, v_w_up, v_w_down, v_norm_ffn_post_w):
    w = dict(norm_mix_pre_w=norm_mix_pre_w, w_in=w_in, b_gate=b_gate, conv_w=conv_w, conv_b=conv_b, dt_bias=dt_bias, a_log=a_log,
             d_skip=d_skip, ssm_norm_w=ssm_norm_w, w_att_proj=w_att_proj, w_ssm_proj=w_ssm_proj, w_out=w_out,
             norm_mix_post_w=norm_mix_post_w, norm_ffn_pre_w=norm_ffn_pre_w, w_up=w_up, w_down=w_down, norm_ffn_post_w=norm_ffn_post_w)
    m = dict(norm_mix_pre_w=m_norm_mix_pre_w, w_in=m_w_in, b_gate=m_b_gate, conv_w=m_conv_w, conv_b=m_conv_b, dt_bias=m_dt_bias,
             a_log=m_a_log, d_skip=m_d_skip, ssm_norm_w=m_ssm_norm_w, w_att_proj=m_w_att_proj, w_ssm_proj=m_w_ssm_proj, w_out=m_w_out,
             norm_mix_post_w=m_norm_mix_post_w, norm_ffn_pre_w=m_norm_ffn_pre_w, w_up=m_w_up, w_down=m_w_down, norm_ffn_post_w=m_norm_ffn_post_w)
    v = dict(norm_mix_pre_w=v_norm_mix_pre_w, w_in=v_w_in, b_gate=v_b_gate, conv_w=v_conv_w, conv_b=v_conv_b, dt_bias=v_dt_bias,
             a_log=v_a_log, d_skip=v_d_skip, ssm_norm_w=v_ssm_norm_w, w_att_proj=v_w_att_proj, w_ssm_proj=v_w_ssm_proj, w_out=v_w_out,
             norm_mix_post_w=v_norm_mix_post_w, norm_ffn_pre_w=v_norm_ffn_pre_w, w_up=v_w_up, w_down=v_w_down, norm_ffn_post_w=v_norm_ffn_post_w)
    shard_shapes = {n: w[n].shape[1:] for n in ORDER}

    mx, my, mc = _mesh_pos()
    me = 4 * mx + 2 * my + mc

    g_in, g_conv = _all_gather([w["w_in"][0].astype(BF16), w["conv_w"][0]])
    conv_full = jnp.moveaxis(g_conv, 0, 1).reshape(SSM_CONV, CONV_DIM)
    late_shards = [w[n][0].astype(BF16) for n in LATE]
    late_handle, token = _remote_start(late_shards, False, "late_weights_start")
    w_pre = w["norm_mix_pre_w"] + jnp.tile(token[0:1, :], (1, D_MODEL // LANE))

    def late_weights(after):
        full = dict(zip(LATE, _with_own(_remote_wait(late_handle, after, "late_weights_wait"), late_shards, me)))
        for n in ROW_SHARDED:
            full[n] = full[n].reshape(-1, full[n].shape[2])
        w_att = jnp.moveaxis(full["w_att_proj"], 0, 1).reshape(ATT_WIDTH, D_MODEL)
        return w_att, full["w_ssm_proj"], full["w_out"], full["w_up"], full["w_down"]

    started = {}

    def start_exchange(tag, slabs):
        own = [lax.dynamic_index_in_dim(s, me, 0, keepdims=False) for s in slabs]
        handle, tok = _remote_start(slabs, True, tag + "_grads_start")
        started[tag] = (handle, own)
        return tok

    def on_mid_grads(g):
        slabs = dict(w_up=g["w_up"], w_att_proj=jnp.moveaxis(g["w_att_proj"].reshape(ATT_WIDTH, N_DEV, -1), 1, 0))
        for n in ROW_SHARDED:
            slabs[n] = g[n].reshape(N_DEV, -1, g[n].shape[1])
        return start_exchange("mid", [slabs[n] for n in LATE])

    def on_in_proj_grads(gw_in_r, g_conv_w):
        return start_exchange("in_proj", [_in_proj_slabs(gw_in_r), jnp.moveaxis(g_conv_w.reshape(SSM_CONV, N_DEV, -1), 1, 0)])

    loss, g_x, grads = _local_step(
        x[0], loss_target[0], w_pre, _assemble_in_proj(g_in), w["b_gate"], conv_full, w["conv_b"], w["dt_bias"], w["a_log"],
        w["d_skip"], w["ssm_norm_w"], late_weights, w["norm_mix_post_w"], w["norm_ffn_pre_w"], w["norm_ffn_post_w"],
        on_mid_grads, on_in_proj_grads)

    recv = {}
    for tag, names in (("mid", LATE), ("in_proj", ("w_in", "conv_w"))):
        handle, own = started[tag]
        recv.update(zip(names, _with_own(_remote_wait(handle, g_x, tag + "_grads_wait"), own, me)))
    small = _pack([grads[n].astype(F32) for n in SMALL], 8)
    _, small_all = _exchange_grads([], small)

    small_shapes = [shard_shapes[n] for n in SMALL]
    small_out = _adamw(*[_pack([d_[n][0] for n in SMALL], 8) for d_ in (w, m, v)], small_all, "adamw_replicated", small_all.shape[1])
    big_out = {n: _adamw(w[n][0], m[n][0], v[n][0], recv[n], "adamw_" + n, ADAMW_ROWS[n]) for n in BIG}
    res = []
    for which, small_flat in enumerate(small_out):
        vals = {n: big_out[n][which] for n in BIG}
        vals.update(zip(SMALL, _unpack(small_flat, small_shapes)))
        res.append([vals[n][None] for n in ORDER])
    g_out, d_out, m_out, v_out = res
    total = lax.psum(loss[0, 0], ("x", "y", "c"))
    return (total, g_x[None], *g_out, *d_out, *m_out, *v_out)
```

```python
import functools
import math

import jax
import jax.numpy as jnp
import numpy as np
from jax import lax
from jax.experimental import pallas as pl
from jax.experimental.pallas import tpu as pltpu

F32 = jnp.float32
BF16 = jnp.bfloat16

D_MODEL = 1024
HEAD_DIM = 64
N_ATT_HEADS = 12
ATT_WIDTH = N_ATT_HEADS * HEAD_DIM
DILATED_PATTERNS = ((128, 1), (512, 4), (2048, 16))
ATT_BLOCK = 128
SSM_INNER = 2048
SSM_HEAD_DIM = 64
SSM_HEADS = 32
SSM_GROUPS = 8
SSM_STATE = 128
SSM_CHUNK = 128
CONV_DIM = 4096
SSM_CONV = 4
FFN_HIDDEN = 4096
RMS_EPS = 1e-6
N_DEV = 8

ADAM_LR = 0.001
ADAM_B1 = 0.9
ADAM_B2 = 0.999
ADAM_EPS = 1e-08
ADAM_WD = 0.01
ADAM_STEP = 10

LANE = 128
OFF_Z, OFF_GL, OFF_XBC, OFF_QKV, OFF_DT = 0, 2048, 4096, 8192, 10496
PROJ_W = 10752
PROJ_BLOCKS = PROJ_W // LANE
VMEM_LIMIT = 52 * 1024 * 1024
NEG = -1e30

HI = lax.Precision.HIGHEST
NT_DIMS = (((1,), (1,)), ((), ()))
TN_DIMS = (((0,), (0,)), ((), ()))
S = jax.ShapeDtypeStruct


def _params(sem):
    return pltpu.CompilerParams(dimension_semantics=sem, vmem_limit_bytes=VMEM_LIMIT)


def _matmul(a, b, *, mode, out_dtype, name, tm, tn, tk, epilogue=None, extra=None, stacked=False, after=None):
    if mode == "nn":
        m, k = a.shape
        n = b.shape[0] * b.shape[2] if stacked else b.shape[1]
        a_spec = pl.BlockSpec((tm, tk), lambda i, j, kk: (i, kk))
        b_spec = pl.BlockSpec((None, tk, tn), lambda i, j, kk: (j, kk, 0)) if stacked else pl.BlockSpec((tk, tn), lambda i, j, kk: (kk, j))
        dims = (((1,), (0,)), ((), ()))
    elif mode == "nt":
        m, k = a.shape
        n = b.shape[1] if stacked else b.shape[0]
        a_spec = pl.BlockSpec((tm, tk), lambda i, j, kk: (i, kk))
        b_spec = pl.BlockSpec((None, tn, tk), lambda i, j, kk: (kk, j, 0)) if stacked else pl.BlockSpec((tn, tk), lambda i, j, kk: (j, kk))
        dims = NT_DIMS
    else:
        (k, m), n = a.shape, b.shape[1]
        a_spec = pl.BlockSpec((tk, tm), lambda i, j, kk: (kk, i))
        b_spec = pl.BlockSpec((tk, tn), lambda i, j, kk: (kk, j))
        dims = TN_DIMS
    assert m % tm == 0 and n % tn == 0 and k % tk == 0, (name, m, n, k)
    if stacked:
        assert (tk if mode == "nt" else tn) * N_DEV == (k if mode == "nt" else n), name
    nk = k // tk
    o_spec = pl.BlockSpec((tm, tn), lambda i, j, kk: (i, j))
    in_specs, args = [a_spec, b_spec], [a, b]
    if epilogue == "relu2":
        out_shape = (S((m, n), BF16), S((m, n), BF16))
        out_specs = (o_spec, o_spec)
    elif stacked and mode == "tn":
        out_shape, out_specs = S((N_DEV, m, tn), out_dtype), pl.BlockSpec((None, tm, tn), lambda i, j, kk: (j, i, 0))
    else:
        out_shape, out_specs = S((m, n), out_dtype), o_spec
    if epilogue == "relu2_bwd":
        in_specs.append(o_spec)
        args.append(extra)
    n_in = len(args)
    if after is not None:
        in_specs.append(pl.BlockSpec(after.shape, lambda i, j, kk: (0,) * after.ndim))
        args.append(after)

    def finish(acc, refs):
        if epilogue == "relu2":
            r = jnp.maximum(acc, 0.0)
            refs[0][...] = (r * r).astype(BF16)
            refs[1][...] = acc.astype(BF16)
        elif epilogue == "relu2_bwd":
            up = refs[0][...].astype(F32)
            refs[1][...] = (acc * (2.0 * jnp.maximum(up, 0.0))).astype(out_dtype)
        else:
            refs[0][...] = acc.astype(out_dtype)

    def body(a_ref, b_ref, *rest):
        rest = rest[:n_in - 2] + rest[len(args) - 2:]
        part = lax.dot_general(a_ref[...].astype(BF16), b_ref[...].astype(BF16), dims, preferred_element_type=F32)
        if nk == 1:
            finish(part, rest)
            return
        acc_ref = rest[-1]
        kk = pl.program_id(2)

        @pl.when(kk == 0)
        def _():
            acc_ref[...] = part

        @pl.when(kk > 0)
        def _():
            acc_ref[...] += part

        @pl.when(kk == nk - 1)
        def _():
            finish(acc_ref[...], rest[:-1])

    scratch = [] if nk == 1 else [pltpu.VMEM((tm, tn), F32)]
    return pl.pallas_call(
        body, grid=(m // tm, n // tn, nk), in_specs=in_specs, out_specs=out_specs, out_shape=out_shape,
        scratch_shapes=scratch, name=name, compiler_params=_params(("parallel", "parallel", "arbitrary")),
    )(*args)


def _rowcall(body, name, n_rows, tr, ins, outs, scratch=()):
    res = pl.pallas_call(
        body, grid=(n_rows // tr,),
        in_specs=[pl.BlockSpec(bs, im) for _, bs, im in ins],
        out_specs=[pl.BlockSpec(bs, im) for _, _, bs, im in outs],
        out_shape=[S(sh, dt) for sh, dt, _, _ in outs],
        scratch_shapes=list(scratch), name=name, compiler_params=_params(("arbitrary",)),
    )(*[a for a, _, _ in ins])
    return res


def _rows(arr, tr, width=None, cb=0):
    width = arr.shape[1] if width is None else width
    return (arr, (tr, width), lambda i, cb=cb: (i, cb))


def _whole(arr):
    nd = arr.ndim
    return (arr, arr.shape, lambda i, nd=nd: (0,) * nd)


def _orow(n_rows, width, dtype, tr):
    return ((n_rows, width), dtype, (tr, width), lambda i: (i, 0))


def _oacc(width):
    return ((1, width), F32, (1, width), lambda i: (0, 0))


def _accumulate(ref, value):
    first = pl.program_id(0) == 0

    @pl.when(first)
    def _():
        ref[...] = value

    @pl.when(jnp.logical_not(first))
    def _():
        ref[...] += value


def _colsum(v):
    return jnp.sum(v, axis=0, keepdims=True)


def _rms_fwd(x, w):
    r = lax.rsqrt(jnp.mean(x * x, axis=-1, keepdims=True) + RMS_EPS)
    return x * r * w


def _rms_bwd(gy, x, w):
    r = lax.rsqrt(jnp.mean(x * x, axis=-1, keepdims=True) + RMS_EPS)
    xn = x * r
    gxn = gy * w
    gx = r * (gxn - xn * jnp.mean(gxn * xn, axis=-1, keepdims=True))
    return gx, _colsum(gy * xn)


def _sigmoid(x):
    return 1.0 / (1.0 + jnp.exp(-x))


def _head_expand(n_heads_pad, n_heads, width):
    h = lax.broadcasted_iota(jnp.int32, (n_heads_pad, n_heads * width), 0)
    c = lax.broadcasted_iota(jnp.int32, (n_heads_pad, n_heads * width), 1)
    return (c // width == h).astype(F32)


def _head_reduce(n_heads, width, n_heads_pad):
    c = lax.broadcasted_iota(jnp.int32, (n_heads * width, n_heads_pad), 0)
    h = lax.broadcasted_iota(jnp.int32, (n_heads * width, n_heads_pad), 1)
    return (c // width == h).astype(F32)


def _block_ones(n, width):
    r = lax.broadcasted_iota(jnp.int32, (n, n), 0)
    c = lax.broadcasted_iota(jnp.int32, (n, n), 1)
    return (r // width == c // width).astype(F32)


def _pre_norm(x, w_pre, tr=512):
    t = x.shape[0]

    def body(x_ref, w_ref, u_ref):
        u_ref[...] = _rms_fwd(x_ref[...], w_ref[...]).astype(BF16)

    return _rowcall(body, "pre_norm", t, tr, [_rows(x, tr), _whole(w_pre)], [_orow(t, D_MODEL, BF16, tr)])[0]


def _conv_fwd(proj, conv_w, conv_b, tr=256):
    t = proj.shape[0]
    cb = OFF_XBC // CONV_DIM
    halo = (proj, (8, CONV_DIM), lambda i: (jnp.maximum(i * (tr // 8) - 1, 0), cb))

    def body(cur_ref, prev_ref, w_ref, b_ref, o_ref, ext):
        ext[pl.ds(0, 8), :] = jnp.where(pl.program_id(0) > 0, prev_ref[...], 0.0)
        ext[pl.ds(8, tr), :] = cur_ref[...]
        acc = b_ref[...] + w_ref[3:4, :] * cur_ref[...]
        for k in range(SSM_CONV - 1):
            acc = acc + w_ref[k:k + 1, :] * ext[pl.ds(8 - 3 + k, tr), :]
        o_ref[...] = acc * _sigmoid(acc)

    return _rowcall(body, "conv_fwd", t, tr, [_rows(proj, tr, CONV_DIM, cb), halo, _whole(conv_w), _whole(conv_b)],
                    [_orow(t, CONV_DIM, F32, tr)], scratch=[pltpu.VMEM((tr + 8, CONV_DIM), F32)])[0]


def _dt_fwd(proj, dt_bias_pad, alog_pad, tr=512):
    t = proj.shape[0]

    def body(raw_ref, b_ref, al_ref, dtx_ref, csx_ref, cst_ref):
        v = raw_ref[...] + b_ref[...]
        dt = jnp.maximum(v, 0.0) + jnp.log1p(jnp.exp(-jnp.abs(v)))
        expand = _head_expand(LANE, SSM_HEADS, SSM_HEAD_DIM)
        dtx_ref[...] = jnp.dot(dt, expand, precision=HI, preferred_element_type=F32)
        la = dt * (-jnp.exp(al_ref[...]))
        row = lax.broadcasted_iota(jnp.int32, (SSM_CHUNK, SSM_CHUNK), 0)
        col = lax.broadcasted_iota(jnp.int32, (SSM_CHUNK, SSM_CHUNK), 1)
        tril = (col <= row).astype(F32)
        cs = jnp.concatenate([jnp.dot(tril, la[k * SSM_CHUNK:(k + 1) * SSM_CHUNK, :], precision=HI, preferred_element_type=F32)
                              for k in range(tr // SSM_CHUNK)], axis=0)
        csx_ref[...] = jnp.dot(cs, expand, precision=HI, preferred_element_type=F32)
        cst_ref[...] = cs.T

    return _rowcall(body, "dt_fwd", t, tr, [_rows(proj, tr, LANE, OFF_DT // LANE), _whole(dt_bias_pad), _whole(alog_pad)],
                    [_orow(t, SSM_INNER, F32, tr), _orow(t, SSM_INNER, F32, tr), ((LANE, t), F32, (LANE, tr), lambda i: (0, i))])


def _gate_norm_fwd(y_ssd, xa, proj, dskip_x, norm_w, tr=256):
    t = y_ssd.shape[0]
    gw = SSM_INNER // SSM_GROUPS

    def body(y_ref, xs_ref, z_ref, d_ref, w_ref, o_ref):
        z = z_ref[...]
        y3 = (y_ref[...] + d_ref[...] * xs_ref[...]) * (z * _sigmoid(z))
        for g in range(SSM_GROUPS):
            sl = slice(g * gw, (g + 1) * gw)
            o_ref[:, sl] = _rms_fwd(y3[:, sl], w_ref[:, sl]).astype(BF16)

    return _rowcall(body, "gate_norm_fwd", t, tr,
                    [_rows(y_ssd, tr), _rows(xa, tr, SSM_INNER, 0), _rows(proj, tr, SSM_INNER, OFF_Z // SSM_INNER), _whole(dskip_x), _whole(norm_w)],
                    [_orow(t, SSM_INNER, BF16, tr)])[0]


def _gating_fwd(proj, b_gate, att_p, ssm_p, tr=512):
    t = proj.shape[0]

    def body(gl_ref, b_ref, a_ref, s_ref, o_ref):
        gates = _sigmoid(gl_ref[...] + b_ref[...])
        o_ref[...] = (gates[:, :D_MODEL] * a_ref[...] + gates[:, D_MODEL:] * s_ref[...]).astype(BF16)

    return _rowcall(body, "gating_fwd", t, tr, [_rows(proj, tr, 2 * D_MODEL, OFF_GL // (2 * D_MODEL)), _whole(b_gate), _rows(att_p, tr), _rows(ssm_p, tr)],
                    [_orow(t, D_MODEL, BF16, tr)])[0]


def _mix_post_ffn_pre(x, mixed, w_post, w_fpre, tr=512):
    t = x.shape[0]

    def body(x_ref, m_ref, wp_ref, wf_ref, h1_ref, f_ref):
        h1 = x_ref[...] + _rms_fwd(m_ref[...], wp_ref[...])
        h1_ref[...] = h1
        f_ref[...] = _rms_fwd(h1, wf_ref[...]).astype(BF16)

    return _rowcall(body, "mix_post_ffn_pre", t, tr, [_rows(x, tr), _rows(mixed, tr), _whole(w_post), _whole(w_fpre)],
                    [_orow(t, D_MODEL, F32, tr), _orow(t, D_MODEL, BF16, tr)])


def _loss_and_ffn_post_bwd(h1, dn, w_fpost, target, tr=512):
    t = h1.shape[0]

    def body(h1_ref, dn_ref, w_ref, tg_ref, loss_ref, gh2_ref, gdn_ref, gw_ref):
        dn = dn_ref[...]
        w = w_ref[...]
        err = h1_ref[...] + _rms_fwd(dn, w) - tg_ref[...]
        _accumulate(loss_ref, jnp.zeros((1, LANE), F32) + 0.5 * jnp.sum(jnp.mean(err * err, axis=-1, keepdims=True)))
        gh2 = err * (1.0 / D_MODEL)
        gh2_ref[...] = gh2
        gdn, gw = _rms_bwd(gh2, dn, w)
        gdn_ref[...] = gdn.astype(BF16)
        _accumulate(gw_ref, gw)

    return _rowcall(body, "loss_ffn_post_bwd", t, tr, [_rows(h1, tr), _rows(dn, tr), _whole(w_fpost), _rows(target, tr)],
                    [_oacc(LANE), _orow(t, D_MODEL, F32, tr), _orow(t, D_MODEL, BF16, tr), _oacc(D_MODEL)])


def _ffn_pre_mix_post_bwd(g_h2, g_f, h1, w_fpre, mixed, w_post, tr=512):
    t = h1.shape[0]

    def body(gh2_ref, gf_ref, h1_ref, wf_ref, m_ref, wp_ref, gh1_ref, gm_ref, gwf_ref, gwp_ref):
        gx, gwf = _rms_bwd(gf_ref[...], h1_ref[...], wf_ref[...])
        gh1 = gh2_ref[...] + gx
        gh1_ref[...] = gh1
        gm, gwp = _rms_bwd(gh1, m_ref[...], wp_ref[...])
        gm_ref[...] = gm.astype(BF16)
        _accumulate(gwf_ref, gwf)
        _accumulate(gwp_ref, gwp)

    return _rowcall(body, "ffn_pre_mix_post_bwd", t, tr,
                    [_rows(g_h2, tr), _rows(g_f, tr), _rows(h1, tr), _whole(w_fpre), _rows(mixed, tr), _whole(w_post)],
                    [_orow(t, D_MODEL, F32, tr), _orow(t, D_MODEL, BF16, tr), _oacc(D_MODEL), _oacc(D_MODEL)])


def _gating_bwd(g_mixin, proj, b_gate, att_p, ssm_p, tr=512):
    t = proj.shape[0]

    def body(gm_ref, gl_ref, b_ref, a_ref, s_ref, ga_ref, gs_ref, ggl_ref, gb_ref):
        gates = _sigmoid(gl_ref[...] + b_ref[...])
        gm = gm_ref[...]
        g_att, g_ssm = gates[:, :D_MODEL], gates[:, D_MODEL:]
        ga_ref[...] = (gm * g_att).astype(BF16)
        gs_ref[...] = (gm * g_ssm).astype(BF16)
        ggl_a = gm * a_ref[...] * g_att * (1.0 - g_att)
        ggl_s = gm * s_ref[...] * g_ssm * (1.0 - g_ssm)
        ggl_ref[:, :D_MODEL] = ggl_a.astype(BF16)
        ggl_ref[:, D_MODEL:] = ggl_s.astype(BF16)
        _accumulate(gb_ref.at[:, :D_MODEL], _colsum(ggl_a))
        _accumulate(gb_ref.at[:, D_MODEL:], _colsum(ggl_s))

    return _rowcall(body, "gating_bwd", t, tr,
                    [_rows(g_mixin, tr), _rows(proj, tr, 2 * D_MODEL, OFF_GL // (2 * D_MODEL)), _whole(b_gate), _rows(att_p, tr), _rows(ssm_p, tr)],
                    [_orow(t, D_MODEL, BF16, tr), _orow(t, D_MODEL, BF16, tr), _orow(t, 2 * D_MODEL, BF16, tr), _oacc(2 * D_MODEL)])


def _gate_norm_bwd(g_y4, y_ssd, xa, proj, dskip_x, norm_w, tr=256):
    t = y_ssd.shape[0]
    gw = SSM_INNER // SSM_GROUPS

    def body(g_ref, y_ref, xs_ref, z_ref, d_ref, w_ref, gy2_ref, gz_ref, gnw_ref, gdx_ref, gd_ref):
        z = z_ref[...]
        xs = xs_ref[...]
        sg = _sigmoid(z)
        sz = z * sg
        y2 = y_ref[...] + d_ref[...] * xs
        y3 = y2 * sz
        g4 = g_ref[...]
        for g in range(SSM_GROUPS):
            sl = slice(g * gw, (g + 1) * gw)
            gy3, gnw = _rms_bwd(g4[:, sl], y3[:, sl], w_ref[:, sl])
            _accumulate(gnw_ref.at[:, sl], gnw)
            gy2 = gy3 * sz[:, sl]
            gy2_ref[:, sl] = gy2
            gz_ref[:, sl] = (gy3 * y2[:, sl] * (sg[:, sl] * (1.0 + z[:, sl] * (1.0 - sg[:, sl])))).astype(BF16)
            _accumulate(gdx_ref.at[:, sl], _colsum(gy2 * xs[:, sl]))
        tot = jnp.broadcast_to(gdx_ref[...], (8, SSM_INNER))
        gd_ref[...] = jnp.dot(tot, _head_reduce(SSM_HEADS, SSM_HEAD_DIM, LANE), precision=HI, preferred_element_type=F32)[0:1, :]

    return _rowcall(body, "gate_norm_bwd", t, tr,
                    [_rows(g_y4, tr), _rows(y_ssd, tr), _rows(xa, tr, SSM_INNER, 0), _rows(proj, tr, SSM_INNER, OFF_Z // SSM_INNER), _whole(dskip_x), _whole(norm_w)],
                    [_orow(t, SSM_INNER, F32, tr), _orow(t, SSM_INNER, BF16, tr), _oacc(SSM_INNER), _oacc(SSM_INNER), _oacc(LANE)])


def _dt_bwd(g_dtx, ga_rows, proj, dt_bias_pad, tr=512):
    t = proj.shape[0]

    def body(g_ref, ga_ref, raw_ref, b_ref, o_ref, gb_ref, gal_ref):
        red = _head_reduce(SSM_HEADS, SSM_HEAD_DIM, LANE)
        gdt = jnp.dot(g_ref[...], red, precision=HI, preferred_element_type=F32)
        graw = gdt * _sigmoid(raw_ref[...] + b_ref[...])
        o_ref[...] = graw.astype(BF16)
        _accumulate(gb_ref, _colsum(graw))
        tot = jnp.broadcast_to(_colsum(ga_ref[...]), (8, SSM_INNER))
        gal_ref[...] = jnp.dot(tot, red, precision=HI, preferred_element_type=F32)[0:1, :]

    return _rowcall(body, "dt_bwd", t, tr, [_rows(g_dtx, tr), _whole(ga_rows), _rows(proj, tr, LANE, OFF_DT // LANE), _whole(dt_bias_pad)],
                    [_orow(t, LANE, BF16, tr), _oacc(LANE), _oacc(LANE)])


def _conv_bwd_act(g_xs, g_b, g_c, proj, conv_w, conv_b, tr=256):
    t = proj.shape[0]
    cb = OFF_XBC // CONV_DIM
    halo = (proj, (8, CONV_DIM), lambda i: (jnp.maximum(i * (tr // 8) - 1, 0), cb))
    nb, nc = SSM_INNER, SSM_INNER + SSM_GROUPS * SSM_STATE

    def body(gxs_ref, gb_ref, gc_ref, cur_ref, prev_ref, w_ref, b_ref, o_ref, gcb_ref, gw0, gw1, gw2, gw3, ext):
        ext[pl.ds(0, 8), :] = jnp.where(pl.program_id(0) > 0, prev_ref[...], 0.0)
        ext[pl.ds(8, tr), :] = cur_ref[...]
        acc = b_ref[...] + w_ref[3:4, :] * cur_ref[...]
        for k in range(SSM_CONV - 1):
            acc = acc + w_ref[k:k + 1, :] * ext[pl.ds(8 - 3 + k, tr), :]
        sg = _sigmoid(acc)
        dsilu = sg * (1.0 + acc * (1.0 - sg))
        o_ref[:, :nb] = gxs_ref[...] * dsilu[:, :nb]
        o_ref[:, nb:nc] = gb_ref[...] * dsilu[:, nb:nc]
        o_ref[:, nc:] = gc_ref[...] * dsilu[:, nc:]
        gxc = o_ref[...]
        _accumulate(gcb_ref, _colsum(gxc))
        for k, gw in enumerate((gw0, gw1, gw2, gw3)):
            _accumulate(gw, _colsum(gxc * ext[pl.ds(8 - 3 + k, tr), :]))

    return _rowcall(body, "conv_bwd_act", t, tr,
                    [_rows(g_xs, tr), _rows(g_b, tr), _rows(g_c, tr), _rows(proj, tr, CONV_DIM, cb), halo, _whole(conv_w), _whole(conv_b)],
                    [_orow(t, CONV_DIM, F32, tr)] + [_oacc(CONV_DIM)] * 5, scratch=[pltpu.VMEM((tr + 8, CONV_DIM), F32)])


def _conv_bwd_in(g_xc, conv_w, tr=256):
    t = g_xc.shape[0]
    n_blk = t // tr
    halo = (g_xc, (8, CONV_DIM), lambda i: (jnp.minimum((i + 1) * (tr // 8), t // 8 - 1), 0))

    def body(cur_ref, nxt_ref, w_ref, o_ref, ext):
        ext[pl.ds(0, tr), :] = cur_ref[...]
        ext[pl.ds(tr, 8), :] = jnp.where(pl.program_id(0) < n_blk - 1, nxt_ref[...], 0.0)
        acc = w_ref[3:4, :] * cur_ref[...]
        for k in range(SSM_CONV - 1):
            acc = acc + w_ref[k:k + 1, :] * ext[pl.ds(3 - k, tr), :]
        o_ref[...] = acc.astype(BF16)

    return _rowcall(body, "conv_bwd_in", t, tr, [_rows(g_xc, tr), halo, _whole(conv_w)], [_orow(t, CONV_DIM, BF16, tr)],
                    scratch=[pltpu.VMEM((tr + 8, CONV_DIM), F32)])[0]


def _pre_norm_bwd(g_h1, g_u, x, w_pre, tr=512):
    t = x.shape[0]

    def body(gh_ref, gu_ref, x_ref, w_ref, gx_ref, gw_ref):
        gx, gw = _rms_bwd(gu_ref[...], x_ref[...], w_ref[...])
        gx_ref[...] = gh_ref[...] + gx
        _accumulate(gw_ref, gw)

    return _rowcall(body, "pre_norm_bwd", t, tr, [_rows(g_h1, tr), _rows(g_u, tr), _rows(x, tr), _whole(w_pre)],
                    [_orow(t, D_MODEL, F32, tr), _oacc(D_MODEL)])


def _alibi_slopes(n):
    def pow2(m):
        start = 2.0 ** (-8.0 / m)
        return [start ** (i + 1) for i in range(m)]
    if (n & (n - 1)) == 0:
        s = pow2(n)
    else:
        c = 2 ** int(math.floor(math.log2(n)))
        s = pow2(c) + pow2(2 * c)[0::2][: n - c]
    return np.array(s, dtype=np.float32)


def _slope_rows():
    s = _alibi_slopes(N_ATT_HEADS).reshape(N_ATT_HEADS // 2, 2)
    return jnp.asarray(np.broadcast_to(np.repeat(s, HEAD_DIM, axis=1)[:, None, :], (N_ATT_HEADS // 2, 8, LANE)).copy())


ATT_MAX_BLOCK_ROWS = 2048


RESIDUE_MAJOR_FROM = 16


class _AttLayout:
    def __init__(self, t, dil):
        self.t, self.dil = t, dil
        self.rows = t // dil
        self.residue_major = dil >= RESIDUE_MAJOR_FROM
        if self.residue_major:
            bq, self.stride = min(512, self.rows), 1
        else:
            bq, self.stride = min(512, self.rows, ATT_MAX_BLOCK_ROWS // dil), dil
        self.nsub = bq // ATT_BLOCK
        self.nblk = self.rows // bq
        self.rb = bq * self.stride
        self.pb = ATT_BLOCK * self.stride
        self.n_pb = self.rows * self.stride // self.pb

    def qkv(self, proj):
        if self.residue_major:
            qkv = proj[:, OFF_QKV:OFF_QKV + 3 * ATT_WIDTH]
            return qkv.reshape(self.rows, self.dil * 3 * ATT_WIDTH), 3 * ATT_WIDTH // LANE, 0
        return proj, 0, OFF_QKV // LANE

    def act(self, a):
        return a.reshape(self.rows, self.dil * ATT_WIDTH) if self.residue_major else a

    def act_shape(self):
        return (self.rows, self.dil * ATT_WIDTH) if self.residue_major else (self.t, ATT_WIDTH)

    def col(self, r, band, c):
        return r * band + c if self.residue_major else c


def _residue_rows(r, stride, first_block, n_blocks=1):
    if stride == 1:
        return pl.ds(first_block * ATT_BLOCK, n_blocks * ATT_BLOCK)
    return pl.ds(r + first_block * ATT_BLOCK * stride, n_blocks * ATT_BLOCK, stride=stride)


def _lane_half():
    return lax.broadcasted_iota(jnp.int32, (ATT_BLOCK, LANE), 1) // HEAD_DIM


def _att_scores_mask(dil, first):
    iq = lax.broadcasted_iota(jnp.int32, (ATT_BLOCK, 2 * ATT_BLOCK), 0)
    jk = lax.broadcasted_iota(jnp.int32, (ATT_BLOCK, 2 * ATT_BLOCK), 1)
    dist = ATT_BLOCK + iq - jk
    valid = (dist >= 0) & (dist <= ATT_BLOCK) & (jnp.logical_not(first) | (jk >= ATT_BLOCK))
    return (dist * dil).astype(F32), valid


def _att_fwd(proj, dil, slopes):
    t = proj.shape[0]
    lay = _AttLayout(t, dil)
    nsub, nblk, rb, pb = lay.nsub, lay.nblk, lay.rb, lay.pb
    src, band, qb = lay.qkv(proj)
    aw = ATT_WIDTH // LANE

    def spec(off, prev=False):
        if prev:
            return pl.BlockSpec((pb, LANE), lambda hp, i, r: (jnp.maximum(i * nsub - 1, 0), lay.col(r, band, qb + off + hp)))
        return pl.BlockSpec((rb, LANE), lambda hp, i, r: (i, lay.col(r, band, qb + off + hp)))

    o_spec = pl.BlockSpec((rb, LANE), lambda hp, i, r: (i, lay.col(r, aw, hp)))

    def body(q_ref, kc_ref, kp_ref, vc_ref, vp_ref, sl_ref, o_ref, lse_ref):
        i, r = pl.program_id(1), pl.program_id(2)
        half = _lane_half()
        for sub in range(nsub):
            rs = _residue_rows(r, lay.stride, sub)
            q = (q_ref[rs, :] * (HEAD_DIM ** -0.5)).astype(BF16)
            if sub == 0:
                r0 = _residue_rows(r, lay.stride, 0)
                kk = jnp.concatenate([kp_ref[r0, :], kc_ref[rs, :]], axis=0).astype(BF16)
                vv = jnp.concatenate([vp_ref[r0, :], vc_ref[rs, :]], axis=0).astype(BF16)
                first = i == 0
            else:
                ks = _residue_rows(r, lay.stride, sub - 1, 2)
                kk, vv = kc_ref[ks, :].astype(BF16), vc_ref[ks, :].astype(BF16)
                first = jnp.bool_(False)
            dist, valid = _att_scores_mask(dil, first)
            outs, lses = [], []
            for e in range(2):
                qe = jnp.where(half == e, q, jnp.zeros_like(q))
                s = lax.dot_general(qe, kk, NT_DIMS, preferred_element_type=F32)
                s = s + jnp.where(valid, -sl_ref[0:1, e * HEAD_DIM:e * HEAD_DIM + 1] * dist, NEG)
                m = jnp.max(s, axis=-1, keepdims=True)
                p = jnp.exp(s - m)
                l = jnp.sum(p, axis=-1, keepdims=True)
                outs.append(jnp.dot(p.astype(BF16), vv, preferred_element_type=F32) / l)
                lses.append(m + jnp.log(l))
            o_ref[rs, :] = jnp.where(half == 0, outs[0], outs[1])
            lse_ref[rs, :] = jnp.where(half == 0, lses[0], lses[1])

    o, lse = pl.pallas_call(
        body, grid=(N_ATT_HEADS // 2, nblk, dil),
        in_specs=[spec(0), spec(6), spec(6, True), spec(12), spec(12, True), pl.BlockSpec((None, 8, LANE), lambda hp, i, r: (hp, 0, 0))],
        out_specs=[o_spec, o_spec], out_shape=[S(lay.act_shape(), F32)] * 2,
        name=f"att_fwd_d{dil}", compiler_params=_params(("parallel", "parallel", "arbitrary")),
    )(src, src, src, src, src, slopes)
    return o.reshape(t, ATT_WIDTH), lse.reshape(t, ATT_WIDTH)


def _att_combine(outs, lses, tr=512):
    t = outs[0].shape[0]

    def body(o0, o1, o2, l0, l1, l2, att_ref, lse_ref):
        ls = [l0[...], l1[...], l2[...]]
        m = jnp.maximum(jnp.maximum(ls[0], ls[1]), ls[2])
        ws = [jnp.exp(l - m) for l in ls]
        tot = ws[0] + ws[1] + ws[2]
        num = ws[0] * o0[...].astype(F32) + ws[1] * o1[...].astype(F32) + ws[2] * o2[...].astype(F32)
        att_ref[...] = (num / tot).astype(BF16)
        lse_ref[...] = m + jnp.log(tot)

    return _rowcall(body, "att_combine", t, tr, [_rows(a, tr) for a in list(outs) + list(lses)],
                    [_orow(t, ATT_WIDTH, BF16, tr), _orow(t, ATT_WIDTH, F32, tr)])


def _att_delta(g_att, att, tr=512):
    t = att.shape[0]

    def body(g_ref, a_ref, o_ref):
        prod = g_ref[...] * a_ref[...].astype(F32)
        o_ref[...] = jnp.dot(prod, _block_ones(ATT_WIDTH, HEAD_DIM), precision=HI, preferred_element_type=F32)

    return _rowcall(body, "att_delta", t, tr, [_rows(g_att, tr), _rows(att, tr)], [_orow(t, ATT_WIDTH, F32, tr)])[0]


def _att_bwd(proj, g_att, lse, delta, dil, slopes):
    t = proj.shape[0]
    lay = _AttLayout(t, dil)
    nsub, nblk, rb, pb, n_pb = lay.nsub, lay.nblk, lay.rb, lay.pb, lay.n_pb
    src, band, qb = lay.qkv(proj)
    aw = ATT_WIDTH // LANE

    def near(i, which):
        return jnp.maximum(i * nsub - 1, 0) if which == "prev" else jnp.minimum((i + 1) * nsub, n_pb - 1)

    def pspec(off, which=None):
        if which:
            return pl.BlockSpec((pb, LANE), lambda hp, i, r: (near(i, which), lay.col(r, band, qb + off + hp)))
        return pl.BlockSpec((rb, LANE), lambda hp, i, r: (i, lay.col(r, band, qb + off + hp)))

    def aspec(which=None):
        if which:
            return pl.BlockSpec((pb, LANE), lambda hp, i, r: (near(i, which), lay.col(r, aw, hp)))
        return pl.BlockSpec((rb, LANE), lambda hp, i, r: (i, lay.col(r, aw, hp)))

    scale = HEAD_DIM ** -0.5

    def body(q_ref, qn_ref, kc_ref, kp_ref, vc_ref, vp_ref, do_ref, don_ref, lse_ref, lsen_ref, dl_ref, dln_ref, sl_ref,
             dq_ref, dk_ref, dv_ref):
        i, r = pl.program_id(1), pl.program_id(2)
        half = _lane_half()

        def tile_grads(q, do, lse_q, dl_q, kk, vv, dist, valid):
            dqs, dks, dvs = [], [], []
            for e in range(2):
                c = e * HEAD_DIM
                qe = jnp.where(half == e, q, jnp.zeros_like(q))
                doe = jnp.where(half == e, do, jnp.zeros_like(do))
                s = lax.dot_general(qe, kk, NT_DIMS, preferred_element_type=F32)
                s = s + jnp.where(valid, -sl_ref[0:1, c:c + 1] * dist, NEG)
                p = jnp.exp(s - lse_q[:, c:c + 1])
                dp = lax.dot_general(doe, vv, NT_DIMS, preferred_element_type=F32)
                ds16 = (p * (dp - dl_q[:, c:c + 1])).astype(BF16)
                dqs.append(jnp.dot(ds16, kk, preferred_element_type=F32))
                dks.append(lax.dot_general(ds16, q, TN_DIMS, preferred_element_type=F32))
                dvs.append(lax.dot_general(p.astype(BF16), do, TN_DIMS, preferred_element_type=F32))
            halfk = lax.broadcasted_iota(jnp.int32, dks[0].shape, 1) // HEAD_DIM
            return (jnp.where(half == 0, dqs[0], dqs[1]) * scale,
                    jnp.where(halfk == 0, dks[0], dks[1]), jnp.where(halfk == 0, dvs[0], dvs[1]))

        carry_k = carry_v = None
        for sub in range(nsub):
            rs = _residue_rows(r, lay.stride, sub)
            q = (q_ref[rs, :] * scale).astype(BF16)
            do = do_ref[rs, :].astype(BF16)
            if sub == 0:
                r0 = _residue_rows(r, lay.stride, 0)
                kk = jnp.concatenate([kp_ref[r0, :], kc_ref[rs, :]], axis=0).astype(BF16)
                vv = jnp.concatenate([vp_ref[r0, :], vc_ref[rs, :]], axis=0).astype(BF16)
                first = i == 0
            else:
                ks = _residue_rows(r, lay.stride, sub - 1, 2)
                kk, vv = kc_ref[ks, :].astype(BF16), vc_ref[ks, :].astype(BF16)
                first = jnp.bool_(False)
            dist, valid = _att_scores_mask(dil, first)
            dq, dk2, dv2 = tile_grads(q, do, lse_ref[rs, :], dl_ref[rs, :], kk, vv, dist, valid)
            dq_ref[rs, :] = dq
            if sub > 0:
                rp = _residue_rows(r, lay.stride, sub - 1)
                dk_ref[rp, :] = carry_k + dk2[:ATT_BLOCK, :]
                dv_ref[rp, :] = carry_v + dv2[:ATT_BLOCK, :]
            carry_k, carry_v = dk2[ATT_BLOCK:, :], dv2[ATT_BLOCK:, :]
        rl = _residue_rows(r, lay.stride, nsub - 1)
        rn = _residue_rows(r, lay.stride, 0)
        iq = lax.broadcasted_iota(jnp.int32, (ATT_BLOCK, ATT_BLOCK), 0)
        jk = lax.broadcasted_iota(jnp.int32, (ATT_BLOCK, ATT_BLOCK), 1)
        dist_i = ATT_BLOCK + iq - jk
        valid = (dist_i >= 0) & (dist_i <= ATT_BLOCK) & (i < nblk - 1)
        qn = (qn_ref[rn, :] * scale).astype(BF16)
        _, dk1, dv1 = tile_grads(qn, don_ref[rn, :].astype(BF16), lsen_ref[rn, :], dln_ref[rn, :],
                                 kc_ref[rl, :].astype(BF16), vc_ref[rl, :].astype(BF16), (dist_i * dil).astype(F32), valid)
        dk_ref[rl, :] = carry_k + dk1
        dv_ref[rl, :] = carry_v + dv1

    gv, lv, dlv = lay.act(g_att), lay.act(lse), lay.act(delta)
    dq, dk, dv = pl.pallas_call(
        body, grid=(N_ATT_HEADS // 2, nblk, dil),
        in_specs=[pspec(0), pspec(0, "next"), pspec(6), pspec(6, "prev"), pspec(12), pspec(12, "prev"),
                  aspec(), aspec("next"), aspec(), aspec("next"), aspec(), aspec("next"),
                  pl.BlockSpec((None, 8, LANE), lambda hp, i, r: (hp, 0, 0))],
        out_specs=[aspec(), aspec(), aspec()], out_shape=[S(lay.act_shape(), F32)] * 3,
        name=f"att_bwd_d{dil}", compiler_params=_params(("parallel", "parallel", "arbitrary")),
    )(src, src, src, src, src, src, gv, gv, lv, lv, dlv, dlv, slopes)
    return dq.reshape(t, ATT_WIDTH), dk.reshape(t, ATT_WIDTH), dv.reshape(t, ATT_WIDTH)


def _att_grad_sum(dqs, dks, dvs, tr=512):
    t = dqs[0].shape[0]

    def body(*refs):
        o_ref = refs[-1]
        for n in range(3):
            tot = refs[3 * n][...] + refs[3 * n + 1][...] + refs[3 * n + 2][...]
            o_ref[:, n * ATT_WIDTH:(n + 1) * ATT_WIDTH] = tot.astype(BF16)

    return _rowcall(body, "att_grad_sum", t, tr, [_rows(a, tr) for a in list(dqs) + list(dks) + list(dvs)],
                    [_orow(t, 3 * ATT_WIDTH, BF16, tr)])[0]


def _ssd_common(xs, dtx, cs, cs_t):
    ch = SSM_CHUNK
    row = lax.broadcasted_iota(jnp.int32, (ch, ch), 0)
    col = lax.broadcasted_iota(jnp.int32, (ch, ch), 1)
    cs_last = cs[ch - 1:ch, :]
    return dict(tril=col <= row, row=row, col=col, cs=cs, cs_t=cs_t, cs_last=cs_last,
                e=jnp.exp(cs), w=jnp.exp(cs_last - cs), xd=xs * dtx)


def _dot_split(a, b, split):
    ops = [a, b]
    x = ops[split]
    hi = x.astype(BF16)
    lo = (x - hi.astype(F32)).astype(BF16)
    other = ops[1 - split].astype(BF16)
    if split == 1:
        return jnp.dot(other, hi, preferred_element_type=F32) + jnp.dot(other, lo, preferred_element_type=F32)
    return jnp.dot(hi, other, preferred_element_type=F32) + jnp.dot(lo, other, preferred_element_type=F32)


def _decay_col(cs_t, heads_per_group):
    r = lax.broadcasted_iota(jnp.int32, (heads_per_group * SSM_HEAD_DIM, SSM_STATE), 0) // SSM_HEAD_DIM
    out = jnp.zeros((heads_per_group * SSM_HEAD_DIM, SSM_STATE), F32)
    for j in range(heads_per_group):
        out = jnp.where(r == j, jnp.exp(cs_t[j:j + 1, SSM_CHUNK - 1:SSM_CHUNK]), out)
    return out


SSD_GROUPS_PER_STEP = 2


def _ssd_specs(t):
    hg = SSM_HEADS // SSM_GROUPS
    gw = hg * SSM_HEAD_DIM
    nb0 = SSM_INNER // SSM_STATE
    return hg, gw, nb0


def _ssd_group_views(gi, gw, wide, narrow, stacked):
    w = [r.at[:, pl.ds(gi * gw, gw)] for r in wide]
    n = [r.at[:, pl.ds(gi * SSM_STATE, SSM_STATE)] for r in narrow]
    return w, n, [r.at[gi] for r in stacked]


def _ssd_fwd(xa, dtx, csx, cst_g):
    t = xa.shape[0]
    nch = t // SSM_CHUNK
    hg, gw, nb0 = _ssd_specs(t)
    ch = SSM_CHUNK
    gp = SSD_GROUPS_PER_STEP

    def body(xs_ref, b_ref, c_ref, dtx_ref, cs_ref, cst_ref, y_ref, st_ref, h_scr):
        for gi in range(gp):
            (xs_g, dtx_g, cs_g, y_g), (b_g, c_g), (cst_gi, st_g) = _ssd_group_views(
                gi, gw, (xs_ref, dtx_ref, cs_ref, y_ref), (b_ref, c_ref), (cst_ref, st_ref))
            group_body(pl.program_id(0), pl.program_id(1) * gp + gi, xs_g, b_g, c_g, dtx_g, cs_g, cst_gi, y_g, st_g, h_scr)

    def group_body(cc, g, xs_ref, b_ref, c_ref, dtx_ref, cs_ref, cst_ref, y_ref, st_ref, h_scr):
        @pl.when(cc == 0)
        def _():
            h_scr[g] = jnp.zeros((gw, SSM_STATE), F32)

        q = _ssd_common(xs_ref[...], dtx_ref[...], cs_ref[...], cst_ref[...])
        bb, cb = b_ref[...].astype(BF16), c_ref[...].astype(BF16)
        cbm = lax.dot_general(cb, bb, NT_DIMS, preferred_element_type=F32)
        h = h_scr[g]
        st_ref[...] = h
        xd16 = q["xd"].astype(BF16)
        y = lax.dot_general(cb, h.astype(BF16), NT_DIMS, preferred_element_type=F32) * q["e"]
        lane_head = lax.broadcasted_iota(jnp.int32, (ch, gw), 1) // SSM_HEAD_DIM
        for j in range(hg):
            diff = q["cs"][:, j * SSM_HEAD_DIM:j * SSM_HEAD_DIM + 1] - q["cs_t"][j:j + 1, :]
            gmat = cbm * jnp.exp(jnp.where(q["tril"], diff, NEG))
            yj = jnp.dot(gmat.astype(BF16), xd16, preferred_element_type=F32)
            y = y + jnp.where(lane_head == j, yj, 0.0)
        y_ref[...] = y
        s_new = lax.dot_general((q["xd"] * q["w"]).astype(BF16), bb, TN_DIMS, preferred_element_type=F32)
        h_scr[g] = _decay_col(q["cs_t"], hg) * h + s_new

    wide = pl.BlockSpec((ch, gp * gw), lambda cc, g: (cc, g))
    return pl.pallas_call(
        body, grid=(nch, SSM_GROUPS // gp),
        in_specs=[wide,
                  pl.BlockSpec((ch, gp * SSM_STATE), lambda cc, g: (cc, nb0 // gp + g)),
                  pl.BlockSpec((ch, gp * SSM_STATE), lambda cc, g: (cc, (nb0 + SSM_GROUPS) // gp + g)),
                  wide, wide,
                  pl.BlockSpec((gp, 8, ch), lambda cc, g: (g, 0, cc))],
        out_specs=[wide, pl.BlockSpec((None, gp, gw, SSM_STATE), lambda cc, g: (cc, g, 0, 0))],
        out_shape=[S((t, SSM_INNER), F32), S((nch, SSM_GROUPS, gw, SSM_STATE), F32)],
        scratch_shapes=[pltpu.VMEM((SSM_GROUPS, gw, SSM_STATE), F32)],
        name="ssd_fwd", compiler_params=_params(("arbitrary", "arbitrary")),
    )(xa, xa, xa, dtx, csx, cst_g)


def _ssd_bwd(xa, dtx, csx, cst_g, alog_x, g_y, states, dskip_x):
    t = xa.shape[0]
    nch = t // SSM_CHUNK
    hg, gw, nb0 = _ssd_specs(t)
    ch = SSM_CHUNK
    gp = SSD_GROUPS_PER_STEP

    def rc(cc):
        return nch - 1 - cc

    def body(xs_ref, b_ref, c_ref, dtx_ref, cs_ref, cst_ref, alx_ref, gy_ref, st_ref, dsk_ref,
             gxs_ref, gb_ref, gc_ref, gdt_ref, ga_ref, gh_scr):
        for gi in range(gp):
            wide, narrow, stacked = _ssd_group_views(
                gi, gw, (xs_ref, dtx_ref, cs_ref, alx_ref, gy_ref, dsk_ref, gxs_ref, gdt_ref, ga_ref), (b_ref, c_ref, gb_ref, gc_ref),
                (cst_ref, st_ref))
            xs_g, dtx_g, cs_g, alx_g, gy_g, dsk_g, gxs_g, gdt_g, ga_g = wide
            b_g, c_g, gb_g, gc_g = narrow
            group_body(pl.program_id(0), pl.program_id(1) * gp + gi, xs_g, b_g, c_g, dtx_g, cs_g, stacked[0], alx_g, gy_g, stacked[1],
                       dsk_g, gxs_g, gb_g, gc_g, gdt_g, ga_g, gh_scr)

    def group_body(cc, g, xs_ref, b_ref, c_ref, dtx_ref, cs_ref, cst_ref, alx_ref, gy_ref, st_ref, dsk_ref,
                   gxs_ref, gb_ref, gc_ref, gdt_ref, ga_ref, gh_scr):
        @pl.when(cc == 0)
        def _():
            gh_scr[g] = jnp.zeros((gw, SSM_STATE), F32)

        xs, dtx = xs_ref[...], dtx_ref[...]
        q = _ssd_common(xs, dtx, cs_ref[...], cst_ref[...])
        cs, cs_t, e, w, xd = q["cs"], q["cs_t"], q["e"], q["w"], q["xd"]
        bb, cb = b_ref[...].astype(BF16), c_ref[...].astype(BF16)
        gy = gy_ref[...]
        gy16, xd16 = gy.astype(BF16), xd.astype(BF16)
        h = st_ref[...]
        h16 = h.astype(BF16)
        ghn = gh_scr[g]
        ghn16 = ghn.astype(BF16)
        seg = _block_ones(gw, SSM_HEAD_DIM)
        cbm = lax.dot_general(cb, bb, NT_DIMS, preferred_element_type=F32)
        cbt = lax.dot_general(bb, cb, NT_DIMS, preferred_element_type=F32)

        gye16 = (gy * e).astype(BF16)
        chm = lax.dot_general(cb, h16, NT_DIMS, preferred_element_type=F32)
        g_c = jnp.dot(gye16, h16, preferred_element_type=F32)
        gh_off = lax.dot_general(gye16, cb, TN_DIMS, preferred_element_type=F32)
        g_e = _dot_split(gy * chm, seg, 0)
        bgs = lax.dot_general(bb, ghn16, NT_DIMS, preferred_element_type=F32)
        g_xd = w * bgs
        g_w = _dot_split(xd * bgs, seg, 0)
        g_b = jnp.dot((xd * w).astype(BF16), ghn16, preferred_element_type=F32)
        decay = _decay_col(cs_t, hg)
        gh_scr[g] = decay * ghn + gh_off
        rsum = jnp.sum(ghn * h, axis=1, keepdims=True)
        lane_head = lax.broadcasted_iota(jnp.int32, (ch, gw), 1) // SSM_HEAD_DIM
        lane_head1 = lax.broadcasted_iota(jnp.int32, (1, gw), 1) // SSM_HEAD_DIM
        g_el = jnp.zeros((1, gw), F32)
        g_cs = g_e * e - g_w * w
        upper = q["row"] <= q["col"]
        for j in range(hg):
            g_el = jnp.where(lane_head1 == j, jnp.sum(rsum[j * SSM_HEAD_DIM:(j + 1) * SSM_HEAD_DIM, :], axis=0, keepdims=True), g_el)
            csc = cs[:, j * SSM_HEAD_DIM:j * SSM_HEAD_DIM + 1]
            csr = cs_t[j:j + 1, :]
            lm = jnp.exp(jnp.where(q["tril"], csc - csr, NEG))
            lmt = jnp.exp(jnp.where(upper, csr - csc, NEG))
            gyj = jnp.where(lane_head == j, gy16, jnp.zeros_like(gy16))
            xdj = jnp.where(lane_head == j, xd16, jnp.zeros_like(xd16))
            gg = lax.dot_general(gyj, xd16, NT_DIMS, preferred_element_type=F32)
            ggt = lax.dot_general(xdj, gy16, NT_DIMS, preferred_element_type=F32)
            gcb, gcbt = gg * lm, ggt * lmt
            g_c = g_c + jnp.dot(gcb.astype(BF16), bb, preferred_element_type=F32)
            g_b = g_b + jnp.dot(gcbt.astype(BF16), cb, preferred_element_type=F32)
            gxdj = jnp.dot((cbt * lmt).astype(BF16), gy16, preferred_element_type=F32)
            g_xd = g_xd + jnp.where(lane_head == j, gxdj, 0.0)
            d_cs = jnp.sum(gcb * cbm, axis=1, keepdims=True) - jnp.sum(gcbt * cbt, axis=1, keepdims=True)
            g_cs = g_cs + jnp.where(lane_head == j, d_cs, 0.0)
        extra = _colsum(g_w * w) + g_el * jnp.exp(q["cs_last"])
        g_cs = g_cs + jnp.where(lax.broadcasted_iota(jnp.int32, (ch, gw), 0) == ch - 1, extra, 0.0)
        g_la = _dot_split(upper, g_cs, 1)
        a_x = -jnp.exp(alx_ref[...])
        gdt_ref[...] = g_xd * xs + g_la * a_x * (1.0 / SSM_HEAD_DIM)
        ga_row = _colsum(g_la * (dtx * a_x)) * (1.0 / SSM_HEAD_DIM)
        ga_ref[...] = jnp.where(lax.broadcasted_iota(jnp.int32, (8, gw), 0) == 0, ga_row, 0.0)
        gxs_ref[...] = g_xd * dtx + gy * dsk_ref[...]
        gb_ref[...] = g_b
        gc_ref[...] = g_c

    wide = pl.BlockSpec((ch, gp * gw), lambda cc, g: (rc(cc), g))
    narrow = pl.BlockSpec((ch, gp * SSM_STATE), lambda cc, g: (rc(cc), g))
    row = pl.BlockSpec((1, gp * gw), lambda cc, g: (0, g))
    return pl.pallas_call(
        body, grid=(nch, SSM_GROUPS // gp),
        in_specs=[wide,
                  pl.BlockSpec((ch, gp * SSM_STATE), lambda cc, g: (rc(cc), nb0 // gp + g)),
                  pl.BlockSpec((ch, gp * SSM_STATE), lambda cc, g: (rc(cc), (nb0 + SSM_GROUPS) // gp + g)),
                  wide, wide,
                  pl.BlockSpec((gp, 8, ch), lambda cc, g: (g, 0, rc(cc))),
                  row, wide,
                  pl.BlockSpec((None, gp, gw, SSM_STATE), lambda cc, g: (rc(cc), g, 0, 0)),
                  row],
        out_specs=[wide, narrow, narrow, wide, pl.BlockSpec((8, gp * gw), lambda cc, g: (rc(cc), g))],
        out_shape=[S((t, SSM_INNER), F32), S((t, SSM_GROUPS * SSM_STATE), F32), S((t, SSM_GROUPS * SSM_STATE), F32),
                   S((t, SSM_INNER), F32), S((nch * 8, SSM_INNER), F32)],
        scratch_shapes=[pltpu.VMEM((SSM_GROUPS, gw, SSM_STATE), F32)],
        name="ssd_bwd", compiler_params=_params(("arbitrary", "arbitrary")),
    )(xa, xa, xa, dtx, csx, cst_g, alog_x, g_y, states, dskip_x)


def _local_step(x, target, w_pre, w_in_r, b_gate, conv_w, conv_b, dt_bias, a_log, d_skip, ssm_norm_w,
                late_weights, w_post, w_fpre, w_fpost, on_mid_grads, on_in_proj_grads):
    t = x.shape[0]
    mm = functools.partial(_matmul, tm=512)
    slopes = _slope_rows()
    hg = SSM_HEADS // SSM_GROUPS
    dt_bias_pad = jnp.pad(dt_bias, ((0, 0), (0, LANE - SSM_HEADS)))
    alog_x = jnp.repeat(a_log, SSM_HEAD_DIM, axis=1)
    alog_pad = jnp.pad(a_log, ((0, 0), (0, LANE - SSM_HEADS)))
    dskip_x = jnp.repeat(d_skip, SSM_HEAD_DIM, axis=1)

    u = _pre_norm(x, w_pre)
    proj = mm(u, w_in_r, mode="nn", out_dtype=F32, name="in_proj", tn=1792, tk=D_MODEL)
    fwd = [_att_fwd(proj, dil, slopes) for _, dil in DILATED_PATTERNS]
    att, lse = _att_combine([o for o, _ in fwd], [l for _, l in fwd])
    xa = _conv_fwd(proj, conv_w, conv_b)
    dtx, csx, cst = _dt_fwd(proj, dt_bias_pad, alog_pad)
    cst_g = jnp.pad(cst[:SSM_HEADS].reshape(SSM_GROUPS, hg, t), ((0, 0), (0, 8 - hg), (0, 0)))
    y_ssd, states = _ssd_fwd(xa, dtx, csx, cst_g)
    y4 = _gate_norm_fwd(y_ssd, xa, proj, dskip_x, ssm_norm_w)
    w_att, w_ssm, w_out, w_up, w_down = late_weights(y4)
    att_p = mm(att, w_att, mode="nn", out_dtype=F32, name="att_proj", tn=D_MODEL, tk=ATT_WIDTH)
    ssm_p = mm(y4, w_ssm, mode="nn", out_dtype=F32, name="ssm_proj", tn=D_MODEL, tk=SSM_INNER)
    mixin = _gating_fwd(proj, b_gate, att_p, ssm_p)
    mixed = mm(mixin, w_out, mode="nn", out_dtype=F32, name="out_proj", tn=D_MODEL, tk=D_MODEL)
    h1, f = _mix_post_ffn_pre(x, mixed, w_post, w_fpre)
    act, up = _matmul(f, w_up, mode="nn", out_dtype=BF16, name="ffn_up", tm=2048, tn=FFN_HIDDEN // N_DEV, tk=D_MODEL, epilogue="relu2", stacked=True)
    dn = mm(act, w_down, mode="nn", out_dtype=F32, name="ffn_down", tn=D_MODEL, tk=FFN_HIDDEN)
    loss, g_h2, g_dn, gw_fpost = _loss_and_ffn_post_bwd(h1, dn, w_fpost, target)

    g_up = mm(g_dn, w_down, mode="nt", out_dtype=BF16, name="ffn_down_bwd_x", tn=2048, tk=D_MODEL, epilogue="relu2_bwd", extra=up)
    gw_down = _matmul(act, g_dn, mode="tn", out_dtype=BF16, name="ffn_down_bwd_w", tm=1024, tn=D_MODEL, tk=512)
    g_f = _matmul(g_up, w_up, mode="nt", out_dtype=F32, name="ffn_up_bwd_x", tm=2048, tn=D_MODEL, tk=FFN_HIDDEN // N_DEV, stacked=True)
    gw_up = _matmul(f, g_up, mode="tn", out_dtype=BF16, name="ffn_up_bwd_w", tm=D_MODEL, tn=FFN_HIDDEN // N_DEV, tk=2048, stacked=True)
    g_h1, g_mixed, gw_fpre, gw_post = _ffn_pre_mix_post_bwd(g_h2, g_f, h1, w_fpre, mixed, w_post)
    g_mixin = mm(g_mixed, w_out, mode="nt", out_dtype=F32, name="out_proj_bwd_x", tn=D_MODEL, tk=D_MODEL)
    gw_out = _matmul(mixin, g_mixed, mode="tn", out_dtype=BF16, name="out_proj_bwd_w", tm=D_MODEL, tn=D_MODEL, tk=512)
    g_att_p, g_ssm_p, g_gl, g_b_gate = _gating_bwd(g_mixin, proj, b_gate, att_p, ssm_p)
    g_att = mm(g_att_p, w_att, mode="nt", out_dtype=F32, name="att_proj_bwd_x", tn=ATT_WIDTH, tk=D_MODEL)
    gw_att = _matmul(att, g_att_p, mode="tn", out_dtype=BF16, name="att_proj_bwd_w", tm=ATT_WIDTH, tn=D_MODEL, tk=512)
    g_y4 = mm(g_ssm_p, w_ssm, mode="nt", out_dtype=F32, name="ssm_proj_bwd_x", tn=SSM_INNER, tk=D_MODEL)
    gw_ssm = _matmul(y4, g_ssm_p, mode="tn", out_dtype=BF16, name="ssm_proj_bwd_w", tm=1024, tn=D_MODEL, tk=512)
    token = on_mid_grads(dict(w_att_proj=gw_att, w_ssm_proj=gw_ssm, w_out=gw_out, w_up=gw_up, w_down=gw_down))
    if token is not None:
        ssm_norm_w = ssm_norm_w + jnp.tile(token[0:1, :], (1, SSM_INNER // LANE))
    g_y2, g_z, g_norm_w, _, g_d_skip = _gate_norm_bwd(g_y4, y_ssd, xa, proj, dskip_x, ssm_norm_w)
    g_xs, g_bm, g_cm, g_dtx, ga_rows = _ssd_bwd(xa, dtx, csx, cst_g, alog_x, g_y2, states, dskip_x)
    g_dt_raw, g_dt_bias, g_a_log = _dt_bwd(g_dtx, ga_rows, proj, dt_bias_pad)
    g_xc, g_conv_b, gcw0, gcw1, gcw2, gcw3 = _conv_bwd_act(g_xs, g_bm, g_cm, proj, conv_w, conv_b)
    g_xbc = _conv_bwd_in(g_xc, conv_w)
    delta = _att_delta(g_att, att)
    dqs, dks, dvs = [], [], []
    for _, dil in DILATED_PATTERNS:
        dq, dk, dv = _att_bwd(proj, g_att, lse, delta, dil, slopes)
        dqs.append(dq)
        dks.append(dk)
        dvs.append(dv)
    g_qkv = _att_grad_sum(dqs, dks, dvs)
    g_proj = jnp.concatenate([g_z, g_gl, g_xbc, g_qkv, g_dt_raw, jnp.zeros((t, PROJ_W - OFF_DT - LANE), BF16)], axis=1)
    gw_in_r = _matmul(u, g_proj, mode="tn", out_dtype=BF16, name="in_proj_bwd_w", tm=D_MODEL, tn=1792, tk=512)
    token = on_in_proj_grads(gw_in_r, jnp.concatenate([gcw0, gcw1, gcw2, gcw3], axis=0))
    g_u = mm(g_proj, w_in_r, mode="nt", out_dtype=F32, name="in_proj_bwd_x", tn=D_MODEL, tk=1792, after=token)
    g_x, gw_pre = _pre_norm_bwd(g_h1, g_u, x, w_pre)

    grads = dict(
        norm_mix_pre_w=gw_pre, b_gate=g_b_gate, conv_b=g_conv_b, dt_bias=g_dt_bias[:, :SSM_HEADS], a_log=g_a_log[:, :SSM_HEADS],
        d_skip=g_d_skip[:, :SSM_HEADS], ssm_norm_w=g_norm_w, norm_mix_post_w=gw_post, norm_ffn_pre_w=gw_fpre, norm_ffn_post_w=gw_fpost)
    return loss, g_x, grads


def _mesh_pos():
    return lax.axis_index("x"), lax.axis_index("y"), lax.axis_index("c")


def _all_gather(shards):
    n = len(shards)

    def body(*refs):
        x_refs, o_refs = refs[:n], refs[n:2 * n]
        send_sems, recv_sems, local_sems = refs[2 * n:]
        x, y, c = _mesh_pos()
        me, sibling = (x, y, c), (x, y, 1 - c)
        chips = [(1 - x, y), (x, 1 - y), (1 - x, 1 - y)]

        def copy(a, k, block, to, src=None):
            dst = o_refs[a].at[4 * block[0] + 2 * block[1] + block[2]]
            return pltpu.make_async_remote_copy(
                src_ref=dst if src is None else src, dst_ref=dst, send_sem=send_sems.at[7 * a + k], recv_sem=recv_sems.at[7 * a + k],
                device_id=to, device_id_type=pl.DeviceIdType.MESH)

        mine = [pltpu.make_async_copy(x_refs[a], o_refs[a].at[4 * x + 2 * y + c], local_sems.at[a]) for a in range(n)]
        for cp in mine:
            cp.start()
        first = []
        for a in range(n):
            first.append(copy(a, 0, me, sibling, src=x_refs[a]))
            first += [copy(a, 1 + j, me, (*chip, c), src=x_refs[a]) for j, chip in enumerate(chips)]
        for cp in first:
            cp.start()
        passed = []
        for j, chip in enumerate(chips):
            for a in range(n):
                copy(a, 1 + j, (*chip, c), me).wait_recv()
                passed.append(copy(a, 4 + j, (*chip, c), sibling))
                passed[-1].start()
        for a in range(n):
            copy(a, 0, sibling, me).wait_recv()
            for j, chip in enumerate(chips):
                copy(a, 4 + j, (*chip, 1 - c), me).wait_recv()
        for cp in first + passed:
            cp.wait_send()
        for cp in mine:
            cp.wait()

    hbm = pl.BlockSpec(memory_space=pltpu.HBM)
    return pl.pallas_call(
        body, out_shape=[S((N_DEV,) + s.shape, s.dtype) for s in shards],
        in_specs=[hbm] * n, out_specs=[hbm] * n,
        scratch_shapes=[pltpu.SemaphoreType.DMA((7 * n,)), pltpu.SemaphoreType.DMA((7 * n,)), pltpu.SemaphoreType.DMA((n,))],
        name="weights_all_gather",
    )(*shards)


def _exchange_grads(slab_arrays, small):
    n = len(slab_arrays)
    r_small = small.shape[0]

    def body(*refs):
        slab_refs, small_ref = refs[:n], refs[n]
        recv_refs, gsm_ref = refs[n + 1:2 * n + 1], refs[2 * n + 1]
        send_sems, recv_sems, local_sems = refs[2 * n + 2:]
        x, y, c = _mesh_pos()
        me = 4 * x + 2 * y + c

        def peer(k):
            px = 1 - x if k & 4 else x
            py = 1 - y if k & 2 else y
            pc = 1 - c if k & 1 else c
            return (px, py, pc), 4 * px + 2 * py + pc

        def copy(a, k, sending):
            to, lin = peer(k)
            sem = 7 * a + k - 1
            if a == n:
                src, dst = small_ref, gsm_ref.at[me if sending else lin]
            else:
                src, dst = slab_refs[a].at[lin], recv_refs[a].at[me if sending else lin]
            return pltpu.make_async_remote_copy(src_ref=src, dst_ref=dst, send_sem=send_sems.at[sem], recv_sem=recv_sems.at[sem],
                                                device_id=to, device_id_type=pl.DeviceIdType.MESH)

        own = [pltpu.make_async_copy(slab_refs[a].at[me], recv_refs[a].at[me], local_sems.at[a]) for a in range(n)]
        own.append(pltpu.make_async_copy(small_ref, gsm_ref.at[me], local_sems.at[n]))
        for cp in own:
            cp.start()
        order = [n] + list(range(n))
        sends = [copy(a, k, True) for a in order for k in range(1, N_DEV)]
        for cp in sends:
            cp.start()
        for a in order:
            for k in range(1, N_DEV):
                copy(a, k, False).wait_recv()
        for cp in sends:
            cp.wait_send()
        for cp in own:
            cp.wait()

    hbm = pl.BlockSpec(memory_space=pltpu.HBM)
    n_sem = 7 * (n + 1)
    res = pl.pallas_call(
        body, out_shape=[S(a.shape, a.dtype) for a in slab_arrays] + [S((N_DEV, r_small, LANE), small.dtype)],
        in_specs=[hbm] * (n + 1), out_specs=[hbm] * (n + 1),
        scratch_shapes=[pltpu.SemaphoreType.DMA((n_sem,)), pltpu.SemaphoreType.DMA((n_sem,)), pltpu.SemaphoreType.DMA((n + 1,))],
        name="grad_exchange",
    )(*slab_arrays, small)
    return res[:n], res[n]


def _peer_of(k, x, y, c):
    px = 1 - x if k & 4 else x
    py = 1 - y if k & 2 else y
    pc = 1 - c if k & 1 else c
    return (px, py, pc), 4 * px + 2 * py + pc


def _split_copies(src_refs, land_refs, send_sems, recv_sems, per_peer):
    x, y, c = _mesh_pos()
    me = 4 * x + 2 * y + c
    sends, recvs = [], []
    for a, (src, land) in enumerate(zip(src_refs, land_refs)):
        for k in range(1, N_DEV):
            to, lin = _peer_of(k, x, y, c)
            sem = 7 * a + k - 1
            piece = src.at[lin] if per_peer else src
            for slot, out in ((me, sends), (lin, recvs)):
                out.append(pltpu.make_async_remote_copy(
                    src_ref=piece, dst_ref=land.at[slot], send_sem=send_sems.at[sem], recv_sem=recv_sems.at[sem],
                    device_id=to, device_id_type=pl.DeviceIdType.MESH))
    return sends, recvs


def _remote_start(srcs, per_peer, name):
    n = len(srcs)
    lands = [lax.empty((N_DEV,) + (s.shape[1:] if per_peer else s.shape), s.dtype) for s in srcs]

    def body(*refs):
        src_refs, land_refs = refs[:n], refs[n:2 * n]
        send_sems, recv_sems = refs[2 * n], refs[2 * n + 1]
        token = refs[-1]
        sends, _ = _split_copies(src_refs, land_refs, send_sems, recv_sems, per_peer)
        for cp in sends:
            cp.start()
        token[...] = jnp.zeros_like(token)

    hbm = pl.BlockSpec(memory_space=pltpu.HBM)
    sem = pl.BlockSpec(memory_space=pltpu.SEMAPHORE)
    res = pl.pallas_call(
        body, name=name,
        out_shape=(pltpu.SemaphoreType.DMA((7 * n,)), pltpu.SemaphoreType.DMA((7 * n,)),
                   *[pltpu.HBM(a.shape, a.dtype) for a in srcs + lands], S((8, LANE), F32)),
        in_specs=[hbm] * (2 * n), out_specs=(sem, sem, *[hbm] * (2 * n), pl.BlockSpec(memory_space=pltpu.VMEM)),
        input_output_aliases={i: 2 + i for i in range(2 * n)},
        compiler_params=pltpu.CompilerParams(has_side_effects=pltpu.SideEffectType.DATAFLOW_SIDE_EFFECTING),
    )(*[pltpu.with_memory_space_constraint(a, pltpu.HBM) for a in srcs + lands])
    return dict(sems=res[:2], srcs=list(res[2:2 + n]), lands=list(res[2 + n:2 + 2 * n]), per_peer=per_peer), res[-1]


def _remote_wait(handle, after, name):
    n = len(handle["srcs"])
    per_peer = handle["per_peer"]

    def body(*refs):
        src_refs, land_refs = refs[:n], refs[n:2 * n]
        send_sems, recv_sems = refs[2 * n], refs[2 * n + 1]
        sends, recvs = _split_copies(src_refs, land_refs, send_sems, recv_sems, per_peer)
        for cp in sends:
            cp.wait_send()
        for cp in recvs:
            cp.wait_recv()

    hbm = pl.BlockSpec(memory_space=pltpu.HBM)
    sem = pl.BlockSpec(memory_space=pltpu.SEMAPHORE)
    arrays = handle["srcs"] + handle["lands"]
    res = pl.pallas_call(
        body, name=name, out_shape=tuple(pltpu.HBM(a.shape, a.dtype) for a in arrays),
        in_specs=[hbm] * (2 * n) + [sem, sem, pl.BlockSpec(memory_space=pl.ANY)], out_specs=tuple([hbm] * (2 * n)),
        input_output_aliases={i: i for i in range(2 * n)},
        compiler_params=pltpu.CompilerParams(has_side_effects=pltpu.SideEffectType.DATAFLOW_SIDE_EFFECTING),
    )(*arrays, *handle["sems"], after)
    return list(res[n:])


def _with_own(lands, own, me):
    return [lax.dynamic_update_index_in_dim(land, o.astype(land.dtype), me, 0) for land, o in zip(lands, own)]


def _adamw(w, m, v, slabs, name, tr):
    r, cols = w.shape
    c1 = 1.0 - ADAM_B1 ** ADAM_STEP
    c2 = 1.0 - ADAM_B2 ** ADAM_STEP

    def body(w_ref, m_ref, v_ref, s_ref, g_ref, d_ref, nm_ref, nv_ref):
        g = s_ref[0].astype(F32)
        for d in range(1, N_DEV):
            g = g + s_ref[d].astype(F32)
        nm = ADAM_B1 * m_ref[...] + (1.0 - ADAM_B1) * g
        nv = ADAM_B2 * v_ref[...] + (1.0 - ADAM_B2) * (g * g)
        g_ref[...] = g
        nm_ref[...] = nm
        nv_ref[...] = nv
        d_ref[...] = -ADAM_LR * ((nm / c1) / (jnp.sqrt(nv / c2) + ADAM_EPS) + ADAM_WD * w_ref[...])

    assert r % tr == 0, name
    blk = pl.BlockSpec((tr, cols), lambda i: (i, 0))
    return pl.pallas_call(
        body, grid=(r // tr,), in_specs=[blk, blk, blk, pl.BlockSpec((N_DEV, tr, cols), lambda i: (0, i, 0))],
        out_specs=[blk] * 4, out_shape=[S((r, cols), F32)] * 4, name=name, compiler_params=_params(("parallel",)),
    )(w, m, v, slabs)


BIG = ("w_in", "w_att_proj", "w_up", "w_ssm_proj", "w_out", "w_down", "conv_w")
ADAMW_ROWS = dict(w_in=256, w_att_proj=768, w_up=512, w_ssm_proj=256, w_out=128, w_down=256, conv_w=4)
SMALL = ("norm_mix_pre_w", "b_gate", "conv_b", "dt_bias", "a_log", "d_skip", "ssm_norm_w", "norm_mix_post_w",
         "norm_ffn_pre_w", "norm_ffn_post_w")
ORDER = ("norm_mix_pre_w", "w_in", "b_gate", "conv_w", "conv_b", "dt_bias", "a_log", "d_skip", "ssm_norm_w", "w_att_proj",
         "w_ssm_proj", "w_out", "norm_mix_post_w", "norm_ffn_pre_w", "w_up", "w_down", "norm_ffn_post_w")
ROW_SHARDED = ("w_ssm_proj", "w_out", "w_down")
LATE = ("w_att_proj", "w_ssm_proj", "w_out", "w_up", "w_down")
IN_PROJ_W = 10528
IN_SHARD_W = IN_PROJ_W // N_DEV
IN_SEGMENTS = ((2304, 4352), (8480, 10528), (4352, 8448), (0, 2304), (8448, 8480))


def _pack(parts, rows_multiple):
    flat = jnp.concatenate([p.reshape(-1) for p in parts])
    pad = (-flat.shape[0]) % (rows_multiple * LANE)
    return jnp.pad(flat, (0, pad)).reshape(-1, LANE)


def _unpack(flat2d, shapes):
    flat, out, off = flat2d.reshape(-1), [], 0
    for sh in shapes:
        n = int(np.prod(sh))
        out.append(flat[off:off + n].reshape(sh))
        off += n
    return out


def _reorder_in_proj(w):
    qkv, z, xbc = w[:, :2304], w[:, 2304:4352], w[:, 4352:8448]
    dt, gate = w[:, 8448:8480], w[:, 8480:10528]
    return jnp.concatenate([z, gate, xbc, qkv, dt, jnp.zeros((w.shape[0], PROJ_W - 10528), w.dtype)], axis=1)


def _restore_in_proj(wr):
    return jnp.concatenate([wr[:, OFF_QKV:OFF_QKV + 2304], wr[:, OFF_Z:OFF_Z + 2048], wr[:, OFF_XBC:OFF_XBC + 4096],
                            wr[:, OFF_DT:OFF_DT + 32], wr[:, OFF_GL:OFF_GL + 2048]], axis=1)


def _assemble_in_proj(g):
    pieces = []
    for lo, hi in IN_SEGMENTS:
        while lo < hi:
            d = lo // IN_SHARD_W
            end = min(hi, (d + 1) * IN_SHARD_W)
            pieces.append(g[d][:, lo - d * IN_SHARD_W:end - d * IN_SHARD_W])
            lo = end
    pieces.append(jnp.zeros((g.shape[1], PROJ_W - IN_PROJ_W), g.dtype))
    return jnp.concatenate(pieces, axis=1)


def _in_proj_slabs(wr):
    orig = _restore_in_proj(wr)
    return jnp.stack([orig[:, d * IN_SHARD_W:(d + 1) * IN_SHARD_W] for d in range(N_DEV)])


def kernel(x, norm_mix_pre_w, w_in, b_gate, conv_w, conv_b, dt_bias, a_log, d_skip, ssm_norm_w, w_att_proj, w_ssm_proj, w_out, norm_mix_post_w, norm_ffn_pre_w, w_up, w_down, norm_ffn_post_w, loss_target, m_norm_mix_pre_w, m_w_in, m_b_gate, m_conv_w, m_conv_b, m_dt_bias, m_a_log, m_d_skip, m_ssm_norm_w, m_w_att_proj, m_w_ssm_proj, m_w_out, m_norm_mix_post_w, m_norm_ffn_pre_w, m_w_up, m_w_down, m_norm_ffn_post_w, v_norm_mix_pre_w, v_w_in, v_b_gate, v_conv_w, v_conv_b, v_dt_bias, v_a_log, v_d_skip, v_ssm_norm_w, v_w_att_proj, v_w_ssm_proj, v_w_out, v_norm_mix_post_w, v_norm_ffn_pre_w, v_w_up, v_w_down, v_norm_ffn_post_w):
    w = dict(norm_mix_pre_w=norm_mix_pre_w, w_in=w_in, b_gate=b_gate, conv_w=conv_w, conv_b=conv_b, dt_bias=dt_bias, a_log=a_log,
             d_skip=d_skip, ssm_norm_w=ssm_norm_w, w_att_proj=w_att_proj, w_ssm_proj=w_ssm_proj, w_out=w_out,
             norm_mix_post_w=norm_mix_post_w, norm_ffn_pre_w=norm_ffn_pre_w, w_up=w_up, w_down=w_down, norm_ffn_post_w=norm_ffn_post_w)
    m = dict(norm_mix_pre_w=m_norm_mix_pre_w, w_in=m_w_in, b_gate=m_b_gate, conv_w=m_conv_w, conv_b=m_conv_b, dt_bias=m_dt_bias,
             a_log=m_a_log, d_skip=m_d_skip, ssm_norm_w=m_ssm_norm_w, w_att_proj=m_w_att_proj, w_ssm_proj=m_w_ssm_proj, w_out=m_w_out,
             norm_mix_post_w=m_norm_mix_post_w, norm_ffn_pre_w=m_norm_ffn_pre_w, w_up=m_w_up, w_down=m_w_down, norm_ffn_post_w=m_norm_ffn_post_w)
    v = dict(norm_mix_pre_w=v_norm_mix_pre_w, w_in=v_w_in, b_gate=v_b_gate, conv_w=v_conv_w, conv_b=v_conv_b, dt_bias=v_dt_bias,
             a_log=v_a_log, d_skip=v_d_skip, ssm_norm_w=v_ssm_norm_w, w_att_proj=v_w_att_proj, w_ssm_proj=v_w_ssm_proj, w_out=v_w_out,
             norm_mix_post_w=v_norm_mix_post_w, norm_ffn_pre_w=v_norm_ffn_pre_w, w_up=v_w_up, w_down=v_w_down, norm_ffn_post_w=v_norm_ffn_post_w)
    shard_shapes = {n: w[n].shape[1:] for n in ORDER}

    mx, my, mc = _mesh_pos()
    me = 4 * mx + 2 * my + mc

    g_in, g_conv = _all_gather([w["w_in"][0].astype(BF16), w["conv_w"][0]])
    conv_full = jnp.moveaxis(g_conv, 0, 1).reshape(SSM_CONV, CONV_DIM)
    late_shards = [w[n][0].astype(BF16) for n in LATE]
    late_handle, token = _remote_start(late_shards, False, "late_weights_start")
    w_pre = w["norm_mix_pre_w"] + jnp.tile(token[0:1, :], (1, D_MODEL // LANE))

    def late_weights(after):
        full = dict(zip(LATE, _with_own(_remote_wait(late_handle, after, "late_weights_wait"), late_shards, me)))
        for n in ROW_SHARDED:
            full[n] = full[n].reshape(-1, full[n].shape[2])
        w_att = jnp.moveaxis(full["w_att_proj"], 0, 1).reshape(ATT_WIDTH, D_MODEL)
        return w_att, full["w_ssm_proj"], full["w_out"], full["w_up"], full["w_down"]

    started = {}

    def start_exchange(tag, slabs):
        own = [lax.dynamic_index_in_dim(s, me, 0, keepdims=False) for s in slabs]
        handle, tok = _remote_start(slabs, True, tag + "_grads_start")
        started[tag] = (handle, own)
        return tok

    def on_mid_grads(g):
        slabs = dict(w_up=g["w_up"], w_att_proj=jnp.moveaxis(g["w_att_proj"].reshape(ATT_WIDTH, N_DEV, -1), 1, 0))
        for n in ROW_SHARDED:
            slabs[n] = g[n].reshape(N_DEV, -1, g[n].shape[1])
        return start_exchange("mid", [slabs[n] for n in LATE])

    def on_in_proj_grads(gw_in_r, g_conv_w):
        return start_exchange("in_proj", [_in_proj_slabs(gw_in_r), jnp.moveaxis(g_conv_w.reshape(SSM_CONV, N_DEV, -1), 1, 0)])

    loss, g_x, grads = _local_step(
        x[0], loss_target[0], w_pre, _assemble_in_proj(g_in), w["b_gate"], conv_full, w["conv_b"], w["dt_bias"], w["a_log"],
        w["d_skip"], w["ssm_norm_w"], late_weights, w["norm_mix_post_w"], w["norm_ffn_pre_w"], w["norm_ffn_post_w"],
        on_mid_grads, on_in_proj_grads)

    recv = {}
    for tag, names in (("mid", LATE), ("in_proj", ("w_in", "conv_w"))):
        handle, own = started[tag]
        recv.update(zip(names, _with_own(_remote_wait(handle, g_x, tag + "_grads_wait"), own, me)))
    small = _pack([grads[n].astype(F32) for n in SMALL], 8)
    _, small_all = _exchange_grads([], small)

    small_shapes = [shard_shapes[n] for n in SMALL]
    small_out = _adamw(*[_pack([d_[n][0] for n in SMALL], 8) for d_ in (w, m, v)], small_all, "adamw_replicated", small_all.shape[1])
    big_out = {n: _adamw(w[n][0], m[n][0], v[n][0], recv[n], "adamw_" + n, ADAMW_ROWS[n]) for n in BIG}
    res = []
    for which, small_flat in enumerate(small_out):
        vals = {n: big_out[n][which] for n in BIG}
        vals.update(zip(SMALL, _unpack(small_flat, small_shapes)))
        res.append([vals[n][None] for n in ORDER])
    g_out, d_out, m_out, v_out = res
    total = lax.psum(loss[0, 0], ("x", "y", "c"))
    return (total, g_x[None], *g_out, *d_out, *m_out, *v_out)
```

```python
import functools
import math

import jax
import jax.numpy as jnp
import numpy as np
from jax import lax
from jax.experimental import pallas as pl
from jax.experimental.pallas import tpu as pltpu

F32 = jnp.float32
BF16 = jnp.bfloat16

D_MODEL = 1024
HEAD_DIM = 64
N_ATT_HEADS = 12
ATT_WIDTH = N_ATT_HEADS * HEAD_DIM
DILATED_PATTERNS = ((128, 1), (512, 4), (2048, 16))
ATT_BLOCK = 128
SSM_INNER = 2048
SSM_HEAD_DIM = 64
SSM_HEADS = 32
SSM_GROUPS = 8
SSM_STATE = 128
SSM_CHUNK = 128
CONV_DIM = 4096
SSM_CONV = 4
FFN_HIDDEN = 4096
RMS_EPS = 1e-6
N_DEV = 8

ADAM_LR = 0.001
ADAM_B1 = 0.9
ADAM_B2 = 0.999
ADAM_EPS = 1e-08
ADAM_WD = 0.01
ADAM_STEP = 10

LANE = 128
OFF_Z, OFF_GL, OFF_XBC, OFF_QKV, OFF_DT = 0, 2048, 4096, 8192, 10496
PROJ_W = 10752
PROJ_BLOCKS = PROJ_W // LANE
VMEM_LIMIT = 52 * 1024 * 1024
NEG = -1e30

HI = lax.Precision.HIGHEST
NT_DIMS = (((1,), (1,)), ((), ()))
TN_DIMS = (((0,), (0,)), ((), ()))
S = jax.ShapeDtypeStruct


def _params(sem):
    return pltpu.CompilerParams(dimension_semantics=sem, vmem_limit_bytes=VMEM_LIMIT)


def _matmul(a, b, *, mode, out_dtype, name, tm, tn, tk, epilogue=None, extra=None, stacked=False, after=None):
    if mode == "nn":
        m, k = a.shape
        n = b.shape[0] * b.shape[2] if stacked else b.shape[1]
        a_spec = pl.BlockSpec((tm, tk), lambda i, j, kk: (i, kk))
        b_spec = pl.BlockSpec((None, tk, tn), lambda i, j, kk: (j, kk, 0)) if stacked else pl.BlockSpec((tk, tn), lambda i, j, kk: (kk, j))
        dims = (((1,), (0,)), ((), ()))
    elif mode == "nt":
        m, k = a.shape
        n = b.shape[1] if stacked else b.shape[0]
        a_spec = pl.BlockSpec((tm, tk), lambda i, j, kk: (i, kk))
        b_spec = pl.BlockSpec((None, tn, tk), lambda i, j, kk: (kk, j, 0)) if stacked else pl.BlockSpec((tn, tk), lambda i, j, kk: (j, kk))
        dims = NT_DIMS
    else:
        (k, m), n = a.shape, b.shape[1]
        a_spec = pl.BlockSpec((tk, tm), lambda i, j, kk: (kk, i))
        b_spec = pl.BlockSpec((tk, tn), lambda i, j, kk: (kk, j))
        dims = TN_DIMS
    assert m % tm == 0 and n % tn == 0 and k % tk == 0, (name, m, n, k)
    if stacked:
        assert (tk if mode == "nt" else tn) * N_DEV == (k if mode == "nt" else n), name
    nk = k // tk
    o_spec = pl.BlockSpec((tm, tn), lambda i, j, kk: (i, j))
    in_specs, args = [a_spec, b_spec], [a, b]
    if epilogue == "relu2":
        out_shape = (S((m, n), BF16), S((m, n), BF16))
        out_specs = (o_spec, o_spec)
    elif stacked and mode == "tn":
        out_shape, out_specs = S((N_DEV, m, tn), out_dtype), pl.BlockSpec((None, tm, tn), lambda i, j, kk: (j, i, 0))
    else:
        out_shape, out_specs = S((m, n), out_dtype), o_spec
    if epilogue == "relu2_bwd":
        in_specs.append(o_spec)
        args.append(extra)
    n_in = len(args)
    if after is not None:
        in_specs.append(pl.BlockSpec(after.shape, lambda i, j, kk: (0,) * after.ndim))
        args.append(after)

    def finish(acc, refs):
        if epilogue == "relu2":
            r = jnp.maximum(acc, 0.0)
            refs[0][...] = (r * r).astype(BF16)
            refs[1][...] = acc.astype(BF16)
        elif epilogue == "relu2_bwd":
            up = refs[0][...].astype(F32)
            refs[1][...] = (acc * (2.0 * jnp.maximum(up, 0.0))).astype(out_dtype)
        else:
            refs[0][...] = acc.astype(out_dtype)

    def body(a_ref, b_ref, *rest):
        rest = rest[:n_in - 2] + rest[len(args) - 2:]
        part = lax.dot_general(a_ref[...].astype(BF16), b_ref[...].astype(BF16), dims, preferred_element_type=F32)
        if nk == 1:
            finish(part, rest)
            return
        acc_ref = rest[-1]
        kk = pl.program_id(2)

        @pl.when(kk == 0)
        def _():
            acc_ref[...] = part

        @pl.when(kk > 0)
        def _():
            acc_ref[...] += part

        @pl.when(kk == nk - 1)
        def _():
            finish(acc_ref[...], rest[:-1])

    scratch = [] if nk == 1 else [pltpu.VMEM((tm, tn), F32)]
    return pl.pallas_call(
        body, grid=(m // tm, n // tn, nk), in_specs=in_specs, out_specs=out_specs, out_shape=out_shape,
        scratch_shapes=scratch, name=name, compiler_params=_params(("parallel", "parallel", "arbitrary")),
    )(*args)


def _rowcall(body, name, n_rows, tr, ins, outs, scratch=()):
    res = pl.pallas_call(
        body, grid=(n_rows // tr,),
        in_specs=[pl.BlockSpec(bs, im) for _, bs, im in ins],
        out_specs=[pl.BlockSpec(bs, im) for _, _, bs, im in outs],
        out_shape=[S(sh, dt) for sh, dt, _, _ in outs],
        scratch_shapes=list(scratch), name=name, compiler_params=_params(("arbitrary",)),
    )(*[a for a, _, _ in ins])
    return res


def _rows(arr, tr, width=None, cb=0):
    width = arr.shape[1] if width is None else width
    return (arr, (tr, width), lambda i, cb=cb: (i, cb))


def _whole(arr):
    nd = arr.ndim
    return (arr, arr.shape, lambda i, nd=nd: (0,) * nd)


def _orow(n_rows, width, dtype, tr):
    return ((n_rows, width), dtype, (tr, width), lambda i: (i, 0))


def _oacc(width):
    return ((1, width), F32, (1, width), lambda i: (0, 0))


def _accumulate(ref, value):
    first = pl.program_id(0) == 0

    @pl.when(first)
    def _():
        ref[...] = value

    @pl.when(jnp.logical_not(first))
    def _():
        ref[...] += value


def _colsum(v):
    return jnp.sum(v, axis=0, keepdims=True)


def _rms_fwd(x, w):
    r = lax.rsqrt(jnp.mean(x * x, axis=-1, keepdims=True) + RMS_EPS)
    return x * r * w


def _rms_bwd(gy, x, w):
    r = lax.rsqrt(jnp.mean(x * x, axis=-1, keepdims=True) + RMS_EPS)
    xn = x * r
    gxn = gy * w
    gx = r * (gxn - xn * jnp.mean(gxn * xn, axis=-1, keepdims=True))
    return gx, _colsum(gy * xn)


def _sigmoid(x):
    return 1.0 / (1.0 + jnp.exp(-x))


def _head_expand(n_heads_pad, n_heads, width):
    h = lax.broadcasted_iota(jnp.int32, (n_heads_pad, n_heads * width), 0)
    c = lax.broadcasted_iota(jnp.int32, (n_heads_pad, n_heads * width), 1)
    return (c // width == h).astype(F32)


def _head_reduce(n_heads, width, n_heads_pad):
    c = lax.broadcasted_iota(jnp.int32, (n_heads * width, n_heads_pad), 0)
    h = lax.broadcasted_iota(jnp.int32, (n_heads * width, n_heads_pad), 1)
    return (c // width == h).astype(F32)


def _block_ones(n, width):
    r = lax.broadcasted_iota(jnp.int32, (n, n), 0)
    c = lax.broadcasted_iota(jnp.int32, (n, n), 1)
    return (r // width == c // width).astype(F32)


def _pre_norm(x, w_pre, tr=512):
    t = x.shape[0]

    def body(x_ref, w_ref, u_ref):
        u_ref[...] = _rms_fwd(x_ref[...], w_ref[...]).astype(BF16)

    return _rowcall(body, "pre_norm", t, tr, [_rows(x, tr), _whole(w_pre)], [_orow(t, D_MODEL, BF16, tr)])[0]


def _conv_fwd(proj, conv_w, conv_b, tr=256):
    t = proj.shape[0]
    cb = OFF_XBC // CONV_DIM
    halo = (proj, (8, CONV_DIM), lambda i: (jnp.maximum(i * (tr // 8) - 1, 0), cb))

    def body(cur_ref, prev_ref, w_ref, b_ref, o_ref, xc_ref, ext):
        ext[pl.ds(0, 8), :] = jnp.where(pl.program_id(0) > 0, prev_ref[...], 0.0)
        ext[pl.ds(8, tr), :] = cur_ref[...]
        acc = b_ref[...] + w_ref[3:4, :] * cur_ref[...]
        for k in range(SSM_CONV - 1):
            acc = acc + w_ref[k:k + 1, :] * ext[pl.ds(8 - 3 + k, tr), :]
        o_ref[...] = acc * _sigmoid(acc)
        xc_ref[...] = acc.astype(BF16)

    return _rowcall(body, "conv_fwd", t, tr, [_rows(proj, tr, CONV_DIM, cb), halo, _whole(conv_w), _whole(conv_b)],
                    [_orow(t, CONV_DIM, F32, tr), _orow(t, CONV_DIM, BF16, tr)], scratch=[pltpu.VMEM((tr + 8, CONV_DIM), F32)])


def _dt_fwd(proj, dt_bias_pad, alog_pad, tr=512):
    t = proj.shape[0]

    def body(raw_ref, b_ref, al_ref, dtx_ref, csx_ref, cst_ref):
        v = raw_ref[...] + b_ref[...]
        dt = jnp.maximum(v, 0.0) + jnp.log1p(jnp.exp(-jnp.abs(v)))
        expand = _head_expand(LANE, SSM_HEADS, SSM_HEAD_DIM)
        dtx_ref[...] = _dot_split(dt, expand, 0, 3)
        la = dt * (-jnp.exp(al_ref[...]))
        row = lax.broadcasted_iota(jnp.int32, (SSM_CHUNK, SSM_CHUNK), 0)
        col = lax.broadcasted_iota(jnp.int32, (SSM_CHUNK, SSM_CHUNK), 1)
        tril = (col <= row).astype(F32)
        cs = jnp.concatenate([_dot_split(tril, la[k * SSM_CHUNK:(k + 1) * SSM_CHUNK, :], 1, 3) for k in range(tr // SSM_CHUNK)], axis=0)
        csx_ref[...] = _dot_split(cs, expand, 0, 3)
        cst_ref[...] = cs.T

    return _rowcall(body, "dt_fwd", t, tr, [_rows(proj, tr, LANE, OFF_DT // LANE), _whole(dt_bias_pad), _whole(alog_pad)],
                    [_orow(t, SSM_INNER, F32, tr), _orow(t, SSM_INNER, F32, tr), ((LANE, t), F32, (LANE, tr), lambda i: (0, i))])


def _gate_norm_fwd(y_ssd, xa, proj, dskip_x, norm_w, tr=256):
    t = y_ssd.shape[0]
    gw = SSM_INNER // SSM_GROUPS

    def body(y_ref, xs_ref, z_ref, d_ref, w_ref, o_ref):
        z = z_ref[...]
        y3 = (y_ref[...] + d_ref[...] * xs_ref[...]) * (z * _sigmoid(z))
        for g in range(SSM_GROUPS):
            sl = slice(g * gw, (g + 1) * gw)
            o_ref[:, sl] = _rms_fwd(y3[:, sl], w_ref[:, sl]).astype(BF16)

    return _rowcall(body, "gate_norm_fwd", t, tr,
                    [_rows(y_ssd, tr), _rows(xa, tr, SSM_INNER, 0), _rows(proj, tr, SSM_INNER, OFF_Z // SSM_INNER), _whole(dskip_x), _whole(norm_w)],
                    [_orow(t, SSM_INNER, BF16, tr)])[0]


def _gating_fwd(proj, b_gate, att_p, ssm_p, tr=512):
    t = proj.shape[0]

    def body(gl_ref, b_ref, a_ref, s_ref, o_ref):
        gates = _sigmoid(gl_ref[...] + b_ref[...])
        o_ref[...] = (gates[:, :D_MODEL] * a_ref[...] + gates[:, D_MODEL:] * s_ref[...]).astype(BF16)

    return _rowcall(body, "gating_fwd", t, tr, [_rows(proj, tr, 2 * D_MODEL, OFF_GL // (2 * D_MODEL)), _whole(b_gate), _rows(att_p, tr), _rows(ssm_p, tr)],
                    [_orow(t, D_MODEL, BF16, tr)])[0]


def _mix_post_ffn_pre(x, mixed, w_post, w_fpre, tr=512):
    t = x.shape[0]

    def body(x_ref, m_ref, wp_ref, wf_ref, h1_ref, f_ref):
        h1 = x_ref[...] + _rms_fwd(m_ref[...], wp_ref[...])
        h1_ref[...] = h1
        f_ref[...] = _rms_fwd(h1, wf_ref[...]).astype(BF16)

    return _rowcall(body, "mix_post_ffn_pre", t, tr, [_rows(x, tr), _rows(mixed, tr), _whole(w_post), _whole(w_fpre)],
                    [_orow(t, D_MODEL, F32, tr), _orow(t, D_MODEL, BF16, tr)])


def _loss_and_ffn_post_bwd(h1, dn, w_fpost, target, tr=512):
    t = h1.shape[0]

    def body(h1_ref, dn_ref, w_ref, tg_ref, loss_ref, gh2_ref, gdn_ref, gw_ref):
        dn = dn_ref[...]
        w = w_ref[...]
        err = h1_ref[...] + _rms_fwd(dn, w) - tg_ref[...]
        _accumulate(loss_ref, jnp.zeros((1, LANE), F32) + 0.5 * jnp.sum(jnp.mean(err * err, axis=-1, keepdims=True)))
        gh2 = err * (1.0 / D_MODEL)
        gh2_ref[...] = gh2
        gdn, gw = _rms_bwd(gh2, dn, w)
        gdn_ref[...] = gdn.astype(BF16)
        _accumulate(gw_ref, gw)

    return _rowcall(body, "loss_ffn_post_bwd", t, tr, [_rows(h1, tr), _rows(dn, tr), _whole(w_fpost), _rows(target, tr)],
                    [_oacc(LANE), _orow(t, D_MODEL, F32, tr), _orow(t, D_MODEL, BF16, tr), _oacc(D_MODEL)])


def _ffn_pre_mix_post_bwd(g_h2, g_f, h1, w_fpre, mixed, w_post, tr=512):
    t = h1.shape[0]

    def body(gh2_ref, gf_ref, h1_ref, wf_ref, m_ref, wp_ref, gh1_ref, gm_ref, gwf_ref, gwp_ref):
        gx, gwf = _rms_bwd(gf_ref[...], h1_ref[...], wf_ref[...])
        gh1 = gh2_ref[...] + gx
        gh1_ref[...] = gh1
        gm, gwp = _rms_bwd(gh1, m_ref[...], wp_ref[...])
        gm_ref[...] = gm.astype(BF16)
        _accumulate(gwf_ref, gwf)
        _accumulate(gwp_ref, gwp)

    return _rowcall(body, "ffn_pre_mix_post_bwd", t, tr,
                    [_rows(g_h2, tr), _rows(g_f, tr), _rows(h1, tr), _whole(w_fpre), _rows(mixed, tr), _whole(w_post)],
                    [_orow(t, D_MODEL, F32, tr), _orow(t, D_MODEL, BF16, tr), _oacc(D_MODEL), _oacc(D_MODEL)])


def _gating_bwd(g_mixin, proj, b_gate, att_p, ssm_p, tr=512):
    t = proj.shape[0]

    def body(gm_ref, gl_ref, b_ref, a_ref, s_ref, ga_ref, gs_ref, ggl_ref, gb_ref):
        gates = _sigmoid(gl_ref[...] + b_ref[...])
        gm = gm_ref[...]
        g_att, g_ssm = gates[:, :D_MODEL], gates[:, D_MODEL:]
        ga_ref[...] = (gm * g_att).astype(BF16)
        gs_ref[...] = (gm * g_ssm).astype(BF16)
        ggl_a = gm * a_ref[...] * g_att * (1.0 - g_att)
        ggl_s = gm * s_ref[...] * g_ssm * (1.0 - g_ssm)
        ggl_ref[:, :D_MODEL] = ggl_a.astype(BF16)
        ggl_ref[:, D_MODEL:] = ggl_s.astype(BF16)
        _accumulate(gb_ref.at[:, :D_MODEL], _colsum(ggl_a))
        _accumulate(gb_ref.at[:, D_MODEL:], _colsum(ggl_s))

    return _rowcall(body, "gating_bwd", t, tr,
                    [_rows(g_mixin, tr), _rows(proj, tr, 2 * D_MODEL, OFF_GL // (2 * D_MODEL)), _whole(b_gate), _rows(att_p, tr), _rows(ssm_p, tr)],
                    [_orow(t, D_MODEL, BF16, tr), _orow(t, D_MODEL, BF16, tr), _orow(t, 2 * D_MODEL, BF16, tr), _oacc(2 * D_MODEL)])


def _gate_norm_bwd(g_y4, y_ssd, xa, proj, dskip_x, norm_w, tr=256):
    t = y_ssd.shape[0]
    gw = SSM_INNER // SSM_GROUPS

    def body(g_ref, y_ref, xs_ref, z_ref, d_ref, w_ref, gy2_ref, gz_ref, gnw_ref, gdx_ref, gd_ref):
        z = z_ref[...]
        xs = xs_ref[...]
        sg = _sigmoid(z)
        sz = z * sg
        y2 = y_ref[...] + d_ref[...] * xs
        y3 = y2 * sz
        g4 = g_ref[...]
        for g in range(SSM_GROUPS):
            sl = slice(g * gw, (g + 1) * gw)
            gy3, gnw = _rms_bwd(g4[:, sl], y3[:, sl], w_ref[:, sl])
            _accumulate(gnw_ref.at[:, sl], gnw)
            gy2 = gy3 * sz[:, sl]
            gy2_ref[:, sl] = gy2
            gz_ref[:, sl] = (gy3 * y2[:, sl] * (sg[:, sl] * (1.0 + z[:, sl] * (1.0 - sg[:, sl])))).astype(BF16)
            _accumulate(gdx_ref.at[:, sl], _colsum(gy2 * xs[:, sl]))
        tot = jnp.broadcast_to(gdx_ref[...], (8, SSM_INNER))
        gd_ref[...] = jnp.dot(tot, _head_reduce(SSM_HEADS, SSM_HEAD_DIM, LANE), precision=HI, preferred_element_type=F32)[0:1, :]

    return _rowcall(body, "gate_norm_bwd", t, tr,
                    [_rows(g_y4, tr), _rows(y_ssd, tr), _rows(xa, tr, SSM_INNER, 0), _rows(proj, tr, SSM_INNER, OFF_Z // SSM_INNER), _whole(dskip_x), _whole(norm_w)],
                    [_orow(t, SSM_INNER, F32, tr), _orow(t, SSM_INNER, BF16, tr), _oacc(SSM_INNER), _oacc(SSM_INNER), _oacc(LANE)])


def _dt_bwd(g_dtx, ga_rows, proj, dt_bias_pad, tr=512):
    t = proj.shape[0]

    def body(g_ref, ga_ref, raw_ref, b_ref, o_ref, gb_ref, gal_ref):
        red = _head_reduce(SSM_HEADS, SSM_HEAD_DIM, LANE)
        gdt = _dot_split(g_ref[...], red, 0, 3)
        graw = gdt * _sigmoid(raw_ref[...] + b_ref[...])
        o_ref[...] = graw.astype(BF16)
        _accumulate(gb_ref, _colsum(graw))
        tot = jnp.broadcast_to(_colsum(ga_ref[...]), (8, SSM_INNER))
        gal_ref[...] = jnp.dot(tot, red, precision=HI, preferred_element_type=F32)[0:1, :]

    return _rowcall(body, "dt_bwd", t, tr, [_rows(g_dtx, tr), _whole(ga_rows), _rows(proj, tr, LANE, OFF_DT // LANE), _whole(dt_bias_pad)],
                    [_orow(t, LANE, BF16, tr), _oacc(LANE), _oacc(LANE)])


def _conv_bwd(g_xs, g_b, g_c, xc, proj, conv_w, tr=256):
    t = proj.shape[0]
    n_blk = t // tr
    cb = OFF_XBC // CONV_DIM
    nb, nc = SSM_INNER, SSM_INNER + SSM_GROUPS * SSM_STATE
    halo_rows = 16

    def nxt(arr, width):
        return (arr, (halo_rows, width), lambda i: (jnp.minimum((i + 1) * (tr // halo_rows), t // halo_rows - 1), 0))

    def body(gxs_ref, gxs_n, gb_ref, gb_n, gc_ref, gc_n, xc_ref, xc_n, x_ref, w_ref, o_ref, gcb_ref, gw0, gw1, gw2, gw3, ext):
        def store_gxc(rows, n_rows, gxs, gb, gc, xc, keep):
            xcf = xc[...].astype(F32)
            sg = _sigmoid(xcf)
            dsilu = jnp.where(keep, sg * (1.0 + xcf * (1.0 - sg)), 0.0)
            ext[pl.ds(rows, n_rows), :nb] = gxs[...] * dsilu[:, :nb]
            ext[pl.ds(rows, n_rows), nb:nc] = gb[...] * dsilu[:, nb:nc]
            ext[pl.ds(rows, n_rows), nc:] = gc[...] * dsilu[:, nc:]

        store_gxc(0, tr, gxs_ref, gb_ref, gc_ref, xc_ref, True)
        store_gxc(tr, halo_rows, gxs_n, gb_n, gc_n, xc_n, pl.program_id(0) < n_blk - 1)
        x = x_ref[...]
        acc = None
        for k, gw in enumerate((gw0, gw1, gw2, gw3)):
            shifted = ext[pl.ds(3 - k, tr), :]
            term = w_ref[k:k + 1, :] * shifted
            acc = term if acc is None else acc + term
            _accumulate(gw, _colsum(shifted * x))
            if k == SSM_CONV - 1:
                _accumulate(gcb_ref, _colsum(shifted))
        o_ref[...] = acc.astype(BF16)

    ins = []
    for arr, width in ((g_xs, SSM_INNER), (g_b, nc - nb), (g_c, nc - nb), (xc, CONV_DIM)):
        ins += [_rows(arr, tr), nxt(arr, width)]
    ins += [_rows(proj, tr, CONV_DIM, cb), _whole(conv_w)]
    return _rowcall(body, "conv_bwd", t, tr, ins, [_orow(t, CONV_DIM, BF16, tr)] + [_oacc(CONV_DIM)] * 5,
                    scratch=[pltpu.VMEM((tr + halo_rows, CONV_DIM), F32)])


def _pre_norm_bwd(g_h1, g_u, x, w_pre, tr=512):
    t = x.shape[0]

    def body(gh_ref, gu_ref, x_ref, w_ref, gx_ref, gw_ref):
        gx, gw = _rms_bwd(gu_ref[...], x_ref[...], w_ref[...])
        gx_ref[...] = gh_ref[...] + gx
        _accumulate(gw_ref, gw)

    return _rowcall(body, "pre_norm_bwd", t, tr, [_rows(g_h1, tr), _rows(g_u, tr), _rows(x, tr), _whole(w_pre)],
                    [_orow(t, D_MODEL, F32, tr), _oacc(D_MODEL)])


def _alibi_slopes(n):
    def pow2(m):
        start = 2.0 ** (-8.0 / m)
        return [start ** (i + 1) for i in range(m)]
    if (n & (n - 1)) == 0:
        s = pow2(n)
    else:
        c = 2 ** int(math.floor(math.log2(n)))
        s = pow2(c) + pow2(2 * c)[0::2][: n - c]
    return np.array(s, dtype=np.float32)


def _slope_rows():
    s = _alibi_slopes(N_ATT_HEADS).reshape(N_ATT_HEADS // 2, 2)
    return jnp.asarray(np.broadcast_to(np.repeat(s, HEAD_DIM, axis=1)[:, None, :], (N_ATT_HEADS // 2, 8, LANE)).copy())


ATT_MAX_BLOCK_ROWS = 2048


RESIDUE_MAJOR_FROM = 16


class _AttLayout:
    def __init__(self, t, dil):
        self.t, self.dil = t, dil
        self.rows = t // dil
        self.residue_major = dil >= RESIDUE_MAJOR_FROM
        if self.residue_major:
            bq, self.stride = min(512, self.rows), 1
        else:
            bq, self.stride = min(512, self.rows, ATT_MAX_BLOCK_ROWS // dil), dil
        self.nsub = bq // ATT_BLOCK
        self.nblk = self.rows // bq
        self.rb = bq * self.stride
        self.pb = ATT_BLOCK * self.stride
        self.n_pb = self.rows * self.stride // self.pb

    def qkv(self, proj):
        if self.residue_major:
            qkv = proj[:, OFF_QKV:OFF_QKV + 3 * ATT_WIDTH]
            return qkv.reshape(self.rows, self.dil * 3 * ATT_WIDTH), 3 * ATT_WIDTH // LANE, 0
        return proj, 0, OFF_QKV // LANE

    def act(self, a):
        return a.reshape(self.rows, self.dil * ATT_WIDTH) if self.residue_major else a

    def act_shape(self):
        return (self.rows, self.dil * ATT_WIDTH) if self.residue_major else (self.t, ATT_WIDTH)

    def col(self, r, band, c):
        return r * band + c if self.residue_major else c


def _residue_rows(r, stride, first_block, n_blocks=1):
    if stride == 1:
        return pl.ds(first_block * ATT_BLOCK, n_blocks * ATT_BLOCK)
    return pl.ds(r + first_block * ATT_BLOCK * stride, n_blocks * ATT_BLOCK, stride=stride)


def _lane_half():
    return lax.broadcasted_iota(jnp.int32, (ATT_BLOCK, LANE), 1) // HEAD_DIM


def _att_scores_mask(dil, first):
    iq = lax.broadcasted_iota(jnp.int32, (ATT_BLOCK, 2 * ATT_BLOCK), 0)
    jk = lax.broadcasted_iota(jnp.int32, (ATT_BLOCK, 2 * ATT_BLOCK), 1)
    dist = ATT_BLOCK + iq - jk
    valid = (dist >= 0) & (dist <= ATT_BLOCK) & (jnp.logical_not(first) | (jk >= ATT_BLOCK))
    return (dist * dil).astype(F32), valid


def _att_fwd(proj, dil, slopes):
    t = proj.shape[0]
    lay = _AttLayout(t, dil)
    nsub, nblk, rb, pb = lay.nsub, lay.nblk, lay.rb, lay.pb
    src, band, qb = lay.qkv(proj)
    aw = ATT_WIDTH // LANE

    def spec(off, prev=False):
        if prev:
            return pl.BlockSpec((pb, LANE), lambda hp, i, r: (jnp.maximum(i * nsub - 1, 0), lay.col(r, band, qb + off + hp)))
        return pl.BlockSpec((rb, LANE), lambda hp, i, r: (i, lay.col(r, band, qb + off + hp)))

    o_spec = pl.BlockSpec((rb, LANE), lambda hp, i, r: (i, lay.col(r, aw, hp)))

    def body(q_ref, kc_ref, kp_ref, vc_ref, vp_ref, sl_ref, o_ref, lse_ref):
        i, r = pl.program_id(1), pl.program_id(2)
        half = _lane_half()
        for sub in range(nsub):
            rs = _residue_rows(r, lay.stride, sub)
            q = (q_ref[rs, :] * (HEAD_DIM ** -0.5)).astype(BF16)
            if sub == 0:
                r0 = _residue_rows(r, lay.stride, 0)
                kk = jnp.concatenate([kp_ref[r0, :], kc_ref[rs, :]], axis=0).astype(BF16)
                vv = jnp.concatenate([vp_ref[r0, :], vc_ref[rs, :]], axis=0).astype(BF16)
                first = i == 0
            else:
                ks = _residue_rows(r, lay.stride, sub - 1, 2)
                kk, vv = kc_ref[ks, :].astype(BF16), vc_ref[ks, :].astype(BF16)
                first = jnp.bool_(False)
            dist, valid = _att_scores_mask(dil, first)
            outs, lses = [], []
            for e in range(2):
                qe = jnp.where(half == e, q, jnp.zeros_like(q))
                s = lax.dot_general(qe, kk, NT_DIMS, preferred_element_type=F32)
                s = s + jnp.where(valid, -sl_ref[0:1, e * HEAD_DIM:e * HEAD_DIM + 1] * dist, NEG)
                m = jnp.max(s, axis=-1, keepdims=True)
                p = jnp.exp(s - m)
                l = jnp.sum(p, axis=-1, keepdims=True)
                outs.append(jnp.dot(p.astype(BF16), vv, preferred_element_type=F32) / l)
                lses.append(m + jnp.log(l))
            o_ref[rs, :] = jnp.where(half == 0, outs[0], outs[1])
            lse_ref[rs, :] = jnp.where(half == 0, lses[0], lses[1])

    o, lse = pl.pallas_call(
        body, grid=(N_ATT_HEADS // 2, nblk, dil),
        in_specs=[spec(0), spec(6), spec(6, True), spec(12), spec(12, True), pl.BlockSpec((None, 8, LANE), lambda hp, i, r: (hp, 0, 0))],
        out_specs=[o_spec, o_spec], out_shape=[S(lay.act_shape(), F32)] * 2,
        name=f"att_fwd_d{dil}", compiler_params=_params(("parallel", "parallel", "arbitrary")),
    )(src, src, src, src, src, slopes)
    return o.reshape(t, ATT_WIDTH), lse.reshape(t, ATT_WIDTH)


def _att_combine(outs, lses, tr=512):
    t = outs[0].shape[0]

    def body(o0, o1, o2, l0, l1, l2, att_ref, lse_ref):
        ls = [l0[...], l1[...], l2[...]]
        m = jnp.maximum(jnp.maximum(ls[0], ls[1]), ls[2])
        ws = [jnp.exp(l - m) for l in ls]
        tot = ws[0] + ws[1] + ws[2]
        num = ws[0] * o0[...].astype(F32) + ws[1] * o1[...].astype(F32) + ws[2] * o2[...].astype(F32)
        att_ref[...] = (num / tot).astype(BF16)
        lse_ref[...] = m + jnp.log(tot)

    return _rowcall(body, "att_combine", t, tr, [_rows(a, tr) for a in list(outs) + list(lses)],
                    [_orow(t, ATT_WIDTH, BF16, tr), _orow(t, ATT_WIDTH, F32, tr)])


def _att_delta(g_att, att, tr=512):
    t = att.shape[0]

    def body(g_ref, a_ref, o_ref):
        prod = g_ref[...] * a_ref[...].astype(F32)
        o_ref[...] = _dot_split(prod, _block_ones(ATT_WIDTH, HEAD_DIM), 0, 3)

    return _rowcall(body, "att_delta", t, tr, [_rows(g_att, tr), _rows(att, tr)], [_orow(t, ATT_WIDTH, F32, tr)])[0]


def _att_bwd(proj, g_att, lse, delta, dil, slopes):
    t = proj.shape[0]
    lay = _AttLayout(t, dil)
    nsub, nblk, rb, pb, n_pb = lay.nsub, lay.nblk, lay.rb, lay.pb, lay.n_pb
    src, band, qb = lay.qkv(proj)
    aw = ATT_WIDTH // LANE

    def near(i, which):
        return jnp.maximum(i * nsub - 1, 0) if which == "prev" else jnp.minimum((i + 1) * nsub, n_pb - 1)

    def pspec(off, which=None):
        if which:
            return pl.BlockSpec((pb, LANE), lambda hp, i, r: (near(i, which), lay.col(r, band, qb + off + hp)))
        return pl.BlockSpec((rb, LANE), lambda hp, i, r: (i, lay.col(r, band, qb + off + hp)))

    def aspec(which=None):
        if which:
            return pl.BlockSpec((pb, LANE), lambda hp, i, r: (near(i, which), lay.col(r, aw, hp)))
        return pl.BlockSpec((rb, LANE), lambda hp, i, r: (i, lay.col(r, aw, hp)))

    scale = HEAD_DIM ** -0.5

    def body(q_ref, qn_ref, kc_ref, kp_ref, vc_ref, vp_ref, do_ref, don_ref, lse_ref, lsen_ref, dl_ref, dln_ref, sl_ref,
             dq_ref, dk_ref, dv_ref):
        i, r = pl.program_id(1), pl.program_id(2)
        half = _lane_half()

        def tile_grads(q, do, lse_q, dl_q, kk, vv, dist, valid):
            dqs, dks, dvs = [], [], []
            for e in range(2):
                c = e * HEAD_DIM
                qe = jnp.where(half == e, q, jnp.zeros_like(q))
                doe = jnp.where(half == e, do, jnp.zeros_like(do))
                s = lax.dot_general(qe, kk, NT_DIMS, preferred_element_type=F32)
                s = s + jnp.where(valid, -sl_ref[0:1, c:c + 1] * dist, NEG)
                p = jnp.exp(s - lse_q[:, c:c + 1])
                dp = lax.dot_general(doe, vv, NT_DIMS, preferred_element_type=F32)
                ds16 = (p * (dp - dl_q[:, c:c + 1])).astype(BF16)
                dqs.append(jnp.dot(ds16, kk, preferred_element_type=F32))
                dks.append(lax.dot_general(ds16, q, TN_DIMS, preferred_element_type=F32))
                dvs.append(lax.dot_general(p.astype(BF16), do, TN_DIMS, preferred_element_type=F32))
            halfk = lax.broadcasted_iota(jnp.int32, dks[0].shape, 1) // HEAD_DIM
            return (jnp.where(half == 0, dqs[0], dqs[1]) * scale,
                    jnp.where(halfk == 0, dks[0], dks[1]), jnp.where(halfk == 0, dvs[0], dvs[1]))

        carry_k = carry_v = None
        for sub in range(nsub):
            rs = _residue_rows(r, lay.stride, sub)
            q = (q_ref[rs, :] * scale).astype(BF16)
            do = do_ref[rs, :].astype(BF16)
            if sub == 0:
                r0 = _residue_rows(r, lay.stride, 0)
                kk = jnp.concatenate([kp_ref[r0, :], kc_ref[rs, :]], axis=0).astype(BF16)
                vv = jnp.concatenate([vp_ref[r0, :], vc_ref[rs, :]], axis=0).astype(BF16)
                first = i == 0
            else:
                ks = _residue_rows(r, lay.stride, sub - 1, 2)
                kk, vv = kc_ref[ks, :].astype(BF16), vc_ref[ks, :].astype(BF16)
                first = jnp.bool_(False)
            dist, valid = _att_scores_mask(dil, first)
            dq, dk2, dv2 = tile_grads(q, do, lse_ref[rs, :], dl_ref[rs, :], kk, vv, dist, valid)
            dq_ref[rs, :] = dq
            if sub > 0:
                rp = _residue_rows(r, lay.stride, sub - 1)
                dk_ref[rp, :] = carry_k + dk2[:ATT_BLOCK, :]
                dv_ref[rp, :] = carry_v + dv2[:ATT_BLOCK, :]
            carry_k, carry_v = dk2[ATT_BLOCK:, :], dv2[ATT_BLOCK:, :]
        rl = _residue_rows(r, lay.stride, nsub - 1)
        rn = _residue_rows(r, lay.stride, 0)
        iq = lax.broadcasted_iota(jnp.int32, (ATT_BLOCK, ATT_BLOCK), 0)
        jk = lax.broadcasted_iota(jnp.int32, (ATT_BLOCK, ATT_BLOCK), 1)
        dist_i = ATT_BLOCK + iq - jk
        valid = (dist_i >= 0) & (dist_i <= ATT_BLOCK) & (i < nblk - 1)
        qn = (qn_ref[rn, :] * scale).astype(BF16)
        _, dk1, dv1 = tile_grads(qn, don_ref[rn, :].astype(BF16), lsen_ref[rn, :], dln_ref[rn, :],
                                 kc_ref[rl, :].astype(BF16), vc_ref[rl, :].astype(BF16), (dist_i * dil).astype(F32), valid)
        dk_ref[rl, :] = carry_k + dk1
        dv_ref[rl, :] = carry_v + dv1

    gv, lv, dlv = lay.act(g_att), lay.act(lse), lay.act(delta)
    dq, dk, dv = pl.pallas_call(
        body, grid=(N_ATT_HEADS // 2, nblk, dil),
        in_specs=[pspec(0), pspec(0, "next"), pspec(6), pspec(6, "prev"), pspec(12), pspec(12, "prev"),
                  aspec(), aspec("next"), aspec(), aspec("next"), aspec(), aspec("next"),
                  pl.BlockSpec((None, 8, LANE), lambda hp, i, r: (hp, 0, 0))],
        out_specs=[aspec(), aspec(), aspec()], out_shape=[S(lay.act_shape(), F32)] * 3,
        name=f"att_bwd_d{dil}", compiler_params=_params(("parallel", "parallel", "arbitrary")),
    )(src, src, src, src, src, src, gv, gv, lv, lv, dlv, dlv, slopes)
    return dq.reshape(t, ATT_WIDTH), dk.reshape(t, ATT_WIDTH), dv.reshape(t, ATT_WIDTH)


def _att_grad_sum(dqs, dks, dvs, tr=512):
    t = dqs[0].shape[0]

    def body(*refs):
        o_ref = refs[-1]
        for n in range(3):
            tot = refs[3 * n][...] + refs[3 * n + 1][...] + refs[3 * n + 2][...]
            o_ref[:, n * ATT_WIDTH:(n + 1) * ATT_WIDTH] = tot.astype(BF16)

    return _rowcall(body, "att_grad_sum", t, tr, [_rows(a, tr) for a in list(dqs) + list(dks) + list(dvs)],
                    [_orow(t, 3 * ATT_WIDTH, BF16, tr)])[0]


def _ssd_common(xs, dtx, cs, cs_t):
    ch = SSM_CHUNK
    row = lax.broadcasted_iota(jnp.int32, (ch, ch), 0)
    col = lax.broadcasted_iota(jnp.int32, (ch, ch), 1)
    cs_last = cs[ch - 1:ch, :]
    return dict(tril=col <= row, row=row, col=col, cs=cs, cs_t=cs_t, cs_last=cs_last,
                e=jnp.exp(cs), w=jnp.exp(cs_last - cs), xd=xs * dtx)


def _dot_split(a, b, split, terms=2):
    ops = [a, b]
    rest = ops[split]
    other = ops[1 - split].astype(BF16)
    out = None
    for _ in range(terms):
        piece = rest.astype(BF16)
        rest = rest - piece.astype(F32)
        part = jnp.dot(other, piece, preferred_element_type=F32) if split == 1 else jnp.dot(piece, other, preferred_element_type=F32)
        out = part if out is None else out + part
    return out


def _decay_col(cs_t, heads_per_group):
    r = lax.broadcasted_iota(jnp.int32, (heads_per_group * SSM_HEAD_DIM, SSM_STATE), 0) // SSM_HEAD_DIM
    out = jnp.zeros((heads_per_group * SSM_HEAD_DIM, SSM_STATE), F32)
    for j in range(heads_per_group):
        out = jnp.where(r == j, jnp.exp(cs_t[j:j + 1, SSM_CHUNK - 1:SSM_CHUNK]), out)
    return out


SSD_GROUPS_PER_STEP = 2


def _ssd_specs(t):
    hg = SSM_HEADS // SSM_GROUPS
    gw = hg * SSM_HEAD_DIM
    nb0 = SSM_INNER // SSM_STATE
    return hg, gw, nb0


def _ssd_group_views(gi, gw, wide, narrow, stacked):
    w = [r.at[:, pl.ds(gi * gw, gw)] for r in wide]
    n = [r.at[:, pl.ds(gi * SSM_STATE, SSM_STATE)] for r in narrow]
    return w, n, [r.at[gi] for r in stacked]


def _ssd_fwd(xa, dtx, csx, cst_g):
    t = xa.shape[0]
    nch = t // SSM_CHUNK
    hg, gw, nb0 = _ssd_specs(t)
    ch = SSM_CHUNK
    gp = SSD_GROUPS_PER_STEP

    def body(xs_ref, b_ref, c_ref, dtx_ref, cs_ref, cst_ref, y_ref, st_ref, h_scr):
        for gi in range(gp):
            (xs_g, dtx_g, cs_g, y_g), (b_g, c_g), (cst_gi, st_g) = _ssd_group_views(
                gi, gw, (xs_ref, dtx_ref, cs_ref, y_ref), (b_ref, c_ref), (cst_ref, st_ref))
            group_body(pl.program_id(0), pl.program_id(1) * gp + gi, xs_g, b_g, c_g, dtx_g, cs_g, cst_gi, y_g, st_g, h_scr)

    def group_body(cc, g, xs_ref, b_ref, c_ref, dtx_ref, cs_ref, cst_ref, y_ref, st_ref, h_scr):
        @pl.when(cc == 0)
        def _():
            h_scr[g] = jnp.zeros((gw, SSM_STATE), F32)

        q = _ssd_common(xs_ref[...], dtx_ref[...], cs_ref[...], cst_ref[...])
        bb, cb = b_ref[...].astype(BF16), c_ref[...].astype(BF16)
        cbm = lax.dot_general(cb, bb, NT_DIMS, preferred_element_type=F32)
        h = h_scr[g]
        st_ref[...] = h
        xd16 = q["xd"].astype(BF16)
        y = lax.dot_general(cb, h.astype(BF16), NT_DIMS, preferred_element_type=F32) * q["e"]
        lane_head = lax.broadcasted_iota(jnp.int32, (ch, gw), 1) // SSM_HEAD_DIM
        for j in range(hg):
            diff = q["cs"][:, j * SSM_HEAD_DIM:j * SSM_HEAD_DIM + 1] - q["cs_t"][j:j + 1, :]
            gmat = cbm * jnp.exp(jnp.where(q["tril"], diff, NEG))
            yj = jnp.dot(gmat.astype(BF16), xd16, preferred_element_type=F32)
            y = y + jnp.where(lane_head == j, yj, 0.0)
        y_ref[...] = y
        s_new = lax.dot_general((q["xd"] * q["w"]).astype(BF16), bb, TN_DIMS, preferred_element_type=F32)
        h_scr[g] = _decay_col(q["cs_t"], hg) * h + s_new

    wide = pl.BlockSpec((ch, gp * gw), lambda cc, g: (cc, g))
    return pl.pallas_call(
        body, grid=(nch, SSM_GROUPS // gp),
        in_specs=[wide,
                  pl.BlockSpec((ch, gp * SSM_STATE), lambda cc, g: (cc, nb0 // gp + g)),
                  pl.BlockSpec((ch, gp * SSM_STATE), lambda cc, g: (cc, (nb0 + SSM_GROUPS) // gp + g)),
                  wide, wide,
                  pl.BlockSpec((gp, 8, ch), lambda cc, g: (g, 0, cc))],
        out_specs=[wide, pl.BlockSpec((None, gp, gw, SSM_STATE), lambda cc, g: (cc, g, 0, 0))],
        out_shape=[S((t, SSM_INNER), F32), S((nch, SSM_GROUPS, gw, SSM_STATE), F32)],
        scratch_shapes=[pltpu.VMEM((SSM_GROUPS, gw, SSM_STATE), F32)],
        name="ssd_fwd", compiler_params=_params(("arbitrary", "arbitrary")),
    )(xa, xa, xa, dtx, csx, cst_g)


def _ssd_bwd(xa, dtx, csx, cst_g, alog_x, g_y, states, dskip_x):
    t = xa.shape[0]
    nch = t // SSM_CHUNK
    hg, gw, nb0 = _ssd_specs(t)
    ch = SSM_CHUNK
    gp = SSD_GROUPS_PER_STEP

    def rc(cc):
        return nch - 1 - cc

    def body(xs_ref, b_ref, c_ref, dtx_ref, cs_ref, cst_ref, alx_ref, gy_ref, st_ref, dsk_ref,
             gxs_ref, gb_ref, gc_ref, gdt_ref, ga_ref, gh_scr):
        for gi in range(gp):
            wide, narrow, stacked = _ssd_group_views(
                gi, gw, (xs_ref, dtx_ref, cs_ref, alx_ref, gy_ref, dsk_ref, gxs_ref, gdt_ref, ga_ref), (b_ref, c_ref, gb_ref, gc_ref),
                (cst_ref, st_ref))
            xs_g, dtx_g, cs_g, alx_g, gy_g, dsk_g, gxs_g, gdt_g, ga_g = wide
            b_g, c_g, gb_g, gc_g = narrow
            group_body(pl.program_id(0), pl.program_id(1) * gp + gi, xs_g, b_g, c_g, dtx_g, cs_g, stacked[0], alx_g, gy_g, stacked[1],
                       dsk_g, gxs_g, gb_g, gc_g, gdt_g, ga_g, gh_scr)

    def group_body(cc, g, xs_ref, b_ref, c_ref, dtx_ref, cs_ref, cst_ref, alx_ref, gy_ref, st_ref, dsk_ref,
                   gxs_ref, gb_ref, gc_ref, gdt_ref, ga_ref, gh_scr):
        @pl.when(cc == 0)
        def _():
            gh_scr[g] = jnp.zeros((gw, SSM_STATE), F32)

        xs, dtx = xs_ref[...], dtx_ref[...]
        q = _ssd_common(xs, dtx, cs_ref[...], cst_ref[...])
        cs, cs_t, e, w, xd = q["cs"], q["cs_t"], q["e"], q["w"], q["xd"]
        bb, cb = b_ref[...].astype(BF16), c_ref[...].astype(BF16)
        gy = gy_ref[...]
        gy16, xd16 = gy.astype(BF16), xd.astype(BF16)
        h = st_ref[...]
        h16 = h.astype(BF16)
        ghn = gh_scr[g]
        ghn16 = ghn.astype(BF16)
        seg = _block_ones(gw, SSM_HEAD_DIM)
        cbm = lax.dot_general(cb, bb, NT_DIMS, preferred_element_type=F32)
        cbt = lax.dot_general(bb, cb, NT_DIMS, preferred_element_type=F32)

        gye16 = (gy * e).astype(BF16)
        chm = lax.dot_general(cb, h16, NT_DIMS, preferred_element_type=F32)
        g_c = jnp.dot(gye16, h16, preferred_element_type=F32)
        gh_off = lax.dot_general(gye16, cb, TN_DIMS, preferred_element_type=F32)
        g_e = _dot_split(gy * chm, seg, 0)
        bgs = lax.dot_general(bb, ghn16, NT_DIMS, preferred_element_type=F32)
        g_xd = w * bgs
        g_w = _dot_split(xd * bgs, seg, 0)
        g_b = jnp.dot((xd * w).astype(BF16), ghn16, preferred_element_type=F32)
        decay = _decay_col(cs_t, hg)
        gh_scr[g] = decay * ghn + gh_off
        rsum = jnp.sum(ghn * h, axis=1, keepdims=True)
        lane_head = lax.broadcasted_iota(jnp.int32, (ch, gw), 1) // SSM_HEAD_DIM
        lane_head1 = lax.broadcasted_iota(jnp.int32, (1, gw), 1) // SSM_HEAD_DIM
        g_el = jnp.zeros((1, gw), F32)
        g_cs = g_e * e - g_w * w
        upper = q["row"] <= q["col"]
        for j in range(hg):
            g_el = jnp.where(lane_head1 == j, jnp.sum(rsum[j * SSM_HEAD_DIM:(j + 1) * SSM_HEAD_DIM, :], axis=0, keepdims=True), g_el)
            csc = cs[:, j * SSM_HEAD_DIM:j * SSM_HEAD_DIM + 1]
            csr = cs_t[j:j + 1, :]
            lm = jnp.exp(jnp.where(q["tril"], csc - csr, NEG))
            lmt = jnp.exp(jnp.where(upper, csr - csc, NEG))
            gyj = jnp.where(lane_head == j, gy16, jnp.zeros_like(gy16))
            xdj = jnp.where(lane_head == j, xd16, jnp.zeros_like(xd16))
            gg = lax.dot_general(gyj, xd16, NT_DIMS, preferred_element_type=F32)
            ggt = lax.dot_general(xdj, gy16, NT_DIMS, preferred_element_type=F32)
            gcb, gcbt = gg * lm, ggt * lmt
            g_c = g_c + jnp.dot(gcb.astype(BF16), bb, preferred_element_type=F32)
            g_b = g_b + jnp.dot(gcbt.astype(BF16), cb, preferred_element_type=F32)
            gxdj = jnp.dot((cbt * lmt).astype(BF16), gy16, preferred_element_type=F32)
            g_xd = g_xd + jnp.where(lane_head == j, gxdj, 0.0)
            d_cs = jnp.sum(gcb * cbm, axis=1, keepdims=True) - jnp.sum(gcbt * cbt, axis=1, keepdims=True)
            g_cs = g_cs + jnp.where(lane_head == j, d_cs, 0.0)
        extra = _colsum(g_w * w) + g_el * jnp.exp(q["cs_last"])
        g_cs = g_cs + jnp.where(lax.broadcasted_iota(jnp.int32, (ch, gw), 0) == ch - 1, extra, 0.0)
        g_la = _dot_split(upper, g_cs, 1)
        a_x = -jnp.exp(alx_ref[...])
        gdt_ref[...] = g_xd * xs + g_la * a_x * (1.0 / SSM_HEAD_DIM)
        ga_row = _colsum(g_la * (dtx * a_x)) * (1.0 / SSM_HEAD_DIM)
        ga_ref[...] = jnp.where(lax.broadcasted_iota(jnp.int32, (8, gw), 0) == 0, ga_row, 0.0)
        gxs_ref[...] = g_xd * dtx + gy * dsk_ref[...]
        gb_ref[...] = g_b
        gc_ref[...] = g_c

    wide = pl.BlockSpec((ch, gp * gw), lambda cc, g: (rc(cc), g))
    narrow = pl.BlockSpec((ch, gp * SSM_STATE), lambda cc, g: (rc(cc), g))
    row = pl.BlockSpec((1, gp * gw), lambda cc, g: (0, g))
    return pl.pallas_call(
        body, grid=(nch, SSM_GROUPS // gp),
        in_specs=[wide,
                  pl.BlockSpec((ch, gp * SSM_STATE), lambda cc, g: (rc(cc), nb0 // gp + g)),
                  pl.BlockSpec((ch, gp * SSM_STATE), lambda cc, g: (rc(cc), (nb0 + SSM_GROUPS) // gp + g)),
                  wide, wide,
                  pl.BlockSpec((gp, 8, ch), lambda cc, g: (g, 0, rc(cc))),
                  row, wide,
                  pl.BlockSpec((None, gp, gw, SSM_STATE), lambda cc, g: (rc(cc), g, 0, 0)),
                  row],
        out_specs=[wide, narrow, narrow, wide, pl.BlockSpec((8, gp * gw), lambda cc, g: (rc(cc), g))],
        out_shape=[S((t, SSM_INNER), F32), S((t, SSM_GROUPS * SSM_STATE), F32), S((t, SSM_GROUPS * SSM_STATE), F32),
                   S((t, SSM_INNER), F32), S((nch * 8, SSM_INNER), F32)],
        scratch_shapes=[pltpu.VMEM((SSM_GROUPS, gw, SSM_STATE), F32)],
        name="ssd_bwd", compiler_params=_params(("arbitrary", "arbitrary")),
    )(xa, xa, xa, dtx, csx, cst_g, alog_x, g_y, states, dskip_x)


def _local_step(x, target, w_pre, w_in_r, b_gate, conv_w, conv_b, dt_bias, a_log, d_skip, ssm_norm_w,
                late_weights, w_post, w_fpre, w_fpost, on_mid_grads, on_in_proj_grads):
    t = x.shape[0]
    mm = functools.partial(_matmul, tm=512)
    slopes = _slope_rows()
    hg = SSM_HEADS // SSM_GROUPS
    dt_bias_pad = jnp.pad(dt_bias, ((0, 0), (0, LANE - SSM_HEADS)))
    alog_x = jnp.repeat(a_log, SSM_HEAD_DIM, axis=1)
    alog_pad = jnp.pad(a_log, ((0, 0), (0, LANE - SSM_HEADS)))
    dskip_x = jnp.repeat(d_skip, SSM_HEAD_DIM, axis=1)

    u = _pre_norm(x, w_pre)
    proj = mm(u, w_in_r, mode="nn", out_dtype=F32, name="in_proj", tn=1792, tk=D_MODEL)
    fwd = [_att_fwd(proj, dil, slopes) for _, dil in DILATED_PATTERNS]
    att, lse = _att_combine([o for o, _ in fwd], [l for _, l in fwd])
    xa, xc = _conv_fwd(proj, conv_w, conv_b)
    dtx, csx, cst = _dt_fwd(proj, dt_bias_pad, alog_pad)
    cst_g = jnp.pad(cst[:SSM_HEADS].reshape(SSM_GROUPS, hg, t), ((0, 0), (0, 8 - hg), (0, 0)))
    y_ssd, states = _ssd_fwd(xa, dtx, csx, cst_g)
    y4 = _gate_norm_fwd(y_ssd, xa, proj, dskip_x, ssm_norm_w)
    w_att, w_ssm, w_out, w_up, w_down = late_weights(y4)
    att_p = mm(att, w_att, mode="nn", out_dtype=F32, name="att_proj", tn=D_MODEL, tk=ATT_WIDTH)
    ssm_p = mm(y4, w_ssm, mode="nn", out_dtype=F32, name="ssm_proj", tn=D_MODEL, tk=SSM_INNER)
    mixin = _gating_fwd(proj, b_gate, att_p, ssm_p)
    mixed = mm(mixin, w_out, mode="nn", out_dtype=F32, name="out_proj", tn=D_MODEL, tk=D_MODEL)
    h1, f = _mix_post_ffn_pre(x, mixed, w_post, w_fpre)
    act, up = _matmul(f, w_up, mode="nn", out_dtype=BF16, name="ffn_up", tm=2048, tn=FFN_HIDDEN // N_DEV, tk=D_MODEL, epilogue="relu2", stacked=True)
    dn = mm(act, w_down, mode="nn", out_dtype=F32, name="ffn_down", tn=D_MODEL, tk=FFN_HIDDEN)
    loss, g_h2, g_dn, gw_fpost = _loss_and_ffn_post_bwd(h1, dn, w_fpost, target)

    g_up = mm(g_dn, w_down, mode="nt", out_dtype=BF16, name="ffn_down_bwd_x", tn=2048, tk=D_MODEL, epilogue="relu2_bwd", extra=up)
    gw_down = _matmul(act, g_dn, mode="tn", out_dtype=BF16, name="ffn_down_bwd_w", tm=1024, tn=D_MODEL, tk=512)
    g_f = _matmul(g_up, w_up, mode="nt", out_dtype=F32, name="ffn_up_bwd_x", tm=2048, tn=D_MODEL, tk=FFN_HIDDEN // N_DEV, stacked=True)
    gw_up = _matmul(f, g_up, mode="tn", out_dtype=BF16, name="ffn_up_bwd_w", tm=D_MODEL, tn=FFN_HIDDEN // N_DEV, tk=2048, stacked=True)
    g_h1, g_mixed, gw_fpre, gw_post = _ffn_pre_mix_post_bwd(g_h2, g_f, h1, w_fpre, mixed, w_post)
    g_mixin = mm(g_mixed, w_out, mode="nt", out_dtype=F32, name="out_proj_bwd_x", tn=D_MODEL, tk=D_MODEL)
    gw_out = _matmul(mixin, g_mixed, mode="tn", out_dtype=BF16, name="out_proj_bwd_w", tm=D_MODEL, tn=D_MODEL, tk=512)
    g_att_p, g_ssm_p, g_gl, g_b_gate = _gating_bwd(g_mixin, proj, b_gate, att_p, ssm_p)
    g_att = mm(g_att_p, w_att, mode="nt", out_dtype=F32, name="att_proj_bwd_x", tn=ATT_WIDTH, tk=D_MODEL)
    gw_att = _matmul(att, g_att_p, mode="tn", out_dtype=BF16, name="att_proj_bwd_w", tm=ATT_WIDTH, tn=D_MODEL, tk=512)
    g_y4 = mm(g_ssm_p, w_ssm, mode="nt", out_dtype=F32, name="ssm_proj_bwd_x", tn=SSM_INNER, tk=D_MODEL)
    gw_ssm = _matmul(y4, g_ssm_p, mode="tn", out_dtype=BF16, name="ssm_proj_bwd_w", tm=1024, tn=D_MODEL, tk=512)
    token = on_mid_grads(dict(w_att_proj=gw_att, w_ssm_proj=gw_ssm, w_out=gw_out, w_up=gw_up, w_down=gw_down))
    if token is not None:
        ssm_norm_w = ssm_norm_w + jnp.tile(token[0:1, :], (1, SSM_INNER // LANE))
    g_y2, g_z, g_norm_w, _, g_d_skip = _gate_norm_bwd(g_y4, y_ssd, xa, proj, dskip_x, ssm_norm_w)
    g_xs, g_bm, g_cm, g_dtx, ga_rows = _ssd_bwd(xa, dtx, csx, cst_g, alog_x, g_y2, states, dskip_x)
    g_dt_raw, g_dt_bias, g_a_log = _dt_bwd(g_dtx, ga_rows, proj, dt_bias_pad)
    g_xbc, g_conv_b, gcw0, gcw1, gcw2, gcw3 = _conv_bwd(g_xs, g_bm, g_cm, xc, proj, conv_w)
    delta = _att_delta(g_att, att)
    dqs, dks, dvs = [], [], []
    for _, dil in DILATED_PATTERNS:
        dq, dk, dv = _att_bwd(proj, g_att, lse, delta, dil, slopes)
        dqs.append(dq)
        dks.append(dk)
        dvs.append(dv)
    g_qkv = _att_grad_sum(dqs, dks, dvs)
    g_proj = jnp.concatenate([g_z, g_gl, g_xbc, g_qkv, g_dt_raw, jnp.zeros((t, PROJ_W - OFF_DT - LANE), BF16)], axis=1)
    gw_in_r = _matmul(u, g_proj, mode="tn", out_dtype=BF16, name="in_proj_bwd_w", tm=D_MODEL, tn=1792, tk=512)
    token = on_in_proj_grads(gw_in_r, jnp.concatenate([gcw0, gcw1, gcw2, gcw3], axis=0))
    g_u = _matmul(g_proj, w_in_r, mode="nt", out_dtype=F32, name="in_proj_bwd_x", tm=1024, tn=D_MODEL, tk=1792, after=token)
    g_x, gw_pre = _pre_norm_bwd(g_h1, g_u, x, w_pre)

    grads = dict(
        norm_mix_pre_w=gw_pre, b_gate=g_b_gate, conv_b=g_conv_b, dt_bias=g_dt_bias[:, :SSM_HEADS], a_log=g_a_log[:, :SSM_HEADS],
        d_skip=g_d_skip[:, :SSM_HEADS], ssm_norm_w=g_norm_w, norm_mix_post_w=gw_post, norm_ffn_pre_w=gw_fpre, norm_ffn_post_w=gw_fpost)
    return loss, g_x, grads


def _mesh_pos():
    return lax.axis_index("x"), lax.axis_index("y"), lax.axis_index("c")


def _all_gather(shards):
    n = len(shards)

    def body(*refs):
        x_refs, o_refs = refs[:n], refs[n:2 * n]
        send_sems, recv_sems, local_sems = refs[2 * n:]
        x, y, c = _mesh_pos()
        me, sibling = (x, y, c), (x, y, 1 - c)
        chips = [(1 - x, y), (x, 1 - y), (1 - x, 1 - y)]

        def copy(a, k, block, to, src=None):
            dst = o_refs[a].at[4 * block[0] + 2 * block[1] + block[2]]
            return pltpu.make_async_remote_copy(
                src_ref=dst if src is None else src, dst_ref=dst, send_sem=send_sems.at[7 * a + k], recv_sem=recv_sems.at[7 * a + k],
                device_id=to, device_id_type=pl.DeviceIdType.MESH)

        mine = [pltpu.make_async_copy(x_refs[a], o_refs[a].at[4 * x + 2 * y + c], local_sems.at[a]) for a in range(n)]
        for cp in mine:
            cp.start()
        first = []
        for a in range(n):
            first.append(copy(a, 0, me, sibling, src=x_refs[a]))
            first += [copy(a, 1 + j, me, (*chip, c), src=x_refs[a]) for j, chip in enumerate(chips)]
        for cp in first:
            cp.start()
        passed = []
        for j, chip in enumerate(chips):
            for a in range(n):
                copy(a, 1 + j, (*chip, c), me).wait_recv()
                passed.append(copy(a, 4 + j, (*chip, c), sibling))
                passed[-1].start()
        for a in range(n):
            copy(a, 0, sibling, me).wait_recv()
            for j, chip in enumerate(chips):
                copy(a, 4 + j, (*chip, 1 - c), me).wait_recv()
        for cp in first + passed:
            cp.wait_send()
        for cp in mine:
            cp.wait()

    hbm = pl.BlockSpec(memory_space=pltpu.HBM)
    return pl.pallas_call(
        body, out_shape=[S((N_DEV,) + s.shape, s.dtype) for s in shards],
        in_specs=[hbm] * n, out_specs=[hbm] * n,
        scratch_shapes=[pltpu.SemaphoreType.DMA((7 * n,)), pltpu.SemaphoreType.DMA((7 * n,)), pltpu.SemaphoreType.DMA((n,))],
        name="weights_all_gather",
    )(*shards)


def _exchange_grads(slab_arrays, small):
    n = len(slab_arrays)
    r_small = small.shape[0]

    def body(*refs):
        slab_refs, small_ref = refs[:n], refs[n]
        recv_refs, gsm_ref = refs[n + 1:2 * n + 1], refs[2 * n + 1]
        send_sems, recv_sems, local_sems = refs[2 * n + 2:]
        x, y, c = _mesh_pos()
        me = 4 * x + 2 * y + c

        def peer(k):
            px = 1 - x if k & 4 else x
            py = 1 - y if k & 2 else y
            pc = 1 - c if k & 1 else c
            return (px, py, pc), 4 * px + 2 * py + pc

        def copy(a, k, sending):
            to, lin = peer(k)
            sem = 7 * a + k - 1
            if a == n:
                src, dst = small_ref, gsm_ref.at[me if sending else lin]
            else:
                src, dst = slab_refs[a].at[lin], recv_refs[a].at[me if sending else lin]
            return pltpu.make_async_remote_copy(src_ref=src, dst_ref=dst, send_sem=send_sems.at[sem], recv_sem=recv_sems.at[sem],
                                                device_id=to, device_id_type=pl.DeviceIdType.MESH)

        own = [pltpu.make_async_copy(slab_refs[a].at[me], recv_refs[a].at[me], local_sems.at[a]) for a in range(n)]
        own.append(pltpu.make_async_copy(small_ref, gsm_ref.at[me], local_sems.at[n]))
        for cp in own:
            cp.start()
        order = [n] + list(range(n))
        sends = [copy(a, k, True) for a in order for k in range(1, N_DEV)]
        for cp in sends:
            cp.start()
        for a in order:
            for k in range(1, N_DEV):
                copy(a, k, False).wait_recv()
        for cp in sends:
            cp.wait_send()
        for cp in own:
            cp.wait()

    hbm = pl.BlockSpec(memory_space=pltpu.HBM)
    n_sem = 7 * (n + 1)
    res = pl.pallas_call(
        body, out_shape=[S(a.shape, a.dtype) for a in slab_arrays] + [S((N_DEV, r_small, LANE), small.dtype)],
        in_specs=[hbm] * (n + 1), out_specs=[hbm] * (n + 1),
        scratch_shapes=[pltpu.SemaphoreType.DMA((n_sem,)), pltpu.SemaphoreType.DMA((n_sem,)), pltpu.SemaphoreType.DMA((n + 1,))],
        name="grad_exchange",
    )(*slab_arrays, small)
    return res[:n], res[n]


def _peer_of(k, x, y, c):
    px = 1 - x if k & 4 else x
    py = 1 - y if k & 2 else y
    pc = 1 - c if k & 1 else c
    return (px, py, pc), 4 * px + 2 * py + pc


def _split_copies(src_refs, land_refs, send_sems, recv_sems, per_peer):
    x, y, c = _mesh_pos()
    me = 4 * x + 2 * y + c
    sends, recvs = [], []
    for a, (src, land) in enumerate(zip(src_refs, land_refs)):
        for k in range(1, N_DEV):
            to, lin = _peer_of(k, x, y, c)
            sem = 7 * a + k - 1
            piece = src.at[lin] if per_peer else src
            for slot, out in ((me, sends), (lin, recvs)):
                out.append(pltpu.make_async_remote_copy(
                    src_ref=piece, dst_ref=land.at[slot], send_sem=send_sems.at[sem], recv_sem=recv_sems.at[sem],
                    device_id=to, device_id_type=pl.DeviceIdType.MESH))
    return sends, recvs


def _remote_start(srcs, per_peer, name):
    n = len(srcs)
    lands = [lax.empty((N_DEV,) + (s.shape[1:] if per_peer else s.shape), s.dtype) for s in srcs]

    def body(*refs):
        src_refs, land_refs = refs[:n], refs[n:2 * n]
        send_sems, recv_sems = refs[2 * n], refs[2 * n + 1]
        token = refs[-1]
        sends, _ = _split_copies(src_refs, land_refs, send_sems, recv_sems, per_peer)
        for cp in sends:
            cp.start()
        token[...] = jnp.zeros_like(token)

    hbm = pl.BlockSpec(memory_space=pltpu.HBM)
    sem = pl.BlockSpec(memory_space=pltpu.SEMAPHORE)
    res = pl.pallas_call(
        body, name=name,
        out_shape=(pltpu.SemaphoreType.DMA((7 * n,)), pltpu.SemaphoreType.DMA((7 * n,)),
                   *[pltpu.HBM(a.shape, a.dtype) for a in srcs + lands], S((8, LANE), F32)),
        in_specs=[hbm] * (2 * n), out_specs=(sem, sem, *[hbm] * (2 * n), pl.BlockSpec(memory_space=pltpu.VMEM)),
        input_output_aliases={i: 2 + i for i in range(2 * n)},
        compiler_params=pltpu.CompilerParams(has_side_effects=pltpu.SideEffectType.DATAFLOW_SIDE_EFFECTING),
    )(*[pltpu.with_memory_space_constraint(a, pltpu.HBM) for a in srcs + lands])
    return dict(sems=res[:2], srcs=list(res[2:2 + n]), lands=list(res[2 + n:2 + 2 * n]), per_peer=per_peer), res[-1]


def _remote_wait(handle, after, name):
    n = len(handle["srcs"])
    per_peer = handle["per_peer"]

    def body(*refs):
        src_refs, land_refs = refs[:n], refs[n:2 * n]
        send_sems, recv_sems = refs[2 * n], refs[2 * n + 1]
        sends, recvs = _split_copies(src_refs, land_refs, send_sems, recv_sems, per_peer)
        for cp in sends:
            cp.wait_send()
        for cp in recvs:
            cp.wait_recv()

    hbm = pl.BlockSpec(memory_space=pltpu.HBM)
    sem = pl.BlockSpec(memory_space=pltpu.SEMAPHORE)
    arrays = handle["srcs"] + handle["lands"]
    res = pl.pallas_call(
        body, name=name, out_shape=tuple(pltpu.HBM(a.shape, a.dtype) for a in arrays),
        in_specs=[hbm] * (2 * n) + [sem, sem, pl.BlockSpec(memory_space=pl.ANY)], out_specs=tuple([hbm] * (2 * n)),
        input_output_aliases={i: i for i in range(2 * n)},
        compiler_params=pltpu.CompilerParams(has_side_effects=pltpu.SideEffectType.DATAFLOW_SIDE_EFFECTING),
    )(*arrays, *handle["sems"], after)
    return list(res[n:])


def _with_own(lands, own, me):
    return [lax.dynamic_update_index_in_dim(land, o.astype(land.dtype), me, 0) for land, o in zip(lands, own)]


def _adamw(w, m, v, slabs, name, tr):
    r, cols = w.shape
    c1 = 1.0 - ADAM_B1 ** ADAM_STEP
    c2 = 1.0 - ADAM_B2 ** ADAM_STEP

    def body(w_ref, m_ref, v_ref, s_ref, g_ref, d_ref, nm_ref, nv_ref):
        g = s_ref[0].astype(F32)
        for d in range(1, N_DEV):
            g = g + s_ref[d].astype(F32)
        nm = ADAM_B1 * m_ref[...] + (1.0 - ADAM_B1) * g
        nv = ADAM_B2 * v_ref[...] + (1.0 - ADAM_B2) * (g * g)
        g_ref[...] = g
        nm_ref[...] = nm
        nv_ref[...] = nv
        d_ref[...] = -ADAM_LR * ((nm / c1) / (jnp.sqrt(nv / c2) + ADAM_EPS) + ADAM_WD * w_ref[...])

    assert r % tr == 0, name
    blk = pl.BlockSpec((tr, cols), lambda i: (i, 0))
    return pl.pallas_call(
        body, grid=(r // tr,), in_specs=[blk, blk, blk, pl.BlockSpec((N_DEV, tr, cols), lambda i: (0, i, 0))],
        out_specs=[blk] * 4, out_shape=[S((r, cols), F32)] * 4, name=name, compiler_params=_params(("parallel",)),
    )(w, m, v, slabs)


BIG = ("w_in", "w_att_proj", "w_up", "w_ssm_proj", "w_out", "w_down", "conv_w")
ADAMW_ROWS = dict(w_in=256, w_att_proj=768, w_up=512, w_ssm_proj=256, w_out=128, w_down=256, conv_w=4)
SMALL = ("norm_mix_pre_w", "b_gate", "conv_b", "dt_bias", "a_log", "d_skip", "ssm_norm_w", "norm_mix_post_w",
         "norm_ffn_pre_w", "norm_ffn_post_w")
ORDER = ("norm_mix_pre_w", "w_in", "b_gate", "conv_w", "conv_b", "dt_bias", "a_log", "d_skip", "ssm_norm_w", "w_att_proj",
         "w_ssm_proj", "w_out", "norm_mix_post_w", "norm_ffn_pre_w", "w_up", "w_down", "norm_ffn_post_w")
ROW_SHARDED = ("w_ssm_proj", "w_out", "w_down")
LATE = ("w_att_proj", "w_ssm_proj", "w_out", "w_up", "w_down")
IN_PROJ_W = 10528
IN_SHARD_W = IN_PROJ_W // N_DEV
IN_SEGMENTS = ((2304, 4352), (8480, 10528), (4352, 8448), (0, 2304), (8448, 8480))


def _pack(parts, rows_multiple):
    flat = jnp.concatenate([p.reshape(-1) for p in parts])
    pad = (-flat.shape[0]) % (rows_multiple * LANE)
    return jnp.pad(flat, (0, pad)).reshape(-1, LANE)


def _unpack(flat2d, shapes):
    flat, out, off = flat2d.reshape(-1), [], 0
    for sh in shapes:
        n = int(np.prod(sh))
        out.append(flat[off:off + n].reshape(sh))
        off += n
    return out


def _reorder_in_proj(w):
    qkv, z, xbc = w[:, :2304], w[:, 2304:4352], w[:, 4352:8448]
    dt, gate = w[:, 8448:8480], w[:, 8480:10528]
    return jnp.concatenate([z, gate, xbc, qkv, dt, jnp.zeros((w.shape[0], PROJ_W - 10528), w.dtype)], axis=1)


def _restore_in_proj(wr):
    return jnp.concatenate([wr[:, OFF_QKV:OFF_QKV + 2304], wr[:, OFF_Z:OFF_Z + 2048], wr[:, OFF_XBC:OFF_XBC + 4096],
                            wr[:, OFF_DT:OFF_DT + 32], wr[:, OFF_GL:OFF_GL + 2048]], axis=1)


def _assemble_in_proj(g):
    pieces = []
    for lo, hi in IN_SEGMENTS:
        while lo < hi:
            d = lo // IN_SHARD_W
            end = min(hi, (d + 1) * IN_SHARD_W)
            pieces.append(g[d][:, lo - d * IN_SHARD_W:end - d * IN_SHARD_W])
            lo = end
    pieces.append(jnp.zeros((g.shape[1], PROJ_W - IN_PROJ_W), g.dtype))
    return jnp.concatenate(pieces, axis=1)


def _in_proj_slabs(wr):
    orig = _restore_in_proj(wr)
    return jnp.stack([orig[:, d * IN_SHARD_W:(d + 1) * IN_SHARD_W] for d in range(N_DEV)])


def kernel(x, norm_mix_pre_w, w_in, b_gate, conv_w, conv_b, dt_bias, a_log, d_skip, ssm_norm_w, w_att_proj, w_ssm_proj, w_out, norm_mix_post_w, norm_ffn_pre_w, w_up, w_down, norm_ffn_post_w, loss_target, m_norm_mix_pre_w, m_w_in, m_b_gate, m_conv_w, m_conv_b, m_dt_bias, m_a_log, m_d_skip, m_ssm_norm_w, m_w_att_proj, m_w_ssm_proj, m_w_out, m_norm_mix_post_w, m_norm_ffn_pre_w, m_w_up, m_w_down, m_norm_ffn_post_w, v_norm_mix_pre_w, v_w_in, v_b_gate, v_conv_w, v_conv_b, v_dt_bias, v_a_log, v_d_skip, v_ssm_norm_w, v_w_att_proj, v_w_ssm_proj, v_w_out, v_norm_mix_post_w, v_norm_ffn_pre_w, v_w_up, v_w_down, v_norm_ffn_post_w):
    w = dict(norm_mix_pre_w=norm_mix_pre_w, w_in=w_in, b_gate=b_gate, conv_w=conv_w, conv_b=conv_b, dt_bias=dt_bias, a_log=a_log,
             d_skip=d_skip, ssm_norm_w=ssm_norm_w, w_att_proj=w_att_proj, w_ssm_proj=w_ssm_proj, w_out=w_out,
             norm_mix_post_w=norm_mix_post_w, norm_ffn_pre_w=norm_ffn_pre_w, w_up=w_up, w_down=w_down, norm_ffn_post_w=norm_ffn_post_w)
    m = dict(norm_mix_pre_w=m_norm_mix_pre_w, w_in=m_w_in, b_gate=m_b_gate, conv_w=m_conv_w, conv_b=m_conv_b, dt_bias=m_dt_bias,
             a_log=m_a_log, d_skip=m_d_skip, ssm_norm_w=m_ssm_norm_w, w_att_proj=m_w_att_proj, w_ssm_proj=m_w_ssm_proj, w_out=m_w_out,
             norm_mix_post_w=m_norm_mix_post_w, norm_ffn_pre_w=m_norm_ffn_pre_w, w_up=m_w_up, w_down=m_w_down, norm_ffn_post_w=m_norm_ffn_post_w)
    v = dict(norm_mix_pre_w=v_norm_mix_pre_w, w_in=v_w_in, b_gate=v_b_gate, conv_w=v_conv_w, conv_b=v_conv_b, dt_bias=v_dt_bias,
             a_log=v_a_log, d_skip=v_d_skip, ssm_norm_w=v_ssm_norm_w, w_att_proj=v_w_att_proj, w_ssm_proj=v_w_ssm_proj, w_out=v_w_out,
             norm_mix_post_w=v_norm_mix_post_w, norm_ffn_pre_w=v_norm_ffn_pre_w, w_up=v_w_up, w_down=v_w_down, norm_ffn_post_w=v_norm_ffn_post_w)
    shard_shapes = {n: w[n].shape[1:] for n in ORDER}

    mx, my, mc = _mesh_pos()
    me = 4 * mx + 2 * my + mc

    g_in, g_conv = _all_gather([w["w_in"][0].astype(BF16), w["conv_w"][0]])
    conv_full = jnp.moveaxis(g_conv, 0, 1).reshape(SSM_CONV, CONV_DIM)
    late_shards = [w[n][0].astype(BF16) for n in LATE]
    late_handle, token = _remote_start(late_shards, False, "late_weights_start")
    w_pre = w["norm_mix_pre_w"] + jnp.tile(token[0:1, :], (1, D_MODEL // LANE))

    def late_weights(after):
        full = dict(zip(LATE, _with_own(_remote_wait(late_handle, after, "late_weights_wait"), late_shards, me)))
        for n in ROW_SHARDED:
            full[n] = full[n].reshape(-1, full[n].shape[2])
        w_att = jnp.moveaxis(full["w_att_proj"], 0, 1).reshape(ATT_WIDTH, D_MODEL)
        return w_att, full["w_ssm_proj"], full["w_out"], full["w_up"], full["w_down"]

    started = {}

    def start_exchange(tag, slabs):
        own = [lax.dynamic_index_in_dim(s, me, 0, keepdims=False) for s in slabs]
        handle, tok = _remote_start(slabs, True, tag + "_grads_start")
        started[tag] = (handle, own)
        return tok

    def on_mid_grads(g):
        slabs = dict(w_up=g["w_up"], w_att_proj=jnp.moveaxis(g["w_att_proj"].reshape(ATT_WIDTH, N_DEV, -1), 1, 0))
        for n in ROW_SHARDED:
            slabs[n] = g[n].reshape(N_DEV, -1, g[n].shape[1])
        return start_exchange("mid", [slabs[n] for n in LATE])

    def on_in_proj_grads(gw_in_r, g_conv_w):
        return start_exchange("in_proj", [_in_proj_slabs(gw_in_r), jnp.moveaxis(g_conv_w.reshape(SSM_CONV, N_DEV, -1), 1, 0)])

    loss, g_x, grads = _local_step(
        x[0], loss_target[0], w_pre, _assemble_in_proj(g_in), w["b_gate"], conv_full, w["conv_b"], w["dt_bias"], w["a_log"],
        w["d_skip"], w["ssm_norm_w"], late_weights, w["norm_mix_post_w"], w["norm_ffn_pre_w"], w["norm_ffn_post_w"],
        on_mid_grads, on_in_proj_grads)

    recv = {}
    for tag, names in (("mid", LATE), ("in_proj", ("w_in", "conv_w"))):
        handle, own = started[tag]
        recv.update(zip(names, _with_own(_remote_wait(handle, g_x, tag + "_grads_wait"), own, me)))
    small = _pack([grads[n].astype(F32) for n in SMALL], 8)
    _, small_all = _exchange_grads([], small)

    small_shapes = [shard_shapes[n] for n in SMALL]
    small_out = _adamw(*[_pack([d_[n][0] for n in SMALL], 8) for d_ in (w, m, v)], small_all, "adamw_replicated", small_all.shape[1])
    big_out = {n: _adamw(w[n][0], m[n][0], v[n][0], recv[n], "adamw_" + n, ADAMW_ROWS[n]) for n in BIG}
    res = []
    for which, small_flat in enumerate(small_out):
        vals = {n: big_out[n][which] for n in BIG}
        vals.update(zip(SMALL, _unpack(small_flat, small_shapes)))
        res.append([vals[n][None] for n in ORDER])
    g_out, d_out, m_out, v_out = res
    total = lax.psum(loss[0, 0], ("x", "y", "c"))
    return (total, g_x[None], *g_out, *d_out, *m_out, *v_out)
```

```python
import functools
import math

import jax
import jax.numpy as jnp
import numpy as np
from jax import lax
from jax.experimental import pallas as pl
from jax.experimental.pallas import tpu as pltpu

F32 = jnp.float32
BF16 = jnp.bfloat16

D_MODEL = 1024
HEAD_DIM = 64
N_ATT_HEADS = 12
ATT_WIDTH = N_ATT_HEADS * HEAD_DIM
DILATED_PATTERNS = ((128, 1), (512, 4), (2048, 16))
ATT_BLOCK = 128
SSM_INNER = 2048
SSM_HEAD_DIM = 64
SSM_HEADS = 32
SSM_GROUPS = 8
SSM_STATE = 128
SSM_CHUNK = 128
CONV_DIM = 4096
SSM_CONV = 4
FFN_HIDDEN = 4096
RMS_EPS = 1e-6
N_DEV = 8

ADAM_LR = 0.001
ADAM_B1 = 0.9
ADAM_B2 = 0.999
ADAM_EPS = 1e-08
ADAM_WD = 0.01
ADAM_STEP = 10

LANE = 128
OFF_Z, OFF_GL, OFF_XBC, OFF_QKV, OFF_DT = 0, 2048, 4096, 8192, 10496
PROJ_W = 10752
PROJ_BLOCKS = PROJ_W // LANE
PA_W = OFF_QKV
PB_W = PROJ_W - OFF_QKV
PB_DT = OFF_DT - OFF_QKV
VMEM_LIMIT = 52 * 1024 * 1024
NEG = -1e30

HI = lax.Precision.HIGHEST
NT_DIMS = (((1,), (1,)), ((), ()))
TN_DIMS = (((0,), (0,)), ((), ()))
S = jax.ShapeDtypeStruct


def _params(sem):
    return pltpu.CompilerParams(dimension_semantics=sem, vmem_limit_bytes=VMEM_LIMIT)


def _matmul(a, b, *, mode, out_dtype, name, tm, tn, tk, epilogue=None, extra=None, stacked=False, after=None, b_cols=None):
    if mode == "nn":
        m, k = a.shape
        n = b.shape[0] * b.shape[2] if stacked else b.shape[1]
        col0 = 0
        if b_cols is not None:
            assert b_cols[0] % tn == 0, name
            col0, n = b_cols[0] // tn, b_cols[1]
        a_spec = pl.BlockSpec((tm, tk), lambda i, j, kk: (i, kk))
        b_spec = pl.BlockSpec((None, tk, tn), lambda i, j, kk: (j, kk, 0)) if stacked else pl.BlockSpec((tk, tn), lambda i, j, kk: (kk, col0 + j))
        dims = (((1,), (0,)), ((), ()))
    elif mode == "nt":
        m, k = a.shape
        n = b.shape[1] if stacked else b.shape[0]
        a_spec = pl.BlockSpec((tm, tk), lambda i, j, kk: (i, kk))
        b_spec = pl.BlockSpec((None, tn, tk), lambda i, j, kk: (kk, j, 0)) if stacked else pl.BlockSpec((tn, tk), lambda i, j, kk: (j, kk))
        dims = NT_DIMS
    else:
        (k, m), n = a.shape, b.shape[1]
        a_spec = pl.BlockSpec((tk, tm), lambda i, j, kk: (kk, i))
        b_spec = pl.BlockSpec((tk, tn), lambda i, j, kk: (kk, j))
        dims = TN_DIMS
    assert m % tm == 0 and n % tn == 0 and k % tk == 0, (name, m, n, k)
    if stacked:
        assert (tk if mode == "nt" else tn) * N_DEV == (k if mode == "nt" else n), name
    nk = k // tk
    o_spec = pl.BlockSpec((tm, tn), lambda i, j, kk: (i, j))
    in_specs, args = [a_spec, b_spec], [a, b]
    if epilogue == "relu2":
        out_shape = (S((m, n), BF16), S((m, n), BF16))
        out_specs = (o_spec, o_spec)
    elif stacked and mode == "tn":
        out_shape, out_specs = S((N_DEV, m, tn), out_dtype), pl.BlockSpec((None, tm, tn), lambda i, j, kk: (j, i, 0))
    else:
        out_shape, out_specs = S((m, n), out_dtype), o_spec
    if epilogue == "relu2_bwd":
        in_specs.append(o_spec)
        args.append(extra)
    n_in = len(args)
    if after is not None:
        in_specs.append(pl.BlockSpec(after.shape, lambda i, j, kk: (0,) * after.ndim))
        args.append(after)

    def finish(acc, refs):
        if epilogue == "relu2":
            r = jnp.maximum(acc, 0.0)
            refs[0][...] = (r * r).astype(BF16)
            refs[1][...] = acc.astype(BF16)
        elif epilogue == "relu2_bwd":
            up = refs[0][...].astype(F32)
            refs[1][...] = (acc * (2.0 * jnp.maximum(up, 0.0))).astype(out_dtype)
        else:
            refs[0][...] = acc.astype(out_dtype)

    def body(a_ref, b_ref, *rest):
        rest = rest[:n_in - 2] + rest[len(args) - 2:]
        part = lax.dot_general(a_ref[...].astype(BF16), b_ref[...].astype(BF16), dims, preferred_element_type=F32)
        if nk == 1:
            finish(part, rest)
            return
        acc_ref = rest[-1]
        kk = pl.program_id(2)

        @pl.when(kk == 0)
        def _():
            acc_ref[...] = part

        @pl.when(kk > 0)
        def _():
            acc_ref[...] += part

        @pl.when(kk == nk - 1)
        def _():
            finish(acc_ref[...], rest[:-1])

    scratch = [] if nk == 1 else [pltpu.VMEM((tm, tn), F32)]
    return pl.pallas_call(
        body, grid=(m // tm, n // tn, nk), in_specs=in_specs, out_specs=out_specs, out_shape=out_shape,
        scratch_shapes=scratch, name=name, compiler_params=_params(("parallel", "parallel", "arbitrary")),
    )(*args)


def _rowcall(body, name, n_rows, tr, ins, outs, scratch=()):
    res = pl.pallas_call(
        body, grid=(n_rows // tr,),
        in_specs=[pl.BlockSpec(bs, im) for _, bs, im in ins],
        out_specs=[pl.BlockSpec(bs, im) for _, _, bs, im in outs],
        out_shape=[S(sh, dt) for sh, dt, _, _ in outs],
        scratch_shapes=list(scratch), name=name, compiler_params=_params(("arbitrary",)),
    )(*[a for a, _, _ in ins])
    return res


def _rows(arr, tr, width=None, cb=0):
    width = arr.shape[1] if width is None else width
    return (arr, (tr, width), lambda i, cb=cb: (i, cb))


def _whole(arr):
    nd = arr.ndim
    return (arr, arr.shape, lambda i, nd=nd: (0,) * nd)


def _orow(n_rows, width, dtype, tr):
    return ((n_rows, width), dtype, (tr, width), lambda i: (i, 0))


def _oacc(width):
    return ((1, width), F32, (1, width), lambda i: (0, 0))


def _accumulate(ref, value):
    first = pl.program_id(0) == 0

    @pl.when(first)
    def _():
        ref[...] = value

    @pl.when(jnp.logical_not(first))
    def _():
        ref[...] += value


def _colsum(v):
    return jnp.sum(v, axis=0, keepdims=True)


def _rms_fwd(x, w):
    r = lax.rsqrt(jnp.mean(x * x, axis=-1, keepdims=True) + RMS_EPS)
    return x * r * w


def _rms_bwd(gy, x, w):
    r = lax.rsqrt(jnp.mean(x * x, axis=-1, keepdims=True) + RMS_EPS)
    xn = x * r
    gxn = gy * w
    gx = r * (gxn - xn * jnp.mean(gxn * xn, axis=-1, keepdims=True))
    return gx, _colsum(gy * xn)


def _sigmoid(x):
    return 1.0 / (1.0 + jnp.exp(-x))


def _head_expand(n_heads_pad, n_heads, width):
    h = lax.broadcasted_iota(jnp.int32, (n_heads_pad, n_heads * width), 0)
    c = lax.broadcasted_iota(jnp.int32, (n_heads_pad, n_heads * width), 1)
    return (c // width == h).astype(F32)


def _head_reduce(n_heads, width, n_heads_pad):
    c = lax.broadcasted_iota(jnp.int32, (n_heads * width, n_heads_pad), 0)
    h = lax.broadcasted_iota(jnp.int32, (n_heads * width, n_heads_pad), 1)
    return (c // width == h).astype(F32)


def _block_ones(n, width):
    r = lax.broadcasted_iota(jnp.int32, (n, n), 0)
    c = lax.broadcasted_iota(jnp.int32, (n, n), 1)
    return (r // width == c // width).astype(F32)


def _pre_norm(x, w_pre, tr=512):
    t = x.shape[0]

    def body(x_ref, w_ref, u_ref):
        u_ref[...] = _rms_fwd(x_ref[...], w_ref[...]).astype(BF16)

    return _rowcall(body, "pre_norm", t, tr, [_rows(x, tr), _whole(w_pre)], [_orow(t, D_MODEL, BF16, tr)])[0]


def _conv_fwd(proj, conv_w, conv_b, tr=256):
    t = proj.shape[0]
    cb = OFF_XBC // CONV_DIM
    hr = 16
    halo = (proj, (hr, CONV_DIM), lambda i: (jnp.maximum(i * (tr // hr) - 1, 0), cb))

    def body(cur_ref, prev_ref, w_ref, b_ref, o_ref, xc_ref, ext):
        cur = cur_ref[...].astype(F32)
        ext[pl.ds(0, hr), :] = jnp.where(pl.program_id(0) > 0, prev_ref[...].astype(F32), 0.0)
        ext[pl.ds(hr, tr), :] = cur
        acc = b_ref[...] + w_ref[3:4, :] * cur
        for k in range(SSM_CONV - 1):
            acc = acc + w_ref[k:k + 1, :] * ext[pl.ds(hr - 3 + k, tr), :]
        o_ref[...] = acc * _sigmoid(acc)
        xc_ref[...] = acc.astype(BF16)

    return _rowcall(body, "conv_fwd", t, tr, [_rows(proj, tr, CONV_DIM, cb), halo, _whole(conv_w), _whole(conv_b)],
                    [_orow(t, CONV_DIM, F32, tr), _orow(t, CONV_DIM, BF16, tr)], scratch=[pltpu.VMEM((tr + hr, CONV_DIM), F32)])


def _dt_fwd(proj, dt_bias_pad, alog_pad, tr=512):
    t = proj.shape[0]

    def body(raw_ref, b_ref, al_ref, dtx_ref, csx_ref, cst_ref):
        v = raw_ref[...] + b_ref[...]
        dt = jnp.maximum(v, 0.0) + jnp.log1p(jnp.exp(-jnp.abs(v)))
        expand = _head_expand(LANE, SSM_HEADS, SSM_HEAD_DIM)
        dtx_ref[...] = _dot_split(dt, expand, 0, 3)
        la = dt * (-jnp.exp(al_ref[...]))
        row = lax.broadcasted_iota(jnp.int32, (SSM_CHUNK, SSM_CHUNK), 0)
        col = lax.broadcasted_iota(jnp.int32, (SSM_CHUNK, SSM_CHUNK), 1)
        tril = (col <= row).astype(F32)
        cs = jnp.concatenate([_dot_split(tril, la[k * SSM_CHUNK:(k + 1) * SSM_CHUNK, :], 1, 3) for k in range(tr // SSM_CHUNK)], axis=0)
        csx_ref[...] = _dot_split(cs, expand, 0, 3)
        cst_ref[...] = cs.T

    return _rowcall(body, "dt_fwd", t, tr, [_rows(proj, tr, LANE, PB_DT // LANE), _whole(dt_bias_pad), _whole(alog_pad)],
                    [_orow(t, SSM_INNER, F32, tr), _orow(t, SSM_INNER, F32, tr), ((LANE, t), F32, (LANE, tr), lambda i: (0, i))])


def _gate_norm_fwd(y_ssd, xa, proj, dskip_x, norm_w, tr=256):
    t = y_ssd.shape[0]
    gw = SSM_INNER // SSM_GROUPS

    def body(y_ref, xs_ref, z_ref, d_ref, w_ref, o_ref):
        z = z_ref[...].astype(F32)
        y3 = (y_ref[...].astype(F32) + d_ref[...] * xs_ref[...]) * (z * _sigmoid(z))
        for g in range(SSM_GROUPS):
            sl = slice(g * gw, (g + 1) * gw)
            o_ref[:, sl] = _rms_fwd(y3[:, sl], w_ref[:, sl]).astype(BF16)

    return _rowcall(body, "gate_norm_fwd", t, tr,
                    [_rows(y_ssd, tr), _rows(xa, tr, SSM_INNER, 0), _rows(proj, tr, SSM_INNER, OFF_Z // SSM_INNER), _whole(dskip_x), _whole(norm_w)],
                    [_orow(t, SSM_INNER, BF16, tr)])[0]


def _gating_fwd(proj, b_gate, att_p, ssm_p, tr=512):
    t = proj.shape[0]

    def body(gl_ref, b_ref, a_ref, s_ref, o_ref):
        gates = _sigmoid(gl_ref[...].astype(F32) + b_ref[...])
        o_ref[...] = (gates[:, :D_MODEL] * a_ref[...].astype(F32) + gates[:, D_MODEL:] * s_ref[...].astype(F32)).astype(BF16)

    return _rowcall(body, "gating_fwd", t, tr, [_rows(proj, tr, 2 * D_MODEL, OFF_GL // (2 * D_MODEL)), _whole(b_gate), _rows(att_p, tr), _rows(ssm_p, tr)],
                    [_orow(t, D_MODEL, BF16, tr)])[0]


def _mix_post_ffn_pre(x, mixed, w_post, w_fpre, tr=512):
    t = x.shape[0]

    def body(x_ref, m_ref, wp_ref, wf_ref, h1_ref, f_ref):
        h1 = x_ref[...] + _rms_fwd(m_ref[...], wp_ref[...])
        h1_ref[...] = h1
        f_ref[...] = _rms_fwd(h1, wf_ref[...]).astype(BF16)

    return _rowcall(body, "mix_post_ffn_pre", t, tr, [_rows(x, tr), _rows(mixed, tr), _whole(w_post), _whole(w_fpre)],
                    [_orow(t, D_MODEL, F32, tr), _orow(t, D_MODEL, BF16, tr)])


def _loss_and_ffn_post_bwd(h1, dn, w_fpost, target, tr=512):
    t = h1.shape[0]

    def body(h1_ref, dn_ref, w_ref, tg_ref, loss_ref, gh2_ref, gdn_ref, gw_ref):
        dn = dn_ref[...]
        w = w_ref[...]
        err = h1_ref[...] + _rms_fwd(dn, w) - tg_ref[...]
        _accumulate(loss_ref, jnp.zeros((1, LANE), F32) + 0.5 * jnp.sum(jnp.mean(err * err, axis=-1, keepdims=True)))
        gh2 = err * (1.0 / D_MODEL)
        gh2_ref[...] = gh2
        gdn, gw = _rms_bwd(gh2, dn, w)
        gdn_ref[...] = gdn.astype(BF16)
        _accumulate(gw_ref, gw)

    return _rowcall(body, "loss_ffn_post_bwd", t, tr, [_rows(h1, tr), _rows(dn, tr), _whole(w_fpost), _rows(target, tr)],
                    [_oacc(LANE), _orow(t, D_MODEL, F32, tr), _orow(t, D_MODEL, BF16, tr), _oacc(D_MODEL)])


def _ffn_pre_mix_post_bwd(g_h2, g_f, h1, w_fpre, mixed, w_post, tr=512):
    t = h1.shape[0]

    def body(gh2_ref, gf_ref, h1_ref, wf_ref, m_ref, wp_ref, gh1_ref, gm_ref, gwf_ref, gwp_ref):
        gx, gwf = _rms_bwd(gf_ref[...], h1_ref[...], wf_ref[...])
        gh1 = gh2_ref[...] + gx
        gh1_ref[...] = gh1
        gm, gwp = _rms_bwd(gh1, m_ref[...], wp_ref[...])
        gm_ref[...] = gm.astype(BF16)
        _accumulate(gwf_ref, gwf)
        _accumulate(gwp_ref, gwp)

    return _rowcall(body, "ffn_pre_mix_post_bwd", t, tr,
                    [_rows(g_h2, tr), _rows(g_f, tr), _rows(h1, tr), _whole(w_fpre), _rows(mixed, tr), _whole(w_post)],
                    [_orow(t, D_MODEL, F32, tr), _orow(t, D_MODEL, BF16, tr), _oacc(D_MODEL), _oacc(D_MODEL)])


def _gating_bwd(g_mixin, proj, b_gate, att_p, ssm_p, tr=512):
    t = proj.shape[0]

    def body(gm_ref, gl_ref, b_ref, a_ref, s_ref, ga_ref, gs_ref, ggl_ref, gb_ref):
        gates = _sigmoid(gl_ref[...].astype(F32) + b_ref[...])
        gm = gm_ref[...].astype(F32)
        g_att, g_ssm = gates[:, :D_MODEL], gates[:, D_MODEL:]
        ga_ref[...] = (gm * g_att).astype(BF16)
        gs_ref[...] = (gm * g_ssm).astype(BF16)
        ggl_a = gm * a_ref[...].astype(F32) * g_att * (1.0 - g_att)
        ggl_s = gm * s_ref[...].astype(F32) * g_ssm * (1.0 - g_ssm)
        ggl_ref[:, :D_MODEL] = ggl_a.astype(BF16)
        ggl_ref[:, D_MODEL:] = ggl_s.astype(BF16)
        _accumulate(gb_ref.at[:, :D_MODEL], _colsum(ggl_a))
        _accumulate(gb_ref.at[:, D_MODEL:], _colsum(ggl_s))

    return _rowcall(body, "gating_bwd", t, tr,
                    [_rows(g_mixin, tr), _rows(proj, tr, 2 * D_MODEL, OFF_GL // (2 * D_MODEL)), _whole(b_gate), _rows(att_p, tr), _rows(ssm_p, tr)],
                    [_orow(t, D_MODEL, BF16, tr), _orow(t, D_MODEL, BF16, tr), _orow(t, 2 * D_MODEL, BF16, tr), _oacc(2 * D_MODEL)])


def _gate_norm_bwd(g_y4, y_ssd, xa, proj, dskip_x, norm_w, tr=256):
    t = y_ssd.shape[0]
    gw = SSM_INNER // SSM_GROUPS

    def body(g_ref, y_ref, xs_ref, z_ref, d_ref, w_ref, gy2_ref, gz_ref, gnw_ref, gdx_ref, gd_ref):
        z = z_ref[...].astype(F32)
        xs = xs_ref[...]
        sg = _sigmoid(z)
        sz = z * sg
        y2 = y_ref[...].astype(F32) + d_ref[...] * xs
        y3 = y2 * sz
        g4 = g_ref[...].astype(F32)
        for g in range(SSM_GROUPS):
            sl = slice(g * gw, (g + 1) * gw)
            gy3, gnw = _rms_bwd(g4[:, sl], y3[:, sl], w_ref[:, sl])
            _accumulate(gnw_ref.at[:, sl], gnw)
            gy2 = gy3 * sz[:, sl]
            gy2_ref[:, sl] = gy2
            gz_ref[:, sl] = (gy3 * y2[:, sl] * (sg[:, sl] * (1.0 + z[:, sl] * (1.0 - sg[:, sl])))).astype(BF16)
            _accumulate(gdx_ref.at[:, sl], _colsum(gy2 * xs[:, sl]))
        tot = jnp.broadcast_to(gdx_ref[...], (8, SSM_INNER))
        gd_ref[...] = jnp.dot(tot, _head_reduce(SSM_HEADS, SSM_HEAD_DIM, LANE), precision=HI, preferred_element_type=F32)[0:1, :]

    return _rowcall(body, "gate_norm_bwd", t, tr,
                    [_rows(g_y4, tr), _rows(y_ssd, tr), _rows(xa, tr, SSM_INNER, 0), _rows(proj, tr, SSM_INNER, OFF_Z // SSM_INNER), _whole(dskip_x), _whole(norm_w)],
                    [_orow(t, SSM_INNER, F32, tr), _orow(t, SSM_INNER, BF16, tr), _oacc(SSM_INNER), _oacc(SSM_INNER), _oacc(LANE)])


def _dt_bwd(g_dtx, ga_rows, proj, dt_bias_pad, tr=512):
    t = proj.shape[0]

    def body(g_ref, ga_ref, raw_ref, b_ref, o_ref, gb_ref, gal_ref):
        red = _head_reduce(SSM_HEADS, SSM_HEAD_DIM, LANE)
        gdt = _dot_split(g_ref[...], red, 0, 3)
        graw = gdt * _sigmoid(raw_ref[...] + b_ref[...])
        o_ref[...] = graw.astype(BF16)
        _accumulate(gb_ref, _colsum(graw))
        tot = jnp.broadcast_to(_colsum(ga_ref[...]), (8, SSM_INNER))
        gal_ref[...] = jnp.dot(tot, red, precision=HI, preferred_element_type=F32)[0:1, :]

    return _rowcall(body, "dt_bwd", t, tr, [_rows(g_dtx, tr), _whole(ga_rows), _rows(proj, tr, LANE, PB_DT // LANE), _whole(dt_bias_pad)],
                    [_orow(t, LANE, BF16, tr), _oacc(LANE), _oacc(LANE)])


def _conv_bwd(g_xs, g_b, g_c, xc, proj, conv_w, tr=256):
    t = proj.shape[0]
    n_blk = t // tr
    cb = OFF_XBC // CONV_DIM
    nb, nc = SSM_INNER, SSM_INNER + SSM_GROUPS * SSM_STATE
    halo_rows = 16

    def nxt(arr, width):
        return (arr, (halo_rows, width), lambda i: (jnp.minimum((i + 1) * (tr // halo_rows), t // halo_rows - 1), 0))

    def body(gxs_ref, gxs_n, gb_ref, gb_n, gc_ref, gc_n, xc_ref, xc_n, x_ref, w_ref, o_ref, gcb_ref, gw0, gw1, gw2, gw3, ext):
        def store_gxc(rows, n_rows, gxs, gb, gc, xc, keep):
            xcf = xc[...].astype(F32)
            sg = _sigmoid(xcf)
            dsilu = jnp.where(keep, sg * (1.0 + xcf * (1.0 - sg)), 0.0)
            ext[pl.ds(rows, n_rows), :nb] = gxs[...] * dsilu[:, :nb]
            ext[pl.ds(rows, n_rows), nb:nc] = gb[...] * dsilu[:, nb:nc]
            ext[pl.ds(rows, n_rows), nc:] = gc[...] * dsilu[:, nc:]

        store_gxc(0, tr, gxs_ref, gb_ref, gc_ref, xc_ref, True)
        store_gxc(tr, halo_rows, gxs_n, gb_n, gc_n, xc_n, pl.program_id(0) < n_blk - 1)
        x = x_ref[...].astype(F32)
        acc = None
        for k, gw in enumerate((gw0, gw1, gw2, gw3)):
            shifted = ext[pl.ds(3 - k, tr), :]
            term = w_ref[k:k + 1, :] * shifted
            acc = term if acc is None else acc + term
            _accumulate(gw, _colsum(shifted * x))
            if k == SSM_CONV - 1:
                _accumulate(gcb_ref, _colsum(shifted))
        o_ref[...] = acc.astype(BF16)

    ins = []
    for arr, width in ((g_xs, SSM_INNER), (g_b, nc - nb), (g_c, nc - nb), (xc, CONV_DIM)):
        ins += [_rows(arr, tr), nxt(arr, width)]
    ins += [_rows(proj, tr, CONV_DIM, cb), _whole(conv_w)]
    return _rowcall(body, "conv_bwd", t, tr, ins, [_orow(t, CONV_DIM, BF16, tr)] + [_oacc(CONV_DIM)] * 5,
                    scratch=[pltpu.VMEM((tr + halo_rows, CONV_DIM), F32)])


def _pre_norm_bwd(g_h1, g_u, x, w_pre, tr=512):
    t = x.shape[0]

    def body(gh_ref, gu_ref, x_ref, w_ref, gx_ref, gw_ref):
        gx, gw = _rms_bwd(gu_ref[...], x_ref[...], w_ref[...])
        gx_ref[...] = gh_ref[...] + gx
        _accumulate(gw_ref, gw)

    return _rowcall(body, "pre_norm_bwd", t, tr, [_rows(g_h1, tr), _rows(g_u, tr), _rows(x, tr), _whole(w_pre)],
                    [_orow(t, D_MODEL, F32, tr), _oacc(D_MODEL)])


def _alibi_slopes(n):
    def pow2(m):
        start = 2.0 ** (-8.0 / m)
        return [start ** (i + 1) for i in range(m)]
    if (n & (n - 1)) == 0:
        s = pow2(n)
    else:
        c = 2 ** int(math.floor(math.log2(n)))
        s = pow2(c) + pow2(2 * c)[0::2][: n - c]
    return np.array(s, dtype=np.float32)


def _slope_rows():
    s = _alibi_slopes(N_ATT_HEADS).reshape(N_ATT_HEADS // 2, 2)
    return jnp.asarray(np.broadcast_to(np.repeat(s, HEAD_DIM, axis=1)[:, None, :], (N_ATT_HEADS // 2, 8, LANE)).copy())


ATT_MAX_BLOCK_ROWS = 2048


RESIDUE_MAJOR_FROM = 16


class _AttLayout:
    def __init__(self, t, dil):
        self.t, self.dil = t, dil
        self.rows = t // dil
        self.residue_major = dil >= RESIDUE_MAJOR_FROM
        if self.residue_major:
            bq, self.stride = min(512, self.rows), 1
        else:
            bq, self.stride = min(512, self.rows, ATT_MAX_BLOCK_ROWS // dil), dil
        self.nsub = bq // ATT_BLOCK
        self.nblk = self.rows // bq
        self.rb = bq * self.stride
        self.pb = ATT_BLOCK * self.stride
        self.n_pb = self.rows * self.stride // self.pb
        self.out_dtype = F32 if self.stride > 1 else BF16

    def qkv(self, proj):
        if self.residue_major:
            qkv = proj[:, :3 * ATT_WIDTH]
            return qkv.reshape(self.rows, self.dil * 3 * ATT_WIDTH), 3 * ATT_WIDTH // LANE, 0
        return proj, 0, 0

    def act(self, a):
        return a.reshape(self.rows, self.dil * ATT_WIDTH) if self.residue_major else a

    def act_shape(self):
        return (self.rows, self.dil * ATT_WIDTH) if self.residue_major else (self.t, ATT_WIDTH)

    def col(self, r, band, c):
        return r * band + c if self.residue_major else c


def _residue_rows(r, stride, first_block, n_blocks=1):
    if stride == 1:
        return pl.ds(first_block * ATT_BLOCK, n_blocks * ATT_BLOCK)
    return pl.ds(r + first_block * ATT_BLOCK * stride, n_blocks * ATT_BLOCK, stride=stride)


def _lane_half():
    return lax.broadcasted_iota(jnp.int32, (ATT_BLOCK, LANE), 1) // HEAD_DIM


def _att_scores_mask(dil, first):
    iq = lax.broadcasted_iota(jnp.int32, (ATT_BLOCK, 2 * ATT_BLOCK), 0)
    jk = lax.broadcasted_iota(jnp.int32, (ATT_BLOCK, 2 * ATT_BLOCK), 1)
    dist = ATT_BLOCK + iq - jk
    valid = (dist >= 0) & (dist <= ATT_BLOCK) & (jnp.logical_not(first) | (jk >= ATT_BLOCK))
    return (dist * dil).astype(F32), valid


def _att_fwd(proj, dil, slopes):
    t = proj.shape[0]
    lay = _AttLayout(t, dil)
    nsub, nblk, rb, pb = lay.nsub, lay.nblk, lay.rb, lay.pb
    src, band, qb = lay.qkv(proj)
    aw = ATT_WIDTH // LANE

    def spec(off, prev=False):
        if prev:
            return pl.BlockSpec((pb, LANE), lambda hp, i, r: (jnp.maximum(i * nsub - 1, 0), lay.col(r, band, qb + off + hp)))
        return pl.BlockSpec((rb, LANE), lambda hp, i, r: (i, lay.col(r, band, qb + off + hp)))

    o_spec = pl.BlockSpec((rb, LANE), lambda hp, i, r: (i, lay.col(r, aw, hp)))

    def body(q_ref, kc_ref, kp_ref, vc_ref, vp_ref, sl_ref, o_ref, lse_ref):
        i, r = pl.program_id(1), pl.program_id(2)
        half = _lane_half()
        for sub in range(nsub):
            rs = _residue_rows(r, lay.stride, sub)
            q = (q_ref[rs, :] * (HEAD_DIM ** -0.5)).astype(BF16)
            if sub == 0:
                r0 = _residue_rows(r, lay.stride, 0)
                kk = jnp.concatenate([kp_ref[r0, :], kc_ref[rs, :]], axis=0).astype(BF16)
                vv = jnp.concatenate([vp_ref[r0, :], vc_ref[rs, :]], axis=0).astype(BF16)
                first = i == 0
            else:
                ks = _residue_rows(r, lay.stride, sub - 1, 2)
                kk, vv = kc_ref[ks, :].astype(BF16), vc_ref[ks, :].astype(BF16)
                first = jnp.bool_(False)
            dist, valid = _att_scores_mask(dil, first)
            outs, lses = [], []
            for e in range(2):
                qe = jnp.where(half == e, q, jnp.zeros_like(q))
                s = lax.dot_general(qe, kk, NT_DIMS, preferred_element_type=F32)
                s = s + jnp.where(valid, -sl_ref[0:1, e * HEAD_DIM:e * HEAD_DIM + 1] * dist, NEG)
                m = jnp.max(s, axis=-1, keepdims=True)
                p = jnp.exp(s - m)
                l = jnp.sum(p, axis=-1, keepdims=True)
                outs.append(jnp.dot(p.astype(BF16), vv, preferred_element_type=F32) / l)
                lses.append(m + jnp.log(l))
            o_ref[rs, :] = jnp.where(half == 0, outs[0], outs[1]).astype(lay.out_dtype)
            lse_ref[rs, :] = jnp.where(half == 0, lses[0], lses[1])

    o, lse = pl.pallas_call(
        body, grid=(N_ATT_HEADS // 2, nblk, dil),
        in_specs=[spec(0), spec(6), spec(6, True), spec(12), spec(12, True), pl.BlockSpec((None, 8, LANE), lambda hp, i, r: (hp, 0, 0))],
        out_specs=[o_spec, o_spec], out_shape=[S(lay.act_shape(), lay.out_dtype), S(lay.act_shape(), F32)],
        name=f"att_fwd_d{dil}", compiler_params=_params(("parallel", "parallel", "arbitrary")),
    )(src, src, src, src, src, slopes)
    return o.reshape(t, ATT_WIDTH), lse.reshape(t, ATT_WIDTH)


def _att_combine(outs, lses, tr=512):
    t = outs[0].shape[0]

    def body(o0, o1, o2, l0, l1, l2, att_ref, lse_ref):
        ls = [l0[...], l1[...], l2[...]]
        m = jnp.maximum(jnp.maximum(ls[0], ls[1]), ls[2])
        ws = [jnp.exp(l - m) for l in ls]
        tot = ws[0] + ws[1] + ws[2]
        num = ws[0] * o0[...].astype(F32) + ws[1] * o1[...].astype(F32) + ws[2] * o2[...].astype(F32)
        att_ref[...] = (num / tot).astype(BF16)
        lse_ref[...] = m + jnp.log(tot)

    return _rowcall(body, "att_combine", t, tr, [_rows(a, tr) for a in list(outs) + list(lses)],
                    [_orow(t, ATT_WIDTH, BF16, tr), _orow(t, ATT_WIDTH, F32, tr)])


def _att_delta(g_att, att, tr=512):
    t = att.shape[0]

    def body(g_ref, a_ref, o_ref):
        prod = g_ref[...] * a_ref[...].astype(F32)
        o_ref[...] = _dot_split(prod, _block_ones(ATT_WIDTH, HEAD_DIM), 0, 3)

    return _rowcall(body, "att_delta", t, tr, [_rows(g_att, tr), _rows(att, tr)], [_orow(t, ATT_WIDTH, F32, tr)])[0]


def _att_bwd(proj, g_att, lse, delta, dil, slopes):
    t = proj.shape[0]
    lay = _AttLayout(t, dil)
    nsub, nblk, rb, pb, n_pb = lay.nsub, lay.nblk, lay.rb, lay.pb, lay.n_pb
    src, band, qb = lay.qkv(proj)
    aw = ATT_WIDTH // LANE

    def near(i, which):
        return jnp.maximum(i * nsub - 1, 0) if which == "prev" else jnp.minimum((i + 1) * nsub, n_pb - 1)

    def pspec(off, which=None):
        if which:
            return pl.BlockSpec((pb, LANE), lambda hp, i, r: (near(i, which), lay.col(r, band, qb + off + hp)))
        return pl.BlockSpec((rb, LANE), lambda hp, i, r: (i, lay.col(r, band, qb + off + hp)))

    def aspec(which=None):
        if which:
            return pl.BlockSpec((pb, LANE), lambda hp, i, r: (near(i, which), lay.col(r, aw, hp)))
        return pl.BlockSpec((rb, LANE), lambda hp, i, r: (i, lay.col(r, aw, hp)))

    scale = HEAD_DIM ** -0.5

    def body(q_ref, qn_ref, kc_ref, kp_ref, vc_ref, vp_ref, do_ref, don_ref, lse_ref, lsen_ref, dl_ref, dln_ref, sl_ref,
             dq_ref, dk_ref, dv_ref):
        i, r = pl.program_id(1), pl.program_id(2)
        half = _lane_half()

        def tile_grads(q, do, lse_q, dl_q, kk, vv, dist, valid):
            dqs, dks, dvs = [], [], []
            for e in range(2):
                c = e * HEAD_DIM
                qe = jnp.where(half == e, q, jnp.zeros_like(q))
                doe = jnp.where(half == e, do, jnp.zeros_like(do))
                s = lax.dot_general(qe, kk, NT_DIMS, preferred_element_type=F32)
                s = s + jnp.where(valid, -sl_ref[0:1, c:c + 1] * dist, NEG)
                p = jnp.exp(s - lse_q[:, c:c + 1])
                dp = lax.dot_general(doe, vv, NT_DIMS, preferred_element_type=F32)
                ds16 = (p * (dp - dl_q[:, c:c + 1])).astype(BF16)
                dqs.append(jnp.dot(ds16, kk, preferred_element_type=F32))
                dks.append(lax.dot_general(ds16, q, TN_DIMS, preferred_element_type=F32))
                dvs.append(lax.dot_general(p.astype(BF16), do, TN_DIMS, preferred_element_type=F32))
            halfk = lax.broadcasted_iota(jnp.int32, dks[0].shape, 1) // HEAD_DIM
            return (jnp.where(half == 0, dqs[0], dqs[1]) * scale,
                    jnp.where(halfk == 0, dks[0], dks[1]), jnp.where(halfk == 0, dvs[0], dvs[1]))

        carry_k = carry_v = None
        for sub in range(nsub):
            rs = _residue_rows(r, lay.stride, sub)
            q = (q_ref[rs, :] * scale).astype(BF16)
            do = do_ref[rs, :].astype(BF16)
            if sub == 0:
                r0 = _residue_rows(r, lay.stride, 0)
                kk = jnp.concatenate([kp_ref[r0, :], kc_ref[rs, :]], axis=0).astype(BF16)
                vv = jnp.concatenate([vp_ref[r0, :], vc_ref[rs, :]], axis=0).astype(BF16)
                first = i == 0
            else:
                ks = _residue_rows(r, lay.stride, sub - 1, 2)
                kk, vv = kc_ref[ks, :].astype(BF16), vc_ref[ks, :].astype(BF16)
                first = jnp.bool_(False)
            dist, valid = _att_scores_mask(dil, first)
            dq, dk2, dv2 = tile_grads(q, do, lse_ref[rs, :], dl_ref[rs, :], kk, vv, dist, valid)
            dq_ref[rs, :] = dq.astype(lay.out_dtype)
            if sub > 0:
                rp = _residue_rows(r, lay.stride, sub - 1)
                dk_ref[rp, :] = (carry_k + dk2[:ATT_BLOCK, :]).astype(lay.out_dtype)
                dv_ref[rp, :] = (carry_v + dv2[:ATT_BLOCK, :]).astype(lay.out_dtype)
            carry_k, carry_v = dk2[ATT_BLOCK:, :], dv2[ATT_BLOCK:, :]
        rl = _residue_rows(r, lay.stride, nsub - 1)
        rn = _residue_rows(r, lay.stride, 0)
        iq = lax.broadcasted_iota(jnp.int32, (ATT_BLOCK, ATT_BLOCK), 0)
        jk = lax.broadcasted_iota(jnp.int32, (ATT_BLOCK, ATT_BLOCK), 1)
        dist_i = ATT_BLOCK + iq - jk
        valid = (dist_i >= 0) & (dist_i <= ATT_BLOCK) & (i < nblk - 1)
        qn = (qn_ref[rn, :] * scale).astype(BF16)
        _, dk1, dv1 = tile_grads(qn, don_ref[rn, :].astype(BF16), lsen_ref[rn, :], dln_ref[rn, :],
                                 kc_ref[rl, :].astype(BF16), vc_ref[rl, :].astype(BF16), (dist_i * dil).astype(F32), valid)
        dk_ref[rl, :] = (carry_k + dk1).astype(lay.out_dtype)
        dv_ref[rl, :] = (carry_v + dv1).astype(lay.out_dtype)

    gv, lv, dlv = lay.act(g_att), lay.act(lse), lay.act(delta)
    dq, dk, dv = pl.pallas_call(
        body, grid=(N_ATT_HEADS // 2, nblk, dil),
        in_specs=[pspec(0), pspec(0, "next"), pspec(6), pspec(6, "prev"), pspec(12), pspec(12, "prev"),
                  aspec(), aspec("next"), aspec(), aspec("next"), aspec(), aspec("next"),
                  pl.BlockSpec((None, 8, LANE), lambda hp, i, r: (hp, 0, 0))],
        out_specs=[aspec(), aspec(), aspec()], out_shape=[S(lay.act_shape(), lay.out_dtype)] * 3,
        name=f"att_bwd_d{dil}", compiler_params=_params(("parallel", "parallel", "arbitrary")),
    )(src, src, src, src, src, src, gv, gv, lv, lv, dlv, dlv, slopes)
    return dq.reshape(t, ATT_WIDTH), dk.reshape(t, ATT_WIDTH), dv.reshape(t, ATT_WIDTH)


def _att_grad_sum(dqs, dks, dvs, tr=512):
    t = dqs[0].shape[0]

    def body(*refs):
        o_ref = refs[-1]
        for n in range(3):
            tot = refs[3 * n][...].astype(F32) + refs[3 * n + 1][...].astype(F32) + refs[3 * n + 2][...].astype(F32)
            o_ref[:, n * ATT_WIDTH:(n + 1) * ATT_WIDTH] = tot.astype(BF16)

    return _rowcall(body, "att_grad_sum", t, tr, [_rows(a, tr) for a in list(dqs) + list(dks) + list(dvs)],
                    [_orow(t, 3 * ATT_WIDTH, BF16, tr)])[0]


def _ssd_common(xs, dtx, cs, cs_t):
    ch = SSM_CHUNK
    row = lax.broadcasted_iota(jnp.int32, (ch, ch), 0)
    col = lax.broadcasted_iota(jnp.int32, (ch, ch), 1)
    cs_last = cs[ch - 1:ch, :]
    return dict(tril=col <= row, row=row, col=col, cs=cs, cs_t=cs_t, cs_last=cs_last,
                e=jnp.exp(cs), w=jnp.exp(cs_last - cs), xd=xs * dtx)


def _dot_split(a, b, split, terms=2):
    ops = [a, b]
    rest = ops[split]
    other = ops[1 - split].astype(BF16)
    out = None
    for _ in range(terms):
        piece = rest.astype(BF16)
        rest = rest - piece.astype(F32)
        part = jnp.dot(other, piece, preferred_element_type=F32) if split == 1 else jnp.dot(piece, other, preferred_element_type=F32)
        out = part if out is None else out + part
    return out


def _decay_col(cs_t, heads_per_group):
    r = lax.broadcasted_iota(jnp.int32, (heads_per_group * SSM_HEAD_DIM, SSM_STATE), 0) // SSM_HEAD_DIM
    out = jnp.zeros((heads_per_group * SSM_HEAD_DIM, SSM_STATE), F32)
    for j in range(heads_per_group):
        out = jnp.where(r == j, jnp.exp(cs_t[j:j + 1, SSM_CHUNK - 1:SSM_CHUNK]), out)
    return out


SSD_GROUPS_PER_STEP = 2


def _ssd_specs(t):
    hg = SSM_HEADS // SSM_GROUPS
    gw = hg * SSM_HEAD_DIM
    nb0 = SSM_INNER // SSM_STATE
    return hg, gw, nb0


def _ssd_group_views(gi, gw, wide, narrow, stacked):
    w = [r.at[:, pl.ds(gi * gw, gw)] for r in wide]
    n = [r.at[:, pl.ds(gi * SSM_STATE, SSM_STATE)] for r in narrow]
    return w, n, [r.at[gi] for r in stacked]


def _ssd_fwd(xa, dtx, csx, cst_g):
    t = xa.shape[0]
    nch = t // SSM_CHUNK
    hg, gw, nb0 = _ssd_specs(t)
    ch = SSM_CHUNK
    gp = SSD_GROUPS_PER_STEP

    def body(xs_ref, b_ref, c_ref, dtx_ref, cs_ref, cst_ref, y_ref, st_ref, h_scr):
        for gi in range(gp):
            (xs_g, dtx_g, cs_g, y_g), (b_g, c_g), (cst_gi, st_g) = _ssd_group_views(
                gi, gw, (xs_ref, dtx_ref, cs_ref, y_ref), (b_ref, c_ref), (cst_ref, st_ref))
            group_body(pl.program_id(0), pl.program_id(1) * gp + gi, xs_g, b_g, c_g, dtx_g, cs_g, cst_gi, y_g, st_g, h_scr)

    def group_body(cc, g, xs_ref, b_ref, c_ref, dtx_ref, cs_ref, cst_ref, y_ref, st_ref, h_scr):
        @pl.when(cc == 0)
        def _():
            h_scr[g] = jnp.zeros((gw, SSM_STATE), F32)

        q = _ssd_common(xs_ref[...], dtx_ref[...], cs_ref[...], cst_ref[...])
        bb, cb = b_ref[...].astype(BF16), c_ref[...].astype(BF16)
        cbm = lax.dot_general(cb, bb, NT_DIMS, preferred_element_type=F32)
        h = h_scr[g]
        st_ref[...] = h
        xd16 = q["xd"].astype(BF16)
        y = lax.dot_general(cb, h.astype(BF16), NT_DIMS, preferred_element_type=F32) * q["e"]
        lane_head = lax.broadcasted_iota(jnp.int32, (ch, gw), 1) // SSM_HEAD_DIM
        for j in range(hg):
            diff = q["cs"][:, j * SSM_HEAD_DIM:j * SSM_HEAD_DIM + 1] - q["cs_t"][j:j + 1, :]
            gmat = cbm * jnp.exp(jnp.where(q["tril"], diff, NEG))
            yj = jnp.dot(gmat.astype(BF16), xd16, preferred_element_type=F32)
            y = y + jnp.where(lane_head == j, yj, 0.0)
        y_ref[...] = y.astype(BF16)
        s_new = lax.dot_general((q["xd"] * q["w"]).astype(BF16), bb, TN_DIMS, preferred_element_type=F32)
        h_scr[g] = _decay_col(q["cs_t"], hg) * h + s_new

    wide = pl.BlockSpec((ch, gp * gw), lambda cc, g: (cc, g))
    return pl.pallas_call(
        body, grid=(nch, SSM_GROUPS // gp),
        in_specs=[wide,
                  pl.BlockSpec((ch, gp * SSM_STATE), lambda cc, g: (cc, nb0 // gp + g)),
                  pl.BlockSpec((ch, gp * SSM_STATE), lambda cc, g: (cc, (nb0 + SSM_GROUPS) // gp + g)),
                  wide, wide,
                  pl.BlockSpec((gp, 8, ch), lambda cc, g: (g, 0, cc))],
        out_specs=[wide, pl.BlockSpec((None, gp, gw, SSM_STATE), lambda cc, g: (cc, g, 0, 0))],
        out_shape=[S((t, SSM_INNER), BF16), S((nch, SSM_GROUPS, gw, SSM_STATE), F32)],
        scratch_shapes=[pltpu.VMEM((SSM_GROUPS, gw, SSM_STATE), F32)],
        name="ssd_fwd", compiler_params=_params(("arbitrary", "arbitrary")),
    )(xa, xa, xa, dtx, csx, cst_g)


def _ssd_bwd(xa, dtx, csx, cst_g, alog_x, g_y, states, dskip_x):
    t = xa.shape[0]
    nch = t // SSM_CHUNK
    hg, gw, nb0 = _ssd_specs(t)
    ch = SSM_CHUNK
    gp = SSD_GROUPS_PER_STEP

    def rc(cc):
        return nch - 1 - cc

    def body(xs_ref, b_ref, c_ref, dtx_ref, cs_ref, cst_ref, alx_ref, gy_ref, st_ref, dsk_ref,
             gxs_ref, gb_ref, gc_ref, gdt_ref, ga_ref, gh_scr):
        for gi in range(gp):
            wide, narrow, stacked = _ssd_group_views(
                gi, gw, (xs_ref, dtx_ref, cs_ref, alx_ref, gy_ref, dsk_ref, gxs_ref, gdt_ref, ga_ref), (b_ref, c_ref, gb_ref, gc_ref),
                (cst_ref, st_ref))
            xs_g, dtx_g, cs_g, alx_g, gy_g, dsk_g, gxs_g, gdt_g, ga_g = wide
            b_g, c_g, gb_g, gc_g = narrow
            group_body(pl.program_id(0), pl.program_id(1) * gp + gi, xs_g, b_g, c_g, dtx_g, cs_g, stacked[0], alx_g, gy_g, stacked[1],
                       dsk_g, gxs_g, gb_g, gc_g, gdt_g, ga_g, gh_scr)

    def group_body(cc, g, xs_ref, b_ref, c_ref, dtx_ref, cs_ref, cst_ref, alx_ref, gy_ref, st_ref, dsk_ref,
                   gxs_ref, gb_ref, gc_ref, gdt_ref, ga_ref, gh_scr):
        @pl.when(cc == 0)
        def _():
            gh_scr[g] = jnp.zeros((gw, SSM_STATE), F32)

        xs, dtx = xs_ref[...], dtx_ref[...]
        q = _ssd_common(xs, dtx, cs_ref[...], cst_ref[...])
        cs, cs_t, e, w, xd = q["cs"], q["cs_t"], q["e"], q["w"], q["xd"]
        bb, cb = b_ref[...].astype(BF16), c_ref[...].astype(BF16)
        gy = gy_ref[...]
        gy16, xd16 = gy.astype(BF16), xd.astype(BF16)
        h = st_ref[...]
        h16 = h.astype(BF16)
        ghn = gh_scr[g]
        ghn16 = ghn.astype(BF16)
        seg = _block_ones(gw, SSM_HEAD_DIM)
        cbm = lax.dot_general(cb, bb, NT_DIMS, preferred_element_type=F32)
        cbt = lax.dot_general(bb, cb, NT_DIMS, preferred_element_type=F32)

        gye16 = (gy * e).astype(BF16)
        chm = lax.dot_general(cb, h16, NT_DIMS, preferred_element_type=F32)
        g_c = jnp.dot(gye16, h16, preferred_element_type=F32)
        gh_off = lax.dot_general(gye16, cb, TN_DIMS, preferred_element_type=F32)
        g_e = _dot_split(gy * chm, seg, 0)
        bgs = lax.dot_general(bb, ghn16, NT_DIMS, preferred_element_type=F32)
        g_xd = w * bgs
        g_w = _dot_split(xd * bgs, seg, 0)
        g_b = jnp.dot((xd * w).astype(BF16), ghn16, preferred_element_type=F32)
        decay = _decay_col(cs_t, hg)
        gh_scr[g] = decay * ghn + gh_off
        rsum = jnp.sum(ghn * h, axis=1, keepdims=True)
        lane_head = lax.broadcasted_iota(jnp.int32, (ch, gw), 1) // SSM_HEAD_DIM
        lane_head1 = lax.broadcasted_iota(jnp.int32, (1, gw), 1) // SSM_HEAD_DIM
        g_el = jnp.zeros((1, gw), F32)
        g_cs = g_e * e - g_w * w
        upper = q["row"] <= q["col"]
        for j in range(hg):
            g_el = jnp.where(lane_head1 == j, jnp.sum(rsum[j * SSM_HEAD_DIM:(j + 1) * SSM_HEAD_DIM, :], axis=0, keepdims=True), g_el)
            csc = cs[:, j * SSM_HEAD_DIM:j * SSM_HEAD_DIM + 1]
            csr = cs_t[j:j + 1, :]
            lm = jnp.exp(jnp.where(q["tril"], csc - csr, NEG))
            lmt = jnp.exp(jnp.where(upper, csr - csc, NEG))
            gyj = jnp.where(lane_head == j, gy16, jnp.zeros_like(gy16))
            xdj = jnp.where(lane_head == j, xd16, jnp.zeros_like(xd16))
            gg = lax.dot_general(gyj, xd16, NT_DIMS, preferred_element_type=F32)
            ggt = lax.dot_general(xdj, gy16, NT_DIMS, preferred_element_type=F32)
            gcb, gcbt = gg * lm, ggt * lmt
            g_c = g_c + jnp.dot(gcb.astype(BF16), bb, preferred_element_type=F32)
            g_b = g_b + jnp.dot(gcbt.astype(BF16), cb, preferred_element_type=F32)
            gxdj = jnp.dot((cbt * lmt).astype(BF16), gy16, preferred_element_type=F32)
            g_xd = g_xd + jnp.where(lane_head == j, gxdj, 0.0)
            d_cs = jnp.sum(gcb * cbm, axis=1, keepdims=True) - jnp.sum(gcbt * cbt, axis=1, keepdims=True)
            g_cs = g_cs + jnp.where(lane_head == j, d_cs, 0.0)
        extra = _colsum(g_w * w) + g_el * jnp.exp(q["cs_last"])
        g_cs = g_cs + jnp.where(lax.broadcasted_iota(jnp.int32, (ch, gw), 0) == ch - 1, extra, 0.0)
        g_la = _dot_split(upper, g_cs, 1)
        a_x = -jnp.exp(alx_ref[...])
        gdt_ref[...] = g_xd * xs + g_la * a_x * (1.0 / SSM_HEAD_DIM)
        ga_row = _colsum(g_la * (dtx * a_x)) * (1.0 / SSM_HEAD_DIM)
        ga_ref[...] = jnp.where(lax.broadcasted_iota(jnp.int32, (8, gw), 0) == 0, ga_row, 0.0)
        gxs_ref[...] = g_xd * dtx + gy * dsk_ref[...]
        gb_ref[...] = g_b
        gc_ref[...] = g_c

    wide = pl.BlockSpec((ch, gp * gw), lambda cc, g: (rc(cc), g))
    narrow = pl.BlockSpec((ch, gp * SSM_STATE), lambda cc, g: (rc(cc), g))
    row = pl.BlockSpec((1, gp * gw), lambda cc, g: (0, g))
    return pl.pallas_call(
        body, grid=(nch, SSM_GROUPS // gp),
        in_specs=[wide,
                  pl.BlockSpec((ch, gp * SSM_STATE), lambda cc, g: (rc(cc), nb0 // gp + g)),
                  pl.BlockSpec((ch, gp * SSM_STATE), lambda cc, g: (rc(cc), (nb0 + SSM_GROUPS) // gp + g)),
                  wide, wide,
                  pl.BlockSpec((gp, 8, ch), lambda cc, g: (g, 0, rc(cc))),
                  row, wide,
                  pl.BlockSpec((None, gp, gw, SSM_STATE), lambda cc, g: (rc(cc), g, 0, 0)),
                  row],
        out_specs=[wide, narrow, narrow, wide, pl.BlockSpec((8, gp * gw), lambda cc, g: (rc(cc), g))],
        out_shape=[S((t, SSM_INNER), F32), S((t, SSM_GROUPS * SSM_STATE), F32), S((t, SSM_GROUPS * SSM_STATE), F32),
                   S((t, SSM_INNER), F32), S((nch * 8, SSM_INNER), F32)],
        scratch_shapes=[pltpu.VMEM((SSM_GROUPS, gw, SSM_STATE), F32)],
        name="ssd_bwd", compiler_params=_params(("arbitrary", "arbitrary")),
    )(xa, xa, xa, dtx, csx, cst_g, alog_x, g_y, states, dskip_x)


def _local_step(x, target, w_pre, w_in_r, b_gate, conv_w, conv_b, dt_bias, a_log, d_skip, ssm_norm_w,
                late_weights, w_post, w_fpre, w_fpost, on_mid_grads, on_in_proj_grads):
    t = x.shape[0]
    mm = functools.partial(_matmul, tm=512)
    slopes = _slope_rows()
    hg = SSM_HEADS // SSM_GROUPS
    dt_bias_pad = jnp.pad(dt_bias, ((0, 0), (0, LANE - SSM_HEADS)))
    alog_x = jnp.repeat(a_log, SSM_HEAD_DIM, axis=1)
    alog_pad = jnp.pad(a_log, ((0, 0), (0, LANE - SSM_HEADS)))
    dskip_x = jnp.repeat(d_skip, SSM_HEAD_DIM, axis=1)

    u = _pre_norm(x, w_pre)
    pa = mm(u, w_in_r, mode="nn", out_dtype=BF16, name="in_proj_zgx", tn=2048, tk=D_MODEL, b_cols=(0, PA_W))
    pb = mm(u, w_in_r, mode="nn", out_dtype=F32, name="in_proj_qkvdt", tn=512, tk=D_MODEL, b_cols=(PA_W, PB_W))
    fwd = [_att_fwd(pb, dil, slopes) for _, dil in DILATED_PATTERNS]
    att, lse = _att_combine([o for o, _ in fwd], [l for _, l in fwd])
    xa, xc = _conv_fwd(pa, conv_w, conv_b)
    dtx, csx, cst = _dt_fwd(pb, dt_bias_pad, alog_pad)
    cst_g = jnp.pad(cst[:SSM_HEADS].reshape(SSM_GROUPS, hg, t), ((0, 0), (0, 8 - hg), (0, 0)))
    y_ssd, states = _ssd_fwd(xa, dtx, csx, cst_g)
    y4 = _gate_norm_fwd(y_ssd, xa, pa, dskip_x, ssm_norm_w)
    w_att, w_ssm, w_out, w_up, w_down = late_weights(y4)
    att_p = mm(att, w_att, mode="nn", out_dtype=BF16, name="att_proj", tn=D_MODEL, tk=ATT_WIDTH)
    ssm_p = mm(y4, w_ssm, mode="nn", out_dtype=BF16, name="ssm_proj", tn=D_MODEL, tk=SSM_INNER)
    mixin = _gating_fwd(pa, b_gate, att_p, ssm_p)
    mixed = mm(mixin, w_out, mode="nn", out_dtype=F32, name="out_proj", tn=D_MODEL, tk=D_MODEL)
    h1, f = _mix_post_ffn_pre(x, mixed, w_post, w_fpre)
    act, up = _matmul(f, w_up, mode="nn", out_dtype=BF16, name="ffn_up", tm=2048, tn=FFN_HIDDEN // N_DEV, tk=D_MODEL, epilogue="relu2", stacked=True)
    dn = mm(act, w_down, mode="nn", out_dtype=F32, name="ffn_down", tn=D_MODEL, tk=FFN_HIDDEN)
    loss, g_h2, g_dn, gw_fpost = _loss_and_ffn_post_bwd(h1, dn, w_fpost, target)

    g_up = mm(g_dn, w_down, mode="nt", out_dtype=BF16, name="ffn_down_bwd_x", tn=2048, tk=D_MODEL, epilogue="relu2_bwd", extra=up)
    gw_down = _matmul(act, g_dn, mode="tn", out_dtype=BF16, name="ffn_down_bwd_w", tm=1024, tn=D_MODEL, tk=512)
    g_f = _matmul(g_up, w_up, mode="nt", out_dtype=F32, name="ffn_up_bwd_x", tm=2048, tn=D_MODEL, tk=FFN_HIDDEN // N_DEV, stacked=True)
    gw_up = _matmul(f, g_up, mode="tn", out_dtype=BF16, name="ffn_up_bwd_w", tm=D_MODEL, tn=FFN_HIDDEN // N_DEV, tk=2048, stacked=True)
    g_h1, g_mixed, gw_fpre, gw_post = _ffn_pre_mix_post_bwd(g_h2, g_f, h1, w_fpre, mixed, w_post)
    g_mixin = mm(g_mixed, w_out, mode="nt", out_dtype=BF16, name="out_proj_bwd_x", tn=D_MODEL, tk=D_MODEL)
    gw_out = _matmul(mixin, g_mixed, mode="tn", out_dtype=BF16, name="out_proj_bwd_w", tm=D_MODEL, tn=D_MODEL, tk=512)
    g_att_p, g_ssm_p, g_gl, g_b_gate = _gating_bwd(g_mixin, pa, b_gate, att_p, ssm_p)
    g_att = mm(g_att_p, w_att, mode="nt", out_dtype=F32, name="att_proj_bwd_x", tn=ATT_WIDTH, tk=D_MODEL)
    gw_att = _matmul(att, g_att_p, mode="tn", out_dtype=BF16, name="att_proj_bwd_w", tm=ATT_WIDTH, tn=D_MODEL, tk=512)
    g_y4 = mm(g_ssm_p, w_ssm, mode="nt", out_dtype=BF16, name="ssm_proj_bwd_x", tn=SSM_INNER, tk=D_MODEL)
    gw_ssm = _matmul(y4, g_ssm_p, mode="tn", out_dtype=BF16, name="ssm_proj_bwd_w", tm=1024, tn=D_MODEL, tk=512)
    token = on_mid_grads(dict(w_att_proj=gw_att, w_ssm_proj=gw_ssm, w_out=gw_out, w_up=gw_up, w_down=gw_down))
    if token is not None:
        ssm_norm_w = ssm_norm_w + jnp.tile(token[0:1, :], (1, SSM_INNER // LANE))
    g_y2, g_z, g_norm_w, _, g_d_skip = _gate_norm_bwd(g_y4, y_ssd, xa, pa, dskip_x, ssm_norm_w)
    g_xs, g_bm, g_cm, g_dtx, ga_rows = _ssd_bwd(xa, dtx, csx, cst_g, alog_x, g_y2, states, dskip_x)
    g_dt_raw, g_dt_bias, g_a_log = _dt_bwd(g_dtx, ga_rows, pb, dt_bias_pad)
    g_xbc, g_conv_b, gcw0, gcw1, gcw2, gcw3 = _conv_bwd(g_xs, g_bm, g_cm, xc, pa, conv_w)
    delta = _att_delta(g_att, att)
    dqs, dks, dvs = [], [], []
    for _, dil in DILATED_PATTERNS:
        dq, dk, dv = _att_bwd(pb, g_att, lse, delta, dil, slopes)
        dqs.append(dq)
        dks.append(dk)
        dvs.append(dv)
    g_qkv = _att_grad_sum(dqs, dks, dvs)
    g_proj = jnp.concatenate([g_z, g_gl, g_xbc, g_qkv, g_dt_raw, jnp.zeros((t, PROJ_W - OFF_DT - LANE), BF16)], axis=1)
    gw_in_r = _matmul(u, g_proj, mode="tn", out_dtype=BF16, name="in_proj_bwd_w", tm=D_MODEL, tn=1792, tk=512)
    token = on_in_proj_grads(gw_in_r, jnp.concatenate([gcw0, gcw1, gcw2, gcw3], axis=0))
    g_u = _matmul(g_proj, w_in_r, mode="nt", out_dtype=F32, name="in_proj_bwd_x", tm=1024, tn=D_MODEL, tk=1792, after=token)
    g_x, gw_pre = _pre_norm_bwd(g_h1, g_u, x, w_pre)

    grads = dict(
        norm_mix_pre_w=gw_pre, b_gate=g_b_gate, conv_b=g_conv_b, dt_bias=g_dt_bias[:, :SSM_HEADS], a_log=g_a_log[:, :SSM_HEADS],
        d_skip=g_d_skip[:, :SSM_HEADS], ssm_norm_w=g_norm_w, norm_mix_post_w=gw_post, norm_ffn_pre_w=gw_fpre, norm_ffn_post_w=gw_fpost)
    return loss, g_x, grads


def _mesh_pos():
    return lax.axis_index("x"), lax.axis_index("y"), lax.axis_index("c")


def _all_gather(shards):
    n = len(shards)

    def body(*refs):
        x_refs, o_refs = refs[:n], refs[n:2 * n]
        send_sems, recv_sems, local_sems = refs[2 * n:]
        x, y, c = _mesh_pos()
        me, sibling = (x, y, c), (x, y, 1 - c)
        chips = [(1 - x, y), (x, 1 - y), (1 - x, 1 - y)]

        def copy(a, k, block, to, src=None):
            dst = o_refs[a].at[4 * block[0] + 2 * block[1] + block[2]]
            return pltpu.make_async_remote_copy(
                src_ref=dst if src is None else src, dst_ref=dst, send_sem=send_sems.at[7 * a + k], recv_sem=recv_sems.at[7 * a + k],
                device_id=to, device_id_type=pl.DeviceIdType.MESH)

        mine = [pltpu.make_async_copy(x_refs[a], o_refs[a].at[4 * x + 2 * y + c], local_sems.at[a]) for a in range(n)]
        for cp in mine:
            cp.start()
        first = []
        for a in range(n):
            first.append(copy(a, 0, me, sibling, src=x_refs[a]))
            first += [copy(a, 1 + j, me, (*chip, c), src=x_refs[a]) for j, chip in enumerate(chips)]
        for cp in first:
            cp.start()
        passed = []
        for j, chip in enumerate(chips):
            for a in range(n):
                copy(a, 1 + j, (*chip, c), me).wait_recv()
                passed.append(copy(a, 4 + j, (*chip, c), sibling))
                passed[-1].start()
        for a in range(n):
            copy(a, 0, sibling, me).wait_recv()
            for j, chip in enumerate(chips):
                copy(a, 4 + j, (*chip, 1 - c), me).wait_recv()
        for cp in first + passed:
            cp.wait_send()
        for cp in mine:
            cp.wait()

    hbm = pl.BlockSpec(memory_space=pltpu.HBM)
    return pl.pallas_call(
        body, out_shape=[S((N_DEV,) + s.shape, s.dtype) for s in shards],
        in_specs=[hbm] * n, out_specs=[hbm] * n,
        scratch_shapes=[pltpu.SemaphoreType.DMA((7 * n,)), pltpu.SemaphoreType.DMA((7 * n,)), pltpu.SemaphoreType.DMA((n,))],
        name="weights_all_gather",
    )(*shards)


def _exchange_grads(slab_arrays, small):
    n = len(slab_arrays)
    r_small = small.shape[0]

    def body(*refs):
        slab_refs, small_ref = refs[:n], refs[n]
        recv_refs, gsm_ref = refs[n + 1:2 * n + 1], refs[2 * n + 1]
        send_sems, recv_sems, local_sems = refs[2 * n + 2:]
        x, y, c = _mesh_pos()
        me = 4 * x + 2 * y + c

        def peer(k):
            px = 1 - x if k & 4 else x
            py = 1 - y if k & 2 else y
            pc = 1 - c if k & 1 else c
            return (px, py, pc), 4 * px + 2 * py + pc

        def copy(a, k, sending):
            to, lin = peer(k)
            sem = 7 * a + k - 1
            if a == n:
                src, dst = small_ref, gsm_ref.at[me if sending else lin]
            else:
                src, dst = slab_refs[a].at[lin], recv_refs[a].at[me if sending else lin]
            return pltpu.make_async_remote_copy(src_ref=src, dst_ref=dst, send_sem=send_sems.at[sem], recv_sem=recv_sems.at[sem],
                                                device_id=to, device_id_type=pl.DeviceIdType.MESH)

        own = [pltpu.make_async_copy(slab_refs[a].at[me], recv_refs[a].at[me], local_sems.at[a]) for a in range(n)]
        own.append(pltpu.make_async_copy(small_ref, gsm_ref.at[me], local_sems.at[n]))
        for cp in own:
            cp.start()
        order = [n] + list(range(n))
        sends = [copy(a, k, True) for a in order for k in range(1, N_DEV)]
        for cp in sends:
            cp.start()
        for a in order:
            for k in range(1, N_DEV):
                copy(a, k, False).wait_recv()
        for cp in sends:
            cp.wait_send()
        for cp in own:
            cp.wait()

    hbm = pl.BlockSpec(memory_space=pltpu.HBM)
    n_sem = 7 * (n + 1)
    res = pl.pallas_call(
        body, out_shape=[S(a.shape, a.dtype) for a in slab_arrays] + [S((N_DEV, r_small, LANE), small.dtype)],
        in_specs=[hbm] * (n + 1), out_specs=[hbm] * (n + 1),
        scratch_shapes=[pltpu.SemaphoreType.DMA((n_sem,)), pltpu.SemaphoreType.DMA((n_sem,)), pltpu.SemaphoreType.DMA((n + 1,))],
        name="grad_exchange",
    )(*slab_arrays, small)
    return res[:n], res[n]


def _peer_of(k, x, y, c):
    px = 1 - x if k & 4 else x
    py = 1 - y if k & 2 else y
    pc = 1 - c if k & 1 else c
    return (px, py, pc), 4 * px + 2 * py + pc


def _split_copies(src_refs, land_refs, send_sems, recv_sems, per_peer):
    x, y, c = _mesh_pos()
    me = 4 * x + 2 * y + c
    sends, recvs = [], []
    for a, (src, land) in enumerate(zip(src_refs, land_refs)):
        for k in range(1, N_DEV):
            to, lin = _peer_of(k, x, y, c)
            sem = 7 * a + k - 1
            piece = src.at[lin] if per_peer else src
            for slot, out in ((me, sends), (lin, recvs)):
                out.append(pltpu.make_async_remote_copy(
                    src_ref=piece, dst_ref=land.at[slot], send_sem=send_sems.at[sem], recv_sem=recv_sems.at[sem],
                    device_id=to, device_id_type=pl.DeviceIdType.MESH))
    return sends, recvs


def _remote_start(srcs, per_peer, name):
    n = len(srcs)
    lands = [lax.empty((N_DEV,) + (s.shape[1:] if per_peer else s.shape), s.dtype) for s in srcs]

    def body(*refs):
        src_refs, land_refs = refs[:n], refs[n:2 * n]
        send_sems, recv_sems = refs[2 * n], refs[2 * n + 1]
        token = refs[-1]
        sends, _ = _split_copies(src_refs, land_refs, send_sems, recv_sems, per_peer)
        for cp in sends:
            cp.start()
        token[...] = jnp.zeros_like(token)

    hbm = pl.BlockSpec(memory_space=pltpu.HBM)
    sem = pl.BlockSpec(memory_space=pltpu.SEMAPHORE)
    res = pl.pallas_call(
        body, name=name,
        out_shape=(pltpu.SemaphoreType.DMA((7 * n,)), pltpu.SemaphoreType.DMA((7 * n,)),
                   *[pltpu.HBM(a.shape, a.dtype) for a in srcs + lands], S((8, LANE), F32)),
        in_specs=[hbm] * (2 * n), out_specs=(sem, sem, *[hbm] * (2 * n), pl.BlockSpec(memory_space=pltpu.VMEM)),
        input_output_aliases={i: 2 + i for i in range(2 * n)},
        compiler_params=pltpu.CompilerParams(has_side_effects=pltpu.SideEffectType.DATAFLOW_SIDE_EFFECTING),
    )(*[pltpu.with_memory_space_constraint(a, pltpu.HBM) for a in srcs + lands])
    return dict(sems=res[:2], srcs=list(res[2:2 + n]), lands=list(res[2 + n:2 + 2 * n]), per_peer=per_peer), res[-1]


def _remote_wait(handle, after, name):
    n = len(handle["srcs"])
    per_peer = handle["per_peer"]

    def body(*refs):
        src_refs, land_refs = refs[:n], refs[n:2 * n]
        send_sems, recv_sems = refs[2 * n], refs[2 * n + 1]
        sends, recvs = _split_copies(src_refs, land_refs, send_sems, recv_sems, per_peer)
        for cp in sends:
            cp.wait_send()
        for cp in recvs:
            cp.wait_recv()

    hbm = pl.BlockSpec(memory_space=pltpu.HBM)
    sem = pl.BlockSpec(memory_space=pltpu.SEMAPHORE)
    arrays = handle["srcs"] + handle["lands"]
    res = pl.pallas_call(
        body, name=name, out_shape=tuple(pltpu.HBM(a.shape, a.dtype) for a in arrays),
        in_specs=[hbm] * (2 * n) + [sem, sem, pl.BlockSpec(memory_space=pl.ANY)], out_specs=tuple([hbm] * (2 * n)),
        input_output_aliases={i: i for i in range(2 * n)},
        compiler_params=pltpu.CompilerParams(has_side_effects=pltpu.SideEffectType.DATAFLOW_SIDE_EFFECTING),
    )(*arrays, *handle["sems"], after)
    return list(res[n:])


def _with_own(lands, own, me):
    return [lax.dynamic_update_index_in_dim(land, o.astype(land.dtype), me, 0) for land, o in zip(lands, own)]


def _adamw(w, m, v, slabs, name, tr):
    r, cols = w.shape
    c1 = 1.0 - ADAM_B1 ** ADAM_STEP
    c2 = 1.0 - ADAM_B2 ** ADAM_STEP

    def body(w_ref, m_ref, v_ref, s_ref, g_ref, d_ref, nm_ref, nv_ref):
        g = s_ref[0].astype(F32)
        for d in range(1, N_DEV):
            g = g + s_ref[d].astype(F32)
        nm = ADAM_B1 * m_ref[...] + (1.0 - ADAM_B1) * g
        nv = ADAM_B2 * v_ref[...] + (1.0 - ADAM_B2) * (g * g)
        g_ref[...] = g
        nm_ref[...] = nm
        nv_ref[...] = nv
        d_ref[...] = -ADAM_LR * ((nm / c1) / (jnp.sqrt(nv / c2) + ADAM_EPS) + ADAM_WD * w_ref[...])

    assert r % tr == 0, name
    blk = pl.BlockSpec((tr, cols), lambda i: (i, 0))
    return pl.pallas_call(
        body, grid=(r // tr,), in_specs=[blk, blk, blk, pl.BlockSpec((N_DEV, tr, cols), lambda i: (0, i, 0))],
        out_specs=[blk] * 4, out_shape=[S((r, cols), F32)] * 4, name=name, compiler_params=_params(("parallel",)),
    )(w, m, v, slabs)


BIG = ("w_in", "w_att_proj", "w_up", "w_ssm_proj", "w_out", "w_down", "conv_w")
ADAMW_ROWS = dict(w_in=256, w_att_proj=768, w_up=512, w_ssm_proj=256, w_out=128, w_down=256, conv_w=4)
SMALL = ("norm_mix_pre_w", "b_gate", "conv_b", "dt_bias", "a_log", "d_skip", "ssm_norm_w", "norm_mix_post_w",
         "norm_ffn_pre_w", "norm_ffn_post_w")
ORDER = ("norm_mix_pre_w", "w_in", "b_gate", "conv_w", "conv_b", "dt_bias", "a_log", "d_skip", "ssm_norm_w", "w_att_proj",
         "w_ssm_proj", "w_out", "norm_mix_post_w", "norm_ffn_pre_w", "w_up", "w_down", "norm_ffn_post_w")
ROW_SHARDED = ("w_ssm_proj", "w_out", "w_down")
LATE = ("w_att_proj", "w_ssm_proj", "w_out", "w_up", "w_down")
IN_PROJ_W = 10528
IN_SHARD_W = IN_PROJ_W // N_DEV
IN_SEGMENTS = ((2304, 4352), (8480, 10528), (4352, 8448), (0, 2304), (8448, 8480))


def _pack(parts, rows_multiple):
    flat = jnp.concatenate([p.reshape(-1) for p in parts])
    pad = (-flat.shape[0]) % (rows_multiple * LANE)
    return jnp.pad(flat, (0, pad)).reshape(-1, LANE)


def _unpack(flat2d, shapes):
    flat, out, off = flat2d.reshape(-1), [], 0
    for sh in shapes:
        n = int(np.prod(sh))
        out.append(flat[off:off + n].reshape(sh))
        off += n
    return out


def _reorder_in_proj(w):
    qkv, z, xbc = w[:, :2304], w[:, 2304:4352], w[:, 4352:8448]
    dt, gate = w[:, 8448:8480], w[:, 8480:10528]
    return jnp.concatenate([z, gate, xbc, qkv, dt, jnp.zeros((w.shape[0], PROJ_W - 10528), w.dtype)], axis=1)


def _restore_in_proj(wr):
    return jnp.concatenate([wr[:, OFF_QKV:OFF_QKV + 2304], wr[:, OFF_Z:OFF_Z + 2048], wr[:, OFF_XBC:OFF_XBC + 4096],
                            wr[:, OFF_DT:OFF_DT + 32], wr[:, OFF_GL:OFF_GL + 2048]], axis=1)


def _assemble_in_proj(g):
    pieces = []
    for lo, hi in IN_SEGMENTS:
        while lo < hi:
            d = lo // IN_SHARD_W
            end = min(hi, (d + 1) * IN_SHARD_W)
            pieces.append(g[d][:, lo - d * IN_SHARD_W:end - d * IN_SHARD_W])
            lo = end
    pieces.append(jnp.zeros((g.shape[1], PROJ_W - IN_PROJ_W), g.dtype))
    return jnp.concatenate(pieces, axis=1)


def _in_proj_slabs(wr):
    orig = _restore_in_proj(wr)
    return jnp.stack([orig[:, d * IN_SHARD_W:(d + 1) * IN_SHARD_W] for d in range(N_DEV)])


def kernel(x, norm_mix_pre_w, w_in, b_gate, conv_w, conv_b, dt_bias, a_log, d_skip, ssm_norm_w, w_att_proj, w_ssm_proj, w_out, norm_mix_post_w, norm_ffn_pre_w, w_up, w_down, norm_ffn_post_w, loss_target, m_norm_mix_pre_w, m_w_in, m_b_gate, m_conv_w, m_conv_b, m_dt_bias, m_a_log, m_d_skip, m_ssm_norm_w, m_w_att_proj, m_w_ssm_proj, m_w_out, m_norm_mix_post_w, m_norm_ffn_pre_w, m_w_up, m_w_down, m_norm_ffn_post_w, v_norm_mix_pre_w, v_w_in, v_b_gate, v_conv_w, v_conv_b, v_dt_bias, v_a_log, v_d_skip, v_ssm_norm_w, v_w_att_proj, v_w_ssm_proj, v_w_out, v_norm_mix_post_w, v_norm_ffn_pre_w, v_w_up, v_w_down, v_norm_ffn_post_w):
    w = dict(norm_mix_pre_w=norm_mix_pre_w, w_in=w_in, b_gate=b_gate, conv_w=conv_w, conv_b=conv_b, dt_bias=dt_bias, a_log=a_log,
             d_skip=d_skip, ssm_norm_w=ssm_norm_w, w_att_proj=w_att_proj, w_ssm_proj=w_ssm_proj, w_out=w_out,
             norm_mix_post_w=norm_mix_post_w, norm_ffn_pre_w=norm_ffn_pre_w, w_up=w_up, w_down=w_down, norm_ffn_post_w=norm_ffn_post_w)
    m = dict(norm_mix_pre_w=m_norm_mix_pre_w, w_in=m_w_in, b_gate=m_b_gate, conv_w=m_conv_w, conv_b=m_conv_b, dt_bias=m_dt_bias,
             a_log=m_a_log, d_skip=m_d_skip, ssm_norm_w=m_ssm_norm_w, w_att_proj=m_w_att_proj, w_ssm_proj=m_w_ssm_proj, w_out=m_w_out,
             norm_mix_post_w=m_norm_mix_post_w, norm_ffn_pre_w=m_norm_ffn_pre_w, w_up=m_w_up, w_down=m_w_down, norm_ffn_post_w=m_norm_ffn_post_w)
    v = dict(norm_mix_pre_w=v_norm_mix_pre_w, w_in=v_w_in, b_gate=v_b_gate, conv_w=v_conv_w, conv_b=v_conv_b, dt_bias=v_dt_bias,
             a_log=v_a_log, d_skip=v_d_skip, ssm_norm_w=v_ssm_norm_w, w_att_proj=v_w_att_proj, w_ssm_proj=v_w_ssm_proj, w_out=v_w_out,
             norm_mix_post_w=v_norm_mix_post_w, norm_ffn_pre_w=v_norm_ffn_pre_w, w_up=v_w_up, w_down=v_w_down, norm_ffn_post_w=v_norm_ffn_post_w)
    shard_shapes = {n: w[n].shape[1:] for n in ORDER}

    mx, my, mc = _mesh_pos()
    me = 4 * mx + 2 * my + mc

    g_in, g_conv = _all_gather([w["w_in"][0].astype(BF16), w["conv_w"][0]])
    conv_full = jnp.moveaxis(g_conv, 0, 1).reshape(SSM_CONV, CONV_DIM)
    late_shards = [w[n][0].astype(BF16) for n in LATE]
    late_handle, token = _remote_start(late_shards, False, "late_weights_start")
    w_pre = w["norm_mix_pre_w"] + jnp.tile(token[0:1, :], (1, D_MODEL // LANE))

    def late_weights(after):
        full = dict(zip(LATE, _with_own(_remote_wait(late_handle, after, "late_weights_wait"), late_shards, me)))
        for n in ROW_SHARDED:
            full[n] = full[n].reshape(-1, full[n].shape[2])
        w_att = jnp.moveaxis(full["w_att_proj"], 0, 1).reshape(ATT_WIDTH, D_MODEL)
        return w_att, full["w_ssm_proj"], full["w_out"], full["w_up"], full["w_down"]

    started = {}

    def start_exchange(tag, slabs):
        own = [lax.dynamic_index_in_dim(s, me, 0, keepdims=False) for s in slabs]
        handle, tok = _remote_start(slabs, True, tag + "_grads_start")
        started[tag] = (handle, own)
        return tok

    def on_mid_grads(g):
        slabs = dict(w_up=g["w_up"], w_att_proj=jnp.moveaxis(g["w_att_proj"].reshape(ATT_WIDTH, N_DEV, -1), 1, 0))
        for n in ROW_SHARDED:
            slabs[n] = g[n].reshape(N_DEV, -1, g[n].shape[1])
        return start_exchange("mid", [slabs[n] for n in LATE])

    def on_in_proj_grads(gw_in_r, g_conv_w):
        return start_exchange("in_proj", [_in_proj_slabs(gw_in_r), jnp.moveaxis(g_conv_w.reshape(SSM_CONV, N_DEV, -1), 1, 0)])

    loss, g_x, grads = _local_step(
        x[0], loss_target[0], w_pre, _assemble_in_proj(g_in), w["b_gate"], conv_full, w["conv_b"], w["dt_bias"], w["a_log"],
        w["d_skip"], w["ssm_norm_w"], late_weights, w["norm_mix_post_w"], w["norm_ffn_pre_w"], w["norm_ffn_post_w"],
        on_mid_grads, on_in_proj_grads)

    recv = {}
    for tag, names in (("mid", LATE), ("in_proj", ("w_in", "conv_w"))):
        handle, own = started[tag]
        recv.update(zip(names, _with_own(_remote_wait(handle, g_x, tag + "_grads_wait"), own, me)))
    small = _pack([grads[n].astype(F32) for n in SMALL], 8)
    _, small_all = _exchange_grads([], small)

    small_shapes = [shard_shapes[n] for n in SMALL]
    small_out = _adamw(*[_pack([d_[n][0] for n in SMALL], 8) for d_ in (w, m, v)], small_all, "adamw_replicated", small_all.shape[1])
    big_out = {n: _adamw(w[n][0], m[n][0], v[n][0], recv[n], "adamw_" + n, ADAMW_ROWS[n]) for n in BIG}
    res = []
    for which, small_flat in enumerate(small_out):
        vals = {n: big_out[n][which] for n in BIG}
        vals.update(zip(SMALL, _unpack(small_flat, small_shapes)))
        res.append([vals[n][None] for n in ORDER])
    g_out, d_out, m_out, v_out = res
    total = lax.psum(loss[0, 0], ("x", "y", "c"))
    return (total, g_x[None], *g_out, *d_out, *m_out, *v_out)
```

```python
import functools
import math

import jax
import jax.numpy as jnp
import numpy as np
from jax import lax
from jax.experimental import pallas as pl
from jax.experimental.pallas import tpu as pltpu

F32 = jnp.float32
BF16 = jnp.bfloat16

D_MODEL = 1024
HEAD_DIM = 64
N_ATT_HEADS = 12
ATT_WIDTH = N_ATT_HEADS * HEAD_DIM
DILATED_PATTERNS = ((128, 1), (512, 4), (2048, 16))
ATT_BLOCK = 128
SSM_INNER = 2048
SSM_HEAD_DIM = 64
SSM_HEADS = 32
SSM_GROUPS = 8
SSM_STATE = 128
SSM_CHUNK = 128
CONV_DIM = 4096
SSM_CONV = 4
FFN_HIDDEN = 4096
RMS_EPS = 1e-6
N_DEV = 8

ADAM_LR = 0.001
ADAM_B1 = 0.9
ADAM_B2 = 0.999
ADAM_EPS = 1e-08
ADAM_WD = 0.01
ADAM_STEP = 10

LANE = 128
OFF_Z, OFF_GL, OFF_XBC, OFF_QKV, OFF_DT = 0, 2048, 4096, 8192, 10496
PROJ_W = 10752
PROJ_BLOCKS = PROJ_W // LANE
PA_W = OFF_QKV
PB_W = PROJ_W - OFF_QKV
PB_DT = OFF_DT - OFF_QKV
VMEM_LIMIT = 52 * 1024 * 1024
NEG = -1e30

HI = lax.Precision.HIGHEST
NT_DIMS = (((1,), (1,)), ((), ()))
TN_DIMS = (((0,), (0,)), ((), ()))
S = jax.ShapeDtypeStruct


def _params(sem):
    return pltpu.CompilerParams(dimension_semantics=sem, vmem_limit_bytes=VMEM_LIMIT)


def _matmul(a, b, *, mode, out_dtype, name, tm, tn, tk, epilogue=None, extra=None, stacked=False, after=None, b_cols=None):
    if mode == "nn":
        m, k = a.shape
        n = b.shape[0] * b.shape[2] if stacked else b.shape[1]
        col0 = 0
        if b_cols is not None:
            assert b_cols[0] % tn == 0, name
            col0, n = b_cols[0] // tn, b_cols[1]
        a_spec = pl.BlockSpec((tm, tk), lambda i, j, kk: (i, kk))
        b_spec = pl.BlockSpec((None, tk, tn), lambda i, j, kk: (j, kk, 0)) if stacked else pl.BlockSpec((tk, tn), lambda i, j, kk: (kk, col0 + j))
        dims = (((1,), (0,)), ((), ()))
    elif mode == "nt":
        m, k = a.shape
        n = b.shape[1] if stacked else b.shape[0]
        a_spec = pl.BlockSpec((tm, tk), lambda i, j, kk: (i, kk))
        b_spec = pl.BlockSpec((None, tn, tk), lambda i, j, kk: (kk, j, 0)) if stacked else pl.BlockSpec((tn, tk), lambda i, j, kk: (j, kk))
        dims = NT_DIMS
    else:
        (k, m), n = a.shape, b.shape[1]
        a_spec = pl.BlockSpec((tk, tm), lambda i, j, kk: (kk, i))
        b_spec = pl.BlockSpec((tk, tn), lambda i, j, kk: (kk, j))
        dims = TN_DIMS
    assert m % tm == 0 and n % tn == 0 and k % tk == 0, (name, m, n, k)
    if stacked:
        assert (tk if mode == "nt" else tn) * N_DEV == (k if mode == "nt" else n), name
    nk = k // tk
    o_spec = pl.BlockSpec((tm, tn), lambda i, j, kk: (i, j))
    in_specs, args = [a_spec, b_spec], [a, b]
    if epilogue == "relu2":
        out_shape = (S((m, n), BF16), S((m, n), BF16))
        out_specs = (o_spec, o_spec)
    elif stacked and mode == "tn":
        out_shape, out_specs = S((N_DEV, m, tn), out_dtype), pl.BlockSpec((None, tm, tn), lambda i, j, kk: (j, i, 0))
    else:
        out_shape, out_specs = S((m, n), out_dtype), o_spec
    if epilogue == "relu2_bwd":
        in_specs.append(o_spec)
        args.append(extra)
    n_in = len(args)
    if after is not None:
        in_specs.append(pl.BlockSpec(after.shape, lambda i, j, kk: (0,) * after.ndim))
        args.append(after)

    def finish(acc, refs):
        if epilogue == "relu2":
            r = jnp.maximum(acc, 0.0)
            refs[0][...] = (r * r).astype(BF16)
            refs[1][...] = acc.astype(BF16)
        elif epilogue == "relu2_bwd":
            up = refs[0][...].astype(F32)
            refs[1][...] = (acc * (2.0 * jnp.maximum(up, 0.0))).astype(out_dtype)
        else:
            refs[0][...] = acc.astype(out_dtype)

    def body(a_ref, b_ref, *rest):
        rest = rest[:n_in - 2] + rest[len(args) - 2:]
        part = lax.dot_general(a_ref[...].astype(BF16), b_ref[...].astype(BF16), dims, preferred_element_type=F32)
        if nk == 1:
            finish(part, rest)
            return
        acc_ref = rest[-1]
        kk = pl.program_id(2)

        @pl.when(kk == 0)
        def _():
            acc_ref[...] = part

        @pl.when(kk > 0)
        def _():
            acc_ref[...] += part

        @pl.when(kk == nk - 1)
        def _():
            finish(acc_ref[...], rest[:-1])

    scratch = [] if nk == 1 else [pltpu.VMEM((tm, tn), F32)]
    return pl.pallas_call(
        body, grid=(m // tm, n // tn, nk), in_specs=in_specs, out_specs=out_specs, out_shape=out_shape,
        scratch_shapes=scratch, name=name, compiler_params=_params(("parallel", "parallel", "arbitrary")),
    )(*args)


def _rowcall(body, name, n_rows, tr, ins, outs, scratch=()):
    res = pl.pallas_call(
        body, grid=(n_rows // tr,),
        in_specs=[pl.BlockSpec(bs, im) for _, bs, im in ins],
        out_specs=[pl.BlockSpec(bs, im) for _, _, bs, im in outs],
        out_shape=[S(sh, dt) for sh, dt, _, _ in outs],
        scratch_shapes=list(scratch), name=name, compiler_params=_params(("arbitrary",)),
    )(*[a for a, _, _ in ins])
    return res


def _rows(arr, tr, width=None, cb=0):
    width = arr.shape[1] if width is None else width
    return (arr, (tr, width), lambda i, cb=cb: (i, cb))


def _whole(arr):
    nd = arr.ndim
    return (arr, arr.shape, lambda i, nd=nd: (0,) * nd)


def _orow(n_rows, width, dtype, tr):
    return ((n_rows, width), dtype, (tr, width), lambda i: (i, 0))


def _oacc(width):
    return ((1, width), F32, (1, width), lambda i: (0, 0))


def _accumulate(ref, value):
    first = pl.program_id(0) == 0

    @pl.when(first)
    def _():
        ref[...] = value

    @pl.when(jnp.logical_not(first))
    def _():
        ref[...] += value


def _colsum(v):
    return jnp.sum(v, axis=0, keepdims=True)


def _rms_fwd(x, w):
    r = lax.rsqrt(jnp.mean(x * x, axis=-1, keepdims=True) + RMS_EPS)
    return x * r * w


def _rms_bwd(gy, x, w):
    r = lax.rsqrt(jnp.mean(x * x, axis=-1, keepdims=True) + RMS_EPS)
    xn = x * r
    gxn = gy * w
    gx = r * (gxn - xn * jnp.mean(gxn * xn, axis=-1, keepdims=True))
    return gx, _colsum(gy * xn)


def _sigmoid(x):
    return 1.0 / (1.0 + jnp.exp(-x))


def _head_expand(n_heads_pad, n_heads, width):
    h = lax.broadcasted_iota(jnp.int32, (n_heads_pad, n_heads * width), 0)
    c = lax.broadcasted_iota(jnp.int32, (n_heads_pad, n_heads * width), 1)
    return (c // width == h).astype(F32)


def _head_reduce(n_heads, width, n_heads_pad):
    c = lax.broadcasted_iota(jnp.int32, (n_heads * width, n_heads_pad), 0)
    h = lax.broadcasted_iota(jnp.int32, (n_heads * width, n_heads_pad), 1)
    return (c // width == h).astype(F32)


def _block_ones(n, width):
    r = lax.broadcasted_iota(jnp.int32, (n, n), 0)
    c = lax.broadcasted_iota(jnp.int32, (n, n), 1)
    return (r // width == c // width).astype(F32)


def _pre_norm(x, w_pre, tr=512):
    t = x.shape[0]

    def body(x_ref, w_ref, u_ref):
        u_ref[...] = _rms_fwd(x_ref[...], w_ref[...]).astype(BF16)

    return _rowcall(body, "pre_norm", t, tr, [_rows(x, tr), _whole(w_pre)], [_orow(t, D_MODEL, BF16, tr)])[0]


def _conv_fwd(proj, conv_w, conv_b, tr=256):
    t = proj.shape[0]
    cb = OFF_XBC // CONV_DIM
    hr = 16
    halo = (proj, (hr, CONV_DIM), lambda i: (jnp.maximum(i * (tr // hr) - 1, 0), cb))

    def body(cur_ref, prev_ref, w_ref, b_ref, o_ref, xc_ref, ext):
        cur = cur_ref[...].astype(F32)
        ext[pl.ds(0, hr), :] = jnp.where(pl.program_id(0) > 0, prev_ref[...].astype(F32), 0.0)
        ext[pl.ds(hr, tr), :] = cur
        acc = b_ref[...] + w_ref[3:4, :] * cur
        for k in range(SSM_CONV - 1):
            acc = acc + w_ref[k:k + 1, :] * ext[pl.ds(hr - 3 + k, tr), :]
        o_ref[...] = acc * _sigmoid(acc)
        xc_ref[...] = acc.astype(BF16)

    return _rowcall(body, "conv_fwd", t, tr, [_rows(proj, tr, CONV_DIM, cb), halo, _whole(conv_w), _whole(conv_b)],
                    [_orow(t, CONV_DIM, F32, tr), _orow(t, CONV_DIM, BF16, tr)], scratch=[pltpu.VMEM((tr + hr, CONV_DIM), F32)])


def _dt_fwd(proj, dt_bias_pad, alog_pad, tr=512):
    t = proj.shape[0]

    def body(raw_ref, b_ref, al_ref, dtx_ref, csx_ref, cst_ref):
        v = raw_ref[...] + b_ref[...]
        dt = jnp.maximum(v, 0.0) + jnp.log1p(jnp.exp(-jnp.abs(v)))
        expand = _head_expand(LANE, SSM_HEADS, SSM_HEAD_DIM)
        dtx_ref[...] = _dot_split(dt, expand, 0, 3)
        la = dt * (-jnp.exp(al_ref[...]))
        row = lax.broadcasted_iota(jnp.int32, (SSM_CHUNK, SSM_CHUNK), 0)
        col = lax.broadcasted_iota(jnp.int32, (SSM_CHUNK, SSM_CHUNK), 1)
        tril = (col <= row).astype(F32)
        cs = jnp.concatenate([_dot_split(tril, la[k * SSM_CHUNK:(k + 1) * SSM_CHUNK, :], 1, 3) for k in range(tr // SSM_CHUNK)], axis=0)
        csx_ref[...] = _dot_split(cs, expand, 0, 3)
        cst_ref[...] = cs.T

    return _rowcall(body, "dt_fwd", t, tr, [_rows(proj, tr, LANE, PB_DT // LANE), _whole(dt_bias_pad), _whole(alog_pad)],
                    [_orow(t, SSM_INNER, F32, tr), _orow(t, SSM_INNER, F32, tr), ((LANE, t), F32, (LANE, tr), lambda i: (0, i))])


def _gate_norm_fwd(y_ssd, xa, proj, dskip_x, norm_w, tr=256):
    t = y_ssd.shape[0]
    gw = SSM_INNER // SSM_GROUPS

    def body(y_ref, xs_ref, z_ref, d_ref, w_ref, o_ref):
        z = z_ref[...].astype(F32)
        y3 = (y_ref[...].astype(F32) + d_ref[...] * xs_ref[...]) * (z * _sigmoid(z))
        for g in range(SSM_GROUPS):
            sl = slice(g * gw, (g + 1) * gw)
            o_ref[:, sl] = _rms_fwd(y3[:, sl], w_ref[:, sl]).astype(BF16)

    return _rowcall(body, "gate_norm_fwd", t, tr,
                    [_rows(y_ssd, tr), _rows(xa, tr, SSM_INNER, 0), _rows(proj, tr, SSM_INNER, OFF_Z // SSM_INNER), _whole(dskip_x), _whole(norm_w)],
                    [_orow(t, SSM_INNER, BF16, tr)])[0]


def _gating_fwd(proj, b_gate, att_p, ssm_p, tr=512):
    t = proj.shape[0]

    def body(gl_ref, b_ref, a_ref, s_ref, o_ref):
        gates = _sigmoid(gl_ref[...].astype(F32) + b_ref[...])
        o_ref[...] = (gates[:, :D_MODEL] * a_ref[...].astype(F32) + gates[:, D_MODEL:] * s_ref[...].astype(F32)).astype(BF16)

    return _rowcall(body, "gating_fwd", t, tr, [_rows(proj, tr, 2 * D_MODEL, OFF_GL // (2 * D_MODEL)), _whole(b_gate), _rows(att_p, tr), _rows(ssm_p, tr)],
                    [_orow(t, D_MODEL, BF16, tr)])[0]


def _mix_post_ffn_pre(x, mixed, w_post, w_fpre, tr=512):
    t = x.shape[0]

    def body(x_ref, m_ref, wp_ref, wf_ref, h1_ref, f_ref):
        h1 = x_ref[...] + _rms_fwd(m_ref[...], wp_ref[...])
        h1_ref[...] = h1
        f_ref[...] = _rms_fwd(h1, wf_ref[...]).astype(BF16)

    return _rowcall(body, "mix_post_ffn_pre", t, tr, [_rows(x, tr), _rows(mixed, tr), _whole(w_post), _whole(w_fpre)],
                    [_orow(t, D_MODEL, F32, tr), _orow(t, D_MODEL, BF16, tr)])


def _loss_and_ffn_post_bwd(h1, dn, w_fpost, target, tr=512):
    t = h1.shape[0]

    def body(h1_ref, dn_ref, w_ref, tg_ref, loss_ref, gh2_ref, gdn_ref, gw_ref):
        dn = dn_ref[...]
        w = w_ref[...]
        err = h1_ref[...] + _rms_fwd(dn, w) - tg_ref[...]
        _accumulate(loss_ref, jnp.zeros((1, LANE), F32) + 0.5 * jnp.sum(jnp.mean(err * err, axis=-1, keepdims=True)))
        gh2 = err * (1.0 / D_MODEL)
        gh2_ref[...] = gh2
        gdn, gw = _rms_bwd(gh2, dn, w)
        gdn_ref[...] = gdn.astype(BF16)
        _accumulate(gw_ref, gw)

    return _rowcall(body, "loss_ffn_post_bwd", t, tr, [_rows(h1, tr), _rows(dn, tr), _whole(w_fpost), _rows(target, tr)],
                    [_oacc(LANE), _orow(t, D_MODEL, F32, tr), _orow(t, D_MODEL, BF16, tr), _oacc(D_MODEL)])


def _ffn_pre_mix_post_bwd(g_h2, g_f, h1, w_fpre, mixed, w_post, tr=512):
    t = h1.shape[0]

    def body(gh2_ref, gf_ref, h1_ref, wf_ref, m_ref, wp_ref, gh1_ref, gm_ref, gwf_ref, gwp_ref):
        gx, gwf = _rms_bwd(gf_ref[...], h1_ref[...], wf_ref[...])
        gh1 = gh2_ref[...] + gx
        gh1_ref[...] = gh1
        gm, gwp = _rms_bwd(gh1, m_ref[...], wp_ref[...])
        gm_ref[...] = gm.astype(BF16)
        _accumulate(gwf_ref, gwf)
        _accumulate(gwp_ref, gwp)

    return _rowcall(body, "ffn_pre_mix_post_bwd", t, tr,
                    [_rows(g_h2, tr), _rows(g_f, tr), _rows(h1, tr), _whole(w_fpre), _rows(mixed, tr), _whole(w_post)],
                    [_orow(t, D_MODEL, F32, tr), _orow(t, D_MODEL, BF16, tr), _oacc(D_MODEL), _oacc(D_MODEL)])


def _gating_bwd(g_mixin, proj, b_gate, att_p, ssm_p, tr=512):
    t = proj.shape[0]

    def body(gm_ref, gl_ref, b_ref, a_ref, s_ref, ga_ref, gs_ref, ggl_ref, gb_ref):
        gates = _sigmoid(gl_ref[...].astype(F32) + b_ref[...])
        gm = gm_ref[...].astype(F32)
        g_att, g_ssm = gates[:, :D_MODEL], gates[:, D_MODEL:]
        ga_ref[...] = (gm * g_att).astype(BF16)
        gs_ref[...] = (gm * g_ssm).astype(BF16)
        ggl_a = gm * a_ref[...].astype(F32) * g_att * (1.0 - g_att)
        ggl_s = gm * s_ref[...].astype(F32) * g_ssm * (1.0 - g_ssm)
        ggl_ref[:, :D_MODEL] = ggl_a.astype(BF16)
        ggl_ref[:, D_MODEL:] = ggl_s.astype(BF16)
        _accumulate(gb_ref.at[:, :D_MODEL], _colsum(ggl_a))
        _accumulate(gb_ref.at[:, D_MODEL:], _colsum(ggl_s))

    return _rowcall(body, "gating_bwd", t, tr,
                    [_rows(g_mixin, tr), _rows(proj, tr, 2 * D_MODEL, OFF_GL // (2 * D_MODEL)), _whole(b_gate), _rows(att_p, tr), _rows(ssm_p, tr)],
                    [_orow(t, D_MODEL, BF16, tr), _orow(t, D_MODEL, BF16, tr), _orow(t, 2 * D_MODEL, BF16, tr), _oacc(2 * D_MODEL)])


def _gate_norm_bwd(g_y4, y_ssd, xa, proj, dskip_x, norm_w, tr=256):
    t = y_ssd.shape[0]
    gw = SSM_INNER // SSM_GROUPS

    def body(g_ref, y_ref, xs_ref, z_ref, d_ref, w_ref, gy2_ref, gz_ref, gnw_ref, gdx_ref, gd_ref):
        z = z_ref[...].astype(F32)
        xs = xs_ref[...]
        sg = _sigmoid(z)
        sz = z * sg
        y2 = y_ref[...].astype(F32) + d_ref[...] * xs
        y3 = y2 * sz
        g4 = g_ref[...].astype(F32)
        for g in range(SSM_GROUPS):
            sl = slice(g * gw, (g + 1) * gw)
            gy3, gnw = _rms_bwd(g4[:, sl], y3[:, sl], w_ref[:, sl])
            _accumulate(gnw_ref.at[:, sl], gnw)
            gy2 = gy3 * sz[:, sl]
            gy2_ref[:, sl] = gy2
            gz_ref[:, sl] = (gy3 * y2[:, sl] * (sg[:, sl] * (1.0 + z[:, sl] * (1.0 - sg[:, sl])))).astype(BF16)
            _accumulate(gdx_ref.at[:, sl], _colsum(gy2 * xs[:, sl]))
        tot = jnp.broadcast_to(gdx_ref[...], (8, SSM_INNER))
        gd_ref[...] = jnp.dot(tot, _head_reduce(SSM_HEADS, SSM_HEAD_DIM, LANE), precision=HI, preferred_element_type=F32)[0:1, :]

    return _rowcall(body, "gate_norm_bwd", t, tr,
                    [_rows(g_y4, tr), _rows(y_ssd, tr), _rows(xa, tr, SSM_INNER, 0), _rows(proj, tr, SSM_INNER, OFF_Z // SSM_INNER), _whole(dskip_x), _whole(norm_w)],
                    [_orow(t, SSM_INNER, F32, tr), _orow(t, SSM_INNER, BF16, tr), _oacc(SSM_INNER), _oacc(SSM_INNER), _oacc(LANE)])


def _dt_bwd(g_dtx, ga_rows, proj, dt_bias_pad, tr=512):
    t = proj.shape[0]

    def body(g_ref, ga_ref, raw_ref, b_ref, o_ref, gb_ref, gal_ref):
        red = _head_reduce(SSM_HEADS, SSM_HEAD_DIM, LANE)
        gdt = _dot_split(g_ref[...], red, 0, 3)
        graw = gdt * _sigmoid(raw_ref[...] + b_ref[...])
        o_ref[...] = graw.astype(BF16)
        _accumulate(gb_ref, _colsum(graw))
        tot = jnp.broadcast_to(_colsum(ga_ref[...]), (8, SSM_INNER))
        gal_ref[...] = jnp.dot(tot, red, precision=HI, preferred_element_type=F32)[0:1, :]

    return _rowcall(body, "dt_bwd", t, tr, [_rows(g_dtx, tr), _whole(ga_rows), _rows(proj, tr, LANE, PB_DT // LANE), _whole(dt_bias_pad)],
                    [_orow(t, LANE, BF16, tr), _oacc(LANE), _oacc(LANE)])


def _conv_bwd(g_xs, g_b, g_c, xc, proj, conv_w, tr=256):
    t = proj.shape[0]
    n_blk = t // tr
    cb = OFF_XBC // CONV_DIM
    nb, nc = SSM_INNER, SSM_INNER + SSM_GROUPS * SSM_STATE
    halo_rows = 16

    def nxt(arr, width):
        return (arr, (halo_rows, width), lambda i: (jnp.minimum((i + 1) * (tr // halo_rows), t // halo_rows - 1), 0))

    def body(gxs_ref, gxs_n, gb_ref, gb_n, gc_ref, gc_n, xc_ref, xc_n, x_ref, w_ref, o_ref, gcb_ref, gw0, gw1, gw2, gw3, ext):
        def store_gxc(rows, n_rows, gxs, gb, gc, xc, keep):
            xcf = xc[...].astype(F32)
            sg = _sigmoid(xcf)
            dsilu = jnp.where(keep, sg * (1.0 + xcf * (1.0 - sg)), 0.0)
            ext[pl.ds(rows, n_rows), :nb] = gxs[...] * dsilu[:, :nb]
            ext[pl.ds(rows, n_rows), nb:nc] = gb[...] * dsilu[:, nb:nc]
            ext[pl.ds(rows, n_rows), nc:] = gc[...] * dsilu[:, nc:]

        store_gxc(0, tr, gxs_ref, gb_ref, gc_ref, xc_ref, True)
        store_gxc(tr, halo_rows, gxs_n, gb_n, gc_n, xc_n, pl.program_id(0) < n_blk - 1)
        x = x_ref[...].astype(F32)
        acc = None
        for k, gw in enumerate((gw0, gw1, gw2, gw3)):
            shifted = ext[pl.ds(3 - k, tr), :]
            term = w_ref[k:k + 1, :] * shifted
            acc = term if acc is None else acc + term
            _accumulate(gw, _colsum(shifted * x))
            if k == SSM_CONV - 1:
                _accumulate(gcb_ref, _colsum(shifted))
        o_ref[...] = acc.astype(BF16)

    ins = []
    for arr, width in ((g_xs, SSM_INNER), (g_b, nc - nb), (g_c, nc - nb), (xc, CONV_DIM)):
        ins += [_rows(arr, tr), nxt(arr, width)]
    ins += [_rows(proj, tr, CONV_DIM, cb), _whole(conv_w)]
    return _rowcall(body, "conv_bwd", t, tr, ins, [_orow(t, CONV_DIM, BF16, tr)] + [_oacc(CONV_DIM)] * 5,
                    scratch=[pltpu.VMEM((tr + halo_rows, CONV_DIM), F32)])


def _pre_norm_bwd(g_h1, g_u, x, w_pre, tr=512):
    t = x.shape[0]

    def body(gh_ref, gu_ref, x_ref, w_ref, gx_ref, gw_ref):
        gx, gw = _rms_bwd(gu_ref[...], x_ref[...], w_ref[...])
        gx_ref[...] = gh_ref[...] + gx
        _accumulate(gw_ref, gw)

    return _rowcall(body, "pre_norm_bwd", t, tr, [_rows(g_h1, tr), _rows(g_u, tr), _rows(x, tr), _whole(w_pre)],
                    [_orow(t, D_MODEL, F32, tr), _oacc(D_MODEL)])


def _alibi_slopes(n):
    def pow2(m):
        start = 2.0 ** (-8.0 / m)
        return [start ** (i + 1) for i in range(m)]
    if (n & (n - 1)) == 0:
        s = pow2(n)
    else:
        c = 2 ** int(math.floor(math.log2(n)))
        s = pow2(c) + pow2(2 * c)[0::2][: n - c]
    return np.array(s, dtype=np.float32)


def _slope_rows():
    s = _alibi_slopes(N_ATT_HEADS).reshape(N_ATT_HEADS // 2, 2)
    return jnp.asarray(np.broadcast_to(np.repeat(s, HEAD_DIM, axis=1)[:, None, :], (N_ATT_HEADS // 2, 8, LANE)).copy())


ATT_MAX_BLOCK_ROWS = 2048


RESIDUE_MAJOR_FROM = 16


class _AttLayout:
    def __init__(self, t, dil):
        self.t, self.dil = t, dil
        self.rows = t // dil
        self.residue_major = dil >= RESIDUE_MAJOR_FROM
        if self.residue_major:
            bq, self.stride = min(512, self.rows), 1
        else:
            bq, self.stride = min(512, self.rows, ATT_MAX_BLOCK_ROWS // dil), dil
        self.nsub = bq // ATT_BLOCK
        self.nblk = self.rows // bq
        self.rb = bq * self.stride
        self.pb = ATT_BLOCK * self.stride
        self.n_pb = self.rows * self.stride // self.pb
        self.out_dtype = F32 if self.stride > 1 else BF16

    def qkv(self, proj):
        if self.residue_major:
            qkv = proj[:, :3 * ATT_WIDTH]
            return qkv.reshape(self.rows, self.dil * 3 * ATT_WIDTH), 3 * ATT_WIDTH // LANE, 0
        return proj, 0, 0

    def act(self, a):
        return a.reshape(self.rows, self.dil * ATT_WIDTH) if self.residue_major else a

    def act_shape(self):
        return (self.rows, self.dil * ATT_WIDTH) if self.residue_major else (self.t, ATT_WIDTH)

    def col(self, r, band, c):
        return r * band + c if self.residue_major else c


def _residue_rows(r, stride, first_block, n_blocks=1):
    if stride == 1:
        return pl.ds(first_block * ATT_BLOCK, n_blocks * ATT_BLOCK)
    return pl.ds(r + first_block * ATT_BLOCK * stride, n_blocks * ATT_BLOCK, stride=stride)


def _lane_half():
    return lax.broadcasted_iota(jnp.int32, (ATT_BLOCK, LANE), 1) // HEAD_DIM


def _att_scores_mask(dil, first):
    iq = lax.broadcasted_iota(jnp.int32, (ATT_BLOCK, 2 * ATT_BLOCK), 0)
    jk = lax.broadcasted_iota(jnp.int32, (ATT_BLOCK, 2 * ATT_BLOCK), 1)
    dist = ATT_BLOCK + iq - jk
    valid = (dist >= 0) & (dist <= ATT_BLOCK) & (jnp.logical_not(first) | (jk >= ATT_BLOCK))
    return (dist * dil).astype(F32), valid


def _att_fwd(proj, dil, slopes):
    t = proj.shape[0]
    lay = _AttLayout(t, dil)
    nsub, nblk, rb, pb = lay.nsub, lay.nblk, lay.rb, lay.pb
    src, band, qb = lay.qkv(proj)
    aw = ATT_WIDTH // LANE

    def spec(off, prev=False):
        if prev:
            return pl.BlockSpec((pb, LANE), lambda hp, i, r: (jnp.maximum(i * nsub - 1, 0), lay.col(r, band, qb + off + hp)))
        return pl.BlockSpec((rb, LANE), lambda hp, i, r: (i, lay.col(r, band, qb + off + hp)))

    o_spec = pl.BlockSpec((rb, LANE), lambda hp, i, r: (i, lay.col(r, aw, hp)))

    def body(q_ref, kc_ref, kp_ref, vc_ref, vp_ref, sl_ref, o_ref, lse_ref):
        i, r = pl.program_id(1), pl.program_id(2)
        half = _lane_half()
        for sub in range(nsub):
            rs = _residue_rows(r, lay.stride, sub)
            q = (q_ref[rs, :] * (HEAD_DIM ** -0.5)).astype(BF16)
            if sub == 0:
                r0 = _residue_rows(r, lay.stride, 0)
                kk = jnp.concatenate([kp_ref[r0, :], kc_ref[rs, :]], axis=0).astype(BF16)
                vv = jnp.concatenate([vp_ref[r0, :], vc_ref[rs, :]], axis=0).astype(BF16)
                first = i == 0
            else:
                ks = _residue_rows(r, lay.stride, sub - 1, 2)
                kk, vv = kc_ref[ks, :].astype(BF16), vc_ref[ks, :].astype(BF16)
                first = jnp.bool_(False)
            dist, valid = _att_scores_mask(dil, first)
            outs, lses = [], []
            for e in range(2):
                qe = jnp.where(half == e, q, jnp.zeros_like(q))
                s = lax.dot_general(qe, kk, NT_DIMS, preferred_element_type=F32)
                s = s + jnp.where(valid, -sl_ref[0:1, e * HEAD_DIM:e * HEAD_DIM + 1] * dist, NEG)
                m = jnp.max(s, axis=-1, keepdims=True)
                p = jnp.exp(s - m)
                l = jnp.sum(p, axis=-1, keepdims=True)
                outs.append(jnp.dot(p.astype(BF16), vv, preferred_element_type=F32) / l)
                lses.append(m + jnp.log(l))
            o_ref[rs, :] = jnp.where(half == 0, outs[0], outs[1]).astype(lay.out_dtype)
            lse_ref[rs, :] = jnp.where(half == 0, lses[0], lses[1])

    o, lse = pl.pallas_call(
        body, grid=(N_ATT_HEADS // 2, nblk, dil),
        in_specs=[spec(0), spec(6), spec(6, True), spec(12), spec(12, True), pl.BlockSpec((None, 8, LANE), lambda hp, i, r: (hp, 0, 0))],
        out_specs=[o_spec, o_spec], out_shape=[S(lay.act_shape(), lay.out_dtype), S(lay.act_shape(), F32)],
        name=f"att_fwd_d{dil}", compiler_params=_params(("parallel", "parallel", "arbitrary")),
    )(src, src, src, src, src, slopes)
    return o.reshape(t, ATT_WIDTH), lse.reshape(t, ATT_WIDTH)


def _att_combine(outs, lses, tr=512):
    t = outs[0].shape[0]

    def body(o0, o1, o2, l0, l1, l2, att_ref, lse_ref):
        ls = [l0[...], l1[...], l2[...]]
        m = jnp.maximum(jnp.maximum(ls[0], ls[1]), ls[2])
        ws = [jnp.exp(l - m) for l in ls]
        tot = ws[0] + ws[1] + ws[2]
        num = ws[0] * o0[...].astype(F32) + ws[1] * o1[...].astype(F32) + ws[2] * o2[...].astype(F32)
        att_ref[...] = (num / tot).astype(BF16)
        lse_ref[...] = m + jnp.log(tot)

    return _rowcall(body, "att_combine", t, tr, [_rows(a, tr) for a in list(outs) + list(lses)],
                    [_orow(t, ATT_WIDTH, BF16, tr), _orow(t, ATT_WIDTH, F32, tr)])


def _att_delta(g_att, att, tr=512):
    t = att.shape[0]

    def body(g_ref, a_ref, o_ref):
        prod = g_ref[...] * a_ref[...].astype(F32)
        o_ref[...] = _dot_split(prod, _block_ones(ATT_WIDTH, HEAD_DIM), 0, 3)

    return _rowcall(body, "att_delta", t, tr, [_rows(g_att, tr), _rows(att, tr)], [_orow(t, ATT_WIDTH, F32, tr)])[0]


def _att_bwd(proj, g_att, lse, delta, dil, slopes):
    t = proj.shape[0]
    lay = _AttLayout(t, dil)
    nsub, nblk, rb, pb, n_pb = lay.nsub, lay.nblk, lay.rb, lay.pb, lay.n_pb
    src, band, qb = lay.qkv(proj)
    aw = ATT_WIDTH // LANE

    def near(i, which):
        return jnp.maximum(i * nsub - 1, 0) if which == "prev" else jnp.minimum((i + 1) * nsub, n_pb - 1)

    def pspec(off, which=None):
        if which:
            return pl.BlockSpec((pb, LANE), lambda hp, i, r: (near(i, which), lay.col(r, band, qb + off + hp)))
        return pl.BlockSpec((rb, LANE), lambda hp, i, r: (i, lay.col(r, band, qb + off + hp)))

    def aspec(which=None):
        if which:
            return pl.BlockSpec((pb, LANE), lambda hp, i, r: (near(i, which), lay.col(r, aw, hp)))
        return pl.BlockSpec((rb, LANE), lambda hp, i, r: (i, lay.col(r, aw, hp)))

    scale = HEAD_DIM ** -0.5

    def body(q_ref, qn_ref, kc_ref, kp_ref, vc_ref, vp_ref, do_ref, don_ref, lse_ref, lsen_ref, dl_ref, dln_ref, sl_ref,
             dq_ref, dk_ref, dv_ref):
        i, r = pl.program_id(1), pl.program_id(2)
        half = _lane_half()

        def tile_grads(q, do, lse_q, dl_q, kk, vv, dist, valid):
            dqs, dks, dvs = [], [], []
            for e in range(2):
                c = e * HEAD_DIM
                qe = jnp.where(half == e, q, jnp.zeros_like(q))
                doe = jnp.where(half == e, do, jnp.zeros_like(do))
                s = lax.dot_general(qe, kk, NT_DIMS, preferred_element_type=F32)
                s = s + jnp.where(valid, -sl_ref[0:1, c:c + 1] * dist, NEG)
                p = jnp.exp(s - lse_q[:, c:c + 1])
                dp = lax.dot_general(doe, vv, NT_DIMS, preferred_element_type=F32)
                ds16 = (p * (dp - dl_q[:, c:c + 1])).astype(BF16)
                dqs.append(jnp.dot(ds16, kk, preferred_element_type=F32))
                dks.append(lax.dot_general(ds16, q, TN_DIMS, preferred_element_type=F32))
                dvs.append(lax.dot_general(p.astype(BF16), do, TN_DIMS, preferred_element_type=F32))
            halfk = lax.broadcasted_iota(jnp.int32, dks[0].shape, 1) // HEAD_DIM
            return (jnp.where(half == 0, dqs[0], dqs[1]) * scale,
                    jnp.where(halfk == 0, dks[0], dks[1]), jnp.where(halfk == 0, dvs[0], dvs[1]))

        carry_k = carry_v = None
        for sub in range(nsub):
            rs = _residue_rows(r, lay.stride, sub)
            q = (q_ref[rs, :] * scale).astype(BF16)
            do = do_ref[rs, :].astype(BF16)
            if sub == 0:
                r0 = _residue_rows(r, lay.stride, 0)
                kk = jnp.concatenate([kp_ref[r0, :], kc_ref[rs, :]], axis=0).astype(BF16)
                vv = jnp.concatenate([vp_ref[r0, :], vc_ref[rs, :]], axis=0).astype(BF16)
                first = i == 0
            else:
                ks = _residue_rows(r, lay.stride, sub - 1, 2)
                kk, vv = kc_ref[ks, :].astype(BF16), vc_ref[ks, :].astype(BF16)
                first = jnp.bool_(False)
            dist, valid = _att_scores_mask(dil, first)
            dq, dk2, dv2 = tile_grads(q, do, lse_ref[rs, :], dl_ref[rs, :], kk, vv, dist, valid)
            dq_ref[rs, :] = dq.astype(lay.out_dtype)
            if sub > 0:
                rp = _residue_rows(r, lay.stride, sub - 1)
                dk_ref[rp, :] = (carry_k + dk2[:ATT_BLOCK, :]).astype(lay.out_dtype)
                dv_ref[rp, :] = (carry_v + dv2[:ATT_BLOCK, :]).astype(lay.out_dtype)
            carry_k, carry_v = dk2[ATT_BLOCK:, :], dv2[ATT_BLOCK:, :]
        rl = _residue_rows(r, lay.stride, nsub - 1)
        rn = _residue_rows(r, lay.stride, 0)
        iq = lax.broadcasted_iota(jnp.int32, (ATT_BLOCK, ATT_BLOCK), 0)
        jk = lax.broadcasted_iota(jnp.int32, (ATT_BLOCK, ATT_BLOCK), 1)
        dist_i = ATT_BLOCK + iq - jk
        valid = (dist_i >= 0) & (dist_i <= ATT_BLOCK) & (i < nblk - 1)
        qn = (qn_ref[rn, :] * scale).astype(BF16)
        _, dk1, dv1 = tile_grads(qn, don_ref[rn, :].astype(BF16), lsen_ref[rn, :], dln_ref[rn, :],
                                 kc_ref[rl, :].astype(BF16), vc_ref[rl, :].astype(BF16), (dist_i * dil).astype(F32), valid)
        dk_ref[rl, :] = (carry_k + dk1).astype(lay.out_dtype)
        dv_ref[rl, :] = (carry_v + dv1).astype(lay.out_dtype)

    gv, lv, dlv = lay.act(g_att), lay.act(lse), lay.act(delta)
    dq, dk, dv = pl.pallas_call(
        body, grid=(N_ATT_HEADS // 2, nblk, dil),
        in_specs=[pspec(0), pspec(0, "next"), pspec(6), pspec(6, "prev"), pspec(12), pspec(12, "prev"),
                  aspec(), aspec("next"), aspec(), aspec("next"), aspec(), aspec("next"),
                  pl.BlockSpec((None, 8, LANE), lambda hp, i, r: (hp, 0, 0))],
        out_specs=[aspec(), aspec(), aspec()], out_shape=[S(lay.act_shape(), lay.out_dtype)] * 3,
        name=f"att_bwd_d{dil}", compiler_params=_params(("parallel", "parallel", "arbitrary")),
    )(src, src, src, src, src, src, gv, gv, lv, lv, dlv, dlv, slopes)
    return dq.reshape(t, ATT_WIDTH), dk.reshape(t, ATT_WIDTH), dv.reshape(t, ATT_WIDTH)


def _att_grad_sum(dqs, dks, dvs, tr=512):
    t = dqs[0].shape[0]

    def body(*refs):
        o_ref = refs[-1]
        for n in range(3):
            tot = refs[3 * n][...].astype(F32) + refs[3 * n + 1][...].astype(F32) + refs[3 * n + 2][...].astype(F32)
            o_ref[:, n * ATT_WIDTH:(n + 1) * ATT_WIDTH] = tot.astype(BF16)

    return _rowcall(body, "att_grad_sum", t, tr, [_rows(a, tr) for a in list(dqs) + list(dks) + list(dvs)],
                    [_orow(t, 3 * ATT_WIDTH, BF16, tr)])[0]


def _ssd_common(xs, dtx, cs, cs_t):
    ch = SSM_CHUNK
    row = lax.broadcasted_iota(jnp.int32, (ch, ch), 0)
    col = lax.broadcasted_iota(jnp.int32, (ch, ch), 1)
    cs_last = cs[ch - 1:ch, :]
    return dict(tril=col <= row, row=row, col=col, cs=cs, cs_t=cs_t, cs_last=cs_last,
                e=jnp.exp(cs), w=jnp.exp(cs_last - cs), xd=xs * dtx)


def _dot_split(a, b, split, terms=2):
    ops = [a, b]
    rest = ops[split]
    other = ops[1 - split].astype(BF16)
    out = None
    for _ in range(terms):
        piece = rest.astype(BF16)
        rest = rest - piece.astype(F32)
        part = jnp.dot(other, piece, preferred_element_type=F32) if split == 1 else jnp.dot(piece, other, preferred_element_type=F32)
        out = part if out is None else out + part
    return out


def _decay_col(cs_t, heads_per_group):
    r = lax.broadcasted_iota(jnp.int32, (heads_per_group * SSM_HEAD_DIM, SSM_STATE), 0) // SSM_HEAD_DIM
    out = jnp.zeros((heads_per_group * SSM_HEAD_DIM, SSM_STATE), F32)
    for j in range(heads_per_group):
        out = jnp.where(r == j, jnp.exp(cs_t[j:j + 1, SSM_CHUNK - 1:SSM_CHUNK]), out)
    return out


SSD_GROUPS_PER_STEP = 2


def _ssd_specs(t):
    hg = SSM_HEADS // SSM_GROUPS
    gw = hg * SSM_HEAD_DIM
    nb0 = SSM_INNER // SSM_STATE
    return hg, gw, nb0


def _ssd_group_views(gi, gw, wide, narrow, stacked):
    w = [r.at[:, pl.ds(gi * gw, gw)] for r in wide]
    n = [r.at[:, pl.ds(gi * SSM_STATE, SSM_STATE)] for r in narrow]
    return w, n, [r.at[gi] for r in stacked]


def _ssd_fwd(xa, dtx, csx, cst_g):
    t = xa.shape[0]
    nch = t // SSM_CHUNK
    hg, gw, nb0 = _ssd_specs(t)
    ch = SSM_CHUNK
    gp = SSD_GROUPS_PER_STEP

    def body(xs_ref, b_ref, c_ref, dtx_ref, cs_ref, cst_ref, y_ref, st_ref, h_scr):
        for gi in range(gp):
            (xs_g, dtx_g, cs_g, y_g), (b_g, c_g), (cst_gi, st_g) = _ssd_group_views(
                gi, gw, (xs_ref, dtx_ref, cs_ref, y_ref), (b_ref, c_ref), (cst_ref, st_ref))
            group_body(pl.program_id(0), pl.program_id(1) * gp + gi, xs_g, b_g, c_g, dtx_g, cs_g, cst_gi, y_g, st_g, h_scr)

    def group_body(cc, g, xs_ref, b_ref, c_ref, dtx_ref, cs_ref, cst_ref, y_ref, st_ref, h_scr):
        @pl.when(cc == 0)
        def _():
            h_scr[g] = jnp.zeros((gw, SSM_STATE), F32)

        q = _ssd_common(xs_ref[...], dtx_ref[...], cs_ref[...], cst_ref[...])
        bb, cb = b_ref[...].astype(BF16), c_ref[...].astype(BF16)
        cbm = lax.dot_general(cb, bb, NT_DIMS, preferred_element_type=F32)
        h = h_scr[g]
        st_ref[...] = h
        xd16 = q["xd"].astype(BF16)
        y = lax.dot_general(cb, h.astype(BF16), NT_DIMS, preferred_element_type=F32) * q["e"]
        lane_head = lax.broadcasted_iota(jnp.int32, (ch, gw), 1) // SSM_HEAD_DIM
        for j in range(hg):
            diff = q["cs"][:, j * SSM_HEAD_DIM:j * SSM_HEAD_DIM + 1] - q["cs_t"][j:j + 1, :]
            gmat = cbm * jnp.exp(jnp.where(q["tril"], diff, NEG))
            yj = jnp.dot(gmat.astype(BF16), xd16, preferred_element_type=F32)
            y = y + jnp.where(lane_head == j, yj, 0.0)
        y_ref[...] = y.astype(BF16)
        s_new = lax.dot_general((q["xd"] * q["w"]).astype(BF16), bb, TN_DIMS, preferred_element_type=F32)
        h_scr[g] = _decay_col(q["cs_t"], hg) * h + s_new

    wide = pl.BlockSpec((ch, gp * gw), lambda cc, g: (cc, g))
    return pl.pallas_call(
        body, grid=(nch, SSM_GROUPS // gp),
        in_specs=[wide,
                  pl.BlockSpec((ch, gp * SSM_STATE), lambda cc, g: (cc, nb0 // gp + g)),
                  pl.BlockSpec((ch, gp * SSM_STATE), lambda cc, g: (cc, (nb0 + SSM_GROUPS) // gp + g)),
                  wide, wide,
                  pl.BlockSpec((gp, 8, ch), lambda cc, g: (g, 0, cc))],
        out_specs=[wide, pl.BlockSpec((None, gp, gw, SSM_STATE), lambda cc, g: (cc, g, 0, 0))],
        out_shape=[S((t, SSM_INNER), BF16), S((nch, SSM_GROUPS, gw, SSM_STATE), F32)],
        scratch_shapes=[pltpu.VMEM((SSM_GROUPS, gw, SSM_STATE), F32)],
        name="ssd_fwd", compiler_params=_params(("arbitrary", "arbitrary")),
    )(xa, xa, xa, dtx, csx, cst_g)


def _ssd_bwd(xa, dtx, csx, cst_g, alog_x, g_y, states, dskip_x):
    t = xa.shape[0]
    nch = t // SSM_CHUNK
    hg, gw, nb0 = _ssd_specs(t)
    ch = SSM_CHUNK
    gp = SSD_GROUPS_PER_STEP

    def rc(cc):
        return nch - 1 - cc

    def body(xs_ref, b_ref, c_ref, dtx_ref, cs_ref, cst_ref, alx_ref, gy_ref, st_ref, dsk_ref,
             gxs_ref, gb_ref, gc_ref, gdt_ref, ga_ref, gh_scr):
        for gi in range(gp):
            wide, narrow, stacked = _ssd_group_views(
                gi, gw, (xs_ref, dtx_ref, cs_ref, alx_ref, gy_ref, dsk_ref, gxs_ref, gdt_ref, ga_ref), (b_ref, c_ref, gb_ref, gc_ref),
                (cst_ref, st_ref))
            xs_g, dtx_g, cs_g, alx_g, gy_g, dsk_g, gxs_g, gdt_g, ga_g = wide
            b_g, c_g, gb_g, gc_g = narrow
            group_body(pl.program_id(0), pl.program_id(1) * gp + gi, xs_g, b_g, c_g, dtx_g, cs_g, stacked[0], alx_g, gy_g, stacked[1],
                       dsk_g, gxs_g, gb_g, gc_g, gdt_g, ga_g, gh_scr)

    def group_body(cc, g, xs_ref, b_ref, c_ref, dtx_ref, cs_ref, cst_ref, alx_ref, gy_ref, st_ref, dsk_ref,
                   gxs_ref, gb_ref, gc_ref, gdt_ref, ga_ref, gh_scr):
        @pl.when(cc == 0)
        def _():
            gh_scr[g] = jnp.zeros((gw, SSM_STATE), F32)

        xs, dtx = xs_ref[...], dtx_ref[...]
        q = _ssd_common(xs, dtx, cs_ref[...], cst_ref[...])
        cs, cs_t, e, w, xd = q["cs"], q["cs_t"], q["e"], q["w"], q["xd"]
        bb, cb = b_ref[...].astype(BF16), c_ref[...].astype(BF16)
        gy = gy_ref[...]
        gy16, xd16 = gy.astype(BF16), xd.astype(BF16)
        h = st_ref[...]
        h16 = h.astype(BF16)
        ghn = gh_scr[g]
        ghn16 = ghn.astype(BF16)
        seg = _block_ones(gw, SSM_HEAD_DIM)
        cbm = lax.dot_general(cb, bb, NT_DIMS, preferred_element_type=F32)

        gye16 = (gy * e).astype(BF16)
        chm = lax.dot_general(cb, h16, NT_DIMS, preferred_element_type=F32)
        g_c = jnp.dot(gye16, h16, preferred_element_type=F32)
        gh_off = lax.dot_general(gye16, cb, TN_DIMS, preferred_element_type=F32)
        g_e = _dot_split(gy * chm, seg, 0)
        bgs = lax.dot_general(bb, ghn16, NT_DIMS, preferred_element_type=F32)
        g_xd = w * bgs
        g_w = _dot_split(xd * bgs, seg, 0)
        g_b = jnp.dot((xd * w).astype(BF16), ghn16, preferred_element_type=F32)
        decay = _decay_col(cs_t, hg)
        gh_scr[g] = decay * ghn + gh_off
        rsum = jnp.sum(ghn * h, axis=1, keepdims=True)
        lane_head = lax.broadcasted_iota(jnp.int32, (ch, gw), 1) // SSM_HEAD_DIM
        lane_head1 = lax.broadcasted_iota(jnp.int32, (1, gw), 1) // SSM_HEAD_DIM
        g_el = jnp.zeros((1, gw), F32)
        g_cs = g_e * e - g_w * w
        upper = q["row"] <= q["col"]
        for j in range(hg):
            g_el = jnp.where(lane_head1 == j, jnp.sum(rsum[j * SSM_HEAD_DIM:(j + 1) * SSM_HEAD_DIM, :], axis=0, keepdims=True), g_el)
            csc = cs[:, j * SSM_HEAD_DIM:j * SSM_HEAD_DIM + 1]
            csr = cs_t[j:j + 1, :]
            lm = jnp.exp(jnp.where(q["tril"], csc - csr, NEG))
            gyj = jnp.where(lane_head == j, gy16, jnp.zeros_like(gy16))
            gg = lax.dot_general(gyj, xd16, NT_DIMS, preferred_element_type=F32)
            gcb = gg * lm
            gcb16 = gcb.astype(BF16)
            g_c = g_c + jnp.dot(gcb16, bb, preferred_element_type=F32)
            g_b = g_b + lax.dot_general(gcb16, cb, TN_DIMS, preferred_element_type=F32)
            gxdj = lax.dot_general((cbm * lm).astype(BF16), gy16, TN_DIMS, preferred_element_type=F32)
            g_xd = g_xd + jnp.where(lane_head == j, gxdj, 0.0)
            m_ls = gcb * cbm
            d_cs = jnp.sum(m_ls, axis=1, keepdims=True) - jnp.sum(m_ls.T, axis=1, keepdims=True)
            g_cs = g_cs + jnp.where(lane_head == j, d_cs, 0.0)
        extra = _colsum(g_w * w) + g_el * jnp.exp(q["cs_last"])
        g_cs = g_cs + jnp.where(lax.broadcasted_iota(jnp.int32, (ch, gw), 0) == ch - 1, extra, 0.0)
        g_la = _dot_split(upper, g_cs, 1)
        a_x = -jnp.exp(alx_ref[...])
        gdt_ref[...] = g_xd * xs + g_la * a_x * (1.0 / SSM_HEAD_DIM)
        ga_row = _colsum(g_la * (dtx * a_x)) * (1.0 / SSM_HEAD_DIM)
        ga_ref[...] = jnp.where(lax.broadcasted_iota(jnp.int32, (8, gw), 0) == 0, ga_row, 0.0)
        gxs_ref[...] = g_xd * dtx + gy * dsk_ref[...]
        gb_ref[...] = g_b
        gc_ref[...] = g_c

    wide = pl.BlockSpec((ch, gp * gw), lambda cc, g: (rc(cc), g))
    narrow = pl.BlockSpec((ch, gp * SSM_STATE), lambda cc, g: (rc(cc), g))
    row = pl.BlockSpec((1, gp * gw), lambda cc, g: (0, g))
    return pl.pallas_call(
        body, grid=(nch, SSM_GROUPS // gp),
        in_specs=[wide,
                  pl.BlockSpec((ch, gp * SSM_STATE), lambda cc, g: (rc(cc), nb0 // gp + g)),
                  pl.BlockSpec((ch, gp * SSM_STATE), lambda cc, g: (rc(cc), (nb0 + SSM_GROUPS) // gp + g)),
                  wide, wide,
                  pl.BlockSpec((gp, 8, ch), lambda cc, g: (g, 0, rc(cc))),
                  row, wide,
                  pl.BlockSpec((None, gp, gw, SSM_STATE), lambda cc, g: (rc(cc), g, 0, 0)),
                  row],
        out_specs=[wide, narrow, narrow, wide, pl.BlockSpec((8, gp * gw), lambda cc, g: (rc(cc), g))],
        out_shape=[S((t, SSM_INNER), F32), S((t, SSM_GROUPS * SSM_STATE), F32), S((t, SSM_GROUPS * SSM_STATE), F32),
                   S((t, SSM_INNER), F32), S((nch * 8, SSM_INNER), F32)],
        scratch_shapes=[pltpu.VMEM((SSM_GROUPS, gw, SSM_STATE), F32)],
        name="ssd_bwd", compiler_params=_params(("arbitrary", "arbitrary")),
    )(xa, xa, xa, dtx, csx, cst_g, alog_x, g_y, states, dskip_x)


def _local_step(x, target, w_pre, w_in_r, b_gate, conv_w, conv_b, dt_bias, a_log, d_skip, ssm_norm_w,
                late_weights, w_post, w_fpre, w_fpost, on_mid_grads, on_in_proj_grads):
    t = x.shape[0]
    mm = functools.partial(_matmul, tm=512)
    slopes = _slope_rows()
    hg = SSM_HEADS // SSM_GROUPS
    dt_bias_pad = jnp.pad(dt_bias, ((0, 0), (0, LANE - SSM_HEADS)))
    alog_x = jnp.repeat(a_log, SSM_HEAD_DIM, axis=1)
    alog_pad = jnp.pad(a_log, ((0, 0), (0, LANE - SSM_HEADS)))
    dskip_x = jnp.repeat(d_skip, SSM_HEAD_DIM, axis=1)

    u = _pre_norm(x, w_pre)
    pa = mm(u, w_in_r, mode="nn", out_dtype=BF16, name="in_proj_zgx", tn=2048, tk=D_MODEL, b_cols=(0, PA_W))
    pb = mm(u, w_in_r[:, PA_W:], mode="nn", out_dtype=F32, name="in_proj_qkvdt", tn=PB_W // 2, tk=D_MODEL)
    fwd = [_att_fwd(pb, dil, slopes) for _, dil in DILATED_PATTERNS]
    att, lse = _att_combine([o for o, _ in fwd], [l for _, l in fwd])
    xa, xc = _conv_fwd(pa, conv_w, conv_b)
    dtx, csx, cst = _dt_fwd(pb, dt_bias_pad, alog_pad)
    cst_g = jnp.pad(cst[:SSM_HEADS].reshape(SSM_GROUPS, hg, t), ((0, 0), (0, 8 - hg), (0, 0)))
    y_ssd, states = _ssd_fwd(xa, dtx, csx, cst_g)
    y4 = _gate_norm_fwd(y_ssd, xa, pa, dskip_x, ssm_norm_w)
    w_att, w_ssm, w_out, w_up, w_down = late_weights(y4)
    att_p = mm(att, w_att, mode="nn", out_dtype=BF16, name="att_proj", tn=D_MODEL, tk=ATT_WIDTH)
    ssm_p = mm(y4, w_ssm, mode="nn", out_dtype=BF16, name="ssm_proj", tn=D_MODEL, tk=SSM_INNER)
    mixin = _gating_fwd(pa, b_gate, att_p, ssm_p)
    mixed = mm(mixin, w_out, mode="nn", out_dtype=F32, name="out_proj", tn=D_MODEL, tk=D_MODEL)
    h1, f = _mix_post_ffn_pre(x, mixed, w_post, w_fpre)
    act, up = _matmul(f, w_up, mode="nn", out_dtype=BF16, name="ffn_up", tm=2048, tn=FFN_HIDDEN // N_DEV, tk=D_MODEL, epilogue="relu2", stacked=True)
    dn = mm(act, w_down, mode="nn", out_dtype=F32, name="ffn_down", tn=D_MODEL, tk=FFN_HIDDEN)
    loss, g_h2, g_dn, gw_fpost = _loss_and_ffn_post_bwd(h1, dn, w_fpost, target)

    g_up = mm(g_dn, w_down, mode="nt", out_dtype=BF16, name="ffn_down_bwd_x", tn=2048, tk=D_MODEL, epilogue="relu2_bwd", extra=up)
    gw_down = _matmul(act, g_dn, mode="tn", out_dtype=BF16, name="ffn_down_bwd_w", tm=1024, tn=D_MODEL, tk=2048)
    w_up_rows = jnp.moveaxis(w_up, 0, 1).reshape(D_MODEL, FFN_HIDDEN)
    g_f = mm(g_up, w_up_rows, mode="nt", out_dtype=F32, name="ffn_up_bwd_x", tn=D_MODEL, tk=FFN_HIDDEN)
    gw_up = _matmul(f, g_up, mode="tn", out_dtype=BF16, name="ffn_up_bwd_w", tm=D_MODEL, tn=FFN_HIDDEN // N_DEV, tk=2048, stacked=True)
    g_h1, g_mixed, gw_fpre, gw_post = _ffn_pre_mix_post_bwd(g_h2, g_f, h1, w_fpre, mixed, w_post)
    g_mixin = mm(g_mixed, w_out, mode="nt", out_dtype=BF16, name="out_proj_bwd_x", tn=D_MODEL, tk=D_MODEL)
    gw_out = _matmul(mixin, g_mixed, mode="tn", out_dtype=BF16, name="out_proj_bwd_w", tm=D_MODEL, tn=D_MODEL, tk=2048)
    g_att_p, g_ssm_p, g_gl, g_b_gate = _gating_bwd(g_mixin, pa, b_gate, att_p, ssm_p)
    g_att = mm(g_att_p, w_att, mode="nt", out_dtype=F32, name="att_proj_bwd_x", tn=ATT_WIDTH, tk=D_MODEL)
    gw_att = _matmul(att, g_att_p, mode="tn", out_dtype=BF16, name="att_proj_bwd_w", tm=ATT_WIDTH, tn=D_MODEL, tk=2048)
    g_y4 = mm(g_ssm_p, w_ssm, mode="nt", out_dtype=BF16, name="ssm_proj_bwd_x", tn=SSM_INNER, tk=D_MODEL)
    gw_ssm = _matmul(y4, g_ssm_p, mode="tn", out_dtype=BF16, name="ssm_proj_bwd_w", tm=1024, tn=D_MODEL, tk=2048)
    token = on_mid_grads(dict(w_att_proj=gw_att, w_ssm_proj=gw_ssm, w_out=gw_out, w_up=gw_up, w_down=gw_down))
    if token is not None:
        ssm_norm_w = ssm_norm_w + jnp.tile(token[0:1, :], (1, SSM_INNER // LANE))
    g_y2, g_z, g_norm_w, _, g_d_skip = _gate_norm_bwd(g_y4, y_ssd, xa, pa, dskip_x, ssm_norm_w)
    g_xs, g_bm, g_cm, g_dtx, ga_rows = _ssd_bwd(xa, dtx, csx, cst_g, alog_x, g_y2, states, dskip_x)
    g_dt_raw, g_dt_bias, g_a_log = _dt_bwd(g_dtx, ga_rows, pb, dt_bias_pad)
    g_xbc, g_conv_b, gcw0, gcw1, gcw2, gcw3 = _conv_bwd(g_xs, g_bm, g_cm, xc, pa, conv_w)
    delta = _att_delta(g_att, att)
    dqs, dks, dvs = [], [], []
    for _, dil in DILATED_PATTERNS:
        dq, dk, dv = _att_bwd(pb, g_att, lse, delta, dil, slopes)
        dqs.append(dq)
        dks.append(dk)
        dvs.append(dv)
    g_qkv = _att_grad_sum(dqs, dks, dvs)
    g_proj = jnp.concatenate([g_z, g_gl, g_xbc, g_qkv, g_dt_raw, jnp.zeros((t, PROJ_W - OFF_DT - LANE), BF16)], axis=1)
    gw_in_r = _matmul(u, g_proj, mode="tn", out_dtype=BF16, name="in_proj_bwd_w", tm=D_MODEL, tn=1792, tk=2048)
    token = on_in_proj_grads(gw_in_r, jnp.concatenate([gcw0, gcw1, gcw2, gcw3], axis=0))
    g_u = _matmul(g_proj, w_in_r, mode="nt", out_dtype=F32, name="in_proj_bwd_x", tm=1024, tn=D_MODEL, tk=3584, after=token)
    g_x, gw_pre = _pre_norm_bwd(g_h1, g_u, x, w_pre)

    grads = dict(
        norm_mix_pre_w=gw_pre, b_gate=g_b_gate, conv_b=g_conv_b, dt_bias=g_dt_bias[:, :SSM_HEADS], a_log=g_a_log[:, :SSM_HEADS],
        d_skip=g_d_skip[:, :SSM_HEADS], ssm_norm_w=g_norm_w, norm_mix_post_w=gw_post, norm_ffn_pre_w=gw_fpre, norm_ffn_post_w=gw_fpost)
    return loss, g_x, grads


def _mesh_pos():
    return lax.axis_index("x"), lax.axis_index("y"), lax.axis_index("c")


def _all_gather(shards):
    n = len(shards)

    def body(*refs):
        x_refs, o_refs = refs[:n], refs[n:2 * n]
        send_sems, recv_sems, local_sems = refs[2 * n:]
        x, y, c = _mesh_pos()
        me, sibling = (x, y, c), (x, y, 1 - c)
        chips = [(1 - x, y), (x, 1 - y), (1 - x, 1 - y)]

        def copy(a, k, block, to, src=None):
            dst = o_refs[a].at[4 * block[0] + 2 * block[1] + block[2]]
            return pltpu.make_async_remote_copy(
                src_ref=dst if src is None else src, dst_ref=dst, send_sem=send_sems.at[7 * a + k], recv_sem=recv_sems.at[7 * a + k],
                device_id=to, device_id_type=pl.DeviceIdType.MESH)

        mine = [pltpu.make_async_copy(x_refs[a], o_refs[a].at[4 * x + 2 * y + c], local_sems.at[a]) for a in range(n)]
        for cp in mine:
            cp.start()
        first = []
        for a in range(n):
            first.append(copy(a, 0, me, sibling, src=x_refs[a]))
            first += [copy(a, 1 + j, me, (*chip, c), src=x_refs[a]) for j, chip in enumerate(chips)]
        for cp in first:
            cp.start()
        passed = []
        for j, chip in enumerate(chips):
            for a in range(n):
                copy(a, 1 + j, (*chip, c), me).wait_recv()
                passed.append(copy(a, 4 + j, (*chip, c), sibling))
                passed[-1].start()
        for a in range(n):
            copy(a, 0, sibling, me).wait_recv()
            for j, chip in enumerate(chips):
                copy(a, 4 + j, (*chip, 1 - c), me).wait_recv()
        for cp in first + passed:
            cp.wait_send()
        for cp in mine:
            cp.wait()

    hbm = pl.BlockSpec(memory_space=pltpu.HBM)
    return pl.pallas_call(
        body, out_shape=[S((N_DEV,) + s.shape, s.dtype) for s in shards],
        in_specs=[hbm] * n, out_specs=[hbm] * n,
        scratch_shapes=[pltpu.SemaphoreType.DMA((7 * n,)), pltpu.SemaphoreType.DMA((7 * n,)), pltpu.SemaphoreType.DMA((n,))],
        name="weights_all_gather",
    )(*shards)


def _exchange_grads(slab_arrays, small):
    n = len(slab_arrays)
    r_small = small.shape[0]

    def body(*refs):
        slab_refs, small_ref = refs[:n], refs[n]
        recv_refs, gsm_ref = refs[n + 1:2 * n + 1], refs[2 * n + 1]
        send_sems, recv_sems, local_sems = refs[2 * n + 2:]
        x, y, c = _mesh_pos()
        me = 4 * x + 2 * y + c

        def peer(k):
            px = 1 - x if k & 4 else x
            py = 1 - y if k & 2 else y
            pc = 1 - c if k & 1 else c
            return (px, py, pc), 4 * px + 2 * py + pc

        def copy(a, k, sending):
            to, lin = peer(k)
            sem = 7 * a + k - 1
            if a == n:
                src, dst = small_ref, gsm_ref.at[me if sending else lin]
            else:
                src, dst = slab_refs[a].at[lin], recv_refs[a].at[me if sending else lin]
            return pltpu.make_async_remote_copy(src_ref=src, dst_ref=dst, send_sem=send_sems.at[sem], recv_sem=recv_sems.at[sem],
                                                device_id=to, device_id_type=pl.DeviceIdType.MESH)

        own = [pltpu.make_async_copy(slab_refs[a].at[me], recv_refs[a].at[me], local_sems.at[a]) for a in range(n)]
        own.append(pltpu.make_async_copy(small_ref, gsm_ref.at[me], local_sems.at[n]))
        for cp in own:
            cp.start()
        order = [n] + list(range(n))
        sends = [copy(a, k, True) for a in order for k in range(1, N_DEV)]
        for cp in sends:
            cp.start()
        for a in order:
            for k in range(1, N_DEV):
                copy(a, k, False).wait_recv()
        for cp in sends:
            cp.wait_send()
        for cp in own:
            cp.wait()

    hbm = pl.BlockSpec(memory_space=pltpu.HBM)
    n_sem = 7 * (n + 1)
    res = pl.pallas_call(
        body, out_shape=[S(a.shape, a.dtype) for a in slab_arrays] + [S((N_DEV, r_small, LANE), small.dtype)],
        in_specs=[hbm] * (n + 1), out_specs=[hbm] * (n + 1),
        scratch_shapes=[pltpu.SemaphoreType.DMA((n_sem,)), pltpu.SemaphoreType.DMA((n_sem,)), pltpu.SemaphoreType.DMA((n + 1,))],
        name="grad_exchange",
    )(*slab_arrays, small)
    return res[:n], res[n]


def _peer_of(k, x, y, c):
    px = 1 - x if k & 4 else x
    py = 1 - y if k & 2 else y
    pc = 1 - c if k & 1 else c
    return (px, py, pc), 4 * px + 2 * py + pc


def _split_copies(src_refs, land_refs, send_sems, recv_sems, per_peer):
    x, y, c = _mesh_pos()
    me = 4 * x + 2 * y + c
    sends, recvs = [], []
    for a, (src, land) in enumerate(zip(src_refs, land_refs)):
        for k in range(1, N_DEV):
            to, lin = _peer_of(k, x, y, c)
            sem = 7 * a + k - 1
            piece = src.at[lin] if per_peer else src
            for slot, out in ((me, sends), (lin, recvs)):
                out.append(pltpu.make_async_remote_copy(
                    src_ref=piece, dst_ref=land.at[slot], send_sem=send_sems.at[sem], recv_sem=recv_sems.at[sem],
                    device_id=to, device_id_type=pl.DeviceIdType.MESH))
    return sends, recvs


def _remote_start(srcs, per_peer, name):
    n = len(srcs)
    lands = [lax.empty((N_DEV,) + (s.shape[1:] if per_peer else s.shape), s.dtype) for s in srcs]

    def body(*refs):
        src_refs, land_refs = refs[:n], refs[n:2 * n]
        send_sems, recv_sems = refs[2 * n], refs[2 * n + 1]
        token = refs[-1]
        sends, _ = _split_copies(src_refs, land_refs, send_sems, recv_sems, per_peer)
        for cp in sends:
            cp.start()
        token[...] = jnp.zeros_like(token)

    hbm = pl.BlockSpec(memory_space=pltpu.HBM)
    sem = pl.BlockSpec(memory_space=pltpu.SEMAPHORE)
    res = pl.pallas_call(
        body, name=name,
        out_shape=(pltpu.SemaphoreType.DMA((7 * n,)), pltpu.SemaphoreType.DMA((7 * n,)),
                   *[pltpu.HBM(a.shape, a.dtype) for a in srcs + lands], S((8, LANE), F32)),
        in_specs=[hbm] * (2 * n), out_specs=(sem, sem, *[hbm] * (2 * n), pl.BlockSpec(memory_space=pltpu.VMEM)),
        input_output_aliases={i: 2 + i for i in range(2 * n)},
        compiler_params=pltpu.CompilerParams(has_side_effects=pltpu.SideEffectType.DATAFLOW_SIDE_EFFECTING),
    )(*[pltpu.with_memory_space_constraint(a, pltpu.HBM) for a in srcs + lands])
    return dict(sems=res[:2], srcs=list(res[2:2 + n]), lands=list(res[2 + n:2 + 2 * n]), per_peer=per_peer), res[-1]


def _remote_wait(handle, after, name):
    n = len(handle["srcs"])
    per_peer = handle["per_peer"]

    def body(*refs):
        src_refs, land_refs = refs[:n], refs[n:2 * n]
        send_sems, recv_sems = refs[2 * n], refs[2 * n + 1]
        sends, recvs = _split_copies(src_refs, land_refs, send_sems, recv_sems, per_peer)
        for cp in sends:
            cp.wait_send()
        for cp in recvs:
            cp.wait_recv()

    hbm = pl.BlockSpec(memory_space=pltpu.HBM)
    sem = pl.BlockSpec(memory_space=pltpu.SEMAPHORE)
    arrays = handle["srcs"] + handle["lands"]
    res = pl.pallas_call(
        body, name=name, out_shape=tuple(pltpu.HBM(a.shape, a.dtype) for a in arrays),
        in_specs=[hbm] * (2 * n) + [sem, sem, pl.BlockSpec(memory_space=pl.ANY)], out_specs=tuple([hbm] * (2 * n)),
        input_output_aliases={i: i for i in range(2 * n)},
        compiler_params=pltpu.CompilerParams(has_side_effects=pltpu.SideEffectType.DATAFLOW_SIDE_EFFECTING),
    )(*arrays, *handle["sems"], after)
    return list(res[n:])


def _with_own(lands, own, me):
    return [lax.dynamic_update_index_in_dim(land, o.astype(land.dtype), me, 0) for land, o in zip(lands, own)]


def _adamw(w, m, v, slabs, name, tr):
    r, cols = w.shape
    c1 = 1.0 - ADAM_B1 ** ADAM_STEP
    c2 = 1.0 - ADAM_B2 ** ADAM_STEP

    def body(w_ref, m_ref, v_ref, s_ref, g_ref, d_ref, nm_ref, nv_ref):
        g = s_ref[0].astype(F32)
        for d in range(1, N_DEV):
            g = g + s_ref[d].astype(F32)
        nm = ADAM_B1 * m_ref[...] + (1.0 - ADAM_B1) * g
        nv = ADAM_B2 * v_ref[...] + (1.0 - ADAM_B2) * (g * g)
        g_ref[...] = g
        nm_ref[...] = nm
        nv_ref[...] = nv
        d_ref[...] = -ADAM_LR * ((nm / c1) / (jnp.sqrt(nv / c2) + ADAM_EPS) + ADAM_WD * w_ref[...])

    assert r % tr == 0, name
    blk = pl.BlockSpec((tr, cols), lambda i: (i, 0))
    return pl.pallas_call(
        body, grid=(r // tr,), in_specs=[blk, blk, blk, pl.BlockSpec((N_DEV, tr, cols), lambda i: (0, i, 0))],
        out_specs=[blk] * 4, out_shape=[S((r, cols), F32)] * 4, name=name, compiler_params=_params(("parallel",)),
    )(w, m, v, slabs)


BIG = ("w_in", "w_att_proj", "w_up", "w_ssm_proj", "w_out", "w_down", "conv_w")
ADAMW_ROWS = dict(w_in=256, w_att_proj=768, w_up=512, w_ssm_proj=256, w_out=128, w_down=256, conv_w=4)
SMALL = ("norm_mix_pre_w", "b_gate", "conv_b", "dt_bias", "a_log", "d_skip", "ssm_norm_w", "norm_mix_post_w",
         "norm_ffn_pre_w", "norm_ffn_post_w")
ORDER = ("norm_mix_pre_w", "w_in", "b_gate", "conv_w", "conv_b", "dt_bias", "a_log", "d_skip", "ssm_norm_w", "w_att_proj",
         "w_ssm_proj", "w_out", "norm_mix_post_w", "norm_ffn_pre_w", "w_up", "w_down", "norm_ffn_post_w")
ROW_SHARDED = ("w_ssm_proj", "w_out", "w_down")
LATE = ("w_att_proj", "w_ssm_proj", "w_out", "w_up", "w_down")
IN_PROJ_W = 10528
IN_SHARD_W = IN_PROJ_W // N_DEV
IN_SEGMENTS = ((2304, 4352), (8480, 10528), (4352, 8448), (0, 2304), (8448, 8480))


def _pack(parts, rows_multiple):
    flat = jnp.concatenate([p.reshape(-1) for p in parts])
    pad = (-flat.shape[0]) % (rows_multiple * LANE)
    return jnp.pad(flat, (0, pad)).reshape(-1, LANE)


def _unpack(flat2d, shapes):
    flat, out, off = flat2d.reshape(-1), [], 0
    for sh in shapes:
        n = int(np.prod(sh))
        out.append(flat[off:off + n].reshape(sh))
        off += n
    return out


def _reorder_in_proj(w):
    qkv, z, xbc = w[:, :2304], w[:, 2304:4352], w[:, 4352:8448]
    dt, gate = w[:, 8448:8480], w[:, 8480:10528]
    return jnp.concatenate([z, gate, xbc, qkv, dt, jnp.zeros((w.shape[0], PROJ_W - 10528), w.dtype)], axis=1)


def _restore_in_proj(wr):
    return jnp.concatenate([wr[:, OFF_QKV:OFF_QKV + 2304], wr[:, OFF_Z:OFF_Z + 2048], wr[:, OFF_XBC:OFF_XBC + 4096],
                            wr[:, OFF_DT:OFF_DT + 32], wr[:, OFF_GL:OFF_GL + 2048]], axis=1)


def _assemble_in_proj(g):
    pieces = []
    for lo, hi in IN_SEGMENTS:
        while lo < hi:
            d = lo // IN_SHARD_W
            end = min(hi, (d + 1) * IN_SHARD_W)
            pieces.append(g[d][:, lo - d * IN_SHARD_W:end - d * IN_SHARD_W])
            lo = end
    pieces.append(jnp.zeros((g.shape[1], PROJ_W - IN_PROJ_W), g.dtype))
    return jnp.concatenate(pieces, axis=1)


def _in_proj_slabs(wr):
    orig = _restore_in_proj(wr)
    return jnp.stack([orig[:, d * IN_SHARD_W:(d + 1) * IN_SHARD_W] for d in range(N_DEV)])


def kernel(x, norm_mix_pre_w, w_in, b_gate, conv_w, conv_b, dt_bias, a_log, d_skip, ssm_norm_w, w_att_proj, w_ssm_proj, w_out, norm_mix_post_w, norm_ffn_pre_w, w_up, w_down, norm_ffn_post_w, loss_target, m_norm_mix_pre_w, m_w_in, m_b_gate, m_conv_w, m_conv_b, m_dt_bias, m_a_log, m_d_skip, m_ssm_norm_w, m_w_att_proj, m_w_ssm_proj, m_w_out, m_norm_mix_post_w, m_norm_ffn_pre_w, m_w_up, m_w_down, m_norm_ffn_post_w, v_norm_mix_pre_w, v_w_in, v_b_gate, v_conv_w, v_conv_b, v_dt_bias, v_a_log, v_d_skip, v_ssm_norm_w, v_w_att_proj, v_w_ssm_proj, v_w_out, v_norm_mix_post_w, v_norm_ffn_pre_w, v_w_up, v_w_down, v_norm_ffn_post_w):
    w = dict(norm_mix_pre_w=norm_mix_pre_w, w_in=w_in, b_gate=b_gate, conv_w=conv_w, conv_b=conv_b, dt_bias=dt_bias, a_log=a_log,
             d_skip=d_skip, ssm_norm_w=ssm_norm_w, w_att_proj=w_att_proj, w_ssm_proj=w_ssm_proj, w_out=w_out,
             norm_mix_post_w=norm_mix_post_w, norm_ffn_pre_w=norm_ffn_pre_w, w_up=w_up, w_down=w_down, norm_ffn_post_w=norm_ffn_post_w)
    m = dict(norm_mix_pre_w=m_norm_mix_pre_w, w_in=m_w_in, b_gate=m_b_gate, conv_w=m_conv_w, conv_b=m_conv_b, dt_bias=m_dt_bias,
             a_log=m_a_log, d_skip=m_d_skip, ssm_norm_w=m_ssm_norm_w, w_att_proj=m_w_att_proj, w_ssm_proj=m_w_ssm_proj, w_out=m_w_out,
             norm_mix_post_w=m_norm_mix_post_w, norm_ffn_pre_w=m_norm_ffn_pre_w, w_up=m_w_up, w_down=m_w_down, norm_ffn_post_w=m_norm_ffn_post_w)
    v = dict(norm_mix_pre_w=v_norm_mix_pre_w, w_in=v_w_in, b_gate=v_b_gate, conv_w=v_conv_w, conv_b=v_conv_b, dt_bias=v_dt_bias,
             a_log=v_a_log, d_skip=v_d_skip, ssm_norm_w=v_ssm_norm_w, w_att_proj=v_w_att_proj, w_ssm_proj=v_w_ssm_proj, w_out=v_w_out,
             norm_mix_post_w=v_norm_mix_post_w, norm_ffn_pre_w=v_norm_ffn_pre_w, w_up=v_w_up, w_down=v_w_down, norm_ffn_post_w=v_norm_ffn_post_w)
    shard_shapes = {n: w[n].shape[1:] for n in ORDER}

    mx, my, mc = _mesh_pos()
    me = 4 * mx + 2 * my + mc

    g_in, g_conv = _all_gather([w["w_in"][0].astype(BF16), w["conv_w"][0]])
    conv_full = jnp.moveaxis(g_conv, 0, 1).reshape(SSM_CONV, CONV_DIM)
    late_shards = [w[n][0].astype(BF16) for n in LATE]
    late_handle, token = _remote_start(late_shards, False, "late_weights_start")
    w_pre = w["norm_mix_pre_w"] + jnp.tile(token[0:1, :], (1, D_MODEL // LANE))

    def late_weights(after):
        full = dict(zip(LATE, _with_own(_remote_wait(late_handle, after, "late_weights_wait"), late_shards, me)))
        for n in ROW_SHARDED:
            full[n] = full[n].reshape(-1, full[n].shape[2])
        w_att = jnp.moveaxis(full["w_att_proj"], 0, 1).reshape(ATT_WIDTH, D_MODEL)
        return w_att, full["w_ssm_proj"], full["w_out"], full["w_up"], full["w_down"]

    started = {}

    def start_exchange(tag, slabs):
        own = [lax.dynamic_index_in_dim(s, me, 0, keepdims=False) for s in slabs]
        handle, tok = _remote_start(slabs, True, tag + "_grads_start")
        started[tag] = (handle, own)
        return tok

    def on_mid_grads(g):
        slabs = dict(w_up=g["w_up"], w_att_proj=jnp.moveaxis(g["w_att_proj"].reshape(ATT_WIDTH, N_DEV, -1), 1, 0))
        for n in ROW_SHARDED:
            slabs[n] = g[n].reshape(N_DEV, -1, g[n].shape[1])
        return start_exchange("mid", [slabs[n] for n in LATE])

    def on_in_proj_grads(gw_in_r, g_conv_w):
        return start_exchange("in_proj", [_in_proj_slabs(gw_in_r), jnp.moveaxis(g_conv_w.reshape(SSM_CONV, N_DEV, -1), 1, 0)])

    loss, g_x, grads = _local_step(
        x[0], loss_target[0], w_pre, _assemble_in_proj(g_in), w["b_gate"], conv_full, w["conv_b"], w["dt_bias"], w["a_log"],
        w["d_skip"], w["ssm_norm_w"], late_weights, w["norm_mix_post_w"], w["norm_ffn_pre_w"], w["norm_ffn_post_w"],
        on_mid_grads, on_in_proj_grads)

    recv = {}
    for tag, names in (("mid", LATE), ("in_proj", ("w_in", "conv_w"))):
        handle, own = started[tag]
        recv.update(zip(names, _with_own(_remote_wait(handle, g_x, tag + "_grads_wait"), own, me)))
    small = _pack([grads[n].astype(F32) for n in SMALL], 8)
    _, small_all = _exchange_grads([], small)

    small_shapes = [shard_shapes[n] for n in SMALL]
    small_out = _adamw(*[_pack([d_[n][0] for n in SMALL], 8) for d_ in (w, m, v)], small_all, "adamw_replicated", small_all.shape[1])
    big_out = {n: _adamw(w[n][0], m[n][0], v[n][0], recv[n], "adamw_" + n, ADAMW_ROWS[n]) for n in BIG}
    res = []
    for which, small_flat in enumerate(small_out):
        vals = {n: big_out[n][which] for n in BIG}
        vals.update(zip(SMALL, _unpack(small_flat, small_shapes)))
        res.append([vals[n][None] for n in ORDER])
    g_out, d_out, m_out, v_out = res
    total = lax.psum(loss[0, 0], ("x", "y", "c"))
    return (total, g_x[None], *g_out, *d_out, *m_out, *v_out)
```

```python
import functools
import math

import jax
import jax.numpy as jnp
import numpy as np
from jax import lax
from jax.experimental import pallas as pl
from jax.experimental.pallas import tpu as pltpu

F32 = jnp.float32
BF16 = jnp.bfloat16

D_MODEL = 1024
HEAD_DIM = 64
N_ATT_HEADS = 12
ATT_WIDTH = N_ATT_HEADS * HEAD_DIM
DILATED_PATTERNS = ((128, 1), (512, 4), (2048, 16))
ATT_BLOCK = 128
SSM_INNER = 2048
SSM_HEAD_DIM = 64
SSM_HEADS = 32
SSM_GROUPS = 8
SSM_STATE = 128
SSM_CHUNK = 128
CONV_DIM = 4096
SSM_CONV = 4
FFN_HIDDEN = 4096
RMS_EPS = 1e-6
N_DEV = 8

ADAM_LR = 0.001
ADAM_B1 = 0.9
ADAM_B2 = 0.999
ADAM_EPS = 1e-08
ADAM_WD = 0.01
ADAM_STEP = 10

LANE = 128
OFF_Z, OFF_GL, OFF_XBC, OFF_QKV, OFF_DT = 0, 2048, 4096, 8192, 10496
PROJ_W = 10752
PROJ_BLOCKS = PROJ_W // LANE
PA_W = OFF_QKV
PB_W = PROJ_W - OFF_QKV
PB_DT = OFF_DT - OFF_QKV
VMEM_LIMIT = 52 * 1024 * 1024
NEG = -1e30

HI = lax.Precision.HIGHEST
NT_DIMS = (((1,), (1,)), ((), ()))
TN_DIMS = (((0,), (0,)), ((), ()))
S = jax.ShapeDtypeStruct


def _params(sem):
    return pltpu.CompilerParams(dimension_semantics=sem, vmem_limit_bytes=VMEM_LIMIT)


def _matmul(a, b, *, mode, out_dtype, name, tm, tn, tk, epilogue=None, extra=None, stacked=False, after=None, b_cols=None):
    if mode == "nn":
        m, k = a.shape
        n = b.shape[0] * b.shape[2] if stacked else b.shape[1]
        col0 = 0
        if b_cols is not None:
            assert b_cols[0] % tn == 0, name
            col0, n = b_cols[0] // tn, b_cols[1]
        a_spec = pl.BlockSpec((tm, tk), lambda i, j, kk: (i, kk))
        b_spec = pl.BlockSpec((None, tk, tn), lambda i, j, kk: (j, kk, 0)) if stacked else pl.BlockSpec((tk, tn), lambda i, j, kk: (kk, col0 + j))
        dims = (((1,), (0,)), ((), ()))
    elif mode == "nt":
        m, k = a.shape
        n = b.shape[1] if stacked else b.shape[0]
        a_spec = pl.BlockSpec((tm, tk), lambda i, j, kk: (i, kk))
        b_spec = pl.BlockSpec((None, tn, tk), lambda i, j, kk: (kk, j, 0)) if stacked else pl.BlockSpec((tn, tk), lambda i, j, kk: (j, kk))
        dims = NT_DIMS
    else:
        (k, m), n = a.shape, b.shape[1]
        a_spec = pl.BlockSpec((tk, tm), lambda i, j, kk: (kk, i))
        b_spec = pl.BlockSpec((tk, tn), lambda i, j, kk: (kk, j))
        dims = TN_DIMS
    assert m % tm == 0 and n % tn == 0 and k % tk == 0, (name, m, n, k)
    if stacked:
        assert (tk if mode == "nt" else tn) * N_DEV == (k if mode == "nt" else n), name
    nk = k // tk
    o_spec = pl.BlockSpec((tm, tn), lambda i, j, kk: (i, j))
    in_specs, args = [a_spec, b_spec], [a, b]
    if epilogue == "relu2":
        out_shape = (S((m, n), BF16), S((m, n), BF16))
        out_specs = (o_spec, o_spec)
    elif stacked and mode == "tn":
        out_shape, out_specs = S((N_DEV, m, tn), out_dtype), pl.BlockSpec((None, tm, tn), lambda i, j, kk: (j, i, 0))
    else:
        out_shape, out_specs = S((m, n), out_dtype), o_spec
    if epilogue == "relu2_bwd":
        in_specs.append(o_spec)
        args.append(extra)
    n_in = len(args)
    if after is not None:
        in_specs.append(pl.BlockSpec(after.shape, lambda i, j, kk: (0,) * after.ndim))
        args.append(after)

    def finish(acc, refs):
        if epilogue == "relu2":
            r = jnp.maximum(acc, 0.0)
            refs[0][...] = (r * r).astype(BF16)
            refs[1][...] = acc.astype(BF16)
        elif epilogue == "relu2_bwd":
            up = refs[0][...].astype(F32)
            refs[1][...] = (acc * (2.0 * jnp.maximum(up, 0.0))).astype(out_dtype)
        else:
            refs[0][...] = acc.astype(out_dtype)

    def body(a_ref, b_ref, *rest):
        rest = rest[:n_in - 2] + rest[len(args) - 2:]
        part = lax.dot_general(a_ref[...].astype(BF16), b_ref[...].astype(BF16), dims, preferred_element_type=F32)
        if nk == 1:
            finish(part, rest)
            return
        acc_ref = rest[-1]
        kk = pl.program_id(2)

        @pl.when(kk == 0)
        def _():
            acc_ref[...] = part

        @pl.when(kk > 0)
        def _():
            acc_ref[...] += part

        @pl.when(kk == nk - 1)
        def _():
            finish(acc_ref[...], rest[:-1])

    scratch = [] if nk == 1 else [pltpu.VMEM((tm, tn), F32)]
    return pl.pallas_call(
        body, grid=(m // tm, n // tn, nk), in_specs=in_specs, out_specs=out_specs, out_shape=out_shape,
        scratch_shapes=scratch, name=name, compiler_params=_params(("parallel", "parallel", "arbitrary")),
    )(*args)


def _rowcall(body, name, n_rows, tr, ins, outs, scratch=()):
    res = pl.pallas_call(
        body, grid=(n_rows // tr,),
        in_specs=[pl.BlockSpec(bs, im) for _, bs, im in ins],
        out_specs=[pl.BlockSpec(bs, im) for _, _, bs, im in outs],
        out_shape=[S(sh, dt) for sh, dt, _, _ in outs],
        scratch_shapes=list(scratch), name=name, compiler_params=_params(("arbitrary",)),
    )(*[a for a, _, _ in ins])
    return res


def _rows(arr, tr, width=None, cb=0):
    width = arr.shape[1] if width is None else width
    return (arr, (tr, width), lambda i, cb=cb: (i, cb))


def _whole(arr):
    nd = arr.ndim
    return (arr, arr.shape, lambda i, nd=nd: (0,) * nd)


def _orow(n_rows, width, dtype, tr):
    return ((n_rows, width), dtype, (tr, width), lambda i: (i, 0))


def _oacc(width):
    return ((1, width), F32, (1, width), lambda i: (0, 0))


def _accumulate(ref, value):
    first = pl.program_id(0) == 0

    @pl.when(first)
    def _():
        ref[...] = value

    @pl.when(jnp.logical_not(first))
    def _():
        ref[...] += value


def _colsum(v):
    return jnp.sum(v, axis=0, keepdims=True)


def _rms_fwd(x, w):
    r = lax.rsqrt(jnp.mean(x * x, axis=-1, keepdims=True) + RMS_EPS)
    return x * r * w


def _rms_bwd(gy, x, w):
    r = lax.rsqrt(jnp.mean(x * x, axis=-1, keepdims=True) + RMS_EPS)
    xn = x * r
    gxn = gy * w
    gx = r * (gxn - xn * jnp.mean(gxn * xn, axis=-1, keepdims=True))
    return gx, _colsum(gy * xn)


def _sigmoid(x):
    return 1.0 / (1.0 + jnp.exp(-x))


def _head_expand(n_heads_pad, n_heads, width):
    h = lax.broadcasted_iota(jnp.int32, (n_heads_pad, n_heads * width), 0)
    c = lax.broadcasted_iota(jnp.int32, (n_heads_pad, n_heads * width), 1)
    return (c // width == h).astype(F32)


def _head_reduce(n_heads, width, n_heads_pad):
    c = lax.broadcasted_iota(jnp.int32, (n_heads * width, n_heads_pad), 0)
    h = lax.broadcasted_iota(jnp.int32, (n_heads * width, n_heads_pad), 1)
    return (c // width == h).astype(F32)


def _block_ones(n, width):
    r = lax.broadcasted_iota(jnp.int32, (n, n), 0)
    c = lax.broadcasted_iota(jnp.int32, (n, n), 1)
    return (r // width == c // width).astype(F32)


def _pre_norm(x, w_pre, tr=512):
    t = x.shape[0]

    def body(x_ref, w_ref, u_ref):
        u_ref[...] = _rms_fwd(x_ref[...], w_ref[...]).astype(BF16)

    return _rowcall(body, "pre_norm", t, tr, [_rows(x, tr), _whole(w_pre)], [_orow(t, D_MODEL, BF16, tr)])[0]


CONV_HALO = 16


def _row_shift(cur, halo, j):
    tr = cur.shape[0]
    r = lax.broadcasted_iota(jnp.int32, (tr, tr), 0)
    c = lax.broadcasted_iota(jnp.int32, (tr, tr), 1)
    main = jnp.dot((c == r + j).astype(BF16), cur, preferred_element_type=F32)
    er = lax.broadcasted_iota(jnp.int32, (CONV_HALO, CONV_HALO), 0)
    ec = lax.broadcasted_iota(jnp.int32, (CONV_HALO, CONV_HALO), 1)
    if j < 0:
        edge = jnp.dot((ec == CONV_HALO + er + j).astype(BF16), halo, preferred_element_type=F32)
        return jnp.concatenate([main[:CONV_HALO] + edge, main[CONV_HALO:]], axis=0)
    edge = jnp.dot((ec == er + j - CONV_HALO).astype(BF16), halo, preferred_element_type=F32)
    return jnp.concatenate([main[:tr - CONV_HALO], main[tr - CONV_HALO:] + edge], axis=0)


def _conv_fwd(proj, conv_w, conv_b, tr=256):
    t = proj.shape[0]
    cb = OFF_XBC // CONV_DIM
    halo = (proj, (CONV_HALO, CONV_DIM), lambda i: (jnp.maximum(i * (tr // CONV_HALO) - 1, 0), cb))

    def body(cur_ref, prev_ref, w_ref, b_ref, o_ref, xc_ref):
        cur = cur_ref[...]
        prev = jnp.where(pl.program_id(0) > 0, prev_ref[...], jnp.zeros_like(prev_ref[...]))
        acc = b_ref[...] + w_ref[3:4, :] * cur.astype(F32)
        for k in range(SSM_CONV - 1):
            acc = acc + w_ref[k:k + 1, :] * _row_shift(cur, prev, -(SSM_CONV - 1 - k))
        o_ref[...] = acc * _sigmoid(acc)
        xc_ref[...] = acc.astype(BF16)

    return _rowcall(body, "conv_fwd", t, tr, [_rows(proj, tr, CONV_DIM, cb), halo, _whole(conv_w), _whole(conv_b)],
                    [_orow(t, CONV_DIM, F32, tr), _orow(t, CONV_DIM, BF16, tr)])


def _dt_fwd(proj, dt_bias_pad, alog_pad, tr=512):
    t = proj.shape[0]

    def body(raw_ref, b_ref, al_ref, dtx_ref, csx_ref, cst_ref):
        v = raw_ref[...] + b_ref[...]
        dt = jnp.maximum(v, 0.0) + jnp.log1p(jnp.exp(-jnp.abs(v)))
        expand = _head_expand(LANE, SSM_HEADS, SSM_HEAD_DIM)
        dtx_ref[...] = _dot_split(dt, expand, 0, 3)
        la = dt * (-jnp.exp(al_ref[...]))
        row = lax.broadcasted_iota(jnp.int32, (SSM_CHUNK, SSM_CHUNK), 0)
        col = lax.broadcasted_iota(jnp.int32, (SSM_CHUNK, SSM_CHUNK), 1)
        tril = (col <= row).astype(F32)
        cs = jnp.concatenate([_dot_split(tril, la[k * SSM_CHUNK:(k + 1) * SSM_CHUNK, :], 1, 3) for k in range(tr // SSM_CHUNK)], axis=0)
        csx_ref[...] = _dot_split(cs, expand, 0, 3)
        cst_ref[...] = cs.T

    return _rowcall(body, "dt_fwd", t, tr, [_rows(proj, tr, LANE, PB_DT // LANE), _whole(dt_bias_pad), _whole(alog_pad)],
                    [_orow(t, SSM_INNER, F32, tr), _orow(t, SSM_INNER, F32, tr), ((LANE, t), F32, (LANE, tr), lambda i: (0, i))])


def _gate_norm_fwd(y_ssd, xa, proj, dskip_x, norm_w, tr=256):
    t = y_ssd.shape[0]
    gw = SSM_INNER // SSM_GROUPS

    def body(y_ref, xs_ref, z_ref, d_ref, w_ref, o_ref):
        z = z_ref[...].astype(F32)
        y3 = (y_ref[...].astype(F32) + d_ref[...] * xs_ref[...]) * (z * _sigmoid(z))
        for g in range(SSM_GROUPS):
            sl = slice(g * gw, (g + 1) * gw)
            o_ref[:, sl] = _rms_fwd(y3[:, sl], w_ref[:, sl]).astype(BF16)

    return _rowcall(body, "gate_norm_fwd", t, tr,
                    [_rows(y_ssd, tr), _rows(xa, tr, SSM_INNER, 0), _rows(proj, tr, SSM_INNER, OFF_Z // SSM_INNER), _whole(dskip_x), _whole(norm_w)],
                    [_orow(t, SSM_INNER, BF16, tr)])[0]


def _gating_fwd(proj, b_gate, att_p, ssm_p, tr=512):
    t = proj.shape[0]

    def body(gl_ref, b_ref, a_ref, s_ref, o_ref):
        gates = _sigmoid(gl_ref[...].astype(F32) + b_ref[...])
        o_ref[...] = (gates[:, :D_MODEL] * a_ref[...].astype(F32) + gates[:, D_MODEL:] * s_ref[...].astype(F32)).astype(BF16)

    return _rowcall(body, "gating_fwd", t, tr, [_rows(proj, tr, 2 * D_MODEL, OFF_GL // (2 * D_MODEL)), _whole(b_gate), _rows(att_p, tr), _rows(ssm_p, tr)],
                    [_orow(t, D_MODEL, BF16, tr)])[0]


def _mix_post_ffn_pre(x, mixed, w_post, w_fpre, tr=512):
    t = x.shape[0]

    def body(x_ref, m_ref, wp_ref, wf_ref, h1_ref, f_ref):
        h1 = x_ref[...] + _rms_fwd(m_ref[...], wp_ref[...])
        h1_ref[...] = h1
        f_ref[...] = _rms_fwd(h1, wf_ref[...]).astype(BF16)

    return _rowcall(body, "mix_post_ffn_pre", t, tr, [_rows(x, tr), _rows(mixed, tr), _whole(w_post), _whole(w_fpre)],
                    [_orow(t, D_MODEL, F32, tr), _orow(t, D_MODEL, BF16, tr)])


def _loss_and_ffn_post_bwd(h1, dn, w_fpost, target, tr=512):
    t = h1.shape[0]

    def body(h1_ref, dn_ref, w_ref, tg_ref, loss_ref, gh2_ref, gdn_ref, gw_ref):
        dn = dn_ref[...]
        w = w_ref[...]
        err = h1_ref[...] + _rms_fwd(dn, w) - tg_ref[...]
        _accumulate(loss_ref, jnp.zeros((1, LANE), F32) + 0.5 * jnp.sum(jnp.mean(err * err, axis=-1, keepdims=True)))
        gh2 = err * (1.0 / D_MODEL)
        gh2_ref[...] = gh2
        gdn, gw = _rms_bwd(gh2, dn, w)
        gdn_ref[...] = gdn.astype(BF16)
        _accumulate(gw_ref, gw)

    return _rowcall(body, "loss_ffn_post_bwd", t, tr, [_rows(h1, tr), _rows(dn, tr), _whole(w_fpost), _rows(target, tr)],
                    [_oacc(LANE), _orow(t, D_MODEL, F32, tr), _orow(t, D_MODEL, BF16, tr), _oacc(D_MODEL)])


def _ffn_pre_mix_post_bwd(g_h2, g_f, h1, w_fpre, mixed, w_post, tr=512):
    t = h1.shape[0]

    def body(gh2_ref, gf_ref, h1_ref, wf_ref, m_ref, wp_ref, gh1_ref, gm_ref, gwf_ref, gwp_ref):
        gx, gwf = _rms_bwd(gf_ref[...], h1_ref[...], wf_ref[...])
        gh1 = gh2_ref[...] + gx
        gh1_ref[...] = gh1
        gm, gwp = _rms_bwd(gh1, m_ref[...], wp_ref[...])
        gm_ref[...] = gm.astype(BF16)
        _accumulate(gwf_ref, gwf)
        _accumulate(gwp_ref, gwp)

    return _rowcall(body, "ffn_pre_mix_post_bwd", t, tr,
                    [_rows(g_h2, tr), _rows(g_f, tr), _rows(h1, tr), _whole(w_fpre), _rows(mixed, tr), _whole(w_post)],
                    [_orow(t, D_MODEL, F32, tr), _orow(t, D_MODEL, BF16, tr), _oacc(D_MODEL), _oacc(D_MODEL)])


def _gating_bwd(g_mixin, proj, b_gate, att_p, ssm_p, tr=512):
    t = proj.shape[0]

    def body(gm_ref, gl_ref, b_ref, a_ref, s_ref, ga_ref, gs_ref, ggl_ref, gb_ref):
        gates = _sigmoid(gl_ref[...].astype(F32) + b_ref[...])
        gm = gm_ref[...].astype(F32)
        g_att, g_ssm = gates[:, :D_MODEL], gates[:, D_MODEL:]
        ga_ref[...] = (gm * g_att).astype(BF16)
        gs_ref[...] = (gm * g_ssm).astype(BF16)
        ggl_a = gm * a_ref[...].astype(F32) * g_att * (1.0 - g_att)
        ggl_s = gm * s_ref[...].astype(F32) * g_ssm * (1.0 - g_ssm)
        ggl_ref[:, :D_MODEL] = ggl_a.astype(BF16)
        ggl_ref[:, D_MODEL:] = ggl_s.astype(BF16)
        _accumulate(gb_ref.at[:, :D_MODEL], _colsum(ggl_a))
        _accumulate(gb_ref.at[:, D_MODEL:], _colsum(ggl_s))

    return _rowcall(body, "gating_bwd", t, tr,
                    [_rows(g_mixin, tr), _rows(proj, tr, 2 * D_MODEL, OFF_GL // (2 * D_MODEL)), _whole(b_gate), _rows(att_p, tr), _rows(ssm_p, tr)],
                    [_orow(t, D_MODEL, BF16, tr), _orow(t, D_MODEL, BF16, tr), _orow(t, 2 * D_MODEL, BF16, tr), _oacc(2 * D_MODEL)])


def _gate_norm_bwd(g_y4, y_ssd, xa, proj, dskip_x, norm_w, tr=256):
    t = y_ssd.shape[0]
    gw = SSM_INNER // SSM_GROUPS

    def body(g_ref, y_ref, xs_ref, z_ref, d_ref, w_ref, gy2_ref, gz_ref, gnw_ref, gdx_ref, gd_ref):
        z = z_ref[...].astype(F32)
        xs = xs_ref[...]
        sg = _sigmoid(z)
        sz = z * sg
        y2 = y_ref[...].astype(F32) + d_ref[...] * xs
        y3 = y2 * sz
        g4 = g_ref[...].astype(F32)
        for g in range(SSM_GROUPS):
            sl = slice(g * gw, (g + 1) * gw)
            gy3, gnw = _rms_bwd(g4[:, sl], y3[:, sl], w_ref[:, sl])
            _accumulate(gnw_ref.at[:, sl], gnw)
            gy2 = gy3 * sz[:, sl]
            gy2_ref[:, sl] = gy2
            gz_ref[:, sl] = (gy3 * y2[:, sl] * (sg[:, sl] * (1.0 + z[:, sl] * (1.0 - sg[:, sl])))).astype(BF16)
            _accumulate(gdx_ref.at[:, sl], _colsum(gy2 * xs[:, sl]))
        tot = jnp.broadcast_to(gdx_ref[...], (8, SSM_INNER))
        gd_ref[...] = jnp.dot(tot, _head_reduce(SSM_HEADS, SSM_HEAD_DIM, LANE), precision=HI, preferred_element_type=F32)[0:1, :]

    return _rowcall(body, "gate_norm_bwd", t, tr,
                    [_rows(g_y4, tr), _rows(y_ssd, tr), _rows(xa, tr, SSM_INNER, 0), _rows(proj, tr, SSM_INNER, OFF_Z // SSM_INNER), _whole(dskip_x), _whole(norm_w)],
                    [_orow(t, SSM_INNER, F32, tr), _orow(t, SSM_INNER, BF16, tr), _oacc(SSM_INNER), _oacc(SSM_INNER), _oacc(LANE)])


def _dt_bwd(g_dtx, ga_rows, proj, dt_bias_pad, tr=512):
    t = proj.shape[0]

    def body(g_ref, ga_ref, raw_ref, b_ref, o_ref, gb_ref, gal_ref):
        red = _head_reduce(SSM_HEADS, SSM_HEAD_DIM, LANE)
        gdt = _dot_split(g_ref[...], red, 0, 3)
        graw = gdt * _sigmoid(raw_ref[...] + b_ref[...])
        o_ref[...] = graw.astype(BF16)
        _accumulate(gb_ref, _colsum(graw))
        tot = jnp.broadcast_to(_colsum(ga_ref[...]), (8, SSM_INNER))
        gal_ref[...] = jnp.dot(tot, red, precision=HI, preferred_element_type=F32)[0:1, :]

    return _rowcall(body, "dt_bwd", t, tr, [_rows(g_dtx, tr), _whole(ga_rows), _rows(proj, tr, LANE, PB_DT // LANE), _whole(dt_bias_pad)],
                    [_orow(t, LANE, BF16, tr), _oacc(LANE), _oacc(LANE)])


def _conv_bwd(g_xs, g_b, g_c, xc, proj, conv_w, tr=256):
    t = proj.shape[0]
    n_blk = t // tr
    cb = OFF_XBC // CONV_DIM
    nb, nc = SSM_INNER, SSM_INNER + SSM_GROUPS * SSM_STATE
    def nxt(arr, width):
        return (arr, (CONV_HALO, width), lambda i: (jnp.minimum((i + 1) * (tr // CONV_HALO), t // CONV_HALO - 1), 0))

    def body(gxs_ref, gxs_n, gb_ref, gb_n, gc_ref, gc_n, xc_ref, xc_n, x_ref, w_ref, o_ref, gcb_ref, gw0, gw1, gw2, gw3):
        def gxc_of(gxs, gb, gc, xc, keep):
            xcf = xc[...].astype(F32)
            sg = _sigmoid(xcf)
            dsilu = jnp.where(keep, sg * (1.0 + xcf * (1.0 - sg)), 0.0)
            return jnp.concatenate([gxs[...] * dsilu[:, :nb], gb[...] * dsilu[:, nb:nc], gc[...] * dsilu[:, nc:]], axis=1)

        gxc = gxc_of(gxs_ref, gb_ref, gc_ref, xc_ref, True)
        gxc16 = gxc.astype(BF16)
        nxt16 = gxc_of(gxs_n, gb_n, gc_n, xc_n, pl.program_id(0) < n_blk - 1).astype(BF16)
        x = x_ref[...].astype(F32)
        acc = w_ref[3:4, :] * gxc
        _accumulate(gw3, _colsum(gxc * x))
        _accumulate(gcb_ref, _colsum(gxc))
        for k, gw in enumerate((gw0, gw1, gw2)):
            shifted = _row_shift(gxc16, nxt16, SSM_CONV - 1 - k)
            acc = acc + w_ref[k:k + 1, :] * shifted
            _accumulate(gw, _colsum(shifted * x))
        o_ref[...] = acc.astype(BF16)

    ins = []
    for arr, width in ((g_xs, SSM_INNER), (g_b, nc - nb), (g_c, nc - nb), (xc, CONV_DIM)):
        ins += [_rows(arr, tr), nxt(arr, width)]
    ins += [_rows(proj, tr, CONV_DIM, cb), _whole(conv_w)]
    return _rowcall(body, "conv_bwd", t, tr, ins, [_orow(t, CONV_DIM, BF16, tr)] + [_oacc(CONV_DIM)] * 5)


def _pre_norm_bwd(g_h1, g_u, x, w_pre, tr=512):
    t = x.shape[0]

    def body(gh_ref, gu_ref, x_ref, w_ref, gx_ref, gw_ref):
        gx, gw = _rms_bwd(gu_ref[...], x_ref[...], w_ref[...])
        gx_ref[...] = gh_ref[...] + gx
        _accumulate(gw_ref, gw)

    return _rowcall(body, "pre_norm_bwd", t, tr, [_rows(g_h1, tr), _rows(g_u, tr), _rows(x, tr), _whole(w_pre)],
                    [_orow(t, D_MODEL, F32, tr), _oacc(D_MODEL)])


def _alibi_slopes(n):
    def pow2(m):
        start = 2.0 ** (-8.0 / m)
        return [start ** (i + 1) for i in range(m)]
    if (n & (n - 1)) == 0:
        s = pow2(n)
    else:
        c = 2 ** int(math.floor(math.log2(n)))
        s = pow2(c) + pow2(2 * c)[0::2][: n - c]
    return np.array(s, dtype=np.float32)


def _slope_rows():
    s = _alibi_slopes(N_ATT_HEADS).reshape(N_ATT_HEADS // 2, 2)
    return jnp.asarray(np.broadcast_to(np.repeat(s, HEAD_DIM, axis=1)[:, None, :], (N_ATT_HEADS // 2, 8, LANE)).copy())


ATT_MAX_BLOCK_ROWS = 2048


RESIDUE_MAJOR_FROM = 16


class _AttLayout:
    def __init__(self, t, dil):
        self.t, self.dil = t, dil
        self.rows = t // dil
        self.residue_major = dil >= RESIDUE_MAJOR_FROM
        if self.residue_major:
            bq, self.stride = min(512, self.rows), 1
        else:
            bq, self.stride = min(512, self.rows, ATT_MAX_BLOCK_ROWS // dil), dil
        self.nsub = bq // ATT_BLOCK
        self.nblk = self.rows // bq
        self.rb = bq * self.stride
        self.pb = ATT_BLOCK * self.stride
        self.n_pb = self.rows * self.stride // self.pb
        self.out_dtype = F32 if self.stride > 1 else BF16

    def qkv(self, proj):
        if self.residue_major:
            qkv = proj[:, :3 * ATT_WIDTH]
            return qkv.reshape(self.rows, self.dil * 3 * ATT_WIDTH), 3 * ATT_WIDTH // LANE, 0
        return proj, 0, 0

    def act(self, a):
        return a.reshape(self.rows, self.dil * ATT_WIDTH) if self.residue_major else a

    def act_shape(self):
        return (self.rows, self.dil * ATT_WIDTH) if self.residue_major else (self.t, ATT_WIDTH)

    def col(self, r, band, c):
        return r * band + c if self.residue_major else c


def _residue_rows(r, stride, first_block, n_blocks=1):
    if stride == 1:
        return pl.ds(first_block * ATT_BLOCK, n_blocks * ATT_BLOCK)
    return pl.ds(r + first_block * ATT_BLOCK * stride, n_blocks * ATT_BLOCK, stride=stride)


def _lane_half():
    return lax.broadcasted_iota(jnp.int32, (ATT_BLOCK, LANE), 1) // HEAD_DIM


def _att_scores_mask(dil, first):
    iq = lax.broadcasted_iota(jnp.int32, (ATT_BLOCK, 2 * ATT_BLOCK), 0)
    jk = lax.broadcasted_iota(jnp.int32, (ATT_BLOCK, 2 * ATT_BLOCK), 1)
    dist = ATT_BLOCK + iq - jk
    valid = (dist >= 0) & (dist <= ATT_BLOCK) & (jnp.logical_not(first) | (jk >= ATT_BLOCK))
    return (dist * dil).astype(F32), valid


def _att_fwd(proj, dil, slopes):
    t = proj.shape[0]
    lay = _AttLayout(t, dil)
    nsub, nblk, rb, pb = lay.nsub, lay.nblk, lay.rb, lay.pb
    src, band, qb = lay.qkv(proj)
    aw = ATT_WIDTH // LANE

    def spec(off, prev=False):
        if prev:
            return pl.BlockSpec((pb, LANE), lambda hp, i, r: (jnp.maximum(i * nsub - 1, 0), lay.col(r, band, qb + off + hp)))
        return pl.BlockSpec((rb, LANE), lambda hp, i, r: (i, lay.col(r, band, qb + off + hp)))

    o_spec = pl.BlockSpec((rb, LANE), lambda hp, i, r: (i, lay.col(r, aw, hp)))

    def body(q_ref, kc_ref, kp_ref, vc_ref, vp_ref, sl_ref, o_ref, lse_ref):
        i, r = pl.program_id(1), pl.program_id(2)
        half = _lane_half()
        for sub in range(nsub):
            rs = _residue_rows(r, lay.stride, sub)
            q = (q_ref[rs, :] * (HEAD_DIM ** -0.5)).astype(BF16)
            if sub == 0:
                r0 = _residue_rows(r, lay.stride, 0)
                kk = jnp.concatenate([kp_ref[r0, :], kc_ref[rs, :]], axis=0).astype(BF16)
                vv = jnp.concatenate([vp_ref[r0, :], vc_ref[rs, :]], axis=0).astype(BF16)
                first = i == 0
            else:
                ks = _residue_rows(r, lay.stride, sub - 1, 2)
                kk, vv = kc_ref[ks, :].astype(BF16), vc_ref[ks, :].astype(BF16)
                first = jnp.bool_(False)
            dist, valid = _att_scores_mask(dil, first)
            outs, lses = [], []
            for e in range(2):
                qe = jnp.where(half == e, q, jnp.zeros_like(q))
                s = lax.dot_general(qe, kk, NT_DIMS, preferred_element_type=F32)
                s = s + jnp.where(valid, -sl_ref[0:1, e * HEAD_DIM:e * HEAD_DIM + 1] * dist, NEG)
                m = jnp.max(s, axis=-1, keepdims=True)
                p = jnp.exp(s - m)
                l = jnp.sum(p, axis=-1, keepdims=True)
                outs.append(jnp.dot(p.astype(BF16), vv, preferred_element_type=F32) / l)
                lses.append(m + jnp.log(l))
            o_ref[rs, :] = jnp.where(half == 0, outs[0], outs[1]).astype(lay.out_dtype)
            lse_ref[rs, :] = jnp.where(half == 0, lses[0], lses[1])

    o, lse = pl.pallas_call(
        body, grid=(N_ATT_HEADS // 2, nblk, dil),
        in_specs=[spec(0), spec(6), spec(6, True), spec(12), spec(12, True), pl.BlockSpec((None, 8, LANE), lambda hp, i, r: (hp, 0, 0))],
        out_specs=[o_spec, o_spec], out_shape=[S(lay.act_shape(), lay.out_dtype), S(lay.act_shape(), F32)],
        name=f"att_fwd_d{dil}", compiler_params=_params(("parallel", "parallel", "arbitrary")),
    )(src, src, src, src, src, slopes)
    return o.reshape(t, ATT_WIDTH), lse.reshape(t, ATT_WIDTH)


def _att_combine(outs, lses, tr=512):
    t = outs[0].shape[0]

    def body(o0, o1, o2, l0, l1, l2, att_ref, lse_ref):
        ls = [l0[...], l1[...], l2[...]]
        m = jnp.maximum(jnp.maximum(ls[0], ls[1]), ls[2])
        ws = [jnp.exp(l - m) for l in ls]
        tot = ws[0] + ws[1] + ws[2]
        num = ws[0] * o0[...].astype(F32) + ws[1] * o1[...].astype(F32) + ws[2] * o2[...].astype(F32)
        att_ref[...] = (num / tot).astype(BF16)
        lse_ref[...] = m + jnp.log(tot)

    return _rowcall(body, "att_combine", t, tr, [_rows(a, tr) for a in list(outs) + list(lses)],
                    [_orow(t, ATT_WIDTH, BF16, tr), _orow(t, ATT_WIDTH, F32, tr)])


def _att_delta(g_att, att, tr=512):
    t = att.shape[0]

    def body(g_ref, a_ref, o_ref):
        prod = g_ref[...] * a_ref[...].astype(F32)
        o_ref[...] = _dot_split(prod, _block_ones(ATT_WIDTH, HEAD_DIM), 0, 3)

    return _rowcall(body, "att_delta", t, tr, [_rows(g_att, tr), _rows(att, tr)], [_orow(t, ATT_WIDTH, F32, tr)])[0]


def _att_bwd(proj, g_att, lse, delta, dil, slopes):
    t = proj.shape[0]
    lay = _AttLayout(t, dil)
    nsub, nblk, rb, pb, n_pb = lay.nsub, lay.nblk, lay.rb, lay.pb, lay.n_pb
    src, band, qb = lay.qkv(proj)
    aw = ATT_WIDTH // LANE

    def near(i, which):
        return jnp.maximum(i * nsub - 1, 0) if which == "prev" else jnp.minimum((i + 1) * nsub, n_pb - 1)

    def pspec(off, which=None):
        if which:
            return pl.BlockSpec((pb, LANE), lambda hp, i, r: (near(i, which), lay.col(r, band, qb + off + hp)))
        return pl.BlockSpec((rb, LANE), lambda hp, i, r: (i, lay.col(r, band, qb + off + hp)))

    def aspec(which=None):
        if which:
            return pl.BlockSpec((pb, LANE), lambda hp, i, r: (near(i, which), lay.col(r, aw, hp)))
        return pl.BlockSpec((rb, LANE), lambda hp, i, r: (i, lay.col(r, aw, hp)))

    scale = HEAD_DIM ** -0.5

    def body(q_ref, qn_ref, kc_ref, kp_ref, vc_ref, vp_ref, do_ref, don_ref, lse_ref, lsen_ref, dl_ref, dln_ref, sl_ref,
             dq_ref, dk_ref, dv_ref):
        i, r = pl.program_id(1), pl.program_id(2)
        half = _lane_half()

        def tile_grads(q, do, lse_q, dl_q, kk, vv, dist, valid):
            dqs, dks, dvs = [], [], []
            for e in range(2):
                c = e * HEAD_DIM
                qe = jnp.where(half == e, q, jnp.zeros_like(q))
                doe = jnp.where(half == e, do, jnp.zeros_like(do))
                s = lax.dot_general(qe, kk, NT_DIMS, preferred_element_type=F32)
                s = s + jnp.where(valid, -sl_ref[0:1, c:c + 1] * dist, NEG)
                p = jnp.exp(s - lse_q[:, c:c + 1])
                dp = lax.dot_general(doe, vv, NT_DIMS, preferred_element_type=F32)
                ds16 = (p * (dp - dl_q[:, c:c + 1])).astype(BF16)
                dqs.append(jnp.dot(ds16, kk, preferred_element_type=F32))
                dks.append(lax.dot_general(ds16, q, TN_DIMS, preferred_element_type=F32))
                dvs.append(lax.dot_general(p.astype(BF16), do, TN_DIMS, preferred_element_type=F32))
            halfk = lax.broadcasted_iota(jnp.int32, dks[0].shape, 1) // HEAD_DIM
            return (jnp.where(half == 0, dqs[0], dqs[1]) * scale,
                    jnp.where(halfk == 0, dks[0], dks[1]), jnp.where(halfk == 0, dvs[0], dvs[1]))

        carry_k = carry_v = None
        for sub in range(nsub):
            rs = _residue_rows(r, lay.stride, sub)
            q = (q_ref[rs, :] * scale).astype(BF16)
            do = do_ref[rs, :].astype(BF16)
            if sub == 0:
                r0 = _residue_rows(r, lay.stride, 0)
                kk = jnp.concatenate([kp_ref[r0, :], kc_ref[rs, :]], axis=0).astype(BF16)
                vv = jnp.concatenate([vp_ref[r0, :], vc_ref[rs, :]], axis=0).astype(BF16)
                first = i == 0
            else:
                ks = _residue_rows(r, lay.stride, sub - 1, 2)
                kk, vv = kc_ref[ks, :].astype(BF16), vc_ref[ks, :].astype(BF16)
                first = jnp.bool_(False)
            dist, valid = _att_scores_mask(dil, first)
            dq, dk2, dv2 = tile_grads(q, do, lse_ref[rs, :], dl_ref[rs, :], kk, vv, dist, valid)
            dq_ref[rs, :] = dq.astype(lay.out_dtype)
            if sub > 0:
                rp = _residue_rows(r, lay.stride, sub - 1)
                dk_ref[rp, :] = (carry_k + dk2[:ATT_BLOCK, :]).astype(lay.out_dtype)
                dv_ref[rp, :] = (carry_v + dv2[:ATT_BLOCK, :]).astype(lay.out_dtype)
            carry_k, carry_v = dk2[ATT_BLOCK:, :], dv2[ATT_BLOCK:, :]
        rl = _residue_rows(r, lay.stride, nsub - 1)
        rn = _residue_rows(r, lay.stride, 0)
        iq = lax.broadcasted_iota(jnp.int32, (ATT_BLOCK, ATT_BLOCK), 0)
        jk = lax.broadcasted_iota(jnp.int32, (ATT_BLOCK, ATT_BLOCK), 1)
        dist_i = ATT_BLOCK + iq - jk
        valid = (dist_i >= 0) & (dist_i <= ATT_BLOCK) & (i < nblk - 1)
        qn = (qn_ref[rn, :] * scale).astype(BF16)
        _, dk1, dv1 = tile_grads(qn, don_ref[rn, :].astype(BF16), lsen_ref[rn, :], dln_ref[rn, :],
                                 kc_ref[rl, :].astype(BF16), vc_ref[rl, :].astype(BF16), (dist_i * dil).astype(F32), valid)
        dk_ref[rl, :] = (carry_k + dk1).astype(lay.out_dtype)
        dv_ref[rl, :] = (carry_v + dv1).astype(lay.out_dtype)

    gv, lv, dlv = lay.act(g_att), lay.act(lse), lay.act(delta)
    dq, dk, dv = pl.pallas_call(
        body, grid=(N_ATT_HEADS // 2, nblk, dil),
        in_specs=[pspec(0), pspec(0, "next"), pspec(6), pspec(6, "prev"), pspec(12), pspec(12, "prev"),
                  aspec(), aspec("next"), aspec(), aspec("next"), aspec(), aspec("next"),
                  pl.BlockSpec((None, 8, LANE), lambda hp, i, r: (hp, 0, 0))],
        out_specs=[aspec(), aspec(), aspec()], out_shape=[S(lay.act_shape(), lay.out_dtype)] * 3,
        name=f"att_bwd_d{dil}", compiler_params=_params(("parallel", "parallel", "arbitrary")),
    )(src, src, src, src, src, src, gv, gv, lv, lv, dlv, dlv, slopes)
    return dq.reshape(t, ATT_WIDTH), dk.reshape(t, ATT_WIDTH), dv.reshape(t, ATT_WIDTH)


def _att_grad_sum(dqs, dks, dvs, tr=512):
    t = dqs[0].shape[0]

    def body(*refs):
        o_ref = refs[-1]
        for n in range(3):
            tot = refs[3 * n][...].astype(F32) + refs[3 * n + 1][...].astype(F32) + refs[3 * n + 2][...].astype(F32)
            o_ref[:, n * ATT_WIDTH:(n + 1) * ATT_WIDTH] = tot.astype(BF16)

    return _rowcall(body, "att_grad_sum", t, tr, [_rows(a, tr) for a in list(dqs) + list(dks) + list(dvs)],
                    [_orow(t, 3 * ATT_WIDTH, BF16, tr)])[0]


def _ssd_common(xs, dtx, cs, cs_t):
    ch = SSM_CHUNK
    row = lax.broadcasted_iota(jnp.int32, (ch, ch), 0)
    col = lax.broadcasted_iota(jnp.int32, (ch, ch), 1)
    cs_last = cs[ch - 1:ch, :]
    return dict(tril=col <= row, row=row, col=col, cs=cs, cs_t=cs_t, cs_last=cs_last,
                e=jnp.exp(cs), w=jnp.exp(cs_last - cs), xd=xs * dtx)


def _dot_split(a, b, split, terms=2):
    ops = [a, b]
    rest = ops[split]
    other = ops[1 - split].astype(BF16)
    out = None
    for _ in range(terms):
        piece = rest.astype(BF16)
        rest = rest - piece.astype(F32)
        part = jnp.dot(other, piece, preferred_element_type=F32) if split == 1 else jnp.dot(piece, other, preferred_element_type=F32)
        out = part if out is None else out + part
    return out


def _decay_col(cs_t, heads_per_group):
    r = lax.broadcasted_iota(jnp.int32, (heads_per_group * SSM_HEAD_DIM, SSM_STATE), 0) // SSM_HEAD_DIM
    out = jnp.zeros((heads_per_group * SSM_HEAD_DIM, SSM_STATE), F32)
    for j in range(heads_per_group):
        out = jnp.where(r == j, jnp.exp(cs_t[j:j + 1, SSM_CHUNK - 1:SSM_CHUNK]), out)
    return out


SSD_GROUPS_PER_STEP = 2


def _ssd_specs(t):
    hg = SSM_HEADS // SSM_GROUPS
    gw = hg * SSM_HEAD_DIM
    nb0 = SSM_INNER // SSM_STATE
    return hg, gw, nb0


def _ssd_group_views(gi, gw, wide, narrow, stacked):
    w = [r.at[:, pl.ds(gi * gw, gw)] for r in wide]
    n = [r.at[:, pl.ds(gi * SSM_STATE, SSM_STATE)] for r in narrow]
    return w, n, [r.at[gi] for r in stacked]


def _ssd_fwd(xa, dtx, csx, cst_g):
    t = xa.shape[0]
    nch = t // SSM_CHUNK
    hg, gw, nb0 = _ssd_specs(t)
    ch = SSM_CHUNK
    gp = SSD_GROUPS_PER_STEP

    def body(xs_ref, b_ref, c_ref, dtx_ref, cs_ref, cst_ref, y_ref, st_ref, h_scr):
        for gi in range(gp):
            (xs_g, dtx_g, cs_g, y_g), (b_g, c_g), (cst_gi, st_g) = _ssd_group_views(
                gi, gw, (xs_ref, dtx_ref, cs_ref, y_ref), (b_ref, c_ref), (cst_ref, st_ref))
            group_body(pl.program_id(0), pl.program_id(1) * gp + gi, xs_g, b_g, c_g, dtx_g, cs_g, cst_gi, y_g, st_g, h_scr)

    def group_body(cc, g, xs_ref, b_ref, c_ref, dtx_ref, cs_ref, cst_ref, y_ref, st_ref, h_scr):
        @pl.when(cc == 0)
        def _():
            h_scr[g] = jnp.zeros((gw, SSM_STATE), F32)

        q = _ssd_common(xs_ref[...], dtx_ref[...], cs_ref[...], cst_ref[...])
        bb, cb = b_ref[...].astype(BF16), c_ref[...].astype(BF16)
        cbm = lax.dot_general(cb, bb, NT_DIMS, preferred_element_type=F32)
        h = h_scr[g]
        st_ref[...] = h
        xd16 = q["xd"].astype(BF16)
        y = lax.dot_general(cb, h.astype(BF16), NT_DIMS, preferred_element_type=F32) * q["e"]
        lane_head = lax.broadcasted_iota(jnp.int32, (ch, gw), 1) // SSM_HEAD_DIM
        for j in range(hg):
            diff = q["cs"][:, j * SSM_HEAD_DIM:j * SSM_HEAD_DIM + 1] - q["cs_t"][j:j + 1, :]
            gmat = cbm * jnp.exp(jnp.where(q["tril"], diff, NEG))
            yj = jnp.dot(gmat.astype(BF16), xd16, preferred_element_type=F32)
            y = y + jnp.where(lane_head == j, yj, 0.0)
        y_ref[...] = y.astype(BF16)
        s_new = lax.dot_general((q["xd"] * q["w"]).astype(BF16), bb, TN_DIMS, preferred_element_type=F32)
        h_scr[g] = _decay_col(q["cs_t"], hg) * h + s_new

    wide = pl.BlockSpec((ch, gp * gw), lambda cc, g: (cc, g))
    return pl.pallas_call(
        body, grid=(nch, SSM_GROUPS // gp),
        in_specs=[wide,
                  pl.BlockSpec((ch, gp * SSM_STATE), lambda cc, g: (cc, nb0 // gp + g)),
                  pl.BlockSpec((ch, gp * SSM_STATE), lambda cc, g: (cc, (nb0 + SSM_GROUPS) // gp + g)),
                  wide, wide,
                  pl.BlockSpec((gp, 8, ch), lambda cc, g: (g, 0, cc))],
        out_specs=[wide, pl.BlockSpec((None, gp, gw, SSM_STATE), lambda cc, g: (cc, g, 0, 0))],
        out_shape=[S((t, SSM_INNER), BF16), S((nch, SSM_GROUPS, gw, SSM_STATE), F32)],
        scratch_shapes=[pltpu.VMEM((SSM_GROUPS, gw, SSM_STATE), F32)],
        name="ssd_fwd", compiler_params=_params(("arbitrary", "arbitrary")),
    )(xa, xa, xa, dtx, csx, cst_g)


def _ssd_bwd(xa, dtx, csx, cst_g, alog_x, g_y, states, dskip_x):
    t = xa.shape[0]
    nch = t // SSM_CHUNK
    hg, gw, nb0 = _ssd_specs(t)
    ch = SSM_CHUNK
    gp = SSD_GROUPS_PER_STEP

    def rc(cc):
        return nch - 1 - cc

    def body(xs_ref, b_ref, c_ref, dtx_ref, cs_ref, cst_ref, alx_ref, gy_ref, st_ref, dsk_ref,
             gxs_ref, gb_ref, gc_ref, gdt_ref, ga_ref, gh_scr):
        for gi in range(gp):
            wide, narrow, stacked = _ssd_group_views(
                gi, gw, (xs_ref, dtx_ref, cs_ref, alx_ref, gy_ref, dsk_ref, gxs_ref, gdt_ref, ga_ref), (b_ref, c_ref, gb_ref, gc_ref),
                (cst_ref, st_ref))
            xs_g, dtx_g, cs_g, alx_g, gy_g, dsk_g, gxs_g, gdt_g, ga_g = wide
            b_g, c_g, gb_g, gc_g = narrow
            group_body(pl.program_id(0), pl.program_id(1) * gp + gi, xs_g, b_g, c_g, dtx_g, cs_g, stacked[0], alx_g, gy_g, stacked[1],
                       dsk_g, gxs_g, gb_g, gc_g, gdt_g, ga_g, gh_scr)

    def group_body(cc, g, xs_ref, b_ref, c_ref, dtx_ref, cs_ref, cst_ref, alx_ref, gy_ref, st_ref, dsk_ref,
                   gxs_ref, gb_ref, gc_ref, gdt_ref, ga_ref, gh_scr):
        @pl.when(cc == 0)
        def _():
            gh_scr[g] = jnp.zeros((gw, SSM_STATE), F32)

        xs, dtx = xs_ref[...], dtx_ref[...]
        q = _ssd_common(xs, dtx, cs_ref[...], cst_ref[...])
        cs, cs_t, e, w, xd = q["cs"], q["cs_t"], q["e"], q["w"], q["xd"]
        bb, cb = b_ref[...].astype(BF16), c_ref[...].astype(BF16)
        gy = gy_ref[...]
        gy16, xd16 = gy.astype(BF16), xd.astype(BF16)
        h = st_ref[...]
        h16 = h.astype(BF16)
        ghn = gh_scr[g]
        ghn16 = ghn.astype(BF16)
        seg = _block_ones(gw, SSM_HEAD_DIM)
        cbm = lax.dot_general(cb, bb, NT_DIMS, preferred_element_type=F32)

        gye16 = (gy * e).astype(BF16)
        chm = lax.dot_general(cb, h16, NT_DIMS, preferred_element_type=F32)
        g_c = jnp.dot(gye16, h16, preferred_element_type=F32)
        gh_off = lax.dot_general(gye16, cb, TN_DIMS, preferred_element_type=F32)
        g_e = _dot_split(gy * chm, seg, 0)
        bgs = lax.dot_general(bb, ghn16, NT_DIMS, preferred_element_type=F32)
        g_xd = w * bgs
        g_w = _dot_split(xd * bgs, seg, 0)
        g_b = jnp.dot((xd * w).astype(BF16), ghn16, preferred_element_type=F32)
        decay = _decay_col(cs_t, hg)
        gh_scr[g] = decay * ghn + gh_off
        rsum = jnp.sum(ghn * h, axis=1, keepdims=True)
        lane_head = lax.broadcasted_iota(jnp.int32, (ch, gw), 1) // SSM_HEAD_DIM
        lane_head1 = lax.broadcasted_iota(jnp.int32, (1, gw), 1) // SSM_HEAD_DIM
        g_el = jnp.zeros((1, gw), F32)
        g_cs = g_e * e - g_w * w
        upper = q["row"] <= q["col"]
        ones16 = jnp.ones((ch, ch), BF16)
        for j in range(hg):
            g_el = jnp.where(lane_head1 == j, jnp.sum(rsum[j * SSM_HEAD_DIM:(j + 1) * SSM_HEAD_DIM, :], axis=0, keepdims=True), g_el)
            csc = cs[:, j * SSM_HEAD_DIM:j * SSM_HEAD_DIM + 1]
            csr = cs_t[j:j + 1, :]
            lm = jnp.exp(jnp.where(q["tril"], csc - csr, NEG))
            gyj = jnp.where(lane_head == j, gy16, jnp.zeros_like(gy16))
            gg = lax.dot_general(gyj, xd16, NT_DIMS, preferred_element_type=F32)
            gcb = gg * lm
            gcb16 = gcb.astype(BF16)
            g_c = g_c + jnp.dot(gcb16, bb, preferred_element_type=F32)
            g_b = g_b + lax.dot_general(gcb16, cb, TN_DIMS, preferred_element_type=F32)
            gxdj = lax.dot_general((cbm * lm).astype(BF16), gy16, TN_DIMS, preferred_element_type=F32)
            g_xd = g_xd + jnp.where(lane_head == j, gxdj, 0.0)
            m_ls = gcb * cbm
            m_hi = m_ls.astype(BF16)
            m_lo = (m_ls - m_hi.astype(F32)).astype(BF16)
            d_cs = (jnp.dot(m_hi, ones16, preferred_element_type=F32) + jnp.dot(m_lo, ones16, preferred_element_type=F32)
                    - lax.dot_general(m_hi, ones16, TN_DIMS, preferred_element_type=F32)
                    - lax.dot_general(m_lo, ones16, TN_DIMS, preferred_element_type=F32))
            g_cs = g_cs + jnp.where(lane_head == j, jnp.concatenate([d_cs] * (gw // ch), axis=1), 0.0)
        extra = _colsum(g_w * w) + g_el * jnp.exp(q["cs_last"])
        g_cs = g_cs + jnp.where(lax.broadcasted_iota(jnp.int32, (ch, gw), 0) == ch - 1, extra, 0.0)
        g_la = _dot_split(upper, g_cs, 1)
        a_x = -jnp.exp(alx_ref[...])
        gdt_ref[...] = g_xd * xs + g_la * a_x * (1.0 / SSM_HEAD_DIM)
        ga_row = _colsum(g_la * (dtx * a_x)) * (1.0 / SSM_HEAD_DIM)
        ga_ref[...] = jnp.where(lax.broadcasted_iota(jnp.int32, (8, gw), 0) == 0, ga_row, 0.0)
        gxs_ref[...] = g_xd * dtx + gy * dsk_ref[...]
        gb_ref[...] = g_b
        gc_ref[...] = g_c

    wide = pl.BlockSpec((ch, gp * gw), lambda cc, g: (rc(cc), g))
    narrow = pl.BlockSpec((ch, gp * SSM_STATE), lambda cc, g: (rc(cc), g))
    row = pl.BlockSpec((1, gp * gw), lambda cc, g: (0, g))
    return pl.pallas_call(
        body, grid=(nch, SSM_GROUPS // gp),
        in_specs=[wide,
                  pl.BlockSpec((ch, gp * SSM_STATE), lambda cc, g: (rc(cc), nb0 // gp + g)),
                  pl.BlockSpec((ch, gp * SSM_STATE), lambda cc, g: (rc(cc), (nb0 + SSM_GROUPS) // gp + g)),
                  wide, wide,
                  pl.BlockSpec((gp, 8, ch), lambda cc, g: (g, 0, rc(cc))),
                  row, wide,
                  pl.BlockSpec((None, gp, gw, SSM_STATE), lambda cc, g: (rc(cc), g, 0, 0)),
                  row],
        out_specs=[wide, narrow, narrow, wide, pl.BlockSpec((8, gp * gw), lambda cc, g: (rc(cc), g))],
        out_shape=[S((t, SSM_INNER), F32), S((t, SSM_GROUPS * SSM_STATE), F32), S((t, SSM_GROUPS * SSM_STATE), F32),
                   S((t, SSM_INNER), F32), S((nch * 8, SSM_INNER), F32)],
        scratch_shapes=[pltpu.VMEM((SSM_GROUPS, gw, SSM_STATE), F32)],
        name="ssd_bwd", compiler_params=_params(("arbitrary", "arbitrary")),
    )(xa, xa, xa, dtx, csx, cst_g, alog_x, g_y, states, dskip_x)


def _local_step(x, target, w_pre, w_in_r, b_gate, conv_w, conv_b, dt_bias, a_log, d_skip, ssm_norm_w,
                late_weights, w_post, w_fpre, w_fpost, on_mid_grads, on_in_proj_grads):
    t = x.shape[0]
    mm = functools.partial(_matmul, tm=512)
    slopes = _slope_rows()
    hg = SSM_HEADS // SSM_GROUPS
    dt_bias_pad = jnp.pad(dt_bias, ((0, 0), (0, LANE - SSM_HEADS)))
    alog_x = jnp.repeat(a_log, SSM_HEAD_DIM, axis=1)
    alog_pad = jnp.pad(a_log, ((0, 0), (0, LANE - SSM_HEADS)))
    dskip_x = jnp.repeat(d_skip, SSM_HEAD_DIM, axis=1)

    u = _pre_norm(x, w_pre)
    pa = mm(u, w_in_r, mode="nn", out_dtype=BF16, name="in_proj_zgx", tn=2048, tk=D_MODEL, b_cols=(0, PA_W))
    pb = mm(u, w_in_r[:, PA_W:], mode="nn", out_dtype=F32, name="in_proj_qkvdt", tn=PB_W // 2, tk=D_MODEL)
    fwd = [_att_fwd(pb, dil, slopes) for _, dil in DILATED_PATTERNS]
    att, lse = _att_combine([o for o, _ in fwd], [l for _, l in fwd])
    xa, xc = _conv_fwd(pa, conv_w, conv_b)
    dtx, csx, cst = _dt_fwd(pb, dt_bias_pad, alog_pad)
    cst_g = jnp.pad(cst[:SSM_HEADS].reshape(SSM_GROUPS, hg, t), ((0, 0), (0, 8 - hg), (0, 0)))
    y_ssd, states = _ssd_fwd(xa, dtx, csx, cst_g)
    y4 = _gate_norm_fwd(y_ssd, xa, pa, dskip_x, ssm_norm_w)
    w_att, w_ssm, w_out, w_up, w_down = late_weights(y4)
    att_p = mm(att, w_att, mode="nn", out_dtype=BF16, name="att_proj", tn=D_MODEL, tk=ATT_WIDTH)
    ssm_p = mm(y4, w_ssm, mode="nn", out_dtype=BF16, name="ssm_proj", tn=D_MODEL, tk=SSM_INNER)
    mixin = _gating_fwd(pa, b_gate, att_p, ssm_p)
    mixed = mm(mixin, w_out, mode="nn", out_dtype=F32, name="out_proj", tn=D_MODEL, tk=D_MODEL)
    h1, f = _mix_post_ffn_pre(x, mixed, w_post, w_fpre)
    act, up = _matmul(f, w_up, mode="nn", out_dtype=BF16, name="ffn_up", tm=2048, tn=FFN_HIDDEN // N_DEV, tk=D_MODEL, epilogue="relu2", stacked=True)
    dn = mm(act, w_down, mode="nn", out_dtype=F32, name="ffn_down", tn=D_MODEL, tk=FFN_HIDDEN)
    loss, g_h2, g_dn, gw_fpost = _loss_and_ffn_post_bwd(h1, dn, w_fpost, target)

    g_up = mm(g_dn, w_down, mode="nt", out_dtype=BF16, name="ffn_down_bwd_x", tn=2048, tk=D_MODEL, epilogue="relu2_bwd", extra=up)
    gw_down = _matmul(act, g_dn, mode="tn", out_dtype=BF16, name="ffn_down_bwd_w", tm=1024, tn=D_MODEL, tk=2048)
    w_up_rows = jnp.moveaxis(w_up, 0, 1).reshape(D_MODEL, FFN_HIDDEN)
    g_f = mm(g_up, w_up_rows, mode="nt", out_dtype=F32, name="ffn_up_bwd_x", tn=D_MODEL, tk=FFN_HIDDEN)
    gw_up = _matmul(f, g_up, mode="tn", out_dtype=BF16, name="ffn_up_bwd_w", tm=D_MODEL, tn=FFN_HIDDEN // N_DEV, tk=2048, stacked=True)
    g_h1, g_mixed, gw_fpre, gw_post = _ffn_pre_mix_post_bwd(g_h2, g_f, h1, w_fpre, mixed, w_post)
    g_mixin = mm(g_mixed, w_out, mode="nt", out_dtype=BF16, name="out_proj_bwd_x", tn=D_MODEL, tk=D_MODEL)
    gw_out = _matmul(mixin, g_mixed, mode="tn", out_dtype=BF16, name="out_proj_bwd_w", tm=D_MODEL, tn=D_MODEL, tk=2048)
    g_att_p, g_ssm_p, g_gl, g_b_gate = _gating_bwd(g_mixin, pa, b_gate, att_p, ssm_p)
    g_att = mm(g_att_p, w_att, mode="nt", out_dtype=F32, name="att_proj_bwd_x", tn=ATT_WIDTH, tk=D_MODEL)
    gw_att = _matmul(att, g_att_p, mode="tn", out_dtype=BF16, name="att_proj_bwd_w", tm=ATT_WIDTH, tn=D_MODEL, tk=2048)
    g_y4 = mm(g_ssm_p, w_ssm, mode="nt", out_dtype=BF16, name="ssm_proj_bwd_x", tn=SSM_INNER, tk=D_MODEL)
    gw_ssm = _matmul(y4, g_ssm_p, mode="tn", out_dtype=BF16, name="ssm_proj_bwd_w", tm=1024, tn=D_MODEL, tk=2048)
    token = on_mid_grads(dict(w_att_proj=gw_att, w_ssm_proj=gw_ssm, w_out=gw_out, w_up=gw_up, w_down=gw_down))
    if token is not None:
        ssm_norm_w = ssm_norm_w + jnp.tile(token[0:1, :], (1, SSM_INNER // LANE))
    g_y2, g_z, g_norm_w, _, g_d_skip = _gate_norm_bwd(g_y4, y_ssd, xa, pa, dskip_x, ssm_norm_w)
    g_xs, g_bm, g_cm, g_dtx, ga_rows = _ssd_bwd(xa, dtx, csx, cst_g, alog_x, g_y2, states, dskip_x)
    g_dt_raw, g_dt_bias, g_a_log = _dt_bwd(g_dtx, ga_rows, pb, dt_bias_pad)
    g_xbc, g_conv_b, gcw0, gcw1, gcw2, gcw3 = _conv_bwd(g_xs, g_bm, g_cm, xc, pa, conv_w)
    delta = _att_delta(g_att, att)
    dqs, dks, dvs = [], [], []
    for _, dil in DILATED_PATTERNS:
        dq, dk, dv = _att_bwd(pb, g_att, lse, delta, dil, slopes)
        dqs.append(dq)
        dks.append(dk)
        dvs.append(dv)
    g_qkv = _att_grad_sum(dqs, dks, dvs)
    g_proj = jnp.concatenate([g_z, g_gl, g_xbc, g_qkv, g_dt_raw, jnp.zeros((t, PROJ_W - OFF_DT - LANE), BF16)], axis=1)
    gw_in_r = _matmul(u, g_proj, mode="tn", out_dtype=BF16, name="in_proj_bwd_w", tm=D_MODEL, tn=1792, tk=2048)
    token = on_in_proj_grads(gw_in_r, jnp.concatenate([gcw0, gcw1, gcw2, gcw3], axis=0))
    g_u = _matmul(g_proj, w_in_r, mode="nt", out_dtype=F32, name="in_proj_bwd_x", tm=1024, tn=D_MODEL, tk=3584, after=token)
    g_x, gw_pre = _pre_norm_bwd(g_h1, g_u, x, w_pre)

    grads = dict(
        norm_mix_pre_w=gw_pre, b_gate=g_b_gate, conv_b=g_conv_b, dt_bias=g_dt_bias[:, :SSM_HEADS], a_log=g_a_log[:, :SSM_HEADS],
        d_skip=g_d_skip[:, :SSM_HEADS], ssm_norm_w=g_norm_w, norm_mix_post_w=gw_post, norm_ffn_pre_w=gw_fpre, norm_ffn_post_w=gw_fpost)
    return loss, g_x, grads


def _mesh_pos():
    return lax.axis_index("x"), lax.axis_index("y"), lax.axis_index("c")


def _all_gather(shards):
    n = len(shards)

    def body(*refs):
        x_refs, o_refs = refs[:n], refs[n:2 * n]
        send_sems, recv_sems, local_sems = refs[2 * n:]
        x, y, c = _mesh_pos()
        me, sibling = (x, y, c), (x, y, 1 - c)
        chips = [(1 - x, y), (x, 1 - y), (1 - x, 1 - y)]

        def copy(a, k, block, to, src=None):
            dst = o_refs[a].at[4 * block[0] + 2 * block[1] + block[2]]
            return pltpu.make_async_remote_copy(
                src_ref=dst if src is None else src, dst_ref=dst, send_sem=send_sems.at[7 * a + k], recv_sem=recv_sems.at[7 * a + k],
                device_id=to, device_id_type=pl.DeviceIdType.MESH)

        mine = [pltpu.make_async_copy(x_refs[a], o_refs[a].at[4 * x + 2 * y + c], local_sems.at[a]) for a in range(n)]
        for cp in mine:
            cp.start()
        first = []
        for a in range(n):
            first.append(copy(a, 0, me, sibling, src=x_refs[a]))
            first += [copy(a, 1 + j, me, (*chip, c), src=x_refs[a]) for j, chip in enumerate(chips)]
        for cp in first:
            cp.start()
        passed = []
        for j, chip in enumerate(chips):
            for a in range(n):
                copy(a, 1 + j, (*chip, c), me).wait_recv()
                passed.append(copy(a, 4 + j, (*chip, c), sibling))
                passed[-1].start()
        for a in range(n):
            copy(a, 0, sibling, me).wait_recv()
            for j, chip in enumerate(chips):
                copy(a, 4 + j, (*chip, 1 - c), me).wait_recv()
        for cp in first + passed:
            cp.wait_send()
        for cp in mine:
            cp.wait()

    hbm = pl.BlockSpec(memory_space=pltpu.HBM)
    return pl.pallas_call(
        body, out_shape=[S((N_DEV,) + s.shape, s.dtype) for s in shards],
        in_specs=[hbm] * n, out_specs=[hbm] * n,
        scratch_shapes=[pltpu.SemaphoreType.DMA((7 * n,)), pltpu.SemaphoreType.DMA((7 * n,)), pltpu.SemaphoreType.DMA((n,))],
        name="weights_all_gather",
    )(*shards)


def _exchange_grads(slab_arrays, small):
    n = len(slab_arrays)
    r_small = small.shape[0]

    def body(*refs):
        slab_refs, small_ref = refs[:n], refs[n]
        recv_refs, gsm_ref = refs[n + 1:2 * n + 1], refs[2 * n + 1]
        send_sems, recv_sems, local_sems = refs[2 * n + 2:]
        x, y, c = _mesh_pos()
        me = 4 * x + 2 * y + c

        def peer(k):
            px = 1 - x if k & 4 else x
            py = 1 - y if k & 2 else y
            pc = 1 - c if k & 1 else c
            return (px, py, pc), 4 * px + 2 * py + pc

        def copy(a, k, sending):
            to, lin = peer(k)
            sem = 7 * a + k - 1
            if a == n:
                src, dst = small_ref, gsm_ref.at[me if sending else lin]
            else:
                src, dst = slab_refs[a].at[lin], recv_refs[a].at[me if sending else lin]
            return pltpu.make_async_remote_copy(src_ref=src, dst_ref=dst, send_sem=send_sems.at[sem], recv_sem=recv_sems.at[sem],
                                                device_id=to, device_id_type=pl.DeviceIdType.MESH)

        own = [pltpu.make_async_copy(slab_refs[a].at[me], recv_refs[a].at[me], local_sems.at[a]) for a in range(n)]
        own.append(pltpu.make_async_copy(small_ref, gsm_ref.at[me], local_sems.at[n]))
        for cp in own:
            cp.start()
        order = [n] + list(range(n))
        sends = [copy(a, k, True) for a in order for k in range(1, N_DEV)]
        for cp in sends:
            cp.start()
        for a in order:
            for k in range(1, N_DEV):
                copy(a, k, False).wait_recv()
        for cp in sends:
            cp.wait_send()
        for cp in own:
            cp.wait()

    hbm = pl.BlockSpec(memory_space=pltpu.HBM)
    n_sem = 7 * (n + 1)
    res = pl.pallas_call(
        body, out_shape=[S(a.shape, a.dtype) for a in slab_arrays] + [S((N_DEV, r_small, LANE), small.dtype)],
        in_specs=[hbm] * (n + 1), out_specs=[hbm] * (n + 1),
        scratch_shapes=[pltpu.SemaphoreType.DMA((n_sem,)), pltpu.SemaphoreType.DMA((n_sem,)), pltpu.SemaphoreType.DMA((n + 1,))],
        name="grad_exchange",
    )(*slab_arrays, small)
    return res[:n], res[n]


def _peer_of(k, x, y, c):
    px = 1 - x if k & 4 else x
    py = 1 - y if k & 2 else y
    pc = 1 - c if k & 1 else c
    return (px, py, pc), 4 * px + 2 * py + pc


def _split_copies(src_refs, land_refs, send_sems, recv_sems, per_peer):
    x, y, c = _mesh_pos()
    me = 4 * x + 2 * y + c
    sends, recvs = [], []
    for a, (src, land) in enumerate(zip(src_refs, land_refs)):
        for k in range(1, N_DEV):
            to, lin = _peer_of(k, x, y, c)
            sem = 7 * a + k - 1
            piece = src.at[lin] if per_peer else src
            for slot, out in ((me, sends), (lin, recvs)):
                out.append(pltpu.make_async_remote_copy(
                    src_ref=piece, dst_ref=land.at[slot], send_sem=send_sems.at[sem], recv_sem=recv_sems.at[sem],
                    device_id=to, device_id_type=pl.DeviceIdType.MESH))
    return sends, recvs


def _remote_start(srcs, per_peer, name):
    n = len(srcs)
    lands = [lax.empty((N_DEV,) + (s.shape[1:] if per_peer else s.shape), s.dtype) for s in srcs]

    def body(*refs):
        src_refs, land_refs = refs[:n], refs[n:2 * n]
        send_sems, recv_sems = refs[2 * n], refs[2 * n + 1]
        token = refs[-1]
        sends, _ = _split_copies(src_refs, land_refs, send_sems, recv_sems, per_peer)
        for cp in sends:
            cp.start()
        token[...] = jnp.zeros_like(token)

    hbm = pl.BlockSpec(memory_space=pltpu.HBM)
    sem = pl.BlockSpec(memory_space=pltpu.SEMAPHORE)
    res = pl.pallas_call(
        body, name=name,
        out_shape=(pltpu.SemaphoreType.DMA((7 * n,)), pltpu.SemaphoreType.DMA((7 * n,)),
                   *[pltpu.HBM(a.shape, a.dtype) for a in srcs + lands], S((8, LANE), F32)),
        in_specs=[hbm] * (2 * n), out_specs=(sem, sem, *[hbm] * (2 * n), pl.BlockSpec(memory_space=pltpu.VMEM)),
        input_output_aliases={i: 2 + i for i in range(2 * n)},
        compiler_params=pltpu.CompilerParams(has_side_effects=pltpu.SideEffectType.DATAFLOW_SIDE_EFFECTING),
    )(*[pltpu.with_memory_space_constraint(a, pltpu.HBM) for a in srcs + lands])
    return dict(sems=res[:2], srcs=list(res[2:2 + n]), lands=list(res[2 + n:2 + 2 * n]), per_peer=per_peer), res[-1]


def _remote_wait(handle, after, name):
    n = len(handle["srcs"])
    per_peer = handle["per_peer"]

    def body(*refs):
        src_refs, land_refs = refs[:n], refs[n:2 * n]
        send_sems, recv_sems = refs[2 * n], refs[2 * n + 1]
        sends, recvs = _split_copies(src_refs, land_refs, send_sems, recv_sems, per_peer)
        for cp in sends:
            cp.wait_send()
        for cp in recvs:
            cp.wait_recv()

    hbm = pl.BlockSpec(memory_space=pltpu.HBM)
    sem = pl.BlockSpec(memory_space=pltpu.SEMAPHORE)
    arrays = handle["srcs"] + handle["lands"]
    res = pl.pallas_call(
        body, name=name, out_shape=tuple(pltpu.HBM(a.shape, a.dtype) for a in arrays),
        in_specs=[hbm] * (2 * n) + [sem, sem, pl.BlockSpec(memory_space=pl.ANY)], out_specs=tuple([hbm] * (2 * n)),
        input_output_aliases={i: i for i in range(2 * n)},
        compiler_params=pltpu.CompilerParams(has_side_effects=pltpu.SideEffectType.DATAFLOW_SIDE_EFFECTING),
    )(*arrays, *handle["sems"], after)
    return list(res[n:])


def _with_own(lands, own, me):
    return [lax.dynamic_update_index_in_dim(land, o.astype(land.dtype), me, 0) for land, o in zip(lands, own)]


def _adamw(w, m, v, slabs, name, tr):
    r, cols = w.shape
    c1 = 1.0 - ADAM_B1 ** ADAM_STEP
    c2 = 1.0 - ADAM_B2 ** ADAM_STEP

    def body(w_ref, m_ref, v_ref, s_ref, g_ref, d_ref, nm_ref, nv_ref):
        g = s_ref[0].astype(F32)
        for d in range(1, N_DEV):
            g = g + s_ref[d].astype(F32)
        nm = ADAM_B1 * m_ref[...] + (1.0 - ADAM_B1) * g
        nv = ADAM_B2 * v_ref[...] + (1.0 - ADAM_B2) * (g * g)
        g_ref[...] = g
        nm_ref[...] = nm
        nv_ref[...] = nv
        d_ref[...] = -ADAM_LR * ((nm / c1) / (jnp.sqrt(nv / c2) + ADAM_EPS) + ADAM_WD * w_ref[...])

    assert r % tr == 0, name
    blk = pl.BlockSpec((tr, cols), lambda i: (i, 0))
    return pl.pallas_call(
        body, grid=(r // tr,), in_specs=[blk, blk, blk, pl.BlockSpec((N_DEV, tr, cols), lambda i: (0, i, 0))],
        out_specs=[blk] * 4, out_shape=[S((r, cols), F32)] * 4, name=name, compiler_params=_params(("parallel",)),
    )(w, m, v, slabs)


BIG = ("w_in", "w_att_proj", "w_up", "w_ssm_proj", "w_out", "w_down", "conv_w")
ADAMW_ROWS = dict(w_in=256, w_att_proj=768, w_up=512, w_ssm_proj=256, w_out=128, w_down=256, conv_w=4)
SMALL = ("norm_mix_pre_w", "b_gate", "conv_b", "dt_bias", "a_log", "d_skip", "ssm_norm_w", "norm_mix_post_w",
         "norm_ffn_pre_w", "norm_ffn_post_w")
ORDER = ("norm_mix_pre_w", "w_in", "b_gate", "conv_w", "conv_b", "dt_bias", "a_log", "d_skip", "ssm_norm_w", "w_att_proj",
         "w_ssm_proj", "w_out", "norm_mix_post_w", "norm_ffn_pre_w", "w_up", "w_down", "norm_ffn_post_w")
ROW_SHARDED = ("w_ssm_proj", "w_out", "w_down")
LATE = ("w_att_proj", "w_ssm_proj", "w_out", "w_up", "w_down")
IN_PROJ_W = 10528
IN_SHARD_W = IN_PROJ_W // N_DEV
IN_SEGMENTS = ((2304, 4352), (8480, 10528), (4352, 8448), (0, 2304), (8448, 8480))


def _pack(parts, rows_multiple):
    flat = jnp.concatenate([p.reshape(-1) for p in parts])
    pad = (-flat.shape[0]) % (rows_multiple * LANE)
    return jnp.pad(flat, (0, pad)).reshape(-1, LANE)


def _unpack(flat2d, shapes):
    flat, out, off = flat2d.reshape(-1), [], 0
    for sh in shapes:
        n = int(np.prod(sh))
        out.append(flat[off:off + n].reshape(sh))
        off += n
    return out


def _reorder_in_proj(w):
    qkv, z, xbc = w[:, :2304], w[:, 2304:4352], w[:, 4352:8448]
    dt, gate = w[:, 8448:8480], w[:, 8480:10528]
    return jnp.concatenate([z, gate, xbc, qkv, dt, jnp.zeros((w.shape[0], PROJ_W - 10528), w.dtype)], axis=1)


def _restore_in_proj(wr):
    return jnp.concatenate([wr[:, OFF_QKV:OFF_QKV + 2304], wr[:, OFF_Z:OFF_Z + 2048], wr[:, OFF_XBC:OFF_XBC + 4096],
                            wr[:, OFF_DT:OFF_DT + 32], wr[:, OFF_GL:OFF_GL + 2048]], axis=1)


def _assemble_in_proj(g):
    pieces = []
    for lo, hi in IN_SEGMENTS:
        while lo < hi:
            d = lo // IN_SHARD_W
            end = min(hi, (d + 1) * IN_SHARD_W)
            pieces.append(g[d][:, lo - d * IN_SHARD_W:end - d * IN_SHARD_W])
            lo = end
    pieces.append(jnp.zeros((g.shape[1], PROJ_W - IN_PROJ_W), g.dtype))
    return jnp.concatenate(pieces, axis=1)


def _in_proj_slabs(wr):
    orig = _restore_in_proj(wr)
    return jnp.stack([orig[:, d * IN_SHARD_W:(d + 1) * IN_SHARD_W] for d in range(N_DEV)])


def kernel(x, norm_mix_pre_w, w_in, b_gate, conv_w, conv_b, dt_bias, a_log, d_skip, ssm_norm_w, w_att_proj, w_ssm_proj, w_out, norm_mix_post_w, norm_ffn_pre_w, w_up, w_down, norm_ffn_post_w, loss_target, m_norm_mix_pre_w, m_w_in, m_b_gate, m_conv_w, m_conv_b, m_dt_bias, m_a_log, m_d_skip, m_ssm_norm_w, m_w_att_proj, m_w_ssm_proj, m_w_out, m_norm_mix_post_w, m_norm_ffn_pre_w, m_w_up, m_w_down, m_norm_ffn_post_w, v_norm_mix_pre_w, v_w_in, v_b_gate, v_conv_w, v_conv_b, v_dt_bias, v_a_log, v_d_skip, v_ssm_norm_w, v_w_att_proj, v_w_ssm_proj, v_w_out, v_norm_mix_post_w, v_norm_ffn_pre_w, v_w_up, v_w_down, v_norm_ffn_post_w):
    w = dict(norm_mix_pre_w=norm_mix_pre_w, w_in=w_in, b_gate=b_gate, conv_w=conv_w, conv_b=conv_b, dt_bias=dt_bias, a_log=a_log,
             d_skip=d_skip, ssm_norm_w=ssm_norm_w, w_att_proj=w_att_proj, w_ssm_proj=w_ssm_proj, w_out=w_out,
             norm_mix_post_w=norm_mix_post_w, norm_ffn_pre_w=norm_ffn_pre_w, w_up=w_up, w_down=w_down, norm_ffn_post_w=norm_ffn_post_w)
    m = dict(norm_mix_pre_w=m_norm_mix_pre_w, w_in=m_w_in, b_gate=m_b_gate, conv_w=m_conv_w, conv_b=m_conv_b, dt_bias=m_dt_bias,
             a_log=m_a_log, d_skip=m_d_skip, ssm_norm_w=m_ssm_norm_w, w_att_proj=m_w_att_proj, w_ssm_proj=m_w_ssm_proj, w_out=m_w_out,
             norm_mix_post_w=m_norm_mix_post_w, norm_ffn_pre_w=m_norm_ffn_pre_w, w_up=m_w_up, w_down=m_w_down, norm_ffn_post_w=m_norm_ffn_post_w)
    v = dict(norm_mix_pre_w=v_norm_mix_pre_w, w_in=v_w_in, b_gate=v_b_gate, conv_w=v_conv_w, conv_b=v_conv_b, dt_bias=v_dt_bias,
             a_log=v_a_log, d_skip=v_d_skip, ssm_norm_w=v_ssm_norm_w, w_att_proj=v_w_att_proj, w_ssm_proj=v_w_ssm_proj, w_out=v_w_out,
             norm_mix_post_w=v_norm_mix_post_w, norm_ffn_pre_w=v_norm_ffn_pre_w, w_up=v_w_up, w_down=v_w_down, norm_ffn_post_w=v_norm_ffn_post_w)
    shard_shapes = {n: w[n].shape[1:] for n in ORDER}

    mx, my, mc = _mesh_pos()
    me = 4 * mx + 2 * my + mc

    g_in, g_conv = _all_gather([w["w_in"][0].astype(BF16), w["conv_w"][0]])
    conv_full = jnp.moveaxis(g_conv, 0, 1).reshape(SSM_CONV, CONV_DIM)
    late_shards = [w[n][0].astype(BF16) for n in LATE]
    late_handle, token = _remote_start(late_shards, False, "late_weights_start")
    w_pre = w["norm_mix_pre_w"] + jnp.tile(token[0:1, :], (1, D_MODEL // LANE))

    def late_weights(after):
        full = dict(zip(LATE, _with_own(_remote_wait(late_handle, after, "late_weights_wait"), late_shards, me)))
        for n in ROW_SHARDED:
            full[n] = full[n].reshape(-1, full[n].shape[2])
        w_att = jnp.moveaxis(full["w_att_proj"], 0, 1).reshape(ATT_WIDTH, D_MODEL)
        return w_att, full["w_ssm_proj"], full["w_out"], full["w_up"], full["w_down"]

    started = {}

    def start_exchange(tag, slabs):
        own = [lax.dynamic_index_in_dim(s, me, 0, keepdims=False) for s in slabs]
        handle, tok = _remote_start(slabs, True, tag + "_grads_start")
        started[tag] = (handle, own)
        return tok

    def on_mid_grads(g):
        slabs = dict(w_up=g["w_up"], w_att_proj=jnp.moveaxis(g["w_att_proj"].reshape(ATT_WIDTH, N_DEV, -1), 1, 0))
        for n in ROW_SHARDED:
            slabs[n] = g[n].reshape(N_DEV, -1, g[n].shape[1])
        return start_exchange("mid", [slabs[n] for n in LATE])

    def on_in_proj_grads(gw_in_r, g_conv_w):
        return start_exchange("in_proj", [_in_proj_slabs(gw_in_r), jnp.moveaxis(g_conv_w.reshape(SSM_CONV, N_DEV, -1), 1, 0)])

    loss, g_x, grads = _local_step(
        x[0], loss_target[0], w_pre, _assemble_in_proj(g_in), w["b_gate"], conv_full, w["conv_b"], w["dt_bias"], w["a_log"],
        w["d_skip"], w["ssm_norm_w"], late_weights, w["norm_mix_post_w"], w["norm_ffn_pre_w"], w["norm_ffn_post_w"],
        on_mid_grads, on_in_proj_grads)

    recv = {}
    for tag, names in (("mid", LATE), ("in_proj", ("w_in", "conv_w"))):
        handle, own = started[tag]
        recv.update(zip(names, _with_own(_remote_wait(handle, g_x, tag + "_grads_wait"), own, me)))
    small = _pack([grads[n].astype(F32) for n in SMALL], 8)
    _, small_all = _exchange_grads([], small)

    small_shapes = [shard_shapes[n] for n in SMALL]
    small_out = _adamw(*[_pack([d_[n][0] for n in SMALL], 8) for d_ in (w, m, v)], small_all, "adamw_replicated", small_all.shape[1])
    big_out = {n: _adamw(w[n][0], m[n][0], v[n][0], recv[n], "adamw_" + n, ADAMW_ROWS[n]) for n in BIG}
    res = []
    for which, small_flat in enumerate(small_out):
        vals = {n: big_out[n][which] for n in BIG}
        vals.update(zip(SMALL, _unpack(small_flat, small_shapes)))
        res.append([vals[n][None] for n in ORDER])
    g_out, d_out, m_out, v_out = res
    total = lax.psum(loss[0, 0], ("x", "y", "c"))
    return (total, g_x[None], *g_out, *d_out, *m_out, *v_out)
```

```python
import functools
import math

import jax
import jax.numpy as jnp
import numpy as np
from jax import lax
from jax.experimental import pallas as pl
from jax.experimental.pallas import tpu as pltpu

F32 = jnp.float32
BF16 = jnp.bfloat16

D_MODEL = 1024
HEAD_DIM = 64
N_ATT_HEADS = 12
ATT_WIDTH = N_ATT_HEADS * HEAD_DIM
DILATED_PATTERNS = ((128, 1), (512, 4), (2048, 16))
ATT_BLOCK = 128
SSM_INNER = 2048
SSM_HEAD_DIM = 64
SSM_HEADS = 32
SSM_GROUPS = 8
SSM_STATE = 128
SSM_CHUNK = 128
CONV_DIM = 4096
SSM_CONV = 4
FFN_HIDDEN = 4096
RMS_EPS = 1e-6
N_DEV = 8

ADAM_LR = 0.001
ADAM_B1 = 0.9
ADAM_B2 = 0.999
ADAM_EPS = 1e-08
ADAM_WD = 0.01
ADAM_STEP = 10

LANE = 128
OFF_Z, OFF_GL, OFF_XBC, OFF_QKV, OFF_DT = 0, 2048, 4096, 8192, 10496
PROJ_W = 10752
PROJ_BLOCKS = PROJ_W // LANE
PA_W = OFF_QKV
PB_W = PROJ_W - OFF_QKV
PB_DT = OFF_DT - OFF_QKV
VMEM_LIMIT = 52 * 1024 * 1024
NEG = -1e30

HI = lax.Precision.HIGHEST
NT_DIMS = (((1,), (1,)), ((), ()))
TN_DIMS = (((0,), (0,)), ((), ()))
S = jax.ShapeDtypeStruct


def _params(sem):
    return pltpu.CompilerParams(dimension_semantics=sem, vmem_limit_bytes=VMEM_LIMIT)


def _matmul(a, b, *, mode, out_dtype, name, tm, tn, tk, epilogue=None, extra=None, stacked=False, after=None, b_cols=None):
    if mode == "nn":
        m, k = a.shape
        n = b.shape[0] * b.shape[2] if stacked else b.shape[1]
        col0 = 0
        if b_cols is not None:
            assert b_cols[0] % tn == 0, name
            col0, n = b_cols[0] // tn, b_cols[1]
        a_spec = pl.BlockSpec((tm, tk), lambda i, j, kk: (i, kk))
        b_spec = pl.BlockSpec((None, tk, tn), lambda i, j, kk: (j, kk, 0)) if stacked else pl.BlockSpec((tk, tn), lambda i, j, kk: (kk, col0 + j))
        dims = (((1,), (0,)), ((), ()))
    elif mode == "nt":
        m, k = a.shape
        n = b.shape[1] if stacked else b.shape[0]
        a_spec = pl.BlockSpec((tm, tk), lambda i, j, kk: (i, kk))
        b_spec = pl.BlockSpec((None, tn, tk), lambda i, j, kk: (kk, j, 0)) if stacked else pl.BlockSpec((tn, tk), lambda i, j, kk: (j, kk))
        dims = NT_DIMS
    else:
        (k, m), n = a.shape, b.shape[1]
        a_spec = pl.BlockSpec((tk, tm), lambda i, j, kk: (kk, i))
        b_spec = pl.BlockSpec((tk, tn), lambda i, j, kk: (kk, j))
        dims = TN_DIMS
    assert m % tm == 0 and n % tn == 0 and k % tk == 0, (name, m, n, k)
    if stacked:
        assert (tk if mode == "nt" else tn) * N_DEV == (k if mode == "nt" else n), name
    nk = k // tk
    o_spec = pl.BlockSpec((tm, tn), lambda i, j, kk: (i, j))
    in_specs, args = [a_spec, b_spec], [a, b]
    if epilogue == "relu2":
        out_shape = (S((m, n), BF16), S((m, n), BF16))
        out_specs = (o_spec, o_spec)
    elif stacked and mode == "tn":
        out_shape, out_specs = S((N_DEV, m, tn), out_dtype), pl.BlockSpec((None, tm, tn), lambda i, j, kk: (j, i, 0))
    else:
        out_shape, out_specs = S((m, n), out_dtype), o_spec
    if epilogue == "relu2_bwd":
        in_specs.append(o_spec)
        args.append(extra)
    n_in = len(args)
    if after is not None:
        in_specs.append(pl.BlockSpec(after.shape, lambda i, j, kk: (0,) * after.ndim))
        args.append(after)

    def finish(acc, refs):
        if epilogue == "relu2":
            r = jnp.maximum(acc, 0.0)
            refs[0][...] = (r * r).astype(BF16)
            refs[1][...] = acc.astype(BF16)
        elif epilogue == "relu2_bwd":
            up = refs[0][...].astype(F32)
            refs[1][...] = (acc * (2.0 * jnp.maximum(up, 0.0))).astype(out_dtype)
        else:
            refs[0][...] = acc.astype(out_dtype)

    def body(a_ref, b_ref, *rest):
        rest = rest[:n_in - 2] + rest[len(args) - 2:]
        part = lax.dot_general(a_ref[...].astype(BF16), b_ref[...].astype(BF16), dims, preferred_element_type=F32)
        if nk == 1:
            finish(part, rest)
            return
        acc_ref = rest[-1]
        kk = pl.program_id(2)

        @pl.when(kk == 0)
        def _():
            acc_ref[...] = part

        @pl.when(kk > 0)
        def _():
            acc_ref[...] += part

        @pl.when(kk == nk - 1)
        def _():
            finish(acc_ref[...], rest[:-1])

    scratch = [] if nk == 1 else [pltpu.VMEM((tm, tn), F32)]
    return pl.pallas_call(
        body, grid=(m // tm, n // tn, nk), in_specs=in_specs, out_specs=out_specs, out_shape=out_shape,
        scratch_shapes=scratch, name=name, compiler_params=_params(("parallel", "parallel", "arbitrary")),
    )(*args)


def _rowcall(body, name, n_rows, tr, ins, outs, scratch=()):
    res = pl.pallas_call(
        body, grid=(n_rows // tr,),
        in_specs=[pl.BlockSpec(bs, im) for _, bs, im in ins],
        out_specs=[pl.BlockSpec(bs, im) for _, _, bs, im in outs],
        out_shape=[S(sh, dt) for sh, dt, _, _ in outs],
        scratch_shapes=list(scratch), name=name, compiler_params=_params(("arbitrary",)),
    )(*[a for a, _, _ in ins])
    return res


def _rows(arr, tr, width=None, cb=0):
    width = arr.shape[1] if width is None else width
    return (arr, (tr, width), lambda i, cb=cb: (i, cb))


def _whole(arr):
    nd = arr.ndim
    return (arr, arr.shape, lambda i, nd=nd: (0,) * nd)


def _orow(n_rows, width, dtype, tr):
    return ((n_rows, width), dtype, (tr, width), lambda i: (i, 0))


def _oacc(width):
    return ((1, width), F32, (1, width), lambda i: (0, 0))


def _accumulate(ref, value):
    first = pl.program_id(0) == 0

    @pl.when(first)
    def _():
        ref[...] = value

    @pl.when(jnp.logical_not(first))
    def _():
        ref[...] += value


def _colsum(v):
    return jnp.sum(v, axis=0, keepdims=True)


def _rms_fwd(x, w):
    r = lax.rsqrt(jnp.mean(x * x, axis=-1, keepdims=True) + RMS_EPS)
    return x * r * w


def _rms_bwd(gy, x, w):
    r = lax.rsqrt(jnp.mean(x * x, axis=-1, keepdims=True) + RMS_EPS)
    xn = x * r
    gxn = gy * w
    gx = r * (gxn - xn * jnp.mean(gxn * xn, axis=-1, keepdims=True))
    return gx, _colsum(gy * xn)


def _sigmoid(x):
    return 1.0 / (1.0 + jnp.exp(-x))


def _head_expand(n_heads_pad, n_heads, width):
    h = lax.broadcasted_iota(jnp.int32, (n_heads_pad, n_heads * width), 0)
    c = lax.broadcasted_iota(jnp.int32, (n_heads_pad, n_heads * width), 1)
    return (c // width == h).astype(F32)


def _head_reduce(n_heads, width, n_heads_pad):
    c = lax.broadcasted_iota(jnp.int32, (n_heads * width, n_heads_pad), 0)
    h = lax.broadcasted_iota(jnp.int32, (n_heads * width, n_heads_pad), 1)
    return (c // width == h).astype(F32)


def _block_ones(n, width):
    r = lax.broadcasted_iota(jnp.int32, (n, n), 0)
    c = lax.broadcasted_iota(jnp.int32, (n, n), 1)
    return (r // width == c // width).astype(F32)


def _pre_norm(x, w_pre, tr=512):
    t = x.shape[0]

    def body(x_ref, w_ref, u_ref):
        u_ref[...] = _rms_fwd(x_ref[...], w_ref[...]).astype(BF16)

    return _rowcall(body, "pre_norm", t, tr, [_rows(x, tr), _whole(w_pre)], [_orow(t, D_MODEL, BF16, tr)])[0]


CONV_HALO = 16


def _row_shift(cur, halo, j):
    tr = cur.shape[0]
    r = lax.broadcasted_iota(jnp.int32, (tr, tr), 0)
    c = lax.broadcasted_iota(jnp.int32, (tr, tr), 1)
    main = jnp.dot((c == r + j).astype(BF16), cur, preferred_element_type=F32)
    er = lax.broadcasted_iota(jnp.int32, (CONV_HALO, CONV_HALO), 0)
    ec = lax.broadcasted_iota(jnp.int32, (CONV_HALO, CONV_HALO), 1)
    if j < 0:
        edge = jnp.dot((ec == CONV_HALO + er + j).astype(BF16), halo, preferred_element_type=F32)
        return jnp.concatenate([main[:CONV_HALO] + edge, main[CONV_HALO:]], axis=0)
    edge = jnp.dot((ec == er + j - CONV_HALO).astype(BF16), halo, preferred_element_type=F32)
    return jnp.concatenate([main[:tr - CONV_HALO], main[tr - CONV_HALO:] + edge], axis=0)


def _conv_fwd(proj, conv_w, conv_b, tr=256):
    t = proj.shape[0]
    cb = OFF_XBC // CONV_DIM
    halo = (proj, (CONV_HALO, CONV_DIM), lambda i: (jnp.maximum(i * (tr // CONV_HALO) - 1, 0), cb))

    def body(cur_ref, prev_ref, w_ref, b_ref, o_ref, xc_ref):
        cur = cur_ref[...]
        prev = jnp.where(pl.program_id(0) > 0, prev_ref[...], jnp.zeros_like(prev_ref[...]))
        acc = b_ref[...] + w_ref[3:4, :] * cur.astype(F32)
        for k in range(SSM_CONV - 1):
            acc = acc + w_ref[k:k + 1, :] * _row_shift(cur, prev, -(SSM_CONV - 1 - k))
        o_ref[...] = acc * _sigmoid(acc)
        xc_ref[...] = acc.astype(BF16)

    return _rowcall(body, "conv_fwd", t, tr, [_rows(proj, tr, CONV_DIM, cb), halo, _whole(conv_w), _whole(conv_b)],
                    [_orow(t, CONV_DIM, F32, tr), _orow(t, CONV_DIM, BF16, tr)])


def _dt_fwd(proj, dt_bias_pad, alog_pad, tr=512):
    t = proj.shape[0]

    def body(raw_ref, b_ref, al_ref, dtx_ref, csx_ref, cst_ref):
        v = raw_ref[...] + b_ref[...]
        dt = jnp.maximum(v, 0.0) + jnp.log1p(jnp.exp(-jnp.abs(v)))
        expand = _head_expand(LANE, SSM_HEADS, SSM_HEAD_DIM)
        dtx_ref[...] = _dot_split(dt, expand, 0, 3)
        la = dt * (-jnp.exp(al_ref[...]))
        row = lax.broadcasted_iota(jnp.int32, (SSM_CHUNK, SSM_CHUNK), 0)
        col = lax.broadcasted_iota(jnp.int32, (SSM_CHUNK, SSM_CHUNK), 1)
        tril = (col <= row).astype(F32)
        cs = jnp.concatenate([_dot_split(tril, la[k * SSM_CHUNK:(k + 1) * SSM_CHUNK, :], 1, 3) for k in range(tr // SSM_CHUNK)], axis=0)
        csx_ref[...] = _dot_split(cs, expand, 0, 3)
        cst_ref[...] = cs.T

    return _rowcall(body, "dt_fwd", t, tr, [_rows(proj, tr, LANE, PB_DT // LANE), _whole(dt_bias_pad), _whole(alog_pad)],
                    [_orow(t, SSM_INNER, F32, tr), _orow(t, SSM_INNER, F32, tr), ((LANE, t), F32, (LANE, tr), lambda i: (0, i))])


def _gate_norm_fwd(y_ssd, xa, proj, dskip_x, norm_w, tr=256):
    t = y_ssd.shape[0]
    gw = SSM_INNER // SSM_GROUPS

    def body(y_ref, xs_ref, z_ref, d_ref, w_ref, o_ref):
        z = z_ref[...].astype(F32)
        y3 = (y_ref[...].astype(F32) + d_ref[...] * xs_ref[...]) * (z * _sigmoid(z))
        for g in range(SSM_GROUPS):
            sl = slice(g * gw, (g + 1) * gw)
            o_ref[:, sl] = _rms_fwd(y3[:, sl], w_ref[:, sl]).astype(BF16)

    return _rowcall(body, "gate_norm_fwd", t, tr,
                    [_rows(y_ssd, tr), _rows(xa, tr, SSM_INNER, 0), _rows(proj, tr, SSM_INNER, OFF_Z // SSM_INNER), _whole(dskip_x), _whole(norm_w)],
                    [_orow(t, SSM_INNER, BF16, tr)])[0]


def _gating_fwd(proj, b_gate, att_p, ssm_p, tr=512):
    t = proj.shape[0]

    def body(gl_ref, b_ref, a_ref, s_ref, o_ref):
        gates = _sigmoid(gl_ref[...].astype(F32) + b_ref[...])
        o_ref[...] = (gates[:, :D_MODEL] * a_ref[...].astype(F32) + gates[:, D_MODEL:] * s_ref[...].astype(F32)).astype(BF16)

    return _rowcall(body, "gating_fwd", t, tr, [_rows(proj, tr, 2 * D_MODEL, OFF_GL // (2 * D_MODEL)), _whole(b_gate), _rows(att_p, tr), _rows(ssm_p, tr)],
                    [_orow(t, D_MODEL, BF16, tr)])[0]


def _mix_post_ffn_pre(x, mixed, w_post, w_fpre, tr=512):
    t = x.shape[0]

    def body(x_ref, m_ref, wp_ref, wf_ref, h1_ref, f_ref):
        h1 = x_ref[...] + _rms_fwd(m_ref[...], wp_ref[...])
        h1_ref[...] = h1
        f_ref[...] = _rms_fwd(h1, wf_ref[...]).astype(BF16)

    return _rowcall(body, "mix_post_ffn_pre", t, tr, [_rows(x, tr), _rows(mixed, tr), _whole(w_post), _whole(w_fpre)],
                    [_orow(t, D_MODEL, F32, tr), _orow(t, D_MODEL, BF16, tr)])


def _loss_and_ffn_post_bwd(h1, dn, w_fpost, target, tr=512):
    t = h1.shape[0]

    def body(h1_ref, dn_ref, w_ref, tg_ref, loss_ref, gh2_ref, gdn_ref, gw_ref):
        dn = dn_ref[...]
        w = w_ref[...]
        err = h1_ref[...] + _rms_fwd(dn, w) - tg_ref[...]
        _accumulate(loss_ref, jnp.zeros((1, LANE), F32) + 0.5 * jnp.sum(jnp.mean(err * err, axis=-1, keepdims=True)))
        gh2 = err * (1.0 / D_MODEL)
        gh2_ref[...] = gh2
        gdn, gw = _rms_bwd(gh2, dn, w)
        gdn_ref[...] = gdn.astype(BF16)
        _accumulate(gw_ref, gw)

    return _rowcall(body, "loss_ffn_post_bwd", t, tr, [_rows(h1, tr), _rows(dn, tr), _whole(w_fpost), _rows(target, tr)],
                    [_oacc(LANE), _orow(t, D_MODEL, F32, tr), _orow(t, D_MODEL, BF16, tr), _oacc(D_MODEL)])


def _ffn_pre_mix_post_bwd(g_h2, g_f, h1, w_fpre, mixed, w_post, tr=512):
    t = h1.shape[0]

    def body(gh2_ref, gf_ref, h1_ref, wf_ref, m_ref, wp_ref, gh1_ref, gm_ref, gwf_ref, gwp_ref):
        gx, gwf = _rms_bwd(gf_ref[...], h1_ref[...], wf_ref[...])
        gh1 = gh2_ref[...] + gx
        gh1_ref[...] = gh1
        gm, gwp = _rms_bwd(gh1, m_ref[...], wp_ref[...])
        gm_ref[...] = gm.astype(BF16)
        _accumulate(gwf_ref, gwf)
        _accumulate(gwp_ref, gwp)

    return _rowcall(body, "ffn_pre_mix_post_bwd", t, tr,
                    [_rows(g_h2, tr), _rows(g_f, tr), _rows(h1, tr), _whole(w_fpre), _rows(mixed, tr), _whole(w_post)],
                    [_orow(t, D_MODEL, F32, tr), _orow(t, D_MODEL, BF16, tr), _oacc(D_MODEL), _oacc(D_MODEL)])


def _gating_bwd(g_mixin, proj, b_gate, att_p, ssm_p, tr=512):
    t = proj.shape[0]

    def body(gm_ref, gl_ref, b_ref, a_ref, s_ref, ga_ref, gs_ref, ggl_ref, gb_ref):
        gates = _sigmoid(gl_ref[...].astype(F32) + b_ref[...])
        gm = gm_ref[...].astype(F32)
        g_att, g_ssm = gates[:, :D_MODEL], gates[:, D_MODEL:]
        ga_ref[...] = (gm * g_att).astype(BF16)
        gs_ref[...] = (gm * g_ssm).astype(BF16)
        ggl_a = gm * a_ref[...].astype(F32) * g_att * (1.0 - g_att)
        ggl_s = gm * s_ref[...].astype(F32) * g_ssm * (1.0 - g_ssm)
        ggl_ref[:, :D_MODEL] = ggl_a.astype(BF16)
        ggl_ref[:, D_MODEL:] = ggl_s.astype(BF16)
        _accumulate(gb_ref.at[:, :D_MODEL], _colsum(ggl_a))
        _accumulate(gb_ref.at[:, D_MODEL:], _colsum(ggl_s))

    return _rowcall(body, "gating_bwd", t, tr,
                    [_rows(g_mixin, tr), _rows(proj, tr, 2 * D_MODEL, OFF_GL // (2 * D_MODEL)), _whole(b_gate), _rows(att_p, tr), _rows(ssm_p, tr)],
                    [_orow(t, D_MODEL, BF16, tr), _orow(t, D_MODEL, BF16, tr), _orow(t, 2 * D_MODEL, BF16, tr), _oacc(2 * D_MODEL)])


def _gate_norm_bwd(g_y4, y_ssd, xa, proj, dskip_x, norm_w, tr=256):
    t = y_ssd.shape[0]
    gw = SSM_INNER // SSM_GROUPS

    def body(g_ref, y_ref, xs_ref, z_ref, d_ref, w_ref, gy2_ref, gz_ref, gnw_ref, gdx_ref, gd_ref):
        z = z_ref[...].astype(F32)
        xs = xs_ref[...]
        sg = _sigmoid(z)
        sz = z * sg
        y2 = y_ref[...].astype(F32) + d_ref[...] * xs
        y3 = y2 * sz
        g4 = g_ref[...].astype(F32)
        for g in range(SSM_GROUPS):
            sl = slice(g * gw, (g + 1) * gw)
            gy3, gnw = _rms_bwd(g4[:, sl], y3[:, sl], w_ref[:, sl])
            _accumulate(gnw_ref.at[:, sl], gnw)
            gy2 = gy3 * sz[:, sl]
            gy2_ref[:, sl] = gy2
            gz_ref[:, sl] = (gy3 * y2[:, sl] * (sg[:, sl] * (1.0 + z[:, sl] * (1.0 - sg[:, sl])))).astype(BF16)
            _accumulate(gdx_ref.at[:, sl], _colsum(gy2 * xs[:, sl]))
        tot = jnp.broadcast_to(gdx_ref[...], (8, SSM_INNER))
        gd_ref[...] = jnp.dot(tot, _head_reduce(SSM_HEADS, SSM_HEAD_DIM, LANE), precision=HI, preferred_element_type=F32)[0:1, :]

    return _rowcall(body, "gate_norm_bwd", t, tr,
                    [_rows(g_y4, tr), _rows(y_ssd, tr), _rows(xa, tr, SSM_INNER, 0), _rows(proj, tr, SSM_INNER, OFF_Z // SSM_INNER), _whole(dskip_x), _whole(norm_w)],
                    [_orow(t, SSM_INNER, F32, tr), _orow(t, SSM_INNER, BF16, tr), _oacc(SSM_INNER), _oacc(SSM_INNER), _oacc(LANE)])


def _dt_bwd(g_dtx, ga_rows, proj, dt_bias_pad, tr=512):
    t = proj.shape[0]

    def body(g_ref, ga_ref, raw_ref, b_ref, o_ref, gb_ref, gal_ref):
        red = _head_reduce(SSM_HEADS, SSM_HEAD_DIM, LANE)
        gdt = _dot_split(g_ref[...], red, 0, 3)
        graw = gdt * _sigmoid(raw_ref[...] + b_ref[...])
        o_ref[...] = graw.astype(BF16)
        _accumulate(gb_ref, _colsum(graw))
        tot = jnp.broadcast_to(_colsum(ga_ref[...]), (8, SSM_INNER))
        gal_ref[...] = jnp.dot(tot, red, precision=HI, preferred_element_type=F32)[0:1, :]

    return _rowcall(body, "dt_bwd", t, tr, [_rows(g_dtx, tr), _whole(ga_rows), _rows(proj, tr, LANE, PB_DT // LANE), _whole(dt_bias_pad)],
                    [_orow(t, LANE, BF16, tr), _oacc(LANE), _oacc(LANE)])


def _conv_bwd(g_xs, g_b, g_c, xc, proj, conv_w, tr=256):
    t = proj.shape[0]
    n_blk = t // tr
    cb = OFF_XBC // CONV_DIM
    nb, nc = SSM_INNER, SSM_INNER + SSM_GROUPS * SSM_STATE
    def nxt(arr, width):
        return (arr, (CONV_HALO, width), lambda i: (jnp.minimum((i + 1) * (tr // CONV_HALO), t // CONV_HALO - 1), 0))

    def body(gxs_ref, gxs_n, gb_ref, gb_n, gc_ref, gc_n, xc_ref, xc_n, x_ref, w_ref, o_ref, gcb_ref, gw0, gw1, gw2, gw3):
        def gxc_of(gxs, gb, gc, xc, keep):
            xcf = xc[...].astype(F32)
            sg = _sigmoid(xcf)
            dsilu = jnp.where(keep, sg * (1.0 + xcf * (1.0 - sg)), 0.0)
            return jnp.concatenate([gxs[...] * dsilu[:, :nb], gb[...] * dsilu[:, nb:nc], gc[...] * dsilu[:, nc:]], axis=1)

        gxc = gxc_of(gxs_ref, gb_ref, gc_ref, xc_ref, True)
        gxc16 = gxc.astype(BF16)
        nxt16 = gxc_of(gxs_n, gb_n, gc_n, xc_n, pl.program_id(0) < n_blk - 1).astype(BF16)
        x = x_ref[...].astype(F32)
        acc = w_ref[3:4, :] * gxc
        _accumulate(gw3, _colsum(gxc * x))
        _accumulate(gcb_ref, _colsum(gxc))
        for k, gw in enumerate((gw0, gw1, gw2)):
            shifted = _row_shift(gxc16, nxt16, SSM_CONV - 1 - k)
            acc = acc + w_ref[k:k + 1, :] * shifted
            _accumulate(gw, _colsum(shifted * x))
        o_ref[...] = acc.astype(BF16)

    ins = []
    for arr, width in ((g_xs, SSM_INNER), (g_b, nc - nb), (g_c, nc - nb), (xc, CONV_DIM)):
        ins += [_rows(arr, tr), nxt(arr, width)]
    ins += [_rows(proj, tr, CONV_DIM, cb), _whole(conv_w)]
    return _rowcall(body, "conv_bwd", t, tr, ins, [_orow(t, CONV_DIM, BF16, tr)] + [_oacc(CONV_DIM)] * 5)


def _pre_norm_bwd(g_h1, g_u, x, w_pre, tr=512):
    t = x.shape[0]

    def body(gh_ref, gu_ref, x_ref, w_ref, gx_ref, gw_ref):
        gx, gw = _rms_bwd(gu_ref[...], x_ref[...], w_ref[...])
        gx_ref[...] = gh_ref[...] + gx
        _accumulate(gw_ref, gw)

    return _rowcall(body, "pre_norm_bwd", t, tr, [_rows(g_h1, tr), _rows(g_u, tr), _rows(x, tr), _whole(w_pre)],
                    [_orow(t, D_MODEL, F32, tr), _oacc(D_MODEL)])


def _alibi_slopes(n):
    def pow2(m):
        start = 2.0 ** (-8.0 / m)
        return [start ** (i + 1) for i in range(m)]
    if (n & (n - 1)) == 0:
        s = pow2(n)
    else:
        c = 2 ** int(math.floor(math.log2(n)))
        s = pow2(c) + pow2(2 * c)[0::2][: n - c]
    return np.array(s, dtype=np.float32)


def _slope_rows():
    s = _alibi_slopes(N_ATT_HEADS).reshape(N_ATT_HEADS // 2, 2)
    return jnp.asarray(np.broadcast_to(np.repeat(s, HEAD_DIM, axis=1)[:, None, :], (N_ATT_HEADS // 2, 8, LANE)).copy())


ATT_MAX_BLOCK_ROWS = 2048


RESIDUE_MAJOR_FROM = 16


class _AttLayout:
    def __init__(self, t, dil):
        self.t, self.dil = t, dil
        self.rows = t // dil
        self.residue_major = dil >= RESIDUE_MAJOR_FROM
        if self.residue_major:
            bq, self.stride = min(512, self.rows), 1
        else:
            bq, self.stride = min(512, self.rows, ATT_MAX_BLOCK_ROWS // dil), dil
        self.nsub = bq // ATT_BLOCK
        self.nblk = self.rows // bq
        self.rb = bq * self.stride
        self.pb = ATT_BLOCK * self.stride
        self.n_pb = self.rows * self.stride // self.pb
        self.out_dtype = F32 if self.stride > 1 else BF16

    def qkv(self, proj):
        if self.residue_major:
            qkv = proj[:, :3 * ATT_WIDTH]
            return qkv.reshape(self.rows, self.dil * 3 * ATT_WIDTH), 3 * ATT_WIDTH // LANE, 0
        return proj, 0, 0

    def act(self, a):
        return a.reshape(self.rows, self.dil * ATT_WIDTH) if self.residue_major else a

    def act_shape(self):
        return (self.rows, self.dil * ATT_WIDTH) if self.residue_major else (self.t, ATT_WIDTH)

    def col(self, r, band, c):
        return r * band + c if self.residue_major else c


def _residue_rows(r, stride, first_block, n_blocks=1):
    if stride == 1:
        return pl.ds(first_block * ATT_BLOCK, n_blocks * ATT_BLOCK)
    return pl.ds(r + first_block * ATT_BLOCK * stride, n_blocks * ATT_BLOCK, stride=stride)


def _lane_half():
    return lax.broadcasted_iota(jnp.int32, (ATT_BLOCK, LANE), 1) // HEAD_DIM


def _att_scores_mask(dil, first):
    iq = lax.broadcasted_iota(jnp.int32, (ATT_BLOCK, 2 * ATT_BLOCK), 0)
    jk = lax.broadcasted_iota(jnp.int32, (ATT_BLOCK, 2 * ATT_BLOCK), 1)
    dist = ATT_BLOCK + iq - jk
    valid = (dist >= 0) & (dist <= ATT_BLOCK) & (jnp.logical_not(first) | (jk >= ATT_BLOCK))
    return (dist * dil).astype(F32), valid


def _stack_heads(x):
    half = _lane_half()
    return jnp.concatenate([jnp.where(half == 0, x, jnp.zeros_like(x)), jnp.where(half == 1, x, jnp.zeros_like(x))], axis=0)


def _unstack_heads(x):
    return jnp.where(_lane_half() == 0, x[:ATT_BLOCK], x[ATT_BLOCK:])


def _head_columns(x):
    return jnp.concatenate([x[:, 0:1], x[:, HEAD_DIM:HEAD_DIM + 1]], axis=0)


def _stacked_bias(sl_ref, dist, valid):
    d2 = jnp.concatenate([dist, dist], axis=0)
    v2 = jnp.concatenate([valid, valid], axis=0)
    top = lax.broadcasted_iota(jnp.int32, d2.shape, 0) < ATT_BLOCK
    slope = jnp.where(top, sl_ref[0:1, 0:1], sl_ref[0:1, HEAD_DIM:HEAD_DIM + 1])
    return jnp.where(v2, -slope * d2, NEG)


def _att_fwd(proj, dil, slopes):
    t = proj.shape[0]
    lay = _AttLayout(t, dil)
    nsub, nblk, rb, pb = lay.nsub, lay.nblk, lay.rb, lay.pb
    src, band, qb = lay.qkv(proj)
    aw = ATT_WIDTH // LANE

    def spec(off, prev=False):
        if prev:
            return pl.BlockSpec((pb, LANE), lambda hp, i, r: (jnp.maximum(i * nsub - 1, 0), lay.col(r, band, qb + off + hp)))
        return pl.BlockSpec((rb, LANE), lambda hp, i, r: (i, lay.col(r, band, qb + off + hp)))

    o_spec = pl.BlockSpec((rb, LANE), lambda hp, i, r: (i, lay.col(r, aw, hp)))

    def body(q_ref, kc_ref, kp_ref, vc_ref, vp_ref, sl_ref, o_ref, lse_ref):
        i, r = pl.program_id(1), pl.program_id(2)
        half = _lane_half()
        for sub in range(nsub):
            rs = _residue_rows(r, lay.stride, sub)
            q = (q_ref[rs, :] * (HEAD_DIM ** -0.5)).astype(BF16)
            if sub == 0:
                r0 = _residue_rows(r, lay.stride, 0)
                kk = jnp.concatenate([kp_ref[r0, :], kc_ref[rs, :]], axis=0).astype(BF16)
                vv = jnp.concatenate([vp_ref[r0, :], vc_ref[rs, :]], axis=0).astype(BF16)
                first = i == 0
            else:
                ks = _residue_rows(r, lay.stride, sub - 1, 2)
                kk, vv = kc_ref[ks, :].astype(BF16), vc_ref[ks, :].astype(BF16)
                first = jnp.bool_(False)
            dist, valid = _att_scores_mask(dil, first)
            s = lax.dot_general(_stack_heads(q), kk, NT_DIMS, preferred_element_type=F32) + _stacked_bias(sl_ref, dist, valid)
            m = jnp.max(s, axis=-1, keepdims=True)
            p = jnp.exp(s - m)
            l = jnp.sum(p, axis=-1, keepdims=True)
            o_ref[rs, :] = _unstack_heads(jnp.dot(p.astype(BF16), vv, preferred_element_type=F32) / l).astype(lay.out_dtype)
            lse_ref[rs, :] = _unstack_heads(jnp.broadcast_to(m + jnp.log(l), (2 * ATT_BLOCK, LANE)))

    o, lse = pl.pallas_call(
        body, grid=(N_ATT_HEADS // 2, nblk, dil),
        in_specs=[spec(0), spec(6), spec(6, True), spec(12), spec(12, True), pl.BlockSpec((None, 8, LANE), lambda hp, i, r: (hp, 0, 0))],
        out_specs=[o_spec, o_spec], out_shape=[S(lay.act_shape(), lay.out_dtype), S(lay.act_shape(), F32)],
        name=f"att_fwd_d{dil}", compiler_params=_params(("parallel", "parallel", "arbitrary")),
    )(src, src, src, src, src, slopes)
    return o.reshape(t, ATT_WIDTH), lse.reshape(t, ATT_WIDTH)


def _att_combine(outs, lses, tr=512):
    t = outs[0].shape[0]

    def body(o0, o1, o2, l0, l1, l2, att_ref, lse_ref):
        ls = [l0[...], l1[...], l2[...]]
        m = jnp.maximum(jnp.maximum(ls[0], ls[1]), ls[2])
        ws = [jnp.exp(l - m) for l in ls]
        tot = ws[0] + ws[1] + ws[2]
        num = ws[0] * o0[...].astype(F32) + ws[1] * o1[...].astype(F32) + ws[2] * o2[...].astype(F32)
        att_ref[...] = (num / tot).astype(BF16)
        lse_ref[...] = m + jnp.log(tot)

    return _rowcall(body, "att_combine", t, tr, [_rows(a, tr) for a in list(outs) + list(lses)],
                    [_orow(t, ATT_WIDTH, BF16, tr), _orow(t, ATT_WIDTH, F32, tr)])


def _att_delta(g_att, att, tr=512):
    t = att.shape[0]

    def body(g_ref, a_ref, o_ref):
        prod = g_ref[...] * a_ref[...].astype(F32)
        o_ref[...] = _dot_split(prod, _block_ones(ATT_WIDTH, HEAD_DIM), 0, 3)

    return _rowcall(body, "att_delta", t, tr, [_rows(g_att, tr), _rows(att, tr)], [_orow(t, ATT_WIDTH, F32, tr)])[0]


def _att_bwd(proj, g_att, lse, delta, dil, slopes):
    t = proj.shape[0]
    lay = _AttLayout(t, dil)
    nsub, nblk, rb, pb, n_pb = lay.nsub, lay.nblk, lay.rb, lay.pb, lay.n_pb
    src, band, qb = lay.qkv(proj)
    aw = ATT_WIDTH // LANE

    def near(i, which):
        return jnp.maximum(i * nsub - 1, 0) if which == "prev" else jnp.minimum((i + 1) * nsub, n_pb - 1)

    def pspec(off, which=None):
        if which:
            return pl.BlockSpec((pb, LANE), lambda hp, i, r: (near(i, which), lay.col(r, band, qb + off + hp)))
        return pl.BlockSpec((rb, LANE), lambda hp, i, r: (i, lay.col(r, band, qb + off + hp)))

    def aspec(which=None):
        if which:
            return pl.BlockSpec((pb, LANE), lambda hp, i, r: (near(i, which), lay.col(r, aw, hp)))
        return pl.BlockSpec((rb, LANE), lambda hp, i, r: (i, lay.col(r, aw, hp)))

    scale = HEAD_DIM ** -0.5

    def body(q_ref, qn_ref, kc_ref, kp_ref, vc_ref, vp_ref, do_ref, don_ref, lse_ref, lsen_ref, dl_ref, dln_ref, sl_ref,
             dq_ref, dk_ref, dv_ref):
        i, r = pl.program_id(1), pl.program_id(2)
        half = _lane_half()

        def tile_grads(q, do, lse_q, dl_q, kk, vv, dist, valid):
            q2, do2 = _stack_heads(q), _stack_heads(do)
            s = lax.dot_general(q2, kk, NT_DIMS, preferred_element_type=F32) + _stacked_bias(sl_ref, dist, valid)
            p = jnp.exp(s - _head_columns(lse_q))
            dp = lax.dot_general(do2, vv, NT_DIMS, preferred_element_type=F32)
            ds16 = (p * (dp - _head_columns(dl_q))).astype(BF16)
            dq = _unstack_heads(jnp.dot(ds16, kk, preferred_element_type=F32)) * scale
            dk = lax.dot_general(ds16, q2, TN_DIMS, preferred_element_type=F32)
            dv = lax.dot_general(p.astype(BF16), do2, TN_DIMS, preferred_element_type=F32)
            return dq, dk, dv

        carry_k = carry_v = None
        for sub in range(nsub):
            rs = _residue_rows(r, lay.stride, sub)
            q = (q_ref[rs, :] * scale).astype(BF16)
            do = do_ref[rs, :].astype(BF16)
            if sub == 0:
                r0 = _residue_rows(r, lay.stride, 0)
                kk = jnp.concatenate([kp_ref[r0, :], kc_ref[rs, :]], axis=0).astype(BF16)
                vv = jnp.concatenate([vp_ref[r0, :], vc_ref[rs, :]], axis=0).astype(BF16)
                first = i == 0
            else:
                ks = _residue_rows(r, lay.stride, sub - 1, 2)
                kk, vv = kc_ref[ks, :].astype(BF16), vc_ref[ks, :].astype(BF16)
                first = jnp.bool_(False)
            dist, valid = _att_scores_mask(dil, first)
            dq, dk2, dv2 = tile_grads(q, do, lse_ref[rs, :], dl_ref[rs, :], kk, vv, dist, valid)
            dq_ref[rs, :] = dq.astype(lay.out_dtype)
            if sub > 0:
                rp = _residue_rows(r, lay.stride, sub - 1)
                dk_ref[rp, :] = (carry_k + dk2[:ATT_BLOCK, :]).astype(lay.out_dtype)
                dv_ref[rp, :] = (carry_v + dv2[:ATT_BLOCK, :]).astype(lay.out_dtype)
            carry_k, carry_v = dk2[ATT_BLOCK:, :], dv2[ATT_BLOCK:, :]
        rl = _residue_rows(r, lay.stride, nsub - 1)
        rn = _residue_rows(r, lay.stride, 0)
        iq = lax.broadcasted_iota(jnp.int32, (ATT_BLOCK, ATT_BLOCK), 0)
        jk = lax.broadcasted_iota(jnp.int32, (ATT_BLOCK, ATT_BLOCK), 1)
        dist_i = ATT_BLOCK + iq - jk
        valid = (dist_i >= 0) & (dist_i <= ATT_BLOCK) & (i < nblk - 1)
        qn = (qn_ref[rn, :] * scale).astype(BF16)
        _, dk1, dv1 = tile_grads(qn, don_ref[rn, :].astype(BF16), lsen_ref[rn, :], dln_ref[rn, :],
                                 kc_ref[rl, :].astype(BF16), vc_ref[rl, :].astype(BF16), (dist_i * dil).astype(F32), valid)
        dk_ref[rl, :] = (carry_k + dk1).astype(lay.out_dtype)
        dv_ref[rl, :] = (carry_v + dv1).astype(lay.out_dtype)

    gv, lv, dlv = lay.act(g_att), lay.act(lse), lay.act(delta)
    dq, dk, dv = pl.pallas_call(
        body, grid=(N_ATT_HEADS // 2, nblk, dil),
        in_specs=[pspec(0), pspec(0, "next"), pspec(6), pspec(6, "prev"), pspec(12), pspec(12, "prev"),
                  aspec(), aspec("next"), aspec(), aspec("next"), aspec(), aspec("next"),
                  pl.BlockSpec((None, 8, LANE), lambda hp, i, r: (hp, 0, 0))],
        out_specs=[aspec(), aspec(), aspec()], out_shape=[S(lay.act_shape(), lay.out_dtype)] * 3,
        name=f"att_bwd_d{dil}", compiler_params=_params(("parallel", "parallel", "arbitrary")),
    )(src, src, src, src, src, src, gv, gv, lv, lv, dlv, dlv, slopes)
    return dq.reshape(t, ATT_WIDTH), dk.reshape(t, ATT_WIDTH), dv.reshape(t, ATT_WIDTH)


def _att_grad_sum(dqs, dks, dvs, tr=512):
    t = dqs[0].shape[0]

    def body(*refs):
        o_ref = refs[-1]
        for n in range(3):
            tot = refs[3 * n][...].astype(F32) + refs[3 * n + 1][...].astype(F32) + refs[3 * n + 2][...].astype(F32)
            o_ref[:, n * ATT_WIDTH:(n + 1) * ATT_WIDTH] = tot.astype(BF16)

    return _rowcall(body, "att_grad_sum", t, tr, [_rows(a, tr) for a in list(dqs) + list(dks) + list(dvs)],
                    [_orow(t, 3 * ATT_WIDTH, BF16, tr)])[0]


def _ssd_common(xs, dtx, cs, cs_t):
    ch = SSM_CHUNK
    row = lax.broadcasted_iota(jnp.int32, (ch, ch), 0)
    col = lax.broadcasted_iota(jnp.int32, (ch, ch), 1)
    cs_last = cs[ch - 1:ch, :]
    return dict(tril=col <= row, row=row, col=col, cs=cs, cs_t=cs_t, cs_last=cs_last,
                e=jnp.exp(cs), w=jnp.exp(cs_last - cs), xd=xs * dtx)


def _dot_split(a, b, split, terms=2):
    ops = [a, b]
    rest = ops[split]
    other = ops[1 - split].astype(BF16)
    out = None
    for _ in range(terms):
        piece = rest.astype(BF16)
        rest = rest - piece.astype(F32)
        part = jnp.dot(other, piece, preferred_element_type=F32) if split == 1 else jnp.dot(piece, other, preferred_element_type=F32)
        out = part if out is None else out + part
    return out


def _decay_col(cs_t, heads_per_group):
    r = lax.broadcasted_iota(jnp.int32, (heads_per_group * SSM_HEAD_DIM, SSM_STATE), 0) // SSM_HEAD_DIM
    out = jnp.zeros((heads_per_group * SSM_HEAD_DIM, SSM_STATE), F32)
    for j in range(heads_per_group):
        out = jnp.where(r == j, jnp.exp(cs_t[j:j + 1, SSM_CHUNK - 1:SSM_CHUNK]), out)
    return out


SSD_GROUPS_PER_STEP = 2


def _ssd_specs(t):
    hg = SSM_HEADS // SSM_GROUPS
    gw = hg * SSM_HEAD_DIM
    nb0 = SSM_INNER // SSM_STATE
    return hg, gw, nb0


def _ssd_group_views(gi, gw, wide, narrow, stacked):
    w = [r.at[:, pl.ds(gi * gw, gw)] for r in wide]
    n = [r.at[:, pl.ds(gi * SSM_STATE, SSM_STATE)] for r in narrow]
    return w, n, [r.at[gi] for r in stacked]


def _ssd_fwd(xa, dtx, csx, cst_g):
    t = xa.shape[0]
    nch = t // SSM_CHUNK
    hg, gw, nb0 = _ssd_specs(t)
    ch = SSM_CHUNK
    gp = SSD_GROUPS_PER_STEP

    def body(xs_ref, b_ref, c_ref, dtx_ref, cs_ref, cst_ref, y_ref, st_ref, h_scr):
        for gi in range(gp):
            (xs_g, dtx_g, cs_g, y_g), (b_g, c_g), (cst_gi, st_g) = _ssd_group_views(
                gi, gw, (xs_ref, dtx_ref, cs_ref, y_ref), (b_ref, c_ref), (cst_ref, st_ref))
            group_body(pl.program_id(0), pl.program_id(1) * gp + gi, xs_g, b_g, c_g, dtx_g, cs_g, cst_gi, y_g, st_g, h_scr)

    def group_body(cc, g, xs_ref, b_ref, c_ref, dtx_ref, cs_ref, cst_ref, y_ref, st_ref, h_scr):
        @pl.when(cc == 0)
        def _():
            h_scr[g] = jnp.zeros((gw, SSM_STATE), F32)

        q = _ssd_common(xs_ref[...], dtx_ref[...], cs_ref[...], cst_ref[...])
        bb, cb = b_ref[...].astype(BF16), c_ref[...].astype(BF16)
        cbm = lax.dot_general(cb, bb, NT_DIMS, preferred_element_type=F32)
        h = h_scr[g]
        st_ref[...] = h
        xd16 = q["xd"].astype(BF16)
        y = lax.dot_general(cb, h.astype(BF16), NT_DIMS, preferred_element_type=F32) * q["e"]
        lane_head = lax.broadcasted_iota(jnp.int32, (ch, gw), 1) // SSM_HEAD_DIM
        gmats, xds = [], []
        for j in range(hg):
            diff = q["cs"][:, j * SSM_HEAD_DIM:j * SSM_HEAD_DIM + 1] - q["cs_t"][j:j + 1, :]
            gmats.append((cbm * jnp.exp(jnp.where(q["tril"], diff, NEG))).astype(BF16))
            xds.append(jnp.where(lane_head == j, xd16, jnp.zeros_like(xd16)))
        y = y + jnp.dot(jnp.concatenate(gmats, axis=1), jnp.concatenate(xds, axis=0), preferred_element_type=F32)
        y_ref[...] = y.astype(BF16)
        s_new = lax.dot_general((q["xd"] * q["w"]).astype(BF16), bb, TN_DIMS, preferred_element_type=F32)
        h_scr[g] = _decay_col(q["cs_t"], hg) * h + s_new

    wide = pl.BlockSpec((ch, gp * gw), lambda cc, g: (cc, g))
    return pl.pallas_call(
        body, grid=(nch, SSM_GROUPS // gp),
        in_specs=[wide,
                  pl.BlockSpec((ch, gp * SSM_STATE), lambda cc, g: (cc, nb0 // gp + g)),
                  pl.BlockSpec((ch, gp * SSM_STATE), lambda cc, g: (cc, (nb0 + SSM_GROUPS) // gp + g)),
                  wide, wide,
                  pl.BlockSpec((gp, 8, ch), lambda cc, g: (g, 0, cc))],
        out_specs=[wide, pl.BlockSpec((None, gp, gw, SSM_STATE), lambda cc, g: (cc, g, 0, 0))],
        out_shape=[S((t, SSM_INNER), BF16), S((nch, SSM_GROUPS, gw, SSM_STATE), F32)],
        scratch_shapes=[pltpu.VMEM((SSM_GROUPS, gw, SSM_STATE), F32)],
        name="ssd_fwd", compiler_params=_params(("arbitrary", "arbitrary")),
    )(xa, xa, xa, dtx, csx, cst_g)


def _ssd_bwd(xa, dtx, csx, cst_g, alog_x, g_y, states, dskip_x):
    t = xa.shape[0]
    nch = t // SSM_CHUNK
    hg, gw, nb0 = _ssd_specs(t)
    ch = SSM_CHUNK
    gp = SSD_GROUPS_PER_STEP

    def rc(cc):
        return nch - 1 - cc

    def body(xs_ref, b_ref, c_ref, dtx_ref, cs_ref, cst_ref, alx_ref, gy_ref, st_ref, dsk_ref,
             gxs_ref, gb_ref, gc_ref, gdt_ref, ga_ref, gh_scr):
        for gi in range(gp):
            wide, narrow, stacked = _ssd_group_views(
                gi, gw, (xs_ref, dtx_ref, cs_ref, alx_ref, gy_ref, dsk_ref, gxs_ref, gdt_ref, ga_ref), (b_ref, c_ref, gb_ref, gc_ref),
                (cst_ref, st_ref))
            xs_g, dtx_g, cs_g, alx_g, gy_g, dsk_g, gxs_g, gdt_g, ga_g = wide
            b_g, c_g, gb_g, gc_g = narrow
            group_body(pl.program_id(0), pl.program_id(1) * gp + gi, xs_g, b_g, c_g, dtx_g, cs_g, stacked[0], alx_g, gy_g, stacked[1],
                       dsk_g, gxs_g, gb_g, gc_g, gdt_g, ga_g, gh_scr)

    def group_body(cc, g, xs_ref, b_ref, c_ref, dtx_ref, cs_ref, cst_ref, alx_ref, gy_ref, st_ref, dsk_ref,
                   gxs_ref, gb_ref, gc_ref, gdt_ref, ga_ref, gh_scr):
        @pl.when(cc == 0)
        def _():
            gh_scr[g] = jnp.zeros((gw, SSM_STATE), F32)

        xs, dtx = xs_ref[...], dtx_ref[...]
        q = _ssd_common(xs, dtx, cs_ref[...], cst_ref[...])
        cs, cs_t, e, w, xd = q["cs"], q["cs_t"], q["e"], q["w"], q["xd"]
        bb, cb = b_ref[...].astype(BF16), c_ref[...].astype(BF16)
        gy = gy_ref[...]
        gy16, xd16 = gy.astype(BF16), xd.astype(BF16)
        h = st_ref[...]
        h16 = h.astype(BF16)
        ghn = gh_scr[g]
        ghn16 = ghn.astype(BF16)
        seg = _block_ones(gw, SSM_HEAD_DIM)
        cbm = lax.dot_general(cb, bb, NT_DIMS, preferred_element_type=F32)

        gye16 = (gy * e).astype(BF16)
        chm = lax.dot_general(cb, h16, NT_DIMS, preferred_element_type=F32)
        g_c = jnp.dot(gye16, h16, preferred_element_type=F32)
        gh_off = lax.dot_general(gye16, cb, TN_DIMS, preferred_element_type=F32)
        bgs = lax.dot_general(bb, ghn16, NT_DIMS, preferred_element_type=F32)
        g_xd = w * bgs
        head_sums = _dot_split(jnp.concatenate([gy * chm, xd * bgs], axis=0), seg, 0)
        g_e, g_w = head_sums[:ch], head_sums[ch:]
        g_b = jnp.dot((xd * w).astype(BF16), ghn16, preferred_element_type=F32)
        decay = _decay_col(cs_t, hg)
        gh_scr[g] = decay * ghn + gh_off
        rsum = jnp.sum(ghn * h, axis=1, keepdims=True)
        lane_head = lax.broadcasted_iota(jnp.int32, (ch, gw), 1) // SSM_HEAD_DIM
        lane_head1 = lax.broadcasted_iota(jnp.int32, (1, gw), 1) // SSM_HEAD_DIM
        g_el = jnp.zeros((1, gw), F32)
        g_cs = g_e * e - g_w * w
        upper = q["row"] <= q["col"]
        lms, gys = [], []
        for j in range(hg):
            g_el = jnp.where(lane_head1 == j, jnp.sum(rsum[j * SSM_HEAD_DIM:(j + 1) * SSM_HEAD_DIM, :], axis=0, keepdims=True), g_el)
            csc = cs[:, j * SSM_HEAD_DIM:j * SSM_HEAD_DIM + 1]
            csr = cs_t[j:j + 1, :]
            lms.append(jnp.exp(jnp.where(q["tril"], csc - csr, NEG)))
            gys.append(jnp.where(lane_head == j, gy16, jnp.zeros_like(gy16)))
        lm_st, gy_st = jnp.concatenate(lms, axis=0), jnp.concatenate(gys, axis=0)
        cbm_st = jnp.concatenate([cbm] * hg, axis=0)
        gcb_st = lax.dot_general(gy_st, xd16, NT_DIMS, preferred_element_type=F32) * lm_st
        gcb_sum = gcb_st[0:ch]
        for j in range(1, hg):
            gcb_sum = gcb_sum + gcb_st[j * ch:(j + 1) * ch]
        gcb16 = gcb_sum.astype(BF16)
        g_c = g_c + jnp.dot(gcb16, bb, preferred_element_type=F32)
        g_b = g_b + lax.dot_general(gcb16, cb, TN_DIMS, preferred_element_type=F32)
        g_xd = g_xd + lax.dot_general((cbm_st * lm_st).astype(BF16), gy_st, TN_DIMS, preferred_element_type=F32)
        m_st = gcb_st * cbm_st
        for j in range(hg):
            m_ls = m_st[j * ch:(j + 1) * ch]
            d_cs = jnp.sum(m_ls, axis=1, keepdims=True) - jnp.sum(m_ls.T, axis=1, keepdims=True)
            g_cs = g_cs + jnp.where(lane_head == j, d_cs, 0.0)
        extra = _colsum(g_w * w) + g_el * jnp.exp(q["cs_last"])
        g_cs = g_cs + jnp.where(lax.broadcasted_iota(jnp.int32, (ch, gw), 0) == ch - 1, extra, 0.0)
        g_la = _dot_split(upper, g_cs, 1)
        a_x = -jnp.exp(alx_ref[...])
        gdt_ref[...] = g_xd * xs + g_la * a_x * (1.0 / SSM_HEAD_DIM)
        ga_row = _colsum(g_la * (dtx * a_x)) * (1.0 / SSM_HEAD_DIM)
        ga_ref[...] = jnp.where(lax.broadcasted_iota(jnp.int32, (8, gw), 0) == 0, ga_row, 0.0)
        gxs_ref[...] = g_xd * dtx + gy * dsk_ref[...]
        gb_ref[...] = g_b
        gc_ref[...] = g_c

    wide = pl.BlockSpec((ch, gp * gw), lambda cc, g: (rc(cc), g))
    narrow = pl.BlockSpec((ch, gp * SSM_STATE), lambda cc, g: (rc(cc), g))
    row = pl.BlockSpec((1, gp * gw), lambda cc, g: (0, g))
    return pl.pallas_call(
        body, grid=(nch, SSM_GROUPS // gp),
        in_specs=[wide,
                  pl.BlockSpec((ch, gp * SSM_STATE), lambda cc, g: (rc(cc), nb0 // gp + g)),
                  pl.BlockSpec((ch, gp * SSM_STATE), lambda cc, g: (rc(cc), (nb0 + SSM_GROUPS) // gp + g)),
                  wide, wide,
                  pl.BlockSpec((gp, 8, ch), lambda cc, g: (g, 0, rc(cc))),
                  row, wide,
                  pl.BlockSpec((None, gp, gw, SSM_STATE), lambda cc, g: (rc(cc), g, 0, 0)),
                  row],
        out_specs=[wide, narrow, narrow, wide, pl.BlockSpec((8, gp * gw), lambda cc, g: (rc(cc), g))],
        out_shape=[S((t, SSM_INNER), F32), S((t, SSM_GROUPS * SSM_STATE), F32), S((t, SSM_GROUPS * SSM_STATE), F32),
                   S((t, SSM_INNER), F32), S((nch * 8, SSM_INNER), F32)],
        scratch_shapes=[pltpu.VMEM((SSM_GROUPS, gw, SSM_STATE), F32)],
        name="ssd_bwd", compiler_params=_params(("arbitrary", "arbitrary")),
    )(xa, xa, xa, dtx, csx, cst_g, alog_x, g_y, states, dskip_x)


def _local_step(x, target, w_pre, w_in_r, b_gate, conv_w, conv_b, dt_bias, a_log, d_skip, ssm_norm_w,
                late_weights, w_post, w_fpre, w_fpost, on_mid_grads, on_in_proj_grads):
    t = x.shape[0]
    mm = functools.partial(_matmul, tm=512)
    slopes = _slope_rows()
    hg = SSM_HEADS // SSM_GROUPS
    dt_bias_pad = jnp.pad(dt_bias, ((0, 0), (0, LANE - SSM_HEADS)))
    alog_x = jnp.repeat(a_log, SSM_HEAD_DIM, axis=1)
    alog_pad = jnp.pad(a_log, ((0, 0), (0, LANE - SSM_HEADS)))
    dskip_x = jnp.repeat(d_skip, SSM_HEAD_DIM, axis=1)

    u = _pre_norm(x, w_pre)
    pa = mm(u, w_in_r, mode="nn", out_dtype=BF16, name="in_proj_zgx", tn=2048, tk=D_MODEL, b_cols=(0, PA_W))
    pb = mm(u, w_in_r[:, PA_W:], mode="nn", out_dtype=F32, name="in_proj_qkvdt", tn=PB_W // 2, tk=D_MODEL)
    fwd = [_att_fwd(pb, dil, slopes) for _, dil in DILATED_PATTERNS]
    att, lse = _att_combine([o for o, _ in fwd], [l for _, l in fwd])
    xa, xc = _conv_fwd(pa, conv_w, conv_b)
    dtx, csx, cst = _dt_fwd(pb, dt_bias_pad, alog_pad)
    cst_g = jnp.pad(cst[:SSM_HEADS].reshape(SSM_GROUPS, hg, t), ((0, 0), (0, 8 - hg), (0, 0)))
    y_ssd, states = _ssd_fwd(xa, dtx, csx, cst_g)
    y4 = _gate_norm_fwd(y_ssd, xa, pa, dskip_x, ssm_norm_w)
    w_att, w_ssm, w_out, w_up, w_down = late_weights(y4)
    att_p = mm(att, w_att, mode="nn", out_dtype=BF16, name="att_proj", tn=D_MODEL, tk=ATT_WIDTH)
    ssm_p = mm(y4, w_ssm, mode="nn", out_dtype=BF16, name="ssm_proj", tn=D_MODEL, tk=SSM_INNER)
    mixin = _gating_fwd(pa, b_gate, att_p, ssm_p)
    mixed = mm(mixin, w_out, mode="nn", out_dtype=F32, name="out_proj", tn=D_MODEL, tk=D_MODEL)
    h1, f = _mix_post_ffn_pre(x, mixed, w_post, w_fpre)
    act, up = _matmul(f, w_up, mode="nn", out_dtype=BF16, name="ffn_up", tm=2048, tn=FFN_HIDDEN // N_DEV, tk=D_MODEL, epilogue="relu2", stacked=True)
    dn = mm(act, w_down, mode="nn", out_dtype=F32, name="ffn_down", tn=D_MODEL, tk=FFN_HIDDEN)
    loss, g_h2, g_dn, gw_fpost = _loss_and_ffn_post_bwd(h1, dn, w_fpost, target)

    g_up = mm(g_dn, w_down, mode="nt", out_dtype=BF16, name="ffn_down_bwd_x", tn=2048, tk=D_MODEL, epilogue="relu2_bwd", extra=up)
    gw_down = _matmul(act, g_dn, mode="tn", out_dtype=BF16, name="ffn_down_bwd_w", tm=1024, tn=D_MODEL, tk=2048)
    w_up_rows = jnp.moveaxis(w_up, 0, 1).reshape(D_MODEL, FFN_HIDDEN)
    g_f = mm(g_up, w_up_rows, mode="nt", out_dtype=F32, name="ffn_up_bwd_x", tn=D_MODEL, tk=FFN_HIDDEN)
    gw_up = _matmul(f, g_up, mode="tn", out_dtype=BF16, name="ffn_up_bwd_w", tm=D_MODEL, tn=FFN_HIDDEN // N_DEV, tk=2048, stacked=True)
    g_h1, g_mixed, gw_fpre, gw_post = _ffn_pre_mix_post_bwd(g_h2, g_f, h1, w_fpre, mixed, w_post)
    g_mixin = mm(g_mixed, w_out, mode="nt", out_dtype=BF16, name="out_proj_bwd_x", tn=D_MODEL, tk=D_MODEL)
    gw_out = _matmul(mixin, g_mixed, mode="tn", out_dtype=BF16, name="out_proj_bwd_w", tm=D_MODEL, tn=D_MODEL, tk=2048)
    g_att_p, g_ssm_p, g_gl, g_b_gate = _gating_bwd(g_mixin, pa, b_gate, att_p, ssm_p)
    g_att = mm(g_att_p, w_att, mode="nt", out_dtype=F32, name="att_proj_bwd_x", tn=ATT_WIDTH, tk=D_MODEL)
    gw_att = _matmul(att, g_att_p, mode="tn", out_dtype=BF16, name="att_proj_bwd_w", tm=ATT_WIDTH, tn=D_MODEL, tk=2048)
    g_y4 = mm(g_ssm_p, w_ssm, mode="nt", out_dtype=BF16, name="ssm_proj_bwd_x", tn=SSM_INNER, tk=D_MODEL)
    gw_ssm = _matmul(y4, g_ssm_p, mode="tn", out_dtype=BF16, name="ssm_proj_bwd_w", tm=1024, tn=D_MODEL, tk=2048)
    token = on_mid_grads(dict(w_att_proj=gw_att, w_ssm_proj=gw_ssm, w_out=gw_out, w_up=gw_up, w_down=gw_down))
    if token is not None:
        ssm_norm_w = ssm_norm_w + jnp.tile(token[0:1, :], (1, SSM_INNER // LANE))
    g_y2, g_z, g_norm_w, _, g_d_skip = _gate_norm_bwd(g_y4, y_ssd, xa, pa, dskip_x, ssm_norm_w)
    g_xs, g_bm, g_cm, g_dtx, ga_rows = _ssd_bwd(xa, dtx, csx, cst_g, alog_x, g_y2, states, dskip_x)
    g_dt_raw, g_dt_bias, g_a_log = _dt_bwd(g_dtx, ga_rows, pb, dt_bias_pad)
    g_xbc, g_conv_b, gcw0, gcw1, gcw2, gcw3 = _conv_bwd(g_xs, g_bm, g_cm, xc, pa, conv_w)
    delta = _att_delta(g_att, att)
    dqs, dks, dvs = [], [], []
    for _, dil in DILATED_PATTERNS:
        dq, dk, dv = _att_bwd(pb, g_att, lse, delta, dil, slopes)
        dqs.append(dq)
        dks.append(dk)
        dvs.append(dv)
    g_qkv = _att_grad_sum(dqs, dks, dvs)
    g_proj = jnp.concatenate([g_z, g_gl, g_xbc, g_qkv, g_dt_raw, jnp.zeros((t, PROJ_W - OFF_DT - LANE), BF16)], axis=1)
    gw_in_r = _matmul(u, g_proj, mode="tn", out_dtype=BF16, name="in_proj_bwd_w", tm=D_MODEL, tn=1792, tk=2048)
    token = on_in_proj_grads(gw_in_r, jnp.concatenate([gcw0, gcw1, gcw2, gcw3], axis=0))
    g_u = _matmul(g_proj, w_in_r, mode="nt", out_dtype=F32, name="in_proj_bwd_x", tm=1024, tn=D_MODEL, tk=3584, after=token)
    g_x, gw_pre = _pre_norm_bwd(g_h1, g_u, x, w_pre)

    grads = dict(
        norm_mix_pre_w=gw_pre, b_gate=g_b_gate, conv_b=g_conv_b, dt_bias=g_dt_bias[:, :SSM_HEADS], a_log=g_a_log[:, :SSM_HEADS],
        d_skip=g_d_skip[:, :SSM_HEADS], ssm_norm_w=g_norm_w, norm_mix_post_w=gw_post, norm_ffn_pre_w=gw_fpre, norm_ffn_post_w=gw_fpost)
    return loss, g_x, grads


def _mesh_pos():
    return lax.axis_index("x"), lax.axis_index("y"), lax.axis_index("c")


def _all_gather(shards):
    n = len(shards)

    def body(*refs):
        x_refs, o_refs = refs[:n], refs[n:2 * n]
        send_sems, recv_sems, local_sems = refs[2 * n:]
        x, y, c = _mesh_pos()
        me, sibling = (x, y, c), (x, y, 1 - c)
        chips = [(1 - x, y), (x, 1 - y), (1 - x, 1 - y)]

        def copy(a, k, block, to, src=None):
            dst = o_refs[a].at[4 * block[0] + 2 * block[1] + block[2]]
            return pltpu.make_async_remote_copy(
                src_ref=dst if src is None else src, dst_ref=dst, send_sem=send_sems.at[7 * a + k], recv_sem=recv_sems.at[7 * a + k],
                device_id=to, device_id_type=pl.DeviceIdType.MESH)

        mine = [pltpu.make_async_copy(x_refs[a], o_refs[a].at[4 * x + 2 * y + c], local_sems.at[a]) for a in range(n)]
        for cp in mine:
            cp.start()
        first = []
        for a in range(n):
            first.append(copy(a, 0, me, sibling, src=x_refs[a]))
            first += [copy(a, 1 + j, me, (*chip, c), src=x_refs[a]) for j, chip in enumerate(chips)]
        for cp in first:
            cp.start()
        passed = []
        for j, chip in enumerate(chips):
            for a in range(n):
                copy(a, 1 + j, (*chip, c), me).wait_recv()
                passed.append(copy(a, 4 + j, (*chip, c), sibling))
                passed[-1].start()
        for a in range(n):
            copy(a, 0, sibling, me).wait_recv()
            for j, chip in enumerate(chips):
                copy(a, 4 + j, (*chip, 1 - c), me).wait_recv()
        for cp in first + passed:
            cp.wait_send()
        for cp in mine:
            cp.wait()

    hbm = pl.BlockSpec(memory_space=pltpu.HBM)
    return pl.pallas_call(
        body, out_shape=[S((N_DEV,) + s.shape, s.dtype) for s in shards],
        in_specs=[hbm] * n, out_specs=[hbm] * n,
        scratch_shapes=[pltpu.SemaphoreType.DMA((7 * n,)), pltpu.SemaphoreType.DMA((7 * n,)), pltpu.SemaphoreType.DMA((n,))],
        name="weights_all_gather",
    )(*shards)


def _exchange_grads(slab_arrays, small):
    n = len(slab_arrays)
    r_small = small.shape[0]

    def body(*refs):
        slab_refs, small_ref = refs[:n], refs[n]
        recv_refs, gsm_ref = refs[n + 1:2 * n + 1], refs[2 * n + 1]
        send_sems, recv_sems, local_sems = refs[2 * n + 2:]
        x, y, c = _mesh_pos()
        me = 4 * x + 2 * y + c

        def peer(k):
            px = 1 - x if k & 4 else x
            py = 1 - y if k & 2 else y
            pc = 1 - c if k & 1 else c
            return (px, py, pc), 4 * px + 2 * py + pc

        def copy(a, k, sending):
            to, lin = peer(k)
            sem = 7 * a + k - 1
            if a == n:
                src, dst = small_ref, gsm_ref.at[me if sending else lin]
            else:
                src, dst = slab_refs[a].at[lin], recv_refs[a].at[me if sending else lin]
            return pltpu.make_async_remote_copy(src_ref=src, dst_ref=dst, send_sem=send_sems.at[sem], recv_sem=recv_sems.at[sem],
                                                device_id=to, device_id_type=pl.DeviceIdType.MESH)

        own = [pltpu.make_async_copy(slab_refs[a].at[me], recv_refs[a].at[me], local_sems.at[a]) for a in range(n)]
        own.append(pltpu.make_async_copy(small_ref, gsm_ref.at[me], local_sems.at[n]))
        for cp in own:
            cp.start()
        order = [n] + list(range(n))
        sends = [copy(a, k, True) for a in order for k in range(1, N_DEV)]
        for cp in sends:
            cp.start()
        for a in order:
            for k in range(1, N_DEV):
                copy(a, k, False).wait_recv()
        for cp in sends:
            cp.wait_send()
        for cp in own:
            cp.wait()

    hbm = pl.BlockSpec(memory_space=pltpu.HBM)
    n_sem = 7 * (n + 1)
    res = pl.pallas_call(
        body, out_shape=[S(a.shape, a.dtype) for a in slab_arrays] + [S((N_DEV, r_small, LANE), small.dtype)],
        in_specs=[hbm] * (n + 1), out_specs=[hbm] * (n + 1),
        scratch_shapes=[pltpu.SemaphoreType.DMA((n_sem,)), pltpu.SemaphoreType.DMA((n_sem,)), pltpu.SemaphoreType.DMA((n + 1,))],
        name="grad_exchange",
    )(*slab_arrays, small)
    return res[:n], res[n]


def _peer_of(k, x, y, c):
    px = 1 - x if k & 4 else x
    py = 1 - y if k & 2 else y
    pc = 1 - c if k & 1 else c
    return (px, py, pc), 4 * px + 2 * py + pc


def _split_copies(src_refs, land_refs, send_sems, recv_sems, per_peer):
    x, y, c = _mesh_pos()
    me = 4 * x + 2 * y + c
    sends, recvs = [], []
    for a, (src, land) in enumerate(zip(src_refs, land_refs)):
        for k in range(1, N_DEV):
            to, lin = _peer_of(k, x, y, c)
            sem = 7 * a + k - 1
            piece = src.at[lin] if per_peer else src
            for slot, out in ((me, sends), (lin, recvs)):
                out.append(pltpu.make_async_remote_copy(
                    src_ref=piece, dst_ref=land.at[slot], send_sem=send_sems.at[sem], recv_sem=recv_sems.at[sem],
                    device_id=to, device_id_type=pl.DeviceIdType.MESH))
    return sends, recvs


def _remote_start(srcs, per_peer, name):
    n = len(srcs)
    lands = [lax.empty((N_DEV,) + (s.shape[1:] if per_peer else s.shape), s.dtype) for s in srcs]

    def body(*refs):
        src_refs, land_refs = refs[:n], refs[n:2 * n]
        send_sems, recv_sems = refs[2 * n], refs[2 * n + 1]
        token = refs[-1]
        sends, _ = _split_copies(src_refs, land_refs, send_sems, recv_sems, per_peer)
        for cp in sends:
            cp.start()
        token[...] = jnp.zeros_like(token)

    hbm = pl.BlockSpec(memory_space=pltpu.HBM)
    sem = pl.BlockSpec(memory_space=pltpu.SEMAPHORE)
    res = pl.pallas_call(
        body, name=name,
        out_shape=(pltpu.SemaphoreType.DMA((7 * n,)), pltpu.SemaphoreType.DMA((7 * n,)),
                   *[pltpu.HBM(a.shape, a.dtype) for a in srcs + lands], S((8, LANE), F32)),
        in_specs=[hbm] * (2 * n), out_specs=(sem, sem, *[hbm] * (2 * n), pl.BlockSpec(memory_space=pltpu.VMEM)),
        input_output_aliases={i: 2 + i for i in range(2 * n)},
        compiler_params=pltpu.CompilerParams(has_side_effects=pltpu.SideEffectType.DATAFLOW_SIDE_EFFECTING),
    )(*[pltpu.with_memory_space_constraint(a, pltpu.HBM) for a in srcs + lands])
    return dict(sems=res[:2], srcs=list(res[2:2 + n]), lands=list(res[2 + n:2 + 2 * n]), per_peer=per_peer), res[-1]


def _remote_wait(handle, after, name):
    n = len(handle["srcs"])
    per_peer = handle["per_peer"]

    def body(*refs):
        src_refs, land_refs = refs[:n], refs[n:2 * n]
        send_sems, recv_sems = refs[2 * n], refs[2 * n + 1]
        sends, recvs = _split_copies(src_refs, land_refs, send_sems, recv_sems, per_peer)
        for cp in sends:
            cp.wait_send()
        for cp in recvs:
            cp.wait_recv()

    hbm = pl.BlockSpec(memory_space=pltpu.HBM)
    sem = pl.BlockSpec(memory_space=pltpu.SEMAPHORE)
    arrays = handle["srcs"] + handle["lands"]
    res = pl.pallas_call(
        body, name=name, out_shape=tuple(pltpu.HBM(a.shape, a.dtype) for a in arrays),
        in_specs=[hbm] * (2 * n) + [sem, sem, pl.BlockSpec(memory_space=pl.ANY)], out_specs=tuple([hbm] * (2 * n)),
        input_output_aliases={i: i for i in range(2 * n)},
        compiler_params=pltpu.CompilerParams(has_side_effects=pltpu.SideEffectType.DATAFLOW_SIDE_EFFECTING),
    )(*arrays, *handle["sems"], after)
    return list(res[n:])


def _with_own(lands, own, me):
    return [lax.dynamic_update_index_in_dim(land, o.astype(land.dtype), me, 0) for land, o in zip(lands, own)]


def _adamw(w, m, v, slabs, name, tr):
    r, cols = w.shape
    c1 = 1.0 - ADAM_B1 ** ADAM_STEP
    c2 = 1.0 - ADAM_B2 ** ADAM_STEP

    def body(w_ref, m_ref, v_ref, s_ref, g_ref, d_ref, nm_ref, nv_ref):
        g = s_ref[0].astype(F32)
        for d in range(1, N_DEV):
            g = g + s_ref[d].astype(F32)
        nm = ADAM_B1 * m_ref[...] + (1.0 - ADAM_B1) * g
        nv = ADAM_B2 * v_ref[...] + (1.0 - ADAM_B2) * (g * g)
        g_ref[...] = g
        nm_ref[...] = nm
        nv_ref[...] = nv
        d_ref[...] = -ADAM_LR * ((nm / c1) / (jnp.sqrt(nv / c2) + ADAM_EPS) + ADAM_WD * w_ref[...])

    assert r % tr == 0, name
    blk = pl.BlockSpec((tr, cols), lambda i: (i, 0))
    return pl.pallas_call(
        body, grid=(r // tr,), in_specs=[blk, blk, blk, pl.BlockSpec((N_DEV, tr, cols), lambda i: (0, i, 0))],
        out_specs=[blk] * 4, out_shape=[S((r, cols), F32)] * 4, name=name, compiler_params=_params(("parallel",)),
    )(w, m, v, slabs)


BIG = ("w_in", "w_att_proj", "w_up", "w_ssm_proj", "w_out", "w_down", "conv_w")
ADAMW_ROWS = dict(w_in=256, w_att_proj=768, w_up=512, w_ssm_proj=256, w_out=128, w_down=256, conv_w=4)
SMALL = ("norm_mix_pre_w", "b_gate", "conv_b", "dt_bias", "a_log", "d_skip", "ssm_norm_w", "norm_mix_post_w",
         "norm_ffn_pre_w", "norm_ffn_post_w")
ORDER = ("norm_mix_pre_w", "w_in", "b_gate", "conv_w", "conv_b", "dt_bias", "a_log", "d_skip", "ssm_norm_w", "w_att_proj",
         "w_ssm_proj", "w_out", "norm_mix_post_w", "norm_ffn_pre_w", "w_up", "w_down", "norm_ffn_post_w")
ROW_SHARDED = ("w_ssm_proj", "w_out", "w_down")
LATE = ("w_att_proj", "w_ssm_proj", "w_out", "w_up", "w_down")
IN_PROJ_W = 10528
IN_SHARD_W = IN_PROJ_W // N_DEV
IN_SEGMENTS = ((2304, 4352), (8480, 10528), (4352, 8448), (0, 2304), (8448, 8480))


def _pack(parts, rows_multiple):
    flat = jnp.concatenate([p.reshape(-1) for p in parts])
    pad = (-flat.shape[0]) % (rows_multiple * LANE)
    return jnp.pad(flat, (0, pad)).reshape(-1, LANE)


def _unpack(flat2d, shapes):
    flat, out, off = flat2d.reshape(-1), [], 0
    for sh in shapes:
        n = int(np.prod(sh))
        out.append(flat[off:off + n].reshape(sh))
        off += n
    return out


def _reorder_in_proj(w):
    qkv, z, xbc = w[:, :2304], w[:, 2304:4352], w[:, 4352:8448]
    dt, gate = w[:, 8448:8480], w[:, 8480:10528]
    return jnp.concatenate([z, gate, xbc, qkv, dt, jnp.zeros((w.shape[0], PROJ_W - 10528), w.dtype)], axis=1)


def _restore_in_proj(wr):
    return jnp.concatenate([wr[:, OFF_QKV:OFF_QKV + 2304], wr[:, OFF_Z:OFF_Z + 2048], wr[:, OFF_XBC:OFF_XBC + 4096],
                            wr[:, OFF_DT:OFF_DT + 32], wr[:, OFF_GL:OFF_GL + 2048]], axis=1)


def _assemble_in_proj(g):
    pieces = []
    for lo, hi in IN_SEGMENTS:
        while lo < hi:
            d = lo // IN_SHARD_W
            end = min(hi, (d + 1) * IN_SHARD_W)
            pieces.append(g[d][:, lo - d * IN_SHARD_W:end - d * IN_SHARD_W])
            lo = end
    pieces.append(jnp.zeros((g.shape[1], PROJ_W - IN_PROJ_W), g.dtype))
    return jnp.concatenate(pieces, axis=1)


def _in_proj_slabs(wr):
    orig = _restore_in_proj(wr)
    return jnp.stack([orig[:, d * IN_SHARD_W:(d + 1) * IN_SHARD_W] for d in range(N_DEV)])


def kernel(x, norm_mix_pre_w, w_in, b_gate, conv_w, conv_b, dt_bias, a_log, d_skip, ssm_norm_w, w_att_proj, w_ssm_proj, w_out, norm_mix_post_w, norm_ffn_pre_w, w_up, w_down, norm_ffn_post_w, loss_target, m_norm_mix_pre_w, m_w_in, m_b_gate, m_conv_w, m_conv_b, m_dt_bias, m_a_log, m_d_skip, m_ssm_norm_w, m_w_att_proj, m_w_ssm_proj, m_w_out, m_norm_mix_post_w, m_norm_ffn_pre_w, m_w_up, m_w_down, m_norm_ffn_post_w, v_norm_mix_pre_w, v_w_in, v_b_gate, v_conv_w, v_conv_b, v_dt_bias, v_a_log, v_d_skip, v_ssm_norm_w, v_w_att_proj, v_w_ssm_proj, v_w_out, v_norm_mix_post_w, v_norm_ffn_pre_w, v_w_up, v_w_down, v_norm_ffn_post_w):
    w = dict(norm_mix_pre_w=norm_mix_pre_w, w_in=w_in, b_gate=b_gate, conv_w=conv_w, conv_b=conv_b, dt_bias=dt_bias, a_log=a_log,
             d_skip=d_skip, ssm_norm_w=ssm_norm_w, w_att_proj=w_att_proj, w_ssm_proj=w_ssm_proj, w_out=w_out,
             norm_mix_post_w=norm_mix_post_w, norm_ffn_pre_w=norm_ffn_pre_w, w_up=w_up, w_down=w_down, norm_ffn_post_w=norm_ffn_post_w)
    m = dict(norm_mix_pre_w=m_norm_mix_pre_w, w_in=m_w_in, b_gate=m_b_gate, conv_w=m_conv_w, conv_b=m_conv_b, dt_bias=m_dt_bias,
             a_log=m_a_log, d_skip=m_d_skip, ssm_norm_w=m_ssm_norm_w, w_att_proj=m_w_att_proj, w_ssm_proj=m_w_ssm_proj, w_out=m_w_out,
             norm_mix_post_w=m_norm_mix_post_w, norm_ffn_pre_w=m_norm_ffn_pre_w, w_up=m_w_up, w_down=m_w_down, norm_ffn_post_w=m_norm_ffn_post_w)
    v = dict(norm_mix_pre_w=v_norm_mix_pre_w, w_in=v_w_in, b_gate=v_b_gate, conv_w=v_conv_w, conv_b=v_conv_b, dt_bias=v_dt_bias,
             a_log=v_a_log, d_skip=v_d_skip, ssm_norm_w=v_ssm_norm_w, w_att_proj=v_w_att_proj, w_ssm_proj=v_w_ssm_proj, w_out=v_w_out,
             norm_mix_post_w=v_norm_mix_post_w, norm_ffn_pre_w=v_norm_ffn_pre_w, w_up=v_w_up, w_down=v_w_down, norm_ffn_post_w=v_norm_ffn_post_w)
    shard_shapes = {n: w[n].shape[1:] for n in ORDER}

    mx, my, mc = _mesh_pos()
    me = 4 * mx + 2 * my + mc

    g_in, g_conv = _all_gather([w["w_in"][0].astype(BF16), w["conv_w"][0]])
    conv_full = jnp.moveaxis(g_conv, 0, 1).reshape(SSM_CONV, CONV_DIM)
    late_shards = [w[n][0].astype(BF16) for n in LATE]
    late_handle, token = _remote_start(late_shards, False, "late_weights_start")
    w_pre = w["norm_mix_pre_w"] + jnp.tile(token[0:1, :], (1, D_MODEL // LANE))

    def late_weights(after):
        full = dict(zip(LATE, _with_own(_remote_wait(late_handle, after, "late_weights_wait"), late_shards, me)))
        for n in ROW_SHARDED:
            full[n] = full[n].reshape(-1, full[n].shape[2])
        w_att = jnp.moveaxis(full["w_att_proj"], 0, 1).reshape(ATT_WIDTH, D_MODEL)
        return w_att, full["w_ssm_proj"], full["w_out"], full["w_up"], full["w_down"]

    started = {}

    def start_exchange(tag, slabs):
        own = [lax.dynamic_index_in_dim(s, me, 0, keepdims=False) for s in slabs]
        handle, tok = _remote_start(slabs, True, tag + "_grads_start")
        started[tag] = (handle, own)
        return tok

    def on_mid_grads(g):
        slabs = dict(w_up=g["w_up"], w_att_proj=jnp.moveaxis(g["w_att_proj"].reshape(ATT_WIDTH, N_DEV, -1), 1, 0))
        for n in ROW_SHARDED:
            slabs[n] = g[n].reshape(N_DEV, -1, g[n].shape[1])
        return start_exchange("mid", [slabs[n] for n in LATE])

    def on_in_proj_grads(gw_in_r, g_conv_w):
        return start_exchange("in_proj", [_in_proj_slabs(gw_in_r), jnp.moveaxis(g_conv_w.reshape(SSM_CONV, N_DEV, -1), 1, 0)])

    loss, g_x, grads = _local_step(
        x[0], loss_target[0], w_pre, _assemble_in_proj(g_in), w["b_gate"], conv_full, w["conv_b"], w["dt_bias"], w["a_log"],
        w["d_skip"], w["ssm_norm_w"], late_weights, w["norm_mix_post_w"], w["norm_ffn_pre_w"], w["norm_ffn_post_w"],
        on_mid_grads, on_in_proj_grads)

    recv = {}
    for tag, names in (("mid", LATE), ("in_proj", ("w_in", "conv_w"))):
        handle, own = started[tag]
        recv.update(zip(names, _with_own(_remote_wait(handle, g_x, tag + "_grads_wait"), own, me)))
    small = _pack([grads[n].astype(F32) for n in SMALL], 8)
    _, small_all = _exchange_grads([], small)

    small_shapes = [shard_shapes[n] for n in SMALL]
    small_out = _adamw(*[_pack([d_[n][0] for n in SMALL], 8) for d_ in (w, m, v)], small_all, "adamw_replicated", small_all.shape[1])
    big_out = {n: _adamw(w[n][0], m[n][0], v[n][0], recv[n], "adamw_" + n, ADAMW_ROWS[n]) for n in BIG}
    res = []
    for which, small_flat in enumerate(small_out):
        vals = {n: big_out[n][which] for n in BIG}
        vals.update(zip(SMALL, _unpack(small_flat, small_shapes)))
        res.append([vals[n][None] for n in ORDER])
    g_out, d_out, m_out, v_out = res
    total = lax.psum(loss[0, 0], ("x", "y", "c"))
    return (total, g_x[None], *g_out, *d_out, *m_out, *v_out)
```

```python
import functools
import math

import jax
import jax.numpy as jnp
import numpy as np
from jax import lax
from jax.experimental import pallas as pl
from jax.experimental.pallas import tpu as pltpu

F32 = jnp.float32
BF16 = jnp.bfloat16

D_MODEL = 1024
HEAD_DIM = 64
N_ATT_HEADS = 12
ATT_WIDTH = N_ATT_HEADS * HEAD_DIM
DILATED_PATTERNS = ((128, 1), (512, 4), (2048, 16))
ATT_BLOCK = 128
SSM_INNER = 2048
SSM_HEAD_DIM = 64
SSM_HEADS = 32
SSM_GROUPS = 8
SSM_STATE = 128
SSM_CHUNK = 128
CONV_DIM = 4096
SSM_CONV = 4
FFN_HIDDEN = 4096
RMS_EPS = 1e-6
N_DEV = 8

ADAM_LR = 0.001
ADAM_B1 = 0.9
ADAM_B2 = 0.999
ADAM_EPS = 1e-08
ADAM_WD = 0.01
ADAM_STEP = 10

LANE = 128
OFF_Z, OFF_GL, OFF_XBC, OFF_QKV, OFF_DT = 0, 2048, 4096, 8192, 10496
PROJ_W = 10752
PROJ_BLOCKS = PROJ_W // LANE
PA_W = OFF_QKV
PB_W = PROJ_W - OFF_QKV
PB_DT = OFF_DT - OFF_QKV
VMEM_LIMIT = 52 * 1024 * 1024
NEG = -1e30

HI = lax.Precision.HIGHEST
NT_DIMS = (((1,), (1,)), ((), ()))
TN_DIMS = (((0,), (0,)), ((), ()))
S = jax.ShapeDtypeStruct


def _params(sem):
    return pltpu.CompilerParams(dimension_semantics=sem, vmem_limit_bytes=VMEM_LIMIT)


def _matmul(a, b, *, mode, out_dtype, name, tm, tn, tk, epilogue=None, extra=None, stacked=False, after=None, b_cols=None):
    if mode == "nn":
        m, k = a.shape
        n = b.shape[0] * b.shape[2] if stacked else b.shape[1]
        col0 = 0
        if b_cols is not None:
            assert b_cols[0] % tn == 0, name
            col0, n = b_cols[0] // tn, b_cols[1]
        a_spec = pl.BlockSpec((tm, tk), lambda i, j, kk: (i, kk))
        b_spec = pl.BlockSpec((None, tk, tn), lambda i, j, kk: (j, kk, 0)) if stacked else pl.BlockSpec((tk, tn), lambda i, j, kk: (kk, col0 + j))
        dims = (((1,), (0,)), ((), ()))
    elif mode == "nt":
        m, k = a.shape
        n = b.shape[1] if stacked else b.shape[0]
        a_spec = pl.BlockSpec((tm, tk), lambda i, j, kk: (i, kk))
        b_spec = pl.BlockSpec((None, tn, tk), lambda i, j, kk: (kk, j, 0)) if stacked else pl.BlockSpec((tn, tk), lambda i, j, kk: (j, kk))
        dims = NT_DIMS
    else:
        (k, m), n = a.shape, b.shape[1]
        a_spec = pl.BlockSpec((tk, tm), lambda i, j, kk: (kk, i))
        b_spec = pl.BlockSpec((tk, tn), lambda i, j, kk: (kk, j))
        dims = TN_DIMS
    assert m % tm == 0 and n % tn == 0 and k % tk == 0, (name, m, n, k)
    if stacked:
        assert (tk if mode == "nt" else tn) * N_DEV == (k if mode == "nt" else n), name
    nk = k // tk
    o_spec = pl.BlockSpec((tm, tn), lambda i, j, kk: (i, j))
    in_specs, args = [a_spec, b_spec], [a, b]
    if epilogue == "relu2":
        out_shape = (S((m, n), BF16), S((m, n), BF16))
        out_specs = (o_spec, o_spec)
    elif epilogue == "also_bf16":
        out_shape = (S((m, n), out_dtype), S((m, n), BF16))
        out_specs = (o_spec, o_spec)
    elif stacked and mode == "tn":
        out_shape, out_specs = S((N_DEV, m, tn), out_dtype), pl.BlockSpec((None, tm, tn), lambda i, j, kk: (j, i, 0))
    else:
        out_shape, out_specs = S((m, n), out_dtype), o_spec
    if epilogue == "relu2_bwd":
        in_specs.append(o_spec)
        args.append(extra)
    n_in = len(args)
    if after is not None:
        in_specs.append(pl.BlockSpec(after.shape, lambda i, j, kk: (0,) * after.ndim))
        args.append(after)

    def finish(acc, refs):
        if epilogue == "relu2":
            r = jnp.maximum(acc, 0.0)
            refs[0][...] = (r * r).astype(BF16)
            refs[1][...] = acc.astype(BF16)
        elif epilogue == "also_bf16":
            refs[0][...] = acc.astype(out_dtype)
            refs[1][...] = acc.astype(BF16)
        elif epilogue == "relu2_bwd":
            up = refs[0][...].astype(F32)
            refs[1][...] = (acc * (2.0 * jnp.maximum(up, 0.0))).astype(out_dtype)
        else:
            refs[0][...] = acc.astype(out_dtype)

    def body(a_ref, b_ref, *rest):
        rest = rest[:n_in - 2] + rest[len(args) - 2:]
        part = lax.dot_general(a_ref[...].astype(BF16), b_ref[...].astype(BF16), dims, preferred_element_type=F32)
        if nk == 1:
            finish(part, rest)
            return
        acc_ref = rest[-1]
        kk = pl.program_id(2)

        @pl.when(kk == 0)
        def _():
            acc_ref[...] = part

        @pl.when(kk > 0)
        def _():
            acc_ref[...] += part

        @pl.when(kk == nk - 1)
        def _():
            finish(acc_ref[...], rest[:-1])

    scratch = [] if nk == 1 else [pltpu.VMEM((tm, tn), F32)]
    return pl.pallas_call(
        body, grid=(m // tm, n // tn, nk), in_specs=in_specs, out_specs=out_specs, out_shape=out_shape,
        scratch_shapes=scratch, name=name, compiler_params=_params(("parallel", "parallel", "arbitrary")),
    )(*args)


def _rowcall(body, name, n_rows, tr, ins, outs, scratch=(), into=None):
    in_specs = [pl.BlockSpec(bs, im) for _, bs, im in ins]
    out_specs = [pl.BlockSpec(bs, im) for _, _, bs, im in outs]
    out_shape = [S(sh, dt) for sh, dt, _, _ in outs]
    args = [a for a, _, _ in ins]
    aliases = {}
    kernel = body
    if into is not None:
        buf, bs, im = into
        n_in = len(args)
        in_specs.append(pl.BlockSpec(memory_space=pl.ANY))
        args.append(buf)
        out_specs.append(pl.BlockSpec(bs, im))
        out_shape.append(S(buf.shape, buf.dtype))
        aliases = {n_in: len(out_shape) - 1}

        def kernel(*refs):
            body(*refs[:n_in], *refs[n_in + 1:])

    return pl.pallas_call(
        kernel, grid=(n_rows // tr,), in_specs=in_specs, out_specs=out_specs, out_shape=out_shape,
        input_output_aliases=aliases, scratch_shapes=list(scratch), name=name, compiler_params=_params(("arbitrary",)),
    )(*args)


def _rows(arr, tr, width=None, cb=0):
    width = arr.shape[1] if width is None else width
    return (arr, (tr, width), lambda i, cb=cb: (i, cb))


def _whole(arr):
    nd = arr.ndim
    return (arr, arr.shape, lambda i, nd=nd: (0,) * nd)


def _orow(n_rows, width, dtype, tr):
    return ((n_rows, width), dtype, (tr, width), lambda i: (i, 0))


def _oacc(width):
    return ((1, width), F32, (1, width), lambda i: (0, 0))


def _accumulate(ref, value):
    first = pl.program_id(0) == 0

    @pl.when(first)
    def _():
        ref[...] = value

    @pl.when(jnp.logical_not(first))
    def _():
        ref[...] += value


def _colsum(v):
    return jnp.sum(v, axis=0, keepdims=True)


def _rms_fwd(x, w):
    r = lax.rsqrt(jnp.mean(x * x, axis=-1, keepdims=True) + RMS_EPS)
    return x * r * w


def _rms_bwd(gy, x, w):
    r = lax.rsqrt(jnp.mean(x * x, axis=-1, keepdims=True) + RMS_EPS)
    xn = x * r
    gxn = gy * w
    gx = r * (gxn - xn * jnp.mean(gxn * xn, axis=-1, keepdims=True))
    return gx, _colsum(gy * xn)


def _sigmoid(x):
    return 1.0 / (1.0 + jnp.exp(-x))


def _head_expand(n_heads_pad, n_heads, width):
    h = lax.broadcasted_iota(jnp.int32, (n_heads_pad, n_heads * width), 0)
    c = lax.broadcasted_iota(jnp.int32, (n_heads_pad, n_heads * width), 1)
    return (c // width == h).astype(F32)


def _head_reduce(n_heads, width, n_heads_pad):
    c = lax.broadcasted_iota(jnp.int32, (n_heads * width, n_heads_pad), 0)
    h = lax.broadcasted_iota(jnp.int32, (n_heads * width, n_heads_pad), 1)
    return (c // width == h).astype(F32)


def _block_ones(n, width):
    r = lax.broadcasted_iota(jnp.int32, (n, n), 0)
    c = lax.broadcasted_iota(jnp.int32, (n, n), 1)
    return (r // width == c // width).astype(F32)


def _pre_norm(x, w_pre, tr=512):
    t = x.shape[0]

    def body(x_ref, w_ref, u_ref):
        u_ref[...] = _rms_fwd(x_ref[...], w_ref[...]).astype(BF16)

    return _rowcall(body, "pre_norm", t, tr, [_rows(x, tr), _whole(w_pre)], [_orow(t, D_MODEL, BF16, tr)])[0]


CONV_HALO = 16


def _row_shift(cur, halo, j):
    tr = cur.shape[0]
    r = lax.broadcasted_iota(jnp.int32, (tr, tr), 0)
    c = lax.broadcasted_iota(jnp.int32, (tr, tr), 1)
    main = jnp.dot((c == r + j).astype(BF16), cur, preferred_element_type=F32)
    er = lax.broadcasted_iota(jnp.int32, (CONV_HALO, CONV_HALO), 0)
    ec = lax.broadcasted_iota(jnp.int32, (CONV_HALO, CONV_HALO), 1)
    if j < 0:
        edge = jnp.dot((ec == CONV_HALO + er + j).astype(BF16), halo, preferred_element_type=F32)
        return jnp.concatenate([main[:CONV_HALO] + edge, main[CONV_HALO:]], axis=0)
    edge = jnp.dot((ec == er + j - CONV_HALO).astype(BF16), halo, preferred_element_type=F32)
    return jnp.concatenate([main[:tr - CONV_HALO], main[tr - CONV_HALO:] + edge], axis=0)


def _conv_fwd(proj, conv_w, conv_b, tr=256):
    t = proj.shape[0]
    cb = OFF_XBC // CONV_DIM
    halo = (proj, (CONV_HALO, CONV_DIM), lambda i: (jnp.maximum(i * (tr // CONV_HALO) - 1, 0), cb))

    def body(cur_ref, prev_ref, w_ref, b_ref, o_ref, xc_ref):
        cur = cur_ref[...]
        prev = jnp.where(pl.program_id(0) > 0, prev_ref[...], jnp.zeros_like(prev_ref[...]))
        acc = b_ref[...] + w_ref[3:4, :] * cur.astype(F32)
        for k in range(SSM_CONV - 1):
            acc = acc + w_ref[k:k + 1, :] * _row_shift(cur, prev, -(SSM_CONV - 1 - k))
        o_ref[...] = acc * _sigmoid(acc)
        xc_ref[...] = acc.astype(BF16)

    return _rowcall(body, "conv_fwd", t, tr, [_rows(proj, tr, CONV_DIM, cb), halo, _whole(conv_w), _whole(conv_b)],
                    [_orow(t, CONV_DIM, F32, tr), _orow(t, CONV_DIM, BF16, tr)])


def _dt_fwd(proj, dt_bias_pad, alog_pad, tr=512):
    t = proj.shape[0]

    def body(raw_ref, b_ref, al_ref, dtx_ref, csx_ref, cst_ref):
        v = raw_ref[...] + b_ref[...]
        dt = jnp.maximum(v, 0.0) + jnp.log1p(jnp.exp(-jnp.abs(v)))
        expand = _head_expand(LANE, SSM_HEADS, SSM_HEAD_DIM)
        dtx_ref[...] = _dot_split(dt, expand, 0, 3)
        la = dt * (-jnp.exp(al_ref[...]))
        row = lax.broadcasted_iota(jnp.int32, (SSM_CHUNK, SSM_CHUNK), 0)
        col = lax.broadcasted_iota(jnp.int32, (SSM_CHUNK, SSM_CHUNK), 1)
        tril = (col <= row).astype(F32)
        cs = jnp.concatenate([_dot_split(tril, la[k * SSM_CHUNK:(k + 1) * SSM_CHUNK, :], 1, 3) for k in range(tr // SSM_CHUNK)], axis=0)
        csx_ref[...] = _dot_split(cs, expand, 0, 3)
        cst_ref[...] = cs.T

    return _rowcall(body, "dt_fwd", t, tr, [_rows(proj, tr, LANE, PB_DT // LANE), _whole(dt_bias_pad), _whole(alog_pad)],
                    [_orow(t, SSM_INNER, F32, tr), _orow(t, SSM_INNER, F32, tr), ((LANE, t), F32, (LANE, tr), lambda i: (0, i))])


def _gate_norm_fwd(y_ssd, xa, proj, dskip_x, norm_w, tr=256):
    t = y_ssd.shape[0]
    gw = SSM_INNER // SSM_GROUPS

    def body(y_ref, xs_ref, z_ref, d_ref, w_ref, o_ref):
        z = z_ref[...].astype(F32)
        y3 = (y_ref[...].astype(F32) + d_ref[...] * xs_ref[...]) * (z * _sigmoid(z))
        for g in range(SSM_GROUPS):
            sl = slice(g * gw, (g + 1) * gw)
            o_ref[:, sl] = _rms_fwd(y3[:, sl], w_ref[:, sl]).astype(BF16)

    return _rowcall(body, "gate_norm_fwd", t, tr,
                    [_rows(y_ssd, tr), _rows(xa, tr, SSM_INNER, 0), _rows(proj, tr, SSM_INNER, OFF_Z // SSM_INNER), _whole(dskip_x), _whole(norm_w)],
                    [_orow(t, SSM_INNER, BF16, tr)])[0]


def _gating_fwd(proj, b_gate, att_p, ssm_p, tr=512):
    t = proj.shape[0]

    def body(gl_ref, b_ref, a_ref, s_ref, o_ref):
        gates = _sigmoid(gl_ref[...].astype(F32) + b_ref[...])
        o_ref[...] = (gates[:, :D_MODEL] * a_ref[...].astype(F32) + gates[:, D_MODEL:] * s_ref[...].astype(F32)).astype(BF16)

    return _rowcall(body, "gating_fwd", t, tr, [_rows(proj, tr, 2 * D_MODEL, OFF_GL // (2 * D_MODEL)), _whole(b_gate), _rows(att_p, tr), _rows(ssm_p, tr)],
                    [_orow(t, D_MODEL, BF16, tr)])[0]


def _mix_post_ffn_pre(x, mixed, w_post, w_fpre, tr=512):
    t = x.shape[0]

    def body(x_ref, m_ref, wp_ref, wf_ref, h1_ref, f_ref):
        h1 = x_ref[...] + _rms_fwd(m_ref[...], wp_ref[...])
        h1_ref[...] = h1
        f_ref[...] = _rms_fwd(h1, wf_ref[...]).astype(BF16)

    return _rowcall(body, "mix_post_ffn_pre", t, tr, [_rows(x, tr), _rows(mixed, tr), _whole(w_post), _whole(w_fpre)],
                    [_orow(t, D_MODEL, F32, tr), _orow(t, D_MODEL, BF16, tr)])


def _loss_and_ffn_post_bwd(h1, dn, w_fpost, target, tr=512):
    t = h1.shape[0]

    def body(h1_ref, dn_ref, w_ref, tg_ref, loss_ref, gh2_ref, gdn_ref, gw_ref):
        dn = dn_ref[...]
        w = w_ref[...]
        err = h1_ref[...] + _rms_fwd(dn, w) - tg_ref[...]
        _accumulate(loss_ref, jnp.zeros((1, LANE), F32) + 0.5 * jnp.sum(jnp.mean(err * err, axis=-1, keepdims=True)))
        gh2 = err * (1.0 / D_MODEL)
        gh2_ref[...] = gh2
        gdn, gw = _rms_bwd(gh2, dn, w)
        gdn_ref[...] = gdn.astype(BF16)
        _accumulate(gw_ref, gw)

    return _rowcall(body, "loss_ffn_post_bwd", t, tr, [_rows(h1, tr), _rows(dn, tr), _whole(w_fpost), _rows(target, tr)],
                    [_oacc(LANE), _orow(t, D_MODEL, F32, tr), _orow(t, D_MODEL, BF16, tr), _oacc(D_MODEL)])


def _ffn_pre_mix_post_bwd(g_h2, g_f, h1, w_fpre, mixed, w_post, tr=512):
    t = h1.shape[0]

    def body(gh2_ref, gf_ref, h1_ref, wf_ref, m_ref, wp_ref, gh1_ref, gm_ref, gwf_ref, gwp_ref):
        gx, gwf = _rms_bwd(gf_ref[...], h1_ref[...], wf_ref[...])
        gh1 = gh2_ref[...] + gx
        gh1_ref[...] = gh1
        gm, gwp = _rms_bwd(gh1, m_ref[...], wp_ref[...])
        gm_ref[...] = gm.astype(BF16)
        _accumulate(gwf_ref, gwf)
        _accumulate(gwp_ref, gwp)

    return _rowcall(body, "ffn_pre_mix_post_bwd", t, tr,
                    [_rows(g_h2, tr), _rows(g_f, tr), _rows(h1, tr), _whole(w_fpre), _rows(mixed, tr), _whole(w_post)],
                    [_orow(t, D_MODEL, F32, tr), _orow(t, D_MODEL, BF16, tr), _oacc(D_MODEL), _oacc(D_MODEL)])


def _gating_bwd(g_mixin, proj, b_gate, att_p, ssm_p, g_proj, tr=512):
    t = proj.shape[0]

    def body(gm_ref, gl_ref, b_ref, a_ref, s_ref, ga_ref, gs_ref, gb_ref, ggl_ref):
        gates = _sigmoid(gl_ref[...].astype(F32) + b_ref[...])
        gm = gm_ref[...].astype(F32)
        g_att, g_ssm = gates[:, :D_MODEL], gates[:, D_MODEL:]
        ga_ref[...] = (gm * g_att).astype(BF16)
        gs_ref[...] = (gm * g_ssm).astype(BF16)
        ggl_a = gm * a_ref[...].astype(F32) * g_att * (1.0 - g_att)
        ggl_s = gm * s_ref[...].astype(F32) * g_ssm * (1.0 - g_ssm)
        ggl_ref[:, :D_MODEL] = ggl_a.astype(BF16)
        ggl_ref[:, D_MODEL:] = ggl_s.astype(BF16)
        _accumulate(gb_ref.at[:, :D_MODEL], _colsum(ggl_a))
        _accumulate(gb_ref.at[:, D_MODEL:], _colsum(ggl_s))

    return _rowcall(body, "gating_bwd", t, tr,
                    [_rows(g_mixin, tr), _rows(proj, tr, 2 * D_MODEL, OFF_GL // (2 * D_MODEL)), _whole(b_gate), _rows(att_p, tr), _rows(ssm_p, tr)],
                    [_orow(t, D_MODEL, BF16, tr), _orow(t, D_MODEL, BF16, tr), _oacc(2 * D_MODEL)],
                    into=(g_proj, (tr, 2 * D_MODEL), lambda i: (i, OFF_GL // (2 * D_MODEL))))


def _gate_norm_bwd(g_y4, y_ssd, xa, proj, dskip_x, norm_w, g_proj, tr=256):
    t = y_ssd.shape[0]
    gw = SSM_INNER // SSM_GROUPS

    def body(g_ref, y_ref, xs_ref, z_ref, d_ref, w_ref, gy2_ref, gnw_ref, gdx_ref, gd_ref, gz_ref):
        z = z_ref[...].astype(F32)
        xs = xs_ref[...]
        sg = _sigmoid(z)
        sz = z * sg
        y2 = y_ref[...].astype(F32) + d_ref[...] * xs
        y3 = y2 * sz
        g4 = g_ref[...].astype(F32)
        for g in range(SSM_GROUPS):
            sl = slice(g * gw, (g + 1) * gw)
            gy3, gnw = _rms_bwd(g4[:, sl], y3[:, sl], w_ref[:, sl])
            _accumulate(gnw_ref.at[:, sl], gnw)
            gy2 = gy3 * sz[:, sl]
            gy2_ref[:, sl] = gy2
            gz_ref[:, sl] = (gy3 * y2[:, sl] * (sg[:, sl] * (1.0 + z[:, sl] * (1.0 - sg[:, sl])))).astype(BF16)
            _accumulate(gdx_ref.at[:, sl], _colsum(gy2 * xs[:, sl]))
        tot = jnp.broadcast_to(gdx_ref[...], (8, SSM_INNER))
        gd_ref[...] = jnp.dot(tot, _head_reduce(SSM_HEADS, SSM_HEAD_DIM, LANE), precision=HI, preferred_element_type=F32)[0:1, :]

    return _rowcall(body, "gate_norm_bwd", t, tr,
                    [_rows(g_y4, tr), _rows(y_ssd, tr), _rows(xa, tr, SSM_INNER, 0), _rows(proj, tr, SSM_INNER, OFF_Z // SSM_INNER), _whole(dskip_x), _whole(norm_w)],
                    [_orow(t, SSM_INNER, F32, tr), _oacc(SSM_INNER), _oacc(SSM_INNER), _oacc(LANE)],
                    into=(g_proj, (tr, SSM_INNER), lambda i: (i, OFF_Z // SSM_INNER)))


def _dt_bwd(g_dtx, ga_rows, proj, dt_bias_pad, g_proj, tr=512):
    t = proj.shape[0]
    tail = PROJ_W - OFF_DT

    def body(g_ref, ga_ref, raw_ref, b_ref, gb_ref, gal_ref, o_ref):
        red = _head_reduce(SSM_HEADS, SSM_HEAD_DIM, LANE)
        gdt = _dot_split(g_ref[...], red, 0, 3)
        graw = gdt * _sigmoid(raw_ref[...] + b_ref[...])
        o_ref[...] = jnp.concatenate([graw.astype(BF16), jnp.zeros((tr, tail - LANE), BF16)], axis=1)
        _accumulate(gb_ref, _colsum(graw))
        tot = jnp.broadcast_to(_colsum(ga_ref[...]), (8, SSM_INNER))
        gal_ref[...] = jnp.dot(tot, red, precision=HI, preferred_element_type=F32)[0:1, :]

    return _rowcall(body, "dt_bwd", t, tr, [_rows(g_dtx, tr), _whole(ga_rows), _rows(proj, tr, LANE, PB_DT // LANE), _whole(dt_bias_pad)],
                    [_oacc(LANE), _oacc(LANE)], into=(g_proj, (tr, tail), lambda i: (i, OFF_DT // tail)))


def _conv_bwd(g_xs, g_b, g_c, xc, proj, conv_w, g_proj, tr=256):
    t = proj.shape[0]
    n_blk = t // tr
    cb = OFF_XBC // CONV_DIM
    nb, nc = SSM_INNER, SSM_INNER + SSM_GROUPS * SSM_STATE
    def nxt(arr, width):
        return (arr, (CONV_HALO, width), lambda i: (jnp.minimum((i + 1) * (tr // CONV_HALO), t // CONV_HALO - 1), 0))

    def body(gxs_ref, gxs_n, gb_ref, gb_n, gc_ref, gc_n, xc_ref, xc_n, x_ref, w_ref, gcb_ref, gw0, gw1, gw2, gw3, o_ref):
        def gxc_of(gxs, gb, gc, xc, keep):
            xcf = xc[...].astype(F32)
            sg = _sigmoid(xcf)
            dsilu = jnp.where(keep, sg * (1.0 + xcf * (1.0 - sg)), 0.0)
            return jnp.concatenate([gxs[...] * dsilu[:, :nb], gb[...] * dsilu[:, nb:nc], gc[...] * dsilu[:, nc:]], axis=1)

        gxc = gxc_of(gxs_ref, gb_ref, gc_ref, xc_ref, True)
        gxc16 = gxc.astype(BF16)
        nxt16 = gxc_of(gxs_n, gb_n, gc_n, xc_n, pl.program_id(0) < n_blk - 1).astype(BF16)
        x = x_ref[...].astype(F32)
        acc = w_ref[3:4, :] * gxc
        _accumulate(gw3, _colsum(gxc * x))
        _accumulate(gcb_ref, _colsum(gxc))
        for k, gw in enumerate((gw0, gw1, gw2)):
            shifted = _row_shift(gxc16, nxt16, SSM_CONV - 1 - k)
            acc = acc + w_ref[k:k + 1, :] * shifted
            _accumulate(gw, _colsum(shifted * x))
        o_ref[...] = acc.astype(BF16)

    ins = []
    for arr, width in ((g_xs, SSM_INNER), (g_b, nc - nb), (g_c, nc - nb), (xc, CONV_DIM)):
        ins += [_rows(arr, tr), nxt(arr, width)]
    ins += [_rows(proj, tr, CONV_DIM, cb), _whole(conv_w)]
    return _rowcall(body, "conv_bwd", t, tr, ins, [_oacc(CONV_DIM)] * 5, into=(g_proj, (tr, CONV_DIM), lambda i: (i, cb)))


def _pre_norm_bwd(g_h1, g_u, x, w_pre, tr=512):
    t = x.shape[0]

    def body(gh_ref, gu_ref, x_ref, w_ref, gx_ref, gw_ref):
        gx, gw = _rms_bwd(gu_ref[...], x_ref[...], w_ref[...])
        gx_ref[...] = gh_ref[...] + gx
        _accumulate(gw_ref, gw)

    return _rowcall(body, "pre_norm_bwd", t, tr, [_rows(g_h1, tr), _rows(g_u, tr), _rows(x, tr), _whole(w_pre)],
                    [_orow(t, D_MODEL, F32, tr), _oacc(D_MODEL)])


def _alibi_slopes(n):
    def pow2(m):
        start = 2.0 ** (-8.0 / m)
        return [start ** (i + 1) for i in range(m)]
    if (n & (n - 1)) == 0:
        s = pow2(n)
    else:
        c = 2 ** int(math.floor(math.log2(n)))
        s = pow2(c) + pow2(2 * c)[0::2][: n - c]
    return np.array(s, dtype=np.float32)


def _slope_rows():
    s = _alibi_slopes(N_ATT_HEADS).reshape(N_ATT_HEADS // 2, 2)
    return jnp.asarray(np.broadcast_to(np.repeat(s, HEAD_DIM, axis=1)[:, None, :], (N_ATT_HEADS // 2, 8, LANE)).copy())


ATT_MAX_BLOCK_ROWS = 2048


RESIDUE_MAJOR_FROM = 16


class _AttLayout:
    def __init__(self, t, dil):
        self.t, self.dil = t, dil
        self.rows = t // dil
        self.residue_major = dil >= RESIDUE_MAJOR_FROM
        if self.residue_major:
            bq, self.stride = min(512, self.rows), 1
        else:
            bq, self.stride = min(512, self.rows, ATT_MAX_BLOCK_ROWS // dil), dil
        self.nsub = bq // ATT_BLOCK
        self.nblk = self.rows // bq
        self.rb = bq * self.stride
        self.pb = ATT_BLOCK * self.stride
        self.n_pb = self.rows * self.stride // self.pb
        self.out_dtype = F32 if self.stride > 1 else BF16

    def qkv(self, proj):
        pb, pb16 = proj
        if self.residue_major:
            return pb16.reshape(self.rows, self.dil * PB_W), PB_W // LANE, 0
        return (pb if self.stride > 1 else pb16), 0, 0

    def act(self, a):
        return a.reshape(self.rows, self.dil * ATT_WIDTH) if self.residue_major else a

    def act_shape(self):
        return (self.rows, self.dil * ATT_WIDTH) if self.residue_major else (self.t, ATT_WIDTH)

    def col(self, r, band, c):
        return r * band + c if self.residue_major else c


def _residue_rows(r, stride, first_block, n_blocks=1):
    if stride == 1:
        return pl.ds(first_block * ATT_BLOCK, n_blocks * ATT_BLOCK)
    return pl.ds(r + first_block * ATT_BLOCK * stride, n_blocks * ATT_BLOCK, stride=stride)


def _lane_half():
    return lax.broadcasted_iota(jnp.int32, (ATT_BLOCK, LANE), 1) // HEAD_DIM


def _att_scores_mask(dil, first):
    iq = lax.broadcasted_iota(jnp.int32, (ATT_BLOCK, 2 * ATT_BLOCK), 0)
    jk = lax.broadcasted_iota(jnp.int32, (ATT_BLOCK, 2 * ATT_BLOCK), 1)
    dist = ATT_BLOCK + iq - jk
    valid = (dist >= 0) & (dist <= ATT_BLOCK) & (jnp.logical_not(first) | (jk >= ATT_BLOCK))
    return (dist * dil).astype(F32), valid


def _stack_heads(x):
    half = _lane_half()
    return jnp.concatenate([jnp.where(half == 0, x, jnp.zeros_like(x)), jnp.where(half == 1, x, jnp.zeros_like(x))], axis=0)


def _unstack_heads(x):
    return jnp.where(_lane_half() == 0, x[:ATT_BLOCK], x[ATT_BLOCK:])


def _head_columns(x):
    return jnp.concatenate([x[:, 0:1], x[:, HEAD_DIM:HEAD_DIM + 1]], axis=0)


def _stacked_bias(sl_ref, dist, valid):
    d2 = jnp.concatenate([dist, dist], axis=0)
    v2 = jnp.concatenate([valid, valid], axis=0)
    top = lax.broadcasted_iota(jnp.int32, d2.shape, 0) < ATT_BLOCK
    slope = jnp.where(top, sl_ref[0:1, 0:1], sl_ref[0:1, HEAD_DIM:HEAD_DIM + 1])
    return jnp.where(v2, -slope * d2, NEG)


def _att_fwd(proj, dil, slopes):
    t = proj[0].shape[0]
    lay = _AttLayout(t, dil)
    nsub, nblk, rb, pb = lay.nsub, lay.nblk, lay.rb, lay.pb
    src, band, qb = lay.qkv(proj)
    aw = ATT_WIDTH // LANE

    def spec(off, prev=False):
        if prev:
            return pl.BlockSpec((pb, LANE), lambda hp, i, r: (jnp.maximum(i * nsub - 1, 0), lay.col(r, band, qb + off + hp)))
        return pl.BlockSpec((rb, LANE), lambda hp, i, r: (i, lay.col(r, band, qb + off + hp)))

    o_spec = pl.BlockSpec((rb, LANE), lambda hp, i, r: (i, lay.col(r, aw, hp)))

    def body(q_ref, kc_ref, kp_ref, vc_ref, vp_ref, sl_ref, o_ref, lse_ref):
        i, r = pl.program_id(1), pl.program_id(2)
        half = _lane_half()
        for sub in range(nsub):
            rs = _residue_rows(r, lay.stride, sub)
            q = (q_ref[rs, :] * (HEAD_DIM ** -0.5)).astype(BF16)
            if sub == 0:
                r0 = _residue_rows(r, lay.stride, 0)
                kk = jnp.concatenate([kp_ref[r0, :], kc_ref[rs, :]], axis=0).astype(BF16)
                vv = jnp.concatenate([vp_ref[r0, :], vc_ref[rs, :]], axis=0).astype(BF16)
                first = i == 0
            else:
                ks = _residue_rows(r, lay.stride, sub - 1, 2)
                kk, vv = kc_ref[ks, :].astype(BF16), vc_ref[ks, :].astype(BF16)
                first = jnp.bool_(False)
            dist, valid = _att_scores_mask(dil, first)
            s = lax.dot_general(_stack_heads(q), kk, NT_DIMS, preferred_element_type=F32) + _stacked_bias(sl_ref, dist, valid)
            m = jnp.max(s, axis=-1, keepdims=True)
            p = jnp.exp(s - m)
            l = jnp.sum(p, axis=-1, keepdims=True)
            o_ref[rs, :] = _unstack_heads(jnp.dot(p.astype(BF16), vv, preferred_element_type=F32) / l).astype(lay.out_dtype)
            lse_ref[rs, :] = _unstack_heads(jnp.broadcast_to(m + jnp.log(l), (2 * ATT_BLOCK, LANE)))

    o, lse = pl.pallas_call(
        body, grid=(N_ATT_HEADS // 2, nblk, dil),
        in_specs=[spec(0), spec(6), spec(6, True), spec(12), spec(12, True), pl.BlockSpec((None, 8, LANE), lambda hp, i, r: (hp, 0, 0))],
        out_specs=[o_spec, o_spec], out_shape=[S(lay.act_shape(), lay.out_dtype), S(lay.act_shape(), F32)],
        name=f"att_fwd_d{dil}", compiler_params=_params(("parallel", "parallel", "arbitrary")),
    )(src, src, src, src, src, slopes)
    return o.reshape(t, ATT_WIDTH), lse.reshape(t, ATT_WIDTH)


def _att_combine(outs, lses, tr=512):
    t = outs[0].shape[0]

    def body(o0, o1, o2, l0, l1, l2, att_ref, lse_ref):
        ls = [l0[...], l1[...], l2[...]]
        m = jnp.maximum(jnp.maximum(ls[0], ls[1]), ls[2])
        ws = [jnp.exp(l - m) for l in ls]
        tot = ws[0] + ws[1] + ws[2]
        num = ws[0] * o0[...].astype(F32) + ws[1] * o1[...].astype(F32) + ws[2] * o2[...].astype(F32)
        att_ref[...] = (num / tot).astype(BF16)
        lse_ref[...] = m + jnp.log(tot)

    return _rowcall(body, "att_combine", t, tr, [_rows(a, tr) for a in list(outs) + list(lses)],
                    [_orow(t, ATT_WIDTH, BF16, tr), _orow(t, ATT_WIDTH, F32, tr)])


def _att_delta(g_att, att, tr=512):
    t = att.shape[0]

    def body(g_ref, a_ref, o_ref):
        prod = g_ref[...] * a_ref[...].astype(F32)
        o_ref[...] = _dot_split(prod, _block_ones(ATT_WIDTH, HEAD_DIM), 0, 3)

    return _rowcall(body, "att_delta", t, tr, [_rows(g_att, tr), _rows(att, tr)], [_orow(t, ATT_WIDTH, F32, tr)])[0]


def _att_bwd(proj, g_att, lse, delta, dil, slopes):
    t = proj[0].shape[0]
    lay = _AttLayout(t, dil)
    nsub, nblk, rb, pb, n_pb = lay.nsub, lay.nblk, lay.rb, lay.pb, lay.n_pb
    src, band, qb = lay.qkv(proj)
    aw = ATT_WIDTH // LANE

    def near(i, which):
        return jnp.maximum(i * nsub - 1, 0) if which == "prev" else jnp.minimum((i + 1) * nsub, n_pb - 1)

    def pspec(off, which=None):
        if which:
            return pl.BlockSpec((pb, LANE), lambda hp, i, r: (near(i, which), lay.col(r, band, qb + off + hp)))
        return pl.BlockSpec((rb, LANE), lambda hp, i, r: (i, lay.col(r, band, qb + off + hp)))

    def aspec(which=None):
        if which:
            return pl.BlockSpec((pb, LANE), lambda hp, i, r: (near(i, which), lay.col(r, aw, hp)))
        return pl.BlockSpec((rb, LANE), lambda hp, i, r: (i, lay.col(r, aw, hp)))

    scale = HEAD_DIM ** -0.5

    def body(q_ref, qn_ref, kc_ref, kp_ref, vc_ref, vp_ref, do_ref, don_ref, lse_ref, lsen_ref, dl_ref, dln_ref, sl_ref,
             dq_ref, dk_ref, dv_ref):
        i, r = pl.program_id(1), pl.program_id(2)
        half = _lane_half()

        def tile_grads(q, do, lse_q, dl_q, kk, vv, dist, valid):
            q2, do2 = _stack_heads(q), _stack_heads(do)
            s = lax.dot_general(q2, kk, NT_DIMS, preferred_element_type=F32) + _stacked_bias(sl_ref, dist, valid)
            p = jnp.exp(s - _head_columns(lse_q))
            dp = lax.dot_general(do2, vv, NT_DIMS, preferred_element_type=F32)
            ds16 = (p * (dp - _head_columns(dl_q))).astype(BF16)
            dq = _unstack_heads(jnp.dot(ds16, kk, preferred_element_type=F32)) * scale
            dk = lax.dot_general(ds16, q2, TN_DIMS, preferred_element_type=F32)
            dv = lax.dot_general(p.astype(BF16), do2, TN_DIMS, preferred_element_type=F32)
            return dq, dk, dv

        carry_k = carry_v = None
        for sub in range(nsub):
            rs = _residue_rows(r, lay.stride, sub)
            q = (q_ref[rs, :] * scale).astype(BF16)
            do = do_ref[rs, :].astype(BF16)
            if sub == 0:
                r0 = _residue_rows(r, lay.stride, 0)
                kk = jnp.concatenate([kp_ref[r0, :], kc_ref[rs, :]], axis=0).astype(BF16)
                vv = jnp.concatenate([vp_ref[r0, :], vc_ref[rs, :]], axis=0).astype(BF16)
                first = i == 0
            else:
                ks = _residue_rows(r, lay.stride, sub - 1, 2)
                kk, vv = kc_ref[ks, :].astype(BF16), vc_ref[ks, :].astype(BF16)
                first = jnp.bool_(False)
            dist, valid = _att_scores_mask(dil, first)
            dq, dk2, dv2 = tile_grads(q, do, lse_ref[rs, :], dl_ref[rs, :], kk, vv, dist, valid)
            dq_ref[rs, :] = dq.astype(lay.out_dtype)
            if sub > 0:
                rp = _residue_rows(r, lay.stride, sub - 1)
                dk_ref[rp, :] = (carry_k + dk2[:ATT_BLOCK, :]).astype(lay.out_dtype)
                dv_ref[rp, :] = (carry_v + dv2[:ATT_BLOCK, :]).astype(lay.out_dtype)
            carry_k, carry_v = dk2[ATT_BLOCK:, :], dv2[ATT_BLOCK:, :]
        rl = _residue_rows(r, lay.stride, nsub - 1)
        rn = _residue_rows(r, lay.stride, 0)
        iq = lax.broadcasted_iota(jnp.int32, (ATT_BLOCK, ATT_BLOCK), 0)
        jk = lax.broadcasted_iota(jnp.int32, (ATT_BLOCK, ATT_BLOCK), 1)
        dist_i = ATT_BLOCK + iq - jk
        valid = (dist_i >= 0) & (dist_i <= ATT_BLOCK) & (i < nblk - 1)
        qn = (qn_ref[rn, :] * scale).astype(BF16)
        _, dk1, dv1 = tile_grads(qn, don_ref[rn, :].astype(BF16), lsen_ref[rn, :], dln_ref[rn, :],
                                 kc_ref[rl, :].astype(BF16), vc_ref[rl, :].astype(BF16), (dist_i * dil).astype(F32), valid)
        dk_ref[rl, :] = (carry_k + dk1).astype(lay.out_dtype)
        dv_ref[rl, :] = (carry_v + dv1).astype(lay.out_dtype)

    gv, lv, dlv = lay.act(g_att), lay.act(lse), lay.act(delta)
    dq, dk, dv = pl.pallas_call(
        body, grid=(N_ATT_HEADS // 2, nblk, dil),
        in_specs=[pspec(0), pspec(0, "next"), pspec(6), pspec(6, "prev"), pspec(12), pspec(12, "prev"),
                  aspec(), aspec("next"), aspec(), aspec("next"), aspec(), aspec("next"),
                  pl.BlockSpec((None, 8, LANE), lambda hp, i, r: (hp, 0, 0))],
        out_specs=[aspec(), aspec(), aspec()], out_shape=[S(lay.act_shape(), lay.out_dtype)] * 3,
        name=f"att_bwd_d{dil}", compiler_params=_params(("parallel", "parallel", "arbitrary")),
    )(src, src, src, src, src, src, gv, gv, lv, lv, dlv, dlv, slopes)
    return dq.reshape(t, ATT_WIDTH), dk.reshape(t, ATT_WIDTH), dv.reshape(t, ATT_WIDTH)


def _att_grad_sum(dqs, dks, dvs, g_proj, tr=1024):
    t = dqs[0].shape[0]
    cw = 2 * LANE
    per = ATT_WIDTH // cw
    arrays = list(dqs) + list(dks) + list(dvs)
    n_pat = len(dqs)

    def body(*refs):
        o_ref = refs[-1]
        which = pl.program_id(1) // per
        tot = jnp.zeros((tr, cw), F32)
        for s in range(3):
            part = refs[s * n_pat][...].astype(F32)
            for g in range(1, n_pat):
                part = part + refs[s * n_pat + g][...].astype(F32)
            tot = jnp.where(which == s, part, tot)
        o_ref[...] = tot.astype(BF16)

    in_specs = [pl.BlockSpec((tr, cw), lambda i, c, s=s: (i, jnp.clip(c - per * s, 0, per - 1))) for s in range(3) for _ in range(n_pat)]
    in_specs.append(pl.BlockSpec(memory_space=pl.ANY))
    return pl.pallas_call(
        lambda *refs: body(*refs[:len(arrays)], refs[-1]), grid=(t // tr, 3 * per), in_specs=in_specs,
        out_specs=pl.BlockSpec((tr, cw), lambda i, c: (i, OFF_QKV // cw + c)), out_shape=S(g_proj.shape, g_proj.dtype),
        input_output_aliases={len(arrays): 0}, name="att_grad_sum", compiler_params=_params(("arbitrary", "arbitrary")),
    )(*arrays, g_proj)


def _ssd_common(xs, dtx, cs, cs_t):
    ch = SSM_CHUNK
    row = lax.broadcasted_iota(jnp.int32, (ch, ch), 0)
    col = lax.broadcasted_iota(jnp.int32, (ch, ch), 1)
    cs_last = cs[ch - 1:ch, :]
    return dict(tril=col <= row, row=row, col=col, cs=cs, cs_t=cs_t, cs_last=cs_last,
                e=jnp.exp(cs), w=jnp.exp(cs_last - cs), xd=xs * dtx)


def _dot_split(a, b, split, terms=2):
    ops = [a, b]
    rest = ops[split]
    other = ops[1 - split].astype(BF16)
    out = None
    for _ in range(terms):
        piece = rest.astype(BF16)
        rest = rest - piece.astype(F32)
        part = jnp.dot(other, piece, preferred_element_type=F32) if split == 1 else jnp.dot(piece, other, preferred_element_type=F32)
        out = part if out is None else out + part
    return out


def _decay_col(cs_t, heads_per_group):
    r = lax.broadcasted_iota(jnp.int32, (heads_per_group * SSM_HEAD_DIM, SSM_STATE), 0) // SSM_HEAD_DIM
    out = jnp.zeros((heads_per_group * SSM_HEAD_DIM, SSM_STATE), F32)
    for j in range(heads_per_group):
        out = jnp.where(r == j, jnp.exp(cs_t[j:j + 1, SSM_CHUNK - 1:SSM_CHUNK]), out)
    return out


SSD_GROUPS_PER_STEP = 2


def _ssd_specs(t):
    hg = SSM_HEADS // SSM_GROUPS
    gw = hg * SSM_HEAD_DIM
    nb0 = SSM_INNER // SSM_STATE
    return hg, gw, nb0


def _ssd_group_views(gi, gw, wide, narrow, stacked):
    w = [r.at[:, pl.ds(gi * gw, gw)] for r in wide]
    n = [r.at[:, pl.ds(gi * SSM_STATE, SSM_STATE)] for r in narrow]
    return w, n, [r.at[gi] for r in stacked]


def _ssd_fwd(xa, dtx, csx, cst_g):
    t = xa.shape[0]
    nch = t // SSM_CHUNK
    hg, gw, nb0 = _ssd_specs(t)
    ch = SSM_CHUNK
    gp = SSD_GROUPS_PER_STEP

    def body(xs_ref, b_ref, c_ref, dtx_ref, cs_ref, cst_ref, y_ref, st_ref, h_scr):
        for gi in range(gp):
            (xs_g, dtx_g, cs_g, y_g), (b_g, c_g), (cst_gi, st_g) = _ssd_group_views(
                gi, gw, (xs_ref, dtx_ref, cs_ref, y_ref), (b_ref, c_ref), (cst_ref, st_ref))
            group_body(pl.program_id(0), pl.program_id(1) * gp + gi, xs_g, b_g, c_g, dtx_g, cs_g, cst_gi, y_g, st_g, h_scr)

    def group_body(cc, g, xs_ref, b_ref, c_ref, dtx_ref, cs_ref, cst_ref, y_ref, st_ref, h_scr):
        @pl.when(cc == 0)
        def _():
            h_scr[g] = jnp.zeros((gw, SSM_STATE), F32)

        q = _ssd_common(xs_ref[...], dtx_ref[...], cs_ref[...], cst_ref[...])
        bb, cb = b_ref[...].astype(BF16), c_ref[...].astype(BF16)
        cbm = lax.dot_general(cb, bb, NT_DIMS, preferred_element_type=F32)
        h = h_scr[g]
        st_ref[...] = h
        xd16 = q["xd"].astype(BF16)
        y = lax.dot_general(cb, h.astype(BF16), NT_DIMS, preferred_element_type=F32) * q["e"]
        lane_head = lax.broadcasted_iota(jnp.int32, (ch, gw), 1) // SSM_HEAD_DIM
        gmats, xds = [], []
        for j in range(hg):
            diff = q["cs"][:, j * SSM_HEAD_DIM:j * SSM_HEAD_DIM + 1] - q["cs_t"][j:j + 1, :]
            gmats.append((cbm * jnp.exp(jnp.where(q["tril"], diff, NEG))).astype(BF16))
            xds.append(jnp.where(lane_head == j, xd16, jnp.zeros_like(xd16)))
        y = y + jnp.dot(jnp.concatenate(gmats, axis=1), jnp.concatenate(xds, axis=0), preferred_element_type=F32)
        y_ref[...] = y.astype(BF16)
        s_new = lax.dot_general((q["xd"] * q["w"]).astype(BF16), bb, TN_DIMS, preferred_element_type=F32)
        h_scr[g] = _decay_col(q["cs_t"], hg) * h + s_new

    wide = pl.BlockSpec((ch, gp * gw), lambda cc, g: (cc, g))
    return pl.pallas_call(
        body, grid=(nch, SSM_GROUPS // gp),
        in_specs=[wide,
                  pl.BlockSpec((ch, gp * SSM_STATE), lambda cc, g: (cc, nb0 // gp + g)),
                  pl.BlockSpec((ch, gp * SSM_STATE), lambda cc, g: (cc, (nb0 + SSM_GROUPS) // gp + g)),
                  wide, wide,
                  pl.BlockSpec((gp, 8, ch), lambda cc, g: (g, 0, cc))],
        out_specs=[wide, pl.BlockSpec((None, gp, gw, SSM_STATE), lambda cc, g: (cc, g, 0, 0))],
        out_shape=[S((t, SSM_INNER), BF16), S((nch, SSM_GROUPS, gw, SSM_STATE), F32)],
        scratch_shapes=[pltpu.VMEM((SSM_GROUPS, gw, SSM_STATE), F32)],
        name="ssd_fwd", compiler_params=_params(("arbitrary", "arbitrary")),
    )(xa, xa, xa, dtx, csx, cst_g)


def _ssd_bwd(xa, dtx, csx, cst_g, alog_x, g_y, states, dskip_x):
    t = xa.shape[0]
    nch = t // SSM_CHUNK
    hg, gw, nb0 = _ssd_specs(t)
    ch = SSM_CHUNK
    gp = SSD_GROUPS_PER_STEP

    def rc(cc):
        return nch - 1 - cc

    def body(xs_ref, b_ref, c_ref, dtx_ref, cs_ref, cst_ref, alx_ref, gy_ref, st_ref, dsk_ref,
             gxs_ref, gb_ref, gc_ref, gdt_ref, ga_ref, gh_scr):
        for gi in range(gp):
            wide, narrow, stacked = _ssd_group_views(
                gi, gw, (xs_ref, dtx_ref, cs_ref, alx_ref, gy_ref, dsk_ref, gxs_ref, gdt_ref, ga_ref), (b_ref, c_ref, gb_ref, gc_ref),
                (cst_ref, st_ref))
            xs_g, dtx_g, cs_g, alx_g, gy_g, dsk_g, gxs_g, gdt_g, ga_g = wide
            b_g, c_g, gb_g, gc_g = narrow
            group_body(pl.program_id(0), pl.program_id(1) * gp + gi, xs_g, b_g, c_g, dtx_g, cs_g, stacked[0], alx_g, gy_g, stacked[1],
                       dsk_g, gxs_g, gb_g, gc_g, gdt_g, ga_g, gh_scr)

    def group_body(cc, g, xs_ref, b_ref, c_ref, dtx_ref, cs_ref, cst_ref, alx_ref, gy_ref, st_ref, dsk_ref,
                   gxs_ref, gb_ref, gc_ref, gdt_ref, ga_ref, gh_scr):
        @pl.when(cc == 0)
        def _():
            gh_scr[g] = jnp.zeros((gw, SSM_STATE), F32)

        xs, dtx = xs_ref[...], dtx_ref[...]
        q = _ssd_common(xs, dtx, cs_ref[...], cst_ref[...])
        cs, cs_t, e, w, xd = q["cs"], q["cs_t"], q["e"], q["w"], q["xd"]
        bb, cb = b_ref[...].astype(BF16), c_ref[...].astype(BF16)
        gy = gy_ref[...]
        gy16, xd16 = gy.astype(BF16), xd.astype(BF16)
        h = st_ref[...]
        h16 = h.astype(BF16)
        ghn = gh_scr[g]
        ghn16 = ghn.astype(BF16)
        seg = _block_ones(gw, SSM_HEAD_DIM)
        cbm = lax.dot_general(cb, bb, NT_DIMS, preferred_element_type=F32)

        gye16 = (gy * e).astype(BF16)
        chm = lax.dot_general(cb, h16, NT_DIMS, preferred_element_type=F32)
        g_c = jnp.dot(gye16, h16, preferred_element_type=F32)
        gh_off = lax.dot_general(gye16, cb, TN_DIMS, preferred_element_type=F32)
        bgs = lax.dot_general(bb, ghn16, NT_DIMS, preferred_element_type=F32)
        g_xd = w * bgs
        head_sums = _dot_split(jnp.concatenate([gy * chm, xd * bgs], axis=0), seg, 0)
        g_e, g_w = head_sums[:ch], head_sums[ch:]
        g_b = jnp.dot((xd * w).astype(BF16), ghn16, preferred_element_type=F32)
        decay = _decay_col(cs_t, hg)
        gh_scr[g] = decay * ghn + gh_off
        rsum = jnp.sum(ghn * h, axis=1, keepdims=True)
        lane_head = lax.broadcasted_iota(jnp.int32, (ch, gw), 1) // SSM_HEAD_DIM
        lane_head1 = lax.broadcasted_iota(jnp.int32, (1, gw), 1) // SSM_HEAD_DIM
        g_el = jnp.zeros((1, gw), F32)
        g_cs = g_e * e - g_w * w
        upper = q["row"] <= q["col"]
        lms, gys = [], []
        for j in range(hg):
            g_el = jnp.where(lane_head1 == j, jnp.sum(rsum[j * SSM_HEAD_DIM:(j + 1) * SSM_HEAD_DIM, :], axis=0, keepdims=True), g_el)
            csc = cs[:, j * SSM_HEAD_DIM:j * SSM_HEAD_DIM + 1]
            csr = cs_t[j:j + 1, :]
            lms.append(jnp.exp(jnp.where(q["tril"], csc - csr, NEG)))
            gys.append(jnp.where(lane_head == j, gy16, jnp.zeros_like(gy16)))
        lm_st, gy_st = jnp.concatenate(lms, axis=0), jnp.concatenate(gys, axis=0)
        cbm_st = jnp.concatenate([cbm] * hg, axis=0)
        gcb_st = lax.dot_general(gy_st, xd16, NT_DIMS, preferred_element_type=F32) * lm_st
        gcb_sum = gcb_st[0:ch]
        for j in range(1, hg):
            gcb_sum = gcb_sum + gcb_st[j * ch:(j + 1) * ch]
        gcb16 = gcb_sum.astype(BF16)
        g_c = g_c + jnp.dot(gcb16, bb, preferred_element_type=F32)
        g_b = g_b + lax.dot_general(gcb16, cb, TN_DIMS, preferred_element_type=F32)
        g_xd = g_xd + lax.dot_general((cbm_st * lm_st).astype(BF16), gy_st, TN_DIMS, preferred_element_type=F32)
        m_st = gcb_st * cbm_st
        for j in range(hg):
            m_ls = m_st[j * ch:(j + 1) * ch]
            d_cs = jnp.sum(m_ls, axis=1, keepdims=True) - jnp.sum(m_ls.T, axis=1, keepdims=True)
            g_cs = g_cs + jnp.where(lane_head == j, d_cs, 0.0)
        extra = _colsum(g_w * w) + g_el * jnp.exp(q["cs_last"])
        g_cs = g_cs + jnp.where(lax.broadcasted_iota(jnp.int32, (ch, gw), 0) == ch - 1, extra, 0.0)
        g_la = _dot_split(upper, g_cs, 1)
        a_x = -jnp.exp(alx_ref[...])
        gdt_ref[...] = g_xd * xs + g_la * a_x * (1.0 / SSM_HEAD_DIM)
        ga_row = _colsum(g_la * (dtx * a_x)) * (1.0 / SSM_HEAD_DIM)
        ga_ref[...] = jnp.where(lax.broadcasted_iota(jnp.int32, (8, gw), 0) == 0, ga_row, 0.0)
        gxs_ref[...] = g_xd * dtx + gy * dsk_ref[...]
        gb_ref[...] = g_b
        gc_ref[...] = g_c

    wide = pl.BlockSpec((ch, gp * gw), lambda cc, g: (rc(cc), g))
    narrow = pl.BlockSpec((ch, gp * SSM_STATE), lambda cc, g: (rc(cc), g))
    row = pl.BlockSpec((1, gp * gw), lambda cc, g: (0, g))
    return pl.pallas_call(
        body, grid=(nch, SSM_GROUPS // gp),
        in_specs=[wide,
                  pl.BlockSpec((ch, gp * SSM_STATE), lambda cc, g: (rc(cc), nb0 // gp + g)),
                  pl.BlockSpec((ch, gp * SSM_STATE), lambda cc, g: (rc(cc), (nb0 + SSM_GROUPS) // gp + g)),
                  wide, wide,
                  pl.BlockSpec((gp, 8, ch), lambda cc, g: (g, 0, rc(cc))),
                  row, wide,
                  pl.BlockSpec((None, gp, gw, SSM_STATE), lambda cc, g: (rc(cc), g, 0, 0)),
                  row],
        out_specs=[wide, narrow, narrow, wide, pl.BlockSpec((8, gp * gw), lambda cc, g: (rc(cc), g))],
        out_shape=[S((t, SSM_INNER), F32), S((t, SSM_GROUPS * SSM_STATE), F32), S((t, SSM_GROUPS * SSM_STATE), F32),
                   S((t, SSM_INNER), F32), S((nch * 8, SSM_INNER), F32)],
        scratch_shapes=[pltpu.VMEM((SSM_GROUPS, gw, SSM_STATE), F32)],
        name="ssd_bwd", compiler_params=_params(("arbitrary", "arbitrary")),
    )(xa, xa, xa, dtx, csx, cst_g, alog_x, g_y, states, dskip_x)


def _local_step(x, target, w_pre, w_in_r, b_gate, conv_w, conv_b, dt_bias, a_log, d_skip, ssm_norm_w,
                late_weights, w_post, w_fpre, w_fpost, on_mid_grads, on_in_proj_grads):
    t = x.shape[0]
    mm = functools.partial(_matmul, tm=512)
    slopes = _slope_rows()
    hg = SSM_HEADS // SSM_GROUPS
    dt_bias_pad = jnp.pad(dt_bias, ((0, 0), (0, LANE - SSM_HEADS)))
    alog_x = jnp.repeat(a_log, SSM_HEAD_DIM, axis=1)
    alog_pad = jnp.pad(a_log, ((0, 0), (0, LANE - SSM_HEADS)))
    dskip_x = jnp.repeat(d_skip, SSM_HEAD_DIM, axis=1)

    u = _pre_norm(x, w_pre)
    pa = mm(u, w_in_r, mode="nn", out_dtype=BF16, name="in_proj_zgx", tn=2048, tk=D_MODEL, b_cols=(0, PA_W))
    pb, pb16 = mm(u, w_in_r[:, PA_W:], mode="nn", out_dtype=F32, name="in_proj_qkvdt", tn=PB_W // 2, tk=D_MODEL, epilogue="also_bf16")
    fwd = [_att_fwd((pb, pb16), dil, slopes) for _, dil in DILATED_PATTERNS]
    att, lse = _att_combine([o for o, _ in fwd], [l for _, l in fwd])
    xa, xc = _conv_fwd(pa, conv_w, conv_b)
    dtx, csx, cst = _dt_fwd(pb, dt_bias_pad, alog_pad)
    cst_g = jnp.pad(cst[:SSM_HEADS].reshape(SSM_GROUPS, hg, t), ((0, 0), (0, 8 - hg), (0, 0)))
    y_ssd, states = _ssd_fwd(xa, dtx, csx, cst_g)
    y4 = _gate_norm_fwd(y_ssd, xa, pa, dskip_x, ssm_norm_w)
    w_att, w_ssm, w_out, w_up, w_down = late_weights(y4)
    att_p = mm(att, w_att, mode="nn", out_dtype=BF16, name="att_proj", tn=D_MODEL, tk=ATT_WIDTH)
    ssm_p = mm(y4, w_ssm, mode="nn", out_dtype=BF16, name="ssm_proj", tn=D_MODEL, tk=SSM_INNER)
    mixin = _gating_fwd(pa, b_gate, att_p, ssm_p)
    mixed = mm(mixin, w_out, mode="nn", out_dtype=F32, name="out_proj", tn=D_MODEL, tk=D_MODEL)
    h1, f = _mix_post_ffn_pre(x, mixed, w_post, w_fpre)
    act, up = _matmul(f, w_up, mode="nn", out_dtype=BF16, name="ffn_up", tm=2048, tn=FFN_HIDDEN // N_DEV, tk=D_MODEL, epilogue="relu2", stacked=True)
    dn = mm(act, w_down, mode="nn", out_dtype=F32, name="ffn_down", tn=D_MODEL, tk=FFN_HIDDEN)
    loss, g_h2, g_dn, gw_fpost = _loss_and_ffn_post_bwd(h1, dn, w_fpost, target)

    g_up = mm(g_dn, w_down, mode="nt", out_dtype=BF16, name="ffn_down_bwd_x", tn=2048, tk=D_MODEL, epilogue="relu2_bwd", extra=up)
    gw_down = _matmul(act, g_dn, mode="tn", out_dtype=BF16, name="ffn_down_bwd_w", tm=1024, tn=D_MODEL, tk=2048)
    w_up_rows = jnp.moveaxis(w_up, 0, 1).reshape(D_MODEL, FFN_HIDDEN)
    g_f = mm(g_up, w_up_rows, mode="nt", out_dtype=F32, name="ffn_up_bwd_x", tn=D_MODEL, tk=FFN_HIDDEN)
    gw_up = _matmul(f, g_up, mode="tn", out_dtype=BF16, name="ffn_up_bwd_w", tm=D_MODEL, tn=FFN_HIDDEN // N_DEV, tk=2048, stacked=True)
    g_h1, g_mixed, gw_fpre, gw_post = _ffn_pre_mix_post_bwd(g_h2, g_f, h1, w_fpre, mixed, w_post)
    g_mixin = mm(g_mixed, w_out, mode="nt", out_dtype=BF16, name="out_proj_bwd_x", tn=D_MODEL, tk=D_MODEL)
    gw_out = _matmul(mixin, g_mixed, mode="tn", out_dtype=BF16, name="out_proj_bwd_w", tm=D_MODEL, tn=D_MODEL, tk=2048)
    g_proj = lax.empty((t, PROJ_W), BF16)
    g_att_p, g_ssm_p, g_b_gate, g_proj = _gating_bwd(g_mixin, pa, b_gate, att_p, ssm_p, g_proj)
    g_att = mm(g_att_p, w_att, mode="nt", out_dtype=F32, name="att_proj_bwd_x", tn=ATT_WIDTH, tk=D_MODEL)
    gw_att = _matmul(att, g_att_p, mode="tn", out_dtype=BF16, name="att_proj_bwd_w", tm=ATT_WIDTH, tn=D_MODEL, tk=2048)
    g_y4 = mm(g_ssm_p, w_ssm, mode="nt", out_dtype=BF16, name="ssm_proj_bwd_x", tn=SSM_INNER, tk=D_MODEL)
    gw_ssm = _matmul(y4, g_ssm_p, mode="tn", out_dtype=BF16, name="ssm_proj_bwd_w", tm=1024, tn=D_MODEL, tk=2048)
    token = on_mid_grads(dict(w_att_proj=gw_att, w_ssm_proj=gw_ssm, w_out=gw_out, w_up=gw_up, w_down=gw_down))
    if token is not None:
        ssm_norm_w = ssm_norm_w + jnp.tile(token[0:1, :], (1, SSM_INNER // LANE))
    g_y2, g_norm_w, _, g_d_skip, g_proj = _gate_norm_bwd(g_y4, y_ssd, xa, pa, dskip_x, ssm_norm_w, g_proj)
    g_xs, g_bm, g_cm, g_dtx, ga_rows = _ssd_bwd(xa, dtx, csx, cst_g, alog_x, g_y2, states, dskip_x)
    g_dt_bias, g_a_log, g_proj = _dt_bwd(g_dtx, ga_rows, pb, dt_bias_pad, g_proj)
    g_conv_b, gcw0, gcw1, gcw2, gcw3, g_proj = _conv_bwd(g_xs, g_bm, g_cm, xc, pa, conv_w, g_proj)
    delta = _att_delta(g_att, att)
    dqs, dks, dvs = [], [], []
    for _, dil in DILATED_PATTERNS:
        dq, dk, dv = _att_bwd((pb, pb16), g_att, lse, delta, dil, slopes)
        dqs.append(dq)
        dks.append(dk)
        dvs.append(dv)
    g_proj = _att_grad_sum(dqs, dks, dvs, g_proj)
    gw_in_r = _matmul(u, g_proj, mode="tn", out_dtype=BF16, name="in_proj_bwd_w", tm=D_MODEL, tn=1792, tk=2048)
    token = on_in_proj_grads(gw_in_r, jnp.concatenate([gcw0, gcw1, gcw2, gcw3], axis=0))
    g_u = _matmul(g_proj, w_in_r, mode="nt", out_dtype=F32, name="in_proj_bwd_x", tm=1024, tn=D_MODEL, tk=3584, after=token)
    g_x, gw_pre = _pre_norm_bwd(g_h1, g_u, x, w_pre)

    grads = dict(
        norm_mix_pre_w=gw_pre, b_gate=g_b_gate, conv_b=g_conv_b, dt_bias=g_dt_bias[:, :SSM_HEADS], a_log=g_a_log[:, :SSM_HEADS],
        d_skip=g_d_skip[:, :SSM_HEADS], ssm_norm_w=g_norm_w, norm_mix_post_w=gw_post, norm_ffn_pre_w=gw_fpre, norm_ffn_post_w=gw_fpost)
    return loss, g_x, grads


def _mesh_pos():
    return lax.axis_index("x"), lax.axis_index("y"), lax.axis_index("c")


def _all_gather(shards):
    n = len(shards)

    def body(*refs):
        x_refs, o_refs = refs[:n], refs[n:2 * n]
        send_sems, recv_sems, local_sems = refs[2 * n:]
        x, y, c = _mesh_pos()
        me, sibling = (x, y, c), (x, y, 1 - c)
        chips = [(1 - x, y), (x, 1 - y), (1 - x, 1 - y)]

        def copy(a, k, block, to, src=None):
            dst = o_refs[a].at[4 * block[0] + 2 * block[1] + block[2]]
            return pltpu.make_async_remote_copy(
                src_ref=dst if src is None else src, dst_ref=dst, send_sem=send_sems.at[7 * a + k], recv_sem=recv_sems.at[7 * a + k],
                device_id=to, device_id_type=pl.DeviceIdType.MESH)

        mine = [pltpu.make_async_copy(x_refs[a], o_refs[a].at[4 * x + 2 * y + c], local_sems.at[a]) for a in range(n)]
        for cp in mine:
            cp.start()
        first = []
        for a in range(n):
            first.append(copy(a, 0, me, sibling, src=x_refs[a]))
            first += [copy(a, 1 + j, me, (*chip, c), src=x_refs[a]) for j, chip in enumerate(chips)]
        for cp in first:
            cp.start()
        passed = []
        for j, chip in enumerate(chips):
            for a in range(n):
                copy(a, 1 + j, (*chip, c), me).wait_recv()
                passed.append(copy(a, 4 + j, (*chip, c), sibling))
                passed[-1].start()
        for a in range(n):
            copy(a, 0, sibling, me).wait_recv()
            for j, chip in enumerate(chips):
                copy(a, 4 + j, (*chip, 1 - c), me).wait_recv()
        for cp in first + passed:
            cp.wait_send()
        for cp in mine:
            cp.wait()

    hbm = pl.BlockSpec(memory_space=pltpu.HBM)
    return pl.pallas_call(
        body, out_shape=[S((N_DEV,) + s.shape, s.dtype) for s in shards],
        in_specs=[hbm] * n, out_specs=[hbm] * n,
        scratch_shapes=[pltpu.SemaphoreType.DMA((7 * n,)), pltpu.SemaphoreType.DMA((7 * n,)), pltpu.SemaphoreType.DMA((n,))],
        name="weights_all_gather",
    )(*shards)


def _exchange_grads(slab_arrays, small):
    n = len(slab_arrays)
    r_small = small.shape[0]

    def body(*refs):
        slab_refs, small_ref = refs[:n], refs[n]
        recv_refs, gsm_ref = refs[n + 1:2 * n + 1], refs[2 * n + 1]
        send_sems, recv_sems, local_sems = refs[2 * n + 2:]
        x, y, c = _mesh_pos()
        me = 4 * x + 2 * y + c

        def peer(k):
            px = 1 - x if k & 4 else x
            py = 1 - y if k & 2 else y
            pc = 1 - c if k & 1 else c
            return (px, py, pc), 4 * px + 2 * py + pc

        def copy(a, k, sending):
            to, lin = peer(k)
            sem = 7 * a + k - 1
            if a == n:
                src, dst = small_ref, gsm_ref.at[me if sending else lin]
            else:
                src, dst = slab_refs[a].at[lin], recv_refs[a].at[me if sending else lin]
            return pltpu.make_async_remote_copy(src_ref=src, dst_ref=dst, send_sem=send_sems.at[sem], recv_sem=recv_sems.at[sem],
                                                device_id=to, device_id_type=pl.DeviceIdType.MESH)

        own = [pltpu.make_async_copy(slab_refs[a].at[me], recv_refs[a].at[me], local_sems.at[a]) for a in range(n)]
        own.append(pltpu.make_async_copy(small_ref, gsm_ref.at[me], local_sems.at[n]))
        for cp in own:
            cp.start()
        order = [n] + list(range(n))
        sends = [copy(a, k, True) for a in order for k in range(1, N_DEV)]
        for cp in sends:
            cp.start()
        for a in order:
            for k in range(1, N_DEV):
                copy(a, k, False).wait_recv()
        for cp in sends:
            cp.wait_send()
        for cp in own:
            cp.wait()

    hbm = pl.BlockSpec(memory_space=pltpu.HBM)
    n_sem = 7 * (n + 1)
    res = pl.pallas_call(
        body, out_shape=[S(a.shape, a.dtype) for a in slab_arrays] + [S((N_DEV, r_small, LANE), small.dtype)],
        in_specs=[hbm] * (n + 1), out_specs=[hbm] * (n + 1),
        scratch_shapes=[pltpu.SemaphoreType.DMA((n_sem,)), pltpu.SemaphoreType.DMA((n_sem,)), pltpu.SemaphoreType.DMA((n + 1,))],
        name="grad_exchange",
    )(*slab_arrays, small)
    return res[:n], res[n]


def _peer_of(k, x, y, c):
    px = 1 - x if k & 4 else x
    py = 1 - y if k & 2 else y
    pc = 1 - c if k & 1 else c
    return (px, py, pc), 4 * px + 2 * py + pc


def _split_copies(src_refs, land_refs, send_sems, recv_sems, per_peer):
    x, y, c = _mesh_pos()
    me = 4 * x + 2 * y + c
    sends, recvs = [], []
    for a, (src, land) in enumerate(zip(src_refs, land_refs)):
        for k in range(1, N_DEV):
            to, lin = _peer_of(k, x, y, c)
            sem = 7 * a + k - 1
            piece = src.at[lin] if per_peer else src
            for slot, out in ((me, sends), (lin, recvs)):
                out.append(pltpu.make_async_remote_copy(
                    src_ref=piece, dst_ref=land.at[slot], send_sem=send_sems.at[sem], recv_sem=recv_sems.at[sem],
                    device_id=to, device_id_type=pl.DeviceIdType.MESH))
    return sends, recvs


def _remote_start(srcs, per_peer, name):
    n = len(srcs)
    lands = [lax.empty((N_DEV,) + (s.shape[1:] if per_peer else s.shape), s.dtype) for s in srcs]

    def body(*refs):
        src_refs, land_refs = refs[:n], refs[n:2 * n]
        send_sems, recv_sems = refs[2 * n], refs[2 * n + 1]
        token = refs[-1]
        sends, _ = _split_copies(src_refs, land_refs, send_sems, recv_sems, per_peer)
        for cp in sends:
            cp.start()
        token[...] = jnp.zeros_like(token)

    hbm = pl.BlockSpec(memory_space=pltpu.HBM)
    sem = pl.BlockSpec(memory_space=pltpu.SEMAPHORE)
    res = pl.pallas_call(
        body, name=name,
        out_shape=(pltpu.SemaphoreType.DMA((7 * n,)), pltpu.SemaphoreType.DMA((7 * n,)),
                   *[pltpu.HBM(a.shape, a.dtype) for a in srcs + lands], S((8, LANE), F32)),
        in_specs=[hbm] * (2 * n), out_specs=(sem, sem, *[hbm] * (2 * n), pl.BlockSpec(memory_space=pltpu.VMEM)),
        input_output_aliases={i: 2 + i for i in range(2 * n)},
        compiler_params=pltpu.CompilerParams(has_side_effects=pltpu.SideEffectType.DATAFLOW_SIDE_EFFECTING),
    )(*[pltpu.with_memory_space_constraint(a, pltpu.HBM) for a in srcs + lands])
    return dict(sems=res[:2], srcs=list(res[2:2 + n]), lands=list(res[2 + n:2 + 2 * n]), per_peer=per_peer), res[-1]


def _remote_wait(handle, after, name):
    n = len(handle["srcs"])
    per_peer = handle["per_peer"]

    def body(*refs):
        src_refs, land_refs = refs[:n], refs[n:2 * n]
        send_sems, recv_sems = refs[2 * n], refs[2 * n + 1]
        sends, recvs = _split_copies(src_refs, land_refs, send_sems, recv_sems, per_peer)
        for cp in sends:
            cp.wait_send()
        for cp in recvs:
            cp.wait_recv()

    hbm = pl.BlockSpec(memory_space=pltpu.HBM)
    sem = pl.BlockSpec(memory_space=pltpu.SEMAPHORE)
    arrays = handle["srcs"] + handle["lands"]
    res = pl.pallas_call(
        body, name=name, out_shape=tuple(pltpu.HBM(a.shape, a.dtype) for a in arrays),
        in_specs=[hbm] * (2 * n) + [sem, sem, pl.BlockSpec(memory_space=pl.ANY)], out_specs=tuple([hbm] * (2 * n)),
        input_output_aliases={i: i for i in range(2 * n)},
        compiler_params=pltpu.CompilerParams(has_side_effects=pltpu.SideEffectType.DATAFLOW_SIDE_EFFECTING),
    )(*arrays, *handle["sems"], after)
    return list(res[n:])


def _with_own(lands, own, me):
    return [lax.dynamic_update_index_in_dim(land, o.astype(land.dtype), me, 0) for land, o in zip(lands, own)]


def _adamw(w, m, v, slabs, name, tr):
    r, cols = w.shape
    c1 = 1.0 - ADAM_B1 ** ADAM_STEP
    c2 = 1.0 - ADAM_B2 ** ADAM_STEP

    def body(w_ref, m_ref, v_ref, s_ref, g_ref, d_ref, nm_ref, nv_ref):
        g = s_ref[0].astype(F32)
        for d in range(1, N_DEV):
            g = g + s_ref[d].astype(F32)
        nm = ADAM_B1 * m_ref[...] + (1.0 - ADAM_B1) * g
        nv = ADAM_B2 * v_ref[...] + (1.0 - ADAM_B2) * (g * g)
        g_ref[...] = g
        nm_ref[...] = nm
        nv_ref[...] = nv
        d_ref[...] = -ADAM_LR * ((nm / c1) / (jnp.sqrt(nv / c2) + ADAM_EPS) + ADAM_WD * w_ref[...])

    assert r % tr == 0, name
    blk = pl.BlockSpec((tr, cols), lambda i: (i, 0))
    return pl.pallas_call(
        body, grid=(r // tr,), in_specs=[blk, blk, blk, pl.BlockSpec((N_DEV, tr, cols), lambda i: (0, i, 0))],
        out_specs=[blk] * 4, out_shape=[S((r, cols), F32)] * 4, name=name, compiler_params=_params(("parallel",)),
    )(w, m, v, slabs)


BIG = ("w_in", "w_att_proj", "w_up", "w_ssm_proj", "w_out", "w_down", "conv_w")
ADAMW_ROWS = dict(w_in=256, w_att_proj=768, w_up=512, w_ssm_proj=256, w_out=128, w_down=256, conv_w=4)
SMALL = ("norm_mix_pre_w", "b_gate", "conv_b", "dt_bias", "a_log", "d_skip", "ssm_norm_w", "norm_mix_post_w",
         "norm_ffn_pre_w", "norm_ffn_post_w")
ORDER = ("norm_mix_pre_w", "w_in", "b_gate", "conv_w", "conv_b", "dt_bias", "a_log", "d_skip", "ssm_norm_w", "w_att_proj",
         "w_ssm_proj", "w_out", "norm_mix_post_w", "norm_ffn_pre_w", "w_up", "w_down", "norm_ffn_post_w")
ROW_SHARDED = ("w_ssm_proj", "w_out", "w_down")
LATE = ("w_att_proj", "w_ssm_proj", "w_out", "w_up", "w_down")
IN_PROJ_W = 10528
IN_SHARD_W = IN_PROJ_W // N_DEV
IN_SEGMENTS = ((2304, 4352), (8480, 10528), (4352, 8448), (0, 2304), (8448, 8480))


def _pack(parts, rows_multiple):
    flat = jnp.concatenate([p.reshape(-1) for p in parts])
    pad = (-flat.shape[0]) % (rows_multiple * LANE)
    return jnp.pad(flat, (0, pad)).reshape(-1, LANE)


def _unpack(flat2d, shapes):
    flat, out, off = flat2d.reshape(-1), [], 0
    for sh in shapes:
        n = int(np.prod(sh))
        out.append(flat[off:off + n].reshape(sh))
        off += n
    return out


def _reorder_in_proj(w):
    qkv, z, xbc = w[:, :2304], w[:, 2304:4352], w[:, 4352:8448]
    dt, gate = w[:, 8448:8480], w[:, 8480:10528]
    return jnp.concatenate([z, gate, xbc, qkv, dt, jnp.zeros((w.shape[0], PROJ_W - 10528), w.dtype)], axis=1)


def _restore_in_proj(wr):
    return jnp.concatenate([wr[:, OFF_QKV:OFF_QKV + 2304], wr[:, OFF_Z:OFF_Z + 2048], wr[:, OFF_XBC:OFF_XBC + 4096],
                            wr[:, OFF_DT:OFF_DT + 32], wr[:, OFF_GL:OFF_GL + 2048]], axis=1)


def _assemble_in_proj(g):
    pieces = []
    for lo, hi in IN_SEGMENTS:
        while lo < hi:
            d = lo // IN_SHARD_W
            end = min(hi, (d + 1) * IN_SHARD_W)
            pieces.append(g[d][:, lo - d * IN_SHARD_W:end - d * IN_SHARD_W])
            lo = end
    pieces.append(jnp.zeros((g.shape[1], PROJ_W - IN_PROJ_W), g.dtype))
    return jnp.concatenate(pieces, axis=1)


def _in_proj_slabs(wr):
    orig = _restore_in_proj(wr)
    return jnp.stack([orig[:, d * IN_SHARD_W:(d + 1) * IN_SHARD_W] for d in range(N_DEV)])


def kernel(x, norm_mix_pre_w, w_in, b_gate, conv_w, conv_b, dt_bias, a_log, d_skip, ssm_norm_w, w_att_proj, w_ssm_proj, w_out, norm_mix_post_w, norm_ffn_pre_w, w_up, w_down, norm_ffn_post_w, loss_target, m_norm_mix_pre_w, m_w_in, m_b_gate, m_conv_w, m_conv_b, m_dt_bias, m_a_log, m_d_skip, m_ssm_norm_w, m_w_att_proj, m_w_ssm_proj, m_w_out, m_norm_mix_post_w, m_norm_ffn_pre_w, m_w_up, m_w_down, m_norm_ffn_post_w, v_norm_mix_pre_w, v_w_in, v_b_gate, v_conv_w, v_conv_b, v_dt_bias, v_a_log, v_d_skip, v_ssm_norm_w, v_w_att_proj, v_w_ssm_proj, v_w_out, v_norm_mix_post_w, v_norm_ffn_pre_w, v_w_up, v_w_down, v_norm_ffn_post_w):
    w = dict(norm_mix_pre_w=norm_mix_pre_w, w_in=w_in, b_gate=b_gate, conv_w=conv_w, conv_b=conv_b, dt_bias=dt_bias, a_log=a_log,
             d_skip=d_skip, ssm_norm_w=ssm_norm_w, w_att_proj=w_att_proj, w_ssm_proj=w_ssm_proj, w_out=w_out,
             norm_mix_post_w=norm_mix_post_w, norm_ffn_pre_w=norm_ffn_pre_w, w_up=w_up, w_down=w_down, norm_ffn_post_w=norm_ffn_post_w)
    m = dict(norm_mix_pre_w=m_norm_mix_pre_w, w_in=m_w_in, b_gate=m_b_gate, conv_w=m_conv_w, conv_b=m_conv_b, dt_bias=m_dt_bias,
             a_log=m_a_log, d_skip=m_d_skip, ssm_norm_w=m_ssm_norm_w, w_att_proj=m_w_att_proj, w_ssm_proj=m_w_ssm_proj, w_out=m_w_out,
             norm_mix_post_w=m_norm_mix_post_w, norm_ffn_pre_w=m_norm_ffn_pre_w, w_up=m_w_up, w_down=m_w_down, norm_ffn_post_w=m_norm_ffn_post_w)
    v = dict(norm_mix_pre_w=v_norm_mix_pre_w, w_in=v_w_in, b_gate=v_b_gate, conv_w=v_conv_w, conv_b=v_conv_b, dt_bias=v_dt_bias,
             a_log=v_a_log, d_skip=v_d_skip, ssm_norm_w=v_ssm_norm_w, w_att_proj=v_w_att_proj, w_ssm_proj=v_w_ssm_proj, w_out=v_w_out,
             norm_mix_post_w=v_norm_mix_post_w, norm_ffn_pre_w=v_norm_ffn_pre_w, w_up=v_w_up, w_down=v_w_down, norm_ffn_post_w=v_norm_ffn_post_w)
    shard_shapes = {n: w[n].shape[1:] for n in ORDER}

    mx, my, mc = _mesh_pos()
    me = 4 * mx + 2 * my + mc

    g_in, g_conv = _all_gather([w["w_in"][0].astype(BF16), w["conv_w"][0]])
    conv_full = jnp.moveaxis(g_conv, 0, 1).reshape(SSM_CONV, CONV_DIM)
    late_shards = [w[n][0].astype(BF16) for n in LATE]
    late_handle, token = _remote_start(late_shards, False, "late_weights_start")
    w_pre = w["norm_mix_pre_w"] + jnp.tile(token[0:1, :], (1, D_MODEL // LANE))

    def late_weights(after):
        full = dict(zip(LATE, _with_own(_remote_wait(late_handle, after, "late_weights_wait"), late_shards, me)))
        for n in ROW_SHARDED:
            full[n] = full[n].reshape(-1, full[n].shape[2])
        w_att = jnp.moveaxis(full["w_att_proj"], 0, 1).reshape(ATT_WIDTH, D_MODEL)
        return w_att, full["w_ssm_proj"], full["w_out"], full["w_up"], full["w_down"]

    started = {}

    def start_exchange(tag, slabs):
        own = [lax.dynamic_index_in_dim(s, me, 0, keepdims=False) for s in slabs]
        handle, tok = _remote_start(slabs, True, tag + "_grads_start")
        started[tag] = (handle, own)
        return tok

    def on_mid_grads(g):
        slabs = dict(w_up=g["w_up"], w_att_proj=jnp.moveaxis(g["w_att_proj"].reshape(ATT_WIDTH, N_DEV, -1), 1, 0))
        for n in ROW_SHARDED:
            slabs[n] = g[n].reshape(N_DEV, -1, g[n].shape[1])
        return start_exchange("mid", [slabs[n] for n in LATE])

    def on_in_proj_grads(gw_in_r, g_conv_w):
        return start_exchange("in_proj", [_in_proj_slabs(gw_in_r), jnp.moveaxis(g_conv_w.reshape(SSM_CONV, N_DEV, -1), 1, 0)])

    loss, g_x, grads = _local_step(
        x[0], loss_target[0], w_pre, _assemble_in_proj(g_in), w["b_gate"], conv_full, w["conv_b"], w["dt_bias"], w["a_log"],
        w["d_skip"], w["ssm_norm_w"], late_weights, w["norm_mix_post_w"], w["norm_ffn_pre_w"], w["norm_ffn_post_w"],
        on_mid_grads, on_in_proj_grads)

    recv = {}
    for tag, names in (("mid", LATE), ("in_proj", ("w_in", "conv_w"))):
        handle, own = started[tag]
        recv.update(zip(names, _with_own(_remote_wait(handle, g_x, tag + "_grads_wait"), own, me)))
    small = _pack([grads[n].astype(F32) for n in SMALL], 8)
    _, small_all = _exchange_grads([], small)

    small_shapes = [shard_shapes[n] for n in SMALL]
    small_out = _adamw(*[_pack([d_[n][0] for n in SMALL], 8) for d_ in (w, m, v)], small_all, "adamw_replicated", small_all.shape[1])
    big_out = {n: _adamw(w[n][0], m[n][0], v[n][0], recv[n], "adamw_" + n, ADAMW_ROWS[n]) for n in BIG}
    res = []
    for which, small_flat in enumerate(small_out):
        vals = {n: big_out[n][which] for n in BIG}
        vals.update(zip(SMALL, _unpack(small_flat, small_shapes)))
        res.append([vals[n][None] for n in ORDER])
    g_out, d_out, m_out, v_out = res
    total = lax.psum(loss[0, 0], ("x", "y", "c"))
    return (total, g_x[None], *g_out, *d_out, *m_out, *v_out)
```

```python
import functools
import math

import jax
import jax.numpy as jnp
import numpy as np
from jax import lax
from jax.experimental import pallas as pl
from jax.experimental.pallas import tpu as pltpu

F32 = jnp.float32
BF16 = jnp.bfloat16

D_MODEL = 1024
HEAD_DIM = 64
N_ATT_HEADS = 12
ATT_WIDTH = N_ATT_HEADS * HEAD_DIM
DILATED_PATTERNS = ((128, 1), (512, 4), (2048, 16))
ATT_BLOCK = 128
SSM_INNER = 2048
SSM_HEAD_DIM = 64
SSM_HEADS = 32
SSM_GROUPS = 8
SSM_STATE = 128
SSM_CHUNK = 128
CONV_DIM = 4096
SSM_CONV = 4
FFN_HIDDEN = 4096
RMS_EPS = 1e-6
N_DEV = 8

ADAM_LR = 0.001
ADAM_B1 = 0.9
ADAM_B2 = 0.999
ADAM_EPS = 1e-08
ADAM_WD = 0.01
ADAM_STEP = 10

LANE = 128
OFF_Z, OFF_GL, OFF_XBC, OFF_QKV, OFF_DT = 0, 2048, 4096, 8192, 10496
PROJ_W = 10752
PROJ_BLOCKS = PROJ_W // LANE
PA_W = OFF_QKV
PB_W = PROJ_W - OFF_QKV
PB_DT = OFF_DT - OFF_QKV
VMEM_LIMIT = 52 * 1024 * 1024
NEG = -1e30

HI = lax.Precision.HIGHEST
NT_DIMS = (((1,), (1,)), ((), ()))
TN_DIMS = (((0,), (0,)), ((), ()))
S = jax.ShapeDtypeStruct


def _params(sem):
    return pltpu.CompilerParams(dimension_semantics=sem, vmem_limit_bytes=VMEM_LIMIT)


def _matmul(a, b, *, mode, out_dtype, name, tm, tn, tk, epilogue=None, extra=None, stacked=False, after=None, b_cols=None):
    if mode == "nn":
        m, k = a.shape
        n = b.shape[0] * b.shape[2] if stacked else b.shape[1]
        col0 = 0
        if b_cols is not None:
            assert b_cols[0] % tn == 0, name
            col0, n = b_cols[0] // tn, b_cols[1]
        a_spec = pl.BlockSpec((tm, tk), lambda i, j, kk: (i, kk))
        b_spec = pl.BlockSpec((None, tk, tn), lambda i, j, kk: (j, kk, 0)) if stacked else pl.BlockSpec((tk, tn), lambda i, j, kk: (kk, col0 + j))
        dims = (((1,), (0,)), ((), ()))
    elif mode == "nt":
        m, k = a.shape
        n = b.shape[1] if stacked else b.shape[0]
        a_spec = pl.BlockSpec((tm, tk), lambda i, j, kk: (i, kk))
        b_spec = pl.BlockSpec((None, tn, tk), lambda i, j, kk: (kk, j, 0)) if stacked else pl.BlockSpec((tn, tk), lambda i, j, kk: (j, kk))
        dims = NT_DIMS
    else:
        (k, m), n = a.shape, b.shape[1]
        a_spec = pl.BlockSpec((tk, tm), lambda i, j, kk: (kk, i))
        b_spec = pl.BlockSpec((tk, tn), lambda i, j, kk: (kk, j))
        dims = TN_DIMS
    assert m % tm == 0 and n % tn == 0 and k % tk == 0, (name, m, n, k)
    if stacked:
        assert (tk if mode == "nt" else tn) * N_DEV == (k if mode == "nt" else n), name
    nk = k // tk
    o_spec = pl.BlockSpec((tm, tn), lambda i, j, kk: (i, j))
    in_specs, args = [a_spec, b_spec], [a, b]
    if epilogue == "relu2":
        out_shape = (S((m, n), BF16), S((m, n), BF16))
        out_specs = (o_spec, o_spec)
    elif epilogue == "also_bf16":
        out_shape = (S((m, n), out_dtype), S((m, n), BF16))
        out_specs = (o_spec, o_spec)
    elif stacked and mode == "tn":
        out_shape, out_specs = S((N_DEV, m, tn), out_dtype), pl.BlockSpec((None, tm, tn), lambda i, j, kk: (j, i, 0))
    else:
        out_shape, out_specs = S((m, n), out_dtype), o_spec
    if epilogue == "relu2_bwd":
        in_specs.append(o_spec)
        args.append(extra)
    n_in = len(args)
    if after is not None:
        in_specs.append(pl.BlockSpec(after.shape, lambda i, j, kk: (0,) * after.ndim))
        args.append(after)

    def finish(acc, refs):
        if epilogue == "relu2":
            r = jnp.maximum(acc, 0.0)
            refs[0][...] = (r * r).astype(BF16)
            refs[1][...] = acc.astype(BF16)
        elif epilogue == "also_bf16":
            refs[0][...] = acc.astype(out_dtype)
            refs[1][...] = acc.astype(BF16)
        elif epilogue == "relu2_bwd":
            up = refs[0][...].astype(F32)
            refs[1][...] = (acc * (2.0 * jnp.maximum(up, 0.0))).astype(out_dtype)
        else:
            refs[0][...] = acc.astype(out_dtype)

    def body(a_ref, b_ref, *rest):
        rest = rest[:n_in - 2] + rest[len(args) - 2:]
        part = lax.dot_general(a_ref[...].astype(BF16), b_ref[...].astype(BF16), dims, preferred_element_type=F32)
        if nk == 1:
            finish(part, rest)
            return
        acc_ref = rest[-1]
        kk = pl.program_id(2)

        @pl.when(kk == 0)
        def _():
            acc_ref[...] = part

        @pl.when(kk > 0)
        def _():
            acc_ref[...] += part

        @pl.when(kk == nk - 1)
        def _():
            finish(acc_ref[...], rest[:-1])

    scratch = [] if nk == 1 else [pltpu.VMEM((tm, tn), F32)]
    return pl.pallas_call(
        body, grid=(m // tm, n // tn, nk), in_specs=in_specs, out_specs=out_specs, out_shape=out_shape,
        scratch_shapes=scratch, name=name, compiler_params=_params(("parallel", "parallel", "arbitrary")),
    )(*args)


def _rowcall(body, name, n_rows, tr, ins, outs, scratch=(), into=None):
    in_specs = [pl.BlockSpec(bs, im) for _, bs, im in ins]
    out_specs = [pl.BlockSpec(bs, im) for _, _, bs, im in outs]
    out_shape = [S(sh, dt) for sh, dt, _, _ in outs]
    args = [a for a, _, _ in ins]
    aliases = {}
    kernel = body
    if into is not None:
        buf, bs, im = into
        n_in = len(args)
        in_specs.append(pl.BlockSpec(memory_space=pl.ANY))
        args.append(buf)
        out_specs.append(pl.BlockSpec(bs, im))
        out_shape.append(S(buf.shape, buf.dtype))
        aliases = {n_in: len(out_shape) - 1}

        def kernel(*refs):
            body(*refs[:n_in], *refs[n_in + 1:])

    return pl.pallas_call(
        kernel, grid=(n_rows // tr,), in_specs=in_specs, out_specs=out_specs, out_shape=out_shape,
        input_output_aliases=aliases, scratch_shapes=list(scratch), name=name, compiler_params=_params(("arbitrary",)),
    )(*args)


def _rows(arr, tr, width=None, cb=0):
    width = arr.shape[1] if width is None else width
    return (arr, (tr, width), lambda i, cb=cb: (i, cb))


def _whole(arr):
    nd = arr.ndim
    return (arr, arr.shape, lambda i, nd=nd: (0,) * nd)


def _orow(n_rows, width, dtype, tr):
    return ((n_rows, width), dtype, (tr, width), lambda i: (i, 0))


def _oacc(width):
    return ((1, width), F32, (1, width), lambda i: (0, 0))


def _accumulate(ref, value):
    first = pl.program_id(0) == 0

    @pl.when(first)
    def _():
        ref[...] = value

    @pl.when(jnp.logical_not(first))
    def _():
        ref[...] += value


def _colsum(v):
    return jnp.sum(v, axis=0, keepdims=True)


def _rms_fwd(x, w):
    r = lax.rsqrt(jnp.mean(x * x, axis=-1, keepdims=True) + RMS_EPS)
    return x * r * w


def _rms_bwd(gy, x, w):
    r = lax.rsqrt(jnp.mean(x * x, axis=-1, keepdims=True) + RMS_EPS)
    xn = x * r
    gxn = gy * w
    gx = r * (gxn - xn * jnp.mean(gxn * xn, axis=-1, keepdims=True))
    return gx, _colsum(gy * xn)


def _sigmoid(x):
    return 1.0 / (1.0 + jnp.exp(-x))


def _head_expand(n_heads_pad, n_heads, width):
    h = lax.broadcasted_iota(jnp.int32, (n_heads_pad, n_heads * width), 0)
    c = lax.broadcasted_iota(jnp.int32, (n_heads_pad, n_heads * width), 1)
    return (c // width == h).astype(F32)


def _head_reduce(n_heads, width, n_heads_pad):
    c = lax.broadcasted_iota(jnp.int32, (n_heads * width, n_heads_pad), 0)
    h = lax.broadcasted_iota(jnp.int32, (n_heads * width, n_heads_pad), 1)
    return (c // width == h).astype(F32)


def _block_ones(n, width):
    r = lax.broadcasted_iota(jnp.int32, (n, n), 0)
    c = lax.broadcasted_iota(jnp.int32, (n, n), 1)
    return (r // width == c // width).astype(F32)


def _pre_norm(x, w_pre, tr=512):
    t = x.shape[0]

    def body(x_ref, w_ref, u_ref):
        u_ref[...] = _rms_fwd(x_ref[...], w_ref[...]).astype(BF16)

    return _rowcall(body, "pre_norm", t, tr, [_rows(x, tr), _whole(w_pre)], [_orow(t, D_MODEL, BF16, tr)])[0]


CONV_HALO = 16


def _row_shift(cur, halo, j):
    tr = cur.shape[0]
    r = lax.broadcasted_iota(jnp.int32, (tr, tr), 0)
    c = lax.broadcasted_iota(jnp.int32, (tr, tr), 1)
    main = jnp.dot((c == r + j).astype(BF16), cur, preferred_element_type=F32)
    er = lax.broadcasted_iota(jnp.int32, (CONV_HALO, CONV_HALO), 0)
    ec = lax.broadcasted_iota(jnp.int32, (CONV_HALO, CONV_HALO), 1)
    if j < 0:
        edge = jnp.dot((ec == CONV_HALO + er + j).astype(BF16), halo, preferred_element_type=F32)
        return jnp.concatenate([main[:CONV_HALO] + edge, main[CONV_HALO:]], axis=0)
    edge = jnp.dot((ec == er + j - CONV_HALO).astype(BF16), halo, preferred_element_type=F32)
    return jnp.concatenate([main[:tr - CONV_HALO], main[tr - CONV_HALO:] + edge], axis=0)


def _conv_fwd(proj, conv_w, conv_b, tr=256):
    t = proj.shape[0]
    cb = OFF_XBC // CONV_DIM
    halo = (proj, (CONV_HALO, CONV_DIM), lambda i: (jnp.maximum(i * (tr // CONV_HALO) - 1, 0), cb))

    def body(cur_ref, prev_ref, w_ref, b_ref, o_ref, xc_ref):
        cur = cur_ref[...]
        prev = jnp.where(pl.program_id(0) > 0, prev_ref[...], jnp.zeros_like(prev_ref[...]))
        acc = b_ref[...] + w_ref[3:4, :] * cur.astype(F32)
        for k in range(SSM_CONV - 1):
            acc = acc + w_ref[k:k + 1, :] * _row_shift(cur, prev, -(SSM_CONV - 1 - k))
        o_ref[...] = acc * _sigmoid(acc)
        xc_ref[...] = acc.astype(BF16)

    return _rowcall(body, "conv_fwd", t, tr, [_rows(proj, tr, CONV_DIM, cb), halo, _whole(conv_w), _whole(conv_b)],
                    [_orow(t, CONV_DIM, F32, tr), _orow(t, CONV_DIM, BF16, tr)])


def _dt_fwd(proj, dt_bias_pad, alog_pad, tr=512):
    t = proj.shape[0]

    def body(raw_ref, b_ref, al_ref, dtx_ref, csx_ref, cst_ref):
        v = raw_ref[...] + b_ref[...]
        dt = jnp.maximum(v, 0.0) + jnp.log1p(jnp.exp(-jnp.abs(v)))
        expand = _head_expand(LANE, SSM_HEADS, SSM_HEAD_DIM)
        dtx_ref[...] = _dot_split(dt, expand, 0, 3)
        la = dt * (-jnp.exp(al_ref[...]))
        row = lax.broadcasted_iota(jnp.int32, (SSM_CHUNK, SSM_CHUNK), 0)
        col = lax.broadcasted_iota(jnp.int32, (SSM_CHUNK, SSM_CHUNK), 1)
        tril = (col <= row).astype(F32)
        cs = jnp.concatenate([_dot_split(tril, la[k * SSM_CHUNK:(k + 1) * SSM_CHUNK, :], 1, 3) for k in range(tr // SSM_CHUNK)], axis=0)
        csx_ref[...] = _dot_split(cs, expand, 0, 3)
        cst_ref[...] = cs.T

    return _rowcall(body, "dt_fwd", t, tr, [_rows(proj, tr, LANE, PB_DT // LANE), _whole(dt_bias_pad), _whole(alog_pad)],
                    [_orow(t, SSM_INNER, F32, tr), _orow(t, SSM_INNER, F32, tr), ((LANE, t), F32, (LANE, tr), lambda i: (0, i))])


def _gate_norm_fwd(y_ssd, xa, proj, dskip_x, norm_w, tr=256):
    t = y_ssd.shape[0]
    gw = SSM_INNER // SSM_GROUPS

    def body(y_ref, xs_ref, z_ref, d_ref, w_ref, o_ref):
        z = z_ref[...].astype(F32)
        y3 = (y_ref[...].astype(F32) + d_ref[...] * xs_ref[...]) * (z * _sigmoid(z))
        for g in range(SSM_GROUPS):
            sl = slice(g * gw, (g + 1) * gw)
            o_ref[:, sl] = _rms_fwd(y3[:, sl], w_ref[:, sl]).astype(BF16)

    return _rowcall(body, "gate_norm_fwd", t, tr,
                    [_rows(y_ssd, tr), _rows(xa, tr, SSM_INNER, 0), _rows(proj, tr, SSM_INNER, OFF_Z // SSM_INNER), _whole(dskip_x), _whole(norm_w)],
                    [_orow(t, SSM_INNER, BF16, tr)])[0]


def _gating_fwd(proj, b_gate, att_p, ssm_p, tr=512):
    t = proj.shape[0]

    def body(gl_ref, b_ref, a_ref, s_ref, o_ref):
        gates = _sigmoid(gl_ref[...].astype(F32) + b_ref[...])
        o_ref[...] = (gates[:, :D_MODEL] * a_ref[...].astype(F32) + gates[:, D_MODEL:] * s_ref[...].astype(F32)).astype(BF16)

    return _rowcall(body, "gating_fwd", t, tr, [_rows(proj, tr, 2 * D_MODEL, OFF_GL // (2 * D_MODEL)), _whole(b_gate), _rows(att_p, tr), _rows(ssm_p, tr)],
                    [_orow(t, D_MODEL, BF16, tr)])[0]


def _mix_post_ffn_pre(x, mixed, w_post, w_fpre, tr=512):
    t = x.shape[0]

    def body(x_ref, m_ref, wp_ref, wf_ref, h1_ref, f_ref):
        h1 = x_ref[...] + _rms_fwd(m_ref[...], wp_ref[...])
        h1_ref[...] = h1
        f_ref[...] = _rms_fwd(h1, wf_ref[...]).astype(BF16)

    return _rowcall(body, "mix_post_ffn_pre", t, tr, [_rows(x, tr), _rows(mixed, tr), _whole(w_post), _whole(w_fpre)],
                    [_orow(t, D_MODEL, F32, tr), _orow(t, D_MODEL, BF16, tr)])


def _loss_and_ffn_post_bwd(h1, dn, w_fpost, target, tr=512):
    t = h1.shape[0]

    def body(h1_ref, dn_ref, w_ref, tg_ref, loss_ref, gh2_ref, gdn_ref, gw_ref):
        dn = dn_ref[...]
        w = w_ref[...]
        err = h1_ref[...] + _rms_fwd(dn, w) - tg_ref[...]
        _accumulate(loss_ref, jnp.zeros((1, LANE), F32) + 0.5 * jnp.sum(jnp.mean(err * err, axis=-1, keepdims=True)))
        gh2 = err * (1.0 / D_MODEL)
        gh2_ref[...] = gh2
        gdn, gw = _rms_bwd(gh2, dn, w)
        gdn_ref[...] = gdn.astype(BF16)
        _accumulate(gw_ref, gw)

    return _rowcall(body, "loss_ffn_post_bwd", t, tr, [_rows(h1, tr), _rows(dn, tr), _whole(w_fpost), _rows(target, tr)],
                    [_oacc(LANE), _orow(t, D_MODEL, F32, tr), _orow(t, D_MODEL, BF16, tr), _oacc(D_MODEL)])


def _ffn_pre_mix_post_bwd(g_h2, g_f, h1, w_fpre, mixed, w_post, tr=512):
    t = h1.shape[0]

    def body(gh2_ref, gf_ref, h1_ref, wf_ref, m_ref, wp_ref, gh1_ref, gm_ref, gwf_ref, gwp_ref):
        gx, gwf = _rms_bwd(gf_ref[...], h1_ref[...], wf_ref[...])
        gh1 = gh2_ref[...] + gx
        gh1_ref[...] = gh1
        gm, gwp = _rms_bwd(gh1, m_ref[...], wp_ref[...])
        gm_ref[...] = gm.astype(BF16)
        _accumulate(gwf_ref, gwf)
        _accumulate(gwp_ref, gwp)

    return _rowcall(body, "ffn_pre_mix_post_bwd", t, tr,
                    [_rows(g_h2, tr), _rows(g_f, tr), _rows(h1, tr), _whole(w_fpre), _rows(mixed, tr), _whole(w_post)],
                    [_orow(t, D_MODEL, F32, tr), _orow(t, D_MODEL, BF16, tr), _oacc(D_MODEL), _oacc(D_MODEL)])


def _gating_bwd(g_mixin, proj, b_gate, att_p, ssm_p, g_proj, tr=512):
    t = proj.shape[0]

    def body(gm_ref, gl_ref, b_ref, a_ref, s_ref, ga_ref, gs_ref, gb_ref, ggl_ref):
        gates = _sigmoid(gl_ref[...].astype(F32) + b_ref[...])
        gm = gm_ref[...].astype(F32)
        g_att, g_ssm = gates[:, :D_MODEL], gates[:, D_MODEL:]
        ga_ref[...] = (gm * g_att).astype(BF16)
        gs_ref[...] = (gm * g_ssm).astype(BF16)
        ggl_a = gm * a_ref[...].astype(F32) * g_att * (1.0 - g_att)
        ggl_s = gm * s_ref[...].astype(F32) * g_ssm * (1.0 - g_ssm)
        ggl_ref[:, :D_MODEL] = ggl_a.astype(BF16)
        ggl_ref[:, D_MODEL:] = ggl_s.astype(BF16)
        _accumulate(gb_ref.at[:, :D_MODEL], _colsum(ggl_a))
        _accumulate(gb_ref.at[:, D_MODEL:], _colsum(ggl_s))

    return _rowcall(body, "gating_bwd", t, tr,
                    [_rows(g_mixin, tr), _rows(proj, tr, 2 * D_MODEL, OFF_GL // (2 * D_MODEL)), _whole(b_gate), _rows(att_p, tr), _rows(ssm_p, tr)],
                    [_orow(t, D_MODEL, BF16, tr), _orow(t, D_MODEL, BF16, tr), _oacc(2 * D_MODEL)],
                    into=(g_proj, (tr, 2 * D_MODEL), lambda i: (i, OFF_GL // (2 * D_MODEL))))


def _gate_norm_bwd(g_y4, y_ssd, xa, proj, dskip_x, norm_w, g_proj, tr=256):
    t = y_ssd.shape[0]
    gw = SSM_INNER // SSM_GROUPS

    def body(g_ref, y_ref, xs_ref, z_ref, d_ref, w_ref, gy2_ref, gnw_ref, gdx_ref, gd_ref, gz_ref):
        z = z_ref[...].astype(F32)
        xs = xs_ref[...]
        sg = _sigmoid(z)
        sz = z * sg
        y2 = y_ref[...].astype(F32) + d_ref[...] * xs
        y3 = y2 * sz
        g4 = g_ref[...].astype(F32)
        for g in range(SSM_GROUPS):
            sl = slice(g * gw, (g + 1) * gw)
            gy3, gnw = _rms_bwd(g4[:, sl], y3[:, sl], w_ref[:, sl])
            _accumulate(gnw_ref.at[:, sl], gnw)
            gy2 = gy3 * sz[:, sl]
            gy2_ref[:, sl] = gy2
            gz_ref[:, sl] = (gy3 * y2[:, sl] * (sg[:, sl] * (1.0 + z[:, sl] * (1.0 - sg[:, sl])))).astype(BF16)
            _accumulate(gdx_ref.at[:, sl], _colsum(gy2 * xs[:, sl]))
        tot = jnp.broadcast_to(gdx_ref[...], (8, SSM_INNER))
        gd_ref[...] = jnp.dot(tot, _head_reduce(SSM_HEADS, SSM_HEAD_DIM, LANE), precision=HI, preferred_element_type=F32)[0:1, :]

    return _rowcall(body, "gate_norm_bwd", t, tr,
                    [_rows(g_y4, tr), _rows(y_ssd, tr), _rows(xa, tr, SSM_INNER, 0), _rows(proj, tr, SSM_INNER, OFF_Z // SSM_INNER), _whole(dskip_x), _whole(norm_w)],
                    [_orow(t, SSM_INNER, F32, tr), _oacc(SSM_INNER), _oacc(SSM_INNER), _oacc(LANE)],
                    into=(g_proj, (tr, SSM_INNER), lambda i: (i, OFF_Z // SSM_INNER)))


def _dt_bwd(g_dtx, ga_rows, proj, dt_bias_pad, g_proj, tr=512):
    t = proj.shape[0]
    tail = PROJ_W - OFF_DT

    def body(g_ref, ga_ref, raw_ref, b_ref, gb_ref, gal_ref, o_ref):
        red = _head_reduce(SSM_HEADS, SSM_HEAD_DIM, LANE)
        gdt = _dot_split(g_ref[...], red, 0, 3)
        graw = gdt * _sigmoid(raw_ref[...] + b_ref[...])
        o_ref[...] = jnp.concatenate([graw.astype(BF16), jnp.zeros((tr, tail - LANE), BF16)], axis=1)
        _accumulate(gb_ref, _colsum(graw))
        tot = jnp.broadcast_to(_colsum(ga_ref[...]), (8, SSM_INNER))
        gal_ref[...] = jnp.dot(tot, red, precision=HI, preferred_element_type=F32)[0:1, :]

    return _rowcall(body, "dt_bwd", t, tr, [_rows(g_dtx, tr), _whole(ga_rows), _rows(proj, tr, LANE, PB_DT // LANE), _whole(dt_bias_pad)],
                    [_oacc(LANE), _oacc(LANE)], into=(g_proj, (tr, tail), lambda i: (i, OFF_DT // tail)))


def _conv_bwd(g_xs, g_b, g_c, xc, proj, conv_w, g_proj, tr=256):
    t = proj.shape[0]
    n_blk = t // tr
    cb = OFF_XBC // CONV_DIM
    nb, nc = SSM_INNER, SSM_INNER + SSM_GROUPS * SSM_STATE
    def nxt(arr, width):
        return (arr, (CONV_HALO, width), lambda i: (jnp.minimum((i + 1) * (tr // CONV_HALO), t // CONV_HALO - 1), 0))

    def body(gxs_ref, gxs_n, gb_ref, gb_n, gc_ref, gc_n, xc_ref, xc_n, x_ref, w_ref, gcb_ref, gw0, gw1, gw2, gw3, o_ref):
        def gxc_of(gxs, gb, gc, xc, keep):
            xcf = xc[...].astype(F32)
            sg = _sigmoid(xcf)
            dsilu = jnp.where(keep, sg * (1.0 + xcf * (1.0 - sg)), 0.0)
            return jnp.concatenate([gxs[...] * dsilu[:, :nb], gb[...] * dsilu[:, nb:nc], gc[...] * dsilu[:, nc:]], axis=1)

        gxc = gxc_of(gxs_ref, gb_ref, gc_ref, xc_ref, True)
        gxc16 = gxc.astype(BF16)
        nxt16 = gxc_of(gxs_n, gb_n, gc_n, xc_n, pl.program_id(0) < n_blk - 1).astype(BF16)
        x = x_ref[...].astype(F32)
        acc = w_ref[3:4, :] * gxc
        _accumulate(gw3, _colsum(gxc * x))
        _accumulate(gcb_ref, _colsum(gxc))
        for k, gw in enumerate((gw0, gw1, gw2)):
            shifted = _row_shift(gxc16, nxt16, SSM_CONV - 1 - k)
            acc = acc + w_ref[k:k + 1, :] * shifted
            _accumulate(gw, _colsum(shifted * x))
        o_ref[...] = acc.astype(BF16)

    ins = []
    for arr, width in ((g_xs, SSM_INNER), (g_b, nc - nb), (g_c, nc - nb), (xc, CONV_DIM)):
        ins += [_rows(arr, tr), nxt(arr, width)]
    ins += [_rows(proj, tr, CONV_DIM, cb), _whole(conv_w)]
    return _rowcall(body, "conv_bwd", t, tr, ins, [_oacc(CONV_DIM)] * 5, into=(g_proj, (tr, CONV_DIM), lambda i: (i, cb)))


def _pre_norm_bwd(g_h1, g_u, x, w_pre, tr=512):
    t = x.shape[0]

    def body(gh_ref, gu_ref, x_ref, w_ref, gx_ref, gw_ref):
        gx, gw = _rms_bwd(gu_ref[...], x_ref[...], w_ref[...])
        gx_ref[...] = gh_ref[...] + gx
        _accumulate(gw_ref, gw)

    return _rowcall(body, "pre_norm_bwd", t, tr, [_rows(g_h1, tr), _rows(g_u, tr), _rows(x, tr), _whole(w_pre)],
                    [_orow(t, D_MODEL, F32, tr), _oacc(D_MODEL)])


def _alibi_slopes(n):
    def pow2(m):
        start = 2.0 ** (-8.0 / m)
        return [start ** (i + 1) for i in range(m)]
    if (n & (n - 1)) == 0:
        s = pow2(n)
    else:
        c = 2 ** int(math.floor(math.log2(n)))
        s = pow2(c) + pow2(2 * c)[0::2][: n - c]
    return np.array(s, dtype=np.float32)


def _slope_rows():
    s = _alibi_slopes(N_ATT_HEADS).reshape(N_ATT_HEADS // 2, 2)
    return jnp.asarray(np.broadcast_to(np.repeat(s, HEAD_DIM, axis=1)[:, None, :], (N_ATT_HEADS // 2, 8, LANE)).copy())


ATT_MAX_BLOCK_ROWS = 2048


RESIDUE_MAJOR_FROM = 16


class _AttLayout:
    def __init__(self, t, dil):
        self.t, self.dil = t, dil
        self.rows = t // dil
        self.residue_major = dil >= RESIDUE_MAJOR_FROM
        if self.residue_major:
            bq, self.stride = min(512, self.rows), 1
        else:
            bq, self.stride = min(512, self.rows, ATT_MAX_BLOCK_ROWS // dil), dil
        self.nsub = bq // ATT_BLOCK
        self.nblk = self.rows // bq
        self.rb = bq * self.stride
        self.pb = ATT_BLOCK * self.stride
        self.n_pb = self.rows * self.stride // self.pb
        self.out_dtype = F32 if self.stride > 1 else BF16

    def qkv(self, proj):
        pb, pb16 = proj
        if self.residue_major:
            return pb16.reshape(self.rows, self.dil * PB_W), PB_W // LANE, 0
        return (pb if self.stride > 1 else pb16), 0, 0

    def act(self, a):
        return a.reshape(self.rows, self.dil * ATT_WIDTH) if self.residue_major else a

    def act_shape(self):
        return (self.rows, self.dil * ATT_WIDTH) if self.residue_major else (self.t, ATT_WIDTH)

    def col(self, r, band, c):
        return r * band + c if self.residue_major else c


def _residue_rows(r, stride, first_block, n_blocks=1):
    if stride == 1:
        return pl.ds(first_block * ATT_BLOCK, n_blocks * ATT_BLOCK)
    return pl.ds(r + first_block * ATT_BLOCK * stride, n_blocks * ATT_BLOCK, stride=stride)


def _lane_half():
    return lax.broadcasted_iota(jnp.int32, (ATT_BLOCK, LANE), 1) // HEAD_DIM


def _att_scores_mask(dil, first):
    iq = lax.broadcasted_iota(jnp.int32, (ATT_BLOCK, 2 * ATT_BLOCK), 0)
    jk = lax.broadcasted_iota(jnp.int32, (ATT_BLOCK, 2 * ATT_BLOCK), 1)
    dist = ATT_BLOCK + iq - jk
    valid = (dist >= 0) & (dist <= ATT_BLOCK) & (jnp.logical_not(first) | (jk >= ATT_BLOCK))
    return (dist * dil).astype(F32), valid


def _stack_heads(x):
    half = _lane_half()
    return jnp.concatenate([jnp.where(half == 0, x, jnp.zeros_like(x)), jnp.where(half == 1, x, jnp.zeros_like(x))], axis=0)


def _unstack_heads(x):
    return jnp.where(_lane_half() == 0, x[:ATT_BLOCK], x[ATT_BLOCK:])


def _head_columns(x):
    return jnp.concatenate([x[:, 0:1], x[:, HEAD_DIM:HEAD_DIM + 1]], axis=0)


def _stacked_bias(sl_ref, dist, valid):
    d2 = jnp.concatenate([dist, dist], axis=0)
    v2 = jnp.concatenate([valid, valid], axis=0)
    top = lax.broadcasted_iota(jnp.int32, d2.shape, 0) < ATT_BLOCK
    slope = jnp.where(top, sl_ref[0:1, 0:1], sl_ref[0:1, HEAD_DIM:HEAD_DIM + 1])
    return jnp.where(v2, -slope * d2, NEG)


def _att_fwd(proj, dil, slopes):
    t = proj[0].shape[0]
    lay = _AttLayout(t, dil)
    nsub, nblk, rb, pb = lay.nsub, lay.nblk, lay.rb, lay.pb
    src, band, qb = lay.qkv(proj)
    aw = ATT_WIDTH // LANE

    def spec(off, prev=False):
        if prev:
            return pl.BlockSpec((pb, LANE), lambda hp, i, r: (jnp.maximum(i * nsub - 1, 0), lay.col(r, band, qb + off + hp)))
        return pl.BlockSpec((rb, LANE), lambda hp, i, r: (i, lay.col(r, band, qb + off + hp)))

    o_spec = pl.BlockSpec((rb, LANE), lambda hp, i, r: (i, lay.col(r, aw, hp)))

    def body(q_ref, kc_ref, kp_ref, vc_ref, vp_ref, sl_ref, o_ref, lse_ref):
        i, r = pl.program_id(1), pl.program_id(2)
        half = _lane_half()
        for sub in range(nsub):
            rs = _residue_rows(r, lay.stride, sub)
            q = (q_ref[rs, :] * (HEAD_DIM ** -0.5)).astype(BF16)
            if sub == 0:
                r0 = _residue_rows(r, lay.stride, 0)
                kk = jnp.concatenate([kp_ref[r0, :], kc_ref[rs, :]], axis=0).astype(BF16)
                vv = jnp.concatenate([vp_ref[r0, :], vc_ref[rs, :]], axis=0).astype(BF16)
                first = i == 0
            else:
                ks = _residue_rows(r, lay.stride, sub - 1, 2)
                kk, vv = kc_ref[ks, :].astype(BF16), vc_ref[ks, :].astype(BF16)
                first = jnp.bool_(False)
            dist, valid = _att_scores_mask(dil, first)
            s = lax.dot_general(_stack_heads(q), kk, NT_DIMS, preferred_element_type=F32) + _stacked_bias(sl_ref, dist, valid)
            m = jnp.max(s, axis=-1, keepdims=True)
            p = jnp.exp(s - m)
            l = jnp.sum(p, axis=-1, keepdims=True)
            o_ref[rs, :] = _unstack_heads(jnp.dot(p.astype(BF16), vv, preferred_element_type=F32) / l).astype(lay.out_dtype)
            lse_ref[rs, :] = _unstack_heads(jnp.broadcast_to(m + jnp.log(l), (2 * ATT_BLOCK, LANE)))

    o, lse = pl.pallas_call(
        body, grid=(N_ATT_HEADS // 2, nblk, dil),
        in_specs=[spec(0), spec(6), spec(6, True), spec(12), spec(12, True), pl.BlockSpec((None, 8, LANE), lambda hp, i, r: (hp, 0, 0))],
        out_specs=[o_spec, o_spec], out_shape=[S(lay.act_shape(), lay.out_dtype), S(lay.act_shape(), F32)],
        name=f"att_fwd_d{dil}", compiler_params=_params(("parallel", "parallel", "arbitrary")),
    )(src, src, src, src, src, slopes)
    return o.reshape(t, ATT_WIDTH), lse.reshape(t, ATT_WIDTH)


def _att_combine(outs, lses, tr=512):
    t = outs[0].shape[0]

    def body(o0, o1, o2, l0, l1, l2, att_ref, lse_ref):
        ls = [l0[...], l1[...], l2[...]]
        m = jnp.maximum(jnp.maximum(ls[0], ls[1]), ls[2])
        ws = [jnp.exp(l - m) for l in ls]
        tot = ws[0] + ws[1] + ws[2]
        num = ws[0] * o0[...].astype(F32) + ws[1] * o1[...].astype(F32) + ws[2] * o2[...].astype(F32)
        att_ref[...] = (num / tot).astype(BF16)
        lse_ref[...] = m + jnp.log(tot)

    return _rowcall(body, "att_combine", t, tr, [_rows(a, tr) for a in list(outs) + list(lses)],
                    [_orow(t, ATT_WIDTH, BF16, tr), _orow(t, ATT_WIDTH, F32, tr)])


def _att_delta(g_att, att, tr=512):
    t = att.shape[0]

    def body(g_ref, a_ref, o_ref):
        prod = g_ref[...] * a_ref[...].astype(F32)
        o_ref[...] = _dot_split(prod, _block_ones(ATT_WIDTH, HEAD_DIM), 0, 3)

    return _rowcall(body, "att_delta", t, tr, [_rows(g_att, tr), _rows(att, tr)], [_orow(t, ATT_WIDTH, F32, tr)])[0]


def _att_bwd(proj, g_att, lse, delta, dil, slopes):
    t = proj[0].shape[0]
    lay = _AttLayout(t, dil)
    nsub, nblk, rb, pb, n_pb = lay.nsub, lay.nblk, lay.rb, lay.pb, lay.n_pb
    src, band, qb = lay.qkv(proj)
    aw = ATT_WIDTH // LANE

    def near(i, which):
        return jnp.maximum(i * nsub - 1, 0) if which == "prev" else jnp.minimum((i + 1) * nsub, n_pb - 1)

    def pspec(off, which=None):
        if which:
            return pl.BlockSpec((pb, LANE), lambda hp, i, r: (near(i, which), lay.col(r, band, qb + off + hp)))
        return pl.BlockSpec((rb, LANE), lambda hp, i, r: (i, lay.col(r, band, qb + off + hp)))

    def aspec(which=None):
        if which:
            return pl.BlockSpec((pb, LANE), lambda hp, i, r: (near(i, which), lay.col(r, aw, hp)))
        return pl.BlockSpec((rb, LANE), lambda hp, i, r: (i, lay.col(r, aw, hp)))

    scale = HEAD_DIM ** -0.5

    def body(q_ref, qn_ref, kc_ref, kp_ref, vc_ref, vp_ref, do_ref, don_ref, lse_ref, lsen_ref, dl_ref, dln_ref, sl_ref,
             dq_ref, dk_ref, dv_ref):
        i, r = pl.program_id(1), pl.program_id(2)
        half = _lane_half()

        def tile_grads(q, do, lse_q, dl_q, kk, vv, dist, valid):
            q2, do2 = _stack_heads(q), _stack_heads(do)
            s = lax.dot_general(q2, kk, NT_DIMS, preferred_element_type=F32) + _stacked_bias(sl_ref, dist, valid)
            p = jnp.exp(s - _head_columns(lse_q))
            dp = lax.dot_general(do2, vv, NT_DIMS, preferred_element_type=F32)
            ds16 = (p * (dp - _head_columns(dl_q))).astype(BF16)
            dq = _unstack_heads(jnp.dot(ds16, kk, preferred_element_type=F32)) * scale
            dk = lax.dot_general(ds16, q2, TN_DIMS, preferred_element_type=F32)
            dv = lax.dot_general(p.astype(BF16), do2, TN_DIMS, preferred_element_type=F32)
            return dq, dk, dv

        carry_k = carry_v = None
        for sub in range(nsub):
            rs = _residue_rows(r, lay.stride, sub)
            q = (q_ref[rs, :] * scale).astype(BF16)
            do = do_ref[rs, :].astype(BF16)
            if sub == 0:
                r0 = _residue_rows(r, lay.stride, 0)
                kk = jnp.concatenate([kp_ref[r0, :], kc_ref[rs, :]], axis=0).astype(BF16)
                vv = jnp.concatenate([vp_ref[r0, :], vc_ref[rs, :]], axis=0).astype(BF16)
                first = i == 0
            else:
                ks = _residue_rows(r, lay.stride, sub - 1, 2)
                kk, vv = kc_ref[ks, :].astype(BF16), vc_ref[ks, :].astype(BF16)
                first = jnp.bool_(False)
            dist, valid = _att_scores_mask(dil, first)
            dq, dk2, dv2 = tile_grads(q, do, lse_ref[rs, :], dl_ref[rs, :], kk, vv, dist, valid)
            dq_ref[rs, :] = dq.astype(lay.out_dtype)
            if sub > 0:
                rp = _residue_rows(r, lay.stride, sub - 1)
                dk_ref[rp, :] = (carry_k + dk2[:ATT_BLOCK, :]).astype(lay.out_dtype)
                dv_ref[rp, :] = (carry_v + dv2[:ATT_BLOCK, :]).astype(lay.out_dtype)
            carry_k, carry_v = dk2[ATT_BLOCK:, :], dv2[ATT_BLOCK:, :]
        rl = _residue_rows(r, lay.stride, nsub - 1)
        rn = _residue_rows(r, lay.stride, 0)
        iq = lax.broadcasted_iota(jnp.int32, (ATT_BLOCK, ATT_BLOCK), 0)
        jk = lax.broadcasted_iota(jnp.int32, (ATT_BLOCK, ATT_BLOCK), 1)
        dist_i = ATT_BLOCK + iq - jk
        valid = (dist_i >= 0) & (dist_i <= ATT_BLOCK) & (i < nblk - 1)
        qn = (qn_ref[rn, :] * scale).astype(BF16)
        _, dk1, dv1 = tile_grads(qn, don_ref[rn, :].astype(BF16), lsen_ref[rn, :], dln_ref[rn, :],
                                 kc_ref[rl, :].astype(BF16), vc_ref[rl, :].astype(BF16), (dist_i * dil).astype(F32), valid)
        dk_ref[rl, :] = (carry_k + dk1).astype(lay.out_dtype)
        dv_ref[rl, :] = (carry_v + dv1).astype(lay.out_dtype)

    gv, lv, dlv = lay.act(g_att), lay.act(lse), lay.act(delta)
    dq, dk, dv = pl.pallas_call(
        body, grid=(N_ATT_HEADS // 2, nblk, dil),
        in_specs=[pspec(0), pspec(0, "next"), pspec(6), pspec(6, "prev"), pspec(12), pspec(12, "prev"),
                  aspec(), aspec("next"), aspec(), aspec("next"), aspec(), aspec("next"),
                  pl.BlockSpec((None, 8, LANE), lambda hp, i, r: (hp, 0, 0))],
        out_specs=[aspec(), aspec(), aspec()], out_shape=[S(lay.act_shape(), lay.out_dtype)] * 3,
        name=f"att_bwd_d{dil}", compiler_params=_params(("parallel", "parallel", "arbitrary")),
    )(src, src, src, src, src, src, gv, gv, lv, lv, dlv, dlv, slopes)
    return dq.reshape(t, ATT_WIDTH), dk.reshape(t, ATT_WIDTH), dv.reshape(t, ATT_WIDTH)


def _att_grad_sum(dqs, dks, dvs, g_proj, tr=2048):
    t = dqs[0].shape[0]
    cw = 2 * LANE
    per = ATT_WIDTH // cw
    arrays = list(dqs) + list(dks) + list(dvs)
    n_pat = len(dqs)

    def body(*refs):
        o_ref = refs[-1]
        which = pl.program_id(1) // per
        tot = jnp.zeros((tr, cw), F32)
        for s in range(3):
            part = refs[s * n_pat][...].astype(F32)
            for g in range(1, n_pat):
                part = part + refs[s * n_pat + g][...].astype(F32)
            tot = jnp.where(which == s, part, tot)
        o_ref[...] = tot.astype(BF16)

    in_specs = [pl.BlockSpec((tr, cw), lambda i, c, s=s: (i, jnp.clip(c - per * s, 0, per - 1))) for s in range(3) for _ in range(n_pat)]
    in_specs.append(pl.BlockSpec(memory_space=pl.ANY))
    return pl.pallas_call(
        lambda *refs: body(*refs[:len(arrays)], refs[-1]), grid=(t // tr, 3 * per), in_specs=in_specs,
        out_specs=pl.BlockSpec((tr, cw), lambda i, c: (i, OFF_QKV // cw + c)), out_shape=S(g_proj.shape, g_proj.dtype),
        input_output_aliases={len(arrays): 0}, name="att_grad_sum", compiler_params=_params(("arbitrary", "arbitrary")),
    )(*arrays, g_proj)


def _ssd_common(xs, dtx, cs, cs_t):
    ch = SSM_CHUNK
    row = lax.broadcasted_iota(jnp.int32, (ch, ch), 0)
    col = lax.broadcasted_iota(jnp.int32, (ch, ch), 1)
    cs_last = cs[ch - 1:ch, :]
    return dict(tril=col <= row, row=row, col=col, cs=cs, cs_t=cs_t, cs_last=cs_last,
                e=jnp.exp(cs), w=jnp.exp(cs_last - cs), xd=xs * dtx)


def _dot_split(a, b, split, terms=2):
    ops = [a, b]
    rest = ops[split]
    other = ops[1 - split].astype(BF16)
    out = None
    for _ in range(terms):
        piece = rest.astype(BF16)
        rest = rest - piece.astype(F32)
        part = jnp.dot(other, piece, preferred_element_type=F32) if split == 1 else jnp.dot(piece, other, preferred_element_type=F32)
        out = part if out is None else out + part
    return out


def _decay_col(cs_t, heads_per_group):
    r = lax.broadcasted_iota(jnp.int32, (heads_per_group * SSM_HEAD_DIM, SSM_STATE), 0) // SSM_HEAD_DIM
    out = jnp.zeros((heads_per_group * SSM_HEAD_DIM, SSM_STATE), F32)
    for j in range(heads_per_group):
        out = jnp.where(r == j, jnp.exp(cs_t[j:j + 1, SSM_CHUNK - 1:SSM_CHUNK]), out)
    return out


SSD_GROUPS_PER_STEP = 2


def _ssd_specs(t):
    hg = SSM_HEADS // SSM_GROUPS
    gw = hg * SSM_HEAD_DIM
    nb0 = SSM_INNER // SSM_STATE
    return hg, gw, nb0


def _ssd_group_views(gi, gw, wide, narrow, stacked):
    w = [r.at[:, pl.ds(gi * gw, gw)] for r in wide]
    n = [r.at[:, pl.ds(gi * SSM_STATE, SSM_STATE)] for r in narrow]
    return w, n, [r.at[gi] for r in stacked]


def _ssd_fwd(xa, dtx, csx, cst_g):
    t = xa.shape[0]
    nch = t // SSM_CHUNK
    hg, gw, nb0 = _ssd_specs(t)
    ch = SSM_CHUNK
    gp = SSD_GROUPS_PER_STEP

    def body(xs_ref, b_ref, c_ref, dtx_ref, cs_ref, cst_ref, y_ref, st_ref, h_scr):
        for gi in range(gp):
            (xs_g, dtx_g, cs_g, y_g), (b_g, c_g), (cst_gi, st_g) = _ssd_group_views(
                gi, gw, (xs_ref, dtx_ref, cs_ref, y_ref), (b_ref, c_ref), (cst_ref, st_ref))
            group_body(pl.program_id(0), pl.program_id(1) * gp + gi, xs_g, b_g, c_g, dtx_g, cs_g, cst_gi, y_g, st_g, h_scr)

    def group_body(cc, g, xs_ref, b_ref, c_ref, dtx_ref, cs_ref, cst_ref, y_ref, st_ref, h_scr):
        @pl.when(cc == 0)
        def _():
            h_scr[g] = jnp.zeros((gw, SSM_STATE), F32)

        q = _ssd_common(xs_ref[...], dtx_ref[...], cs_ref[...], cst_ref[...])
        bb, cb = b_ref[...].astype(BF16), c_ref[...].astype(BF16)
        cbm = lax.dot_general(cb, bb, NT_DIMS, preferred_element_type=F32)
        h = h_scr[g]
        st_ref[...] = h
        xd16 = q["xd"].astype(BF16)
        y = lax.dot_general(cb, h.astype(BF16), NT_DIMS, preferred_element_type=F32) * q["e"]
        lane_head = lax.broadcasted_iota(jnp.int32, (ch, gw), 1) // SSM_HEAD_DIM
        gmats, xds = [], []
        for j in range(hg):
            diff = q["cs"][:, j * SSM_HEAD_DIM:j * SSM_HEAD_DIM + 1] - q["cs_t"][j:j + 1, :]
            gmats.append((cbm * jnp.exp(jnp.where(q["tril"], diff, NEG))).astype(BF16))
            xds.append(jnp.where(lane_head == j, xd16, jnp.zeros_like(xd16)))
        y = y + jnp.dot(jnp.concatenate(gmats, axis=1), jnp.concatenate(xds, axis=0), preferred_element_type=F32)
        y_ref[...] = y.astype(BF16)
        s_new = lax.dot_general((q["xd"] * q["w"]).astype(BF16), bb, TN_DIMS, preferred_element_type=F32)
        h_scr[g] = _decay_col(q["cs_t"], hg) * h + s_new

    wide = pl.BlockSpec((ch, gp * gw), lambda cc, g: (cc, g))
    return pl.pallas_call(
        body, grid=(nch, SSM_GROUPS // gp),
        in_specs=[wide,
                  pl.BlockSpec((ch, gp * SSM_STATE), lambda cc, g: (cc, nb0 // gp + g)),
                  pl.BlockSpec((ch, gp * SSM_STATE), lambda cc, g: (cc, (nb0 + SSM_GROUPS) // gp + g)),
                  wide, wide,
                  pl.BlockSpec((gp, 8, ch), lambda cc, g: (g, 0, cc))],
        out_specs=[wide, pl.BlockSpec((None, gp, gw, SSM_STATE), lambda cc, g: (cc, g, 0, 0))],
        out_shape=[S((t, SSM_INNER), BF16), S((nch, SSM_GROUPS, gw, SSM_STATE), F32)],
        scratch_shapes=[pltpu.VMEM((SSM_GROUPS, gw, SSM_STATE), F32)],
        name="ssd_fwd", compiler_params=_params(("arbitrary", "arbitrary")),
    )(xa, xa, xa, dtx, csx, cst_g)


def _ssd_bwd(xa, dtx, csx, cst_g, alog_x, g_y, states, dskip_x):
    t = xa.shape[0]
    nch = t // SSM_CHUNK
    hg, gw, nb0 = _ssd_specs(t)
    ch = SSM_CHUNK
    gp = SSD_GROUPS_PER_STEP

    def rc(cc):
        return nch - 1 - cc

    def body(xs_ref, b_ref, c_ref, dtx_ref, cs_ref, cst_ref, alx_ref, gy_ref, st_ref, dsk_ref,
             gxs_ref, gb_ref, gc_ref, gdt_ref, ga_ref, gh_scr):
        for gi in range(gp):
            wide, narrow, stacked = _ssd_group_views(
                gi, gw, (xs_ref, dtx_ref, cs_ref, alx_ref, gy_ref, dsk_ref, gxs_ref, gdt_ref, ga_ref), (b_ref, c_ref, gb_ref, gc_ref),
                (cst_ref, st_ref))
            xs_g, dtx_g, cs_g, alx_g, gy_g, dsk_g, gxs_g, gdt_g, ga_g = wide
            b_g, c_g, gb_g, gc_g = narrow
            group_body(pl.program_id(0), pl.program_id(1) * gp + gi, xs_g, b_g, c_g, dtx_g, cs_g, stacked[0], alx_g, gy_g, stacked[1],
                       dsk_g, gxs_g, gb_g, gc_g, gdt_g, ga_g, gh_scr)

    def group_body(cc, g, xs_ref, b_ref, c_ref, dtx_ref, cs_ref, cst_ref, alx_ref, gy_ref, st_ref, dsk_ref,
                   gxs_ref, gb_ref, gc_ref, gdt_ref, ga_ref, gh_scr):
        @pl.when(cc == 0)
        def _():
            gh_scr[g] = jnp.zeros((gw, SSM_STATE), F32)

        xs, dtx = xs_ref[...], dtx_ref[...]
        q = _ssd_common(xs, dtx, cs_ref[...], cst_ref[...])
        cs, cs_t, e, w, xd = q["cs"], q["cs_t"], q["e"], q["w"], q["xd"]
        bb, cb = b_ref[...].astype(BF16), c_ref[...].astype(BF16)
        gy = gy_ref[...]
        gy16, xd16 = gy.astype(BF16), xd.astype(BF16)
        h = st_ref[...]
        h16 = h.astype(BF16)
        ghn = gh_scr[g]
        ghn16 = ghn.astype(BF16)
        seg = _block_ones(gw, SSM_HEAD_DIM)
        cbm = lax.dot_general(cb, bb, NT_DIMS, preferred_element_type=F32)

        gye16 = (gy * e).astype(BF16)
        chm = lax.dot_general(cb, h16, NT_DIMS, preferred_element_type=F32)
        g_c = jnp.dot(gye16, h16, preferred_element_type=F32)
        gh_off = lax.dot_general(gye16, cb, TN_DIMS, preferred_element_type=F32)
        bgs = lax.dot_general(bb, ghn16, NT_DIMS, preferred_element_type=F32)
        g_xd = w * bgs
        head_sums = _dot_split(jnp.concatenate([gy * chm, xd * bgs], axis=0), seg, 0)
        g_e, g_w = head_sums[:ch], head_sums[ch:]
        g_b = jnp.dot((xd * w).astype(BF16), ghn16, preferred_element_type=F32)
        decay = _decay_col(cs_t, hg)
        gh_scr[g] = decay * ghn + gh_off
        rsum = jnp.sum(ghn * h, axis=1, keepdims=True)
        lane_head = lax.broadcasted_iota(jnp.int32, (ch, gw), 1) // SSM_HEAD_DIM
        lane_head1 = lax.broadcasted_iota(jnp.int32, (1, gw), 1) // SSM_HEAD_DIM
        g_el = jnp.zeros((1, gw), F32)
        g_cs = g_e * e - g_w * w
        upper = q["row"] <= q["col"]
        lms, gys = [], []
        for j in range(hg):
            g_el = jnp.where(lane_head1 == j, jnp.sum(rsum[j * SSM_HEAD_DIM:(j + 1) * SSM_HEAD_DIM, :], axis=0, keepdims=True), g_el)
            csc = cs[:, j * SSM_HEAD_DIM:j * SSM_HEAD_DIM + 1]
            csr = cs_t[j:j + 1, :]
            lms.append(jnp.exp(jnp.where(q["tril"], csc - csr, NEG)))
            gys.append(jnp.where(lane_head == j, gy16, jnp.zeros_like(gy16)))
        lm_st, gy_st = jnp.concatenate(lms, axis=0), jnp.concatenate(gys, axis=0)
        cbm_st = jnp.concatenate([cbm] * hg, axis=0)
        gcb_st = lax.dot_general(gy_st, xd16, NT_DIMS, preferred_element_type=F32) * lm_st
        gcb_sum = gcb_st[0:ch]
        for j in range(1, hg):
            gcb_sum = gcb_sum + gcb_st[j * ch:(j + 1) * ch]
        gcb16 = gcb_sum.astype(BF16)
        g_c = g_c + jnp.dot(gcb16, bb, preferred_element_type=F32)
        g_b = g_b + lax.dot_general(gcb16, cb, TN_DIMS, preferred_element_type=F32)
        g_xd = g_xd + lax.dot_general((cbm_st * lm_st).astype(BF16), gy_st, TN_DIMS, preferred_element_type=F32)
        m_st = gcb_st * cbm_st
        for j in range(hg):
            m_ls = m_st[j * ch:(j + 1) * ch]
            d_cs = jnp.sum(m_ls, axis=1, keepdims=True) - jnp.sum(m_ls.T, axis=1, keepdims=True)
            g_cs = g_cs + jnp.where(lane_head == j, d_cs, 0.0)
        extra = _colsum(g_w * w) + g_el * jnp.exp(q["cs_last"])
        g_cs = g_cs + jnp.where(lax.broadcasted_iota(jnp.int32, (ch, gw), 0) == ch - 1, extra, 0.0)
        g_la = _dot_split(upper, g_cs, 1)
        a_x = -jnp.exp(alx_ref[...])
        gdt_ref[...] = g_xd * xs + g_la * a_x * (1.0 / SSM_HEAD_DIM)
        ga_row = _colsum(g_la * (dtx * a_x)) * (1.0 / SSM_HEAD_DIM)
        ga_ref[...] = jnp.where(lax.broadcasted_iota(jnp.int32, (8, gw), 0) == 0, ga_row, 0.0)
        gxs_ref[...] = g_xd * dtx + gy * dsk_ref[...]
        gb_ref[...] = g_b
        gc_ref[...] = g_c

    wide = pl.BlockSpec((ch, gp * gw), lambda cc, g: (rc(cc), g))
    narrow = pl.BlockSpec((ch, gp * SSM_STATE), lambda cc, g: (rc(cc), g))
    row = pl.BlockSpec((1, gp * gw), lambda cc, g: (0, g))
    return pl.pallas_call(
        body, grid=(nch, SSM_GROUPS // gp),
        in_specs=[wide,
                  pl.BlockSpec((ch, gp * SSM_STATE), lambda cc, g: (rc(cc), nb0 // gp + g)),
                  pl.BlockSpec((ch, gp * SSM_STATE), lambda cc, g: (rc(cc), (nb0 + SSM_GROUPS) // gp + g)),
                  wide, wide,
                  pl.BlockSpec((gp, 8, ch), lambda cc, g: (g, 0, rc(cc))),
                  row, wide,
                  pl.BlockSpec((None, gp, gw, SSM_STATE), lambda cc, g: (rc(cc), g, 0, 0)),
                  row],
        out_specs=[wide, narrow, narrow, wide, pl.BlockSpec((8, gp * gw), lambda cc, g: (rc(cc), g))],
        out_shape=[S((t, SSM_INNER), F32), S((t, SSM_GROUPS * SSM_STATE), F32), S((t, SSM_GROUPS * SSM_STATE), F32),
                   S((t, SSM_INNER), F32), S((nch * 8, SSM_INNER), F32)],
        scratch_shapes=[pltpu.VMEM((SSM_GROUPS, gw, SSM_STATE), F32)],
        name="ssd_bwd", compiler_params=_params(("arbitrary", "arbitrary")),
    )(xa, xa, xa, dtx, csx, cst_g, alog_x, g_y, states, dskip_x)


def _local_step(x, target, w_pre, w_in_r, b_gate, conv_w, conv_b, dt_bias, a_log, d_skip, ssm_norm_w,
                late_weights, w_post, w_fpre, w_fpost, on_mid_grads, on_in_proj_grads):
    t = x.shape[0]
    mm = functools.partial(_matmul, tm=512)
    slopes = _slope_rows()
    hg = SSM_HEADS // SSM_GROUPS
    dt_bias_pad = jnp.pad(dt_bias, ((0, 0), (0, LANE - SSM_HEADS)))
    alog_x = jnp.repeat(a_log, SSM_HEAD_DIM, axis=1)
    alog_pad = jnp.pad(a_log, ((0, 0), (0, LANE - SSM_HEADS)))
    dskip_x = jnp.repeat(d_skip, SSM_HEAD_DIM, axis=1)

    u = _pre_norm(x, w_pre)
    pa = _matmul(u, w_in_r, mode="nn", out_dtype=BF16, name="in_proj_zgx", tm=1024, tn=2048, tk=D_MODEL, b_cols=(0, PA_W))
    pb, pb16 = _matmul(u, w_in_r[:, PA_W:], mode="nn", out_dtype=F32, name="in_proj_qkvdt", tm=1024, tn=PB_W // 2, tk=D_MODEL,
                       epilogue="also_bf16")
    fwd = [_att_fwd((pb, pb16), dil, slopes) for _, dil in DILATED_PATTERNS]
    att, lse = _att_combine([o for o, _ in fwd], [l for _, l in fwd])
    xa, xc = _conv_fwd(pa, conv_w, conv_b)
    dtx, csx, cst = _dt_fwd(pb, dt_bias_pad, alog_pad)
    cst_g = jnp.pad(cst[:SSM_HEADS].reshape(SSM_GROUPS, hg, t), ((0, 0), (0, 8 - hg), (0, 0)))
    y_ssd, states = _ssd_fwd(xa, dtx, csx, cst_g)
    y4 = _gate_norm_fwd(y_ssd, xa, pa, dskip_x, ssm_norm_w)
    w_att, w_ssm, w_out, w_up, w_down = late_weights(y4)
    att_p = mm(att, w_att, mode="nn", out_dtype=BF16, name="att_proj", tn=D_MODEL, tk=ATT_WIDTH)
    ssm_p = mm(y4, w_ssm, mode="nn", out_dtype=BF16, name="ssm_proj", tn=D_MODEL, tk=SSM_INNER)
    mixin = _gating_fwd(pa, b_gate, att_p, ssm_p)
    mixed = mm(mixin, w_out, mode="nn", out_dtype=F32, name="out_proj", tn=D_MODEL, tk=D_MODEL)
    h1, f = _mix_post_ffn_pre(x, mixed, w_post, w_fpre)
    act, up = _matmul(f, w_up, mode="nn", out_dtype=BF16, name="ffn_up", tm=2048, tn=FFN_HIDDEN // N_DEV, tk=D_MODEL, epilogue="relu2", stacked=True)
    dn = mm(act, w_down, mode="nn", out_dtype=F32, name="ffn_down", tn=D_MODEL, tk=FFN_HIDDEN)
    loss, g_h2, g_dn, gw_fpost = _loss_and_ffn_post_bwd(h1, dn, w_fpost, target)

    g_up = _matmul(g_dn, w_down, mode="nt", out_dtype=BF16, name="ffn_down_bwd_x", tm=1024, tn=2048, tk=D_MODEL, epilogue="relu2_bwd",
                   extra=up)
    gw_down = _matmul(act, g_dn, mode="tn", out_dtype=BF16, name="ffn_down_bwd_w", tm=1024, tn=D_MODEL, tk=2048)
    w_up_rows = jnp.moveaxis(w_up, 0, 1).reshape(D_MODEL, FFN_HIDDEN)
    g_f = mm(g_up, w_up_rows, mode="nt", out_dtype=F32, name="ffn_up_bwd_x", tn=D_MODEL, tk=FFN_HIDDEN)
    gw_up = _matmul(f, g_up, mode="tn", out_dtype=BF16, name="ffn_up_bwd_w", tm=D_MODEL, tn=FFN_HIDDEN // N_DEV, tk=2048, stacked=True)
    g_h1, g_mixed, gw_fpre, gw_post = _ffn_pre_mix_post_bwd(g_h2, g_f, h1, w_fpre, mixed, w_post)
    g_mixin = mm(g_mixed, w_out, mode="nt", out_dtype=BF16, name="out_proj_bwd_x", tn=D_MODEL, tk=D_MODEL)
    gw_out = _matmul(mixin, g_mixed, mode="tn", out_dtype=BF16, name="out_proj_bwd_w", tm=D_MODEL, tn=D_MODEL, tk=2048)
    g_proj = lax.empty((t, PROJ_W), BF16)
    g_att_p, g_ssm_p, g_b_gate, g_proj = _gating_bwd(g_mixin, pa, b_gate, att_p, ssm_p, g_proj)
    g_att = mm(g_att_p, w_att, mode="nt", out_dtype=F32, name="att_proj_bwd_x", tn=ATT_WIDTH, tk=D_MODEL)
    gw_att = _matmul(att, g_att_p, mode="tn", out_dtype=BF16, name="att_proj_bwd_w", tm=ATT_WIDTH, tn=D_MODEL, tk=2048)
    g_y4 = mm(g_ssm_p, w_ssm, mode="nt", out_dtype=BF16, name="ssm_proj_bwd_x", tn=SSM_INNER, tk=D_MODEL)
    gw_ssm = _matmul(y4, g_ssm_p, mode="tn", out_dtype=BF16, name="ssm_proj_bwd_w", tm=1024, tn=D_MODEL, tk=2048)
    token = on_mid_grads(dict(w_att_proj=gw_att, w_ssm_proj=gw_ssm, w_out=gw_out, w_up=gw_up, w_down=gw_down))
    if token is not None:
        ssm_norm_w = ssm_norm_w + jnp.tile(token[0:1, :], (1, SSM_INNER // LANE))
    g_y2, g_norm_w, _, g_d_skip, g_proj = _gate_norm_bwd(g_y4, y_ssd, xa, pa, dskip_x, ssm_norm_w, g_proj)
    g_xs, g_bm, g_cm, g_dtx, ga_rows = _ssd_bwd(xa, dtx, csx, cst_g, alog_x, g_y2, states, dskip_x)
    g_dt_bias, g_a_log, g_proj = _dt_bwd(g_dtx, ga_rows, pb, dt_bias_pad, g_proj)
    g_conv_b, gcw0, gcw1, gcw2, gcw3, g_proj = _conv_bwd(g_xs, g_bm, g_cm, xc, pa, conv_w, g_proj)
    delta = _att_delta(g_att, att)
    dqs, dks, dvs = [], [], []
    for _, dil in DILATED_PATTERNS:
        dq, dk, dv = _att_bwd((pb, pb16), g_att, lse, delta, dil, slopes)
        dqs.append(dq)
        dks.append(dk)
        dvs.append(dv)
    g_proj = _att_grad_sum(dqs, dks, dvs, g_proj)
    gw_in_r = _matmul(u, g_proj, mode="tn", out_dtype=BF16, name="in_proj_bwd_w", tm=D_MODEL, tn=1792, tk=2048)
    token = on_in_proj_grads(gw_in_r, jnp.concatenate([gcw0, gcw1, gcw2, gcw3], axis=0))
    g_u = _matmul(g_proj, w_in_r, mode="nt", out_dtype=F32, name="in_proj_bwd_x", tm=1024, tn=D_MODEL, tk=3584, after=token)
    g_x, gw_pre = _pre_norm_bwd(g_h1, g_u, x, w_pre)

    grads = dict(
        norm_mix_pre_w=gw_pre, b_gate=g_b_gate, conv_b=g_conv_b, dt_bias=g_dt_bias[:, :SSM_HEADS], a_log=g_a_log[:, :SSM_HEADS],
        d_skip=g_d_skip[:, :SSM_HEADS], ssm_norm_w=g_norm_w, norm_mix_post_w=gw_post, norm_ffn_pre_w=gw_fpre, norm_ffn_post_w=gw_fpost)
    return loss, g_x, grads


def _mesh_pos():
    return lax.axis_index("x"), lax.axis_index("y"), lax.axis_index("c")


def _all_gather(shards):
    n = len(shards)

    def body(*refs):
        x_refs, o_refs = refs[:n], refs[n:2 * n]
        send_sems, recv_sems, local_sems = refs[2 * n:]
        x, y, c = _mesh_pos()
        me, sibling = (x, y, c), (x, y, 1 - c)
        chips = [(1 - x, y), (x, 1 - y), (1 - x, 1 - y)]

        def copy(a, k, block, to, src=None):
            dst = o_refs[a].at[4 * block[0] + 2 * block[1] + block[2]]
            return pltpu.make_async_remote_copy(
                src_ref=dst if src is None else src, dst_ref=dst, send_sem=send_sems.at[7 * a + k], recv_sem=recv_sems.at[7 * a + k],
                device_id=to, device_id_type=pl.DeviceIdType.MESH)

        mine = [pltpu.make_async_copy(x_refs[a], o_refs[a].at[4 * x + 2 * y + c], local_sems.at[a]) for a in range(n)]
        for cp in mine:
            cp.start()
        first = []
        for a in range(n):
            first.append(copy(a, 0, me, sibling, src=x_refs[a]))
            first += [copy(a, 1 + j, me, (*chip, c), src=x_refs[a]) for j, chip in enumerate(chips)]
        for cp in first:
            cp.start()
        passed = []
        for j, chip in enumerate(chips):
            for a in range(n):
                copy(a, 1 + j, (*chip, c), me).wait_recv()
                passed.append(copy(a, 4 + j, (*chip, c), sibling))
                passed[-1].start()
        for a in range(n):
            copy(a, 0, sibling, me).wait_recv()
            for j, chip in enumerate(chips):
                copy(a, 4 + j, (*chip, 1 - c), me).wait_recv()
        for cp in first + passed:
            cp.wait_send()
        for cp in mine:
            cp.wait()

    hbm = pl.BlockSpec(memory_space=pltpu.HBM)
    return pl.pallas_call(
        body, out_shape=[S((N_DEV,) + s.shape, s.dtype) for s in shards],
        in_specs=[hbm] * n, out_specs=[hbm] * n,
        scratch_shapes=[pltpu.SemaphoreType.DMA((7 * n,)), pltpu.SemaphoreType.DMA((7 * n,)), pltpu.SemaphoreType.DMA((n,))],
        name="weights_all_gather",
    )(*shards)


def _exchange_grads(slab_arrays, small):
    n = len(slab_arrays)
    r_small = small.shape[0]

    def body(*refs):
        slab_refs, small_ref = refs[:n], refs[n]
        recv_refs, gsm_ref = refs[n + 1:2 * n + 1], refs[2 * n + 1]
        send_sems, recv_sems, local_sems = refs[2 * n + 2:]
        x, y, c = _mesh_pos()
        me = 4 * x + 2 * y + c

        def peer(k):
            px = 1 - x if k & 4 else x
            py = 1 - y if k & 2 else y
            pc = 1 - c if k & 1 else c
            return (px, py, pc), 4 * px + 2 * py + pc

        def copy(a, k, sending):
            to, lin = peer(k)
            sem = 7 * a + k - 1
            if a == n:
                src, dst = small_ref, gsm_ref.at[me if sending else lin]
            else:
                src, dst = slab_refs[a].at[lin], recv_refs[a].at[me if sending else lin]
            return pltpu.make_async_remote_copy(src_ref=src, dst_ref=dst, send_sem=send_sems.at[sem], recv_sem=recv_sems.at[sem],
                                                device_id=to, device_id_type=pl.DeviceIdType.MESH)

        own = [pltpu.make_async_copy(slab_refs[a].at[me], recv_refs[a].at[me], local_sems.at[a]) for a in range(n)]
        own.append(pltpu.make_async_copy(small_ref, gsm_ref.at[me], local_sems.at[n]))
        for cp in own:
            cp.start()
        order = [n] + list(range(n))
        sends = [copy(a, k, True) for a in order for k in range(1, N_DEV)]
        for cp in sends:
            cp.start()
        for a in order:
            for k in range(1, N_DEV):
                copy(a, k, False).wait_recv()
        for cp in sends:
            cp.wait_send()
        for cp in own:
            cp.wait()

    hbm = pl.BlockSpec(memory_space=pltpu.HBM)
    n_sem = 7 * (n + 1)
    res = pl.pallas_call(
        body, out_shape=[S(a.shape, a.dtype) for a in slab_arrays] + [S((N_DEV, r_small, LANE), small.dtype)],
        in_specs=[hbm] * (n + 1), out_specs=[hbm] * (n + 1),
        scratch_shapes=[pltpu.SemaphoreType.DMA((n_sem,)), pltpu.SemaphoreType.DMA((n_sem,)), pltpu.SemaphoreType.DMA((n + 1,))],
        name="grad_exchange",
    )(*slab_arrays, small)
    return res[:n], res[n]


def _peer_of(k, x, y, c):
    px = 1 - x if k & 4 else x
    py = 1 - y if k & 2 else y
    pc = 1 - c if k & 1 else c
    return (px, py, pc), 4 * px + 2 * py + pc


def _split_copies(src_refs, land_refs, send_sems, recv_sems, per_peer):
    x, y, c = _mesh_pos()
    me = 4 * x + 2 * y + c
    sends, recvs = [], []
    for a, (src, land) in enumerate(zip(src_refs, land_refs)):
        for k in range(1, N_DEV):
            to, lin = _peer_of(k, x, y, c)
            sem = 7 * a + k - 1
            piece = src.at[lin] if per_peer else src
            for slot, out in ((me, sends), (lin, recvs)):
                out.append(pltpu.make_async_remote_copy(
                    src_ref=piece, dst_ref=land.at[slot], send_sem=send_sems.at[sem], recv_sem=recv_sems.at[sem],
                    device_id=to, device_id_type=pl.DeviceIdType.MESH))
    return sends, recvs


def _remote_start(srcs, per_peer, name):
    n = len(srcs)
    lands = [lax.empty((N_DEV,) + (s.shape[1:] if per_peer else s.shape), s.dtype) for s in srcs]

    def body(*refs):
        src_refs, land_refs = refs[:n], refs[n:2 * n]
        send_sems, recv_sems = refs[2 * n], refs[2 * n + 1]
        token = refs[-1]
        sends, _ = _split_copies(src_refs, land_refs, send_sems, recv_sems, per_peer)
        for cp in sends:
            cp.start()
        token[...] = jnp.zeros_like(token)

    hbm = pl.BlockSpec(memory_space=pltpu.HBM)
    sem = pl.BlockSpec(memory_space=pltpu.SEMAPHORE)
    res = pl.pallas_call(
        body, name=name,
        out_shape=(pltpu.SemaphoreType.DMA((7 * n,)), pltpu.SemaphoreType.DMA((7 * n,)),
                   *[pltpu.HBM(a.shape, a.dtype) for a in srcs + lands], S((8, LANE), F32)),
        in_specs=[hbm] * (2 * n), out_specs=(sem, sem, *[hbm] * (2 * n), pl.BlockSpec(memory_space=pltpu.VMEM)),
        input_output_aliases={i: 2 + i for i in range(2 * n)},
        compiler_params=pltpu.CompilerParams(has_side_effects=pltpu.SideEffectType.DATAFLOW_SIDE_EFFECTING),
    )(*[pltpu.with_memory_space_constraint(a, pltpu.HBM) for a in srcs + lands])
    return dict(sems=res[:2], srcs=list(res[2:2 + n]), lands=list(res[2 + n:2 + 2 * n]), per_peer=per_peer), res[-1]


def _remote_wait(handle, after, name):
    n = len(handle["srcs"])
    per_peer = handle["per_peer"]

    def body(*refs):
        src_refs, land_refs = refs[:n], refs[n:2 * n]
        send_sems, recv_sems = refs[2 * n], refs[2 * n + 1]
        sends, recvs = _split_copies(src_refs, land_refs, send_sems, recv_sems, per_peer)
        for cp in sends:
            cp.wait_send()
        for cp in recvs:
            cp.wait_recv()

    hbm = pl.BlockSpec(memory_space=pltpu.HBM)
    sem = pl.BlockSpec(memory_space=pltpu.SEMAPHORE)
    arrays = handle["srcs"] + handle["lands"]
    res = pl.pallas_call(
        body, name=name, out_shape=tuple(pltpu.HBM(a.shape, a.dtype) for a in arrays),
        in_specs=[hbm] * (2 * n) + [sem, sem, pl.BlockSpec(memory_space=pl.ANY)], out_specs=tuple([hbm] * (2 * n)),
        input_output_aliases={i: i for i in range(2 * n)},
        compiler_params=pltpu.CompilerParams(has_side_effects=pltpu.SideEffectType.DATAFLOW_SIDE_EFFECTING),
    )(*arrays, *handle["sems"], after)
    return list(res[n:])


def _with_own(lands, own, me):
    return [lax.dynamic_update_index_in_dim(land, o.astype(land.dtype), me, 0) for land, o in zip(lands, own)]


def _adamw(w, m, v, slabs, name, tr):
    r, cols = w.shape
    c1 = 1.0 - ADAM_B1 ** ADAM_STEP
    c2 = 1.0 - ADAM_B2 ** ADAM_STEP

    def body(w_ref, m_ref, v_ref, s_ref, g_ref, d_ref, nm_ref, nv_ref):
        g = s_ref[0].astype(F32)
        for d in range(1, N_DEV):
            g = g + s_ref[d].astype(F32)
        nm = ADAM_B1 * m_ref[...] + (1.0 - ADAM_B1) * g
        nv = ADAM_B2 * v_ref[...] + (1.0 - ADAM_B2) * (g * g)
        g_ref[...] = g
        nm_ref[...] = nm
        nv_ref[...] = nv
        d_ref[...] = -ADAM_LR * ((nm / c1) / (jnp.sqrt(nv / c2) + ADAM_EPS) + ADAM_WD * w_ref[...])

    assert r % tr == 0, name
    blk = pl.BlockSpec((tr, cols), lambda i: (i, 0))
    return pl.pallas_call(
        body, grid=(r // tr,), in_specs=[blk, blk, blk, pl.BlockSpec((N_DEV, tr, cols), lambda i: (0, i, 0))],
        out_specs=[blk] * 4, out_shape=[S((r, cols), F32)] * 4, name=name, compiler_params=_params(("parallel",)),
    )(w, m, v, slabs)


BIG = ("w_in", "w_att_proj", "w_up", "w_ssm_proj", "w_out", "w_down", "conv_w")
ADAMW_ROWS = dict(w_in=256, w_att_proj=768, w_up=512, w_ssm_proj=256, w_out=128, w_down=256, conv_w=4)
SMALL = ("norm_mix_pre_w", "b_gate", "conv_b", "dt_bias", "a_log", "d_skip", "ssm_norm_w", "norm_mix_post_w",
         "norm_ffn_pre_w", "norm_ffn_post_w")
ORDER = ("norm_mix_pre_w", "w_in", "b_gate", "conv_w", "conv_b", "dt_bias", "a_log", "d_skip", "ssm_norm_w", "w_att_proj",
         "w_ssm_proj", "w_out", "norm_mix_post_w", "norm_ffn_pre_w", "w_up", "w_down", "norm_ffn_post_w")
ROW_SHARDED = ("w_ssm_proj", "w_out", "w_down")
LATE = ("w_att_proj", "w_ssm_proj", "w_out", "w_up", "w_down")
IN_PROJ_W = 10528
IN_SHARD_W = IN_PROJ_W // N_DEV
IN_SEGMENTS = ((2304, 4352), (8480, 10528), (4352, 8448), (0, 2304), (8448, 8480))


def _pack(parts, rows_multiple):
    flat = jnp.concatenate([p.reshape(-1) for p in parts])
    pad = (-flat.shape[0]) % (rows_multiple * LANE)
    return jnp.pad(flat, (0, pad)).reshape(-1, LANE)


def _unpack(flat2d, shapes):
    flat, out, off = flat2d.reshape(-1), [], 0
    for sh in shapes:
        n = int(np.prod(sh))
        out.append(flat[off:off + n].reshape(sh))
        off += n
    return out


def _reorder_in_proj(w):
    qkv, z, xbc = w[:, :2304], w[:, 2304:4352], w[:, 4352:8448]
    dt, gate = w[:, 8448:8480], w[:, 8480:10528]
    return jnp.concatenate([z, gate, xbc, qkv, dt, jnp.zeros((w.shape[0], PROJ_W - 10528), w.dtype)], axis=1)


def _restore_in_proj(wr):
    return jnp.concatenate([wr[:, OFF_QKV:OFF_QKV + 2304], wr[:, OFF_Z:OFF_Z + 2048], wr[:, OFF_XBC:OFF_XBC + 4096],
                            wr[:, OFF_DT:OFF_DT + 32], wr[:, OFF_GL:OFF_GL + 2048]], axis=1)


def _assemble_in_proj(g):
    pieces = []
    for lo, hi in IN_SEGMENTS:
        while lo < hi:
            d = lo // IN_SHARD_W
            end = min(hi, (d + 1) * IN_SHARD_W)
            pieces.append(g[d][:, lo - d * IN_SHARD_W:end - d * IN_SHARD_W])
            lo = end
    pieces.append(jnp.zeros((g.shape[1], PROJ_W - IN_PROJ_W), g.dtype))
    return jnp.concatenate(pieces, axis=1)


def _in_proj_slabs(wr):
    orig = _restore_in_proj(wr)
    return jnp.stack([orig[:, d * IN_SHARD_W:(d + 1) * IN_SHARD_W] for d in range(N_DEV)])


def kernel(x, norm_mix_pre_w, w_in, b_gate, conv_w, conv_b, dt_bias, a_log, d_skip, ssm_norm_w, w_att_proj, w_ssm_proj, w_out, norm_mix_post_w, norm_ffn_pre_w, w_up, w_down, norm_ffn_post_w, loss_target, m_norm_mix_pre_w, m_w_in, m_b_gate, m_conv_w, m_conv_b, m_dt_bias, m_a_log, m_d_skip, m_ssm_norm_w, m_w_att_proj, m_w_ssm_proj, m_w_out, m_norm_mix_post_w, m_norm_ffn_pre_w, m_w_up, m_w_down, m_norm_ffn_post_w, v_norm_mix_pre_w, v_w_in, v_b_gate, v_conv_w, v_conv_b, v_dt_bias, v_a_log, v_d_skip, v_ssm_norm_w, v_w_att_proj, v_w_ssm_proj, v_w_out, v_norm_mix_post_w, v_norm_ffn_pre_w, v_w_up, v_w_down, v_norm_ffn_post_w):
    w = dict(norm_mix_pre_w=norm_mix_pre_w, w_in=w_in, b_gate=b_gate, conv_w=conv_w, conv_b=conv_b, dt_bias=dt_bias, a_log=a_log,
             d_skip=d_skip, ssm_norm_w=ssm_norm_w, w_att_proj=w_att_proj, w_ssm_proj=w_ssm_proj, w_out=w_out,
             norm_mix_post_w=norm_mix_post_w, norm_ffn_pre_w=norm_ffn_pre_w, w_up=w_up, w_down=w_down, norm_ffn_post_w=norm_ffn_post_w)
    m = dict(norm_mix_pre_w=m_norm_mix_pre_w, w_in=m_w_in, b_gate=m_b_gate, conv_w=m_conv_w, conv_b=m_conv_b, dt_bias=m_dt_bias,
             a_log=m_a_log, d_skip=m_d_skip, ssm_norm_w=m_ssm_norm_w, w_att_proj=m_w_att_proj, w_ssm_proj=m_w_ssm_proj, w_out=m_w_out,
             norm_mix_post_w=m_norm_mix_post_w, norm_ffn_pre_w=m_norm_ffn_pre_w, w_up=m_w_up, w_down=m_w_down, norm_ffn_post_w=m_norm_ffn_post_w)
    v = dict(norm_mix_pre_w=v_norm_mix_pre_w, w_in=v_w_in, b_gate=v_b_gate, conv_w=v_conv_w, conv_b=v_conv_b, dt_bias=v_dt_bias,
             a_log=v_a_log, d_skip=v_d_skip, ssm_norm_w=v_ssm_norm_w, w_att_proj=v_w_att_proj, w_ssm_proj=v_w_ssm_proj, w_out=v_w_out,
             norm_mix_post_w=v_norm_mix_post_w, norm_ffn_pre_w=v_norm_ffn_pre_w, w_up=v_w_up, w_down=v_w_down, norm_ffn_post_w=v_norm_ffn_post_w)
    shard_shapes = {n: w[n].shape[1:] for n in ORDER}

    mx, my, mc = _mesh_pos()
    me = 4 * mx + 2 * my + mc

    g_in, g_conv = _all_gather([w["w_in"][0].astype(BF16), w["conv_w"][0]])
    conv_full = jnp.moveaxis(g_conv, 0, 1).reshape(SSM_CONV, CONV_DIM)
    late_shards = [w[n][0].astype(BF16) for n in LATE]
    late_handle, token = _remote_start(late_shards, False, "late_weights_start")
    w_pre = w["norm_mix_pre_w"] + jnp.tile(token[0:1, :], (1, D_MODEL // LANE))

    def late_weights(after):
        full = dict(zip(LATE, _with_own(_remote_wait(late_handle, after, "late_weights_wait"), late_shards, me)))
        for n in ROW_SHARDED:
            full[n] = full[n].reshape(-1, full[n].shape[2])
        w_att = jnp.moveaxis(full["w_att_proj"], 0, 1).reshape(ATT_WIDTH, D_MODEL)
        return w_att, full["w_ssm_proj"], full["w_out"], full["w_up"], full["w_down"]

    started = {}

    def start_exchange(tag, slabs):
        own = [lax.dynamic_index_in_dim(s, me, 0, keepdims=False) for s in slabs]
        handle, tok = _remote_start(slabs, True, tag + "_grads_start")
        started[tag] = (handle, own)
        return tok

    def on_mid_grads(g):
        slabs = dict(w_up=g["w_up"], w_att_proj=jnp.moveaxis(g["w_att_proj"].reshape(ATT_WIDTH, N_DEV, -1), 1, 0))
        for n in ROW_SHARDED:
            slabs[n] = g[n].reshape(N_DEV, -1, g[n].shape[1])
        return start_exchange("mid", [slabs[n] for n in LATE])

    def on_in_proj_grads(gw_in_r, g_conv_w):
        return start_exchange("in_proj", [_in_proj_slabs(gw_in_r), jnp.moveaxis(g_conv_w.reshape(SSM_CONV, N_DEV, -1), 1, 0)])

    loss, g_x, grads = _local_step(
        x[0], loss_target[0], w_pre, _assemble_in_proj(g_in), w["b_gate"], conv_full, w["conv_b"], w["dt_bias"], w["a_log"],
        w["d_skip"], w["ssm_norm_w"], late_weights, w["norm_mix_post_w"], w["norm_ffn_pre_w"], w["norm_ffn_post_w"],
        on_mid_grads, on_in_proj_grads)

    recv = {}
    for tag, names in (("mid", LATE), ("in_proj", ("w_in", "conv_w"))):
        handle, own = started[tag]
        recv.update(zip(names, _with_own(_remote_wait(handle, g_x, tag + "_grads_wait"), own, me)))
    small = _pack([grads[n].astype(F32) for n in SMALL], 8)
    _, small_all = _exchange_grads([], small)

    small_shapes = [shard_shapes[n] for n in SMALL]
    small_out = _adamw(*[_pack([d_[n][0] for n in SMALL], 8) for d_ in (w, m, v)], small_all, "adamw_replicated", small_all.shape[1])
    big_out = {n: _adamw(w[n][0], m[n][0], v[n][0], recv[n], "adamw_" + n, ADAMW_ROWS[n]) for n in BIG}
    res = []
    for which, small_flat in enumerate(small_out):
        vals = {n: big_out[n][which] for n in BIG}
        vals.update(zip(SMALL, _unpack(small_flat, small_shapes)))
        res.append([vals[n][None] for n in ORDER])
    g_out, d_out, m_out, v_out = res
    total = lax.psum(loss[0, 0], ("x", "y", "c"))
    return (total, g_x[None], *g_out, *d_out, *m_out, *v_out)
```

```python
import functools
import math

import jax
import jax.numpy as jnp
import numpy as np
from jax import lax
from jax.experimental import pallas as pl
from jax.experimental.pallas import tpu as pltpu

F32 = jnp.float32
BF16 = jnp.bfloat16

D_MODEL = 1024
HEAD_DIM = 64
N_ATT_HEADS = 12
ATT_WIDTH = N_ATT_HEADS * HEAD_DIM
DILATED_PATTERNS = ((128, 1), (512, 4), (2048, 16))
ATT_BLOCK = 128
SSM_INNER = 2048
SSM_HEAD_DIM = 64
SSM_HEADS = 32
SSM_GROUPS = 8
SSM_STATE = 128
SSM_CHUNK = 128
CONV_DIM = 4096
SSM_CONV = 4
FFN_HIDDEN = 4096
RMS_EPS = 1e-6
N_DEV = 8

ADAM_LR = 0.001
ADAM_B1 = 0.9
ADAM_B2 = 0.999
ADAM_EPS = 1e-08
ADAM_WD = 0.01
ADAM_STEP = 10

LANE = 128
OFF_Z, OFF_GL, OFF_XBC, OFF_QKV, OFF_DT = 0, 2048, 4096, 8192, 10496
PROJ_W = 10752
PROJ_BLOCKS = PROJ_W // LANE
PA_W = OFF_QKV
PB_W = PROJ_W - OFF_QKV
PB_DT = OFF_DT - OFF_QKV
VMEM_LIMIT = 52 * 1024 * 1024
NEG = -1e30

HI = lax.Precision.HIGHEST
NT_DIMS = (((1,), (1,)), ((), ()))
TN_DIMS = (((0,), (0,)), ((), ()))
S = jax.ShapeDtypeStruct


def _params(sem):
    return pltpu.CompilerParams(dimension_semantics=sem, vmem_limit_bytes=VMEM_LIMIT)


def _matmul(a, b, *, mode, out_dtype, name, tm, tn, tk, epilogue=None, extra=None, stacked=False, after=None, b_cols=None):
    if mode == "nn":
        m, k = a.shape
        n = b.shape[0] * b.shape[2] if stacked else b.shape[1]
        col0 = 0
        if b_cols is not None:
            assert b_cols[0] % tn == 0, name
            col0, n = b_cols[0] // tn, b_cols[1]
        a_spec = pl.BlockSpec((tm, tk), lambda i, j, kk: (i, kk))
        b_spec = pl.BlockSpec((None, tk, tn), lambda i, j, kk: (j, kk, 0)) if stacked else pl.BlockSpec((tk, tn), lambda i, j, kk: (kk, col0 + j))
        dims = (((1,), (0,)), ((), ()))
    elif mode == "nt":
        m, k = a.shape
        n = b.shape[1] if stacked else b.shape[0]
        a_spec = pl.BlockSpec((tm, tk), lambda i, j, kk: (i, kk))
        b_spec = pl.BlockSpec((None, tn, tk), lambda i, j, kk: (kk, j, 0)) if stacked else pl.BlockSpec((tn, tk), lambda i, j, kk: (j, kk))
        dims = NT_DIMS
    else:
        (k, m), n = a.shape, b.shape[1]
        a_spec = pl.BlockSpec((tk, tm), lambda i, j, kk: (kk, i))
        b_spec = pl.BlockSpec((tk, tn), lambda i, j, kk: (kk, j))
        dims = TN_DIMS
    assert m % tm == 0 and n % tn == 0 and k % tk == 0, (name, m, n, k)
    if stacked:
        assert (tk if mode == "nt" else tn) * N_DEV == (k if mode == "nt" else n), name
    nk = k // tk
    o_spec = pl.BlockSpec((tm, tn), lambda i, j, kk: (i, j))
    in_specs, args = [a_spec, b_spec], [a, b]
    if epilogue == "relu2":
        out_shape = (S((m, n), BF16), S((m, n), BF16))
        out_specs = (o_spec, o_spec)
    elif epilogue == "also_bf16":
        out_shape = (S((m, n), out_dtype), S((m, n), BF16))
        out_specs = (o_spec, o_spec)
    elif stacked and mode == "tn":
        out_shape, out_specs = S((N_DEV, m, tn), out_dtype), pl.BlockSpec((None, tm, tn), lambda i, j, kk: (j, i, 0))
    else:
        out_shape, out_specs = S((m, n), out_dtype), o_spec
    if epilogue == "relu2_bwd":
        in_specs.append(o_spec)
        args.append(extra)
    n_in = len(args)
    if after is not None:
        in_specs.append(pl.BlockSpec(after.shape, lambda i, j, kk: (0,) * after.ndim))
        args.append(after)

    def finish(acc, refs):
        if epilogue == "relu2":
            r = jnp.maximum(acc, 0.0)
            refs[0][...] = (r * r).astype(BF16)
            refs[1][...] = acc.astype(BF16)
        elif epilogue == "also_bf16":
            refs[0][...] = acc.astype(out_dtype)
            refs[1][...] = acc.astype(BF16)
        elif epilogue == "relu2_bwd":
            up = refs[0][...].astype(F32)
            refs[1][...] = (acc * (2.0 * jnp.maximum(up, 0.0))).astype(out_dtype)
        else:
            refs[0][...] = acc.astype(out_dtype)

    def body(a_ref, b_ref, *rest):
        rest = rest[:n_in - 2] + rest[len(args) - 2:]
        part = lax.dot_general(a_ref[...].astype(BF16), b_ref[...].astype(BF16), dims, preferred_element_type=F32)
        if nk == 1:
            finish(part, rest)
            return
        acc_ref = rest[-1]
        kk = pl.program_id(2)

        @pl.when(kk == 0)
        def _():
            acc_ref[...] = part

        @pl.when(kk > 0)
        def _():
            acc_ref[...] += part

        @pl.when(kk == nk - 1)
        def _():
            finish(acc_ref[...], rest[:-1])

    scratch = [] if nk == 1 else [pltpu.VMEM((tm, tn), F32)]
    return pl.pallas_call(
        body, grid=(m // tm, n // tn, nk), in_specs=in_specs, out_specs=out_specs, out_shape=out_shape,
        scratch_shapes=scratch, name=name, compiler_params=_params(("parallel", "parallel", "arbitrary")),
    )(*args)


def _rowcall(body, name, n_rows, tr, ins, outs, scratch=(), into=None):
    in_specs = [pl.BlockSpec(bs, im) for _, bs, im in ins]
    out_specs = [pl.BlockSpec(bs, im) for _, _, bs, im in outs]
    out_shape = [S(sh, dt) for sh, dt, _, _ in outs]
    args = [a for a, _, _ in ins]
    aliases = {}
    kernel = body
    if into is not None:
        buf, bs, im = into
        n_in = len(args)
        in_specs.append(pl.BlockSpec(memory_space=pl.ANY))
        args.append(buf)
        out_specs.append(pl.BlockSpec(bs, im))
        out_shape.append(S(buf.shape, buf.dtype))
        aliases = {n_in: len(out_shape) - 1}

        def kernel(*refs):
            body(*refs[:n_in], *refs[n_in + 1:])

    return pl.pallas_call(
        kernel, grid=(n_rows // tr,), in_specs=in_specs, out_specs=out_specs, out_shape=out_shape,
        input_output_aliases=aliases, scratch_shapes=list(scratch), name=name, compiler_params=_params(("arbitrary",)),
    )(*args)


def _rows(arr, tr, width=None, cb=0):
    width = arr.shape[1] if width is None else width
    return (arr, (tr, width), lambda i, cb=cb: (i, cb))


def _whole(arr):
    nd = arr.ndim
    return (arr, arr.shape, lambda i, nd=nd: (0,) * nd)


def _orow(n_rows, width, dtype, tr):
    return ((n_rows, width), dtype, (tr, width), lambda i: (i, 0))


def _oacc(width):
    return ((1, width), F32, (1, width), lambda i: (0, 0))


def _accumulate(ref, value):
    first = pl.program_id(0) == 0

    @pl.when(first)
    def _():
        ref[...] = value

    @pl.when(jnp.logical_not(first))
    def _():
        ref[...] += value


def _colsum(v):
    return jnp.sum(v, axis=0, keepdims=True)


def _rms_fwd(x, w):
    r = lax.rsqrt(jnp.mean(x * x, axis=-1, keepdims=True) + RMS_EPS)
    return x * r * w


def _rms_bwd(gy, x, w):
    r = lax.rsqrt(jnp.mean(x * x, axis=-1, keepdims=True) + RMS_EPS)
    xn = x * r
    gxn = gy * w
    gx = r * (gxn - xn * jnp.mean(gxn * xn, axis=-1, keepdims=True))
    return gx, _colsum(gy * xn)


def _sigmoid(x):
    return 1.0 / (1.0 + jnp.exp(-x))


def _head_expand(n_heads_pad, n_heads, width):
    h = lax.broadcasted_iota(jnp.int32, (n_heads_pad, n_heads * width), 0)
    c = lax.broadcasted_iota(jnp.int32, (n_heads_pad, n_heads * width), 1)
    return (c // width == h).astype(F32)


def _head_reduce(n_heads, width, n_heads_pad):
    c = lax.broadcasted_iota(jnp.int32, (n_heads * width, n_heads_pad), 0)
    h = lax.broadcasted_iota(jnp.int32, (n_heads * width, n_heads_pad), 1)
    return (c // width == h).astype(F32)


def _block_ones(n, width):
    r = lax.broadcasted_iota(jnp.int32, (n, n), 0)
    c = lax.broadcasted_iota(jnp.int32, (n, n), 1)
    return (r // width == c // width).astype(F32)


def _pre_norm(x, w_pre, tr=512):
    t = x.shape[0]

    def body(x_ref, w_ref, u_ref):
        u_ref[...] = _rms_fwd(x_ref[...], w_ref[...]).astype(BF16)

    return _rowcall(body, "pre_norm", t, tr, [_rows(x, tr), _whole(w_pre)], [_orow(t, D_MODEL, BF16, tr)])[0]


CONV_HALO = 16


def _row_shift(cur, halo, j):
    tr = cur.shape[0]
    r = lax.broadcasted_iota(jnp.int32, (tr, tr), 0)
    c = lax.broadcasted_iota(jnp.int32, (tr, tr), 1)
    main = jnp.dot((c == r + j).astype(BF16), cur, preferred_element_type=F32)
    er = lax.broadcasted_iota(jnp.int32, (CONV_HALO, CONV_HALO), 0)
    ec = lax.broadcasted_iota(jnp.int32, (CONV_HALO, CONV_HALO), 1)
    if j < 0:
        edge = jnp.dot((ec == CONV_HALO + er + j).astype(BF16), halo, preferred_element_type=F32)
        return jnp.concatenate([main[:CONV_HALO] + edge, main[CONV_HALO:]], axis=0)
    edge = jnp.dot((ec == er + j - CONV_HALO).astype(BF16), halo, preferred_element_type=F32)
    return jnp.concatenate([main[:tr - CONV_HALO], main[tr - CONV_HALO:] + edge], axis=0)


def _conv_fwd(proj, conv_w, conv_b, tr=256):
    t = proj.shape[0]
    cb = OFF_XBC // CONV_DIM
    halo = (proj, (CONV_HALO, CONV_DIM), lambda i: (jnp.maximum(i * (tr // CONV_HALO) - 1, 0), cb))

    def body(cur_ref, prev_ref, w_ref, b_ref, o_ref, xc_ref):
        cur = cur_ref[...]
        prev = jnp.where(pl.program_id(0) > 0, prev_ref[...], jnp.zeros_like(prev_ref[...]))
        acc = b_ref[...] + w_ref[3:4, :] * cur.astype(F32)
        for k in range(SSM_CONV - 1):
            acc = acc + w_ref[k:k + 1, :] * _row_shift(cur, prev, -(SSM_CONV - 1 - k))
        o_ref[...] = acc * _sigmoid(acc)
        xc_ref[...] = acc.astype(BF16)

    return _rowcall(body, "conv_fwd", t, tr, [_rows(proj, tr, CONV_DIM, cb), halo, _whole(conv_w), _whole(conv_b)],
                    [_orow(t, CONV_DIM, F32, tr), _orow(t, CONV_DIM, BF16, tr)])


def _dt_fwd(proj, dt_bias_pad, alog_pad, tr=512):
    t = proj.shape[0]

    def body(raw_ref, b_ref, al_ref, dtx_ref, csx_ref, cst_ref):
        v = raw_ref[...] + b_ref[...]
        dt = jnp.maximum(v, 0.0) + jnp.log1p(jnp.exp(-jnp.abs(v)))
        expand = _head_expand(LANE, SSM_HEADS, SSM_HEAD_DIM)
        dtx_ref[...] = _dot_split(dt, expand, 0, 3)
        la = dt * (-jnp.exp(al_ref[...]))
        row = lax.broadcasted_iota(jnp.int32, (SSM_CHUNK, SSM_CHUNK), 0)
        col = lax.broadcasted_iota(jnp.int32, (SSM_CHUNK, SSM_CHUNK), 1)
        tril = (col <= row).astype(F32)
        cs = jnp.concatenate([_dot_split(tril, la[k * SSM_CHUNK:(k + 1) * SSM_CHUNK, :], 1, 3) for k in range(tr // SSM_CHUNK)], axis=0)
        csx_ref[...] = _dot_split(cs, expand, 0, 3)
        cst_ref[...] = cs.T

    return _rowcall(body, "dt_fwd", t, tr, [_rows(proj, tr, LANE, PB_DT // LANE), _whole(dt_bias_pad), _whole(alog_pad)],
                    [_orow(t, SSM_INNER, F32, tr), _orow(t, SSM_INNER, F32, tr), ((LANE, t), F32, (LANE, tr), lambda i: (0, i))])


def _gate_norm_fwd(y_ssd, xa, proj, dskip_x, norm_w, tr=256):
    t = y_ssd.shape[0]
    gw = SSM_INNER // SSM_GROUPS

    def body(y_ref, xs_ref, z_ref, d_ref, w_ref, o_ref):
        z = z_ref[...].astype(F32)
        y3 = (y_ref[...].astype(F32) + d_ref[...] * xs_ref[...]) * (z * _sigmoid(z))
        for g in range(SSM_GROUPS):
            sl = slice(g * gw, (g + 1) * gw)
            o_ref[:, sl] = _rms_fwd(y3[:, sl], w_ref[:, sl]).astype(BF16)

    return _rowcall(body, "gate_norm_fwd", t, tr,
                    [_rows(y_ssd, tr), _rows(xa, tr, SSM_INNER, 0), _rows(proj, tr, SSM_INNER, OFF_Z // SSM_INNER), _whole(dskip_x), _whole(norm_w)],
                    [_orow(t, SSM_INNER, BF16, tr)])[0]


def _gating_fwd(proj, b_gate, att_p, ssm_p, tr=512):
    t = proj.shape[0]

    def body(gl_ref, b_ref, a_ref, s_ref, o_ref):
        gates = _sigmoid(gl_ref[...].astype(F32) + b_ref[...])
        o_ref[...] = (gates[:, :D_MODEL] * a_ref[...].astype(F32) + gates[:, D_MODEL:] * s_ref[...].astype(F32)).astype(BF16)

    return _rowcall(body, "gating_fwd", t, tr, [_rows(proj, tr, 2 * D_MODEL, OFF_GL // (2 * D_MODEL)), _whole(b_gate), _rows(att_p, tr), _rows(ssm_p, tr)],
                    [_orow(t, D_MODEL, BF16, tr)])[0]


def _mix_post_ffn_pre(x, mixed, w_post, w_fpre, tr=512):
    t = x.shape[0]

    def body(x_ref, m_ref, wp_ref, wf_ref, h1_ref, f_ref):
        h1 = x_ref[...] + _rms_fwd(m_ref[...], wp_ref[...])
        h1_ref[...] = h1
        f_ref[...] = _rms_fwd(h1, wf_ref[...]).astype(BF16)

    return _rowcall(body, "mix_post_ffn_pre", t, tr, [_rows(x, tr), _rows(mixed, tr), _whole(w_post), _whole(w_fpre)],
                    [_orow(t, D_MODEL, F32, tr), _orow(t, D_MODEL, BF16, tr)])


def _loss_and_ffn_post_bwd(h1, dn, w_fpost, target, tr=512):
    t = h1.shape[0]

    def body(h1_ref, dn_ref, w_ref, tg_ref, loss_ref, gh2_ref, gdn_ref, gw_ref):
        dn = dn_ref[...]
        w = w_ref[...]
        err = h1_ref[...] + _rms_fwd(dn, w) - tg_ref[...]
        _accumulate(loss_ref, jnp.zeros((1, LANE), F32) + 0.5 * jnp.sum(jnp.mean(err * err, axis=-1, keepdims=True)))
        gh2 = err * (1.0 / D_MODEL)
        gh2_ref[...] = gh2
        gdn, gw = _rms_bwd(gh2, dn, w)
        gdn_ref[...] = gdn.astype(BF16)
        _accumulate(gw_ref, gw)

    return _rowcall(body, "loss_ffn_post_bwd", t, tr, [_rows(h1, tr), _rows(dn, tr), _whole(w_fpost), _rows(target, tr)],
                    [_oacc(LANE), _orow(t, D_MODEL, F32, tr), _orow(t, D_MODEL, BF16, tr), _oacc(D_MODEL)])


def _ffn_pre_mix_post_bwd(g_h2, g_f, h1, w_fpre, mixed, w_post, tr=512):
    t = h1.shape[0]

    def body(gh2_ref, gf_ref, h1_ref, wf_ref, m_ref, wp_ref, gh1_ref, gm_ref, gwf_ref, gwp_ref):
        gx, gwf = _rms_bwd(gf_ref[...], h1_ref[...], wf_ref[...])
        gh1 = gh2_ref[...] + gx
        gh1_ref[...] = gh1
        gm, gwp = _rms_bwd(gh1, m_ref[...], wp_ref[...])
        gm_ref[...] = gm.astype(BF16)
        _accumulate(gwf_ref, gwf)
        _accumulate(gwp_ref, gwp)

    return _rowcall(body, "ffn_pre_mix_post_bwd", t, tr,
                    [_rows(g_h2, tr), _rows(g_f, tr), _rows(h1, tr), _whole(w_fpre), _rows(mixed, tr), _whole(w_post)],
                    [_orow(t, D_MODEL, F32, tr), _orow(t, D_MODEL, BF16, tr), _oacc(D_MODEL), _oacc(D_MODEL)])


def _gating_bwd(g_mixin, proj, b_gate, att_p, ssm_p, g_proj, tr=512):
    t = proj.shape[0]

    def body(gm_ref, gl_ref, b_ref, a_ref, s_ref, ga_ref, gs_ref, gb_ref, ggl_ref):
        gates = _sigmoid(gl_ref[...].astype(F32) + b_ref[...])
        gm = gm_ref[...].astype(F32)
        g_att, g_ssm = gates[:, :D_MODEL], gates[:, D_MODEL:]
        ga_ref[...] = (gm * g_att).astype(BF16)
        gs_ref[...] = (gm * g_ssm).astype(BF16)
        ggl_a = gm * a_ref[...].astype(F32) * g_att * (1.0 - g_att)
        ggl_s = gm * s_ref[...].astype(F32) * g_ssm * (1.0 - g_ssm)
        ggl_ref[:, :D_MODEL] = ggl_a.astype(BF16)
        ggl_ref[:, D_MODEL:] = ggl_s.astype(BF16)
        _accumulate(gb_ref.at[:, :D_MODEL], _colsum(ggl_a))
        _accumulate(gb_ref.at[:, D_MODEL:], _colsum(ggl_s))

    return _rowcall(body, "gating_bwd", t, tr,
                    [_rows(g_mixin, tr), _rows(proj, tr, 2 * D_MODEL, OFF_GL // (2 * D_MODEL)), _whole(b_gate), _rows(att_p, tr), _rows(ssm_p, tr)],
                    [_orow(t, D_MODEL, BF16, tr), _orow(t, D_MODEL, BF16, tr), _oacc(2 * D_MODEL)],
                    into=(g_proj, (tr, 2 * D_MODEL), lambda i: (i, OFF_GL // (2 * D_MODEL))))


def _gate_norm_bwd(g_y4, y_ssd, xa, proj, dskip_x, norm_w, g_proj, tr=256):
    t = y_ssd.shape[0]
    gw = SSM_INNER // SSM_GROUPS

    def body(g_ref, y_ref, xs_ref, z_ref, d_ref, w_ref, gy2_ref, gnw_ref, gdx_ref, gd_ref, gz_ref):
        z = z_ref[...].astype(F32)
        xs = xs_ref[...]
        sg = _sigmoid(z)
        sz = z * sg
        y2 = y_ref[...].astype(F32) + d_ref[...] * xs
        y3 = y2 * sz
        g4 = g_ref[...].astype(F32)
        for g in range(SSM_GROUPS):
            sl = slice(g * gw, (g + 1) * gw)
            gy3, gnw = _rms_bwd(g4[:, sl], y3[:, sl], w_ref[:, sl])
            _accumulate(gnw_ref.at[:, sl], gnw)
            gy2 = gy3 * sz[:, sl]
            gy2_ref[:, sl] = gy2
            gz_ref[:, sl] = (gy3 * y2[:, sl] * (sg[:, sl] * (1.0 + z[:, sl] * (1.0 - sg[:, sl])))).astype(BF16)
            _accumulate(gdx_ref.at[:, sl], _colsum(gy2 * xs[:, sl]))
        tot = jnp.broadcast_to(gdx_ref[...], (8, SSM_INNER))
        gd_ref[...] = jnp.dot(tot, _head_reduce(SSM_HEADS, SSM_HEAD_DIM, LANE), precision=HI, preferred_element_type=F32)[0:1, :]

    return _rowcall(body, "gate_norm_bwd", t, tr,
                    [_rows(g_y4, tr), _rows(y_ssd, tr), _rows(xa, tr, SSM_INNER, 0), _rows(proj, tr, SSM_INNER, OFF_Z // SSM_INNER), _whole(dskip_x), _whole(norm_w)],
                    [_orow(t, SSM_INNER, F32, tr), _oacc(SSM_INNER), _oacc(SSM_INNER), _oacc(LANE)],
                    into=(g_proj, (tr, SSM_INNER), lambda i: (i, OFF_Z // SSM_INNER)))


def _dt_bwd(g_dtx, ga_rows, proj, dt_bias_pad, g_proj, tr=512):
    t = proj.shape[0]
    tail = PROJ_W - OFF_DT

    def body(g_ref, ga_ref, raw_ref, b_ref, gb_ref, gal_ref, o_ref):
        red = _head_reduce(SSM_HEADS, SSM_HEAD_DIM, LANE)
        gdt = _dot_split(g_ref[...], red, 0, 3)
        graw = gdt * _sigmoid(raw_ref[...] + b_ref[...])
        o_ref[...] = jnp.concatenate([graw.astype(BF16), jnp.zeros((tr, tail - LANE), BF16)], axis=1)
        _accumulate(gb_ref, _colsum(graw))
        tot = jnp.broadcast_to(_colsum(ga_ref[...]), (8, SSM_INNER))
        gal_ref[...] = jnp.dot(tot, red, precision=HI, preferred_element_type=F32)[0:1, :]

    return _rowcall(body, "dt_bwd", t, tr, [_rows(g_dtx, tr), _whole(ga_rows), _rows(proj, tr, LANE, PB_DT // LANE), _whole(dt_bias_pad)],
                    [_oacc(LANE), _oacc(LANE)], into=(g_proj, (tr, tail), lambda i: (i, OFF_DT // tail)))


def _conv_bwd(g_xs, g_b, g_c, xc, proj, conv_w, g_proj, tr=256):
    t = proj.shape[0]
    n_blk = t // tr
    cb = OFF_XBC // CONV_DIM
    nb, nc = SSM_INNER, SSM_INNER + SSM_GROUPS * SSM_STATE
    def nxt(arr, width):
        return (arr, (CONV_HALO, width), lambda i: (jnp.minimum((i + 1) * (tr // CONV_HALO), t // CONV_HALO - 1), 0))

    def body(gxs_ref, gxs_n, gb_ref, gb_n, gc_ref, gc_n, xc_ref, xc_n, x_ref, w_ref, gcb_ref, gw0, gw1, gw2, gw3, o_ref):
        def gxc_of(gxs, gb, gc, xc, keep):
            xcf = xc[...].astype(F32)
            sg = _sigmoid(xcf)
            dsilu = jnp.where(keep, sg * (1.0 + xcf * (1.0 - sg)), 0.0)
            return jnp.concatenate([gxs[...] * dsilu[:, :nb], gb[...] * dsilu[:, nb:nc], gc[...] * dsilu[:, nc:]], axis=1)

        gxc = gxc_of(gxs_ref, gb_ref, gc_ref, xc_ref, True)
        gxc16 = gxc.astype(BF16)
        nxt16 = gxc_of(gxs_n, gb_n, gc_n, xc_n, pl.program_id(0) < n_blk - 1).astype(BF16)
        x = x_ref[...].astype(F32)
        acc = w_ref[3:4, :] * gxc
        _accumulate(gw3, _colsum(gxc * x))
        _accumulate(gcb_ref, _colsum(gxc))
        for k, gw in enumerate((gw0, gw1, gw2)):
            shifted = _row_shift(gxc16, nxt16, SSM_CONV - 1 - k)
            acc = acc + w_ref[k:k + 1, :] * shifted
            _accumulate(gw, _colsum(shifted * x))
        o_ref[...] = acc.astype(BF16)

    ins = []
    for arr, width in ((g_xs, SSM_INNER), (g_b, nc - nb), (g_c, nc - nb), (xc, CONV_DIM)):
        ins += [_rows(arr, tr), nxt(arr, width)]
    ins += [_rows(proj, tr, CONV_DIM, cb), _whole(conv_w)]
    return _rowcall(body, "conv_bwd", t, tr, ins, [_oacc(CONV_DIM)] * 5, into=(g_proj, (tr, CONV_DIM), lambda i: (i, cb)))


def _pre_norm_bwd(g_h1, g_u, x, w_pre, tr=512):
    t = x.shape[0]

    def body(gh_ref, gu_ref, x_ref, w_ref, gx_ref, gw_ref):
        gx, gw = _rms_bwd(gu_ref[...], x_ref[...], w_ref[...])
        gx_ref[...] = gh_ref[...] + gx
        _accumulate(gw_ref, gw)

    return _rowcall(body, "pre_norm_bwd", t, tr, [_rows(g_h1, tr), _rows(g_u, tr), _rows(x, tr), _whole(w_pre)],
                    [_orow(t, D_MODEL, F32, tr), _oacc(D_MODEL)])


def _alibi_slopes(n):
    def pow2(m):
        start = 2.0 ** (-8.0 / m)
        return [start ** (i + 1) for i in range(m)]
    if (n & (n - 1)) == 0:
        s = pow2(n)
    else:
        c = 2 ** int(math.floor(math.log2(n)))
        s = pow2(c) + pow2(2 * c)[0::2][: n - c]
    return np.array(s, dtype=np.float32)


def _slope_rows():
    s = _alibi_slopes(N_ATT_HEADS).reshape(N_ATT_HEADS // 2, 2)
    return jnp.asarray(np.broadcast_to(np.repeat(s, HEAD_DIM, axis=1)[:, None, :], (N_ATT_HEADS // 2, 8, LANE)).copy())


ATT_MAX_BLOCK_ROWS = 2048


RESIDUE_MAJOR_FROM = 16


class _AttLayout:
    def __init__(self, t, dil):
        self.t, self.dil = t, dil
        self.rows = t // dil
        self.residue_major = dil >= RESIDUE_MAJOR_FROM
        if self.residue_major:
            bq, self.stride = min(512, self.rows), 1
        else:
            bq, self.stride = min(512, self.rows, ATT_MAX_BLOCK_ROWS // dil), dil
        self.nsub = bq // ATT_BLOCK
        self.nblk = self.rows // bq
        self.rb = bq * self.stride
        self.pb = ATT_BLOCK * self.stride
        self.n_pb = self.rows * self.stride // self.pb
        self.out_dtype = F32 if self.stride > 1 else BF16

    def qkv(self, proj):
        pb, pb16 = proj
        if self.residue_major:
            return pb16.reshape(self.rows, self.dil * PB_W), PB_W // LANE, 0
        return (pb if self.stride > 1 else pb16), 0, 0

    def act(self, a):
        return a.reshape(self.rows, self.dil * ATT_WIDTH) if self.residue_major else a

    def act_shape(self):
        return (self.rows, self.dil * ATT_WIDTH) if self.residue_major else (self.t, ATT_WIDTH)

    def col(self, r, band, c):
        return r * band + c if self.residue_major else c


def _residue_rows(r, stride, first_block, n_blocks=1):
    if stride == 1:
        return pl.ds(first_block * ATT_BLOCK, n_blocks * ATT_BLOCK)
    return pl.ds(r + first_block * ATT_BLOCK * stride, n_blocks * ATT_BLOCK, stride=stride)


def _lane_half():
    return lax.broadcasted_iota(jnp.int32, (ATT_BLOCK, LANE), 1) // HEAD_DIM


def _att_scores_mask(dil, first):
    iq = lax.broadcasted_iota(jnp.int32, (ATT_BLOCK, 2 * ATT_BLOCK), 0)
    jk = lax.broadcasted_iota(jnp.int32, (ATT_BLOCK, 2 * ATT_BLOCK), 1)
    dist = ATT_BLOCK + iq - jk
    valid = (dist >= 0) & (dist <= ATT_BLOCK) & (jnp.logical_not(first) | (jk >= ATT_BLOCK))
    return (dist * dil).astype(F32), valid


def _stack_heads(x):
    half = _lane_half()
    return jnp.concatenate([jnp.where(half == 0, x, jnp.zeros_like(x)), jnp.where(half == 1, x, jnp.zeros_like(x))], axis=0)


def _unstack_heads(x):
    return jnp.where(_lane_half() == 0, x[:ATT_BLOCK], x[ATT_BLOCK:])


def _head_columns(x):
    return jnp.concatenate([x[:, 0:1], x[:, HEAD_DIM:HEAD_DIM + 1]], axis=0)


def _stacked_bias(sl_ref, dist, valid):
    d2 = jnp.concatenate([dist, dist], axis=0)
    v2 = jnp.concatenate([valid, valid], axis=0)
    top = lax.broadcasted_iota(jnp.int32, d2.shape, 0) < ATT_BLOCK
    slope = jnp.where(top, sl_ref[0:1, 0:1], sl_ref[0:1, HEAD_DIM:HEAD_DIM + 1])
    return jnp.where(v2, -slope * d2, NEG)


def _att_fwd(proj, dil, slopes):
    t = proj[0].shape[0]
    lay = _AttLayout(t, dil)
    nsub, nblk, rb, pb = lay.nsub, lay.nblk, lay.rb, lay.pb
    src, band, qb = lay.qkv(proj)
    aw = ATT_WIDTH // LANE

    def spec(off, prev=False):
        if prev:
            return pl.BlockSpec((pb, LANE), lambda hp, i, r: (jnp.maximum(i * nsub - 1, 0), lay.col(r, band, qb + off + hp)))
        return pl.BlockSpec((rb, LANE), lambda hp, i, r: (i, lay.col(r, band, qb + off + hp)))

    o_spec = pl.BlockSpec((rb, LANE), lambda hp, i, r: (i, lay.col(r, aw, hp)))

    def body(q_ref, kc_ref, kp_ref, vc_ref, vp_ref, sl_ref, o_ref, lse_ref):
        i, r = pl.program_id(1), pl.program_id(2)
        half = _lane_half()
        for sub in range(nsub):
            rs = _residue_rows(r, lay.stride, sub)
            q = (q_ref[rs, :] * (HEAD_DIM ** -0.5)).astype(BF16)
            if sub == 0:
                r0 = _residue_rows(r, lay.stride, 0)
                kk = jnp.concatenate([kp_ref[r0, :], kc_ref[rs, :]], axis=0).astype(BF16)
                vv = jnp.concatenate([vp_ref[r0, :], vc_ref[rs, :]], axis=0).astype(BF16)
                first = i == 0
            else:
                ks = _residue_rows(r, lay.stride, sub - 1, 2)
                kk, vv = kc_ref[ks, :].astype(BF16), vc_ref[ks, :].astype(BF16)
                first = jnp.bool_(False)
            dist, valid = _att_scores_mask(dil, first)
            s = lax.dot_general(_stack_heads(q), kk, NT_DIMS, preferred_element_type=F32) + _stacked_bias(sl_ref, dist, valid)
            m = jnp.max(s, axis=-1, keepdims=True)
            p = jnp.exp(s - m)
            l = jnp.sum(p, axis=-1, keepdims=True)
            o_ref[rs, :] = _unstack_heads(jnp.dot(p.astype(BF16), vv, preferred_element_type=F32) / l).astype(lay.out_dtype)
            lse_ref[rs, :] = _unstack_heads(jnp.broadcast_to(m + jnp.log(l), (2 * ATT_BLOCK, LANE)))

    o, lse = pl.pallas_call(
        body, grid=(N_ATT_HEADS // 2, nblk, dil),
        in_specs=[spec(0), spec(6), spec(6, True), spec(12), spec(12, True), pl.BlockSpec((None, 8, LANE), lambda hp, i, r: (hp, 0, 0))],
        out_specs=[o_spec, o_spec], out_shape=[S(lay.act_shape(), lay.out_dtype), S(lay.act_shape(), F32)],
        name=f"att_fwd_d{dil}", compiler_params=_params(("parallel", "parallel", "arbitrary")),
    )(src, src, src, src, src, slopes)
    return o.reshape(t, ATT_WIDTH), lse.reshape(t, ATT_WIDTH)


def _att_combine(outs, lses, tr=512):
    t = outs[0].shape[0]

    def body(o0, o1, o2, l0, l1, l2, att_ref, lse_ref):
        ls = [l0[...], l1[...], l2[...]]
        m = jnp.maximum(jnp.maximum(ls[0], ls[1]), ls[2])
        ws = [jnp.exp(l - m) for l in ls]
        tot = ws[0] + ws[1] + ws[2]
        num = ws[0] * o0[...].astype(F32) + ws[1] * o1[...].astype(F32) + ws[2] * o2[...].astype(F32)
        att_ref[...] = (num / tot).astype(BF16)
        lse_ref[...] = m + jnp.log(tot)

    return _rowcall(body, "att_combine", t, tr, [_rows(a, tr) for a in list(outs) + list(lses)],
                    [_orow(t, ATT_WIDTH, BF16, tr), _orow(t, ATT_WIDTH, F32, tr)])


def _att_delta(g_att, att, tr=512):
    t = att.shape[0]

    def body(g_ref, a_ref, o_ref):
        prod = g_ref[...] * a_ref[...].astype(F32)
        o_ref[...] = _dot_split(prod, _block_ones(ATT_WIDTH, HEAD_DIM), 0, 3)

    return _rowcall(body, "att_delta", t, tr, [_rows(g_att, tr), _rows(att, tr)], [_orow(t, ATT_WIDTH, F32, tr)])[0]


def _att_bwd(proj, g_att, lse, delta, dil, slopes):
    t = proj[0].shape[0]
    lay = _AttLayout(t, dil)
    nsub, nblk, rb, pb, n_pb = lay.nsub, lay.nblk, lay.rb, lay.pb, lay.n_pb
    src, band, qb = lay.qkv(proj)
    aw = ATT_WIDTH // LANE

    def near(i, which):
        return jnp.maximum(i * nsub - 1, 0) if which == "prev" else jnp.minimum((i + 1) * nsub, n_pb - 1)

    def pspec(off, which=None):
        if which:
            return pl.BlockSpec((pb, LANE), lambda hp, i, r: (near(i, which), lay.col(r, band, qb + off + hp)))
        return pl.BlockSpec((rb, LANE), lambda hp, i, r: (i, lay.col(r, band, qb + off + hp)))

    def aspec(which=None):
        if which:
            return pl.BlockSpec((pb, LANE), lambda hp, i, r: (near(i, which), lay.col(r, aw, hp)))
        return pl.BlockSpec((rb, LANE), lambda hp, i, r: (i, lay.col(r, aw, hp)))

    scale = HEAD_DIM ** -0.5

    def body(q_ref, qn_ref, kc_ref, kp_ref, vc_ref, vp_ref, do_ref, don_ref, lse_ref, lsen_ref, dl_ref, dln_ref, sl_ref,
             dq_ref, dk_ref, dv_ref):
        i, r = pl.program_id(1), pl.program_id(2)
        half = _lane_half()

        def tile_grads(q, do, lse_q, dl_q, kk, vv, dist, valid):
            q2, do2 = _stack_heads(q), _stack_heads(do)
            s = lax.dot_general(q2, kk, NT_DIMS, preferred_element_type=F32) + _stacked_bias(sl_ref, dist, valid)
            p = jnp.exp(s - _head_columns(lse_q))
            dp = lax.dot_general(do2, vv, NT_DIMS, preferred_element_type=F32)
            ds16 = (p * (dp - _head_columns(dl_q))).astype(BF16)
            dq = _unstack_heads(jnp.dot(ds16, kk, preferred_element_type=F32)) * scale
            dk = lax.dot_general(ds16, q2, TN_DIMS, preferred_element_type=F32)
            dv = lax.dot_general(p.astype(BF16), do2, TN_DIMS, preferred_element_type=F32)
            return dq, dk, dv

        carry_k = carry_v = None
        for sub in range(nsub):
            rs = _residue_rows(r, lay.stride, sub)
            q = (q_ref[rs, :] * scale).astype(BF16)
            do = do_ref[rs, :].astype(BF16)
            if sub == 0:
                r0 = _residue_rows(r, lay.stride, 0)
                kk = jnp.concatenate([kp_ref[r0, :], kc_ref[rs, :]], axis=0).astype(BF16)
                vv = jnp.concatenate([vp_ref[r0, :], vc_ref[rs, :]], axis=0).astype(BF16)
                first = i == 0
            else:
                ks = _residue_rows(r, lay.stride, sub - 1, 2)
                kk, vv = kc_ref[ks, :].astype(BF16), vc_ref[ks, :].astype(BF16)
                first = jnp.bool_(False)
            dist, valid = _att_scores_mask(dil, first)
            dq, dk2, dv2 = tile_grads(q, do, lse_ref[rs, :], dl_ref[rs, :], kk, vv, dist, valid)
            dq_ref[rs, :] = dq.astype(lay.out_dtype)
            if sub > 0:
                rp = _residue_rows(r, lay.stride, sub - 1)
                dk_ref[rp, :] = (carry_k + dk2[:ATT_BLOCK, :]).astype(lay.out_dtype)
                dv_ref[rp, :] = (carry_v + dv2[:ATT_BLOCK, :]).astype(lay.out_dtype)
            carry_k, carry_v = dk2[ATT_BLOCK:, :], dv2[ATT_BLOCK:, :]
        rl = _residue_rows(r, lay.stride, nsub - 1)
        rn = _residue_rows(r, lay.stride, 0)
        iq = lax.broadcasted_iota(jnp.int32, (ATT_BLOCK, ATT_BLOCK), 0)
        jk = lax.broadcasted_iota(jnp.int32, (ATT_BLOCK, ATT_BLOCK), 1)
        dist_i = ATT_BLOCK + iq - jk
        valid = (dist_i >= 0) & (dist_i <= ATT_BLOCK) & (i < nblk - 1)
        qn = (qn_ref[rn, :] * scale).astype(BF16)
        _, dk1, dv1 = tile_grads(qn, don_ref[rn, :].astype(BF16), lsen_ref[rn, :], dln_ref[rn, :],
                                 kc_ref[rl, :].astype(BF16), vc_ref[rl, :].astype(BF16), (dist_i * dil).astype(F32), valid)
        dk_ref[rl, :] = (carry_k + dk1).astype(lay.out_dtype)
        dv_ref[rl, :] = (carry_v + dv1).astype(lay.out_dtype)

    gv, lv, dlv = lay.act(g_att), lay.act(lse), lay.act(delta)
    dq, dk, dv = pl.pallas_call(
        body, grid=(N_ATT_HEADS // 2, nblk, dil),
        in_specs=[pspec(0), pspec(0, "next"), pspec(6), pspec(6, "prev"), pspec(12), pspec(12, "prev"),
                  aspec(), aspec("next"), aspec(), aspec("next"), aspec(), aspec("next"),
                  pl.BlockSpec((None, 8, LANE), lambda hp, i, r: (hp, 0, 0))],
        out_specs=[aspec(), aspec(), aspec()], out_shape=[S(lay.act_shape(), lay.out_dtype)] * 3,
        name=f"att_bwd_d{dil}", compiler_params=_params(("parallel", "parallel", "arbitrary")),
    )(src, src, src, src, src, src, gv, gv, lv, lv, dlv, dlv, slopes)
    return dq.reshape(t, ATT_WIDTH), dk.reshape(t, ATT_WIDTH), dv.reshape(t, ATT_WIDTH)


def _att_grad_sum(dqs, dks, dvs, g_proj, tr=2048):
    t = dqs[0].shape[0]
    cw = 2 * LANE
    per = ATT_WIDTH // cw
    arrays = list(dqs) + list(dks) + list(dvs)
    n_pat = len(dqs)

    def body(*refs):
        o_ref = refs[-1]
        which = pl.program_id(1) // per
        tot = jnp.zeros((tr, cw), F32)
        for s in range(3):
            part = refs[s * n_pat][...].astype(F32)
            for g in range(1, n_pat):
                part = part + refs[s * n_pat + g][...].astype(F32)
            tot = jnp.where(which == s, part, tot)
        o_ref[...] = tot.astype(BF16)

    in_specs = [pl.BlockSpec((tr, cw), lambda i, c, s=s: (i, jnp.clip(c - per * s, 0, per - 1))) for s in range(3) for _ in range(n_pat)]
    in_specs.append(pl.BlockSpec(memory_space=pl.ANY))
    return pl.pallas_call(
        lambda *refs: body(*refs[:len(arrays)], refs[-1]), grid=(t // tr, 3 * per), in_specs=in_specs,
        out_specs=pl.BlockSpec((tr, cw), lambda i, c: (i, OFF_QKV // cw + c)), out_shape=S(g_proj.shape, g_proj.dtype),
        input_output_aliases={len(arrays): 0}, name="att_grad_sum", compiler_params=_params(("arbitrary", "arbitrary")),
    )(*arrays, g_proj)


def _ssd_common(xs, dtx, cs, cs_t):
    ch = SSM_CHUNK
    row = lax.broadcasted_iota(jnp.int32, (ch, ch), 0)
    col = lax.broadcasted_iota(jnp.int32, (ch, ch), 1)
    cs_last = cs[ch - 1:ch, :]
    return dict(tril=col <= row, row=row, col=col, cs=cs, cs_t=cs_t, cs_last=cs_last,
                e=jnp.exp(cs), w=jnp.exp(cs_last - cs), xd=xs * dtx)


def _dot_split(a, b, split, terms=2):
    ops = [a, b]
    rest = ops[split]
    other = ops[1 - split].astype(BF16)
    out = None
    for _ in range(terms):
        piece = rest.astype(BF16)
        rest = rest - piece.astype(F32)
        part = jnp.dot(other, piece, preferred_element_type=F32) if split == 1 else jnp.dot(piece, other, preferred_element_type=F32)
        out = part if out is None else out + part
    return out


def _decay_col(cs_t, heads_per_group):
    r = lax.broadcasted_iota(jnp.int32, (heads_per_group * SSM_HEAD_DIM, SSM_STATE), 0) // SSM_HEAD_DIM
    out = jnp.zeros((heads_per_group * SSM_HEAD_DIM, SSM_STATE), F32)
    for j in range(heads_per_group):
        out = jnp.where(r == j, jnp.exp(cs_t[j:j + 1, SSM_CHUNK - 1:SSM_CHUNK]), out)
    return out


SSD_GROUPS_PER_STEP = 4


def _ssd_specs(t):
    hg = SSM_HEADS // SSM_GROUPS
    gw = hg * SSM_HEAD_DIM
    nb0 = SSM_INNER // SSM_STATE
    return hg, gw, nb0


def _ssd_group_views(gi, gw, wide, narrow, stacked):
    w = [r.at[:, pl.ds(gi * gw, gw)] for r in wide]
    n = [r.at[:, pl.ds(gi * SSM_STATE, SSM_STATE)] for r in narrow]
    return w, n, [r.at[gi] for r in stacked]


def _ssd_fwd(xa, dtx, csx, cst_g):
    t = xa.shape[0]
    nch = t // SSM_CHUNK
    hg, gw, nb0 = _ssd_specs(t)
    ch = SSM_CHUNK
    gp = SSD_GROUPS_PER_STEP

    def body(xs_ref, b_ref, c_ref, dtx_ref, cs_ref, cst_ref, y_ref, st_ref, h_scr):
        for gi in range(gp):
            (xs_g, dtx_g, cs_g, y_g), (b_g, c_g), (cst_gi, st_g) = _ssd_group_views(
                gi, gw, (xs_ref, dtx_ref, cs_ref, y_ref), (b_ref, c_ref), (cst_ref, st_ref))
            group_body(pl.program_id(0), pl.program_id(1) * gp + gi, xs_g, b_g, c_g, dtx_g, cs_g, cst_gi, y_g, st_g, h_scr)

    def group_body(cc, g, xs_ref, b_ref, c_ref, dtx_ref, cs_ref, cst_ref, y_ref, st_ref, h_scr):
        @pl.when(cc == 0)
        def _():
            h_scr[g] = jnp.zeros((gw, SSM_STATE), F32)

        q = _ssd_common(xs_ref[...], dtx_ref[...], cs_ref[...], cst_ref[...])
        bb, cb = b_ref[...].astype(BF16), c_ref[...].astype(BF16)
        h = h_scr[g]
        st_ref[...] = h
        xd16 = q["xd"].astype(BF16)
        c_both = lax.dot_general(cb, jnp.concatenate([bb, h.astype(BF16)], axis=0), NT_DIMS, preferred_element_type=F32)
        cbm = c_both[:, :SSM_STATE]
        y = c_both[:, SSM_STATE:] * q["e"]
        lane_head = lax.broadcasted_iota(jnp.int32, (ch, gw), 1) // SSM_HEAD_DIM
        gmats, xds = [], []
        for j in range(hg):
            diff = q["cs"][:, j * SSM_HEAD_DIM:j * SSM_HEAD_DIM + 1] - q["cs_t"][j:j + 1, :]
            gmats.append((cbm * jnp.exp(jnp.where(q["tril"], diff, NEG))).astype(BF16))
            xds.append(jnp.where(lane_head == j, xd16, jnp.zeros_like(xd16)))
        y = y + jnp.dot(jnp.concatenate(gmats, axis=1), jnp.concatenate(xds, axis=0), preferred_element_type=F32)
        y_ref[...] = y.astype(BF16)
        s_new = lax.dot_general((q["xd"] * q["w"]).astype(BF16), bb, TN_DIMS, preferred_element_type=F32)
        h_scr[g] = _decay_col(q["cs_t"], hg) * h + s_new

    wide = pl.BlockSpec((ch, gp * gw), lambda cc, g: (cc, g))
    return pl.pallas_call(
        body, grid=(nch, SSM_GROUPS // gp),
        in_specs=[wide,
                  pl.BlockSpec((ch, gp * SSM_STATE), lambda cc, g: (cc, nb0 // gp + g)),
                  pl.BlockSpec((ch, gp * SSM_STATE), lambda cc, g: (cc, (nb0 + SSM_GROUPS) // gp + g)),
                  wide, wide,
                  pl.BlockSpec((gp, 8, ch), lambda cc, g: (g, 0, cc))],
        out_specs=[wide, pl.BlockSpec((None, gp, gw, SSM_STATE), lambda cc, g: (cc, g, 0, 0))],
        out_shape=[S((t, SSM_INNER), BF16), S((nch, SSM_GROUPS, gw, SSM_STATE), F32)],
        scratch_shapes=[pltpu.VMEM((SSM_GROUPS, gw, SSM_STATE), F32)],
        name="ssd_fwd", compiler_params=_params(("arbitrary", "arbitrary")),
    )(xa, xa, xa, dtx, csx, cst_g)


def _ssd_bwd(xa, dtx, csx, cst_g, alog_x, g_y, states, dskip_x):
    t = xa.shape[0]
    nch = t // SSM_CHUNK
    hg, gw, nb0 = _ssd_specs(t)
    ch = SSM_CHUNK
    gp = SSD_GROUPS_PER_STEP

    def rc(cc):
        return nch - 1 - cc

    def body(xs_ref, b_ref, c_ref, dtx_ref, cs_ref, cst_ref, alx_ref, gy_ref, st_ref, dsk_ref,
             gxs_ref, gb_ref, gc_ref, gdt_ref, ga_ref, gh_scr):
        for gi in range(gp):
            wide, narrow, stacked = _ssd_group_views(
                gi, gw, (xs_ref, dtx_ref, cs_ref, alx_ref, gy_ref, dsk_ref, gxs_ref, gdt_ref, ga_ref), (b_ref, c_ref, gb_ref, gc_ref),
                (cst_ref, st_ref))
            xs_g, dtx_g, cs_g, alx_g, gy_g, dsk_g, gxs_g, gdt_g, ga_g = wide
            b_g, c_g, gb_g, gc_g = narrow
            group_body(pl.program_id(0), pl.program_id(1) * gp + gi, xs_g, b_g, c_g, dtx_g, cs_g, stacked[0], alx_g, gy_g, stacked[1],
                       dsk_g, gxs_g, gb_g, gc_g, gdt_g, ga_g, gh_scr)

    def group_body(cc, g, xs_ref, b_ref, c_ref, dtx_ref, cs_ref, cst_ref, alx_ref, gy_ref, st_ref, dsk_ref,
                   gxs_ref, gb_ref, gc_ref, gdt_ref, ga_ref, gh_scr):
        @pl.when(cc == 0)
        def _():
            gh_scr[g] = jnp.zeros((gw, SSM_STATE), F32)

        xs, dtx = xs_ref[...], dtx_ref[...]
        q = _ssd_common(xs, dtx, cs_ref[...], cst_ref[...])
        cs, cs_t, e, w, xd = q["cs"], q["cs_t"], q["e"], q["w"], q["xd"]
        bb, cb = b_ref[...].astype(BF16), c_ref[...].astype(BF16)
        gy = gy_ref[...]
        gy16, xd16 = gy.astype(BF16), xd.astype(BF16)
        h = st_ref[...]
        h16 = h.astype(BF16)
        ghn = gh_scr[g]
        ghn16 = ghn.astype(BF16)
        seg = _block_ones(gw, SSM_HEAD_DIM)
        c_both = lax.dot_general(cb, jnp.concatenate([bb, h16], axis=0), NT_DIMS, preferred_element_type=F32)
        cbm, chm = c_both[:, :SSM_STATE], c_both[:, SSM_STATE:]

        gye16 = (gy * e).astype(BF16)
        g_c = jnp.dot(gye16, h16, preferred_element_type=F32)
        gh_off = lax.dot_general(gye16, cb, TN_DIMS, preferred_element_type=F32)
        bgs = lax.dot_general(bb, ghn16, NT_DIMS, preferred_element_type=F32)
        g_xd = w * bgs
        head_sums = _dot_split(jnp.concatenate([gy * chm, xd * bgs], axis=0), seg, 0)
        g_e, g_w = head_sums[:ch], head_sums[ch:]
        g_b = jnp.dot((xd * w).astype(BF16), ghn16, preferred_element_type=F32)
        decay = _decay_col(cs_t, hg)
        gh_scr[g] = decay * ghn + gh_off
        rsum = jnp.sum(ghn * h, axis=1, keepdims=True)
        lane_head = lax.broadcasted_iota(jnp.int32, (ch, gw), 1) // SSM_HEAD_DIM
        lane_head1 = lax.broadcasted_iota(jnp.int32, (1, gw), 1) // SSM_HEAD_DIM
        g_el = jnp.zeros((1, gw), F32)
        g_cs = g_e * e - g_w * w
        upper = q["row"] <= q["col"]
        lms, gys = [], []
        for j in range(hg):
            g_el = jnp.where(lane_head1 == j, jnp.sum(rsum[j * SSM_HEAD_DIM:(j + 1) * SSM_HEAD_DIM, :], axis=0, keepdims=True), g_el)
            csc = cs[:, j * SSM_HEAD_DIM:j * SSM_HEAD_DIM + 1]
            csr = cs_t[j:j + 1, :]
            lms.append(jnp.exp(jnp.where(q["tril"], csc - csr, NEG)))
            gys.append(jnp.where(lane_head == j, gy16, jnp.zeros_like(gy16)))
        lm_st, gy_st = jnp.concatenate(lms, axis=0), jnp.concatenate(gys, axis=0)
        cbm_st = jnp.concatenate([cbm] * hg, axis=0)
        gcb_st = lax.dot_general(gy_st, xd16, NT_DIMS, preferred_element_type=F32) * lm_st
        gcb_sum = gcb_st[0:ch]
        for j in range(1, hg):
            gcb_sum = gcb_sum + gcb_st[j * ch:(j + 1) * ch]
        gcb16 = gcb_sum.astype(BF16)
        g_c = g_c + jnp.dot(gcb16, bb, preferred_element_type=F32)
        g_b = g_b + lax.dot_general(gcb16, cb, TN_DIMS, preferred_element_type=F32)
        g_xd = g_xd + lax.dot_general((cbm_st * lm_st).astype(BF16), gy_st, TN_DIMS, preferred_element_type=F32)
        m_st = gcb_st * cbm_st
        for j in range(hg):
            m_ls = m_st[j * ch:(j + 1) * ch]
            d_cs = jnp.sum(m_ls, axis=1, keepdims=True) - jnp.sum(m_ls.T, axis=1, keepdims=True)
            g_cs = g_cs + jnp.where(lane_head == j, d_cs, 0.0)
        extra = _colsum(g_w * w) + g_el * jnp.exp(q["cs_last"])
        g_cs = g_cs + jnp.where(lax.broadcasted_iota(jnp.int32, (ch, gw), 0) == ch - 1, extra, 0.0)
        g_la = _dot_split(upper, g_cs, 1)
        a_x = -jnp.exp(alx_ref[...])
        gdt_ref[...] = g_xd * xs + g_la * a_x * (1.0 / SSM_HEAD_DIM)
        ga_row = _colsum(g_la * (dtx * a_x)) * (1.0 / SSM_HEAD_DIM)
        ga_ref[...] = jnp.where(lax.broadcasted_iota(jnp.int32, (8, gw), 0) == 0, ga_row, 0.0)
        gxs_ref[...] = g_xd * dtx + gy * dsk_ref[...]
        gb_ref[...] = g_b
        gc_ref[...] = g_c

    wide = pl.BlockSpec((ch, gp * gw), lambda cc, g: (rc(cc), g))
    narrow = pl.BlockSpec((ch, gp * SSM_STATE), lambda cc, g: (rc(cc), g))
    row = pl.BlockSpec((1, gp * gw), lambda cc, g: (0, g))
    return pl.pallas_call(
        body, grid=(nch, SSM_GROUPS // gp),
        in_specs=[wide,
                  pl.BlockSpec((ch, gp * SSM_STATE), lambda cc, g: (rc(cc), nb0 // gp + g)),
                  pl.BlockSpec((ch, gp * SSM_STATE), lambda cc, g: (rc(cc), (nb0 + SSM_GROUPS) // gp + g)),
                  wide, wide,
                  pl.BlockSpec((gp, 8, ch), lambda cc, g: (g, 0, rc(cc))),
                  row, wide,
                  pl.BlockSpec((None, gp, gw, SSM_STATE), lambda cc, g: (rc(cc), g, 0, 0)),
                  row],
        out_specs=[wide, narrow, narrow, wide, pl.BlockSpec((8, gp * gw), lambda cc, g: (rc(cc), g))],
        out_shape=[S((t, SSM_INNER), F32), S((t, SSM_GROUPS * SSM_STATE), F32), S((t, SSM_GROUPS * SSM_STATE), F32),
                   S((t, SSM_INNER), F32), S((nch * 8, SSM_INNER), F32)],
        scratch_shapes=[pltpu.VMEM((SSM_GROUPS, gw, SSM_STATE), F32)],
        name="ssd_bwd", compiler_params=_params(("arbitrary", "arbitrary")),
    )(xa, xa, xa, dtx, csx, cst_g, alog_x, g_y, states, dskip_x)


def _local_step(x, target, w_pre, w_in_r, b_gate, conv_w, conv_b, dt_bias, a_log, d_skip, ssm_norm_w,
                late_weights, w_post, w_fpre, w_fpost, on_mid_grads, on_in_proj_grads):
    t = x.shape[0]
    mm = functools.partial(_matmul, tm=512)
    slopes = _slope_rows()
    hg = SSM_HEADS // SSM_GROUPS
    dt_bias_pad = jnp.pad(dt_bias, ((0, 0), (0, LANE - SSM_HEADS)))
    alog_x = jnp.repeat(a_log, SSM_HEAD_DIM, axis=1)
    alog_pad = jnp.pad(a_log, ((0, 0), (0, LANE - SSM_HEADS)))
    dskip_x = jnp.repeat(d_skip, SSM_HEAD_DIM, axis=1)

    u = _pre_norm(x, w_pre)
    pa = _matmul(u, w_in_r, mode="nn", out_dtype=BF16, name="in_proj_zgx", tm=1024, tn=2048, tk=D_MODEL, b_cols=(0, PA_W))
    pb, pb16 = _matmul(u, w_in_r[:, PA_W:], mode="nn", out_dtype=F32, name="in_proj_qkvdt", tm=1024, tn=PB_W // 2, tk=D_MODEL,
                       epilogue="also_bf16")
    fwd = [_att_fwd((pb, pb16), dil, slopes) for _, dil in DILATED_PATTERNS]
    att, lse = _att_combine([o for o, _ in fwd], [l for _, l in fwd])
    xa, xc = _conv_fwd(pa, conv_w, conv_b)
    dtx, csx, cst = _dt_fwd(pb, dt_bias_pad, alog_pad)
    cst_g = jnp.pad(cst[:SSM_HEADS].reshape(SSM_GROUPS, hg, t), ((0, 0), (0, 8 - hg), (0, 0)))
    y_ssd, states = _ssd_fwd(xa, dtx, csx, cst_g)
    y4 = _gate_norm_fwd(y_ssd, xa, pa, dskip_x, ssm_norm_w)
    w_att, w_ssm, w_out, w_up, w_down = late_weights(y4)
    att_p = mm(att, w_att, mode="nn", out_dtype=BF16, name="att_proj", tn=D_MODEL, tk=ATT_WIDTH)
    ssm_p = mm(y4, w_ssm, mode="nn", out_dtype=BF16, name="ssm_proj", tn=D_MODEL, tk=SSM_INNER)
    mixin = _gating_fwd(pa, b_gate, att_p, ssm_p)
    mixed = mm(mixin, w_out, mode="nn", out_dtype=F32, name="out_proj", tn=D_MODEL, tk=D_MODEL)
    h1, f = _mix_post_ffn_pre(x, mixed, w_post, w_fpre)
    act, up = _matmul(f, w_up, mode="nn", out_dtype=BF16, name="ffn_up", tm=2048, tn=FFN_HIDDEN // N_DEV, tk=D_MODEL, epilogue="relu2", stacked=True)
    dn = mm(act, w_down, mode="nn", out_dtype=F32, name="ffn_down", tn=D_MODEL, tk=FFN_HIDDEN)
    loss, g_h2, g_dn, gw_fpost = _loss_and_ffn_post_bwd(h1, dn, w_fpost, target)

    g_up = _matmul(g_dn, w_down, mode="nt", out_dtype=BF16, name="ffn_down_bwd_x", tm=1024, tn=2048, tk=D_MODEL, epilogue="relu2_bwd",
                   extra=up)
    gw_down = _matmul(act, g_dn, mode="tn", out_dtype=BF16, name="ffn_down_bwd_w", tm=1024, tn=D_MODEL, tk=2048)
    w_up_rows = jnp.moveaxis(w_up, 0, 1).reshape(D_MODEL, FFN_HIDDEN)
    g_f = mm(g_up, w_up_rows, mode="nt", out_dtype=F32, name="ffn_up_bwd_x", tn=D_MODEL, tk=FFN_HIDDEN)
    gw_up = _matmul(f, g_up, mode="tn", out_dtype=BF16, name="ffn_up_bwd_w", tm=D_MODEL, tn=FFN_HIDDEN // N_DEV, tk=2048, stacked=True)
    g_h1, g_mixed, gw_fpre, gw_post = _ffn_pre_mix_post_bwd(g_h2, g_f, h1, w_fpre, mixed, w_post)
    g_mixin = mm(g_mixed, w_out, mode="nt", out_dtype=BF16, name="out_proj_bwd_x", tn=D_MODEL, tk=D_MODEL)
    gw_out = _matmul(mixin, g_mixed, mode="tn", out_dtype=BF16, name="out_proj_bwd_w", tm=D_MODEL, tn=D_MODEL, tk=2048)
    g_proj = lax.empty((t, PROJ_W), BF16)
    g_att_p, g_ssm_p, g_b_gate, g_proj = _gating_bwd(g_mixin, pa, b_gate, att_p, ssm_p, g_proj)
    g_att = mm(g_att_p, w_att, mode="nt", out_dtype=F32, name="att_proj_bwd_x", tn=ATT_WIDTH, tk=D_MODEL)
    gw_att = _matmul(att, g_att_p, mode="tn", out_dtype=BF16, name="att_proj_bwd_w", tm=ATT_WIDTH, tn=D_MODEL, tk=2048)
    g_y4 = mm(g_ssm_p, w_ssm, mode="nt", out_dtype=BF16, name="ssm_proj_bwd_x", tn=SSM_INNER, tk=D_MODEL)
    gw_ssm = _matmul(y4, g_ssm_p, mode="tn", out_dtype=BF16, name="ssm_proj_bwd_w", tm=1024, tn=D_MODEL, tk=2048)
    token = on_mid_grads(dict(w_att_proj=gw_att, w_ssm_proj=gw_ssm, w_out=gw_out, w_up=gw_up, w_down=gw_down))
    if token is not None:
        ssm_norm_w = ssm_norm_w + jnp.tile(token[0:1, :], (1, SSM_INNER // LANE))
    g_y2, g_norm_w, _, g_d_skip, g_proj = _gate_norm_bwd(g_y4, y_ssd, xa, pa, dskip_x, ssm_norm_w, g_proj)
    g_xs, g_bm, g_cm, g_dtx, ga_rows = _ssd_bwd(xa, dtx, csx, cst_g, alog_x, g_y2, states, dskip_x)
    g_dt_bias, g_a_log, g_proj = _dt_bwd(g_dtx, ga_rows, pb, dt_bias_pad, g_proj)
    g_conv_b, gcw0, gcw1, gcw2, gcw3, g_proj = _conv_bwd(g_xs, g_bm, g_cm, xc, pa, conv_w, g_proj)
    delta = _att_delta(g_att, att)
    dqs, dks, dvs = [], [], []
    for _, dil in DILATED_PATTERNS:
        dq, dk, dv = _att_bwd((pb, pb16), g_att, lse, delta, dil, slopes)
        dqs.append(dq)
        dks.append(dk)
        dvs.append(dv)
    g_proj = _att_grad_sum(dqs, dks, dvs, g_proj)
    gw_in_r = _matmul(u, g_proj, mode="tn", out_dtype=BF16, name="in_proj_bwd_w", tm=D_MODEL, tn=1792, tk=2048)
    token = on_in_proj_grads(gw_in_r, jnp.concatenate([gcw0, gcw1, gcw2, gcw3], axis=0))
    g_u = _matmul(g_proj, w_in_r, mode="nt", out_dtype=F32, name="in_proj_bwd_x", tm=1024, tn=D_MODEL, tk=3584, after=token)
    g_x, gw_pre = _pre_norm_bwd(g_h1, g_u, x, w_pre)

    grads = dict(
        norm_mix_pre_w=gw_pre, b_gate=g_b_gate, conv_b=g_conv_b, dt_bias=g_dt_bias[:, :SSM_HEADS], a_log=g_a_log[:, :SSM_HEADS],
        d_skip=g_d_skip[:, :SSM_HEADS], ssm_norm_w=g_norm_w, norm_mix_post_w=gw_post, norm_ffn_pre_w=gw_fpre, norm_ffn_post_w=gw_fpost)
    return loss, g_x, grads


def _mesh_pos():
    return lax.axis_index("x"), lax.axis_index("y"), lax.axis_index("c")


def _all_gather(shards):
    n = len(shards)

    def body(*refs):
        x_refs, o_refs = refs[:n], refs[n:2 * n]
        send_sems, recv_sems, local_sems = refs[2 * n:]
        x, y, c = _mesh_pos()
        me, sibling = (x, y, c), (x, y, 1 - c)
        chips = [(1 - x, y), (x, 1 - y), (1 - x, 1 - y)]

        def copy(a, k, block, to, src=None):
            dst = o_refs[a].at[4 * block[0] + 2 * block[1] + block[2]]
            return pltpu.make_async_remote_copy(
                src_ref=dst if src is None else src, dst_ref=dst, send_sem=send_sems.at[7 * a + k], recv_sem=recv_sems.at[7 * a + k],
                device_id=to, device_id_type=pl.DeviceIdType.MESH)

        mine = [pltpu.make_async_copy(x_refs[a], o_refs[a].at[4 * x + 2 * y + c], local_sems.at[a]) for a in range(n)]
        for cp in mine:
            cp.start()
        first = []
        for a in range(n):
            first.append(copy(a, 0, me, sibling, src=x_refs[a]))
            first += [copy(a, 1 + j, me, (*chip, c), src=x_refs[a]) for j, chip in enumerate(chips)]
        for cp in first:
            cp.start()
        passed = []
        for j, chip in enumerate(chips):
            for a in range(n):
                copy(a, 1 + j, (*chip, c), me).wait_recv()
                passed.append(copy(a, 4 + j, (*chip, c), sibling))
                passed[-1].start()
        for a in range(n):
            copy(a, 0, sibling, me).wait_recv()
            for j, chip in enumerate(chips):
                copy(a, 4 + j, (*chip, 1 - c), me).wait_recv()
        for cp in first + passed:
            cp.wait_send()
        for cp in mine:
            cp.wait()

    hbm = pl.BlockSpec(memory_space=pltpu.HBM)
    return pl.pallas_call(
        body, out_shape=[S((N_DEV,) + s.shape, s.dtype) for s in shards],
        in_specs=[hbm] * n, out_specs=[hbm] * n,
        scratch_shapes=[pltpu.SemaphoreType.DMA((7 * n,)), pltpu.SemaphoreType.DMA((7 * n,)), pltpu.SemaphoreType.DMA((n,))],
        name="weights_all_gather",
    )(*shards)


def _exchange_grads(slab_arrays, small):
    n = len(slab_arrays)
    r_small = small.shape[0]

    def body(*refs):
        slab_refs, small_ref = refs[:n], refs[n]
        recv_refs, gsm_ref = refs[n + 1:2 * n + 1], refs[2 * n + 1]
        send_sems, recv_sems, local_sems = refs[2 * n + 2:]
        x, y, c = _mesh_pos()
        me = 4 * x + 2 * y + c

        def peer(k):
            px = 1 - x if k & 4 else x
            py = 1 - y if k & 2 else y
            pc = 1 - c if k & 1 else c
            return (px, py, pc), 4 * px + 2 * py + pc

        def copy(a, k, sending):
            to, lin = peer(k)
            sem = 7 * a + k - 1
            if a == n:
                src, dst = small_ref, gsm_ref.at[me if sending else lin]
            else:
                src, dst = slab_refs[a].at[lin], recv_refs[a].at[me if sending else lin]
            return pltpu.make_async_remote_copy(src_ref=src, dst_ref=dst, send_sem=send_sems.at[sem], recv_sem=recv_sems.at[sem],
                                                device_id=to, device_id_type=pl.DeviceIdType.MESH)

        own = [pltpu.make_async_copy(slab_refs[a].at[me], recv_refs[a].at[me], local_sems.at[a]) for a in range(n)]
        own.append(pltpu.make_async_copy(small_ref, gsm_ref.at[me], local_sems.at[n]))
        for cp in own:
            cp.start()
        order = [n] + list(range(n))
        sends = [copy(a, k, True) for a in order for k in range(1, N_DEV)]
        for cp in sends:
            cp.start()
        for a in order:
            for k in range(1, N_DEV):
                copy(a, k, False).wait_recv()
        for cp in sends:
            cp.wait_send()
        for cp in own:
            cp.wait()

    hbm = pl.BlockSpec(memory_space=pltpu.HBM)
    n_sem = 7 * (n + 1)
    res = pl.pallas_call(
        body, out_shape=[S(a.shape, a.dtype) for a in slab_arrays] + [S((N_DEV, r_small, LANE), small.dtype)],
        in_specs=[hbm] * (n + 1), out_specs=[hbm] * (n + 1),
        scratch_shapes=[pltpu.SemaphoreType.DMA((n_sem,)), pltpu.SemaphoreType.DMA((n_sem,)), pltpu.SemaphoreType.DMA((n + 1,))],
        name="grad_exchange",
    )(*slab_arrays, small)
    return res[:n], res[n]


def _peer_of(k, x, y, c):
    px = 1 - x if k & 4 else x
    py = 1 - y if k & 2 else y
    pc = 1 - c if k & 1 else c
    return (px, py, pc), 4 * px + 2 * py + pc


def _split_copies(src_refs, land_refs, send_sems, recv_sems, per_peer):
    x, y, c = _mesh_pos()
    me = 4 * x + 2 * y + c
    sends, recvs = [], []
    for a, (src, land) in enumerate(zip(src_refs, land_refs)):
        for k in range(1, N_DEV):
            to, lin = _peer_of(k, x, y, c)
            sem = 7 * a + k - 1
            piece = src.at[lin] if per_peer else src
            for slot, out in ((me, sends), (lin, recvs)):
                out.append(pltpu.make_async_remote_copy(
                    src_ref=piece, dst_ref=land.at[slot], send_sem=send_sems.at[sem], recv_sem=recv_sems.at[sem],
                    device_id=to, device_id_type=pl.DeviceIdType.MESH))
    return sends, recvs


def _remote_start(srcs, per_peer, name):
    n = len(srcs)
    lands = [lax.empty((N_DEV,) + (s.shape[1:] if per_peer else s.shape), s.dtype) for s in srcs]

    def body(*refs):
        src_refs, land_refs = refs[:n], refs[n:2 * n]
        send_sems, recv_sems = refs[2 * n], refs[2 * n + 1]
        token = refs[-1]
        sends, _ = _split_copies(src_refs, land_refs, send_sems, recv_sems, per_peer)
        for cp in sends:
            cp.start()
        token[...] = jnp.zeros_like(token)

    hbm = pl.BlockSpec(memory_space=pltpu.HBM)
    sem = pl.BlockSpec(memory_space=pltpu.SEMAPHORE)
    res = pl.pallas_call(
        body, name=name,
        out_shape=(pltpu.SemaphoreType.DMA((7 * n,)), pltpu.SemaphoreType.DMA((7 * n,)),
                   *[pltpu.HBM(a.shape, a.dtype) for a in srcs + lands], S((8, LANE), F32)),
        in_specs=[hbm] * (2 * n), out_specs=(sem, sem, *[hbm] * (2 * n), pl.BlockSpec(memory_space=pltpu.VMEM)),
        input_output_aliases={i: 2 + i for i in range(2 * n)},
        compiler_params=pltpu.CompilerParams(has_side_effects=pltpu.SideEffectType.DATAFLOW_SIDE_EFFECTING),
    )(*[pltpu.with_memory_space_constraint(a, pltpu.HBM) for a in srcs + lands])
    return dict(sems=res[:2], srcs=list(res[2:2 + n]), lands=list(res[2 + n:2 + 2 * n]), per_peer=per_peer), res[-1]


def _remote_wait(handle, after, name):
    n = len(handle["srcs"])
    per_peer = handle["per_peer"]

    def body(*refs):
        src_refs, land_refs = refs[:n], refs[n:2 * n]
        send_sems, recv_sems = refs[2 * n], refs[2 * n + 1]
        sends, recvs = _split_copies(src_refs, land_refs, send_sems, recv_sems, per_peer)
        for cp in sends:
            cp.wait_send()
        for cp in recvs:
            cp.wait_recv()

    hbm = pl.BlockSpec(memory_space=pltpu.HBM)
    sem = pl.BlockSpec(memory_space=pltpu.SEMAPHORE)
    arrays = handle["srcs"] + handle["lands"]
    res = pl.pallas_call(
        body, name=name, out_shape=tuple(pltpu.HBM(a.shape, a.dtype) for a in arrays),
        in_specs=[hbm] * (2 * n) + [sem, sem, pl.BlockSpec(memory_space=pl.ANY)], out_specs=tuple([hbm] * (2 * n)),
        input_output_aliases={i: i for i in range(2 * n)},
        compiler_params=pltpu.CompilerParams(has_side_effects=pltpu.SideEffectType.DATAFLOW_SIDE_EFFECTING),
    )(*arrays, *handle["sems"], after)
    return list(res[n:])


def _with_own(lands, own, me):
    return [lax.dynamic_update_index_in_dim(land, o.astype(land.dtype), me, 0) for land, o in zip(lands, own)]


def _adamw(w, m, v, slabs, name, tr):
    r, cols = w.shape
    c1 = 1.0 - ADAM_B1 ** ADAM_STEP
    c2 = 1.0 - ADAM_B2 ** ADAM_STEP

    def body(w_ref, m_ref, v_ref, s_ref, g_ref, d_ref, nm_ref, nv_ref):
        g = s_ref[0].astype(F32)
        for d in range(1, N_DEV):
            g = g + s_ref[d].astype(F32)
        nm = ADAM_B1 * m_ref[...] + (1.0 - ADAM_B1) * g
        nv = ADAM_B2 * v_ref[...] + (1.0 - ADAM_B2) * (g * g)
        g_ref[...] = g
        nm_ref[...] = nm
        nv_ref[...] = nv
        d_ref[...] = -ADAM_LR * ((nm / c1) / (jnp.sqrt(nv / c2) + ADAM_EPS) + ADAM_WD * w_ref[...])

    assert r % tr == 0, name
    blk = pl.BlockSpec((tr, cols), lambda i: (i, 0))
    return pl.pallas_call(
        body, grid=(r // tr,), in_specs=[blk, blk, blk, pl.BlockSpec((N_DEV, tr, cols), lambda i: (0, i, 0))],
        out_specs=[blk] * 4, out_shape=[S((r, cols), F32)] * 4, name=name, compiler_params=_params(("parallel",)),
    )(w, m, v, slabs)


BIG = ("w_in", "w_att_proj", "w_up", "w_ssm_proj", "w_out", "w_down", "conv_w")
ADAMW_ROWS = dict(w_in=256, w_att_proj=768, w_up=512, w_ssm_proj=256, w_out=128, w_down=256, conv_w=4)
SMALL = ("norm_mix_pre_w", "b_gate", "conv_b", "dt_bias", "a_log", "d_skip", "ssm_norm_w", "norm_mix_post_w",
         "norm_ffn_pre_w", "norm_ffn_post_w")
ORDER = ("norm_mix_pre_w", "w_in", "b_gate", "conv_w", "conv_b", "dt_bias", "a_log", "d_skip", "ssm_norm_w", "w_att_proj",
         "w_ssm_proj", "w_out", "norm_mix_post_w", "norm_ffn_pre_w", "w_up", "w_down", "norm_ffn_post_w")
ROW_SHARDED = ("w_ssm_proj", "w_out", "w_down")
LATE = ("w_att_proj", "w_ssm_proj", "w_out", "w_up", "w_down")
IN_PROJ_W = 10528
IN_SHARD_W = IN_PROJ_W // N_DEV
IN_SEGMENTS = ((2304, 4352), (8480, 10528), (4352, 8448), (0, 2304), (8448, 8480))


def _pack(parts, rows_multiple):
    flat = jnp.concatenate([p.reshape(-1) for p in parts])
    pad = (-flat.shape[0]) % (rows_multiple * LANE)
    return jnp.pad(flat, (0, pad)).reshape(-1, LANE)


def _unpack(flat2d, shapes):
    flat, out, off = flat2d.reshape(-1), [], 0
    for sh in shapes:
        n = int(np.prod(sh))
        out.append(flat[off:off + n].reshape(sh))
        off += n
    return out


def _reorder_in_proj(w):
    qkv, z, xbc = w[:, :2304], w[:, 2304:4352], w[:, 4352:8448]
    dt, gate = w[:, 8448:8480], w[:, 8480:10528]
    return jnp.concatenate([z, gate, xbc, qkv, dt, jnp.zeros((w.shape[0], PROJ_W - 10528), w.dtype)], axis=1)


def _restore_in_proj(wr):
    return jnp.concatenate([wr[:, OFF_QKV:OFF_QKV + 2304], wr[:, OFF_Z:OFF_Z + 2048], wr[:, OFF_XBC:OFF_XBC + 4096],
                            wr[:, OFF_DT:OFF_DT + 32], wr[:, OFF_GL:OFF_GL + 2048]], axis=1)


def _assemble_in_proj(g):
    pieces = []
    for lo, hi in IN_SEGMENTS:
        while lo < hi:
            d = lo // IN_SHARD_W
            end = min(hi, (d + 1) * IN_SHARD_W)
            pieces.append(g[d][:, lo - d * IN_SHARD_W:end - d * IN_SHARD_W])
            lo = end
    pieces.append(jnp.zeros((g.shape[1], PROJ_W - IN_PROJ_W), g.dtype))
    return jnp.concatenate(pieces, axis=1)


def _in_proj_slabs(wr):
    orig = _restore_in_proj(wr)
    return jnp.stack([orig[:, d * IN_SHARD_W:(d + 1) * IN_SHARD_W] for d in range(N_DEV)])


def kernel(x, norm_mix_pre_w, w_in, b_gate, conv_w, conv_b, dt_bias, a_log, d_skip, ssm_norm_w, w_att_proj, w_ssm_proj, w_out, norm_mix_post_w, norm_ffn_pre_w, w_up, w_down, norm_ffn_post_w, loss_target, m_norm_mix_pre_w, m_w_in, m_b_gate, m_conv_w, m_conv_b, m_dt_bias, m_a_log, m_d_skip, m_ssm_norm_w, m_w_att_proj, m_w_ssm_proj, m_w_out, m_norm_mix_post_w, m_norm_ffn_pre_w, m_w_up, m_w_down, m_norm_ffn_post_w, v_norm_mix_pre_w, v_w_in, v_b_gate, v_conv_w, v_conv_b, v_dt_bias, v_a_log, v_d_skip, v_ssm_norm_w, v_w_att_proj, v_w_ssm_proj, v_w_out, v_norm_mix_post_w, v_norm_ffn_pre_w, v_w_up, v_w_down, v_norm_ffn_post_w):
    w = dict(norm_mix_pre_w=norm_mix_pre_w, w_in=w_in, b_gate=b_gate, conv_w=conv_w, conv_b=conv_b, dt_bias=dt_bias, a_log=a_log,
             d_skip=d_skip, ssm_norm_w=ssm_norm_w, w_att_proj=w_att_proj, w_ssm_proj=w_ssm_proj, w_out=w_out,
             norm_mix_post_w=norm_mix_post_w, norm_ffn_pre_w=norm_ffn_pre_w, w_up=w_up, w_down=w_down, norm_ffn_post_w=norm_ffn_post_w)
    m = dict(norm_mix_pre_w=m_norm_mix_pre_w, w_in=m_w_in, b_gate=m_b_gate, conv_w=m_conv_w, conv_b=m_conv_b, dt_bias=m_dt_bias,
             a_log=m_a_log, d_skip=m_d_skip, ssm_norm_w=m_ssm_norm_w, w_att_proj=m_w_att_proj, w_ssm_proj=m_w_ssm_proj, w_out=m_w_out,
             norm_mix_post_w=m_norm_mix_post_w, norm_ffn_pre_w=m_norm_ffn_pre_w, w_up=m_w_up, w_down=m_w_down, norm_ffn_post_w=m_norm_ffn_post_w)
    v = dict(norm_mix_pre_w=v_norm_mix_pre_w, w_in=v_w_in, b_gate=v_b_gate, conv_w=v_conv_w, conv_b=v_conv_b, dt_bias=v_dt_bias,
             a_log=v_a_log, d_skip=v_d_skip, ssm_norm_w=v_ssm_norm_w, w_att_proj=v_w_att_proj, w_ssm_proj=v_w_ssm_proj, w_out=v_w_out,
             norm_mix_post_w=v_norm_mix_post_w, norm_ffn_pre_w=v_norm_ffn_pre_w, w_up=v_w_up, w_down=v_w_down, norm_ffn_post_w=v_norm_ffn_post_w)
    shard_shapes = {n: w[n].shape[1:] for n in ORDER}

    mx, my, mc = _mesh_pos()
    me = 4 * mx + 2 * my + mc

    g_in, g_conv = _all_gather([w["w_in"][0].astype(BF16), w["conv_w"][0]])
    conv_full = jnp.moveaxis(g_conv, 0, 1).reshape(SSM_CONV, CONV_DIM)
    late_shards = [w[n][0].astype(BF16) for n in LATE]
    late_handle, token = _remote_start(late_shards, False, "late_weights_start")
    w_pre = w["norm_mix_pre_w"] + jnp.tile(token[0:1, :], (1, D_MODEL // LANE))

    def late_weights(after):
        full = dict(zip(LATE, _with_own(_remote_wait(late_handle, after, "late_weights_wait"), late_shards, me)))
        for n in ROW_SHARDED:
            full[n] = full[n].reshape(-1, full[n].shape[2])
        w_att = jnp.moveaxis(full["w_att_proj"], 0, 1).reshape(ATT_WIDTH, D_MODEL)
        return w_att, full["w_ssm_proj"], full["w_out"], full["w_up"], full["w_down"]

    started = {}

    def start_exchange(tag, slabs):
        own = [lax.dynamic_index_in_dim(s, me, 0, keepdims=False) for s in slabs]
        handle, tok = _remote_start(slabs, True, tag + "_grads_start")
        started[tag] = (handle, own)
        return tok

    def on_mid_grads(g):
        slabs = dict(w_up=g["w_up"], w_att_proj=jnp.moveaxis(g["w_att_proj"].reshape(ATT_WIDTH, N_DEV, -1), 1, 0))
        for n in ROW_SHARDED:
            slabs[n] = g[n].reshape(N_DEV, -1, g[n].shape[1])
        return start_exchange("mid", [slabs[n] for n in LATE])

    def on_in_proj_grads(gw_in_r, g_conv_w):
        return start_exchange("in_proj", [_in_proj_slabs(gw_in_r), jnp.moveaxis(g_conv_w.reshape(SSM_CONV, N_DEV, -1), 1, 0)])

    loss, g_x, grads = _local_step(
        x[0], loss_target[0], w_pre, _assemble_in_proj(g_in), w["b_gate"], conv_full, w["conv_b"], w["dt_bias"], w["a_log"],
        w["d_skip"], w["ssm_norm_w"], late_weights, w["norm_mix_post_w"], w["norm_ffn_pre_w"], w["norm_ffn_post_w"],
        on_mid_grads, on_in_proj_grads)

    recv = {}
    for tag, names in (("mid", LATE), ("in_proj", ("w_in", "conv_w"))):
        handle, own = started[tag]
        recv.update(zip(names, _with_own(_remote_wait(handle, g_x, tag + "_grads_wait"), own, me)))
    small = _pack([grads[n].astype(F32) for n in SMALL], 8)
    _, small_all = _exchange_grads([], small)

    small_shapes = [shard_shapes[n] for n in SMALL]
    small_out = _adamw(*[_pack([d_[n][0] for n in SMALL], 8) for d_ in (w, m, v)], small_all, "adamw_replicated", small_all.shape[1])
    big_out = {n: _adamw(w[n][0], m[n][0], v[n][0], recv[n], "adamw_" + n, ADAMW_ROWS[n]) for n in BIG}
    res = []
    for which, small_flat in enumerate(small_out):
        vals = {n: big_out[n][which] for n in BIG}
        vals.update(zip(SMALL, _unpack(small_flat, small_shapes)))
        res.append([vals[n][None] for n in ORDER])
    g_out, d_out, m_out, v_out = res
    total = lax.psum(loss[0, 0], ("x", "y", "c"))
    return (total, g_x[None], *g_out, *d_out, *m_out, *v_out)
```

```python
import functools
import math

import jax
import jax.numpy as jnp
import numpy as np
from jax import lax
from jax.experimental import pallas as pl
from jax.experimental.pallas import tpu as pltpu

F32 = jnp.float32
BF16 = jnp.bfloat16

D_MODEL = 1024
HEAD_DIM = 64
N_ATT_HEADS = 12
ATT_WIDTH = N_ATT_HEADS * HEAD_DIM
DILATED_PATTERNS = ((128, 1), (512, 4), (2048, 16))
ATT_BLOCK = 128
SSM_INNER = 2048
SSM_HEAD_DIM = 64
SSM_HEADS = 32
SSM_GROUPS = 8
SSM_STATE = 128
SSM_CHUNK = 128
CONV_DIM = 4096
SSM_CONV = 4
FFN_HIDDEN = 4096
RMS_EPS = 1e-6
N_DEV = 8

ADAM_LR = 0.001
ADAM_B1 = 0.9
ADAM_B2 = 0.999
ADAM_EPS = 1e-08
ADAM_WD = 0.01
ADAM_STEP = 10

LANE = 128
OFF_Z, OFF_GL, OFF_XBC, OFF_QKV, OFF_DT = 0, 2048, 4096, 8192, 10496
PROJ_W = 10752
PROJ_BLOCKS = PROJ_W // LANE
PA_W = OFF_QKV
PB_W = PROJ_W - OFF_QKV
PB_DT = OFF_DT - OFF_QKV
VMEM_LIMIT = 52 * 1024 * 1024
NEG = -1e30

HI = lax.Precision.HIGHEST
NT_DIMS = (((1,), (1,)), ((), ()))
TN_DIMS = (((0,), (0,)), ((), ()))
S = jax.ShapeDtypeStruct


def _params(sem):
    return pltpu.CompilerParams(dimension_semantics=sem, vmem_limit_bytes=VMEM_LIMIT)


def _matmul(a, b, *, mode, out_dtype, name, tm, tn, tk, epilogue=None, extra=None, stacked=False, after=None, b_cols=None):
    if mode == "nn":
        m, k = a.shape
        n = b.shape[0] * b.shape[2] if stacked else b.shape[1]
        col0 = 0
        if b_cols is not None:
            assert b_cols[0] % tn == 0, name
            col0, n = b_cols[0] // tn, b_cols[1]
        a_spec = pl.BlockSpec((tm, tk), lambda i, j, kk: (i, kk))
        b_spec = pl.BlockSpec((None, tk, tn), lambda i, j, kk: (j, kk, 0)) if stacked else pl.BlockSpec((tk, tn), lambda i, j, kk: (kk, col0 + j))
        dims = (((1,), (0,)), ((), ()))
    elif mode == "nt":
        m, k = a.shape
        n = b.shape[1] if stacked else b.shape[0]
        a_spec = pl.BlockSpec((tm, tk), lambda i, j, kk: (i, kk))
        b_spec = pl.BlockSpec((None, tn, tk), lambda i, j, kk: (kk, j, 0)) if stacked else pl.BlockSpec((tn, tk), lambda i, j, kk: (j, kk))
        dims = NT_DIMS
    else:
        (k, m), n = a.shape, b.shape[1]
        a_spec = pl.BlockSpec((tk, tm), lambda i, j, kk: (kk, i))
        b_spec = pl.BlockSpec((tk, tn), lambda i, j, kk: (kk, j))
        dims = TN_DIMS
    assert m % tm == 0 and n % tn == 0 and k % tk == 0, (name, m, n, k)
    if stacked:
        assert (tk if mode == "nt" else tn) * N_DEV == (k if mode == "nt" else n), name
    nk = k // tk
    o_spec = pl.BlockSpec((tm, tn), lambda i, j, kk: (i, j))
    in_specs, args = [a_spec, b_spec], [a, b]
    if epilogue == "relu2":
        out_shape = (S((m, n), BF16), S((m, n), BF16))
        out_specs = (o_spec, o_spec)
    elif epilogue == "also_bf16":
        out_shape = (S((m, n), out_dtype), S((m, n), BF16))
        out_specs = (o_spec, o_spec)
    elif stacked and mode == "tn":
        out_shape, out_specs = S((N_DEV, m, tn), out_dtype), pl.BlockSpec((None, tm, tn), lambda i, j, kk: (j, i, 0))
    else:
        out_shape, out_specs = S((m, n), out_dtype), o_spec
    if epilogue == "relu2_bwd":
        in_specs.append(o_spec)
        args.append(extra)
    n_in = len(args)
    if after is not None:
        in_specs.append(pl.BlockSpec(after.shape, lambda i, j, kk: (0,) * after.ndim))
        args.append(after)

    def finish(acc, refs):
        if epilogue == "relu2":
            r = jnp.maximum(acc, 0.0)
            refs[0][...] = (r * r).astype(BF16)
            refs[1][...] = acc.astype(BF16)
        elif epilogue == "also_bf16":
            refs[0][...] = acc.astype(out_dtype)
            refs[1][...] = acc.astype(BF16)
        elif epilogue == "relu2_bwd":
            up = refs[0][...].astype(F32)
            refs[1][...] = (acc * (2.0 * jnp.maximum(up, 0.0))).astype(out_dtype)
        else:
            refs[0][...] = acc.astype(out_dtype)

    def body(a_ref, b_ref, *rest):
        rest = rest[:n_in - 2] + rest[len(args) - 2:]
        part = lax.dot_general(a_ref[...].astype(BF16), b_ref[...].astype(BF16), dims, preferred_element_type=F32)
        if nk == 1:
            finish(part, rest)
            return
        acc_ref = rest[-1]
        kk = pl.program_id(2)

        @pl.when(kk == 0)
        def _():
            acc_ref[...] = part

        @pl.when(kk > 0)
        def _():
            acc_ref[...] += part

        @pl.when(kk == nk - 1)
        def _():
            finish(acc_ref[...], rest[:-1])

    scratch = [] if nk == 1 else [pltpu.VMEM((tm, tn), F32)]
    return pl.pallas_call(
        body, grid=(m // tm, n // tn, nk), in_specs=in_specs, out_specs=out_specs, out_shape=out_shape,
        scratch_shapes=scratch, name=name, compiler_params=_params(("parallel", "parallel", "arbitrary")),
    )(*args)


def _rowcall(body, name, n_rows, tr, ins, outs, scratch=(), into=None):
    in_specs = [pl.BlockSpec(bs, im) for _, bs, im in ins]
    out_specs = [pl.BlockSpec(bs, im) for _, _, bs, im in outs]
    out_shape = [S(sh, dt) for sh, dt, _, _ in outs]
    args = [a for a, _, _ in ins]
    aliases = {}
    kernel = body
    if into is not None:
        buf, bs, im = into
        n_in = len(args)
        in_specs.append(pl.BlockSpec(memory_space=pl.ANY))
        args.append(buf)
        out_specs.append(pl.BlockSpec(bs, im))
        out_shape.append(S(buf.shape, buf.dtype))
        aliases = {n_in: len(out_shape) - 1}

        def kernel(*refs):
            body(*refs[:n_in], *refs[n_in + 1:])

    return pl.pallas_call(
        kernel, grid=(n_rows // tr,), in_specs=in_specs, out_specs=out_specs, out_shape=out_shape,
        input_output_aliases=aliases, scratch_shapes=list(scratch), name=name, compiler_params=_params(("arbitrary",)),
    )(*args)


def _rows(arr, tr, width=None, cb=0):
    width = arr.shape[1] if width is None else width
    return (arr, (tr, width), lambda i, cb=cb: (i, cb))


def _whole(arr):
    nd = arr.ndim
    return (arr, arr.shape, lambda i, nd=nd: (0,) * nd)


def _orow(n_rows, width, dtype, tr):
    return ((n_rows, width), dtype, (tr, width), lambda i: (i, 0))


def _oacc(width):
    return ((1, width), F32, (1, width), lambda i: (0, 0))


def _accumulate(ref, value):
    first = pl.program_id(0) == 0

    @pl.when(first)
    def _():
        ref[...] = value

    @pl.when(jnp.logical_not(first))
    def _():
        ref[...] += value


def _colsum(v):
    return jnp.sum(v, axis=0, keepdims=True)


def _rms_fwd(x, w):
    r = lax.rsqrt(jnp.mean(x * x, axis=-1, keepdims=True) + RMS_EPS)
    return x * r * w


def _rms_bwd(gy, x, w):
    r = lax.rsqrt(jnp.mean(x * x, axis=-1, keepdims=True) + RMS_EPS)
    xn = x * r
    gxn = gy * w
    gx = r * (gxn - xn * jnp.mean(gxn * xn, axis=-1, keepdims=True))
    return gx, _colsum(gy * xn)


def _sigmoid(x):
    return 1.0 / (1.0 + jnp.exp(-x))


def _head_expand(n_heads_pad, n_heads, width):
    h = lax.broadcasted_iota(jnp.int32, (n_heads_pad, n_heads * width), 0)
    c = lax.broadcasted_iota(jnp.int32, (n_heads_pad, n_heads * width), 1)
    return (c // width == h).astype(F32)


def _head_reduce(n_heads, width, n_heads_pad):
    c = lax.broadcasted_iota(jnp.int32, (n_heads * width, n_heads_pad), 0)
    h = lax.broadcasted_iota(jnp.int32, (n_heads * width, n_heads_pad), 1)
    return (c // width == h).astype(F32)


def _block_ones(n, width):
    r = lax.broadcasted_iota(jnp.int32, (n, n), 0)
    c = lax.broadcasted_iota(jnp.int32, (n, n), 1)
    return (r // width == c // width).astype(F32)


def _pre_norm(x, w_pre, tr=512):
    t = x.shape[0]

    def body(x_ref, w_ref, u_ref):
        u_ref[...] = _rms_fwd(x_ref[...], w_ref[...]).astype(BF16)

    return _rowcall(body, "pre_norm", t, tr, [_rows(x, tr), _whole(w_pre)], [_orow(t, D_MODEL, BF16, tr)])[0]


CONV_HALO = 16


def _row_shift(cur, halo, j):
    tr = cur.shape[0]
    r = lax.broadcasted_iota(jnp.int32, (tr, tr), 0)
    c = lax.broadcasted_iota(jnp.int32, (tr, tr), 1)
    main = jnp.dot((c == r + j).astype(BF16), cur, preferred_element_type=F32)
    er = lax.broadcasted_iota(jnp.int32, (CONV_HALO, CONV_HALO), 0)
    ec = lax.broadcasted_iota(jnp.int32, (CONV_HALO, CONV_HALO), 1)
    if j < 0:
        edge = jnp.dot((ec == CONV_HALO + er + j).astype(BF16), halo, preferred_element_type=F32)
        return jnp.concatenate([main[:CONV_HALO] + edge, main[CONV_HALO:]], axis=0)
    edge = jnp.dot((ec == er + j - CONV_HALO).astype(BF16), halo, preferred_element_type=F32)
    return jnp.concatenate([main[:tr - CONV_HALO], main[tr - CONV_HALO:] + edge], axis=0)


def _conv_fwd(proj, conv_w, conv_b, tr=256):
    t = proj.shape[0]
    cb = OFF_XBC // CONV_DIM
    halo = (proj, (CONV_HALO, CONV_DIM), lambda i: (jnp.maximum(i * (tr // CONV_HALO) - 1, 0), cb))

    def body(cur_ref, prev_ref, w_ref, b_ref, o_ref, xc_ref):
        cur = cur_ref[...]
        prev = jnp.where(pl.program_id(0) > 0, prev_ref[...], jnp.zeros_like(prev_ref[...]))
        acc = b_ref[...] + w_ref[3:4, :] * cur.astype(F32)
        for k in range(SSM_CONV - 1):
            acc = acc + w_ref[k:k + 1, :] * _row_shift(cur, prev, -(SSM_CONV - 1 - k))
        o_ref[...] = acc * _sigmoid(acc)
        xc_ref[...] = acc.astype(BF16)

    return _rowcall(body, "conv_fwd", t, tr, [_rows(proj, tr, CONV_DIM, cb), halo, _whole(conv_w), _whole(conv_b)],
                    [_orow(t, CONV_DIM, F32, tr), _orow(t, CONV_DIM, BF16, tr)])


def _dt_fwd(proj, dt_bias_pad, alog_pad, tr=512):
    t = proj.shape[0]

    def body(raw_ref, b_ref, al_ref, dtx_ref, csx_ref, cst_ref):
        v = raw_ref[...] + b_ref[...]
        dt = jnp.maximum(v, 0.0) + jnp.log1p(jnp.exp(-jnp.abs(v)))
        expand = _head_expand(LANE, SSM_HEADS, SSM_HEAD_DIM)
        dtx_ref[...] = _dot_split(dt, expand, 0, 3)
        la = dt * (-jnp.exp(al_ref[...]))
        row = lax.broadcasted_iota(jnp.int32, (SSM_CHUNK, SSM_CHUNK), 0)
        col = lax.broadcasted_iota(jnp.int32, (SSM_CHUNK, SSM_CHUNK), 1)
        tril = (col <= row).astype(F32)
        cs = jnp.concatenate([_dot_split(tril, la[k * SSM_CHUNK:(k + 1) * SSM_CHUNK, :], 1, 3) for k in range(tr // SSM_CHUNK)], axis=0)
        csx_ref[...] = _dot_split(cs, expand, 0, 3)
        cst_ref[...] = cs.T

    return _rowcall(body, "dt_fwd", t, tr, [_rows(proj, tr, LANE, PB_DT // LANE), _whole(dt_bias_pad), _whole(alog_pad)],
                    [_orow(t, SSM_INNER, F32, tr), _orow(t, SSM_INNER, F32, tr), ((LANE, t), F32, (LANE, tr), lambda i: (0, i))])


def _gate_norm_fwd(y_ssd, xa, proj, dskip_x, norm_w, tr=256):
    t = y_ssd.shape[0]
    gw = SSM_INNER // SSM_GROUPS

    def body(y_ref, xs_ref, z_ref, d_ref, w_ref, o_ref):
        z = z_ref[...].astype(F32)
        y3 = (y_ref[...].astype(F32) + d_ref[...] * xs_ref[...]) * (z * _sigmoid(z))
        for g in range(SSM_GROUPS):
            sl = slice(g * gw, (g + 1) * gw)
            o_ref[:, sl] = _rms_fwd(y3[:, sl], w_ref[:, sl]).astype(BF16)

    return _rowcall(body, "gate_norm_fwd", t, tr,
                    [_rows(y_ssd, tr), _rows(xa, tr, SSM_INNER, 0), _rows(proj, tr, SSM_INNER, OFF_Z // SSM_INNER), _whole(dskip_x), _whole(norm_w)],
                    [_orow(t, SSM_INNER, BF16, tr)])[0]


def _gating_fwd(proj, b_gate, att_p, ssm_p, tr=512):
    t = proj.shape[0]

    def body(gl_ref, b_ref, a_ref, s_ref, o_ref):
        gates = _sigmoid(gl_ref[...].astype(F32) + b_ref[...])
        o_ref[...] = (gates[:, :D_MODEL] * a_ref[...].astype(F32) + gates[:, D_MODEL:] * s_ref[...].astype(F32)).astype(BF16)

    return _rowcall(body, "gating_fwd", t, tr, [_rows(proj, tr, 2 * D_MODEL, OFF_GL // (2 * D_MODEL)), _whole(b_gate), _rows(att_p, tr), _rows(ssm_p, tr)],
                    [_orow(t, D_MODEL, BF16, tr)])[0]


def _mix_post_ffn_pre(x, mixed, w_post, w_fpre, tr=512):
    t = x.shape[0]

    def body(x_ref, m_ref, wp_ref, wf_ref, h1_ref, f_ref):
        h1 = x_ref[...] + _rms_fwd(m_ref[...], wp_ref[...])
        h1_ref[...] = h1
        f_ref[...] = _rms_fwd(h1, wf_ref[...]).astype(BF16)

    return _rowcall(body, "mix_post_ffn_pre", t, tr, [_rows(x, tr), _rows(mixed, tr), _whole(w_post), _whole(w_fpre)],
                    [_orow(t, D_MODEL, F32, tr), _orow(t, D_MODEL, BF16, tr)])


def _loss_and_ffn_post_bwd(h1, dn, w_fpost, target, tr=512):
    t = h1.shape[0]

    def body(h1_ref, dn_ref, w_ref, tg_ref, loss_ref, gh2_ref, gdn_ref, gw_ref):
        dn = dn_ref[...]
        w = w_ref[...]
        err = h1_ref[...] + _rms_fwd(dn, w) - tg_ref[...]
        _accumulate(loss_ref, jnp.zeros((1, LANE), F32) + 0.5 * jnp.sum(jnp.mean(err * err, axis=-1, keepdims=True)))
        gh2 = err * (1.0 / D_MODEL)
        gh2_ref[...] = gh2
        gdn, gw = _rms_bwd(gh2, dn, w)
        gdn_ref[...] = gdn.astype(BF16)
        _accumulate(gw_ref, gw)

    return _rowcall(body, "loss_ffn_post_bwd", t, tr, [_rows(h1, tr), _rows(dn, tr), _whole(w_fpost), _rows(target, tr)],
                    [_oacc(LANE), _orow(t, D_MODEL, F32, tr), _orow(t, D_MODEL, BF16, tr), _oacc(D_MODEL)])


def _ffn_pre_mix_post_bwd(g_h2, g_f, h1, w_fpre, mixed, w_post, tr=512):
    t = h1.shape[0]

    def body(gh2_ref, gf_ref, h1_ref, wf_ref, m_ref, wp_ref, gh1_ref, gm_ref, gwf_ref, gwp_ref):
        gx, gwf = _rms_bwd(gf_ref[...], h1_ref[...], wf_ref[...])
        gh1 = gh2_ref[...] + gx
        gh1_ref[...] = gh1
        gm, gwp = _rms_bwd(gh1, m_ref[...], wp_ref[...])
        gm_ref[...] = gm.astype(BF16)
        _accumulate(gwf_ref, gwf)
        _accumulate(gwp_ref, gwp)

    return _rowcall(body, "ffn_pre_mix_post_bwd", t, tr,
                    [_rows(g_h2, tr), _rows(g_f, tr), _rows(h1, tr), _whole(w_fpre), _rows(mixed, tr), _whole(w_post)],
                    [_orow(t, D_MODEL, F32, tr), _orow(t, D_MODEL, BF16, tr), _oacc(D_MODEL), _oacc(D_MODEL)])


def _gating_bwd(g_mixin, proj, b_gate, att_p, ssm_p, g_proj, tr=512):
    t = proj.shape[0]

    def body(gm_ref, gl_ref, b_ref, a_ref, s_ref, ga_ref, gs_ref, gb_ref, ggl_ref):
        gates = _sigmoid(gl_ref[...].astype(F32) + b_ref[...])
        gm = gm_ref[...].astype(F32)
        g_att, g_ssm = gates[:, :D_MODEL], gates[:, D_MODEL:]
        ga_ref[...] = (gm * g_att).astype(BF16)
        gs_ref[...] = (gm * g_ssm).astype(BF16)
        ggl_a = gm * a_ref[...].astype(F32) * g_att * (1.0 - g_att)
        ggl_s = gm * s_ref[...].astype(F32) * g_ssm * (1.0 - g_ssm)
        ggl_ref[:, :D_MODEL] = ggl_a.astype(BF16)
        ggl_ref[:, D_MODEL:] = ggl_s.astype(BF16)
        _accumulate(gb_ref.at[:, :D_MODEL], _colsum(ggl_a))
        _accumulate(gb_ref.at[:, D_MODEL:], _colsum(ggl_s))

    return _rowcall(body, "gating_bwd", t, tr,
                    [_rows(g_mixin, tr), _rows(proj, tr, 2 * D_MODEL, OFF_GL // (2 * D_MODEL)), _whole(b_gate), _rows(att_p, tr), _rows(ssm_p, tr)],
                    [_orow(t, D_MODEL, BF16, tr), _orow(t, D_MODEL, BF16, tr), _oacc(2 * D_MODEL)],
                    into=(g_proj, (tr, 2 * D_MODEL), lambda i: (i, OFF_GL // (2 * D_MODEL))))


def _gate_norm_bwd(g_y4, y_ssd, xa, proj, dskip_x, norm_w, g_proj, tr=256):
    t = y_ssd.shape[0]
    gw = SSM_INNER // SSM_GROUPS

    def body(g_ref, y_ref, xs_ref, z_ref, d_ref, w_ref, gy2_ref, gnw_ref, gdx_ref, gd_ref, gz_ref):
        z = z_ref[...].astype(F32)
        xs = xs_ref[...]
        sg = _sigmoid(z)
        sz = z * sg
        y2 = y_ref[...].astype(F32) + d_ref[...] * xs
        y3 = y2 * sz
        g4 = g_ref[...].astype(F32)
        for g in range(SSM_GROUPS):
            sl = slice(g * gw, (g + 1) * gw)
            gy3, gnw = _rms_bwd(g4[:, sl], y3[:, sl], w_ref[:, sl])
            _accumulate(gnw_ref.at[:, sl], gnw)
            gy2 = gy3 * sz[:, sl]
            gy2_ref[:, sl] = gy2
            gz_ref[:, sl] = (gy3 * y2[:, sl] * (sg[:, sl] * (1.0 + z[:, sl] * (1.0 - sg[:, sl])))).astype(BF16)
            _accumulate(gdx_ref.at[:, sl], _colsum(gy2 * xs[:, sl]))
        tot = jnp.broadcast_to(gdx_ref[...], (8, SSM_INNER))
        gd_ref[...] = jnp.dot(tot, _head_reduce(SSM_HEADS, SSM_HEAD_DIM, LANE), precision=HI, preferred_element_type=F32)[0:1, :]

    return _rowcall(body, "gate_norm_bwd", t, tr,
                    [_rows(g_y4, tr), _rows(y_ssd, tr), _rows(xa, tr, SSM_INNER, 0), _rows(proj, tr, SSM_INNER, OFF_Z // SSM_INNER), _whole(dskip_x), _whole(norm_w)],
                    [_orow(t, SSM_INNER, F32, tr), _oacc(SSM_INNER), _oacc(SSM_INNER), _oacc(LANE)],
                    into=(g_proj, (tr, SSM_INNER), lambda i: (i, OFF_Z // SSM_INNER)))


def _dt_bwd(g_dtx, ga_rows, proj, dt_bias_pad, g_proj, tr=512):
    t = proj.shape[0]
    tail = PROJ_W - OFF_DT

    def body(g_ref, ga_ref, raw_ref, b_ref, gb_ref, gal_ref, o_ref):
        red = _head_reduce(SSM_HEADS, SSM_HEAD_DIM, LANE)
        gdt = _dot_split(g_ref[...], red, 0, 3)
        graw = gdt * _sigmoid(raw_ref[...] + b_ref[...])
        o_ref[...] = jnp.concatenate([graw.astype(BF16), jnp.zeros((tr, tail - LANE), BF16)], axis=1)
        _accumulate(gb_ref, _colsum(graw))
        tot = jnp.broadcast_to(_colsum(ga_ref[...]), (8, SSM_INNER))
        gal_ref[...] = jnp.dot(tot, red, precision=HI, preferred_element_type=F32)[0:1, :]

    return _rowcall(body, "dt_bwd", t, tr, [_rows(g_dtx, tr), _whole(ga_rows), _rows(proj, tr, LANE, PB_DT // LANE), _whole(dt_bias_pad)],
                    [_oacc(LANE), _oacc(LANE)], into=(g_proj, (tr, tail), lambda i: (i, OFF_DT // tail)))


def _conv_bwd(g_xs, g_b, g_c, xc, proj, conv_w, g_proj, tr=256):
    t = proj.shape[0]
    n_blk = t // tr
    cb = OFF_XBC // CONV_DIM
    nb, nc = SSM_INNER, SSM_INNER + SSM_GROUPS * SSM_STATE
    def nxt(arr, width):
        return (arr, (CONV_HALO, width), lambda i: (jnp.minimum((i + 1) * (tr // CONV_HALO), t // CONV_HALO - 1), 0))

    def body(gxs_ref, gxs_n, gb_ref, gb_n, gc_ref, gc_n, xc_ref, xc_n, x_ref, w_ref, gcb_ref, gw0, gw1, gw2, gw3, o_ref):
        def gxc_of(gxs, gb, gc, xc, keep):
            xcf = xc[...].astype(F32)
            sg = _sigmoid(xcf)
            dsilu = jnp.where(keep, sg * (1.0 + xcf * (1.0 - sg)), 0.0)
            return jnp.concatenate([gxs[...] * dsilu[:, :nb], gb[...] * dsilu[:, nb:nc], gc[...] * dsilu[:, nc:]], axis=1)

        gxc = gxc_of(gxs_ref, gb_ref, gc_ref, xc_ref, True)
        gxc16 = gxc.astype(BF16)
        nxt16 = gxc_of(gxs_n, gb_n, gc_n, xc_n, pl.program_id(0) < n_blk - 1).astype(BF16)
        x = x_ref[...].astype(F32)
        acc = w_ref[3:4, :] * gxc
        _accumulate(gw3, _colsum(gxc * x))
        _accumulate(gcb_ref, _colsum(gxc))
        for k, gw in enumerate((gw0, gw1, gw2)):
            shifted = _row_shift(gxc16, nxt16, SSM_CONV - 1 - k)
            acc = acc + w_ref[k:k + 1, :] * shifted
            _accumulate(gw, _colsum(shifted * x))
        o_ref[...] = acc.astype(BF16)

    ins = []
    for arr, width in ((g_xs, SSM_INNER), (g_b, nc - nb), (g_c, nc - nb), (xc, CONV_DIM)):
        ins += [_rows(arr, tr), nxt(arr, width)]
    ins += [_rows(proj, tr, CONV_DIM, cb), _whole(conv_w)]
    return _rowcall(body, "conv_bwd", t, tr, ins, [_oacc(CONV_DIM)] * 5, into=(g_proj, (tr, CONV_DIM), lambda i: (i, cb)))


def _pre_norm_bwd(g_h1, g_u, x, w_pre, tr=512):
    t = x.shape[0]

    def body(gh_ref, gu_ref, x_ref, w_ref, gx_ref, gw_ref):
        gx, gw = _rms_bwd(gu_ref[...], x_ref[...], w_ref[...])
        gx_ref[...] = gh_ref[...] + gx
        _accumulate(gw_ref, gw)

    return _rowcall(body, "pre_norm_bwd", t, tr, [_rows(g_h1, tr), _rows(g_u, tr), _rows(x, tr), _whole(w_pre)],
                    [_orow(t, D_MODEL, F32, tr), _oacc(D_MODEL)])


def _alibi_slopes(n):
    def pow2(m):
        start = 2.0 ** (-8.0 / m)
        return [start ** (i + 1) for i in range(m)]
    if (n & (n - 1)) == 0:
        s = pow2(n)
    else:
        c = 2 ** int(math.floor(math.log2(n)))
        s = pow2(c) + pow2(2 * c)[0::2][: n - c]
    return np.array(s, dtype=np.float32)


def _slope_rows():
    s = _alibi_slopes(N_ATT_HEADS).reshape(N_ATT_HEADS // 2, 2)
    return jnp.asarray(np.broadcast_to(np.repeat(s, HEAD_DIM, axis=1)[:, None, :], (N_ATT_HEADS // 2, 8, LANE)).copy())


ATT_MAX_BLOCK_ROWS = 2048


RESIDUE_MAJOR_FROM = 16


class _AttLayout:
    def __init__(self, t, dil):
        self.t, self.dil = t, dil
        self.rows = t // dil
        self.residue_major = dil >= RESIDUE_MAJOR_FROM
        if self.residue_major:
            bq, self.stride = min(512, self.rows), 1
        else:
            bq, self.stride = min(512, self.rows, ATT_MAX_BLOCK_ROWS // dil), dil
        self.nsub = bq // ATT_BLOCK
        self.nblk = self.rows // bq
        self.rb = bq * self.stride
        self.pb = ATT_BLOCK * self.stride
        self.n_pb = self.rows * self.stride // self.pb
        self.out_dtype = F32 if self.stride > 1 else BF16

    def qkv(self, proj):
        pb, pb16 = proj
        if self.residue_major:
            return pb16.reshape(self.rows, self.dil * PB_W), PB_W // LANE, 0
        return (pb if self.stride > 1 else pb16), 0, 0

    def act(self, a):
        return a.reshape(self.rows, self.dil * ATT_WIDTH) if self.residue_major else a

    def act_shape(self):
        return (self.rows, self.dil * ATT_WIDTH) if self.residue_major else (self.t, ATT_WIDTH)

    def col(self, r, band, c):
        return r * band + c if self.residue_major else c


def _residue_rows(r, stride, first_block, n_blocks=1):
    if stride == 1:
        return pl.ds(first_block * ATT_BLOCK, n_blocks * ATT_BLOCK)
    return pl.ds(r + first_block * ATT_BLOCK * stride, n_blocks * ATT_BLOCK, stride=stride)


def _lane_half():
    return lax.broadcasted_iota(jnp.int32, (ATT_BLOCK, LANE), 1) // HEAD_DIM


def _att_scores_mask(dil, first):
    iq = lax.broadcasted_iota(jnp.int32, (ATT_BLOCK, 2 * ATT_BLOCK), 0)
    jk = lax.broadcasted_iota(jnp.int32, (ATT_BLOCK, 2 * ATT_BLOCK), 1)
    dist = ATT_BLOCK + iq - jk
    valid = (dist >= 0) & (dist <= ATT_BLOCK) & (jnp.logical_not(first) | (jk >= ATT_BLOCK))
    return (dist * dil).astype(F32), valid


def _stack_heads(x):
    half = _lane_half()
    return jnp.concatenate([jnp.where(half == 0, x, jnp.zeros_like(x)), jnp.where(half == 1, x, jnp.zeros_like(x))], axis=0)


def _unstack_heads(x):
    return jnp.where(_lane_half() == 0, x[:ATT_BLOCK], x[ATT_BLOCK:])


def _head_columns(x):
    return jnp.concatenate([x[:, 0:1], x[:, HEAD_DIM:HEAD_DIM + 1]], axis=0)


def _stacked_bias(sl_ref, dist, valid):
    d2 = jnp.concatenate([dist, dist], axis=0)
    v2 = jnp.concatenate([valid, valid], axis=0)
    top = lax.broadcasted_iota(jnp.int32, d2.shape, 0) < ATT_BLOCK
    slope = jnp.where(top, sl_ref[0:1, 0:1], sl_ref[0:1, HEAD_DIM:HEAD_DIM + 1])
    return jnp.where(v2, -slope * d2, NEG)


def _att_fwd(proj, dil, slopes, others=()):
    t = proj[0].shape[0]
    lay = _AttLayout(t, dil)
    assert not others or lay.stride == 1 and not lay.residue_major
    nsub, nblk, rb, pb = lay.nsub, lay.nblk, lay.rb, lay.pb
    src, band, qb = lay.qkv(proj)
    aw = ATT_WIDTH // LANE
    n_other = 2 * len(others)

    def spec(off, prev=False):
        if prev:
            return pl.BlockSpec((pb, LANE), lambda hp, i, r: (jnp.maximum(i * nsub - 1, 0), lay.col(r, band, qb + off + hp)))
        return pl.BlockSpec((rb, LANE), lambda hp, i, r: (i, lay.col(r, band, qb + off + hp)))

    o_spec = pl.BlockSpec((rb, LANE), lambda hp, i, r: (i, lay.col(r, aw, hp)))

    def body(q_ref, kc_ref, kp_ref, vc_ref, vp_ref, sl_ref, *rest):
        other_refs, (o_ref, lse_ref) = rest[:n_other], rest[n_other:]
        i, r = pl.program_id(1), pl.program_id(2)
        for sub in range(nsub):
            rs = _residue_rows(r, lay.stride, sub)
            q = (q_ref[rs, :] * (HEAD_DIM ** -0.5)).astype(BF16)
            if sub == 0:
                r0 = _residue_rows(r, lay.stride, 0)
                kk = jnp.concatenate([kp_ref[r0, :], kc_ref[rs, :]], axis=0).astype(BF16)
                vv = jnp.concatenate([vp_ref[r0, :], vc_ref[rs, :]], axis=0).astype(BF16)
                first = i == 0
            else:
                ks = _residue_rows(r, lay.stride, sub - 1, 2)
                kk, vv = kc_ref[ks, :].astype(BF16), vc_ref[ks, :].astype(BF16)
                first = jnp.bool_(False)
            dist, valid = _att_scores_mask(dil, first)
            s = lax.dot_general(_stack_heads(q), kk, NT_DIMS, preferred_element_type=F32) + _stacked_bias(sl_ref, dist, valid)
            m = jnp.max(s, axis=-1, keepdims=True)
            p = jnp.exp(s - m)
            l = jnp.sum(p, axis=-1, keepdims=True)
            out = _unstack_heads(jnp.dot(p.astype(BF16), vv, preferred_element_type=F32) / l)
            lse = _unstack_heads(jnp.broadcast_to(m + jnp.log(l), (2 * ATT_BLOCK, LANE)))
            if others:
                outs = [out] + [ref[rs, :].astype(F32) for ref in other_refs[0::2]]
                lses = [lse] + [ref[rs, :] for ref in other_refs[1::2]]
                top = functools.reduce(jnp.maximum, lses)
                ws = [jnp.exp(x - top) for x in lses]
                tot = functools.reduce(jnp.add, ws)
                out = functools.reduce(jnp.add, [w * o for w, o in zip(ws, outs)]) / tot
                lse = top + jnp.log(tot)
            o_ref[rs, :] = out.astype(lay.out_dtype)
            lse_ref[rs, :] = lse

    o, lse = pl.pallas_call(
        body, grid=(N_ATT_HEADS // 2, nblk, dil),
        in_specs=[spec(0), spec(6), spec(6, True), spec(12), spec(12, True), pl.BlockSpec((None, 8, LANE), lambda hp, i, r: (hp, 0, 0))]
        + [o_spec] * n_other,
        out_specs=[o_spec, o_spec], out_shape=[S(lay.act_shape(), lay.out_dtype), S(lay.act_shape(), F32)],
        name=f"att_fwd_d{dil}", compiler_params=_params(("parallel", "parallel", "arbitrary")),
    )(src, src, src, src, src, slopes, *[a for pair in others for a in pair])
    return o.reshape(t, ATT_WIDTH), lse.reshape(t, ATT_WIDTH)


def _att_delta(g_att, att, tr=512):
    t = att.shape[0]

    def body(g_ref, a_ref, o_ref):
        prod = g_ref[...] * a_ref[...].astype(F32)
        o_ref[...] = _dot_split(prod, _block_ones(ATT_WIDTH, HEAD_DIM), 0, 3)

    return _rowcall(body, "att_delta", t, tr, [_rows(g_att, tr), _rows(att, tr)], [_orow(t, ATT_WIDTH, F32, tr)])[0]


def _att_bwd(proj, g_att, lse, delta, dil, slopes):
    t = proj[0].shape[0]
    lay = _AttLayout(t, dil)
    nsub, nblk, rb, pb, n_pb = lay.nsub, lay.nblk, lay.rb, lay.pb, lay.n_pb
    src, band, qb = lay.qkv(proj)
    aw = ATT_WIDTH // LANE

    def near(i, which):
        return jnp.maximum(i * nsub - 1, 0) if which == "prev" else jnp.minimum((i + 1) * nsub, n_pb - 1)

    def pspec(off, which=None):
        if which:
            return pl.BlockSpec((pb, LANE), lambda hp, i, r: (near(i, which), lay.col(r, band, qb + off + hp)))
        return pl.BlockSpec((rb, LANE), lambda hp, i, r: (i, lay.col(r, band, qb + off + hp)))

    def aspec(which=None):
        if which:
            return pl.BlockSpec((pb, LANE), lambda hp, i, r: (near(i, which), lay.col(r, aw, hp)))
        return pl.BlockSpec((rb, LANE), lambda hp, i, r: (i, lay.col(r, aw, hp)))

    scale = HEAD_DIM ** -0.5

    def body(q_ref, qn_ref, kc_ref, kp_ref, vc_ref, vp_ref, do_ref, don_ref, lse_ref, lsen_ref, dl_ref, dln_ref, sl_ref,
             dq_ref, dk_ref, dv_ref):
        i, r = pl.program_id(1), pl.program_id(2)
        half = _lane_half()

        def tile_grads(q, do, lse_q, dl_q, kk, vv, dist, valid):
            q2, do2 = _stack_heads(q), _stack_heads(do)
            s = lax.dot_general(q2, kk, NT_DIMS, preferred_element_type=F32) + _stacked_bias(sl_ref, dist, valid)
            p = jnp.exp(s - _head_columns(lse_q))
            dp = lax.dot_general(do2, vv, NT_DIMS, preferred_element_type=F32)
            ds16 = (p * (dp - _head_columns(dl_q))).astype(BF16)
            dq = _unstack_heads(jnp.dot(ds16, kk, preferred_element_type=F32)) * scale
            dk = lax.dot_general(ds16, q2, TN_DIMS, preferred_element_type=F32)
            dv = lax.dot_general(p.astype(BF16), do2, TN_DIMS, preferred_element_type=F32)
            return dq, dk, dv

        carry_k = carry_v = None
        for sub in range(nsub):
            rs = _residue_rows(r, lay.stride, sub)
            q = (q_ref[rs, :] * scale).astype(BF16)
            do = do_ref[rs, :].astype(BF16)
            if sub == 0:
                r0 = _residue_rows(r, lay.stride, 0)
                kk = jnp.concatenate([kp_ref[r0, :], kc_ref[rs, :]], axis=0).astype(BF16)
                vv = jnp.concatenate([vp_ref[r0, :], vc_ref[rs, :]], axis=0).astype(BF16)
                first = i == 0
            else:
                ks = _residue_rows(r, lay.stride, sub - 1, 2)
                kk, vv = kc_ref[ks, :].astype(BF16), vc_ref[ks, :].astype(BF16)
                first = jnp.bool_(False)
            dist, valid = _att_scores_mask(dil, first)
            dq, dk2, dv2 = tile_grads(q, do, lse_ref[rs, :], dl_ref[rs, :], kk, vv, dist, valid)
            dq_ref[rs, :] = dq.astype(lay.out_dtype)
            if sub > 0:
                rp = _residue_rows(r, lay.stride, sub - 1)
                dk_ref[rp, :] = (carry_k + dk2[:ATT_BLOCK, :]).astype(lay.out_dtype)
                dv_ref[rp, :] = (carry_v + dv2[:ATT_BLOCK, :]).astype(lay.out_dtype)
            carry_k, carry_v = dk2[ATT_BLOCK:, :], dv2[ATT_BLOCK:, :]
        rl = _residue_rows(r, lay.stride, nsub - 1)
        rn = _residue_rows(r, lay.stride, 0)
        iq = lax.broadcasted_iota(jnp.int32, (ATT_BLOCK, ATT_BLOCK), 0)
        jk = lax.broadcasted_iota(jnp.int32, (ATT_BLOCK, ATT_BLOCK), 1)
        dist_i = ATT_BLOCK + iq - jk
        valid = (dist_i >= 0) & (dist_i <= ATT_BLOCK) & (i < nblk - 1)
        qn = (qn_ref[rn, :] * scale).astype(BF16)
        _, dk1, dv1 = tile_grads(qn, don_ref[rn, :].astype(BF16), lsen_ref[rn, :], dln_ref[rn, :],
                                 kc_ref[rl, :].astype(BF16), vc_ref[rl, :].astype(BF16), (dist_i * dil).astype(F32), valid)
        dk_ref[rl, :] = (carry_k + dk1).astype(lay.out_dtype)
        dv_ref[rl, :] = (carry_v + dv1).astype(lay.out_dtype)

    gv, lv, dlv = lay.act(g_att), lay.act(lse), lay.act(delta)
    dq, dk, dv = pl.pallas_call(
        body, grid=(N_ATT_HEADS // 2, nblk, dil),
        in_specs=[pspec(0), pspec(0, "next"), pspec(6), pspec(6, "prev"), pspec(12), pspec(12, "prev"),
                  aspec(), aspec("next"), aspec(), aspec("next"), aspec(), aspec("next"),
                  pl.BlockSpec((None, 8, LANE), lambda hp, i, r: (hp, 0, 0))],
        out_specs=[aspec(), aspec(), aspec()], out_shape=[S(lay.act_shape(), lay.out_dtype)] * 3,
        name=f"att_bwd_d{dil}", compiler_params=_params(("parallel", "parallel", "arbitrary")),
    )(src, src, src, src, src, src, gv, gv, lv, lv, dlv, dlv, slopes)
    return dq.reshape(t, ATT_WIDTH), dk.reshape(t, ATT_WIDTH), dv.reshape(t, ATT_WIDTH)


def _att_grad_sum(dqs, dks, dvs, g_proj, tr=2048):
    t = dqs[0].shape[0]
    cw = 2 * LANE
    per = ATT_WIDTH // cw
    arrays = list(dqs) + list(dks) + list(dvs)
    n_pat = len(dqs)

    def body(*refs):
        o_ref = refs[-1]
        which = pl.program_id(1) // per
        tot = jnp.zeros((tr, cw), F32)
        for s in range(3):
            part = refs[s * n_pat][...].astype(F32)
            for g in range(1, n_pat):
                part = part + refs[s * n_pat + g][...].astype(F32)
            tot = jnp.where(which == s, part, tot)
        o_ref[...] = tot.astype(BF16)

    in_specs = [pl.BlockSpec((tr, cw), lambda i, c, s=s: (i, jnp.clip(c - per * s, 0, per - 1))) for s in range(3) for _ in range(n_pat)]
    in_specs.append(pl.BlockSpec(memory_space=pl.ANY))
    return pl.pallas_call(
        lambda *refs: body(*refs[:len(arrays)], refs[-1]), grid=(t // tr, 3 * per), in_specs=in_specs,
        out_specs=pl.BlockSpec((tr, cw), lambda i, c: (i, OFF_QKV // cw + c)), out_shape=S(g_proj.shape, g_proj.dtype),
        input_output_aliases={len(arrays): 0}, name="att_grad_sum", compiler_params=_params(("arbitrary", "arbitrary")),
    )(*arrays, g_proj)


def _ssd_common(xs, dtx, cs, cs_t):
    ch = SSM_CHUNK
    row = lax.broadcasted_iota(jnp.int32, (ch, ch), 0)
    col = lax.broadcasted_iota(jnp.int32, (ch, ch), 1)
    cs_last = cs[ch - 1:ch, :]
    return dict(tril=col <= row, row=row, col=col, cs=cs, cs_t=cs_t, cs_last=cs_last,
                e=jnp.exp(cs), w=jnp.exp(cs_last - cs), xd=xs * dtx)


def _dot_split(a, b, split, terms=2):
    ops = [a, b]
    rest = ops[split]
    other = ops[1 - split].astype(BF16)
    out = None
    for _ in range(terms):
        piece = rest.astype(BF16)
        rest = rest - piece.astype(F32)
        part = jnp.dot(other, piece, preferred_element_type=F32) if split == 1 else jnp.dot(piece, other, preferred_element_type=F32)
        out = part if out is None else out + part
    return out


def _decay_col(cs_t, heads_per_group):
    r = lax.broadcasted_iota(jnp.int32, (heads_per_group * SSM_HEAD_DIM, SSM_STATE), 0) // SSM_HEAD_DIM
    out = jnp.zeros((heads_per_group * SSM_HEAD_DIM, SSM_STATE), F32)
    for j in range(heads_per_group):
        out = jnp.where(r == j, jnp.exp(cs_t[j:j + 1, SSM_CHUNK - 1:SSM_CHUNK]), out)
    return out


SSD_GROUPS_PER_STEP = 4


def _ssd_specs(t):
    hg = SSM_HEADS // SSM_GROUPS
    gw = hg * SSM_HEAD_DIM
    nb0 = SSM_INNER // SSM_STATE
    return hg, gw, nb0


def _ssd_group_views(gi, gw, wide, narrow, stacked):
    w = [r.at[:, pl.ds(gi * gw, gw)] for r in wide]
    n = [r.at[:, pl.ds(gi * SSM_STATE, SSM_STATE)] for r in narrow]
    return w, n, [r.at[gi] for r in stacked]


def _ssd_fwd(xa, dtx, csx, cst_g):
    t = xa.shape[0]
    nch = t // SSM_CHUNK
    hg, gw, nb0 = _ssd_specs(t)
    ch = SSM_CHUNK
    gp = SSD_GROUPS_PER_STEP

    def body(xs_ref, b_ref, c_ref, dtx_ref, cs_ref, cst_ref, y_ref, st_ref, h_scr):
        for gi in range(gp):
            (xs_g, dtx_g, cs_g, y_g), (b_g, c_g), (cst_gi, st_g) = _ssd_group_views(
                gi, gw, (xs_ref, dtx_ref, cs_ref, y_ref), (b_ref, c_ref), (cst_ref, st_ref))
            group_body(pl.program_id(0), pl.program_id(1) * gp + gi, xs_g, b_g, c_g, dtx_g, cs_g, cst_gi, y_g, st_g, h_scr)

    def group_body(cc, g, xs_ref, b_ref, c_ref, dtx_ref, cs_ref, cst_ref, y_ref, st_ref, h_scr):
        @pl.when(cc == 0)
        def _():
            h_scr[g] = jnp.zeros((gw, SSM_STATE), F32)

        q = _ssd_common(xs_ref[...], dtx_ref[...], cs_ref[...], cst_ref[...])
        bb, cb = b_ref[...].astype(BF16), c_ref[...].astype(BF16)
        h = h_scr[g]
        st_ref[...] = h
        xd16 = q["xd"].astype(BF16)
        c_both = lax.dot_general(cb, jnp.concatenate([bb, h.astype(BF16)], axis=0), NT_DIMS, preferred_element_type=F32)
        cbm = c_both[:, :SSM_STATE]
        y = c_both[:, SSM_STATE:] * q["e"]
        lane_head = lax.broadcasted_iota(jnp.int32, (ch, gw), 1) // SSM_HEAD_DIM
        gmats, xds = [], []
        for j in range(hg):
            diff = q["cs"][:, j * SSM_HEAD_DIM:j * SSM_HEAD_DIM + 1] - q["cs_t"][j:j + 1, :]
            gmats.append((cbm * jnp.exp(jnp.where(q["tril"], diff, NEG))).astype(BF16))
            xds.append(jnp.where(lane_head == j, xd16, jnp.zeros_like(xd16)))
        y = y + jnp.dot(jnp.concatenate(gmats, axis=1), jnp.concatenate(xds, axis=0), preferred_element_type=F32)
        y_ref[...] = y.astype(BF16)
        s_new = lax.dot_general((q["xd"] * q["w"]).astype(BF16), bb, TN_DIMS, preferred_element_type=F32)
        h_scr[g] = _decay_col(q["cs_t"], hg) * h + s_new

    wide = pl.BlockSpec((ch, gp * gw), lambda cc, g: (cc, g))
    return pl.pallas_call(
        body, grid=(nch, SSM_GROUPS // gp),
        in_specs=[wide,
                  pl.BlockSpec((ch, gp * SSM_STATE), lambda cc, g: (cc, nb0 // gp + g)),
                  pl.BlockSpec((ch, gp * SSM_STATE), lambda cc, g: (cc, (nb0 + SSM_GROUPS) // gp + g)),
                  wide, wide,
                  pl.BlockSpec((gp, 8, ch), lambda cc, g: (g, 0, cc))],
        out_specs=[wide, pl.BlockSpec((None, gp, gw, SSM_STATE), lambda cc, g: (cc, g, 0, 0))],
        out_shape=[S((t, SSM_INNER), BF16), S((nch, SSM_GROUPS, gw, SSM_STATE), F32)],
        scratch_shapes=[pltpu.VMEM((SSM_GROUPS, gw, SSM_STATE), F32)],
        name="ssd_fwd", compiler_params=_params(("arbitrary", "arbitrary")),
    )(xa, xa, xa, dtx, csx, cst_g)


def _ssd_bwd(xa, dtx, csx, cst_g, alog_x, g_y, states, dskip_x):
    t = xa.shape[0]
    nch = t // SSM_CHUNK
    hg, gw, nb0 = _ssd_specs(t)
    ch = SSM_CHUNK
    gp = SSD_GROUPS_PER_STEP

    def rc(cc):
        return nch - 1 - cc

    def body(xs_ref, b_ref, c_ref, dtx_ref, cs_ref, cst_ref, alx_ref, gy_ref, st_ref, dsk_ref,
             gxs_ref, gb_ref, gc_ref, gdt_ref, ga_ref, gh_scr):
        for gi in range(gp):
            wide, narrow, stacked = _ssd_group_views(
                gi, gw, (xs_ref, dtx_ref, cs_ref, alx_ref, gy_ref, dsk_ref, gxs_ref, gdt_ref, ga_ref), (b_ref, c_ref, gb_ref, gc_ref),
                (cst_ref, st_ref))
            xs_g, dtx_g, cs_g, alx_g, gy_g, dsk_g, gxs_g, gdt_g, ga_g = wide
            b_g, c_g, gb_g, gc_g = narrow
            group_body(pl.program_id(0), pl.program_id(1) * gp + gi, xs_g, b_g, c_g, dtx_g, cs_g, stacked[0], alx_g, gy_g, stacked[1],
                       dsk_g, gxs_g, gb_g, gc_g, gdt_g, ga_g, gh_scr)

    def group_body(cc, g, xs_ref, b_ref, c_ref, dtx_ref, cs_ref, cst_ref, alx_ref, gy_ref, st_ref, dsk_ref,
                   gxs_ref, gb_ref, gc_ref, gdt_ref, ga_ref, gh_scr):
        @pl.when(cc == 0)
        def _():
            gh_scr[g] = jnp.zeros((gw, SSM_STATE), F32)

        xs, dtx = xs_ref[...], dtx_ref[...]
        q = _ssd_common(xs, dtx, cs_ref[...], cst_ref[...])
        cs, cs_t, e, w, xd = q["cs"], q["cs_t"], q["e"], q["w"], q["xd"]
        bb, cb = b_ref[...].astype(BF16), c_ref[...].astype(BF16)
        gy = gy_ref[...]
        gy16, xd16 = gy.astype(BF16), xd.astype(BF16)
        h = st_ref[...]
        h16 = h.astype(BF16)
        ghn = gh_scr[g]
        ghn16 = ghn.astype(BF16)
        seg = _block_ones(gw, SSM_HEAD_DIM)
        c_both = lax.dot_general(cb, jnp.concatenate([bb, h16], axis=0), NT_DIMS, preferred_element_type=F32)
        cbm, chm = c_both[:, :SSM_STATE], c_both[:, SSM_STATE:]

        gye16 = (gy * e).astype(BF16)
        g_c = jnp.dot(gye16, h16, preferred_element_type=F32)
        gh_off = lax.dot_general(gye16, cb, TN_DIMS, preferred_element_type=F32)
        bgs = lax.dot_general(bb, ghn16, NT_DIMS, preferred_element_type=F32)
        g_xd = w * bgs
        head_sums = _dot_split(jnp.concatenate([gy * chm, xd * bgs], axis=0), seg, 0)
        g_e, g_w = head_sums[:ch], head_sums[ch:]
        g_b = jnp.dot((xd * w).astype(BF16), ghn16, preferred_element_type=F32)
        decay = _decay_col(cs_t, hg)
        gh_scr[g] = decay * ghn + gh_off
        rsum = jnp.sum(ghn * h, axis=1, keepdims=True)
        lane_head = lax.broadcasted_iota(jnp.int32, (ch, gw), 1) // SSM_HEAD_DIM
        lane_head1 = lax.broadcasted_iota(jnp.int32, (1, gw), 1) // SSM_HEAD_DIM
        g_el = jnp.zeros((1, gw), F32)
        g_cs = g_e * e - g_w * w
        upper = q["row"] <= q["col"]
        lms, gys = [], []
        for j in range(hg):
            g_el = jnp.where(lane_head1 == j, jnp.sum(rsum[j * SSM_HEAD_DIM:(j + 1) * SSM_HEAD_DIM, :], axis=0, keepdims=True), g_el)
            csc = cs[:, j * SSM_HEAD_DIM:j * SSM_HEAD_DIM + 1]
            csr = cs_t[j:j + 1, :]
            lms.append(jnp.exp(jnp.where(q["tril"], csc - csr, NEG)))
            gys.append(jnp.where(lane_head == j, gy16, jnp.zeros_like(gy16)))
        lm_st, gy_st = jnp.concatenate(lms, axis=0), jnp.concatenate(gys, axis=0)
        cbm_st = jnp.concatenate([cbm] * hg, axis=0)
        gcb_st = lax.dot_general(gy_st, xd16, NT_DIMS, preferred_element_type=F32) * lm_st
        gcb_sum = gcb_st[0:ch]
        for j in range(1, hg):
            gcb_sum = gcb_sum + gcb_st[j * ch:(j + 1) * ch]
        gcb16 = gcb_sum.astype(BF16)
        g_c = g_c + jnp.dot(gcb16, bb, preferred_element_type=F32)
        g_b = g_b + lax.dot_general(gcb16, cb, TN_DIMS, preferred_element_type=F32)
        g_xd = g_xd + lax.dot_general((cbm_st * lm_st).astype(BF16), gy_st, TN_DIMS, preferred_element_type=F32)
        m_st = gcb_st * cbm_st
        for j in range(hg):
            m_ls = m_st[j * ch:(j + 1) * ch]
            d_cs = jnp.sum(m_ls, axis=1, keepdims=True) - jnp.sum(m_ls.T, axis=1, keepdims=True)
            g_cs = g_cs + jnp.where(lane_head == j, d_cs, 0.0)
        extra = _colsum(g_w * w) + g_el * jnp.exp(q["cs_last"])
        g_cs = g_cs + jnp.where(lax.broadcasted_iota(jnp.int32, (ch, gw), 0) == ch - 1, extra, 0.0)
        g_la = _dot_split(upper, g_cs, 1)
        a_x = -jnp.exp(alx_ref[...])
        gdt_ref[...] = g_xd * xs + g_la * a_x * (1.0 / SSM_HEAD_DIM)
        ga_row = _colsum(g_la * (dtx * a_x)) * (1.0 / SSM_HEAD_DIM)
        ga_ref[...] = jnp.where(lax.broadcasted_iota(jnp.int32, (8, gw), 0) == 0, ga_row, 0.0)
        gxs_ref[...] = g_xd * dtx + gy * dsk_ref[...]
        gb_ref[...] = g_b
        gc_ref[...] = g_c

    wide = pl.BlockSpec((ch, gp * gw), lambda cc, g: (rc(cc), g))
    narrow = pl.BlockSpec((ch, gp * SSM_STATE), lambda cc, g: (rc(cc), g))
    row = pl.BlockSpec((1, gp * gw), lambda cc, g: (0, g))
    return pl.pallas_call(
        body, grid=(nch, SSM_GROUPS // gp),
        in_specs=[wide,
                  pl.BlockSpec((ch, gp * SSM_STATE), lambda cc, g: (rc(cc), nb0 // gp + g)),
                  pl.BlockSpec((ch, gp * SSM_STATE), lambda cc, g: (rc(cc), (nb0 + SSM_GROUPS) // gp + g)),
                  wide, wide,
                  pl.BlockSpec((gp, 8, ch), lambda cc, g: (g, 0, rc(cc))),
                  row, wide,
                  pl.BlockSpec((None, gp, gw, SSM_STATE), lambda cc, g: (rc(cc), g, 0, 0)),
                  row],
        out_specs=[wide, narrow, narrow, wide, pl.BlockSpec((8, gp * gw), lambda cc, g: (rc(cc), g))],
        out_shape=[S((t, SSM_INNER), F32), S((t, SSM_GROUPS * SSM_STATE), F32), S((t, SSM_GROUPS * SSM_STATE), F32),
                   S((t, SSM_INNER), F32), S((nch * 8, SSM_INNER), F32)],
        scratch_shapes=[pltpu.VMEM((SSM_GROUPS, gw, SSM_STATE), F32)],
        name="ssd_bwd", compiler_params=_params(("arbitrary", "arbitrary")),
    )(xa, xa, xa, dtx, csx, cst_g, alog_x, g_y, states, dskip_x)


def _local_step(x, target, w_pre, w_in_r, b_gate, conv_w, conv_b, dt_bias, a_log, d_skip, ssm_norm_w,
                late_weights, w_post, w_fpre, w_fpost, on_mid_grads, on_in_proj_grads):
    t = x.shape[0]
    mm = functools.partial(_matmul, tm=512)
    slopes = _slope_rows()
    hg = SSM_HEADS // SSM_GROUPS
    dt_bias_pad = jnp.pad(dt_bias, ((0, 0), (0, LANE - SSM_HEADS)))
    alog_x = jnp.repeat(a_log, SSM_HEAD_DIM, axis=1)
    alog_pad = jnp.pad(a_log, ((0, 0), (0, LANE - SSM_HEADS)))
    dskip_x = jnp.repeat(d_skip, SSM_HEAD_DIM, axis=1)

    u = _pre_norm(x, w_pre)
    pa = _matmul(u, w_in_r, mode="nn", out_dtype=BF16, name="in_proj_zgx", tm=1024, tn=2048, tk=D_MODEL, b_cols=(0, PA_W))
    pb, pb16 = _matmul(u, w_in_r[:, PA_W:], mode="nn", out_dtype=F32, name="in_proj_qkvdt", tm=1024, tn=PB_W // 2, tk=D_MODEL,
                       epilogue="also_bf16")
    dils = [dil for _, dil in DILATED_PATTERNS]
    wide = [_att_fwd((pb, pb16), dil, slopes) for dil in dils[1:]]
    att, lse = _att_fwd((pb, pb16), dils[0], slopes, others=wide)
    xa, xc = _conv_fwd(pa, conv_w, conv_b)
    dtx, csx, cst = _dt_fwd(pb, dt_bias_pad, alog_pad)
    cst_g = jnp.pad(cst[:SSM_HEADS].reshape(SSM_GROUPS, hg, t), ((0, 0), (0, 8 - hg), (0, 0)))
    y_ssd, states = _ssd_fwd(xa, dtx, csx, cst_g)
    y4 = _gate_norm_fwd(y_ssd, xa, pa, dskip_x, ssm_norm_w)
    w_att, w_ssm, w_out, w_up, w_down = late_weights(y4)
    att_p = mm(att, w_att, mode="nn", out_dtype=BF16, name="att_proj", tn=D_MODEL, tk=ATT_WIDTH)
    ssm_p = mm(y4, w_ssm, mode="nn", out_dtype=BF16, name="ssm_proj", tn=D_MODEL, tk=SSM_INNER)
    mixin = _gating_fwd(pa, b_gate, att_p, ssm_p)
    mixed = mm(mixin, w_out, mode="nn", out_dtype=F32, name="out_proj", tn=D_MODEL, tk=D_MODEL)
    h1, f = _mix_post_ffn_pre(x, mixed, w_post, w_fpre)
    act, up = _matmul(f, w_up, mode="nn", out_dtype=BF16, name="ffn_up", tm=2048, tn=FFN_HIDDEN // N_DEV, tk=D_MODEL, epilogue="relu2", stacked=True)
    dn = mm(act, w_down, mode="nn", out_dtype=F32, name="ffn_down", tn=D_MODEL, tk=FFN_HIDDEN)
    loss, g_h2, g_dn, gw_fpost = _loss_and_ffn_post_bwd(h1, dn, w_fpost, target)

    g_up = _matmul(g_dn, w_down, mode="nt", out_dtype=BF16, name="ffn_down_bwd_x", tm=1024, tn=2048, tk=D_MODEL, epilogue="relu2_bwd",
                   extra=up)
    gw_down = _matmul(act, g_dn, mode="tn", out_dtype=BF16, name="ffn_down_bwd_w", tm=1024, tn=D_MODEL, tk=2048)
    w_up_rows = jnp.moveaxis(w_up, 0, 1).reshape(D_MODEL, FFN_HIDDEN)
    g_f = mm(g_up, w_up_rows, mode="nt", out_dtype=F32, name="ffn_up_bwd_x", tn=D_MODEL, tk=FFN_HIDDEN)
    gw_up = _matmul(f, g_up, mode="tn", out_dtype=BF16, name="ffn_up_bwd_w", tm=D_MODEL, tn=FFN_HIDDEN // N_DEV, tk=2048, stacked=True)
    g_h1, g_mixed, gw_fpre, gw_post = _ffn_pre_mix_post_bwd(g_h2, g_f, h1, w_fpre, mixed, w_post)
    g_mixin = mm(g_mixed, w_out, mode="nt", out_dtype=BF16, name="out_proj_bwd_x", tn=D_MODEL, tk=D_MODEL)
    gw_out = _matmul(mixin, g_mixed, mode="tn", out_dtype=BF16, name="out_proj_bwd_w", tm=D_MODEL, tn=D_MODEL, tk=2048)
    g_proj = lax.empty((t, PROJ_W), BF16)
    g_att_p, g_ssm_p, g_b_gate, g_proj = _gating_bwd(g_mixin, pa, b_gate, att_p, ssm_p, g_proj)
    g_att = mm(g_att_p, w_att, mode="nt", out_dtype=F32, name="att_proj_bwd_x", tn=ATT_WIDTH, tk=D_MODEL)
    gw_att = _matmul(att, g_att_p, mode="tn", out_dtype=BF16, name="att_proj_bwd_w", tm=ATT_WIDTH, tn=D_MODEL, tk=2048)
    g_y4 = mm(g_ssm_p, w_ssm, mode="nt", out_dtype=BF16, name="ssm_proj_bwd_x", tn=SSM_INNER, tk=D_MODEL)
    gw_ssm = _matmul(y4, g_ssm_p, mode="tn", out_dtype=BF16, name="ssm_proj_bwd_w", tm=1024, tn=D_MODEL, tk=2048)
    token = on_mid_grads(dict(w_att_proj=gw_att, w_ssm_proj=gw_ssm, w_out=gw_out, w_up=gw_up, w_down=gw_down))
    if token is not None:
        ssm_norm_w = ssm_norm_w + jnp.tile(token[0:1, :], (1, SSM_INNER // LANE))
    g_y2, g_norm_w, _, g_d_skip, g_proj = _gate_norm_bwd(g_y4, y_ssd, xa, pa, dskip_x, ssm_norm_w, g_proj)
    g_xs, g_bm, g_cm, g_dtx, ga_rows = _ssd_bwd(xa, dtx, csx, cst_g, alog_x, g_y2, states, dskip_x)
    g_dt_bias, g_a_log, g_proj = _dt_bwd(g_dtx, ga_rows, pb, dt_bias_pad, g_proj)
    g_conv_b, gcw0, gcw1, gcw2, gcw3, g_proj = _conv_bwd(g_xs, g_bm, g_cm, xc, pa, conv_w, g_proj)
    delta = _att_delta(g_att, att)
    dqs, dks, dvs = [], [], []
    for _, dil in DILATED_PATTERNS:
        dq, dk, dv = _att_bwd((pb, pb16), g_att, lse, delta, dil, slopes)
        dqs.append(dq)
        dks.append(dk)
        dvs.append(dv)
    g_proj = _att_grad_sum(dqs, dks, dvs, g_proj)
    gw_in_r = _matmul(u, g_proj, mode="tn", out_dtype=BF16, name="in_proj_bwd_w", tm=D_MODEL, tn=1792, tk=2048)
    token = on_in_proj_grads(gw_in_r, jnp.concatenate([gcw0, gcw1, gcw2, gcw3], axis=0))
    g_u = _matmul(g_proj, w_in_r, mode="nt", out_dtype=F32, name="in_proj_bwd_x", tm=1024, tn=D_MODEL, tk=3584, after=token)
    g_x, gw_pre = _pre_norm_bwd(g_h1, g_u, x, w_pre)

    grads = dict(
        norm_mix_pre_w=gw_pre, b_gate=g_b_gate, conv_b=g_conv_b, dt_bias=g_dt_bias[:, :SSM_HEADS], a_log=g_a_log[:, :SSM_HEADS],
        d_skip=g_d_skip[:, :SSM_HEADS], ssm_norm_w=g_norm_w, norm_mix_post_w=gw_post, norm_ffn_pre_w=gw_fpre, norm_ffn_post_w=gw_fpost)
    return loss, g_x, grads


def _mesh_pos():
    return lax.axis_index("x"), lax.axis_index("y"), lax.axis_index("c")


def _all_gather(shards):
    n = len(shards)

    def body(*refs):
        x_refs, o_refs = refs[:n], refs[n:2 * n]
        send_sems, recv_sems, local_sems = refs[2 * n:]
        x, y, c = _mesh_pos()
        me, sibling = (x, y, c), (x, y, 1 - c)
        chips = [(1 - x, y), (x, 1 - y), (1 - x, 1 - y)]

        def copy(a, k, block, to, src=None):
            dst = o_refs[a].at[4 * block[0] + 2 * block[1] + block[2]]
            return pltpu.make_async_remote_copy(
                src_ref=dst if src is None else src, dst_ref=dst, send_sem=send_sems.at[7 * a + k], recv_sem=recv_sems.at[7 * a + k],
                device_id=to, device_id_type=pl.DeviceIdType.MESH)

        mine = [pltpu.make_async_copy(x_refs[a], o_refs[a].at[4 * x + 2 * y + c], local_sems.at[a]) for a in range(n)]
        for cp in mine:
            cp.start()
        first = []
        for a in range(n):
            first.append(copy(a, 0, me, sibling, src=x_refs[a]))
            first += [copy(a, 1 + j, me, (*chip, c), src=x_refs[a]) for j, chip in enumerate(chips)]
        for cp in first:
            cp.start()
        passed = []
        for j, chip in enumerate(chips):
            for a in range(n):
                copy(a, 1 + j, (*chip, c), me).wait_recv()
                passed.append(copy(a, 4 + j, (*chip, c), sibling))
                passed[-1].start()
        for a in range(n):
            copy(a, 0, sibling, me).wait_recv()
            for j, chip in enumerate(chips):
                copy(a, 4 + j, (*chip, 1 - c), me).wait_recv()
        for cp in first + passed:
            cp.wait_send()
        for cp in mine:
            cp.wait()

    hbm = pl.BlockSpec(memory_space=pltpu.HBM)
    return pl.pallas_call(
        body, out_shape=[S((N_DEV,) + s.shape, s.dtype) for s in shards],
        in_specs=[hbm] * n, out_specs=[hbm] * n,
        scratch_shapes=[pltpu.SemaphoreType.DMA((7 * n,)), pltpu.SemaphoreType.DMA((7 * n,)), pltpu.SemaphoreType.DMA((n,))],
        name="weights_all_gather",
    )(*shards)


def _exchange_grads(slab_arrays, small):
    n = len(slab_arrays)
    r_small = small.shape[0]

    def body(*refs):
        slab_refs, small_ref = refs[:n], refs[n]
        recv_refs, gsm_ref = refs[n + 1:2 * n + 1], refs[2 * n + 1]
        send_sems, recv_sems, local_sems = refs[2 * n + 2:]
        x, y, c = _mesh_pos()
        me = 4 * x + 2 * y + c

        def peer(k):
            px = 1 - x if k & 4 else x
            py = 1 - y if k & 2 else y
            pc = 1 - c if k & 1 else c
            return (px, py, pc), 4 * px + 2 * py + pc

        def copy(a, k, sending):
            to, lin = peer(k)
            sem = 7 * a + k - 1
            if a == n:
                src, dst = small_ref, gsm_ref.at[me if sending else lin]
            else:
                src, dst = slab_refs[a].at[lin], recv_refs[a].at[me if sending else lin]
            return pltpu.make_async_remote_copy(src_ref=src, dst_ref=dst, send_sem=send_sems.at[sem], recv_sem=recv_sems.at[sem],
                                                device_id=to, device_id_type=pl.DeviceIdType.MESH)

        own = [pltpu.make_async_copy(slab_refs[a].at[me], recv_refs[a].at[me], local_sems.at[a]) for a in range(n)]
        own.append(pltpu.make_async_copy(small_ref, gsm_ref.at[me], local_sems.at[n]))
        for cp in own:
            cp.start()
        order = [n] + list(range(n))
        sends = [copy(a, k, True) for a in order for k in range(1, N_DEV)]
        for cp in sends:
            cp.start()
        for a in order:
            for k in range(1, N_DEV):
                copy(a, k, False).wait_recv()
        for cp in sends:
            cp.wait_send()
        for cp in own:
            cp.wait()

    hbm = pl.BlockSpec(memory_space=pltpu.HBM)
    n_sem = 7 * (n + 1)
    res = pl.pallas_call(
        body, out_shape=[S(a.shape, a.dtype) for a in slab_arrays] + [S((N_DEV, r_small, LANE), small.dtype)],
        in_specs=[hbm] * (n + 1), out_specs=[hbm] * (n + 1),
        scratch_shapes=[pltpu.SemaphoreType.DMA((n_sem,)), pltpu.SemaphoreType.DMA((n_sem,)), pltpu.SemaphoreType.DMA((n + 1,))],
        name="grad_exchange",
    )(*slab_arrays, small)
    return res[:n], res[n]


def _peer_of(k, x, y, c):
    px = 1 - x if k & 4 else x
    py = 1 - y if k & 2 else y
    pc = 1 - c if k & 1 else c
    return (px, py, pc), 4 * px + 2 * py + pc


def _split_copies(src_refs, land_refs, send_sems, recv_sems, per_peer):
    x, y, c = _mesh_pos()
    me = 4 * x + 2 * y + c
    sends, recvs = [], []
    for a, (src, land) in enumerate(zip(src_refs, land_refs)):
        for k in range(1, N_DEV):
            to, lin = _peer_of(k, x, y, c)
            sem = 7 * a + k - 1
            piece = src.at[lin] if per_peer else src
            for slot, out in ((me, sends), (lin, recvs)):
                out.append(pltpu.make_async_remote_copy(
                    src_ref=piece, dst_ref=land.at[slot], send_sem=send_sems.at[sem], recv_sem=recv_sems.at[sem],
                    device_id=to, device_id_type=pl.DeviceIdType.MESH))
    return sends, recvs


def _remote_start(srcs, per_peer, name):
    n = len(srcs)
    lands = [lax.empty((N_DEV,) + (s.shape[1:] if per_peer else s.shape), s.dtype) for s in srcs]

    def body(*refs):
        src_refs, land_refs = refs[:n], refs[n:2 * n]
        send_sems, recv_sems = refs[2 * n], refs[2 * n + 1]
        token = refs[-1]
        sends, _ = _split_copies(src_refs, land_refs, send_sems, recv_sems, per_peer)
        for cp in sends:
            cp.start()
        token[...] = jnp.zeros_like(token)

    hbm = pl.BlockSpec(memory_space=pltpu.HBM)
    sem = pl.BlockSpec(memory_space=pltpu.SEMAPHORE)
    res = pl.pallas_call(
        body, name=name,
        out_shape=(pltpu.SemaphoreType.DMA((7 * n,)), pltpu.SemaphoreType.DMA((7 * n,)),
                   *[pltpu.HBM(a.shape, a.dtype) for a in srcs + lands], S((8, LANE), F32)),
        in_specs=[hbm] * (2 * n), out_specs=(sem, sem, *[hbm] * (2 * n), pl.BlockSpec(memory_space=pltpu.VMEM)),
        input_output_aliases={i: 2 + i for i in range(2 * n)},
        compiler_params=pltpu.CompilerParams(has_side_effects=pltpu.SideEffectType.DATAFLOW_SIDE_EFFECTING),
    )(*[pltpu.with_memory_space_constraint(a, pltpu.HBM) for a in srcs + lands])
    return dict(sems=res[:2], srcs=list(res[2:2 + n]), lands=list(res[2 + n:2 + 2 * n]), per_peer=per_peer), res[-1]


def _remote_wait(handle, after, name):
    n = len(handle["srcs"])
    per_peer = handle["per_peer"]

    def body(*refs):
        src_refs, land_refs = refs[:n], refs[n:2 * n]
        send_sems, recv_sems = refs[2 * n], refs[2 * n + 1]
        sends, recvs = _split_copies(src_refs, land_refs, send_sems, recv_sems, per_peer)
        for cp in sends:
            cp.wait_send()
        for cp in recvs:
            cp.wait_recv()

    hbm = pl.BlockSpec(memory_space=pltpu.HBM)
    sem = pl.BlockSpec(memory_space=pltpu.SEMAPHORE)
    arrays = handle["srcs"] + handle["lands"]
    res = pl.pallas_call(
        body, name=name, out_shape=tuple(pltpu.HBM(a.shape, a.dtype) for a in arrays),
        in_specs=[hbm] * (2 * n) + [sem, sem, pl.BlockSpec(memory_space=pl.ANY)], out_specs=tuple([hbm] * (2 * n)),
        input_output_aliases={i: i for i in range(2 * n)},
        compiler_params=pltpu.CompilerParams(has_side_effects=pltpu.SideEffectType.DATAFLOW_SIDE_EFFECTING),
    )(*arrays, *handle["sems"], after)
    return list(res[n:])


def _with_own(lands, own, me):
    return [lax.dynamic_update_index_in_dim(land, o.astype(land.dtype), me, 0) for land, o in zip(lands, own)]


def _adamw(w, m, v, slabs, name, tr):
    r, cols = w.shape
    c1 = 1.0 - ADAM_B1 ** ADAM_STEP
    c2 = 1.0 - ADAM_B2 ** ADAM_STEP

    def body(w_ref, m_ref, v_ref, s_ref, g_ref, d_ref, nm_ref, nv_ref):
        g = s_ref[0].astype(F32)
        for d in range(1, N_DEV):
            g = g + s_ref[d].astype(F32)
        nm = ADAM_B1 * m_ref[...] + (1.0 - ADAM_B1) * g
        nv = ADAM_B2 * v_ref[...] + (1.0 - ADAM_B2) * (g * g)
        g_ref[...] = g
        nm_ref[...] = nm
        nv_ref[...] = nv
        d_ref[...] = -ADAM_LR * ((nm / c1) / (jnp.sqrt(nv / c2) + ADAM_EPS) + ADAM_WD * w_ref[...])

    assert r % tr == 0, name
    blk = pl.BlockSpec((tr, cols), lambda i: (i, 0))
    return pl.pallas_call(
        body, grid=(r // tr,), in_specs=[blk, blk, blk, pl.BlockSpec((N_DEV, tr, cols), lambda i: (0, i, 0))],
        out_specs=[blk] * 4, out_shape=[S((r, cols), F32)] * 4, name=name, compiler_params=_params(("parallel",)),
    )(w, m, v, slabs)


BIG = ("w_in", "w_att_proj", "w_up", "w_ssm_proj", "w_out", "w_down", "conv_w")
ADAMW_ROWS = dict(w_in=256, w_att_proj=768, w_up=512, w_ssm_proj=256, w_out=128, w_down=256, conv_w=4)
SMALL = ("norm_mix_pre_w", "b_gate", "conv_b", "dt_bias", "a_log", "d_skip", "ssm_norm_w", "norm_mix_post_w",
         "norm_ffn_pre_w", "norm_ffn_post_w")
ORDER = ("norm_mix_pre_w", "w_in", "b_gate", "conv_w", "conv_b", "dt_bias", "a_log", "d_skip", "ssm_norm_w", "w_att_proj",
         "w_ssm_proj", "w_out", "norm_mix_post_w", "norm_ffn_pre_w", "w_up", "w_down", "norm_ffn_post_w")
ROW_SHARDED = ("w_ssm_proj", "w_out", "w_down")
LATE = ("w_att_proj", "w_ssm_proj", "w_out", "w_up", "w_down")
IN_PROJ_W = 10528
IN_SHARD_W = IN_PROJ_W // N_DEV
IN_SEGMENTS = ((2304, 4352), (8480, 10528), (4352, 8448), (0, 2304), (8448, 8480))


def _pack(parts, rows_multiple):
    flat = jnp.concatenate([p.reshape(-1) for p in parts])
    pad = (-flat.shape[0]) % (rows_multiple * LANE)
    return jnp.pad(flat, (0, pad)).reshape(-1, LANE)


def _unpack(flat2d, shapes):
    flat, out, off = flat2d.reshape(-1), [], 0
    for sh in shapes:
        n = int(np.prod(sh))
        out.append(flat[off:off + n].reshape(sh))
        off += n
    return out


def _reorder_in_proj(w):
    qkv, z, xbc = w[:, :2304], w[:, 2304:4352], w[:, 4352:8448]
    dt, gate = w[:, 8448:8480], w[:, 8480:10528]
    return jnp.concatenate([z, gate, xbc, qkv, dt, jnp.zeros((w.shape[0], PROJ_W - 10528), w.dtype)], axis=1)


def _restore_in_proj(wr):
    return jnp.concatenate([wr[:, OFF_QKV:OFF_QKV + 2304], wr[:, OFF_Z:OFF_Z + 2048], wr[:, OFF_XBC:OFF_XBC + 4096],
                            wr[:, OFF_DT:OFF_DT + 32], wr[:, OFF_GL:OFF_GL + 2048]], axis=1)


def _assemble_in_proj(g):
    pieces = []
    for lo, hi in IN_SEGMENTS:
        while lo < hi:
            d = lo // IN_SHARD_W
            end = min(hi, (d + 1) * IN_SHARD_W)
            pieces.append(g[d][:, lo - d * IN_SHARD_W:end - d * IN_SHARD_W])
            lo = end
    pieces.append(jnp.zeros((g.shape[1], PROJ_W - IN_PROJ_W), g.dtype))
    return jnp.concatenate(pieces, axis=1)


def _in_proj_slabs(wr):
    orig = _restore_in_proj(wr)
    return jnp.stack([orig[:, d * IN_SHARD_W:(d + 1) * IN_SHARD_W] for d in range(N_DEV)])


def kernel(x, norm_mix_pre_w, w_in, b_gate, conv_w, conv_b, dt_bias, a_log, d_skip, ssm_norm_w, w_att_proj, w_ssm_proj, w_out, norm_mix_post_w, norm_ffn_pre_w, w_up, w_down, norm_ffn_post_w, loss_target, m_norm_mix_pre_w, m_w_in, m_b_gate, m_conv_w, m_conv_b, m_dt_bias, m_a_log, m_d_skip, m_ssm_norm_w, m_w_att_proj, m_w_ssm_proj, m_w_out, m_norm_mix_post_w, m_norm_ffn_pre_w, m_w_up, m_w_down, m_norm_ffn_post_w, v_norm_mix_pre_w, v_w_in, v_b_gate, v_conv_w, v_conv_b, v_dt_bias, v_a_log, v_d_skip, v_ssm_norm_w, v_w_att_proj, v_w_ssm_proj, v_w_out, v_norm_mix_post_w, v_norm_ffn_pre_w, v_w_up, v_w_down, v_norm_ffn_post_w):
    w = dict(norm_mix_pre_w=norm_mix_pre_w, w_in=w_in, b_gate=b_gate, conv_w=conv_w, conv_b=conv_b, dt_bias=dt_bias, a_log=a_log,
             d_skip=d_skip, ssm_norm_w=ssm_norm_w, w_att_proj=w_att_proj, w_ssm_proj=w_ssm_proj, w_out=w_out,
             norm_mix_post_w=norm_mix_post_w, norm_ffn_pre_w=norm_ffn_pre_w, w_up=w_up, w_down=w_down, norm_ffn_post_w=norm_ffn_post_w)
    m = dict(norm_mix_pre_w=m_norm_mix_pre_w, w_in=m_w_in, b_gate=m_b_gate, conv_w=m_conv_w, conv_b=m_conv_b, dt_bias=m_dt_bias,
             a_log=m_a_log, d_skip=m_d_skip, ssm_norm_w=m_ssm_norm_w, w_att_proj=m_w_att_proj, w_ssm_proj=m_w_ssm_proj, w_out=m_w_out,
             norm_mix_post_w=m_norm_mix_post_w, norm_ffn_pre_w=m_norm_ffn_pre_w, w_up=m_w_up, w_down=m_w_down, norm_ffn_post_w=m_norm_ffn_post_w)
    v = dict(norm_mix_pre_w=v_norm_mix_pre_w, w_in=v_w_in, b_gate=v_b_gate, conv_w=v_conv_w, conv_b=v_conv_b, dt_bias=v_dt_bias,
             a_log=v_a_log, d_skip=v_d_skip, ssm_norm_w=v_ssm_norm_w, w_att_proj=v_w_att_proj, w_ssm_proj=v_w_ssm_proj, w_out=v_w_out,
             norm_mix_post_w=v_norm_mix_post_w, norm_ffn_pre_w=v_norm_ffn_pre_w, w_up=v_w_up, w_down=v_w_down, norm_ffn_post_w=v_norm_ffn_post_w)
    shard_shapes = {n: w[n].shape[1:] for n in ORDER}

    mx, my, mc = _mesh_pos()
    me = 4 * mx + 2 * my + mc

    g_in, g_conv = _all_gather([w["w_in"][0].astype(BF16), w["conv_w"][0]])
    conv_full = jnp.moveaxis(g_conv, 0, 1).reshape(SSM_CONV, CONV_DIM)
    late_shards = [w[n][0].astype(BF16) for n in LATE]
    late_handle, token = _remote_start(late_shards, False, "late_weights_start")
    w_pre = w["norm_mix_pre_w"] + jnp.tile(token[0:1, :], (1, D_MODEL // LANE))

    def late_weights(after):
        full = dict(zip(LATE, _with_own(_remote_wait(late_handle, after, "late_weights_wait"), late_shards, me)))
        for n in ROW_SHARDED:
            full[n] = full[n].reshape(-1, full[n].shape[2])
        w_att = jnp.moveaxis(full["w_att_proj"], 0, 1).reshape(ATT_WIDTH, D_MODEL)
        return w_att, full["w_ssm_proj"], full["w_out"], full["w_up"], full["w_down"]

    started = {}

    def start_exchange(tag, slabs):
        own = [lax.dynamic_index_in_dim(s, me, 0, keepdims=False) for s in slabs]
        handle, tok = _remote_start(slabs, True, tag + "_grads_start")
        started[tag] = (handle, own)
        return tok

    def on_mid_grads(g):
        slabs = dict(w_up=g["w_up"], w_att_proj=jnp.moveaxis(g["w_att_proj"].reshape(ATT_WIDTH, N_DEV, -1), 1, 0))
        for n in ROW_SHARDED:
            slabs[n] = g[n].reshape(N_DEV, -1, g[n].shape[1])
        return start_exchange("mid", [slabs[n] for n in LATE])

    def on_in_proj_grads(gw_in_r, g_conv_w):
        return start_exchange("in_proj", [_in_proj_slabs(gw_in_r), jnp.moveaxis(g_conv_w.reshape(SSM_CONV, N_DEV, -1), 1, 0)])

    loss, g_x, grads = _local_step(
        x[0], loss_target[0], w_pre, _assemble_in_proj(g_in), w["b_gate"], conv_full, w["conv_b"], w["dt_bias"], w["a_log"],
        w["d_skip"], w["ssm_norm_w"], late_weights, w["norm_mix_post_w"], w["norm_ffn_pre_w"], w["norm_ffn_post_w"],
        on_mid_grads, on_in_proj_grads)

    recv = {}
    for tag, names in (("mid", LATE), ("in_proj", ("w_in", "conv_w"))):
        handle, own = started[tag]
        recv.update(zip(names, _with_own(_remote_wait(handle, g_x, tag + "_grads_wait"), own, me)))
    small = _pack([grads[n].astype(F32) for n in SMALL], 8)
    _, small_all = _exchange_grads([], small)

    small_shapes = [shard_shapes[n] for n in SMALL]
    small_out = _adamw(*[_pack([d_[n][0] for n in SMALL], 8) for d_ in (w, m, v)], small_all, "adamw_replicated", small_all.shape[1])
    big_out = {n: _adamw(w[n][0], m[n][0], v[n][0], recv[n], "adamw_" + n, ADAMW_ROWS[n]) for n in BIG}
    res = []
    for which, small_flat in enumerate(small_out):
        vals = {n: big_out[n][which] for n in BIG}
        vals.update(zip(SMALL, _unpack(small_flat, small_shapes)))
        res.append([vals[n][None] for n in ORDER])
    g_out, d_out, m_out, v_out = res
    total = lax.psum(loss[0, 0], ("x", "y", "c"))
    return (total, g_x[None], *g_out, *d_out, *m_out, *v_out)
```

```python
import functools
import math

import jax
import jax.numpy as jnp
import numpy as np
from jax import lax
from jax.experimental import pallas as pl
from jax.experimental.pallas import tpu as pltpu

F32 = jnp.float32
BF16 = jnp.bfloat16

D_MODEL = 1024
HEAD_DIM = 64
N_ATT_HEADS = 12
ATT_WIDTH = N_ATT_HEADS * HEAD_DIM
DILATED_PATTERNS = ((128, 1), (512, 4), (2048, 16))
ATT_BLOCK = 128
SSM_INNER = 2048
SSM_HEAD_DIM = 64
SSM_HEADS = 32
SSM_GROUPS = 8
SSM_STATE = 128
SSM_CHUNK = 128
CONV_DIM = 4096
SSM_CONV = 4
FFN_HIDDEN = 4096
RMS_EPS = 1e-6
N_DEV = 8

ADAM_LR = 0.001
ADAM_B1 = 0.9
ADAM_B2 = 0.999
ADAM_EPS = 1e-08
ADAM_WD = 0.01
ADAM_STEP = 10

LANE = 128
OFF_Z, OFF_GL, OFF_XBC, OFF_QKV, OFF_DT = 0, 2048, 4096, 8192, 10496
PROJ_W = 10752
PROJ_BLOCKS = PROJ_W // LANE
PA_W = OFF_QKV
PB_W = PROJ_W - OFF_QKV
PB_DT = OFF_DT - OFF_QKV
VMEM_LIMIT = 52 * 1024 * 1024
NEG = -1e30

HI = lax.Precision.HIGHEST
NT_DIMS = (((1,), (1,)), ((), ()))
TN_DIMS = (((0,), (0,)), ((), ()))
S = jax.ShapeDtypeStruct


def _params(sem):
    return pltpu.CompilerParams(dimension_semantics=sem, vmem_limit_bytes=VMEM_LIMIT)


def _matmul(a, b, *, mode, out_dtype, name, tm, tn, tk, epilogue=None, extra=None, stacked=False, after=None, b_cols=None):
    if mode == "nn":
        m, k = a.shape
        n = b.shape[0] * b.shape[2] if stacked else b.shape[1]
        col0 = 0
        if b_cols is not None:
            assert b_cols[0] % tn == 0, name
            col0, n = b_cols[0] // tn, b_cols[1]
        a_spec = pl.BlockSpec((tm, tk), lambda i, j, kk: (i, kk))
        b_spec = pl.BlockSpec((None, tk, tn), lambda i, j, kk: (j, kk, 0)) if stacked else pl.BlockSpec((tk, tn), lambda i, j, kk: (kk, col0 + j))
        dims = (((1,), (0,)), ((), ()))
    elif mode == "nt":
        m, k = a.shape
        n = b.shape[1] if stacked else b.shape[0]
        a_spec = pl.BlockSpec((tm, tk), lambda i, j, kk: (i, kk))
        b_spec = pl.BlockSpec((None, tn, tk), lambda i, j, kk: (kk, j, 0)) if stacked else pl.BlockSpec((tn, tk), lambda i, j, kk: (j, kk))
        dims = NT_DIMS
    else:
        (k, m), n = a.shape, b.shape[1]
        a_spec = pl.BlockSpec((tk, tm), lambda i, j, kk: (kk, i))
        b_spec = pl.BlockSpec((tk, tn), lambda i, j, kk: (kk, j))
        dims = TN_DIMS
    assert m % tm == 0 and n % tn == 0 and k % tk == 0, (name, m, n, k)
    if stacked:
        assert (tk if mode == "nt" else tn) * N_DEV == (k if mode == "nt" else n), name
    nk = k // tk
    o_spec = pl.BlockSpec((tm, tn), lambda i, j, kk: (i, j))
    in_specs, args = [a_spec, b_spec], [a, b]
    if epilogue == "relu2":
        out_shape = (S((m, n), BF16), S((m, n), BF16))
        out_specs = (o_spec, o_spec)
    elif epilogue == "also_bf16":
        out_shape = (S((m, n), out_dtype), S((m, n), BF16))
        out_specs = (o_spec, o_spec)
    elif stacked and mode == "tn":
        out_shape, out_specs = S((N_DEV, m, tn), out_dtype), pl.BlockSpec((None, tm, tn), lambda i, j, kk: (j, i, 0))
    else:
        out_shape, out_specs = S((m, n), out_dtype), o_spec
    if epilogue == "relu2_bwd":
        in_specs.append(o_spec)
        args.append(extra)
    n_in = len(args)
    if after is not None:
        in_specs.append(pl.BlockSpec(after.shape, lambda i, j, kk: (0,) * after.ndim))
        args.append(after)

    def finish(acc, refs):
        if epilogue == "relu2":
            r = jnp.maximum(acc, 0.0)
            refs[0][...] = (r * r).astype(BF16)
            refs[1][...] = acc.astype(BF16)
        elif epilogue == "also_bf16":
            refs[0][...] = acc.astype(out_dtype)
            refs[1][...] = acc.astype(BF16)
        elif epilogue == "relu2_bwd":
            up = refs[0][...].astype(F32)
            refs[1][...] = (acc * (2.0 * jnp.maximum(up, 0.0))).astype(out_dtype)
        else:
            refs[0][...] = acc.astype(out_dtype)

    def body(a_ref, b_ref, *rest):
        rest = rest[:n_in - 2] + rest[len(args) - 2:]
        part = lax.dot_general(a_ref[...].astype(BF16), b_ref[...].astype(BF16), dims, preferred_element_type=F32)
        if nk == 1:
            finish(part, rest)
            return
        acc_ref = rest[-1]
        kk = pl.program_id(2)

        @pl.when(kk == 0)
        def _():
            acc_ref[...] = part

        @pl.when(kk > 0)
        def _():
            acc_ref[...] += part

        @pl.when(kk == nk - 1)
        def _():
            finish(acc_ref[...], rest[:-1])

    scratch = [] if nk == 1 else [pltpu.VMEM((tm, tn), F32)]
    return pl.pallas_call(
        body, grid=(m // tm, n // tn, nk), in_specs=in_specs, out_specs=out_specs, out_shape=out_shape,
        scratch_shapes=scratch, name=name, compiler_params=_params(("parallel", "parallel", "arbitrary")),
    )(*args)


def _rowcall(body, name, n_rows, tr, ins, outs, scratch=(), into=None):
    in_specs = [pl.BlockSpec(bs, im) for _, bs, im in ins]
    out_specs = [pl.BlockSpec(bs, im) for _, _, bs, im in outs]
    out_shape = [S(sh, dt) for sh, dt, _, _ in outs]
    args = [a for a, _, _ in ins]
    aliases = {}
    kernel = body
    if into is not None:
        buf, bs, im = into
        n_in = len(args)
        in_specs.append(pl.BlockSpec(memory_space=pl.ANY))
        args.append(buf)
        out_specs.append(pl.BlockSpec(bs, im))
        out_shape.append(S(buf.shape, buf.dtype))
        aliases = {n_in: len(out_shape) - 1}

        def kernel(*refs):
            body(*refs[:n_in], *refs[n_in + 1:])

    return pl.pallas_call(
        kernel, grid=(n_rows // tr,), in_specs=in_specs, out_specs=out_specs, out_shape=out_shape,
        input_output_aliases=aliases, scratch_shapes=list(scratch), name=name, compiler_params=_params(("arbitrary",)),
    )(*args)


def _rows(arr, tr, width=None, cb=0):
    width = arr.shape[1] if width is None else width
    return (arr, (tr, width), lambda i, cb=cb: (i, cb))


def _whole(arr):
    nd = arr.ndim
    return (arr, arr.shape, lambda i, nd=nd: (0,) * nd)


def _orow(n_rows, width, dtype, tr):
    return ((n_rows, width), dtype, (tr, width), lambda i: (i, 0))


def _oacc(width):
    return ((1, width), F32, (1, width), lambda i: (0, 0))


def _accumulate(ref, value):
    first = pl.program_id(0) == 0

    @pl.when(first)
    def _():
        ref[...] = value

    @pl.when(jnp.logical_not(first))
    def _():
        ref[...] += value


def _colsum(v):
    return jnp.sum(v, axis=0, keepdims=True)


def _rms_fwd(x, w):
    r = lax.rsqrt(jnp.mean(x * x, axis=-1, keepdims=True) + RMS_EPS)
    return x * r * w


def _rms_bwd(gy, x, w):
    r = lax.rsqrt(jnp.mean(x * x, axis=-1, keepdims=True) + RMS_EPS)
    xn = x * r
    gxn = gy * w
    gx = r * (gxn - xn * jnp.mean(gxn * xn, axis=-1, keepdims=True))
    return gx, _colsum(gy * xn)


def _sigmoid(x):
    return 1.0 / (1.0 + jnp.exp(-x))


def _head_expand(n_heads_pad, n_heads, width):
    h = lax.broadcasted_iota(jnp.int32, (n_heads_pad, n_heads * width), 0)
    c = lax.broadcasted_iota(jnp.int32, (n_heads_pad, n_heads * width), 1)
    return (c // width == h).astype(F32)


def _head_reduce(n_heads, width, n_heads_pad):
    c = lax.broadcasted_iota(jnp.int32, (n_heads * width, n_heads_pad), 0)
    h = lax.broadcasted_iota(jnp.int32, (n_heads * width, n_heads_pad), 1)
    return (c // width == h).astype(F32)


def _block_ones(n, width):
    r = lax.broadcasted_iota(jnp.int32, (n, n), 0)
    c = lax.broadcasted_iota(jnp.int32, (n, n), 1)
    return (r // width == c // width).astype(F32)


def _pre_norm(x, w_pre, tr=512):
    t = x.shape[0]

    def body(x_ref, w_ref, u_ref):
        u_ref[...] = _rms_fwd(x_ref[...], w_ref[...]).astype(BF16)

    return _rowcall(body, "pre_norm", t, tr, [_rows(x, tr), _whole(w_pre)], [_orow(t, D_MODEL, BF16, tr)])[0]


CONV_HALO = 16


def _row_shift(cur, halo, j):
    tr = cur.shape[0]
    r = lax.broadcasted_iota(jnp.int32, (tr, tr), 0)
    c = lax.broadcasted_iota(jnp.int32, (tr, tr), 1)
    main = jnp.dot((c == r + j).astype(BF16), cur, preferred_element_type=F32)
    er = lax.broadcasted_iota(jnp.int32, (CONV_HALO, CONV_HALO), 0)
    ec = lax.broadcasted_iota(jnp.int32, (CONV_HALO, CONV_HALO), 1)
    if j < 0:
        edge = jnp.dot((ec == CONV_HALO + er + j).astype(BF16), halo, preferred_element_type=F32)
        return jnp.concatenate([main[:CONV_HALO] + edge, main[CONV_HALO:]], axis=0)
    edge = jnp.dot((ec == er + j - CONV_HALO).astype(BF16), halo, preferred_element_type=F32)
    return jnp.concatenate([main[:tr - CONV_HALO], main[tr - CONV_HALO:] + edge], axis=0)


def _conv_fwd(proj, conv_w, conv_b, tr=256):
    t = proj.shape[0]
    cb = OFF_XBC // CONV_DIM
    halo = (proj, (CONV_HALO, CONV_DIM), lambda i: (jnp.maximum(i * (tr // CONV_HALO) - 1, 0), cb))

    def body(cur_ref, prev_ref, w_ref, b_ref, o_ref, xc_ref):
        cur = cur_ref[...]
        prev = jnp.where(pl.program_id(0) > 0, prev_ref[...], jnp.zeros_like(prev_ref[...]))
        acc = b_ref[...] + w_ref[3:4, :] * cur.astype(F32)
        for k in range(SSM_CONV - 1):
            acc = acc + w_ref[k:k + 1, :] * _row_shift(cur, prev, -(SSM_CONV - 1 - k))
        o_ref[...] = (acc * _sigmoid(acc)).astype(BF16)
        xc_ref[...] = acc.astype(BF16)

    return _rowcall(body, "conv_fwd", t, tr, [_rows(proj, tr, CONV_DIM, cb), halo, _whole(conv_w), _whole(conv_b)],
                    [_orow(t, CONV_DIM, BF16, tr), _orow(t, CONV_DIM, BF16, tr)])


def _dt_fwd(proj, dt_bias_pad, alog_pad, tr=512):
    t = proj.shape[0]

    def body(raw_ref, b_ref, al_ref, dtx_ref, csx_ref, cst_ref):
        v = raw_ref[...] + b_ref[...]
        dt = jnp.maximum(v, 0.0) + jnp.log1p(jnp.exp(-jnp.abs(v)))
        expand = _head_expand(LANE, SSM_HEADS, SSM_HEAD_DIM)
        dtx_ref[...] = _dot_split(dt, expand, 0, 3)
        la = dt * (-jnp.exp(al_ref[...]))
        row = lax.broadcasted_iota(jnp.int32, (SSM_CHUNK, SSM_CHUNK), 0)
        col = lax.broadcasted_iota(jnp.int32, (SSM_CHUNK, SSM_CHUNK), 1)
        tril = (col <= row).astype(F32)
        cs = jnp.concatenate([_dot_split(tril, la[k * SSM_CHUNK:(k + 1) * SSM_CHUNK, :], 1, 3) for k in range(tr // SSM_CHUNK)], axis=0)
        csx_ref[...] = _dot_split(cs, expand, 0, 3)
        cst_ref[...] = cs.T

    return _rowcall(body, "dt_fwd", t, tr, [_rows(proj, tr, LANE, PB_DT // LANE), _whole(dt_bias_pad), _whole(alog_pad)],
                    [_orow(t, SSM_INNER, F32, tr), _orow(t, SSM_INNER, F32, tr), ((LANE, t), F32, (LANE, tr), lambda i: (0, i))])


def _gate_norm_fwd(y_ssd, xa, proj, dskip_x, norm_w, tr=256):
    t = y_ssd.shape[0]
    gw = SSM_INNER // SSM_GROUPS

    def body(y_ref, xs_ref, z_ref, d_ref, w_ref, o_ref):
        z = z_ref[...].astype(F32)
        y3 = (y_ref[...].astype(F32) + d_ref[...] * xs_ref[...].astype(F32)) * (z * _sigmoid(z))
        for g in range(SSM_GROUPS):
            sl = slice(g * gw, (g + 1) * gw)
            o_ref[:, sl] = _rms_fwd(y3[:, sl], w_ref[:, sl]).astype(BF16)

    return _rowcall(body, "gate_norm_fwd", t, tr,
                    [_rows(y_ssd, tr), _rows(xa, tr, SSM_INNER, 0), _rows(proj, tr, SSM_INNER, OFF_Z // SSM_INNER), _whole(dskip_x), _whole(norm_w)],
                    [_orow(t, SSM_INNER, BF16, tr)])[0]


def _gating_fwd(proj, b_gate, att_p, ssm_p, tr=512):
    t = proj.shape[0]

    def body(gl_ref, b_ref, a_ref, s_ref, o_ref):
        gates = _sigmoid(gl_ref[...].astype(F32) + b_ref[...])
        o_ref[...] = (gates[:, :D_MODEL] * a_ref[...].astype(F32) + gates[:, D_MODEL:] * s_ref[...].astype(F32)).astype(BF16)

    return _rowcall(body, "gating_fwd", t, tr, [_rows(proj, tr, 2 * D_MODEL, OFF_GL // (2 * D_MODEL)), _whole(b_gate), _rows(att_p, tr), _rows(ssm_p, tr)],
                    [_orow(t, D_MODEL, BF16, tr)])[0]


def _mix_post_ffn_pre(x, mixed, w_post, w_fpre, tr=512):
    t = x.shape[0]

    def body(x_ref, m_ref, wp_ref, wf_ref, h1_ref, f_ref):
        h1 = x_ref[...] + _rms_fwd(m_ref[...], wp_ref[...])
        h1_ref[...] = h1
        f_ref[...] = _rms_fwd(h1, wf_ref[...]).astype(BF16)

    return _rowcall(body, "mix_post_ffn_pre", t, tr, [_rows(x, tr), _rows(mixed, tr), _whole(w_post), _whole(w_fpre)],
                    [_orow(t, D_MODEL, F32, tr), _orow(t, D_MODEL, BF16, tr)])


def _loss_and_ffn_post_bwd(h1, dn, w_fpost, target, tr=512):
    t = h1.shape[0]

    def body(h1_ref, dn_ref, w_ref, tg_ref, loss_ref, gh2_ref, gdn_ref, gw_ref):
        dn = dn_ref[...]
        w = w_ref[...]
        err = h1_ref[...] + _rms_fwd(dn, w) - tg_ref[...]
        _accumulate(loss_ref, jnp.zeros((1, LANE), F32) + 0.5 * jnp.sum(jnp.mean(err * err, axis=-1, keepdims=True)))
        gh2 = err * (1.0 / D_MODEL)
        gh2_ref[...] = gh2
        gdn, gw = _rms_bwd(gh2, dn, w)
        gdn_ref[...] = gdn.astype(BF16)
        _accumulate(gw_ref, gw)

    return _rowcall(body, "loss_ffn_post_bwd", t, tr, [_rows(h1, tr), _rows(dn, tr), _whole(w_fpost), _rows(target, tr)],
                    [_oacc(LANE), _orow(t, D_MODEL, F32, tr), _orow(t, D_MODEL, BF16, tr), _oacc(D_MODEL)])


def _ffn_pre_mix_post_bwd(g_h2, g_f, h1, w_fpre, mixed, w_post, tr=512):
    t = h1.shape[0]

    def body(gh2_ref, gf_ref, h1_ref, wf_ref, m_ref, wp_ref, gh1_ref, gm_ref, gwf_ref, gwp_ref):
        gx, gwf = _rms_bwd(gf_ref[...], h1_ref[...], wf_ref[...])
        gh1 = gh2_ref[...] + gx
        gh1_ref[...] = gh1
        gm, gwp = _rms_bwd(gh1, m_ref[...], wp_ref[...])
        gm_ref[...] = gm.astype(BF16)
        _accumulate(gwf_ref, gwf)
        _accumulate(gwp_ref, gwp)

    return _rowcall(body, "ffn_pre_mix_post_bwd", t, tr,
                    [_rows(g_h2, tr), _rows(g_f, tr), _rows(h1, tr), _whole(w_fpre), _rows(mixed, tr), _whole(w_post)],
                    [_orow(t, D_MODEL, F32, tr), _orow(t, D_MODEL, BF16, tr), _oacc(D_MODEL), _oacc(D_MODEL)])


def _gating_bwd(g_mixin, proj, b_gate, att_p, ssm_p, g_proj, tr=512):
    t = proj.shape[0]

    def body(gm_ref, gl_ref, b_ref, a_ref, s_ref, ga_ref, gs_ref, gb_ref, ggl_ref):
        gates = _sigmoid(gl_ref[...].astype(F32) + b_ref[...])
        gm = gm_ref[...].astype(F32)
        g_att, g_ssm = gates[:, :D_MODEL], gates[:, D_MODEL:]
        ga_ref[...] = (gm * g_att).astype(BF16)
        gs_ref[...] = (gm * g_ssm).astype(BF16)
        ggl_a = gm * a_ref[...].astype(F32) * g_att * (1.0 - g_att)
        ggl_s = gm * s_ref[...].astype(F32) * g_ssm * (1.0 - g_ssm)
        ggl_ref[:, :D_MODEL] = ggl_a.astype(BF16)
        ggl_ref[:, D_MODEL:] = ggl_s.astype(BF16)
        _accumulate(gb_ref.at[:, :D_MODEL], _colsum(ggl_a))
        _accumulate(gb_ref.at[:, D_MODEL:], _colsum(ggl_s))

    return _rowcall(body, "gating_bwd", t, tr,
                    [_rows(g_mixin, tr), _rows(proj, tr, 2 * D_MODEL, OFF_GL // (2 * D_MODEL)), _whole(b_gate), _rows(att_p, tr), _rows(ssm_p, tr)],
                    [_orow(t, D_MODEL, BF16, tr), _orow(t, D_MODEL, BF16, tr), _oacc(2 * D_MODEL)],
                    into=(g_proj, (tr, 2 * D_MODEL), lambda i: (i, OFF_GL // (2 * D_MODEL))))


def _gate_norm_bwd(g_y4, y_ssd, xa, proj, dskip_x, norm_w, g_proj, tr=256):
    t = y_ssd.shape[0]
    gw = SSM_INNER // SSM_GROUPS

    def body(g_ref, y_ref, xs_ref, z_ref, d_ref, w_ref, gy2_ref, gnw_ref, gdx_ref, gd_ref, gz_ref):
        z = z_ref[...].astype(F32)
        xs = xs_ref[...].astype(F32)
        sg = _sigmoid(z)
        sz = z * sg
        y2 = y_ref[...].astype(F32) + d_ref[...] * xs
        y3 = y2 * sz
        g4 = g_ref[...].astype(F32)
        for g in range(SSM_GROUPS):
            sl = slice(g * gw, (g + 1) * gw)
            gy3, gnw = _rms_bwd(g4[:, sl], y3[:, sl], w_ref[:, sl])
            _accumulate(gnw_ref.at[:, sl], gnw)
            gy2 = gy3 * sz[:, sl]
            gy2_ref[:, sl] = gy2.astype(BF16)
            gz_ref[:, sl] = (gy3 * y2[:, sl] * (sg[:, sl] * (1.0 + z[:, sl] * (1.0 - sg[:, sl])))).astype(BF16)
            _accumulate(gdx_ref.at[:, sl], _colsum(gy2 * xs[:, sl]))
        tot = jnp.broadcast_to(gdx_ref[...], (8, SSM_INNER))
        gd_ref[...] = jnp.dot(tot, _head_reduce(SSM_HEADS, SSM_HEAD_DIM, LANE), precision=HI, preferred_element_type=F32)[0:1, :]

    return _rowcall(body, "gate_norm_bwd", t, tr,
                    [_rows(g_y4, tr), _rows(y_ssd, tr), _rows(xa, tr, SSM_INNER, 0), _rows(proj, tr, SSM_INNER, OFF_Z // SSM_INNER), _whole(dskip_x), _whole(norm_w)],
                    [_orow(t, SSM_INNER, BF16, tr), _oacc(SSM_INNER), _oacc(SSM_INNER), _oacc(LANE)],
                    into=(g_proj, (tr, SSM_INNER), lambda i: (i, OFF_Z // SSM_INNER)))


def _dt_bwd(g_dtx, ga_rows, proj, dt_bias_pad, g_proj, tr=512):
    t = proj.shape[0]
    tail = PROJ_W - OFF_DT

    def body(g_ref, ga_ref, raw_ref, b_ref, gb_ref, gal_ref, o_ref):
        red = _head_reduce(SSM_HEADS, SSM_HEAD_DIM, LANE)
        gdt = _dot_split(g_ref[...], red, 0, 3)
        graw = gdt * _sigmoid(raw_ref[...] + b_ref[...])
        o_ref[...] = jnp.concatenate([graw.astype(BF16), jnp.zeros((tr, tail - LANE), BF16)], axis=1)
        _accumulate(gb_ref, _colsum(graw))
        tot = jnp.broadcast_to(_colsum(ga_ref[...]), (8, SSM_INNER))
        gal_ref[...] = jnp.dot(tot, red, precision=HI, preferred_element_type=F32)[0:1, :]

    return _rowcall(body, "dt_bwd", t, tr, [_rows(g_dtx, tr), _whole(ga_rows), _rows(proj, tr, LANE, PB_DT // LANE), _whole(dt_bias_pad)],
                    [_oacc(LANE), _oacc(LANE)], into=(g_proj, (tr, tail), lambda i: (i, OFF_DT // tail)))


def _conv_bwd(g_xs, g_b, g_c, xc, proj, conv_w, g_proj, tr=256):
    t = proj.shape[0]
    n_blk = t // tr
    cb = OFF_XBC // CONV_DIM
    nb, nc = SSM_INNER, SSM_INNER + SSM_GROUPS * SSM_STATE
    def nxt(arr, width):
        return (arr, (CONV_HALO, width), lambda i: (jnp.minimum((i + 1) * (tr // CONV_HALO), t // CONV_HALO - 1), 0))

    def body(gxs_ref, gxs_n, gb_ref, gb_n, gc_ref, gc_n, xc_ref, xc_n, x_ref, w_ref, gcb_ref, gw0, gw1, gw2, gw3, o_ref):
        def gxc_of(gxs, gb, gc, xc, keep):
            xcf = xc[...].astype(F32)
            sg = _sigmoid(xcf)
            dsilu = jnp.where(keep, sg * (1.0 + xcf * (1.0 - sg)), 0.0)
            return jnp.concatenate([gxs[...] * dsilu[:, :nb], gb[...] * dsilu[:, nb:nc], gc[...] * dsilu[:, nc:]], axis=1)

        gxc = gxc_of(gxs_ref, gb_ref, gc_ref, xc_ref, True)
        gxc16 = gxc.astype(BF16)
        nxt16 = gxc_of(gxs_n, gb_n, gc_n, xc_n, pl.program_id(0) < n_blk - 1).astype(BF16)
        x = x_ref[...].astype(F32)
        acc = w_ref[3:4, :] * gxc
        _accumulate(gw3, _colsum(gxc * x))
        _accumulate(gcb_ref, _colsum(gxc))
        for k, gw in enumerate((gw0, gw1, gw2)):
            shifted = _row_shift(gxc16, nxt16, SSM_CONV - 1 - k)
            acc = acc + w_ref[k:k + 1, :] * shifted
            _accumulate(gw, _colsum(shifted * x))
        o_ref[...] = acc.astype(BF16)

    ins = []
    for arr, width in ((g_xs, SSM_INNER), (g_b, nc - nb), (g_c, nc - nb), (xc, CONV_DIM)):
        ins += [_rows(arr, tr), nxt(arr, width)]
    ins += [_rows(proj, tr, CONV_DIM, cb), _whole(conv_w)]
    return _rowcall(body, "conv_bwd", t, tr, ins, [_oacc(CONV_DIM)] * 5, into=(g_proj, (tr, CONV_DIM), lambda i: (i, cb)))


def _pre_norm_bwd(g_h1, g_u, x, w_pre, tr=512):
    t = x.shape[0]

    def body(gh_ref, gu_ref, x_ref, w_ref, gx_ref, gw_ref):
        gx, gw = _rms_bwd(gu_ref[...], x_ref[...], w_ref[...])
        gx_ref[...] = gh_ref[...] + gx
        _accumulate(gw_ref, gw)

    return _rowcall(body, "pre_norm_bwd", t, tr, [_rows(g_h1, tr), _rows(g_u, tr), _rows(x, tr), _whole(w_pre)],
                    [_orow(t, D_MODEL, F32, tr), _oacc(D_MODEL)])


def _alibi_slopes(n):
    def pow2(m):
        start = 2.0 ** (-8.0 / m)
        return [start ** (i + 1) for i in range(m)]
    if (n & (n - 1)) == 0:
        s = pow2(n)
    else:
        c = 2 ** int(math.floor(math.log2(n)))
        s = pow2(c) + pow2(2 * c)[0::2][: n - c]
    return np.array(s, dtype=np.float32)


def _slope_rows():
    s = _alibi_slopes(N_ATT_HEADS).reshape(N_ATT_HEADS // 2, 2)
    return jnp.asarray(np.broadcast_to(np.repeat(s, HEAD_DIM, axis=1)[:, None, :], (N_ATT_HEADS // 2, 8, LANE)).copy())


ATT_MAX_BLOCK_ROWS = 2048


RESIDUE_MAJOR_FROM = 16


class _AttLayout:
    def __init__(self, t, dil):
        self.t, self.dil = t, dil
        self.rows = t // dil
        self.residue_major = dil >= RESIDUE_MAJOR_FROM
        if self.residue_major:
            bq, self.stride = min(512, self.rows), 1
        else:
            bq, self.stride = min(512, self.rows, ATT_MAX_BLOCK_ROWS // dil), dil
        self.nsub = bq // ATT_BLOCK
        self.nblk = self.rows // bq
        self.rb = bq * self.stride
        self.pb = ATT_BLOCK * self.stride
        self.n_pb = self.rows * self.stride // self.pb
        self.out_dtype = F32 if self.stride > 1 else BF16

    def qkv(self, proj):
        pb, pb16 = proj
        if self.residue_major:
            return pb16.reshape(self.rows, self.dil * PB_W), PB_W // LANE, 0
        return (pb if self.stride > 1 else pb16), 0, 0

    def act(self, a):
        return a.reshape(self.rows, self.dil * ATT_WIDTH) if self.residue_major else a

    def act_shape(self):
        return (self.rows, self.dil * ATT_WIDTH) if self.residue_major else (self.t, ATT_WIDTH)

    def col(self, r, band, c):
        return r * band + c if self.residue_major else c


def _residue_rows(r, stride, first_block, n_blocks=1):
    if stride == 1:
        return pl.ds(first_block * ATT_BLOCK, n_blocks * ATT_BLOCK)
    return pl.ds(r + first_block * ATT_BLOCK * stride, n_blocks * ATT_BLOCK, stride=stride)


def _lane_half():
    return lax.broadcasted_iota(jnp.int32, (ATT_BLOCK, LANE), 1) // HEAD_DIM


def _att_scores_mask(dil, first):
    iq = lax.broadcasted_iota(jnp.int32, (ATT_BLOCK, 2 * ATT_BLOCK), 0)
    jk = lax.broadcasted_iota(jnp.int32, (ATT_BLOCK, 2 * ATT_BLOCK), 1)
    dist = ATT_BLOCK + iq - jk
    valid = (dist >= 0) & (dist <= ATT_BLOCK) & (jnp.logical_not(first) | (jk >= ATT_BLOCK))
    return (dist * dil).astype(F32), valid


def _stack_heads(x):
    half = _lane_half()
    return jnp.concatenate([jnp.where(half == 0, x, jnp.zeros_like(x)), jnp.where(half == 1, x, jnp.zeros_like(x))], axis=0)


def _unstack_heads(x):
    return jnp.where(_lane_half() == 0, x[:ATT_BLOCK], x[ATT_BLOCK:])


def _head_columns(x):
    return jnp.concatenate([x[:, 0:1], x[:, HEAD_DIM:HEAD_DIM + 1]], axis=0)


def _stacked_bias(sl_ref, dist, valid):
    d2 = jnp.concatenate([dist, dist], axis=0)
    v2 = jnp.concatenate([valid, valid], axis=0)
    top = lax.broadcasted_iota(jnp.int32, d2.shape, 0) < ATT_BLOCK
    slope = jnp.where(top, sl_ref[0:1, 0:1], sl_ref[0:1, HEAD_DIM:HEAD_DIM + 1])
    return jnp.where(v2, -slope * d2, NEG)


def _att_fwd(proj, dil, slopes, others=()):
    t = proj[0].shape[0]
    lay = _AttLayout(t, dil)
    assert not others or lay.stride == 1 and not lay.residue_major
    nsub, nblk, rb, pb = lay.nsub, lay.nblk, lay.rb, lay.pb
    src, band, qb = lay.qkv(proj)
    aw = ATT_WIDTH // LANE
    n_other = 2 * len(others)

    def spec(off, prev=False):
        if prev:
            return pl.BlockSpec((pb, LANE), lambda hp, i, r: (jnp.maximum(i * nsub - 1, 0), lay.col(r, band, qb + off + hp)))
        return pl.BlockSpec((rb, LANE), lambda hp, i, r: (i, lay.col(r, band, qb + off + hp)))

    o_spec = pl.BlockSpec((rb, LANE), lambda hp, i, r: (i, lay.col(r, aw, hp)))

    def body(q_ref, kc_ref, kp_ref, vc_ref, vp_ref, sl_ref, *rest):
        other_refs, (o_ref, lse_ref) = rest[:n_other], rest[n_other:]
        i, r = pl.program_id(1), pl.program_id(2)
        for sub in range(nsub):
            rs = _residue_rows(r, lay.stride, sub)
            q = (q_ref[rs, :] * (HEAD_DIM ** -0.5)).astype(BF16)
            if sub == 0:
                r0 = _residue_rows(r, lay.stride, 0)
                kk = jnp.concatenate([kp_ref[r0, :], kc_ref[rs, :]], axis=0).astype(BF16)
                vv = jnp.concatenate([vp_ref[r0, :], vc_ref[rs, :]], axis=0).astype(BF16)
                first = i == 0
            else:
                ks = _residue_rows(r, lay.stride, sub - 1, 2)
                kk, vv = kc_ref[ks, :].astype(BF16), vc_ref[ks, :].astype(BF16)
                first = jnp.bool_(False)
            dist, valid = _att_scores_mask(dil, first)
            s = lax.dot_general(_stack_heads(q), kk, NT_DIMS, preferred_element_type=F32) + _stacked_bias(sl_ref, dist, valid)
            m = jnp.max(s, axis=-1, keepdims=True)
            p = jnp.exp(s - m)
            l = jnp.sum(p, axis=-1, keepdims=True)
            out = _unstack_heads(jnp.dot(p.astype(BF16), vv, preferred_element_type=F32) / l)
            lse = _unstack_heads(jnp.broadcast_to(m + jnp.log(l), (2 * ATT_BLOCK, LANE)))
            if others:
                outs = [out] + [ref[rs, :].astype(F32) for ref in other_refs[0::2]]
                lses = [lse] + [ref[rs, :] for ref in other_refs[1::2]]
                top = functools.reduce(jnp.maximum, lses)
                ws = [jnp.exp(x - top) for x in lses]
                tot = functools.reduce(jnp.add, ws)
                out = functools.reduce(jnp.add, [w * o for w, o in zip(ws, outs)]) / tot
                lse = top + jnp.log(tot)
            o_ref[rs, :] = out.astype(lay.out_dtype)
            lse_ref[rs, :] = lse

    o, lse = pl.pallas_call(
        body, grid=(N_ATT_HEADS // 2, nblk, dil),
        in_specs=[spec(0), spec(6), spec(6, True), spec(12), spec(12, True), pl.BlockSpec((None, 8, LANE), lambda hp, i, r: (hp, 0, 0))]
        + [o_spec] * n_other,
        out_specs=[o_spec, o_spec], out_shape=[S(lay.act_shape(), lay.out_dtype), S(lay.act_shape(), F32)],
        name=f"att_fwd_d{dil}", compiler_params=_params(("parallel", "parallel", "arbitrary")),
    )(src, src, src, src, src, slopes, *[a for pair in others for a in pair])
    return o.reshape(t, ATT_WIDTH), lse.reshape(t, ATT_WIDTH)


def _att_delta(g_att, att, tr=512):
    t = att.shape[0]

    def body(g_ref, a_ref, o_ref):
        prod = g_ref[...] * a_ref[...].astype(F32)
        o_ref[...] = _dot_split(prod, _block_ones(ATT_WIDTH, HEAD_DIM), 0, 3)

    return _rowcall(body, "att_delta", t, tr, [_rows(g_att, tr), _rows(att, tr)], [_orow(t, ATT_WIDTH, F32, tr)])[0]


def _att_bwd(proj, g_att, lse, delta, dil, slopes):
    t = proj[0].shape[0]
    lay = _AttLayout(t, dil)
    nsub, nblk, rb, pb, n_pb = lay.nsub, lay.nblk, lay.rb, lay.pb, lay.n_pb
    src, band, qb = lay.qkv(proj)
    aw = ATT_WIDTH // LANE

    def near(i, which):
        return jnp.maximum(i * nsub - 1, 0) if which == "prev" else jnp.minimum((i + 1) * nsub, n_pb - 1)

    def pspec(off, which=None):
        if which:
            return pl.BlockSpec((pb, LANE), lambda hp, i, r: (near(i, which), lay.col(r, band, qb + off + hp)))
        return pl.BlockSpec((rb, LANE), lambda hp, i, r: (i, lay.col(r, band, qb + off + hp)))

    def aspec(which=None):
        if which:
            return pl.BlockSpec((pb, LANE), lambda hp, i, r: (near(i, which), lay.col(r, aw, hp)))
        return pl.BlockSpec((rb, LANE), lambda hp, i, r: (i, lay.col(r, aw, hp)))

    scale = HEAD_DIM ** -0.5

    def body(q_ref, qn_ref, kc_ref, kp_ref, vc_ref, vp_ref, do_ref, don_ref, lse_ref, lsen_ref, dl_ref, dln_ref, sl_ref,
             dq_ref, dk_ref, dv_ref):
        i, r = pl.program_id(1), pl.program_id(2)
        half = _lane_half()

        def tile_grads(q, do, lse_q, dl_q, kk, vv, dist, valid):
            q2, do2 = _stack_heads(q), _stack_heads(do)
            s = lax.dot_general(q2, kk, NT_DIMS, preferred_element_type=F32) + _stacked_bias(sl_ref, dist, valid)
            p = jnp.exp(s - _head_columns(lse_q))
            dp = lax.dot_general(do2, vv, NT_DIMS, preferred_element_type=F32)
            ds16 = (p * (dp - _head_columns(dl_q))).astype(BF16)
            dq = _unstack_heads(jnp.dot(ds16, kk, preferred_element_type=F32)) * scale
            dk = lax.dot_general(ds16, q2, TN_DIMS, preferred_element_type=F32)
            dv = lax.dot_general(p.astype(BF16), do2, TN_DIMS, preferred_element_type=F32)
            return dq, dk, dv

        carry_k = carry_v = None
        for sub in range(nsub):
            rs = _residue_rows(r, lay.stride, sub)
            q = (q_ref[rs, :] * scale).astype(BF16)
            do = do_ref[rs, :].astype(BF16)
            if sub == 0:
                r0 = _residue_rows(r, lay.stride, 0)
                kk = jnp.concatenate([kp_ref[r0, :], kc_ref[rs, :]], axis=0).astype(BF16)
                vv = jnp.concatenate([vp_ref[r0, :], vc_ref[rs, :]], axis=0).astype(BF16)
                first = i == 0
            else:
                ks = _residue_rows(r, lay.stride, sub - 1, 2)
                kk, vv = kc_ref[ks, :].astype(BF16), vc_ref[ks, :].astype(BF16)
                first = jnp.bool_(False)
            dist, valid = _att_scores_mask(dil, first)
            dq, dk2, dv2 = tile_grads(q, do, lse_ref[rs, :], dl_ref[rs, :], kk, vv, dist, valid)
            dq_ref[rs, :] = dq.astype(lay.out_dtype)
            if sub > 0:
                rp = _residue_rows(r, lay.stride, sub - 1)
                dk_ref[rp, :] = (carry_k + dk2[:ATT_BLOCK, :]).astype(lay.out_dtype)
                dv_ref[rp, :] = (carry_v + dv2[:ATT_BLOCK, :]).astype(lay.out_dtype)
            carry_k, carry_v = dk2[ATT_BLOCK:, :], dv2[ATT_BLOCK:, :]
        rl = _residue_rows(r, lay.stride, nsub - 1)
        rn = _residue_rows(r, lay.stride, 0)
        iq = lax.broadcasted_iota(jnp.int32, (ATT_BLOCK, ATT_BLOCK), 0)
        jk = lax.broadcasted_iota(jnp.int32, (ATT_BLOCK, ATT_BLOCK), 1)
        dist_i = ATT_BLOCK + iq - jk
        valid = (dist_i >= 0) & (dist_i <= ATT_BLOCK) & (i < nblk - 1)
        qn = (qn_ref[rn, :] * scale).astype(BF16)
        _, dk1, dv1 = tile_grads(qn, don_ref[rn, :].astype(BF16), lsen_ref[rn, :], dln_ref[rn, :],
                                 kc_ref[rl, :].astype(BF16), vc_ref[rl, :].astype(BF16), (dist_i * dil).astype(F32), valid)
        dk_ref[rl, :] = (carry_k + dk1).astype(lay.out_dtype)
        dv_ref[rl, :] = (carry_v + dv1).astype(lay.out_dtype)

    gv, lv, dlv = lay.act(g_att), lay.act(lse), lay.act(delta)
    dq, dk, dv = pl.pallas_call(
        body, grid=(N_ATT_HEADS // 2, nblk, dil),
        in_specs=[pspec(0), pspec(0, "next"), pspec(6), pspec(6, "prev"), pspec(12), pspec(12, "prev"),
                  aspec(), aspec("next"), aspec(), aspec("next"), aspec(), aspec("next"),
                  pl.BlockSpec((None, 8, LANE), lambda hp, i, r: (hp, 0, 0))],
        out_specs=[aspec(), aspec(), aspec()], out_shape=[S(lay.act_shape(), lay.out_dtype)] * 3,
        name=f"att_bwd_d{dil}", compiler_params=_params(("parallel", "parallel", "arbitrary")),
    )(src, src, src, src, src, src, gv, gv, lv, lv, dlv, dlv, slopes)
    return dq.reshape(t, ATT_WIDTH), dk.reshape(t, ATT_WIDTH), dv.reshape(t, ATT_WIDTH)


def _att_grad_sum(dqs, dks, dvs, g_proj, tr=2048):
    t = dqs[0].shape[0]
    cw = 2 * LANE
    per = ATT_WIDTH // cw
    arrays = list(dqs) + list(dks) + list(dvs)
    n_pat = len(dqs)

    def body(*refs):
        o_ref = refs[-1]
        which = pl.program_id(1) // per
        tot = jnp.zeros((tr, cw), F32)
        for s in range(3):
            part = refs[s * n_pat][...].astype(F32)
            for g in range(1, n_pat):
                part = part + refs[s * n_pat + g][...].astype(F32)
            tot = jnp.where(which == s, part, tot)
        o_ref[...] = tot.astype(BF16)

    in_specs = [pl.BlockSpec((tr, cw), lambda i, c, s=s: (i, jnp.clip(c - per * s, 0, per - 1))) for s in range(3) for _ in range(n_pat)]
    in_specs.append(pl.BlockSpec(memory_space=pl.ANY))
    return pl.pallas_call(
        lambda *refs: body(*refs[:len(arrays)], refs[-1]), grid=(t // tr, 3 * per), in_specs=in_specs,
        out_specs=pl.BlockSpec((tr, cw), lambda i, c: (i, OFF_QKV // cw + c)), out_shape=S(g_proj.shape, g_proj.dtype),
        input_output_aliases={len(arrays): 0}, name="att_grad_sum", compiler_params=_params(("arbitrary", "arbitrary")),
    )(*arrays, g_proj)


def _ssd_common(xs, dtx, cs, cs_t):
    ch = SSM_CHUNK
    row = lax.broadcasted_iota(jnp.int32, (ch, ch), 0)
    col = lax.broadcasted_iota(jnp.int32, (ch, ch), 1)
    cs_last = cs[ch - 1:ch, :]
    return dict(tril=col <= row, row=row, col=col, cs=cs, cs_t=cs_t, cs_last=cs_last,
                e=jnp.exp(cs), w=jnp.exp(cs_last - cs), xd=xs * dtx)


def _dot_split(a, b, split, terms=2):
    ops = [a, b]
    rest = ops[split]
    other = ops[1 - split].astype(BF16)
    out = None
    for _ in range(terms):
        piece = rest.astype(BF16)
        rest = rest - piece.astype(F32)
        part = jnp.dot(other, piece, preferred_element_type=F32) if split == 1 else jnp.dot(piece, other, preferred_element_type=F32)
        out = part if out is None else out + part
    return out


def _decay_col(cs_t, heads_per_group):
    r = lax.broadcasted_iota(jnp.int32, (heads_per_group * SSM_HEAD_DIM, SSM_STATE), 0) // SSM_HEAD_DIM
    out = jnp.zeros((heads_per_group * SSM_HEAD_DIM, SSM_STATE), F32)
    for j in range(heads_per_group):
        out = jnp.where(r == j, jnp.exp(cs_t[j:j + 1, SSM_CHUNK - 1:SSM_CHUNK]), out)
    return out


SSD_GROUPS_PER_STEP = 8


def _ssd_specs(t):
    hg = SSM_HEADS // SSM_GROUPS
    gw = hg * SSM_HEAD_DIM
    nb0 = SSM_INNER // SSM_STATE
    return hg, gw, nb0


def _ssd_group_views(gi, gw, wide, narrow, stacked):
    w = [r.at[:, pl.ds(gi * gw, gw)] for r in wide]
    n = [r.at[:, pl.ds(gi * SSM_STATE, SSM_STATE)] for r in narrow]
    return w, n, [r.at[gi] for r in stacked]


def _ssd_fwd(xa, dtx, csx, cst_g):
    t = xa.shape[0]
    nch = t // SSM_CHUNK
    hg, gw, nb0 = _ssd_specs(t)
    ch = SSM_CHUNK
    gp = SSD_GROUPS_PER_STEP

    def body(xs_ref, b_ref, c_ref, dtx_ref, cs_ref, cst_ref, y_ref, st_ref, h_scr):
        for gi in range(gp):
            (xs_g, dtx_g, cs_g, y_g), (b_g, c_g), (cst_gi, st_g) = _ssd_group_views(
                gi, gw, (xs_ref, dtx_ref, cs_ref, y_ref), (b_ref, c_ref), (cst_ref, st_ref))
            group_body(pl.program_id(0), pl.program_id(1) * gp + gi, xs_g, b_g, c_g, dtx_g, cs_g, cst_gi, y_g, st_g, h_scr)

    def group_body(cc, g, xs_ref, b_ref, c_ref, dtx_ref, cs_ref, cst_ref, y_ref, st_ref, h_scr):
        @pl.when(cc == 0)
        def _():
            h_scr[g] = jnp.zeros((gw, SSM_STATE), F32)

        q = _ssd_common(xs_ref[...].astype(F32), dtx_ref[...], cs_ref[...], cst_ref[...])
        bb, cb = b_ref[...].astype(BF16), c_ref[...].astype(BF16)
        h = h_scr[g]
        st_ref[...] = h
        xd16 = q["xd"].astype(BF16)
        c_both = lax.dot_general(cb, jnp.concatenate([bb, h.astype(BF16)], axis=0), NT_DIMS, preferred_element_type=F32)
        cbm = c_both[:, :SSM_STATE]
        y = c_both[:, SSM_STATE:] * q["e"]
        lane_head = lax.broadcasted_iota(jnp.int32, (ch, gw), 1) // SSM_HEAD_DIM
        gmats, xds = [], []
        for j in range(hg):
            diff = q["cs"][:, j * SSM_HEAD_DIM:j * SSM_HEAD_DIM + 1] - q["cs_t"][j:j + 1, :]
            gmats.append((cbm * jnp.exp(jnp.where(q["tril"], diff, NEG))).astype(BF16))
            xds.append(jnp.where(lane_head == j, xd16, jnp.zeros_like(xd16)))
        y = y + jnp.dot(jnp.concatenate(gmats, axis=1), jnp.concatenate(xds, axis=0), preferred_element_type=F32)
        y_ref[...] = y.astype(BF16)
        s_new = lax.dot_general((q["xd"] * q["w"]).astype(BF16), bb, TN_DIMS, preferred_element_type=F32)
        h_scr[g] = _decay_col(q["cs_t"], hg) * h + s_new

    wide = pl.BlockSpec((ch, gp * gw), lambda cc, g: (cc, g))
    return pl.pallas_call(
        body, grid=(nch, SSM_GROUPS // gp),
        in_specs=[wide,
                  pl.BlockSpec((ch, gp * SSM_STATE), lambda cc, g: (cc, nb0 // gp + g)),
                  pl.BlockSpec((ch, gp * SSM_STATE), lambda cc, g: (cc, (nb0 + SSM_GROUPS) // gp + g)),
                  wide, wide,
                  pl.BlockSpec((gp, 8, ch), lambda cc, g: (g, 0, cc))],
        out_specs=[wide, pl.BlockSpec((None, gp, gw, SSM_STATE), lambda cc, g: (cc, g, 0, 0))],
        out_shape=[S((t, SSM_INNER), BF16), S((nch, SSM_GROUPS, gw, SSM_STATE), F32)],
        scratch_shapes=[pltpu.VMEM((SSM_GROUPS, gw, SSM_STATE), F32)],
        name="ssd_fwd", compiler_params=_params(("arbitrary", "arbitrary")),
    )(xa, xa, xa, dtx, csx, cst_g)


def _ssd_bwd(xa, dtx, csx, cst_g, alog_x, g_y, states, dskip_x):
    t = xa.shape[0]
    nch = t // SSM_CHUNK
    hg, gw, nb0 = _ssd_specs(t)
    ch = SSM_CHUNK
    gp = SSD_GROUPS_PER_STEP

    def rc(cc):
        return nch - 1 - cc

    def body(xs_ref, b_ref, c_ref, dtx_ref, cs_ref, cst_ref, alx_ref, gy_ref, st_ref, dsk_ref,
             gxs_ref, gb_ref, gc_ref, gdt_ref, ga_ref, gh_scr):
        for gi in range(gp):
            wide, narrow, stacked = _ssd_group_views(
                gi, gw, (xs_ref, dtx_ref, cs_ref, alx_ref, gy_ref, dsk_ref, gxs_ref, gdt_ref, ga_ref), (b_ref, c_ref, gb_ref, gc_ref),
                (cst_ref, st_ref))
            xs_g, dtx_g, cs_g, alx_g, gy_g, dsk_g, gxs_g, gdt_g, ga_g = wide
            b_g, c_g, gb_g, gc_g = narrow
            group_body(pl.program_id(0), pl.program_id(1) * gp + gi, xs_g, b_g, c_g, dtx_g, cs_g, stacked[0], alx_g, gy_g, stacked[1],
                       dsk_g, gxs_g, gb_g, gc_g, gdt_g, ga_g, gh_scr)

    def group_body(cc, g, xs_ref, b_ref, c_ref, dtx_ref, cs_ref, cst_ref, alx_ref, gy_ref, st_ref, dsk_ref,
                   gxs_ref, gb_ref, gc_ref, gdt_ref, ga_ref, gh_scr):
        @pl.when(cc == 0)
        def _():
            gh_scr[g] = jnp.zeros((gw, SSM_STATE), F32)

        xs, dtx = xs_ref[...].astype(F32), dtx_ref[...]
        q = _ssd_common(xs, dtx, cs_ref[...], cst_ref[...])
        cs, cs_t, e, w, xd = q["cs"], q["cs_t"], q["e"], q["w"], q["xd"]
        bb, cb = b_ref[...].astype(BF16), c_ref[...].astype(BF16)
        gy16 = gy_ref[...]
        gy = gy16.astype(F32)
        xd16 = xd.astype(BF16)
        h = st_ref[...]
        h16 = h.astype(BF16)
        ghn = gh_scr[g]
        ghn16 = ghn.astype(BF16)
        seg = _block_ones(gw, SSM_HEAD_DIM)
        c_both = lax.dot_general(cb, jnp.concatenate([bb, h16], axis=0), NT_DIMS, preferred_element_type=F32)
        cbm, chm = c_both[:, :SSM_STATE], c_both[:, SSM_STATE:]

        gye16 = (gy * e).astype(BF16)
        g_c = jnp.dot(gye16, h16, preferred_element_type=F32)
        gh_off = lax.dot_general(gye16, cb, TN_DIMS, preferred_element_type=F32)
        bgs = lax.dot_general(bb, ghn16, NT_DIMS, preferred_element_type=F32)
        g_xd = w * bgs
        head_sums = _dot_split(jnp.concatenate([gy * chm, xd * bgs], axis=0), seg, 0)
        g_e, g_w = head_sums[:ch], head_sums[ch:]
        g_b = jnp.dot((xd * w).astype(BF16), ghn16, preferred_element_type=F32)
        decay = _decay_col(cs_t, hg)
        gh_scr[g] = decay * ghn + gh_off
        rsum = jnp.sum(ghn * h, axis=1, keepdims=True)
        lane_head = lax.broadcasted_iota(jnp.int32, (ch, gw), 1) // SSM_HEAD_DIM
        lane_head1 = lax.broadcasted_iota(jnp.int32, (1, gw), 1) // SSM_HEAD_DIM
        g_el = jnp.zeros((1, gw), F32)
        g_cs = g_e * e - g_w * w
        upper = q["row"] <= q["col"]
        lms, gys = [], []
        for j in range(hg):
            g_el = jnp.where(lane_head1 == j, jnp.sum(rsum[j * SSM_HEAD_DIM:(j + 1) * SSM_HEAD_DIM, :], axis=0, keepdims=True), g_el)
            csc = cs[:, j * SSM_HEAD_DIM:j * SSM_HEAD_DIM + 1]
            csr = cs_t[j:j + 1, :]
            lms.append(jnp.exp(jnp.where(q["tril"], csc - csr, NEG)))
            gys.append(jnp.where(lane_head == j, gy16, jnp.zeros_like(gy16)))
        lm_st, gy_st = jnp.concatenate(lms, axis=0), jnp.concatenate(gys, axis=0)
        cbm_st = jnp.concatenate([cbm] * hg, axis=0)
        gcb_st = lax.dot_general(gy_st, xd16, NT_DIMS, preferred_element_type=F32) * lm_st
        gcb_sum = gcb_st[0:ch]
        for j in range(1, hg):
            gcb_sum = gcb_sum + gcb_st[j * ch:(j + 1) * ch]
        gcb16 = gcb_sum.astype(BF16)
        g_c = g_c + jnp.dot(gcb16, bb, preferred_element_type=F32)
        g_b = g_b + lax.dot_general(gcb16, cb, TN_DIMS, preferred_element_type=F32)
        g_xd = g_xd + lax.dot_general((cbm_st * lm_st).astype(BF16), gy_st, TN_DIMS, preferred_element_type=F32)
        m_st = gcb_st * cbm_st
        for j in range(hg):
            m_ls = m_st[j * ch:(j + 1) * ch]
            d_cs = jnp.sum(m_ls, axis=1, keepdims=True) - jnp.sum(m_ls.T, axis=1, keepdims=True)
            g_cs = g_cs + jnp.where(lane_head == j, d_cs, 0.0)
        extra = _colsum(g_w * w) + g_el * jnp.exp(q["cs_last"])
        g_cs = g_cs + jnp.where(lax.broadcasted_iota(jnp.int32, (ch, gw), 0) == ch - 1, extra, 0.0)
        g_la = _dot_split(upper, g_cs, 1)
        a_x = -jnp.exp(alx_ref[...])
        gdt_ref[...] = g_xd * xs + g_la * a_x * (1.0 / SSM_HEAD_DIM)
        ga_row = _colsum(g_la * (dtx * a_x)) * (1.0 / SSM_HEAD_DIM)
        ga_ref[...] = jnp.where(lax.broadcasted_iota(jnp.int32, (8, gw), 0) == 0, ga_row, 0.0)
        gxs_ref[...] = (g_xd * dtx + gy * dsk_ref[...]).astype(BF16)
        gb_ref[...] = g_b.astype(BF16)
        gc_ref[...] = g_c.astype(BF16)

    wide = pl.BlockSpec((ch, gp * gw), lambda cc, g: (rc(cc), g))
    narrow = pl.BlockSpec((ch, gp * SSM_STATE), lambda cc, g: (rc(cc), g))
    row = pl.BlockSpec((1, gp * gw), lambda cc, g: (0, g))
    return pl.pallas_call(
        body, grid=(nch, SSM_GROUPS // gp),
        in_specs=[wide,
                  pl.BlockSpec((ch, gp * SSM_STATE), lambda cc, g: (rc(cc), nb0 // gp + g)),
                  pl.BlockSpec((ch, gp * SSM_STATE), lambda cc, g: (rc(cc), (nb0 + SSM_GROUPS) // gp + g)),
                  wide, wide,
                  pl.BlockSpec((gp, 8, ch), lambda cc, g: (g, 0, rc(cc))),
                  row, wide,
                  pl.BlockSpec((None, gp, gw, SSM_STATE), lambda cc, g: (rc(cc), g, 0, 0)),
                  row],
        out_specs=[wide, narrow, narrow, wide, pl.BlockSpec((8, gp * gw), lambda cc, g: (rc(cc), g))],
        out_shape=[S((t, SSM_INNER), BF16), S((t, SSM_GROUPS * SSM_STATE), BF16), S((t, SSM_GROUPS * SSM_STATE), BF16),
                   S((t, SSM_INNER), F32), S((nch * 8, SSM_INNER), F32)],
        scratch_shapes=[pltpu.VMEM((SSM_GROUPS, gw, SSM_STATE), F32)],
        name="ssd_bwd", compiler_params=_params(("arbitrary", "arbitrary")),
    )(xa, xa, xa, dtx, csx, cst_g, alog_x, g_y, states, dskip_x)


def _local_step(x, target, w_pre, w_in_r, b_gate, conv_w, conv_b, dt_bias, a_log, d_skip, ssm_norm_w,
                late_weights, w_post, w_fpre, w_fpost, on_mid_grads, on_in_proj_grads):
    t = x.shape[0]
    mm = functools.partial(_matmul, tm=512)
    slopes = _slope_rows()
    hg = SSM_HEADS // SSM_GROUPS
    dt_bias_pad = jnp.pad(dt_bias, ((0, 0), (0, LANE - SSM_HEADS)))
    alog_x = jnp.repeat(a_log, SSM_HEAD_DIM, axis=1)
    alog_pad = jnp.pad(a_log, ((0, 0), (0, LANE - SSM_HEADS)))
    dskip_x = jnp.repeat(d_skip, SSM_HEAD_DIM, axis=1)

    u = _pre_norm(x, w_pre)
    pa = _matmul(u, w_in_r, mode="nn", out_dtype=BF16, name="in_proj_zgx", tm=1024, tn=2048, tk=D_MODEL, b_cols=(0, PA_W))
    pb, pb16 = _matmul(u, w_in_r[:, PA_W:], mode="nn", out_dtype=F32, name="in_proj_qkvdt", tm=1024, tn=PB_W // 2, tk=D_MODEL,
                       epilogue="also_bf16")
    dils = [dil for _, dil in DILATED_PATTERNS]
    wide = [_att_fwd((pb, pb16), dil, slopes) for dil in dils[1:]]
    att, lse = _att_fwd((pb, pb16), dils[0], slopes, others=wide)
    xa, xc = _conv_fwd(pa, conv_w, conv_b)
    dtx, csx, cst = _dt_fwd(pb, dt_bias_pad, alog_pad)
    cst_g = jnp.pad(cst[:SSM_HEADS].reshape(SSM_GROUPS, hg, t), ((0, 0), (0, 8 - hg), (0, 0)))
    y_ssd, states = _ssd_fwd(xa, dtx, csx, cst_g)
    y4 = _gate_norm_fwd(y_ssd, xa, pa, dskip_x, ssm_norm_w)
    w_att, w_ssm, w_out, w_up, w_down = late_weights(y4)
    att_p = mm(att, w_att, mode="nn", out_dtype=BF16, name="att_proj", tn=D_MODEL, tk=ATT_WIDTH)
    ssm_p = mm(y4, w_ssm, mode="nn", out_dtype=BF16, name="ssm_proj", tn=D_MODEL, tk=SSM_INNER)
    mixin = _gating_fwd(pa, b_gate, att_p, ssm_p)
    mixed = mm(mixin, w_out, mode="nn", out_dtype=F32, name="out_proj", tn=D_MODEL, tk=D_MODEL)
    h1, f = _mix_post_ffn_pre(x, mixed, w_post, w_fpre)
    act, up = _matmul(f, w_up, mode="nn", out_dtype=BF16, name="ffn_up", tm=2048, tn=FFN_HIDDEN // N_DEV, tk=D_MODEL, epilogue="relu2", stacked=True)
    dn = mm(act, w_down, mode="nn", out_dtype=F32, name="ffn_down", tn=D_MODEL, tk=FFN_HIDDEN)
    loss, g_h2, g_dn, gw_fpost = _loss_and_ffn_post_bwd(h1, dn, w_fpost, target)

    g_up = _matmul(g_dn, w_down, mode="nt", out_dtype=BF16, name="ffn_down_bwd_x", tm=1024, tn=2048, tk=D_MODEL, epilogue="relu2_bwd",
                   extra=up)
    gw_down = _matmul(act, g_dn, mode="tn", out_dtype=BF16, name="ffn_down_bwd_w", tm=1024, tn=D_MODEL, tk=2048)
    w_up_rows = jnp.moveaxis(w_up, 0, 1).reshape(D_MODEL, FFN_HIDDEN)
    g_f = mm(g_up, w_up_rows, mode="nt", out_dtype=F32, name="ffn_up_bwd_x", tn=D_MODEL, tk=FFN_HIDDEN)
    gw_up = _matmul(f, g_up, mode="tn", out_dtype=BF16, name="ffn_up_bwd_w", tm=D_MODEL, tn=FFN_HIDDEN // N_DEV, tk=2048, stacked=True)
    g_h1, g_mixed, gw_fpre, gw_post = _ffn_pre_mix_post_bwd(g_h2, g_f, h1, w_fpre, mixed, w_post)
    g_mixin = mm(g_mixed, w_out, mode="nt", out_dtype=BF16, name="out_proj_bwd_x", tn=D_MODEL, tk=D_MODEL)
    gw_out = _matmul(mixin, g_mixed, mode="tn", out_dtype=BF16, name="out_proj_bwd_w", tm=D_MODEL, tn=D_MODEL, tk=2048)
    g_proj = lax.empty((t, PROJ_W), BF16)
    g_att_p, g_ssm_p, g_b_gate, g_proj = _gating_bwd(g_mixin, pa, b_gate, att_p, ssm_p, g_proj)
    g_att = mm(g_att_p, w_att, mode="nt", out_dtype=F32, name="att_proj_bwd_x", tn=ATT_WIDTH, tk=D_MODEL)
    gw_att = _matmul(att, g_att_p, mode="tn", out_dtype=BF16, name="att_proj_bwd_w", tm=ATT_WIDTH, tn=D_MODEL, tk=2048)
    g_y4 = mm(g_ssm_p, w_ssm, mode="nt", out_dtype=BF16, name="ssm_proj_bwd_x", tn=SSM_INNER, tk=D_MODEL)
    gw_ssm = _matmul(y4, g_ssm_p, mode="tn", out_dtype=BF16, name="ssm_proj_bwd_w", tm=1024, tn=D_MODEL, tk=2048)
    token = on_mid_grads(dict(w_att_proj=gw_att, w_ssm_proj=gw_ssm, w_out=gw_out, w_up=gw_up, w_down=gw_down))
    if token is not None:
        ssm_norm_w = ssm_norm_w + jnp.tile(token[0:1, :], (1, SSM_INNER // LANE))
    g_y2, g_norm_w, _, g_d_skip, g_proj = _gate_norm_bwd(g_y4, y_ssd, xa, pa, dskip_x, ssm_norm_w, g_proj)
    g_xs, g_bm, g_cm, g_dtx, ga_rows = _ssd_bwd(xa, dtx, csx, cst_g, alog_x, g_y2, states, dskip_x)
    g_dt_bias, g_a_log, g_proj = _dt_bwd(g_dtx, ga_rows, pb, dt_bias_pad, g_proj)
    g_conv_b, gcw0, gcw1, gcw2, gcw3, g_proj = _conv_bwd(g_xs, g_bm, g_cm, xc, pa, conv_w, g_proj)
    delta = _att_delta(g_att, att)
    dqs, dks, dvs = [], [], []
    for _, dil in DILATED_PATTERNS:
        dq, dk, dv = _att_bwd((pb, pb16), g_att, lse, delta, dil, slopes)
        dqs.append(dq)
        dks.append(dk)
        dvs.append(dv)
    g_proj = _att_grad_sum(dqs, dks, dvs, g_proj)
    gw_in_r = _matmul(u, g_proj, mode="tn", out_dtype=BF16, name="in_proj_bwd_w", tm=D_MODEL, tn=1792, tk=2048)
    token = on_in_proj_grads(gw_in_r, jnp.concatenate([gcw0, gcw1, gcw2, gcw3], axis=0))
    g_u = _matmul(g_proj, w_in_r, mode="nt", out_dtype=F32, name="in_proj_bwd_x", tm=1024, tn=D_MODEL, tk=3584, after=token)
    g_x, gw_pre = _pre_norm_bwd(g_h1, g_u, x, w_pre)

    grads = dict(
        norm_mix_pre_w=gw_pre, b_gate=g_b_gate, conv_b=g_conv_b, dt_bias=g_dt_bias[:, :SSM_HEADS], a_log=g_a_log[:, :SSM_HEADS],
        d_skip=g_d_skip[:, :SSM_HEADS], ssm_norm_w=g_norm_w, norm_mix_post_w=gw_post, norm_ffn_pre_w=gw_fpre, norm_ffn_post_w=gw_fpost)
    return loss, g_x, grads


def _mesh_pos():
    return lax.axis_index("x"), lax.axis_index("y"), lax.axis_index("c")


def _all_gather(shards):
    n = len(shards)

    def body(*refs):
        x_refs, o_refs = refs[:n], refs[n:2 * n]
        send_sems, recv_sems, local_sems = refs[2 * n:]
        x, y, c = _mesh_pos()
        me, sibling = (x, y, c), (x, y, 1 - c)
        chips = [(1 - x, y), (x, 1 - y), (1 - x, 1 - y)]

        def copy(a, k, block, to, src=None):
            dst = o_refs[a].at[4 * block[0] + 2 * block[1] + block[2]]
            return pltpu.make_async_remote_copy(
                src_ref=dst if src is None else src, dst_ref=dst, send_sem=send_sems.at[7 * a + k], recv_sem=recv_sems.at[7 * a + k],
                device_id=to, device_id_type=pl.DeviceIdType.MESH)

        mine = [pltpu.make_async_copy(x_refs[a], o_refs[a].at[4 * x + 2 * y + c], local_sems.at[a]) for a in range(n)]
        for cp in mine:
            cp.start()
        first = []
        for a in range(n):
            first.append(copy(a, 0, me, sibling, src=x_refs[a]))
            first += [copy(a, 1 + j, me, (*chip, c), src=x_refs[a]) for j, chip in enumerate(chips)]
        for cp in first:
            cp.start()
        passed = []
        for j, chip in enumerate(chips):
            for a in range(n):
                copy(a, 1 + j, (*chip, c), me).wait_recv()
                passed.append(copy(a, 4 + j, (*chip, c), sibling))
                passed[-1].start()
        for a in range(n):
            copy(a, 0, sibling, me).wait_recv()
            for j, chip in enumerate(chips):
                copy(a, 4 + j, (*chip, 1 - c), me).wait_recv()
        for cp in first + passed:
            cp.wait_send()
        for cp in mine:
            cp.wait()

    hbm = pl.BlockSpec(memory_space=pltpu.HBM)
    return pl.pallas_call(
        body, out_shape=[S((N_DEV,) + s.shape, s.dtype) for s in shards],
        in_specs=[hbm] * n, out_specs=[hbm] * n,
        scratch_shapes=[pltpu.SemaphoreType.DMA((7 * n,)), pltpu.SemaphoreType.DMA((7 * n,)), pltpu.SemaphoreType.DMA((n,))],
        name="weights_all_gather",
    )(*shards)


def _exchange_grads(slab_arrays, small):
    n = len(slab_arrays)
    r_small = small.shape[0]

    def body(*refs):
        slab_refs, small_ref = refs[:n], refs[n]
        recv_refs, gsm_ref = refs[n + 1:2 * n + 1], refs[2 * n + 1]
        send_sems, recv_sems, local_sems = refs[2 * n + 2:]
        x, y, c = _mesh_pos()
        me = 4 * x + 2 * y + c

        def peer(k):
            px = 1 - x if k & 4 else x
            py = 1 - y if k & 2 else y
            pc = 1 - c if k & 1 else c
            return (px, py, pc), 4 * px + 2 * py + pc

        def copy(a, k, sending):
            to, lin = peer(k)
            sem = 7 * a + k - 1
            if a == n:
                src, dst = small_ref, gsm_ref.at[me if sending else lin]
            else:
                src, dst = slab_refs[a].at[lin], recv_refs[a].at[me if sending else lin]
            return pltpu.make_async_remote_copy(src_ref=src, dst_ref=dst, send_sem=send_sems.at[sem], recv_sem=recv_sems.at[sem],
                                                device_id=to, device_id_type=pl.DeviceIdType.MESH)

        own = [pltpu.make_async_copy(slab_refs[a].at[me], recv_refs[a].at[me], local_sems.at[a]) for a in range(n)]
        own.append(pltpu.make_async_copy(small_ref, gsm_ref.at[me], local_sems.at[n]))
        for cp in own:
            cp.start()
        order = [n] + list(range(n))
        sends = [copy(a, k, True) for a in order for k in range(1, N_DEV)]
        for cp in sends:
            cp.start()
        for a in order:
            for k in range(1, N_DEV):
                copy(a, k, False).wait_recv()
        for cp in sends:
            cp.wait_send()
        for cp in own:
            cp.wait()

    hbm = pl.BlockSpec(memory_space=pltpu.HBM)
    n_sem = 7 * (n + 1)
    res = pl.pallas_call(
        body, out_shape=[S(a.shape, a.dtype) for a in slab_arrays] + [S((N_DEV, r_small, LANE), small.dtype)],
        in_specs=[hbm] * (n + 1), out_specs=[hbm] * (n + 1),
        scratch_shapes=[pltpu.SemaphoreType.DMA((n_sem,)), pltpu.SemaphoreType.DMA((n_sem,)), pltpu.SemaphoreType.DMA((n + 1,))],
        name="grad_exchange",
    )(*slab_arrays, small)
    return res[:n], res[n]


def _peer_of(k, x, y, c):
    px = 1 - x if k & 4 else x
    py = 1 - y if k & 2 else y
    pc = 1 - c if k & 1 else c
    return (px, py, pc), 4 * px + 2 * py + pc


def _split_copies(src_refs, land_refs, send_sems, recv_sems, per_peer):
    x, y, c = _mesh_pos()
    me = 4 * x + 2 * y + c
    sends, recvs = [], []
    for a, (src, land) in enumerate(zip(src_refs, land_refs)):
        for k in range(1, N_DEV):
            to, lin = _peer_of(k, x, y, c)
            sem = 7 * a + k - 1
            piece = src.at[lin] if per_peer else src
            for slot, out in ((me, sends), (lin, recvs)):
                out.append(pltpu.make_async_remote_copy(
                    src_ref=piece, dst_ref=land.at[slot], send_sem=send_sems.at[sem], recv_sem=recv_sems.at[sem],
                    device_id=to, device_id_type=pl.DeviceIdType.MESH))
    return sends, recvs


def _remote_start(srcs, per_peer, name):
    n = len(srcs)
    lands = [lax.empty((N_DEV,) + (s.shape[1:] if per_peer else s.shape), s.dtype) for s in srcs]

    def body(*refs):
        src_refs, land_refs = refs[:n], refs[n:2 * n]
        send_sems, recv_sems = refs[2 * n], refs[2 * n + 1]
        token = refs[-1]
        sends, _ = _split_copies(src_refs, land_refs, send_sems, recv_sems, per_peer)
        for cp in sends:
            cp.start()
        token[...] = jnp.zeros_like(token)

    hbm = pl.BlockSpec(memory_space=pltpu.HBM)
    sem = pl.BlockSpec(memory_space=pltpu.SEMAPHORE)
    res = pl.pallas_call(
        body, name=name,
        out_shape=(pltpu.SemaphoreType.DMA((7 * n,)), pltpu.SemaphoreType.DMA((7 * n,)),
                   *[pltpu.HBM(a.shape, a.dtype) for a in srcs + lands], S((8, LANE), F32)),
        in_specs=[hbm] * (2 * n), out_specs=(sem, sem, *[hbm] * (2 * n), pl.BlockSpec(memory_space=pltpu.VMEM)),
        input_output_aliases={i: 2 + i for i in range(2 * n)},
        compiler_params=pltpu.CompilerParams(has_side_effects=pltpu.SideEffectType.DATAFLOW_SIDE_EFFECTING),
    )(*[pltpu.with_memory_space_constraint(a, pltpu.HBM) for a in srcs + lands])
    return dict(sems=res[:2], srcs=list(res[2:2 + n]), lands=list(res[2 + n:2 + 2 * n]), per_peer=per_peer), res[-1]


def _remote_wait(handle, after, name):
    n = len(handle["srcs"])
    per_peer = handle["per_peer"]

    def body(*refs):
        src_refs, land_refs = refs[:n], refs[n:2 * n]
        send_sems, recv_sems = refs[2 * n], refs[2 * n + 1]
        sends, recvs = _split_copies(src_refs, land_refs, send_sems, recv_sems, per_peer)
        for cp in sends:
            cp.wait_send()
        for cp in recvs:
            cp.wait_recv()

    hbm = pl.BlockSpec(memory_space=pltpu.HBM)
    sem = pl.BlockSpec(memory_space=pltpu.SEMAPHORE)
    arrays = handle["srcs"] + handle["lands"]
    res = pl.pallas_call(
        body, name=name, out_shape=tuple(pltpu.HBM(a.shape, a.dtype) for a in arrays),
        in_specs=[hbm] * (2 * n) + [sem, sem, pl.BlockSpec(memory_space=pl.ANY)], out_specs=tuple([hbm] * (2 * n)),
        input_output_aliases={i: i for i in range(2 * n)},
        compiler_params=pltpu.CompilerParams(has_side_effects=pltpu.SideEffectType.DATAFLOW_SIDE_EFFECTING),
    )(*arrays, *handle["sems"], after)
    return list(res[n:])


def _with_own(lands, own, me):
    return [lax.dynamic_update_index_in_dim(land, o.astype(land.dtype), me, 0) for land, o in zip(lands, own)]


def _adamw(w, m, v, slabs, name, tr):
    r, cols = w.shape
    c1 = 1.0 - ADAM_B1 ** ADAM_STEP
    c2 = 1.0 - ADAM_B2 ** ADAM_STEP

    def body(w_ref, m_ref, v_ref, s_ref, g_ref, d_ref, nm_ref, nv_ref):
        g = s_ref[0].astype(F32)
        for d in range(1, N_DEV):
            g = g + s_ref[d].astype(F32)
        nm = ADAM_B1 * m_ref[...] + (1.0 - ADAM_B1) * g
        nv = ADAM_B2 * v_ref[...] + (1.0 - ADAM_B2) * (g * g)
        g_ref[...] = g
        nm_ref[...] = nm
        nv_ref[...] = nv
        d_ref[...] = -ADAM_LR * ((nm / c1) / (jnp.sqrt(nv / c2) + ADAM_EPS) + ADAM_WD * w_ref[...])

    assert r % tr == 0, name
    blk = pl.BlockSpec((tr, cols), lambda i: (i, 0))
    return pl.pallas_call(
        body, grid=(r // tr,), in_specs=[blk, blk, blk, pl.BlockSpec((N_DEV, tr, cols), lambda i: (0, i, 0))],
        out_specs=[blk] * 4, out_shape=[S((r, cols), F32)] * 4, name=name, compiler_params=_params(("parallel",)),
    )(w, m, v, slabs)


BIG = ("w_in", "w_att_proj", "w_up", "w_ssm_proj", "w_out", "w_down", "conv_w")
ADAMW_ROWS = dict(w_in=256, w_att_proj=768, w_up=512, w_ssm_proj=256, w_out=128, w_down=256, conv_w=4)
SMALL = ("norm_mix_pre_w", "b_gate", "conv_b", "dt_bias", "a_log", "d_skip", "ssm_norm_w", "norm_mix_post_w",
         "norm_ffn_pre_w", "norm_ffn_post_w")
ORDER = ("norm_mix_pre_w", "w_in", "b_gate", "conv_w", "conv_b", "dt_bias", "a_log", "d_skip", "ssm_norm_w", "w_att_proj",
         "w_ssm_proj", "w_out", "norm_mix_post_w", "norm_ffn_pre_w", "w_up", "w_down", "norm_ffn_post_w")
ROW_SHARDED = ("w_ssm_proj", "w_out", "w_down")
LATE = ("w_att_proj", "w_ssm_proj", "w_out", "w_up", "w_down")
IN_PROJ_W = 10528
IN_SHARD_W = IN_PROJ_W // N_DEV
IN_SEGMENTS = ((2304, 4352), (8480, 10528), (4352, 8448), (0, 2304), (8448, 8480))


def _pack(parts, rows_multiple):
    flat = jnp.concatenate([p.reshape(-1) for p in parts])
    pad = (-flat.shape[0]) % (rows_multiple * LANE)
    return jnp.pad(flat, (0, pad)).reshape(-1, LANE)


def _unpack(flat2d, shapes):
    flat, out, off = flat2d.reshape(-1), [], 0
    for sh in shapes:
        n = int(np.prod(sh))
        out.append(flat[off:off + n].reshape(sh))
        off += n
    return out


def _reorder_in_proj(w):
    qkv, z, xbc = w[:, :2304], w[:, 2304:4352], w[:, 4352:8448]
    dt, gate = w[:, 8448:8480], w[:, 8480:10528]
    return jnp.concatenate([z, gate, xbc, qkv, dt, jnp.zeros((w.shape[0], PROJ_W - 10528), w.dtype)], axis=1)


def _restore_in_proj(wr):
    return jnp.concatenate([wr[:, OFF_QKV:OFF_QKV + 2304], wr[:, OFF_Z:OFF_Z + 2048], wr[:, OFF_XBC:OFF_XBC + 4096],
                            wr[:, OFF_DT:OFF_DT + 32], wr[:, OFF_GL:OFF_GL + 2048]], axis=1)


def _assemble_in_proj(g):
    pieces = []
    for lo, hi in IN_SEGMENTS:
        while lo < hi:
            d = lo // IN_SHARD_W
            end = min(hi, (d + 1) * IN_SHARD_W)
            pieces.append(g[d][:, lo - d * IN_SHARD_W:end - d * IN_SHARD_W])
            lo = end
    pieces.append(jnp.zeros((g.shape[1], PROJ_W - IN_PROJ_W), g.dtype))
    return jnp.concatenate(pieces, axis=1)


def _in_proj_slabs(wr):
    orig = _restore_in_proj(wr)
    return jnp.stack([orig[:, d * IN_SHARD_W:(d + 1) * IN_SHARD_W] for d in range(N_DEV)])


def kernel(x, norm_mix_pre_w, w_in, b_gate, conv_w, conv_b, dt_bias, a_log, d_skip, ssm_norm_w, w_att_proj, w_ssm_proj, w_out, norm_mix_post_w, norm_ffn_pre_w, w_up, w_down, norm_ffn_post_w, loss_target, m_norm_mix_pre_w, m_w_in, m_b_gate, m_conv_w, m_conv_b, m_dt_bias, m_a_log, m_d_skip, m_ssm_norm_w, m_w_att_proj, m_w_ssm_proj, m_w_out, m_norm_mix_post_w, m_norm_ffn_pre_w, m_w_up, m_w_down, m_norm_ffn_post_w, v_norm_mix_pre_w, v_w_in, v_b_gate, v_conv_w, v_conv_b, v_dt_bias, v_a_log, v_d_skip, v_ssm_norm_w, v_w_att_proj, v_w_ssm_proj, v_w_out, v_norm_mix_post_w, v_norm_ffn_pre_w, v_w_up, v_w_down, v_norm_ffn_post_w):
    w = dict(norm_mix_pre_w=norm_mix_pre_w, w_in=w_in, b_gate=b_gate, conv_w=conv_w, conv_b=conv_b, dt_bias=dt_bias, a_log=a_log,
             d_skip=d_skip, ssm_norm_w=ssm_norm_w, w_att_proj=w_att_proj, w_ssm_proj=w_ssm_proj, w_out=w_out,
             norm_mix_post_w=norm_mix_post_w, norm_ffn_pre_w=norm_ffn_pre_w, w_up=w_up, w_down=w_down, norm_ffn_post_w=norm_ffn_post_w)
    m = dict(norm_mix_pre_w=m_norm_mix_pre_w, w_in=m_w_in, b_gate=m_b_gate, conv_w=m_conv_w, conv_b=m_conv_b, dt_bias=m_dt_bias,
             a_log=m_a_log, d_skip=m_d_skip, ssm_norm_w=m_ssm_norm_w, w_att_proj=m_w_att_proj, w_ssm_proj=m_w_ssm_proj, w_out=m_w_out,
             norm_mix_post_w=m_norm_mix_post_w, norm_ffn_pre_w=m_norm_ffn_pre_w, w_up=m_w_up, w_down=m_w_down, norm_ffn_post_w=m_norm_ffn_post_w)
    v = dict(norm_mix_pre_w=v_norm_mix_pre_w, w_in=v_w_in, b_gate=v_b_gate, conv_w=v_conv_w, conv_b=v_conv_b, dt_bias=v_dt_bias,
             a_log=v_a_log, d_skip=v_d_skip, ssm_norm_w=v_ssm_norm_w, w_att_proj=v_w_att_proj, w_ssm_proj=v_w_ssm_proj, w_out=v_w_out,
             norm_mix_post_w=v_norm_mix_post_w, norm_ffn_pre_w=v_norm_ffn_pre_w, w_up=v_w_up, w_down=v_w_down, norm_ffn_post_w=v_norm_ffn_post_w)
    shard_shapes = {n: w[n].shape[1:] for n in ORDER}

    mx, my, mc = _mesh_pos()
    me = 4 * mx + 2 * my + mc

    g_in, g_conv = _all_gather([w["w_in"][0].astype(BF16), w["conv_w"][0]])
    conv_full = jnp.moveaxis(g_conv, 0, 1).reshape(SSM_CONV, CONV_DIM)
    late_shards = [w[n][0].astype(BF16) for n in LATE]
    late_handle, token = _remote_start(late_shards, False, "late_weights_start")
    w_pre = w["norm_mix_pre_w"] + jnp.tile(token[0:1, :], (1, D_MODEL // LANE))

    def late_weights(after):
        full = dict(zip(LATE, _with_own(_remote_wait(late_handle, after, "late_weights_wait"), late_shards, me)))
        for n in ROW_SHARDED:
            full[n] = full[n].reshape(-1, full[n].shape[2])
        w_att = jnp.moveaxis(full["w_att_proj"], 0, 1).reshape(ATT_WIDTH, D_MODEL)
        return w_att, full["w_ssm_proj"], full["w_out"], full["w_up"], full["w_down"]

    started = {}

    def start_exchange(tag, slabs):
        own = [lax.dynamic_index_in_dim(s, me, 0, keepdims=False) for s in slabs]
        handle, tok = _remote_start(slabs, True, tag + "_grads_start")
        started[tag] = (handle, own)
        return tok

    def on_mid_grads(g):
        slabs = dict(w_up=g["w_up"], w_att_proj=jnp.moveaxis(g["w_att_proj"].reshape(ATT_WIDTH, N_DEV, -1), 1, 0))
        for n in ROW_SHARDED:
            slabs[n] = g[n].reshape(N_DEV, -1, g[n].shape[1])
        return start_exchange("mid", [slabs[n] for n in LATE])

    def on_in_proj_grads(gw_in_r, g_conv_w):
        return start_exchange("in_proj", [_in_proj_slabs(gw_in_r), jnp.moveaxis(g_conv_w.reshape(SSM_CONV, N_DEV, -1), 1, 0)])

    loss, g_x, grads = _local_step(
        x[0], loss_target[0], w_pre, _assemble_in_proj(g_in), w["b_gate"], conv_full, w["conv_b"], w["dt_bias"], w["a_log"],
        w["d_skip"], w["ssm_norm_w"], late_weights, w["norm_mix_post_w"], w["norm_ffn_pre_w"], w["norm_ffn_post_w"],
        on_mid_grads, on_in_proj_grads)

    recv = {}
    for tag, names in (("mid", LATE), ("in_proj", ("w_in", "conv_w"))):
        handle, own = started[tag]
        recv.update(zip(names, _with_own(_remote_wait(handle, g_x, tag + "_grads_wait"), own, me)))
    small = _pack([grads[n].astype(F32) for n in SMALL], 8)
    _, small_all = _exchange_grads([], small)

    small_shapes = [shard_shapes[n] for n in SMALL]
    small_out = _adamw(*[_pack([d_[n][0] for n in SMALL], 8) for d_ in (w, m, v)], small_all, "adamw_replicated", small_all.shape[1])
    big_out = {n: _adamw(w[n][0], m[n][0], v[n][0], recv[n], "adamw_" + n, ADAMW_ROWS[n]) for n in BIG}
    res = []
    for which, small_flat in enumerate(small_out):
        vals = {n: big_out[n][which] for n in BIG}
        vals.update(zip(SMALL, _unpack(small_flat, small_shapes)))
        res.append([vals[n][None] for n in ORDER])
    g_out, d_out, m_out, v_out = res
    total = lax.psum(loss[0, 0], ("x", "y", "c"))
    return (total, g_x[None], *g_out, *d_out, *m_out, *v_out)
```

```python
import functools
import math

import jax
import jax.numpy as jnp
import numpy as np
from jax import lax
from jax.experimental import pallas as pl
from jax.experimental.pallas import tpu as pltpu

F32 = jnp.float32
BF16 = jnp.bfloat16

D_MODEL = 1024
HEAD_DIM = 64
N_ATT_HEADS = 12
ATT_WIDTH = N_ATT_HEADS * HEAD_DIM
DILATED_PATTERNS = ((128, 1), (512, 4), (2048, 16))
ATT_BLOCK = 128
SSM_INNER = 2048
SSM_HEAD_DIM = 64
SSM_HEADS = 32
SSM_GROUPS = 8
SSM_STATE = 128
SSM_CHUNK = 128
CONV_DIM = 4096
SSM_CONV = 4
FFN_HIDDEN = 4096
RMS_EPS = 1e-6
N_DEV = 8

ADAM_LR = 0.001
ADAM_B1 = 0.9
ADAM_B2 = 0.999
ADAM_EPS = 1e-08
ADAM_WD = 0.01
ADAM_STEP = 10

LANE = 128
OFF_Z, OFF_GL, OFF_XBC, OFF_QKV, OFF_DT = 0, 2048, 4096, 8192, 10496
PROJ_W = 10752
PA_W = OFF_QKV
PB_W = PROJ_W - OFF_QKV
PB_DT = OFF_DT - OFF_QKV
VMEM_LIMIT = 52 * 1024 * 1024
NEG = -1e30

HI = lax.Precision.HIGHEST
NT_DIMS = (((1,), (1,)), ((), ()))
TN_DIMS = (((0,), (0,)), ((), ()))
S = jax.ShapeDtypeStruct


def _params(sem):
    return pltpu.CompilerParams(dimension_semantics=sem, vmem_limit_bytes=VMEM_LIMIT)


def _matmul(a, b, *, mode, out_dtype, name, tm, tn, tk, epilogue=None, extra=None, stacked=False, after=None, b_cols=None):
    if mode == "nn":
        m, k = a.shape
        n = b.shape[0] * b.shape[2] if stacked else b.shape[1]
        col0 = 0
        if b_cols is not None:
            assert b_cols[0] % tn == 0, name
            col0, n = b_cols[0] // tn, b_cols[1]
        a_spec = pl.BlockSpec((tm, tk), lambda i, j, kk: (i, kk))
        b_spec = pl.BlockSpec((None, tk, tn), lambda i, j, kk: (j, kk, 0)) if stacked else pl.BlockSpec((tk, tn), lambda i, j, kk: (kk, col0 + j))
        dims = (((1,), (0,)), ((), ()))
    elif mode == "nt":
        m, k = a.shape
        n = b.shape[1] if stacked else b.shape[0]
        a_spec = pl.BlockSpec((tm, tk), lambda i, j, kk: (i, kk))
        b_spec = pl.BlockSpec((None, tn, tk), lambda i, j, kk: (kk, j, 0)) if stacked else pl.BlockSpec((tn, tk), lambda i, j, kk: (j, kk))
        dims = NT_DIMS
    else:
        (k, m), n = a.shape, b.shape[1]
        a_spec = pl.BlockSpec((tk, tm), lambda i, j, kk: (kk, i))
        b_spec = pl.BlockSpec((tk, tn), lambda i, j, kk: (kk, j))
        dims = TN_DIMS
    assert m % tm == 0 and n % tn == 0 and k % tk == 0, (name, m, n, k)
    if stacked:
        assert (tk if mode == "nt" else tn) * N_DEV == (k if mode == "nt" else n), name
    nk = k // tk
    o_spec = pl.BlockSpec((tm, tn), lambda i, j, kk: (i, j))
    in_specs, args = [a_spec, b_spec], [a, b]
    if epilogue == "relu2":
        out_shape = (S((m, n), BF16), S((m, n), BF16))
        out_specs = (o_spec, o_spec)
    elif epilogue == "also_bf16":
        out_shape = (S((m, n), out_dtype), S((m, n), BF16))
        out_specs = (o_spec, o_spec)
    elif stacked and mode == "tn":
        out_shape, out_specs = S((N_DEV, m, tn), out_dtype), pl.BlockSpec((None, tm, tn), lambda i, j, kk: (j, i, 0))
    else:
        out_shape, out_specs = S((m, n), out_dtype), o_spec
    if epilogue == "relu2_bwd":
        in_specs.append(o_spec)
        args.append(extra)
    n_in = len(args)
    if after is not None:
        in_specs.append(pl.BlockSpec(after.shape, lambda i, j, kk: (0,) * after.ndim))
        args.append(after)

    def finish(acc, refs):
        if epilogue == "relu2":
            r = jnp.maximum(acc, 0.0)
            refs[0][...] = (r * r).astype(BF16)
            refs[1][...] = acc.astype(BF16)
        elif epilogue == "also_bf16":
            refs[0][...] = acc.astype(out_dtype)
            refs[1][...] = acc.astype(BF16)
        elif epilogue == "relu2_bwd":
            up = refs[0][...].astype(F32)
            refs[1][...] = (acc * (2.0 * jnp.maximum(up, 0.0))).astype(out_dtype)
        else:
            refs[0][...] = acc.astype(out_dtype)

    def body(a_ref, b_ref, *rest):
        rest = rest[:n_in - 2] + rest[len(args) - 2:]
        part = lax.dot_general(a_ref[...].astype(BF16), b_ref[...].astype(BF16), dims, preferred_element_type=F32)
        if nk == 1:
            finish(part, rest)
            return
        acc_ref = rest[-1]
        kk = pl.program_id(2)

        @pl.when(kk == 0)
        def _():
            acc_ref[...] = part

        @pl.when(kk > 0)
        def _():
            acc_ref[...] += part

        @pl.when(kk == nk - 1)
        def _():
            finish(acc_ref[...], rest[:-1])

    scratch = [] if nk == 1 else [pltpu.VMEM((tm, tn), F32)]
    return pl.pallas_call(
        body, grid=(m // tm, n // tn, nk), in_specs=in_specs, out_specs=out_specs, out_shape=out_shape,
        scratch_shapes=scratch, name=name, compiler_params=_params(("parallel", "parallel", "arbitrary")),
    )(*args)


def _rowcall(body, name, n_rows, tr, ins, outs, scratch=(), into=None):
    in_specs = [pl.BlockSpec(bs, im) for _, bs, im in ins]
    out_specs = [pl.BlockSpec(bs, im) for _, _, bs, im in outs]
    out_shape = [S(sh, dt) for sh, dt, _, _ in outs]
    args = [a for a, _, _ in ins]
    aliases = {}
    kernel = body
    if into is not None:
        buf, bs, im = into
        n_in = len(args)
        in_specs.append(pl.BlockSpec(memory_space=pl.ANY))
        args.append(buf)
        out_specs.append(pl.BlockSpec(bs, im))
        out_shape.append(S(buf.shape, buf.dtype))
        aliases = {n_in: len(out_shape) - 1}

        def kernel(*refs):
            body(*refs[:n_in], *refs[n_in + 1:])

    return pl.pallas_call(
        kernel, grid=(n_rows // tr,), in_specs=in_specs, out_specs=out_specs, out_shape=out_shape,
        input_output_aliases=aliases, scratch_shapes=list(scratch), name=name, compiler_params=_params(("arbitrary",)),
    )(*args)


def _rows(arr, tr, width=None, cb=0):
    width = arr.shape[1] if width is None else width
    return (arr, (tr, width), lambda i, cb=cb: (i, cb))


def _whole(arr):
    nd = arr.ndim
    return (arr, arr.shape, lambda i, nd=nd: (0,) * nd)


def _orow(n_rows, width, dtype, tr):
    return ((n_rows, width), dtype, (tr, width), lambda i: (i, 0))


def _oacc(width):
    return ((1, width), F32, (1, width), lambda i: (0, 0))


def _accumulate(ref, value):
    first = pl.program_id(0) == 0

    @pl.when(first)
    def _():
        ref[...] = value

    @pl.when(jnp.logical_not(first))
    def _():
        ref[...] += value


def _colsum(v):
    return jnp.sum(v, axis=0, keepdims=True)


def _rms_fwd(x, w):
    r = lax.rsqrt(jnp.mean(x * x, axis=-1, keepdims=True) + RMS_EPS)
    return x * r * w


def _rms_bwd(gy, x, w):
    r = lax.rsqrt(jnp.mean(x * x, axis=-1, keepdims=True) + RMS_EPS)
    xn = x * r
    gxn = gy * w
    gx = r * (gxn - xn * jnp.mean(gxn * xn, axis=-1, keepdims=True))
    return gx, _colsum(gy * xn)


def _sigmoid(x):
    return 1.0 / (1.0 + jnp.exp(-x))


def _head_expand(n_heads_pad, n_heads, width):
    h = lax.broadcasted_iota(jnp.int32, (n_heads_pad, n_heads * width), 0)
    c = lax.broadcasted_iota(jnp.int32, (n_heads_pad, n_heads * width), 1)
    return (c // width == h).astype(F32)


def _head_reduce(n_heads, width, n_heads_pad):
    c = lax.broadcasted_iota(jnp.int32, (n_heads * width, n_heads_pad), 0)
    h = lax.broadcasted_iota(jnp.int32, (n_heads * width, n_heads_pad), 1)
    return (c // width == h).astype(F32)


def _block_ones(n, width):
    r = lax.broadcasted_iota(jnp.int32, (n, n), 0)
    c = lax.broadcasted_iota(jnp.int32, (n, n), 1)
    return (r // width == c // width).astype(F32)


def _pre_norm(x, w_pre, tr=512):
    t = x.shape[0]

    def body(x_ref, w_ref, u_ref):
        u_ref[...] = _rms_fwd(x_ref[...], w_ref[...]).astype(BF16)

    return _rowcall(body, "pre_norm", t, tr, [_rows(x, tr), _whole(w_pre)], [_orow(t, D_MODEL, BF16, tr)])[0]


CONV_HALO = 16


def _row_shift(cur, halo, j):
    tr = cur.shape[0]
    r = lax.broadcasted_iota(jnp.int32, (tr, tr), 0)
    c = lax.broadcasted_iota(jnp.int32, (tr, tr), 1)
    main = jnp.dot((c == r + j).astype(BF16), cur, preferred_element_type=F32)
    er = lax.broadcasted_iota(jnp.int32, (CONV_HALO, CONV_HALO), 0)
    ec = lax.broadcasted_iota(jnp.int32, (CONV_HALO, CONV_HALO), 1)
    if j < 0:
        edge = jnp.dot((ec == CONV_HALO + er + j).astype(BF16), halo, preferred_element_type=F32)
        return jnp.concatenate([main[:CONV_HALO] + edge, main[CONV_HALO:]], axis=0)
    edge = jnp.dot((ec == er + j - CONV_HALO).astype(BF16), halo, preferred_element_type=F32)
    return jnp.concatenate([main[:tr - CONV_HALO], main[tr - CONV_HALO:] + edge], axis=0)


def _conv_fwd(proj, conv_w, conv_b, tr=256):
    t = proj.shape[0]
    cb = OFF_XBC // CONV_DIM
    halo = (proj, (CONV_HALO, CONV_DIM), lambda i: (jnp.maximum(i * (tr // CONV_HALO) - 1, 0), cb))

    def body(cur_ref, prev_ref, w_ref, b_ref, o_ref, xc_ref):
        cur = cur_ref[...]
        prev = jnp.where(pl.program_id(0) > 0, prev_ref[...], jnp.zeros_like(prev_ref[...]))
        acc = b_ref[...] + w_ref[3:4, :] * cur.astype(F32)
        for k in range(SSM_CONV - 1):
            acc = acc + w_ref[k:k + 1, :] * _row_shift(cur, prev, -(SSM_CONV - 1 - k))
        o_ref[...] = (acc * _sigmoid(acc)).astype(BF16)
        xc_ref[...] = acc.astype(BF16)

    return _rowcall(body, "conv_fwd", t, tr, [_rows(proj, tr, CONV_DIM, cb), halo, _whole(conv_w), _whole(conv_b)],
                    [_orow(t, CONV_DIM, BF16, tr), _orow(t, CONV_DIM, BF16, tr)])


def _dt_fwd(proj, dt_bias_pad, alog_pad, tr=512):
    t = proj.shape[0]

    def body(raw_ref, b_ref, al_ref, dtx_ref, csx_ref, cst_ref):
        v = raw_ref[...] + b_ref[...]
        dt = jnp.maximum(v, 0.0) + jnp.log1p(jnp.exp(-jnp.abs(v)))
        expand = _head_expand(LANE, SSM_HEADS, SSM_HEAD_DIM)
        dtx_ref[...] = _dot_split(dt, expand, 0, 3)
        la = dt * (-jnp.exp(al_ref[...]))
        row = lax.broadcasted_iota(jnp.int32, (SSM_CHUNK, SSM_CHUNK), 0)
        col = lax.broadcasted_iota(jnp.int32, (SSM_CHUNK, SSM_CHUNK), 1)
        tril = (col <= row).astype(F32)
        cs = jnp.concatenate([_dot_split(tril, la[k * SSM_CHUNK:(k + 1) * SSM_CHUNK, :], 1, 3) for k in range(tr // SSM_CHUNK)], axis=0)
        csx_ref[...] = _dot_split(cs, expand, 0, 3)
        cst_ref[...] = cs.T

    return _rowcall(body, "dt_fwd", t, tr, [_rows(proj, tr, LANE, PB_DT // LANE), _whole(dt_bias_pad), _whole(alog_pad)],
                    [_orow(t, SSM_INNER, F32, tr), _orow(t, SSM_INNER, F32, tr), ((LANE, t), F32, (LANE, tr), lambda i: (0, i))])


def _gate_norm_fwd(y_ssd, xa, proj, dskip_x, norm_w, tr=256):
    t = y_ssd.shape[0]
    gw = SSM_INNER // SSM_GROUPS

    def body(y_ref, xs_ref, z_ref, d_ref, w_ref, o_ref):
        z = z_ref[...].astype(F32)
        y3 = (y_ref[...].astype(F32) + d_ref[...] * xs_ref[...].astype(F32)) * (z * _sigmoid(z))
        for g in range(SSM_GROUPS):
            sl = slice(g * gw, (g + 1) * gw)
            o_ref[:, sl] = _rms_fwd(y3[:, sl], w_ref[:, sl]).astype(BF16)

    return _rowcall(body, "gate_norm_fwd", t, tr,
                    [_rows(y_ssd, tr), _rows(xa, tr, SSM_INNER, 0), _rows(proj, tr, SSM_INNER, OFF_Z // SSM_INNER), _whole(dskip_x), _whole(norm_w)],
                    [_orow(t, SSM_INNER, BF16, tr)])[0]


def _gating_fwd(proj, b_gate, att_p, ssm_p, tr=512):
    t = proj.shape[0]

    def body(gl_ref, b_ref, a_ref, s_ref, o_ref):
        gates = _sigmoid(gl_ref[...].astype(F32) + b_ref[...])
        o_ref[...] = (gates[:, :D_MODEL] * a_ref[...].astype(F32) + gates[:, D_MODEL:] * s_ref[...].astype(F32)).astype(BF16)

    return _rowcall(body, "gating_fwd", t, tr, [_rows(proj, tr, 2 * D_MODEL, OFF_GL // (2 * D_MODEL)), _whole(b_gate), _rows(att_p, tr), _rows(ssm_p, tr)],
                    [_orow(t, D_MODEL, BF16, tr)])[0]


def _mix_post_ffn_pre(x, mixed, w_post, w_fpre, tr=512):
    t = x.shape[0]

    def body(x_ref, m_ref, wp_ref, wf_ref, h1_ref, f_ref):
        h1 = x_ref[...] + _rms_fwd(m_ref[...], wp_ref[...])
        h1_ref[...] = h1
        f_ref[...] = _rms_fwd(h1, wf_ref[...]).astype(BF16)

    return _rowcall(body, "mix_post_ffn_pre", t, tr, [_rows(x, tr), _rows(mixed, tr), _whole(w_post), _whole(w_fpre)],
                    [_orow(t, D_MODEL, F32, tr), _orow(t, D_MODEL, BF16, tr)])


def _loss_and_ffn_post_bwd(h1, dn, w_fpost, target, tr=512):
    t = h1.shape[0]

    def body(h1_ref, dn_ref, w_ref, tg_ref, loss_ref, gh2_ref, gdn_ref, gw_ref):
        dn = dn_ref[...]
        w = w_ref[...]
        err = h1_ref[...] + _rms_fwd(dn, w) - tg_ref[...]
        _accumulate(loss_ref, jnp.zeros((1, LANE), F32) + 0.5 * jnp.sum(jnp.mean(err * err, axis=-1, keepdims=True)))
        gh2 = err * (1.0 / D_MODEL)
        gh2_ref[...] = gh2
        gdn, gw = _rms_bwd(gh2, dn, w)
        gdn_ref[...] = gdn.astype(BF16)
        _accumulate(gw_ref, gw)

    return _rowcall(body, "loss_ffn_post_bwd", t, tr, [_rows(h1, tr), _rows(dn, tr), _whole(w_fpost), _rows(target, tr)],
                    [_oacc(LANE), _orow(t, D_MODEL, F32, tr), _orow(t, D_MODEL, BF16, tr), _oacc(D_MODEL)])


def _ffn_pre_mix_post_bwd(g_h2, g_f, h1, w_fpre, mixed, w_post, tr=512):
    t = h1.shape[0]

    def body(gh2_ref, gf_ref, h1_ref, wf_ref, m_ref, wp_ref, gh1_ref, gm_ref, gwf_ref, gwp_ref):
        gx, gwf = _rms_bwd(gf_ref[...], h1_ref[...], wf_ref[...])
        gh1 = gh2_ref[...] + gx
        gh1_ref[...] = gh1
        gm, gwp = _rms_bwd(gh1, m_ref[...], wp_ref[...])
        gm_ref[...] = gm.astype(BF16)
        _accumulate(gwf_ref, gwf)
        _accumulate(gwp_ref, gwp)

    return _rowcall(body, "ffn_pre_mix_post_bwd", t, tr,
                    [_rows(g_h2, tr), _rows(g_f, tr), _rows(h1, tr), _whole(w_fpre), _rows(mixed, tr), _whole(w_post)],
                    [_orow(t, D_MODEL, F32, tr), _orow(t, D_MODEL, BF16, tr), _oacc(D_MODEL), _oacc(D_MODEL)])


def _gating_bwd(g_mixin, proj, b_gate, att_p, ssm_p, g_proj, tr=512):
    t = proj.shape[0]

    def body(gm_ref, gl_ref, b_ref, a_ref, s_ref, ga_ref, gs_ref, gb_ref, ggl_ref):
        gates = _sigmoid(gl_ref[...].astype(F32) + b_ref[...])
        gm = gm_ref[...].astype(F32)
        g_att, g_ssm = gates[:, :D_MODEL], gates[:, D_MODEL:]
        ga_ref[...] = (gm * g_att).astype(BF16)
        gs_ref[...] = (gm * g_ssm).astype(BF16)
        ggl_a = gm * a_ref[...].astype(F32) * g_att * (1.0 - g_att)
        ggl_s = gm * s_ref[...].astype(F32) * g_ssm * (1.0 - g_ssm)
        ggl_ref[:, :D_MODEL] = ggl_a.astype(BF16)
        ggl_ref[:, D_MODEL:] = ggl_s.astype(BF16)
        _accumulate(gb_ref.at[:, :D_MODEL], _colsum(ggl_a))
        _accumulate(gb_ref.at[:, D_MODEL:], _colsum(ggl_s))

    return _rowcall(body, "gating_bwd", t, tr,
                    [_rows(g_mixin, tr), _rows(proj, tr, 2 * D_MODEL, OFF_GL // (2 * D_MODEL)), _whole(b_gate), _rows(att_p, tr), _rows(ssm_p, tr)],
                    [_orow(t, D_MODEL, BF16, tr), _orow(t, D_MODEL, BF16, tr), _oacc(2 * D_MODEL)],
                    into=(g_proj, (tr, 2 * D_MODEL), lambda i: (i, OFF_GL // (2 * D_MODEL))))


def _gate_norm_bwd(g_y4, y_ssd, xa, proj, dskip_x, norm_w, g_proj, tr=256):
    t = y_ssd.shape[0]
    gw = SSM_INNER // SSM_GROUPS

    def body(g_ref, y_ref, xs_ref, z_ref, d_ref, w_ref, gy2_ref, gnw_ref, gdx_ref, gd_ref, gz_ref):
        z = z_ref[...].astype(F32)
        xs = xs_ref[...].astype(F32)
        sg = _sigmoid(z)
        sz = z * sg
        y2 = y_ref[...].astype(F32) + d_ref[...] * xs
        y3 = y2 * sz
        g4 = g_ref[...].astype(F32)
        for g in range(SSM_GROUPS):
            sl = slice(g * gw, (g + 1) * gw)
            gy3, gnw = _rms_bwd(g4[:, sl], y3[:, sl], w_ref[:, sl])
            _accumulate(gnw_ref.at[:, sl], gnw)
            gy2 = gy3 * sz[:, sl]
            gy2_ref[:, sl] = gy2.astype(BF16)
            gz_ref[:, sl] = (gy3 * y2[:, sl] * (sg[:, sl] * (1.0 + z[:, sl] * (1.0 - sg[:, sl])))).astype(BF16)
            _accumulate(gdx_ref.at[:, sl], _colsum(gy2 * xs[:, sl]))
        tot = jnp.broadcast_to(gdx_ref[...], (8, SSM_INNER))
        gd_ref[...] = jnp.dot(tot, _head_reduce(SSM_HEADS, SSM_HEAD_DIM, LANE), precision=HI, preferred_element_type=F32)[0:1, :]

    return _rowcall(body, "gate_norm_bwd", t, tr,
                    [_rows(g_y4, tr), _rows(y_ssd, tr), _rows(xa, tr, SSM_INNER, 0), _rows(proj, tr, SSM_INNER, OFF_Z // SSM_INNER), _whole(dskip_x), _whole(norm_w)],
                    [_orow(t, SSM_INNER, BF16, tr), _oacc(SSM_INNER), _oacc(SSM_INNER), _oacc(LANE)],
                    into=(g_proj, (tr, SSM_INNER), lambda i: (i, OFF_Z // SSM_INNER)))


def _dt_bwd(g_dtx, ga_rows, proj, dt_bias_pad, g_proj, tr=512):
    t = proj.shape[0]
    tail = PROJ_W - OFF_DT

    def body(g_ref, ga_ref, raw_ref, b_ref, gb_ref, gal_ref, o_ref):
        red = _head_reduce(SSM_HEADS, SSM_HEAD_DIM, LANE)
        gdt = _dot_split(g_ref[...], red, 0, 3)
        graw = gdt * _sigmoid(raw_ref[...] + b_ref[...])
        o_ref[...] = jnp.concatenate([graw.astype(BF16), jnp.zeros((tr, tail - LANE), BF16)], axis=1)
        _accumulate(gb_ref, _colsum(graw))
        tot = jnp.broadcast_to(_colsum(ga_ref[...]), (8, SSM_INNER))
        gal_ref[...] = jnp.dot(tot, red, precision=HI, preferred_element_type=F32)[0:1, :]

    return _rowcall(body, "dt_bwd", t, tr, [_rows(g_dtx, tr), _whole(ga_rows), _rows(proj, tr, LANE, PB_DT // LANE), _whole(dt_bias_pad)],
                    [_oacc(LANE), _oacc(LANE)], into=(g_proj, (tr, tail), lambda i: (i, OFF_DT // tail)))


def _conv_bwd(g_xs, g_b, g_c, xc, proj, conv_w, g_proj, tr=256):
    t = proj.shape[0]
    n_blk = t // tr
    cb = OFF_XBC // CONV_DIM
    nb, nc = SSM_INNER, SSM_INNER + SSM_GROUPS * SSM_STATE
    def nxt(arr, width):
        return (arr, (CONV_HALO, width), lambda i: (jnp.minimum((i + 1) * (tr // CONV_HALO), t // CONV_HALO - 1), 0))

    def body(gxs_ref, gxs_n, gb_ref, gb_n, gc_ref, gc_n, xc_ref, xc_n, x_ref, w_ref, gcb_ref, gw0, gw1, gw2, gw3, o_ref):
        def gxc_of(gxs, gb, gc, xc, keep):
            xcf = xc[...].astype(F32)
            sg = _sigmoid(xcf)
            dsilu = jnp.where(keep, sg * (1.0 + xcf * (1.0 - sg)), 0.0)
            return jnp.concatenate([gxs[...] * dsilu[:, :nb], gb[...] * dsilu[:, nb:nc], gc[...] * dsilu[:, nc:]], axis=1)

        gxc = gxc_of(gxs_ref, gb_ref, gc_ref, xc_ref, True)
        gxc16 = gxc.astype(BF16)
        nxt16 = gxc_of(gxs_n, gb_n, gc_n, xc_n, pl.program_id(0) < n_blk - 1).astype(BF16)
        x = x_ref[...].astype(F32)
        acc = w_ref[3:4, :] * gxc
        _accumulate(gw3, _colsum(gxc * x))
        _accumulate(gcb_ref, _colsum(gxc))
        for k, gw in enumerate((gw0, gw1, gw2)):
            shifted = _row_shift(gxc16, nxt16, SSM_CONV - 1 - k)
            acc = acc + w_ref[k:k + 1, :] * shifted
            _accumulate(gw, _colsum(shifted * x))
        o_ref[...] = acc.astype(BF16)

    ins = []
    for arr, width in ((g_xs, SSM_INNER), (g_b, nc - nb), (g_c, nc - nb), (xc, CONV_DIM)):
        ins += [_rows(arr, tr), nxt(arr, width)]
    ins += [_rows(proj, tr, CONV_DIM, cb), _whole(conv_w)]
    return _rowcall(body, "conv_bwd", t, tr, ins, [_oacc(CONV_DIM)] * 5, into=(g_proj, (tr, CONV_DIM), lambda i: (i, cb)))


def _pre_norm_bwd(g_h1, g_u, x, w_pre, tr=512):
    t = x.shape[0]

    def body(gh_ref, gu_ref, x_ref, w_ref, gx_ref, gw_ref):
        gx, gw = _rms_bwd(gu_ref[...], x_ref[...], w_ref[...])
        gx_ref[...] = gh_ref[...] + gx
        _accumulate(gw_ref, gw)

    return _rowcall(body, "pre_norm_bwd", t, tr, [_rows(g_h1, tr), _rows(g_u, tr), _rows(x, tr), _whole(w_pre)],
                    [_orow(t, D_MODEL, F32, tr), _oacc(D_MODEL)])


def _alibi_slopes(n):
    def pow2(m):
        start = 2.0 ** (-8.0 / m)
        return [start ** (i + 1) for i in range(m)]
    if (n & (n - 1)) == 0:
        s = pow2(n)
    else:
        c = 2 ** int(math.floor(math.log2(n)))
        s = pow2(c) + pow2(2 * c)[0::2][: n - c]
    return np.array(s, dtype=np.float32)


def _slope_rows():
    s = _alibi_slopes(N_ATT_HEADS).reshape(N_ATT_HEADS // 2, 2)
    return jnp.asarray(np.broadcast_to(np.repeat(s, HEAD_DIM, axis=1)[:, None, :], (N_ATT_HEADS // 2, 8, LANE)).copy())


ATT_MAX_BLOCK_ROWS = 2048


RESIDUE_MAJOR_FROM = 16


class _AttLayout:
    def __init__(self, t, dil):
        self.t, self.dil = t, dil
        self.rows = t // dil
        self.residue_major = dil >= RESIDUE_MAJOR_FROM
        if self.residue_major:
            bq, self.stride = min(512, self.rows), 1
        else:
            bq, self.stride = min(512, self.rows, ATT_MAX_BLOCK_ROWS // dil), dil
        self.nsub = bq // ATT_BLOCK
        self.nblk = self.rows // bq
        self.rb = bq * self.stride
        self.pb = ATT_BLOCK * self.stride
        self.n_pb = self.rows * self.stride // self.pb
        self.out_dtype = F32 if self.stride > 1 else BF16
        self.pairs = 1 if self.stride > 1 else 2

    def qkv(self, proj):
        pb, pb16 = proj
        if self.residue_major:
            return pb16.reshape(self.rows, self.dil * PB_W), PB_W // LANE, 0
        return (pb if self.stride > 1 else pb16), 0, 0

    def act(self, a):
        return a.reshape(self.rows, self.dil * ATT_WIDTH) if self.residue_major else a

    def act_shape(self):
        return (self.rows, self.dil * ATT_WIDTH) if self.residue_major else (self.t, ATT_WIDTH)

    def col(self, r, band, c):
        return r * band + c if self.residue_major else c


def _residue_rows(r, stride, first_block, n_blocks=1):
    if stride == 1:
        return pl.ds(first_block * ATT_BLOCK, n_blocks * ATT_BLOCK)
    return pl.ds(r + first_block * ATT_BLOCK * stride, n_blocks * ATT_BLOCK, stride=stride)


def _lane_half():
    return lax.broadcasted_iota(jnp.int32, (ATT_BLOCK, LANE), 1) // HEAD_DIM


def _att_scores_mask(dil, first):
    iq = lax.broadcasted_iota(jnp.int32, (ATT_BLOCK, 2 * ATT_BLOCK), 0)
    jk = lax.broadcasted_iota(jnp.int32, (ATT_BLOCK, 2 * ATT_BLOCK), 1)
    dist = ATT_BLOCK + iq - jk
    valid = (dist >= 0) & (dist <= ATT_BLOCK) & (jnp.logical_not(first) | (jk >= ATT_BLOCK))
    return (dist * dil).astype(F32), valid


def _stack_heads(x):
    half = _lane_half()
    return jnp.concatenate([jnp.where(half == 0, x, jnp.zeros_like(x)), jnp.where(half == 1, x, jnp.zeros_like(x))], axis=0)


def _unstack_heads(x):
    return jnp.where(_lane_half() == 0, x[:ATT_BLOCK], x[ATT_BLOCK:])


def _head_columns(x):
    return jnp.concatenate([x[:, 0:1], x[:, HEAD_DIM:HEAD_DIM + 1]], axis=0)


def _stacked_bias(sl_ref, dist, valid):
    d2 = jnp.concatenate([dist, dist], axis=0)
    v2 = jnp.concatenate([valid, valid], axis=0)
    top = lax.broadcasted_iota(jnp.int32, d2.shape, 0) < ATT_BLOCK
    slope = jnp.where(top, sl_ref[0:1, 0:1], sl_ref[0:1, HEAD_DIM:HEAD_DIM + 1])
    return jnp.where(v2, -slope * d2, NEG)


def _pair_views(refs, e2, pairs, slope_at):
    return [ref.at[e2] if n == slope_at else (ref if pairs == 1 else ref.at[:, pl.ds(e2 * LANE, LANE)]) for n, ref in enumerate(refs)]


def _att_fwd(proj, dil, slopes, others=()):
    t = proj[0].shape[0]
    lay = _AttLayout(t, dil)
    assert not others or lay.stride == 1 and not lay.residue_major
    nsub, nblk, rb, pb = lay.nsub, lay.nblk, lay.rb, lay.pb
    src, band, qb = lay.qkv(proj)
    aw = ATT_WIDTH // LANE
    n_other = 2 * len(others)

    pairs = lay.pairs
    wide = pairs * LANE

    def spec(off, prev=False):
        if prev:
            return pl.BlockSpec((pb, wide), lambda hp, i, r: (jnp.maximum(i * nsub - 1, 0), lay.col(r, band, qb + off) // pairs + hp))
        return pl.BlockSpec((rb, wide), lambda hp, i, r: (i, lay.col(r, band, qb + off) // pairs + hp))

    o_spec = pl.BlockSpec((rb, wide), lambda hp, i, r: (i, lay.col(r, aw, 0) // pairs + hp))

    def body(*refs):
        for e2 in range(pairs):
            pair_body(*_pair_views(refs, e2, pairs, 5))

    def pair_body(q_ref, kc_ref, kp_ref, vc_ref, vp_ref, sl_ref, *rest):
        other_refs, (o_ref, lse_ref) = rest[:n_other], rest[n_other:]
        i, r = pl.program_id(1), pl.program_id(2)
        for sub in range(nsub):
            rs = _residue_rows(r, lay.stride, sub)
            q = (q_ref[rs, :] * (HEAD_DIM ** -0.5)).astype(BF16)
            if sub == 0:
                r0 = _residue_rows(r, lay.stride, 0)
                kk = jnp.concatenate([kp_ref[r0, :], kc_ref[rs, :]], axis=0).astype(BF16)
                vv = jnp.concatenate([vp_ref[r0, :], vc_ref[rs, :]], axis=0).astype(BF16)
                first = i == 0
            else:
                ks = _residue_rows(r, lay.stride, sub - 1, 2)
                kk, vv = kc_ref[ks, :].astype(BF16), vc_ref[ks, :].astype(BF16)
                first = jnp.bool_(False)
            dist, valid = _att_scores_mask(dil, first)
            s = lax.dot_general(_stack_heads(q), kk, NT_DIMS, preferred_element_type=F32) + _stacked_bias(sl_ref, dist, valid)
            m = jnp.max(s, axis=-1, keepdims=True)
            p = jnp.exp(s - m)
            l = jnp.sum(p, axis=-1, keepdims=True)
            out = _unstack_heads(jnp.dot(p.astype(BF16), vv, preferred_element_type=F32) / l)
            lse = _unstack_heads(jnp.broadcast_to(m + jnp.log(l), (2 * ATT_BLOCK, LANE)))
            if others:
                outs = [out] + [ref[rs, :].astype(F32) for ref in other_refs[0::2]]
                lses = [lse] + [ref[rs, :] for ref in other_refs[1::2]]
                top = functools.reduce(jnp.maximum, lses)
                ws = [jnp.exp(x - top) for x in lses]
                tot = functools.reduce(jnp.add, ws)
                out = functools.reduce(jnp.add, [w * o for w, o in zip(ws, outs)]) / tot
                lse = top + jnp.log(tot)
            o_ref[rs, :] = out.astype(lay.out_dtype)
            lse_ref[rs, :] = lse

    o, lse = pl.pallas_call(
        body, grid=(N_ATT_HEADS // 2 // pairs, nblk, dil),
        in_specs=[spec(0), spec(6), spec(6, True), spec(12), spec(12, True), pl.BlockSpec((pairs, 8, LANE), lambda hp, i, r: (hp, 0, 0))]
        + [o_spec] * n_other,
        out_specs=[o_spec, o_spec], out_shape=[S(lay.act_shape(), lay.out_dtype), S(lay.act_shape(), F32)],
        name=f"att_fwd_d{dil}", compiler_params=_params(("parallel", "parallel", "arbitrary")),
    )(src, src, src, src, src, slopes, *[a for pair in others for a in pair])
    return o.reshape(t, ATT_WIDTH), lse.reshape(t, ATT_WIDTH)


def _att_delta(g_att, att, tr=512):
    t = att.shape[0]

    def body(g_ref, a_ref, o_ref):
        prod = g_ref[...] * a_ref[...].astype(F32)
        o_ref[...] = _dot_split(prod, _block_ones(ATT_WIDTH, HEAD_DIM), 0, 3)

    return _rowcall(body, "att_delta", t, tr, [_rows(g_att, tr), _rows(att, tr)], [_orow(t, ATT_WIDTH, F32, tr)])[0]


def _att_bwd(proj, g_att, lse, delta, dil, slopes):
    t = proj[0].shape[0]
    lay = _AttLayout(t, dil)
    nsub, nblk, rb, pb, n_pb = lay.nsub, lay.nblk, lay.rb, lay.pb, lay.n_pb
    src, band, qb = lay.qkv(proj)
    aw = ATT_WIDTH // LANE

    def near(i, which):
        return jnp.maximum(i * nsub - 1, 0) if which == "prev" else jnp.minimum((i + 1) * nsub, n_pb - 1)

    pairs = lay.pairs
    wide = pairs * LANE

    def pspec(off, which=None):
        if which:
            return pl.BlockSpec((pb, wide), lambda hp, i, r: (near(i, which), lay.col(r, band, qb + off) // pairs + hp))
        return pl.BlockSpec((rb, wide), lambda hp, i, r: (i, lay.col(r, band, qb + off) // pairs + hp))

    def aspec(which=None):
        if which:
            return pl.BlockSpec((pb, wide), lambda hp, i, r: (near(i, which), lay.col(r, aw, 0) // pairs + hp))
        return pl.BlockSpec((rb, wide), lambda hp, i, r: (i, lay.col(r, aw, 0) // pairs + hp))

    scale = HEAD_DIM ** -0.5

    def body(*refs):
        for e2 in range(pairs):
            pair_body(*_pair_views(refs, e2, pairs, 12))

    def pair_body(q_ref, qn_ref, kc_ref, kp_ref, vc_ref, vp_ref, do_ref, don_ref, lse_ref, lsen_ref, dl_ref, dln_ref, sl_ref,
                  dq_ref, dk_ref, dv_ref):
        i, r = pl.program_id(1), pl.program_id(2)

        def tile_grads(q, do, lse_q, dl_q, kk, vv, dist, valid):
            q2, do2 = _stack_heads(q), _stack_heads(do)
            s = lax.dot_general(q2, kk, NT_DIMS, preferred_element_type=F32) + _stacked_bias(sl_ref, dist, valid)
            p = jnp.exp(s - _head_columns(lse_q))
            dp = lax.dot_general(do2, vv, NT_DIMS, preferred_element_type=F32)
            ds16 = (p * (dp - _head_columns(dl_q))).astype(BF16)
            dq = _unstack_heads(jnp.dot(ds16, kk, preferred_element_type=F32)) * scale
            dk = lax.dot_general(ds16, q2, TN_DIMS, preferred_element_type=F32)
            dv = lax.dot_general(p.astype(BF16), do2, TN_DIMS, preferred_element_type=F32)
            return dq, dk, dv

        carry_k = carry_v = None
        for sub in range(nsub):
            rs = _residue_rows(r, lay.stride, sub)
            q = (q_ref[rs, :] * scale).astype(BF16)
            do = do_ref[rs, :].astype(BF16)
            if sub == 0:
                r0 = _residue_rows(r, lay.stride, 0)
                kk = jnp.concatenate([kp_ref[r0, :], kc_ref[rs, :]], axis=0).astype(BF16)
                vv = jnp.concatenate([vp_ref[r0, :], vc_ref[rs, :]], axis=0).astype(BF16)
                first = i == 0
            else:
                ks = _residue_rows(r, lay.stride, sub - 1, 2)
                kk, vv = kc_ref[ks, :].astype(BF16), vc_ref[ks, :].astype(BF16)
                first = jnp.bool_(False)
            dist, valid = _att_scores_mask(dil, first)
            dq, dk2, dv2 = tile_grads(q, do, lse_ref[rs, :], dl_ref[rs, :], kk, vv, dist, valid)
            dq_ref[rs, :] = dq.astype(lay.out_dtype)
            if sub > 0:
                rp = _residue_rows(r, lay.stride, sub - 1)
                dk_ref[rp, :] = (carry_k + dk2[:ATT_BLOCK, :]).astype(lay.out_dtype)
                dv_ref[rp, :] = (carry_v + dv2[:ATT_BLOCK, :]).astype(lay.out_dtype)
            carry_k, carry_v = dk2[ATT_BLOCK:, :], dv2[ATT_BLOCK:, :]
        rl = _residue_rows(r, lay.stride, nsub - 1)
        rn = _residue_rows(r, lay.stride, 0)
        iq = lax.broadcasted_iota(jnp.int32, (ATT_BLOCK, ATT_BLOCK), 0)
        jk = lax.broadcasted_iota(jnp.int32, (ATT_BLOCK, ATT_BLOCK), 1)
        dist_i = ATT_BLOCK + iq - jk
        valid = (dist_i >= 0) & (dist_i <= ATT_BLOCK) & (i < nblk - 1)
        qn = (qn_ref[rn, :] * scale).astype(BF16)
        _, dk1, dv1 = tile_grads(qn, don_ref[rn, :].astype(BF16), lsen_ref[rn, :], dln_ref[rn, :],
                                 kc_ref[rl, :].astype(BF16), vc_ref[rl, :].astype(BF16), (dist_i * dil).astype(F32), valid)
        dk_ref[rl, :] = (carry_k + dk1).astype(lay.out_dtype)
        dv_ref[rl, :] = (carry_v + dv1).astype(lay.out_dtype)

    gv, lv, dlv = lay.act(g_att), lay.act(lse), lay.act(delta)
    dq, dk, dv = pl.pallas_call(
        body, grid=(N_ATT_HEADS // 2 // pairs, nblk, dil),
        in_specs=[pspec(0), pspec(0, "next"), pspec(6), pspec(6, "prev"), pspec(12), pspec(12, "prev"),
                  aspec(), aspec("next"), aspec(), aspec("next"), aspec(), aspec("next"),
                  pl.BlockSpec((pairs, 8, LANE), lambda hp, i, r: (hp, 0, 0))],
        out_specs=[aspec(), aspec(), aspec()], out_shape=[S(lay.act_shape(), lay.out_dtype)] * 3,
        name=f"att_bwd_d{dil}", compiler_params=_params(("parallel", "parallel", "arbitrary")),
    )(src, src, src, src, src, src, gv, gv, lv, lv, dlv, dlv, slopes)
    return dq.reshape(t, ATT_WIDTH), dk.reshape(t, ATT_WIDTH), dv.reshape(t, ATT_WIDTH)


def _att_grad_sum(dqs, dks, dvs, g_proj, tr=2048):
    t = dqs[0].shape[0]
    cw = 2 * LANE
    per = ATT_WIDTH // cw
    arrays = list(dqs) + list(dks) + list(dvs)
    n_pat = len(dqs)

    def body(*refs):
        o_ref = refs[-1]
        which = pl.program_id(1) // per
        tot = jnp.zeros((tr, cw), F32)
        for s in range(3):
            part = refs[s * n_pat][...].astype(F32)
            for g in range(1, n_pat):
                part = part + refs[s * n_pat + g][...].astype(F32)
            tot = jnp.where(which == s, part, tot)
        o_ref[...] = tot.astype(BF16)

    in_specs = [pl.BlockSpec((tr, cw), lambda i, c, s=s: (i, jnp.clip(c - per * s, 0, per - 1))) for s in range(3) for _ in range(n_pat)]
    in_specs.append(pl.BlockSpec(memory_space=pl.ANY))
    return pl.pallas_call(
        lambda *refs: body(*refs[:len(arrays)], refs[-1]), grid=(t // tr, 3 * per), in_specs=in_specs,
        out_specs=pl.BlockSpec((tr, cw), lambda i, c: (i, OFF_QKV // cw + c)), out_shape=S(g_proj.shape, g_proj.dtype),
        input_output_aliases={len(arrays): 0}, name="att_grad_sum", compiler_params=_params(("arbitrary", "arbitrary")),
    )(*arrays, g_proj)


def _ssd_common(xs, dtx, cs, cs_t):
    ch = SSM_CHUNK
    row = lax.broadcasted_iota(jnp.int32, (ch, ch), 0)
    col = lax.broadcasted_iota(jnp.int32, (ch, ch), 1)
    cs_last = cs[ch - 1:ch, :]
    return dict(tril=col <= row, row=row, col=col, cs=cs, cs_t=cs_t, cs_last=cs_last,
                e=jnp.exp(cs), w=jnp.exp(cs_last - cs), xd=xs * dtx)


def _dot_split(a, b, split, terms=2):
    ops = [a, b]
    rest = ops[split]
    other = ops[1 - split].astype(BF16)
    out = None
    for _ in range(terms):
        piece = rest.astype(BF16)
        rest = rest - piece.astype(F32)
        part = jnp.dot(other, piece, preferred_element_type=F32) if split == 1 else jnp.dot(piece, other, preferred_element_type=F32)
        out = part if out is None else out + part
    return out


def _decay_col(cs_t, heads_per_group):
    r = lax.broadcasted_iota(jnp.int32, (heads_per_group * SSM_HEAD_DIM, SSM_STATE), 0) // SSM_HEAD_DIM
    out = jnp.zeros((heads_per_group * SSM_HEAD_DIM, SSM_STATE), F32)
    for j in range(heads_per_group):
        out = jnp.where(r == j, jnp.exp(cs_t[j:j + 1, SSM_CHUNK - 1:SSM_CHUNK]), out)
    return out


SSD_GROUPS_PER_STEP = 8


def _ssd_specs(t):
    hg = SSM_HEADS // SSM_GROUPS
    gw = hg * SSM_HEAD_DIM
    nb0 = SSM_INNER // SSM_STATE
    return hg, gw, nb0


def _ssd_group_views(gi, gw, wide, narrow, stacked):
    w = [r.at[:, pl.ds(gi * gw, gw)] for r in wide]
    n = [r.at[:, pl.ds(gi * SSM_STATE, SSM_STATE)] for r in narrow]
    return w, n, [r.at[gi] for r in stacked]


def _ssd_fwd(xa, dtx, csx, cst_g):
    t = xa.shape[0]
    nch = t // SSM_CHUNK
    hg, gw, nb0 = _ssd_specs(t)
    ch = SSM_CHUNK
    gp = SSD_GROUPS_PER_STEP

    def body(xs_ref, b_ref, c_ref, dtx_ref, cs_ref, cst_ref, y_ref, st_ref, h_scr):
        for gi in range(gp):
            (xs_g, dtx_g, cs_g, y_g), (b_g, c_g), (cst_gi, st_g) = _ssd_group_views(
                gi, gw, (xs_ref, dtx_ref, cs_ref, y_ref), (b_ref, c_ref), (cst_ref, st_ref))
            group_body(pl.program_id(0), pl.program_id(1) * gp + gi, xs_g, b_g, c_g, dtx_g, cs_g, cst_gi, y_g, st_g, h_scr)

    def group_body(cc, g, xs_ref, b_ref, c_ref, dtx_ref, cs_ref, cst_ref, y_ref, st_ref, h_scr):
        @pl.when(cc == 0)
        def _():
            h_scr[g] = jnp.zeros((gw, SSM_STATE), F32)

        q = _ssd_common(xs_ref[...].astype(F32), dtx_ref[...], cs_ref[...], cst_ref[...])
        bb, cb = b_ref[...].astype(BF16), c_ref[...].astype(BF16)
        h = h_scr[g]
        st_ref[...] = h
        xd16 = q["xd"].astype(BF16)
        c_both = lax.dot_general(cb, jnp.concatenate([bb, h.astype(BF16)], axis=0), NT_DIMS, preferred_element_type=F32)
        cbm = c_both[:, :SSM_STATE]
        y = c_both[:, SSM_STATE:] * q["e"]
        lane_head = lax.broadcasted_iota(jnp.int32, (ch, gw), 1) // SSM_HEAD_DIM
        gmats, xds = [], []
        for j in range(hg):
            diff = q["cs"][:, j * SSM_HEAD_DIM:j * SSM_HEAD_DIM + 1] - q["cs_t"][j:j + 1, :]
            gmats.append((cbm * jnp.exp(jnp.where(q["tril"], diff, NEG))).astype(BF16))
            xds.append(jnp.where(lane_head == j, xd16, jnp.zeros_like(xd16)))
        y = y + jnp.dot(jnp.concatenate(gmats, axis=1), jnp.concatenate(xds, axis=0), preferred_element_type=F32)
        y_ref[...] = y.astype(BF16)
        s_new = lax.dot_general((q["xd"] * q["w"]).astype(BF16), bb, TN_DIMS, preferred_element_type=F32)
        h_scr[g] = _decay_col(q["cs_t"], hg) * h + s_new

    wide = pl.BlockSpec((ch, gp * gw), lambda cc, g: (cc, g))
    return pl.pallas_call(
        body, grid=(nch, SSM_GROUPS // gp),
        in_specs=[wide,
                  pl.BlockSpec((ch, gp * SSM_STATE), lambda cc, g: (cc, nb0 // gp + g)),
                  pl.BlockSpec((ch, gp * SSM_STATE), lambda cc, g: (cc, (nb0 + SSM_GROUPS) // gp + g)),
                  wide, wide,
                  pl.BlockSpec((gp, 8, ch), lambda cc, g: (g, 0, cc))],
        out_specs=[wide, pl.BlockSpec((None, gp, gw, SSM_STATE), lambda cc, g: (cc, g, 0, 0))],
        out_shape=[S((t, SSM_INNER), BF16), S((nch, SSM_GROUPS, gw, SSM_STATE), F32)],
        scratch_shapes=[pltpu.VMEM((SSM_GROUPS, gw, SSM_STATE), F32)],
        name="ssd_fwd", compiler_params=_params(("arbitrary", "arbitrary")),
    )(xa, xa, xa, dtx, csx, cst_g)


def _ssd_bwd(xa, dtx, csx, cst_g, alog_x, g_y, states, dskip_x):
    t = xa.shape[0]
    nch = t // SSM_CHUNK
    hg, gw, nb0 = _ssd_specs(t)
    ch = SSM_CHUNK
    gp = SSD_GROUPS_PER_STEP

    def rc(cc):
        return nch - 1 - cc

    def body(xs_ref, b_ref, c_ref, dtx_ref, cs_ref, cst_ref, alx_ref, gy_ref, st_ref, dsk_ref,
             gxs_ref, gb_ref, gc_ref, gdt_ref, ga_ref, gh_scr):
        for gi in range(gp):
            wide, narrow, stacked = _ssd_group_views(
                gi, gw, (xs_ref, dtx_ref, cs_ref, alx_ref, gy_ref, dsk_ref, gxs_ref, gdt_ref, ga_ref), (b_ref, c_ref, gb_ref, gc_ref),
                (cst_ref, st_ref))
            xs_g, dtx_g, cs_g, alx_g, gy_g, dsk_g, gxs_g, gdt_g, ga_g = wide
            b_g, c_g, gb_g, gc_g = narrow
            group_body(pl.program_id(0), pl.program_id(1) * gp + gi, xs_g, b_g, c_g, dtx_g, cs_g, stacked[0], alx_g, gy_g, stacked[1],
                       dsk_g, gxs_g, gb_g, gc_g, gdt_g, ga_g, gh_scr)

    def group_body(cc, g, xs_ref, b_ref, c_ref, dtx_ref, cs_ref, cst_ref, alx_ref, gy_ref, st_ref, dsk_ref,
                   gxs_ref, gb_ref, gc_ref, gdt_ref, ga_ref, gh_scr):
        @pl.when(cc == 0)
        def _():
            gh_scr[g] = jnp.zeros((gw, SSM_STATE), F32)

        xs, dtx = xs_ref[...].astype(F32), dtx_ref[...]
        q = _ssd_common(xs, dtx, cs_ref[...], cst_ref[...])
        cs, cs_t, e, w, xd = q["cs"], q["cs_t"], q["e"], q["w"], q["xd"]
        bb, cb = b_ref[...].astype(BF16), c_ref[...].astype(BF16)
        gy16 = gy_ref[...]
        gy = gy16.astype(F32)
        xd16 = xd.astype(BF16)
        h = st_ref[...]
        h16 = h.astype(BF16)
        ghn = gh_scr[g]
        ghn16 = ghn.astype(BF16)
        seg = _block_ones(gw, SSM_HEAD_DIM)
        c_both = lax.dot_general(cb, jnp.concatenate([bb, h16], axis=0), NT_DIMS, preferred_element_type=F32)
        cbm, chm = c_both[:, :SSM_STATE], c_both[:, SSM_STATE:]

        gye16 = (gy * e).astype(BF16)
        g_c = jnp.dot(gye16, h16, preferred_element_type=F32)
        gh_off = lax.dot_general(gye16, cb, TN_DIMS, preferred_element_type=F32)
        bgs = lax.dot_general(bb, ghn16, NT_DIMS, preferred_element_type=F32)
        g_xd = w * bgs
        head_sums = _dot_split(jnp.concatenate([gy * chm, xd * bgs], axis=0), seg, 0)
        g_e, g_w = head_sums[:ch], head_sums[ch:]
        g_b = jnp.dot((xd * w).astype(BF16), ghn16, preferred_element_type=F32)
        decay = _decay_col(cs_t, hg)
        gh_scr[g] = decay * ghn + gh_off
        rsum = jnp.sum(ghn * h, axis=1, keepdims=True)
        lane_head = lax.broadcasted_iota(jnp.int32, (ch, gw), 1) // SSM_HEAD_DIM
        lane_head1 = lax.broadcasted_iota(jnp.int32, (1, gw), 1) // SSM_HEAD_DIM
        g_el = jnp.zeros((1, gw), F32)
        g_cs = g_e * e - g_w * w
        upper = q["row"] <= q["col"]
        lms, gys = [], []
        for j in range(hg):
            g_el = jnp.where(lane_head1 == j, jnp.sum(rsum[j * SSM_HEAD_DIM:(j + 1) * SSM_HEAD_DIM, :], axis=0, keepdims=True), g_el)
            csc = cs[:, j * SSM_HEAD_DIM:j * SSM_HEAD_DIM + 1]
            csr = cs_t[j:j + 1, :]
            lms.append(jnp.exp(jnp.where(q["tril"], csc - csr, NEG)))
            gys.append(jnp.where(lane_head == j, gy16, jnp.zeros_like(gy16)))
        lm_st, gy_st = jnp.concatenate(lms, axis=0), jnp.concatenate(gys, axis=0)
        cbm_st = jnp.concatenate([cbm] * hg, axis=0)
        gcb_st = lax.dot_general(gy_st, xd16, NT_DIMS, preferred_element_type=F32) * lm_st
        gcb_sum = gcb_st[0:ch]
        for j in range(1, hg):
            gcb_sum = gcb_sum + gcb_st[j * ch:(j + 1) * ch]
        gcb16 = gcb_sum.astype(BF16)
        g_c = g_c + jnp.dot(gcb16, bb, preferred_element_type=F32)
        g_b = g_b + lax.dot_general(gcb16, cb, TN_DIMS, preferred_element_type=F32)
        g_xd = g_xd + lax.dot_general((cbm_st * lm_st).astype(BF16), gy_st, TN_DIMS, preferred_element_type=F32)
        m_st = gcb_st * cbm_st
        for j in range(hg):
            m_ls = m_st[j * ch:(j + 1) * ch]
            d_cs = jnp.sum(m_ls, axis=1, keepdims=True) - jnp.sum(m_ls.T, axis=1, keepdims=True)
            g_cs = g_cs + jnp.where(lane_head == j, d_cs, 0.0)
        extra = _colsum(g_w * w) + g_el * jnp.exp(q["cs_last"])
        g_cs = g_cs + jnp.where(lax.broadcasted_iota(jnp.int32, (ch, gw), 0) == ch - 1, extra, 0.0)
        g_la = _dot_split(upper, g_cs, 1)
        a_x = -jnp.exp(alx_ref[...])
        gdt_ref[...] = g_xd * xs + g_la * a_x * (1.0 / SSM_HEAD_DIM)
        ga_row = _colsum(g_la * (dtx * a_x)) * (1.0 / SSM_HEAD_DIM)
        ga_ref[...] = jnp.where(lax.broadcasted_iota(jnp.int32, (8, gw), 0) == 0, ga_row, 0.0)
        gxs_ref[...] = (g_xd * dtx + gy * dsk_ref[...]).astype(BF16)
        gb_ref[...] = g_b.astype(BF16)
        gc_ref[...] = g_c.astype(BF16)

    wide = pl.BlockSpec((ch, gp * gw), lambda cc, g: (rc(cc), g))
    narrow = pl.BlockSpec((ch, gp * SSM_STATE), lambda cc, g: (rc(cc), g))
    row = pl.BlockSpec((1, gp * gw), lambda cc, g: (0, g))
    return pl.pallas_call(
        body, grid=(nch, SSM_GROUPS // gp),
        in_specs=[wide,
                  pl.BlockSpec((ch, gp * SSM_STATE), lambda cc, g: (rc(cc), nb0 // gp + g)),
                  pl.BlockSpec((ch, gp * SSM_STATE), lambda cc, g: (rc(cc), (nb0 + SSM_GROUPS) // gp + g)),
                  wide, wide,
                  pl.BlockSpec((gp, 8, ch), lambda cc, g: (g, 0, rc(cc))),
                  row, wide,
                  pl.BlockSpec((None, gp, gw, SSM_STATE), lambda cc, g: (rc(cc), g, 0, 0)),
                  row],
        out_specs=[wide, narrow, narrow, wide, pl.BlockSpec((8, gp * gw), lambda cc, g: (rc(cc), g))],
        out_shape=[S((t, SSM_INNER), BF16), S((t, SSM_GROUPS * SSM_STATE), BF16), S((t, SSM_GROUPS * SSM_STATE), BF16),
                   S((t, SSM_INNER), F32), S((nch * 8, SSM_INNER), F32)],
        scratch_shapes=[pltpu.VMEM((SSM_GROUPS, gw, SSM_STATE), F32)],
        name="ssd_bwd", compiler_params=_params(("arbitrary", "arbitrary")),
    )(xa, xa, xa, dtx, csx, cst_g, alog_x, g_y, states, dskip_x)


def _local_step(x, target, w_pre, w_in_r, b_gate, conv_w, conv_b, dt_bias, a_log, d_skip, ssm_norm_w,
                late_weights, w_post, w_fpre, w_fpost, on_mid_grads, on_in_proj_grads):
    t = x.shape[0]
    mm = functools.partial(_matmul, tm=512)
    slopes = _slope_rows()
    hg = SSM_HEADS // SSM_GROUPS
    dt_bias_pad = jnp.pad(dt_bias, ((0, 0), (0, LANE - SSM_HEADS)))
    alog_x = jnp.repeat(a_log, SSM_HEAD_DIM, axis=1)
    alog_pad = jnp.pad(a_log, ((0, 0), (0, LANE - SSM_HEADS)))
    dskip_x = jnp.repeat(d_skip, SSM_HEAD_DIM, axis=1)

    u = _pre_norm(x, w_pre)
    pa = _matmul(u, w_in_r, mode="nn", out_dtype=BF16, name="in_proj_zgx", tm=1024, tn=2048, tk=D_MODEL, b_cols=(0, PA_W))
    pb, pb16 = _matmul(u, w_in_r[:, PA_W:], mode="nn", out_dtype=F32, name="in_proj_qkvdt", tm=1024, tn=PB_W // 2, tk=D_MODEL,
                       epilogue="also_bf16")
    dils = [dil for _, dil in DILATED_PATTERNS]
    wide = [_att_fwd((pb, pb16), dil, slopes) for dil in dils[1:]]
    att, lse = _att_fwd((pb, pb16), dils[0], slopes, others=wide)
    xa, xc = _conv_fwd(pa, conv_w, conv_b)
    dtx, csx, cst = _dt_fwd(pb, dt_bias_pad, alog_pad)
    cst_g = jnp.pad(cst[:SSM_HEADS].reshape(SSM_GROUPS, hg, t), ((0, 0), (0, 8 - hg), (0, 0)))
    y_ssd, states = _ssd_fwd(xa, dtx, csx, cst_g)
    y4 = _gate_norm_fwd(y_ssd, xa, pa, dskip_x, ssm_norm_w)
    w_att, w_ssm, w_out, w_up, w_down = late_weights(y4)
    att_p = mm(att, w_att, mode="nn", out_dtype=BF16, name="att_proj", tn=D_MODEL, tk=ATT_WIDTH)
    ssm_p = mm(y4, w_ssm, mode="nn", out_dtype=BF16, name="ssm_proj", tn=D_MODEL, tk=SSM_INNER)
    mixin = _gating_fwd(pa, b_gate, att_p, ssm_p)
    mixed = mm(mixin, w_out, mode="nn", out_dtype=F32, name="out_proj", tn=D_MODEL, tk=D_MODEL)
    h1, f = _mix_post_ffn_pre(x, mixed, w_post, w_fpre)
    act, up = _matmul(f, w_up, mode="nn", out_dtype=BF16, name="ffn_up", tm=2048, tn=FFN_HIDDEN // N_DEV, tk=D_MODEL, epilogue="relu2", stacked=True)
    dn = mm(act, w_down, mode="nn", out_dtype=F32, name="ffn_down", tn=D_MODEL, tk=FFN_HIDDEN)
    loss, g_h2, g_dn, gw_fpost = _loss_and_ffn_post_bwd(h1, dn, w_fpost, target)

    g_up = _matmul(g_dn, w_down, mode="nt", out_dtype=BF16, name="ffn_down_bwd_x", tm=1024, tn=2048, tk=D_MODEL, epilogue="relu2_bwd",
                   extra=up)
    gw_down = _matmul(act, g_dn, mode="tn", out_dtype=BF16, name="ffn_down_bwd_w", tm=1024, tn=D_MODEL, tk=2048)
    w_up_rows = jnp.moveaxis(w_up, 0, 1).reshape(D_MODEL, FFN_HIDDEN)
    g_f = mm(g_up, w_up_rows, mode="nt", out_dtype=F32, name="ffn_up_bwd_x", tn=D_MODEL, tk=FFN_HIDDEN)
    gw_up = _matmul(f, g_up, mode="tn", out_dtype=BF16, name="ffn_up_bwd_w", tm=D_MODEL, tn=FFN_HIDDEN // N_DEV, tk=2048, stacked=True)
    g_h1, g_mixed, gw_fpre, gw_post = _ffn_pre_mix_post_bwd(g_h2, g_f, h1, w_fpre, mixed, w_post)
    g_mixin = mm(g_mixed, w_out, mode="nt", out_dtype=BF16, name="out_proj_bwd_x", tn=D_MODEL, tk=D_MODEL)
    gw_out = _matmul(mixin, g_mixed, mode="tn", out_dtype=BF16, name="out_proj_bwd_w", tm=D_MODEL, tn=D_MODEL, tk=2048)
    g_proj = lax.empty((t, PROJ_W), BF16)
    g_att_p, g_ssm_p, g_b_gate, g_proj = _gating_bwd(g_mixin, pa, b_gate, att_p, ssm_p, g_proj)
    g_att = mm(g_att_p, w_att, mode="nt", out_dtype=F32, name="att_proj_bwd_x", tn=ATT_WIDTH, tk=D_MODEL)
    gw_att = _matmul(att, g_att_p, mode="tn", out_dtype=BF16, name="att_proj_bwd_w", tm=ATT_WIDTH, tn=D_MODEL, tk=2048)
    g_y4 = mm(g_ssm_p, w_ssm, mode="nt", out_dtype=BF16, name="ssm_proj_bwd_x", tn=SSM_INNER, tk=D_MODEL)
    gw_ssm = _matmul(y4, g_ssm_p, mode="tn", out_dtype=BF16, name="ssm_proj_bwd_w", tm=1024, tn=D_MODEL, tk=2048)
    token = on_mid_grads(dict(w_att_proj=gw_att, w_ssm_proj=gw_ssm, w_out=gw_out, w_up=gw_up, w_down=gw_down))
    if token is not None:
        ssm_norm_w = ssm_norm_w + jnp.tile(token[0:1, :], (1, SSM_INNER // LANE))
    g_y2, g_norm_w, _, g_d_skip, g_proj = _gate_norm_bwd(g_y4, y_ssd, xa, pa, dskip_x, ssm_norm_w, g_proj)
    g_xs, g_bm, g_cm, g_dtx, ga_rows = _ssd_bwd(xa, dtx, csx, cst_g, alog_x, g_y2, states, dskip_x)
    g_dt_bias, g_a_log, g_proj = _dt_bwd(g_dtx, ga_rows, pb, dt_bias_pad, g_proj)
    g_conv_b, gcw0, gcw1, gcw2, gcw3, g_proj = _conv_bwd(g_xs, g_bm, g_cm, xc, pa, conv_w, g_proj)
    delta = _att_delta(g_att, att)
    dqs, dks, dvs = [], [], []
    for _, dil in DILATED_PATTERNS:
        dq, dk, dv = _att_bwd((pb, pb16), g_att, lse, delta, dil, slopes)
        dqs.append(dq)
        dks.append(dk)
        dvs.append(dv)
    g_proj = _att_grad_sum(dqs, dks, dvs, g_proj)
    gw_in_r = _matmul(u, g_proj, mode="tn", out_dtype=BF16, name="in_proj_bwd_w", tm=D_MODEL, tn=1792, tk=2048)
    token = on_in_proj_grads(gw_in_r, jnp.concatenate([gcw0, gcw1, gcw2, gcw3], axis=0))
    g_u = _matmul(g_proj, w_in_r, mode="nt", out_dtype=F32, name="in_proj_bwd_x", tm=1024, tn=D_MODEL, tk=3584, after=token)
    g_x, gw_pre = _pre_norm_bwd(g_h1, g_u, x, w_pre)

    grads = dict(
        norm_mix_pre_w=gw_pre, b_gate=g_b_gate, conv_b=g_conv_b, dt_bias=g_dt_bias[:, :SSM_HEADS], a_log=g_a_log[:, :SSM_HEADS],
        d_skip=g_d_skip[:, :SSM_HEADS], ssm_norm_w=g_norm_w, norm_mix_post_w=gw_post, norm_ffn_pre_w=gw_fpre, norm_ffn_post_w=gw_fpost)
    return loss, g_x, grads


def _mesh_pos():
    return lax.axis_index("x"), lax.axis_index("y"), lax.axis_index("c")


def _all_gather(shards):
    n = len(shards)

    def body(*refs):
        x_refs, o_refs = refs[:n], refs[n:2 * n]
        send_sems, recv_sems, local_sems = refs[2 * n:]
        x, y, c = _mesh_pos()
        me, sibling = (x, y, c), (x, y, 1 - c)
        chips = [(1 - x, y), (x, 1 - y), (1 - x, 1 - y)]

        def copy(a, k, block, to, src=None):
            dst = o_refs[a].at[4 * block[0] + 2 * block[1] + block[2]]
            return pltpu.make_async_remote_copy(
                src_ref=dst if src is None else src, dst_ref=dst, send_sem=send_sems.at[7 * a + k], recv_sem=recv_sems.at[7 * a + k],
                device_id=to, device_id_type=pl.DeviceIdType.MESH)

        mine = [pltpu.make_async_copy(x_refs[a], o_refs[a].at[4 * x + 2 * y + c], local_sems.at[a]) for a in range(n)]
        for cp in mine:
            cp.start()
        first = []
        for a in range(n):
            first.append(copy(a, 0, me, sibling, src=x_refs[a]))
            first += [copy(a, 1 + j, me, (*chip, c), src=x_refs[a]) for j, chip in enumerate(chips)]
        for cp in first:
            cp.start()
        passed = []
        for j, chip in enumerate(chips):
            for a in range(n):
                copy(a, 1 + j, (*chip, c), me).wait_recv()
                passed.append(copy(a, 4 + j, (*chip, c), sibling))
                passed[-1].start()
        for a in range(n):
            copy(a, 0, sibling, me).wait_recv()
            for j, chip in enumerate(chips):
                copy(a, 4 + j, (*chip, 1 - c), me).wait_recv()
        for cp in first + passed:
            cp.wait_send()
        for cp in mine:
            cp.wait()

    hbm = pl.BlockSpec(memory_space=pltpu.HBM)
    return pl.pallas_call(
        body, out_shape=[S((N_DEV,) + s.shape, s.dtype) for s in shards],
        in_specs=[hbm] * n, out_specs=[hbm] * n,
        scratch_shapes=[pltpu.SemaphoreType.DMA((7 * n,)), pltpu.SemaphoreType.DMA((7 * n,)), pltpu.SemaphoreType.DMA((n,))],
        name="weights_all_gather",
    )(*shards)


def _exchange_grads(slab_arrays, small):
    n = len(slab_arrays)
    r_small = small.shape[0]

    def body(*refs):
        slab_refs, small_ref = refs[:n], refs[n]
        recv_refs, gsm_ref = refs[n + 1:2 * n + 1], refs[2 * n + 1]
        send_sems, recv_sems, local_sems = refs[2 * n + 2:]
        x, y, c = _mesh_pos()
        me = 4 * x + 2 * y + c

        def peer(k):
            px = 1 - x if k & 4 else x
            py = 1 - y if k & 2 else y
            pc = 1 - c if k & 1 else c
            return (px, py, pc), 4 * px + 2 * py + pc

        def copy(a, k, sending):
            to, lin = peer(k)
            sem = 7 * a + k - 1
            if a == n:
                src, dst = small_ref, gsm_ref.at[me if sending else lin]
            else:
                src, dst = slab_refs[a].at[lin], recv_refs[a].at[me if sending else lin]
            return pltpu.make_async_remote_copy(src_ref=src, dst_ref=dst, send_sem=send_sems.at[sem], recv_sem=recv_sems.at[sem],
                                                device_id=to, device_id_type=pl.DeviceIdType.MESH)

        own = [pltpu.make_async_copy(slab_refs[a].at[me], recv_refs[a].at[me], local_sems.at[a]) for a in range(n)]
        own.append(pltpu.make_async_copy(small_ref, gsm_ref.at[me], local_sems.at[n]))
        for cp in own:
            cp.start()
        order = [n] + list(range(n))
        sends = [copy(a, k, True) for a in order for k in range(1, N_DEV)]
        for cp in sends:
            cp.start()
        for a in order:
            for k in range(1, N_DEV):
                copy(a, k, False).wait_recv()
        for cp in sends:
            cp.wait_send()
        for cp in own:
            cp.wait()

    hbm = pl.BlockSpec(memory_space=pltpu.HBM)
    n_sem = 7 * (n + 1)
    res = pl.pallas_call(
        body, out_shape=[S(a.shape, a.dtype) for a in slab_arrays] + [S((N_DEV, r_small, LANE), small.dtype)],
        in_specs=[hbm] * (n + 1), out_specs=[hbm] * (n + 1),
        scratch_shapes=[pltpu.SemaphoreType.DMA((n_sem,)), pltpu.SemaphoreType.DMA((n_sem,)), pltpu.SemaphoreType.DMA((n + 1,))],
        name="grad_exchange",
    )(*slab_arrays, small)
    return res[:n], res[n]


def _peer_of(k, x, y, c):
    px = 1 - x if k & 4 else x
    py = 1 - y if k & 2 else y
    pc = 1 - c if k & 1 else c
    return (px, py, pc), 4 * px + 2 * py + pc


def _split_copies(src_refs, land_refs, send_sems, recv_sems, per_peer):
    x, y, c = _mesh_pos()
    me = 4 * x + 2 * y + c
    sends, recvs = [], []
    for a, (src, land) in enumerate(zip(src_refs, land_refs)):
        for k in range(1, N_DEV):
            to, lin = _peer_of(k, x, y, c)
            sem = 7 * a + k - 1
            piece = src.at[lin] if per_peer else src
            for slot, out in ((me, sends), (lin, recvs)):
                out.append(pltpu.make_async_remote_copy(
                    src_ref=piece, dst_ref=land.at[slot], send_sem=send_sems.at[sem], recv_sem=recv_sems.at[sem],
                    device_id=to, device_id_type=pl.DeviceIdType.MESH))
    return sends, recvs


def _remote_start(srcs, per_peer, name):
    n = len(srcs)
    lands = [lax.empty((N_DEV,) + (s.shape[1:] if per_peer else s.shape), s.dtype) for s in srcs]

    def body(*refs):
        src_refs, land_refs = refs[:n], refs[n:2 * n]
        send_sems, recv_sems = refs[2 * n], refs[2 * n + 1]
        token = refs[-1]
        sends, _ = _split_copies(src_refs, land_refs, send_sems, recv_sems, per_peer)
        for cp in sends:
            cp.start()
        token[...] = jnp.zeros_like(token)

    hbm = pl.BlockSpec(memory_space=pltpu.HBM)
    sem = pl.BlockSpec(memory_space=pltpu.SEMAPHORE)
    res = pl.pallas_call(
        body, name=name,
        out_shape=(pltpu.SemaphoreType.DMA((7 * n,)), pltpu.SemaphoreType.DMA((7 * n,)),
                   *[pltpu.HBM(a.shape, a.dtype) for a in srcs + lands], S((8, LANE), F32)),
        in_specs=[hbm] * (2 * n), out_specs=(sem, sem, *[hbm] * (2 * n), pl.BlockSpec(memory_space=pltpu.VMEM)),
        input_output_aliases={i: 2 + i for i in range(2 * n)},
        compiler_params=pltpu.CompilerParams(has_side_effects=pltpu.SideEffectType.DATAFLOW_SIDE_EFFECTING),
    )(*[pltpu.with_memory_space_constraint(a, pltpu.HBM) for a in srcs + lands])
    return dict(sems=res[:2], srcs=list(res[2:2 + n]), lands=list(res[2 + n:2 + 2 * n]), per_peer=per_peer), res[-1]


def _remote_wait(handle, after, name):
    n = len(handle["srcs"])
    per_peer = handle["per_peer"]

    def body(*refs):
        src_refs, land_refs = refs[:n], refs[n:2 * n]
        send_sems, recv_sems = refs[2 * n], refs[2 * n + 1]
        sends, recvs = _split_copies(src_refs, land_refs, send_sems, recv_sems, per_peer)
        for cp in sends:
            cp.wait_send()
        for cp in recvs:
            cp.wait_recv()

    hbm = pl.BlockSpec(memory_space=pltpu.HBM)
    sem = pl.BlockSpec(memory_space=pltpu.SEMAPHORE)
    arrays = handle["srcs"] + handle["lands"]
    res = pl.pallas_call(
        body, name=name, out_shape=tuple(pltpu.HBM(a.shape, a.dtype) for a in arrays),
        in_specs=[hbm] * (2 * n) + [sem, sem, pl.BlockSpec(memory_space=pl.ANY)], out_specs=tuple([hbm] * (2 * n)),
        input_output_aliases={i: i for i in range(2 * n)},
        compiler_params=pltpu.CompilerParams(has_side_effects=pltpu.SideEffectType.DATAFLOW_SIDE_EFFECTING),
    )(*arrays, *handle["sems"], after)
    return list(res[n:])


def _with_own(lands, own, me):
    return [lax.dynamic_update_index_in_dim(land, o.astype(land.dtype), me, 0) for land, o in zip(lands, own)]


def _adamw(w, m, v, slabs, name, tr):
    r, cols = w.shape
    c1 = 1.0 - ADAM_B1 ** ADAM_STEP
    c2 = 1.0 - ADAM_B2 ** ADAM_STEP

    def body(w_ref, m_ref, v_ref, s_ref, g_ref, d_ref, nm_ref, nv_ref):
        g = s_ref[0].astype(F32)
        for d in range(1, N_DEV):
            g = g + s_ref[d].astype(F32)
        nm = ADAM_B1 * m_ref[...] + (1.0 - ADAM_B1) * g
        nv = ADAM_B2 * v_ref[...] + (1.0 - ADAM_B2) * (g * g)
        g_ref[...] = g
        nm_ref[...] = nm
        nv_ref[...] = nv
        d_ref[...] = -ADAM_LR * ((nm / c1) / (jnp.sqrt(nv / c2) + ADAM_EPS) + ADAM_WD * w_ref[...])

    assert r % tr == 0, name
    blk = pl.BlockSpec((tr, cols), lambda i: (i, 0))
    return pl.pallas_call(
        body, grid=(r // tr,), in_specs=[blk, blk, blk, pl.BlockSpec((N_DEV, tr, cols), lambda i: (0, i, 0))],
        out_specs=[blk] * 4, out_shape=[S((r, cols), F32)] * 4, name=name, compiler_params=_params(("parallel",)),
    )(w, m, v, slabs)


BIG = ("w_in", "w_att_proj", "w_up", "w_ssm_proj", "w_out", "w_down", "conv_w")
ADAMW_ROWS = dict(w_in=256, w_att_proj=768, w_up=512, w_ssm_proj=256, w_out=128, w_down=256, conv_w=4)
SMALL = ("norm_mix_pre_w", "b_gate", "conv_b", "dt_bias", "a_log", "d_skip", "ssm_norm_w", "norm_mix_post_w",
         "norm_ffn_pre_w", "norm_ffn_post_w")
ORDER = ("norm_mix_pre_w", "w_in", "b_gate", "conv_w", "conv_b", "dt_bias", "a_log", "d_skip", "ssm_norm_w", "w_att_proj",
         "w_ssm_proj", "w_out", "norm_mix_post_w", "norm_ffn_pre_w", "w_up", "w_down", "norm_ffn_post_w")
ROW_SHARDED = ("w_ssm_proj", "w_out", "w_down")
LATE = ("w_att_proj", "w_ssm_proj", "w_out", "w_up", "w_down")
IN_PROJ_W = 10528
IN_SHARD_W = IN_PROJ_W // N_DEV
IN_SEGMENTS = ((2304, 4352), (8480, 10528), (4352, 8448), (0, 2304), (8448, 8480))


def _pack(parts, rows_multiple):
    flat = jnp.concatenate([p.reshape(-1) for p in parts])
    pad = (-flat.shape[0]) % (rows_multiple * LANE)
    return jnp.pad(flat, (0, pad)).reshape(-1, LANE)


def _unpack(flat2d, shapes):
    flat, out, off = flat2d.reshape(-1), [], 0
    for sh in shapes:
        n = int(np.prod(sh))
        out.append(flat[off:off + n].reshape(sh))
        off += n
    return out


def _restore_in_proj(wr):
    return jnp.concatenate([wr[:, OFF_QKV:OFF_QKV + 2304], wr[:, OFF_Z:OFF_Z + 2048], wr[:, OFF_XBC:OFF_XBC + 4096],
                            wr[:, OFF_DT:OFF_DT + 32], wr[:, OFF_GL:OFF_GL + 2048]], axis=1)


def _assemble_in_proj(g):
    pieces = []
    for lo, hi in IN_SEGMENTS:
        while lo < hi:
            d = lo // IN_SHARD_W
            end = min(hi, (d + 1) * IN_SHARD_W)
            pieces.append(g[d][:, lo - d * IN_SHARD_W:end - d * IN_SHARD_W])
            lo = end
    pieces.append(jnp.zeros((g.shape[1], PROJ_W - IN_PROJ_W), g.dtype))
    return jnp.concatenate(pieces, axis=1)


def _in_proj_slabs(wr):
    orig = _restore_in_proj(wr)
    return jnp.stack([orig[:, d * IN_SHARD_W:(d + 1) * IN_SHARD_W] for d in range(N_DEV)])


def kernel(x, norm_mix_pre_w, w_in, b_gate, conv_w, conv_b, dt_bias, a_log, d_skip, ssm_norm_w, w_att_proj, w_ssm_proj, w_out, norm_mix_post_w, norm_ffn_pre_w, w_up, w_down, norm_ffn_post_w, loss_target, m_norm_mix_pre_w, m_w_in, m_b_gate, m_conv_w, m_conv_b, m_dt_bias, m_a_log, m_d_skip, m_ssm_norm_w, m_w_att_proj, m_w_ssm_proj, m_w_out, m_norm_mix_post_w, m_norm_ffn_pre_w, m_w_up, m_w_down, m_norm_ffn_post_w, v_norm_mix_pre_w, v_w_in, v_b_gate, v_conv_w, v_conv_b, v_dt_bias, v_a_log, v_d_skip, v_ssm_norm_w, v_w_att_proj, v_w_ssm_proj, v_w_out, v_norm_mix_post_w, v_norm_ffn_pre_w, v_w_up, v_w_down, v_norm_ffn_post_w):
    w = dict(norm_mix_pre_w=norm_mix_pre_w, w_in=w_in, b_gate=b_gate, conv_w=conv_w, conv_b=conv_b, dt_bias=dt_bias, a_log=a_log,
             d_skip=d_skip, ssm_norm_w=ssm_norm_w, w_att_proj=w_att_proj, w_ssm_proj=w_ssm_proj, w_out=w_out,
             norm_mix_post_w=norm_mix_post_w, norm_ffn_pre_w=norm_ffn_pre_w, w_up=w_up, w_down=w_down, norm_ffn_post_w=norm_ffn_post_w)
    m = dict(norm_mix_pre_w=m_norm_mix_pre_w, w_in=m_w_in, b_gate=m_b_gate, conv_w=m_conv_w, conv_b=m_conv_b, dt_bias=m_dt_bias,
             a_log=m_a_log, d_skip=m_d_skip, ssm_norm_w=m_ssm_norm_w, w_att_proj=m_w_att_proj, w_ssm_proj=m_w_ssm_proj, w_out=m_w_out,
             norm_mix_post_w=m_norm_mix_post_w, norm_ffn_pre_w=m_norm_ffn_pre_w, w_up=m_w_up, w_down=m_w_down, norm_ffn_post_w=m_norm_ffn_post_w)
    v = dict(norm_mix_pre_w=v_norm_mix_pre_w, w_in=v_w_in, b_gate=v_b_gate, conv_w=v_conv_w, conv_b=v_conv_b, dt_bias=v_dt_bias,
             a_log=v_a_log, d_skip=v_d_skip, ssm_norm_w=v_ssm_norm_w, w_att_proj=v_w_att_proj, w_ssm_proj=v_w_ssm_proj, w_out=v_w_out,
             norm_mix_post_w=v_norm_mix_post_w, norm_ffn_pre_w=v_norm_ffn_pre_w, w_up=v_w_up, w_down=v_w_down, norm_ffn_post_w=v_norm_ffn_post_w)
    shard_shapes = {n: w[n].shape[1:] for n in ORDER}

    mx, my, mc = _mesh_pos()
    me = 4 * mx + 2 * my + mc

    g_in, g_conv = _all_gather([w["w_in"][0].astype(BF16), w["conv_w"][0]])
    conv_full = jnp.moveaxis(g_conv, 0, 1).reshape(SSM_CONV, CONV_DIM)
    late_shards = [w[n][0].astype(BF16) for n in LATE]
    late_handle, token = _remote_start(late_shards, False, "late_weights_start")
    w_pre = w["norm_mix_pre_w"] + jnp.tile(token[0:1, :], (1, D_MODEL // LANE))

    def late_weights(after):
        full = dict(zip(LATE, _with_own(_remote_wait(late_handle, after, "late_weights_wait"), late_shards, me)))
        for n in ROW_SHARDED:
            full[n] = full[n].reshape(-1, full[n].shape[2])
        w_att = jnp.moveaxis(full["w_att_proj"], 0, 1).reshape(ATT_WIDTH, D_MODEL)
        return w_att, full["w_ssm_proj"], full["w_out"], full["w_up"], full["w_down"]

    started = {}

    def start_exchange(tag, slabs):
        own = [lax.dynamic_index_in_dim(s, me, 0, keepdims=False) for s in slabs]
        handle, tok = _remote_start(slabs, True, tag + "_grads_start")
        started[tag] = (handle, own)
        return tok

    def on_mid_grads(g):
        slabs = dict(w_up=g["w_up"], w_att_proj=jnp.moveaxis(g["w_att_proj"].reshape(ATT_WIDTH, N_DEV, -1), 1, 0))
        for n in ROW_SHARDED:
            slabs[n] = g[n].reshape(N_DEV, -1, g[n].shape[1])
        return start_exchange("mid", [slabs[n] for n in LATE])

    def on_in_proj_grads(gw_in_r, g_conv_w):
        return start_exchange("in_proj", [_in_proj_slabs(gw_in_r), jnp.moveaxis(g_conv_w.reshape(SSM_CONV, N_DEV, -1), 1, 0)])

    loss, g_x, grads = _local_step(
        x[0], loss_target[0], w_pre, _assemble_in_proj(g_in), w["b_gate"], conv_full, w["conv_b"], w["dt_bias"], w["a_log"],
        w["d_skip"], w["ssm_norm_w"], late_weights, w["norm_mix_post_w"], w["norm_ffn_pre_w"], w["norm_ffn_post_w"],
        on_mid_grads, on_in_proj_grads)

    recv = {}
    for tag, names in (("mid", LATE), ("in_proj", ("w_in", "conv_w"))):
        handle, own = started[tag]
        recv.update(zip(names, _with_own(_remote_wait(handle, g_x, tag + "_grads_wait"), own, me)))
    small = _pack([grads[n].astype(F32) for n in SMALL], 8)
    _, small_all = _exchange_grads([], small)

    small_shapes = [shard_shapes[n] for n in SMALL]
    small_out = _adamw(*[_pack([d_[n][0] for n in SMALL], 8) for d_ in (w, m, v)], small_all, "adamw_replicated", small_all.shape[1])
    big_out = {n: _adamw(w[n][0], m[n][0], v[n][0], recv[n], "adamw_" + n, ADAMW_ROWS[n]) for n in BIG}
    res = []
    for which, small_flat in enumerate(small_out):
        vals = {n: big_out[n][which] for n in BIG}
        vals.update(zip(SMALL, _unpack(small_flat, small_shapes)))
        res.append([vals[n][None] for n in ORDER])
    g_out, d_out, m_out, v_out = res
    total = lax.psum(loss[0, 0], ("x", "y", "c"))
    return (total, g_x[None], *g_out, *d_out, *m_out, *v_out)
```

```python
import functools
import math

import jax
import jax.numpy as jnp
import numpy as np
from jax import lax
from jax.experimental import pallas as pl
from jax.experimental.pallas import tpu as pltpu

F32 = jnp.float32
BF16 = jnp.bfloat16

D_MODEL = 1024
HEAD_DIM = 64
N_ATT_HEADS = 12
ATT_WIDTH = N_ATT_HEADS * HEAD_DIM
DILATED_PATTERNS = ((128, 1), (512, 4), (2048, 16))
ATT_BLOCK = 128
SSM_INNER = 2048
SSM_HEAD_DIM = 64
SSM_HEADS = 32
SSM_GROUPS = 8
SSM_STATE = 128
SSM_CHUNK = 128
CONV_DIM = 4096
SSM_CONV = 4
FFN_HIDDEN = 4096
RMS_EPS = 1e-6
N_DEV = 8

ADAM_LR = 0.001
ADAM_B1 = 0.9
ADAM_B2 = 0.999
ADAM_EPS = 1e-08
ADAM_WD = 0.01
ADAM_STEP = 10

LANE = 128
OFF_Z, OFF_GL, OFF_XBC, OFF_QKV, OFF_DT = 0, 2048, 4096, 8192, 10496
PROJ_W = 10752
PA_W = OFF_QKV
PB_W = PROJ_W - OFF_QKV
PB_DT = OFF_DT - OFF_QKV
VMEM_LIMIT = 52 * 1024 * 1024
NEG = -1e30

HI = lax.Precision.HIGHEST
NT_DIMS = (((1,), (1,)), ((), ()))
TN_DIMS = (((0,), (0,)), ((), ()))
S = jax.ShapeDtypeStruct


def _params(sem):
    return pltpu.CompilerParams(dimension_semantics=sem, vmem_limit_bytes=VMEM_LIMIT)


def _matmul(a, b, *, mode, out_dtype, name, tm, tn, tk, epilogue=None, extra=None, stacked=False, after=None, b_cols=None):
    if mode == "nn":
        m, k = a.shape
        n = b.shape[0] * b.shape[2] if stacked else b.shape[1]
        col0 = 0
        if b_cols is not None:
            assert b_cols[0] % tn == 0, name
            col0, n = b_cols[0] // tn, b_cols[1]
        a_spec = pl.BlockSpec((tm, tk), lambda i, j, kk: (i, kk))
        b_spec = pl.BlockSpec((None, tk, tn), lambda i, j, kk: (j, kk, 0)) if stacked else pl.BlockSpec((tk, tn), lambda i, j, kk: (kk, col0 + j))
        dims = (((1,), (0,)), ((), ()))
    elif mode == "nt":
        m, k = a.shape
        n = b.shape[1] if stacked else b.shape[0]
        a_spec = pl.BlockSpec((tm, tk), lambda i, j, kk: (i, kk))
        b_spec = pl.BlockSpec((None, tn, tk), lambda i, j, kk: (kk, j, 0)) if stacked else pl.BlockSpec((tn, tk), lambda i, j, kk: (j, kk))
        dims = NT_DIMS
    else:
        (k, m), n = a.shape, b.shape[1]
        a_spec = pl.BlockSpec((tk, tm), lambda i, j, kk: (kk, i))
        b_spec = pl.BlockSpec((tk, tn), lambda i, j, kk: (kk, j))
        dims = TN_DIMS
    assert m % tm == 0 and n % tn == 0 and k % tk == 0, (name, m, n, k)
    if stacked:
        assert (tk if mode == "nt" else tn) * N_DEV == (k if mode == "nt" else n), name
    nk = k // tk
    o_spec = pl.BlockSpec((tm, tn), lambda i, j, kk: (i, j))
    in_specs, args = [a_spec, b_spec], [a, b]
    if epilogue == "relu2":
        out_shape = (S((m, n), BF16), S((m, n), BF16))
        out_specs = (o_spec, o_spec)
    elif epilogue == "also_bf16":
        out_shape = (S((m, n), out_dtype), S((m, n), BF16))
        out_specs = (o_spec, o_spec)
    elif stacked and mode == "tn":
        out_shape, out_specs = S((N_DEV, m, tn), out_dtype), pl.BlockSpec((None, tm, tn), lambda i, j, kk: (j, i, 0))
    else:
        out_shape, out_specs = S((m, n), out_dtype), o_spec
    if epilogue == "relu2_bwd":
        in_specs.append(o_spec)
        args.append(extra)
    n_in = len(args)
    if after is not None:
        in_specs.append(pl.BlockSpec(after.shape, lambda i, j, kk: (0,) * after.ndim))
        args.append(after)

    def finish(acc, refs):
        if epilogue == "relu2":
            r = jnp.maximum(acc, 0.0)
            refs[0][...] = (r * r).astype(BF16)
            refs[1][...] = acc.astype(BF16)
        elif epilogue == "also_bf16":
            refs[0][...] = acc.astype(out_dtype)
            refs[1][...] = acc.astype(BF16)
        elif epilogue == "relu2_bwd":
            up = refs[0][...].astype(F32)
            refs[1][...] = (acc * (2.0 * jnp.maximum(up, 0.0))).astype(out_dtype)
        else:
            refs[0][...] = acc.astype(out_dtype)

    def body(a_ref, b_ref, *rest):
        rest = rest[:n_in - 2] + rest[len(args) - 2:]
        part = lax.dot_general(a_ref[...].astype(BF16), b_ref[...].astype(BF16), dims, preferred_element_type=F32)
        if nk == 1:
            finish(part, rest)
            return
        acc_ref = rest[-1]
        kk = pl.program_id(2)

        @pl.when(kk == 0)
        def _():
            acc_ref[...] = part

        @pl.when(kk > 0)
        def _():
            acc_ref[...] += part

        @pl.when(kk == nk - 1)
        def _():
            finish(acc_ref[...], rest[:-1])

    scratch = [] if nk == 1 else [pltpu.VMEM((tm, tn), F32)]
    return pl.pallas_call(
        body, grid=(m // tm, n // tn, nk), in_specs=in_specs, out_specs=out_specs, out_shape=out_shape,
        scratch_shapes=scratch, name=name, compiler_params=_params(("parallel", "parallel", "arbitrary")),
    )(*args)


def _rowcall(body, name, n_rows, tr, ins, outs, scratch=(), into=None):
    in_specs = [pl.BlockSpec(bs, im) for _, bs, im in ins]
    out_specs = [pl.BlockSpec(bs, im) for _, _, bs, im in outs]
    out_shape = [S(sh, dt) for sh, dt, _, _ in outs]
    args = [a for a, _, _ in ins]
    aliases = {}
    kernel = body
    if into is not None:
        buf, bs, im = into
        n_in = len(args)
        in_specs.append(pl.BlockSpec(memory_space=pl.ANY))
        args.append(buf)
        out_specs.append(pl.BlockSpec(bs, im))
        out_shape.append(S(buf.shape, buf.dtype))
        aliases = {n_in: len(out_shape) - 1}

        def kernel(*refs):
            body(*refs[:n_in], *refs[n_in + 1:])

    return pl.pallas_call(
        kernel, grid=(n_rows // tr,), in_specs=in_specs, out_specs=out_specs, out_shape=out_shape,
        input_output_aliases=aliases, scratch_shapes=list(scratch), name=name, compiler_params=_params(("arbitrary",)),
    )(*args)


def _rows(arr, tr, width=None, cb=0):
    width = arr.shape[1] if width is None else width
    return (arr, (tr, width), lambda i, cb=cb: (i, cb))


def _whole(arr):
    nd = arr.ndim
    return (arr, arr.shape, lambda i, nd=nd: (0,) * nd)


def _orow(n_rows, width, dtype, tr):
    return ((n_rows, width), dtype, (tr, width), lambda i: (i, 0))


def _oacc(width):
    return ((1, width), F32, (1, width), lambda i: (0, 0))


def _accumulate(ref, value):
    first = pl.program_id(0) == 0

    @pl.when(first)
    def _():
        ref[...] = value

    @pl.when(jnp.logical_not(first))
    def _():
        ref[...] += value


def _colsum(v):
    return jnp.sum(v, axis=0, keepdims=True)


def _rms_fwd(x, w):
    r = lax.rsqrt(jnp.mean(x * x, axis=-1, keepdims=True) + RMS_EPS)
    return x * r * w


def _rms_bwd(gy, x, w):
    r = lax.rsqrt(jnp.mean(x * x, axis=-1, keepdims=True) + RMS_EPS)
    xn = x * r
    gxn = gy * w
    gx = r * (gxn - xn * jnp.mean(gxn * xn, axis=-1, keepdims=True))
    return gx, _colsum(gy * xn)


def _sigmoid(x):
    return 1.0 / (1.0 + jnp.exp(-x))


def _head_expand(n_heads_pad, n_heads, width):
    h = lax.broadcasted_iota(jnp.int32, (n_heads_pad, n_heads * width), 0)
    c = lax.broadcasted_iota(jnp.int32, (n_heads_pad, n_heads * width), 1)
    return (c // width == h).astype(F32)


def _head_reduce(n_heads, width, n_heads_pad):
    c = lax.broadcasted_iota(jnp.int32, (n_heads * width, n_heads_pad), 0)
    h = lax.broadcasted_iota(jnp.int32, (n_heads * width, n_heads_pad), 1)
    return (c // width == h).astype(F32)


def _block_ones(n, width):
    r = lax.broadcasted_iota(jnp.int32, (n, n), 0)
    c = lax.broadcasted_iota(jnp.int32, (n, n), 1)
    return (r // width == c // width).astype(F32)


def _pre_norm(x, w_pre, tr=512):
    t = x.shape[0]

    def body(x_ref, w_ref, u_ref):
        u_ref[...] = _rms_fwd(x_ref[...], w_ref[...]).astype(BF16)

    return _rowcall(body, "pre_norm", t, tr, [_rows(x, tr), _whole(w_pre)], [_orow(t, D_MODEL, BF16, tr)])[0]


CONV_HALO = 16


def _row_shift(cur, halo, j):
    tr = cur.shape[0]
    r = lax.broadcasted_iota(jnp.int32, (tr, tr), 0)
    c = lax.broadcasted_iota(jnp.int32, (tr, tr), 1)
    main = jnp.dot((c == r + j).astype(BF16), cur, preferred_element_type=F32)
    er = lax.broadcasted_iota(jnp.int32, (CONV_HALO, CONV_HALO), 0)
    ec = lax.broadcasted_iota(jnp.int32, (CONV_HALO, CONV_HALO), 1)
    if j < 0:
        edge = jnp.dot((ec == CONV_HALO + er + j).astype(BF16), halo, preferred_element_type=F32)
        return jnp.concatenate([main[:CONV_HALO] + edge, main[CONV_HALO:]], axis=0)
    edge = jnp.dot((ec == er + j - CONV_HALO).astype(BF16), halo, preferred_element_type=F32)
    return jnp.concatenate([main[:tr - CONV_HALO], main[tr - CONV_HALO:] + edge], axis=0)


def _conv_fwd(proj, conv_w, conv_b, tr=256):
    t = proj.shape[0]
    cb = OFF_XBC // CONV_DIM
    halo = (proj, (CONV_HALO, CONV_DIM), lambda i: (jnp.maximum(i * (tr // CONV_HALO) - 1, 0), cb))

    def body(cur_ref, prev_ref, w_ref, b_ref, o_ref, xc_ref):
        cur = cur_ref[...]
        prev = jnp.where(pl.program_id(0) > 0, prev_ref[...], jnp.zeros_like(prev_ref[...]))
        acc = b_ref[...] + w_ref[3:4, :] * cur.astype(F32)
        for k in range(SSM_CONV - 1):
            acc = acc + w_ref[k:k + 1, :] * _row_shift(cur, prev, -(SSM_CONV - 1 - k))
        o_ref[...] = (acc * _sigmoid(acc)).astype(BF16)
        xc_ref[...] = acc.astype(BF16)

    return _rowcall(body, "conv_fwd", t, tr, [_rows(proj, tr, CONV_DIM, cb), halo, _whole(conv_w), _whole(conv_b)],
                    [_orow(t, CONV_DIM, BF16, tr), _orow(t, CONV_DIM, BF16, tr)])


def _dt_fwd(proj, dt_bias_pad, alog_pad, tr=512):
    t = proj.shape[0]

    def body(raw_ref, b_ref, al_ref, dtx_ref, csx_ref, cst_ref):
        v = raw_ref[...] + b_ref[...]
        dt = jnp.maximum(v, 0.0) + jnp.log1p(jnp.exp(-jnp.abs(v)))
        expand = _head_expand(LANE, SSM_HEADS, SSM_HEAD_DIM)
        dtx_ref[...] = _dot_split(dt, expand, 0, 3)
        la = dt * (-jnp.exp(al_ref[...]))
        row = lax.broadcasted_iota(jnp.int32, (SSM_CHUNK, SSM_CHUNK), 0)
        col = lax.broadcasted_iota(jnp.int32, (SSM_CHUNK, SSM_CHUNK), 1)
        tril = (col <= row).astype(F32)
        cs = jnp.concatenate([_dot_split(tril, la[k * SSM_CHUNK:(k + 1) * SSM_CHUNK, :], 1, 3) for k in range(tr // SSM_CHUNK)], axis=0)
        csx_ref[...] = _dot_split(cs, expand, 0, 3)
        cst_ref[...] = cs.T

    return _rowcall(body, "dt_fwd", t, tr, [_rows(proj, tr, LANE, PB_DT // LANE), _whole(dt_bias_pad), _whole(alog_pad)],
                    [_orow(t, SSM_INNER, F32, tr), _orow(t, SSM_INNER, F32, tr), ((LANE, t), F32, (LANE, tr), lambda i: (0, i))])


def _gate_norm_fwd(y_ssd, xa, proj, dskip_x, norm_w, tr=256):
    t = y_ssd.shape[0]
    gw = SSM_INNER // SSM_GROUPS

    def body(y_ref, xs_ref, z_ref, d_ref, w_ref, o_ref):
        z = z_ref[...].astype(F32)
        y3 = (y_ref[...].astype(F32) + d_ref[...] * xs_ref[...].astype(F32)) * (z * _sigmoid(z))
        for g in range(SSM_GROUPS):
            sl = slice(g * gw, (g + 1) * gw)
            o_ref[:, sl] = _rms_fwd(y3[:, sl], w_ref[:, sl]).astype(BF16)

    return _rowcall(body, "gate_norm_fwd", t, tr,
                    [_rows(y_ssd, tr), _rows(xa, tr, SSM_INNER, 0), _rows(proj, tr, SSM_INNER, OFF_Z // SSM_INNER), _whole(dskip_x), _whole(norm_w)],
                    [_orow(t, SSM_INNER, BF16, tr)])[0]


def _gating_fwd(proj, b_gate, att_p, ssm_p, tr=512):
    t = proj.shape[0]

    def body(gl_ref, b_ref, a_ref, s_ref, o_ref):
        gates = _sigmoid(gl_ref[...].astype(F32) + b_ref[...])
        o_ref[...] = (gates[:, :D_MODEL] * a_ref[...].astype(F32) + gates[:, D_MODEL:] * s_ref[...].astype(F32)).astype(BF16)

    return _rowcall(body, "gating_fwd", t, tr, [_rows(proj, tr, 2 * D_MODEL, OFF_GL // (2 * D_MODEL)), _whole(b_gate), _rows(att_p, tr), _rows(ssm_p, tr)],
                    [_orow(t, D_MODEL, BF16, tr)])[0]


def _mix_post_ffn_pre(x, mixed, w_post, w_fpre, tr=512):
    t = x.shape[0]

    def body(x_ref, m_ref, wp_ref, wf_ref, h1_ref, f_ref):
        h1 = x_ref[...] + _rms_fwd(m_ref[...], wp_ref[...])
        h1_ref[...] = h1
        f_ref[...] = _rms_fwd(h1, wf_ref[...]).astype(BF16)

    return _rowcall(body, "mix_post_ffn_pre", t, tr, [_rows(x, tr), _rows(mixed, tr), _whole(w_post), _whole(w_fpre)],
                    [_orow(t, D_MODEL, F32, tr), _orow(t, D_MODEL, BF16, tr)])


def _loss_and_ffn_post_bwd(h1, dn, w_fpost, target, tr=512):
    t = h1.shape[0]

    def body(h1_ref, dn_ref, w_ref, tg_ref, loss_ref, gh2_ref, gdn_ref, gw_ref):
        dn = dn_ref[...]
        w = w_ref[...]
        err = h1_ref[...] + _rms_fwd(dn, w) - tg_ref[...]
        _accumulate(loss_ref, jnp.zeros((1, LANE), F32) + 0.5 * jnp.sum(jnp.mean(err * err, axis=-1, keepdims=True)))
        gh2 = err * (1.0 / D_MODEL)
        gh2_ref[...] = gh2
        gdn, gw = _rms_bwd(gh2, dn, w)
        gdn_ref[...] = gdn.astype(BF16)
        _accumulate(gw_ref, gw)

    return _rowcall(body, "loss_ffn_post_bwd", t, tr, [_rows(h1, tr), _rows(dn, tr), _whole(w_fpost), _rows(target, tr)],
                    [_oacc(LANE), _orow(t, D_MODEL, F32, tr), _orow(t, D_MODEL, BF16, tr), _oacc(D_MODEL)])


def _ffn_pre_mix_post_bwd(g_h2, g_f, h1, w_fpre, mixed, w_post, tr=512):
    t = h1.shape[0]

    def body(gh2_ref, gf_ref, h1_ref, wf_ref, m_ref, wp_ref, gh1_ref, gm_ref, gwf_ref, gwp_ref):
        gx, gwf = _rms_bwd(gf_ref[...], h1_ref[...], wf_ref[...])
        gh1 = gh2_ref[...] + gx
        gh1_ref[...] = gh1
        gm, gwp = _rms_bwd(gh1, m_ref[...], wp_ref[...])
        gm_ref[...] = gm.astype(BF16)
        _accumulate(gwf_ref, gwf)
        _accumulate(gwp_ref, gwp)

    return _rowcall(body, "ffn_pre_mix_post_bwd", t, tr,
                    [_rows(g_h2, tr), _rows(g_f, tr), _rows(h1, tr), _whole(w_fpre), _rows(mixed, tr), _whole(w_post)],
                    [_orow(t, D_MODEL, F32, tr), _orow(t, D_MODEL, BF16, tr), _oacc(D_MODEL), _oacc(D_MODEL)])


def _gating_bwd(g_mixin, proj, b_gate, att_p, ssm_p, g_proj, tr=512):
    t = proj.shape[0]

    def body(gm_ref, gl_ref, b_ref, a_ref, s_ref, ga_ref, gs_ref, gb_ref, ggl_ref):
        gates = _sigmoid(gl_ref[...].astype(F32) + b_ref[...])
        gm = gm_ref[...].astype(F32)
        g_att, g_ssm = gates[:, :D_MODEL], gates[:, D_MODEL:]
        ga_ref[...] = (gm * g_att).astype(BF16)
        gs_ref[...] = (gm * g_ssm).astype(BF16)
        ggl_a = gm * a_ref[...].astype(F32) * g_att * (1.0 - g_att)
        ggl_s = gm * s_ref[...].astype(F32) * g_ssm * (1.0 - g_ssm)
        ggl_ref[:, :D_MODEL] = ggl_a.astype(BF16)
        ggl_ref[:, D_MODEL:] = ggl_s.astype(BF16)
        _accumulate(gb_ref.at[:, :D_MODEL], _colsum(ggl_a))
        _accumulate(gb_ref.at[:, D_MODEL:], _colsum(ggl_s))

    return _rowcall(body, "gating_bwd", t, tr,
                    [_rows(g_mixin, tr), _rows(proj, tr, 2 * D_MODEL, OFF_GL // (2 * D_MODEL)), _whole(b_gate), _rows(att_p, tr), _rows(ssm_p, tr)],
                    [_orow(t, D_MODEL, BF16, tr), _orow(t, D_MODEL, BF16, tr), _oacc(2 * D_MODEL)],
                    into=(g_proj, (tr, 2 * D_MODEL), lambda i: (i, OFF_GL // (2 * D_MODEL))))


def _gate_norm_bwd(g_y4, y_ssd, xa, proj, dskip_x, norm_w, g_proj, tr=256):
    t = y_ssd.shape[0]
    gw = SSM_INNER // SSM_GROUPS

    def body(g_ref, y_ref, xs_ref, z_ref, d_ref, w_ref, gy2_ref, gnw_ref, gdx_ref, gd_ref, gz_ref):
        z = z_ref[...].astype(F32)
        xs = xs_ref[...].astype(F32)
        sg = _sigmoid(z)
        sz = z * sg
        y2 = y_ref[...].astype(F32) + d_ref[...] * xs
        y3 = y2 * sz
        g4 = g_ref[...].astype(F32)
        for g in range(SSM_GROUPS):
            sl = slice(g * gw, (g + 1) * gw)
            gy3, gnw = _rms_bwd(g4[:, sl], y3[:, sl], w_ref[:, sl])
            _accumulate(gnw_ref.at[:, sl], gnw)
            gy2 = gy3 * sz[:, sl]
            gy2_ref[:, sl] = gy2.astype(BF16)
            gz_ref[:, sl] = (gy3 * y2[:, sl] * (sg[:, sl] * (1.0 + z[:, sl] * (1.0 - sg[:, sl])))).astype(BF16)
            _accumulate(gdx_ref.at[:, sl], _colsum(gy2 * xs[:, sl]))
        tot = jnp.broadcast_to(gdx_ref[...], (8, SSM_INNER))
        gd_ref[...] = jnp.dot(tot, _head_reduce(SSM_HEADS, SSM_HEAD_DIM, LANE), precision=HI, preferred_element_type=F32)[0:1, :]

    return _rowcall(body, "gate_norm_bwd", t, tr,
                    [_rows(g_y4, tr), _rows(y_ssd, tr), _rows(xa, tr, SSM_INNER, 0), _rows(proj, tr, SSM_INNER, OFF_Z // SSM_INNER), _whole(dskip_x), _whole(norm_w)],
                    [_orow(t, SSM_INNER, BF16, tr), _oacc(SSM_INNER), _oacc(SSM_INNER), _oacc(LANE)],
                    into=(g_proj, (tr, SSM_INNER), lambda i: (i, OFF_Z // SSM_INNER)))


def _dt_bwd(g_dtx, ga_rows, proj, dt_bias_pad, g_proj, tr=512):
    t = proj.shape[0]
    tail = PROJ_W - OFF_DT

    def body(g_ref, ga_ref, raw_ref, b_ref, gb_ref, gal_ref, o_ref):
        red = _head_reduce(SSM_HEADS, SSM_HEAD_DIM, LANE)
        gdt = _dot_split(g_ref[...], red, 0, 3)
        graw = gdt * _sigmoid(raw_ref[...] + b_ref[...])
        o_ref[...] = jnp.concatenate([graw.astype(BF16), jnp.zeros((tr, tail - LANE), BF16)], axis=1)
        _accumulate(gb_ref, _colsum(graw))
        tot = jnp.broadcast_to(_colsum(ga_ref[...]), (8, SSM_INNER))
        gal_ref[...] = jnp.dot(tot, red, precision=HI, preferred_element_type=F32)[0:1, :]

    return _rowcall(body, "dt_bwd", t, tr, [_rows(g_dtx, tr), _whole(ga_rows), _rows(proj, tr, LANE, PB_DT // LANE), _whole(dt_bias_pad)],
                    [_oacc(LANE), _oacc(LANE)], into=(g_proj, (tr, tail), lambda i: (i, OFF_DT // tail)))


def _conv_bwd(g_xs, g_b, g_c, xc, proj, conv_w, g_proj, tr=256):
    t = proj.shape[0]
    n_blk = t // tr
    cb = OFF_XBC // CONV_DIM
    nb, nc = SSM_INNER, SSM_INNER + SSM_GROUPS * SSM_STATE
    def nxt(arr, width):
        return (arr, (CONV_HALO, width), lambda i: (jnp.minimum((i + 1) * (tr // CONV_HALO), t // CONV_HALO - 1), 0))

    def body(gxs_ref, gxs_n, gb_ref, gb_n, gc_ref, gc_n, xc_ref, xc_n, x_ref, w_ref, gcb_ref, gw0, gw1, gw2, gw3, o_ref):
        def gxc_of(gxs, gb, gc, xc, keep):
            xcf = xc[...].astype(F32)
            sg = _sigmoid(xcf)
            dsilu = jnp.where(keep, sg * (1.0 + xcf * (1.0 - sg)), 0.0)
            return jnp.concatenate([gxs[...] * dsilu[:, :nb], gb[...] * dsilu[:, nb:nc], gc[...] * dsilu[:, nc:]], axis=1)

        gxc = gxc_of(gxs_ref, gb_ref, gc_ref, xc_ref, True)
        gxc16 = gxc.astype(BF16)
        nxt16 = gxc_of(gxs_n, gb_n, gc_n, xc_n, pl.program_id(0) < n_blk - 1).astype(BF16)
        x = x_ref[...].astype(F32)
        acc = w_ref[3:4, :] * gxc
        _accumulate(gw3, _colsum(gxc * x))
        _accumulate(gcb_ref, _colsum(gxc))
        for k, gw in enumerate((gw0, gw1, gw2)):
            shifted = _row_shift(gxc16, nxt16, SSM_CONV - 1 - k)
            acc = acc + w_ref[k:k + 1, :] * shifted
            _accumulate(gw, _colsum(shifted * x))
        o_ref[...] = acc.astype(BF16)

    ins = []
    for arr, width in ((g_xs, SSM_INNER), (g_b, nc - nb), (g_c, nc - nb), (xc, CONV_DIM)):
        ins += [_rows(arr, tr), nxt(arr, width)]
    ins += [_rows(proj, tr, CONV_DIM, cb), _whole(conv_w)]
    return _rowcall(body, "conv_bwd", t, tr, ins, [_oacc(CONV_DIM)] * 5, into=(g_proj, (tr, CONV_DIM), lambda i: (i, cb)))


def _pre_norm_bwd(g_h1, g_u, x, w_pre, tr=512):
    t = x.shape[0]

    def body(gh_ref, gu_ref, x_ref, w_ref, gx_ref, gw_ref):
        gx, gw = _rms_bwd(gu_ref[...], x_ref[...], w_ref[...])
        gx_ref[...] = gh_ref[...] + gx
        _accumulate(gw_ref, gw)

    return _rowcall(body, "pre_norm_bwd", t, tr, [_rows(g_h1, tr), _rows(g_u, tr), _rows(x, tr), _whole(w_pre)],
                    [_orow(t, D_MODEL, F32, tr), _oacc(D_MODEL)])


def _alibi_slopes(n):
    def pow2(m):
        start = 2.0 ** (-8.0 / m)
        return [start ** (i + 1) for i in range(m)]
    if (n & (n - 1)) == 0:
        s = pow2(n)
    else:
        c = 2 ** int(math.floor(math.log2(n)))
        s = pow2(c) + pow2(2 * c)[0::2][: n - c]
    return np.array(s, dtype=np.float32)


def _slope_rows():
    s = _alibi_slopes(N_ATT_HEADS).reshape(N_ATT_HEADS // 2, 2)
    return jnp.asarray(np.broadcast_to(np.repeat(s, HEAD_DIM, axis=1)[:, None, :], (N_ATT_HEADS // 2, 8, LANE)).copy())


ATT_MAX_BLOCK_ROWS = 2048


RESIDUE_MAJOR_FROM = 16


class _AttLayout:
    def __init__(self, t, dil):
        self.t, self.dil = t, dil
        self.rows = t // dil
        self.residue_major = dil >= RESIDUE_MAJOR_FROM
        if self.residue_major:
            bq, self.stride = min(512, self.rows), 1
        else:
            bq, self.stride = min(512, self.rows, ATT_MAX_BLOCK_ROWS // dil), dil
        self.nsub = bq // ATT_BLOCK
        self.nblk = self.rows // bq
        self.rb = bq * self.stride
        self.pb = ATT_BLOCK * self.stride
        self.n_pb = self.rows * self.stride // self.pb
        self.out_dtype = F32 if self.stride > 1 else BF16
        self.pairs = 1 if self.stride > 1 else 2

    def qkv(self, proj):
        pb, pb16 = proj
        if self.residue_major:
            return pb16.reshape(self.rows, self.dil * PB_W), PB_W // LANE, 0
        return (pb if self.stride > 1 else pb16), 0, 0

    def act(self, a):
        return a.reshape(self.rows, self.dil * ATT_WIDTH) if self.residue_major else a

    def act_shape(self):
        return (self.rows, self.dil * ATT_WIDTH) if self.residue_major else (self.t, ATT_WIDTH)

    def col(self, r, band, c):
        return r * band + c if self.residue_major else c


def _residue_rows(r, stride, first_block, n_blocks=1):
    if stride == 1:
        return pl.ds(first_block * ATT_BLOCK, n_blocks * ATT_BLOCK)
    return pl.ds(r + first_block * ATT_BLOCK * stride, n_blocks * ATT_BLOCK, stride=stride)


def _lane_half():
    return lax.broadcasted_iota(jnp.int32, (ATT_BLOCK, LANE), 1) // HEAD_DIM


def _att_scores_mask(dil, first):
    iq = lax.broadcasted_iota(jnp.int32, (ATT_BLOCK, 2 * ATT_BLOCK), 0)
    jk = lax.broadcasted_iota(jnp.int32, (ATT_BLOCK, 2 * ATT_BLOCK), 1)
    dist = ATT_BLOCK + iq - jk
    valid = (dist >= 0) & (dist <= ATT_BLOCK) & (jnp.logical_not(first) | (jk >= ATT_BLOCK))
    return (dist * dil).astype(F32), valid


def _stack_heads(x):
    half = _lane_half()
    return jnp.concatenate([jnp.where(half == 0, x, jnp.zeros_like(x)), jnp.where(half == 1, x, jnp.zeros_like(x))], axis=0)


def _unstack_heads(x):
    return jnp.where(_lane_half() == 0, x[:ATT_BLOCK], x[ATT_BLOCK:])


def _head_columns(x):
    return jnp.concatenate([x[:, 0:1], x[:, HEAD_DIM:HEAD_DIM + 1]], axis=0)


def _stacked_bias(sl_ref, dist, valid):
    d2 = jnp.concatenate([dist, dist], axis=0)
    v2 = jnp.concatenate([valid, valid], axis=0)
    top = lax.broadcasted_iota(jnp.int32, d2.shape, 0) < ATT_BLOCK
    slope = jnp.where(top, sl_ref[0:1, 0:1], sl_ref[0:1, HEAD_DIM:HEAD_DIM + 1])
    return jnp.where(v2, -slope * d2, NEG)


def _pair_views(refs, e2, pairs, slope_at):
    return [ref.at[e2] if n == slope_at else (ref if pairs == 1 else ref.at[:, pl.ds(e2 * LANE, LANE)]) for n, ref in enumerate(refs)]


def _att_fwd(proj, dil, slopes, others=()):
    t = proj[0].shape[0]
    lay = _AttLayout(t, dil)
    assert not others or lay.stride == 1 and not lay.residue_major
    nsub, nblk, rb, pb = lay.nsub, lay.nblk, lay.rb, lay.pb
    src, band, qb = lay.qkv(proj)
    aw = ATT_WIDTH // LANE
    n_other = 2 * len(others)

    pairs = lay.pairs
    wide = pairs * LANE

    def spec(off, prev=False):
        if prev:
            return pl.BlockSpec((pb, wide), lambda hp, i, r: (jnp.maximum(i * nsub - 1, 0), lay.col(r, band, qb + off) // pairs + hp))
        return pl.BlockSpec((rb, wide), lambda hp, i, r: (i, lay.col(r, band, qb + off) // pairs + hp))

    o_spec = pl.BlockSpec((rb, wide), lambda hp, i, r: (i, lay.col(r, aw, 0) // pairs + hp))

    def body(*refs):
        for e2 in range(pairs):
            pair_body(*_pair_views(refs, e2, pairs, 5))

    def pair_body(q_ref, kc_ref, kp_ref, vc_ref, vp_ref, sl_ref, *rest):
        other_refs, (o_ref, lse_ref) = rest[:n_other], rest[n_other:]
        i, r = pl.program_id(1), pl.program_id(2)
        for sub in range(nsub):
            rs = _residue_rows(r, lay.stride, sub)
            q = (q_ref[rs, :] * (HEAD_DIM ** -0.5)).astype(BF16)
            if sub == 0:
                r0 = _residue_rows(r, lay.stride, 0)
                kk = jnp.concatenate([kp_ref[r0, :], kc_ref[rs, :]], axis=0).astype(BF16)
                vv = jnp.concatenate([vp_ref[r0, :], vc_ref[rs, :]], axis=0).astype(BF16)
                first = i == 0
            else:
                ks = _residue_rows(r, lay.stride, sub - 1, 2)
                kk, vv = kc_ref[ks, :].astype(BF16), vc_ref[ks, :].astype(BF16)
                first = jnp.bool_(False)
            dist, valid = _att_scores_mask(dil, first)
            s = lax.dot_general(_stack_heads(q), kk, NT_DIMS, preferred_element_type=F32) + _stacked_bias(sl_ref, dist, valid)
            m = jnp.max(s, axis=-1, keepdims=True)
            p = jnp.exp(s - m)
            l = jnp.sum(p, axis=-1, keepdims=True)
            out = _unstack_heads(jnp.dot(p.astype(BF16), vv, preferred_element_type=F32) / l)
            lse = _unstack_heads(jnp.broadcast_to(m + jnp.log(l), (2 * ATT_BLOCK, LANE)))
            if others:
                outs = [out] + [ref[rs, :].astype(F32) for ref in other_refs[0::2]]
                lses = [lse] + [ref[rs, :] for ref in other_refs[1::2]]
                top = functools.reduce(jnp.maximum, lses)
                ws = [jnp.exp(x - top) for x in lses]
                tot = functools.reduce(jnp.add, ws)
                out = functools.reduce(jnp.add, [w * o for w, o in zip(ws, outs)]) / tot
                lse = top + jnp.log(tot)
            o_ref[rs, :] = out.astype(lay.out_dtype)
            lse_ref[rs, :] = lse

    o, lse = pl.pallas_call(
        body, grid=(N_ATT_HEADS // 2 // pairs, nblk, dil),
        in_specs=[spec(0), spec(6), spec(6, True), spec(12), spec(12, True), pl.BlockSpec((pairs, 8, LANE), lambda hp, i, r: (hp, 0, 0))]
        + [o_spec] * n_other,
        out_specs=[o_spec, o_spec], out_shape=[S(lay.act_shape(), lay.out_dtype), S(lay.act_shape(), F32)],
        name=f"att_fwd_d{dil}", compiler_params=_params(("parallel", "parallel", "arbitrary")),
    )(src, src, src, src, src, slopes, *[a for pair in others for a in pair])
    return o.reshape(t, ATT_WIDTH), lse.reshape(t, ATT_WIDTH)


def _att_delta(g_att, att, tr=512):
    t = att.shape[0]

    def body(g_ref, a_ref, o_ref):
        prod = g_ref[...] * a_ref[...].astype(F32)
        o_ref[...] = _dot_split(prod, _block_ones(ATT_WIDTH, HEAD_DIM), 0, 3)

    return _rowcall(body, "att_delta", t, tr, [_rows(g_att, tr), _rows(att, tr)], [_orow(t, ATT_WIDTH, F32, tr)])[0]


def _att_bwd(proj, g_att, lse, delta, dil, slopes):
    t = proj[0].shape[0]
    lay = _AttLayout(t, dil)
    nsub, nblk, rb, pb, n_pb = lay.nsub, lay.nblk, lay.rb, lay.pb, lay.n_pb
    src, band, qb = lay.qkv(proj)
    aw = ATT_WIDTH // LANE

    def near(i, which):
        return jnp.maximum(i * nsub - 1, 0) if which == "prev" else jnp.minimum((i + 1) * nsub, n_pb - 1)

    pairs = lay.pairs
    wide = pairs * LANE

    def pspec(off, which=None):
        if which:
            return pl.BlockSpec((pb, wide), lambda hp, i, r: (near(i, which), lay.col(r, band, qb + off) // pairs + hp))
        return pl.BlockSpec((rb, wide), lambda hp, i, r: (i, lay.col(r, band, qb + off) // pairs + hp))

    def aspec(which=None):
        if which:
            return pl.BlockSpec((pb, wide), lambda hp, i, r: (near(i, which), lay.col(r, aw, 0) // pairs + hp))
        return pl.BlockSpec((rb, wide), lambda hp, i, r: (i, lay.col(r, aw, 0) // pairs + hp))

    scale = HEAD_DIM ** -0.5

    def body(*refs):
        for e2 in range(pairs):
            pair_body(*_pair_views(refs, e2, pairs, 12))

    def pair_body(q_ref, qn_ref, kc_ref, kp_ref, vc_ref, vp_ref, do_ref, don_ref, lse_ref, lsen_ref, dl_ref, dln_ref, sl_ref,
                  dq_ref, dk_ref, dv_ref):
        i, r = pl.program_id(1), pl.program_id(2)

        def tile_grads(q, do, lse_q, dl_q, kk, vv, dist, valid):
            q2, do2 = _stack_heads(q), _stack_heads(do)
            s = lax.dot_general(q2, kk, NT_DIMS, preferred_element_type=F32) + _stacked_bias(sl_ref, dist, valid)
            p = jnp.exp(s - _head_columns(lse_q))
            dp = lax.dot_general(do2, vv, NT_DIMS, preferred_element_type=F32)
            ds16 = (p * (dp - _head_columns(dl_q))).astype(BF16)
            dq = _unstack_heads(jnp.dot(ds16, kk, preferred_element_type=F32)) * scale
            dk = lax.dot_general(ds16, q2, TN_DIMS, preferred_element_type=F32)
            dv = lax.dot_general(p.astype(BF16), do2, TN_DIMS, preferred_element_type=F32)
            return dq, dk, dv

        carry_k = carry_v = None
        for sub in range(nsub):
            rs = _residue_rows(r, lay.stride, sub)
            q = (q_ref[rs, :] * scale).astype(BF16)
            do = do_ref[rs, :].astype(BF16)
            if sub == 0:
                r0 = _residue_rows(r, lay.stride, 0)
                kk = jnp.concatenate([kp_ref[r0, :], kc_ref[rs, :]], axis=0).astype(BF16)
                vv = jnp.concatenate([vp_ref[r0, :], vc_ref[rs, :]], axis=0).astype(BF16)
                first = i == 0
            else:
                ks = _residue_rows(r, lay.stride, sub - 1, 2)
                kk, vv = kc_ref[ks, :].astype(BF16), vc_ref[ks, :].astype(BF16)
                first = jnp.bool_(False)
            dist, valid = _att_scores_mask(dil, first)
            dq, dk2, dv2 = tile_grads(q, do, lse_ref[rs, :], dl_ref[rs, :], kk, vv, dist, valid)
            dq_ref[rs, :] = dq.astype(lay.out_dtype)
            if sub > 0:
                rp = _residue_rows(r, lay.stride, sub - 1)
                dk_ref[rp, :] = (carry_k + dk2[:ATT_BLOCK, :]).astype(lay.out_dtype)
                dv_ref[rp, :] = (carry_v + dv2[:ATT_BLOCK, :]).astype(lay.out_dtype)
            carry_k, carry_v = dk2[ATT_BLOCK:, :], dv2[ATT_BLOCK:, :]
        rl = _residue_rows(r, lay.stride, nsub - 1)
        rn = _residue_rows(r, lay.stride, 0)
        iq = lax.broadcasted_iota(jnp.int32, (ATT_BLOCK, ATT_BLOCK), 0)
        jk = lax.broadcasted_iota(jnp.int32, (ATT_BLOCK, ATT_BLOCK), 1)
        dist_i = ATT_BLOCK + iq - jk
        valid = (dist_i >= 0) & (dist_i <= ATT_BLOCK) & (i < nblk - 1)
        qn = (qn_ref[rn, :] * scale).astype(BF16)
        _, dk1, dv1 = tile_grads(qn, don_ref[rn, :].astype(BF16), lsen_ref[rn, :], dln_ref[rn, :],
                                 kc_ref[rl, :].astype(BF16), vc_ref[rl, :].astype(BF16), (dist_i * dil).astype(F32), valid)
        dk_ref[rl, :] = (carry_k + dk1).astype(lay.out_dtype)
        dv_ref[rl, :] = (carry_v + dv1).astype(lay.out_dtype)

    gv, lv, dlv = lay.act(g_att), lay.act(lse), lay.act(delta)
    dq, dk, dv = pl.pallas_call(
        body, grid=(N_ATT_HEADS // 2 // pairs, nblk, dil),
        in_specs=[pspec(0), pspec(0, "next"), pspec(6), pspec(6, "prev"), pspec(12), pspec(12, "prev"),
                  aspec(), aspec("next"), aspec(), aspec("next"), aspec(), aspec("next"),
                  pl.BlockSpec((pairs, 8, LANE), lambda hp, i, r: (hp, 0, 0))],
        out_specs=[aspec(), aspec(), aspec()], out_shape=[S(lay.act_shape(), lay.out_dtype)] * 3,
        name=f"att_bwd_d{dil}", compiler_params=_params(("parallel", "parallel", "arbitrary")),
    )(src, src, src, src, src, src, gv, gv, lv, lv, dlv, dlv, slopes)
    return dq.reshape(t, ATT_WIDTH), dk.reshape(t, ATT_WIDTH), dv.reshape(t, ATT_WIDTH)


def _att_grad_sum(dqs, dks, dvs, g_proj, tr=2048):
    t = dqs[0].shape[0]
    cw = 2 * LANE
    per = ATT_WIDTH // cw
    arrays = list(dqs) + list(dks) + list(dvs)
    n_pat = len(dqs)

    def body(*refs):
        o_ref = refs[-1]
        which = pl.program_id(1) // per
        tot = jnp.zeros((tr, cw), F32)
        for s in range(3):
            part = refs[s * n_pat][...].astype(F32)
            for g in range(1, n_pat):
                part = part + refs[s * n_pat + g][...].astype(F32)
            tot = jnp.where(which == s, part, tot)
        o_ref[...] = tot.astype(BF16)

    in_specs = [pl.BlockSpec((tr, cw), lambda i, c, s=s: (i, jnp.clip(c - per * s, 0, per - 1))) for s in range(3) for _ in range(n_pat)]
    in_specs.append(pl.BlockSpec(memory_space=pl.ANY))
    return pl.pallas_call(
        lambda *refs: body(*refs[:len(arrays)], refs[-1]), grid=(t // tr, 3 * per), in_specs=in_specs,
        out_specs=pl.BlockSpec((tr, cw), lambda i, c: (i, OFF_QKV // cw + c)), out_shape=S(g_proj.shape, g_proj.dtype),
        input_output_aliases={len(arrays): 0}, name="att_grad_sum", compiler_params=_params(("arbitrary", "arbitrary")),
    )(*arrays, g_proj)


def _ssd_common(xs, dtx, cs, cs_t):
    ch = SSM_CHUNK
    row = lax.broadcasted_iota(jnp.int32, (ch, ch), 0)
    col = lax.broadcasted_iota(jnp.int32, (ch, ch), 1)
    cs_last = cs[ch - 1:ch, :]
    return dict(tril=col <= row, row=row, col=col, cs=cs, cs_t=cs_t, cs_last=cs_last,
                e=jnp.exp(cs), w=jnp.exp(cs_last - cs), xd=xs * dtx)


def _dot_split(a, b, split, terms=2):
    ops = [a, b]
    rest = ops[split]
    other = ops[1 - split].astype(BF16)
    out = None
    for _ in range(terms):
        piece = rest.astype(BF16)
        rest = rest - piece.astype(F32)
        part = jnp.dot(other, piece, preferred_element_type=F32) if split == 1 else jnp.dot(piece, other, preferred_element_type=F32)
        out = part if out is None else out + part
    return out


def _decay_col(cs_t, heads_per_group):
    r = lax.broadcasted_iota(jnp.int32, (heads_per_group * SSM_HEAD_DIM, SSM_STATE), 0) // SSM_HEAD_DIM
    out = jnp.zeros((heads_per_group * SSM_HEAD_DIM, SSM_STATE), F32)
    for j in range(heads_per_group):
        out = jnp.where(r == j, jnp.exp(cs_t[j:j + 1, SSM_CHUNK - 1:SSM_CHUNK]), out)
    return out


SSD_GROUPS_PER_STEP = 8


def _ssd_specs(t):
    hg = SSM_HEADS // SSM_GROUPS
    gw = hg * SSM_HEAD_DIM
    nb0 = SSM_INNER // SSM_STATE
    return hg, gw, nb0


def _ssd_group_views(gi, gw, wide, narrow, stacked):
    w = [r.at[:, pl.ds(gi * gw, gw)] for r in wide]
    n = [r.at[:, pl.ds(gi * SSM_STATE, SSM_STATE)] for r in narrow]
    return w, n, [r.at[gi] for r in stacked]


def _ssd_fwd(xa, dtx, csx, cst_g):
    t = xa.shape[0]
    nch = t // SSM_CHUNK
    hg, gw, nb0 = _ssd_specs(t)
    ch = SSM_CHUNK
    gp = SSD_GROUPS_PER_STEP

    def body(xs_ref, b_ref, c_ref, dtx_ref, cs_ref, cst_ref, y_ref, st_ref, h_scr):
        for gi in range(gp):
            (xs_g, dtx_g, cs_g, y_g), (b_g, c_g), (cst_gi, st_g) = _ssd_group_views(
                gi, gw, (xs_ref, dtx_ref, cs_ref, y_ref), (b_ref, c_ref), (cst_ref, st_ref))
            group_body(pl.program_id(0), pl.program_id(1) * gp + gi, xs_g, b_g, c_g, dtx_g, cs_g, cst_gi, y_g, st_g, h_scr)

    def group_body(cc, g, xs_ref, b_ref, c_ref, dtx_ref, cs_ref, cst_ref, y_ref, st_ref, h_scr):
        @pl.when(cc == 0)
        def _():
            h_scr[g] = jnp.zeros((gw, SSM_STATE), F32)

        q = _ssd_common(xs_ref[...].astype(F32), dtx_ref[...], cs_ref[...], cst_ref[...])
        bb, cb = b_ref[...].astype(BF16), c_ref[...].astype(BF16)
        h = h_scr[g]
        st_ref[...] = h
        xd16 = q["xd"].astype(BF16)
        c_both = lax.dot_general(cb, jnp.concatenate([bb, h.astype(BF16)], axis=0), NT_DIMS, preferred_element_type=F32)
        cbm = c_both[:, :SSM_STATE]
        y = c_both[:, SSM_STATE:] * q["e"]
        lane_head = lax.broadcasted_iota(jnp.int32, (ch, gw), 1) // SSM_HEAD_DIM
        gmats, xds = [], []
        for j in range(hg):
            diff = q["cs"][:, j * SSM_HEAD_DIM:j * SSM_HEAD_DIM + 1] - q["cs_t"][j:j + 1, :]
            gmats.append((cbm * jnp.exp(jnp.where(q["tril"], diff, NEG))).astype(BF16))
            xds.append(jnp.where(lane_head == j, xd16, jnp.zeros_like(xd16)))
        y = y + jnp.dot(jnp.concatenate(gmats, axis=1), jnp.concatenate(xds, axis=0), preferred_element_type=F32)
        y_ref[...] = y.astype(BF16)
        s_new = lax.dot_general((q["xd"] * q["w"]).astype(BF16), bb, TN_DIMS, preferred_element_type=F32)
        h_scr[g] = _decay_col(q["cs_t"], hg) * h + s_new

    wide = pl.BlockSpec((ch, gp * gw), lambda cc, g: (cc, g))
    return pl.pallas_call(
        body, grid=(nch, SSM_GROUPS // gp),
        in_specs=[wide,
                  pl.BlockSpec((ch, gp * SSM_STATE), lambda cc, g: (cc, nb0 // gp + g)),
                  pl.BlockSpec((ch, gp * SSM_STATE), lambda cc, g: (cc, (nb0 + SSM_GROUPS) // gp + g)),
                  wide, wide,
                  pl.BlockSpec((gp, 8, ch), lambda cc, g: (g, 0, cc))],
        out_specs=[wide, pl.BlockSpec((None, gp, gw, SSM_STATE), lambda cc, g: (cc, g, 0, 0))],
        out_shape=[S((t, SSM_INNER), BF16), S((nch, SSM_GROUPS, gw, SSM_STATE), F32)],
        scratch_shapes=[pltpu.VMEM((SSM_GROUPS, gw, SSM_STATE), F32)],
        name="ssd_fwd", compiler_params=_params(("arbitrary", "arbitrary")),
    )(xa, xa, xa, dtx, csx, cst_g)


def _ssd_bwd(xa, dtx, csx, cst_g, alog_x, g_y, states, dskip_x):
    t = xa.shape[0]
    nch = t // SSM_CHUNK
    hg, gw, nb0 = _ssd_specs(t)
    ch = SSM_CHUNK
    gp = SSD_GROUPS_PER_STEP

    def rc(cc):
        return nch - 1 - cc

    def body(xs_ref, b_ref, c_ref, dtx_ref, cs_ref, cst_ref, alx_ref, gy_ref, st_ref, dsk_ref,
             gxs_ref, gb_ref, gc_ref, gdt_ref, ga_ref, gh_scr):
        for gi in range(gp):
            wide, narrow, stacked = _ssd_group_views(
                gi, gw, (xs_ref, dtx_ref, cs_ref, alx_ref, gy_ref, dsk_ref, gxs_ref, gdt_ref, ga_ref), (b_ref, c_ref, gb_ref, gc_ref),
                (cst_ref, st_ref))
            xs_g, dtx_g, cs_g, alx_g, gy_g, dsk_g, gxs_g, gdt_g, ga_g = wide
            b_g, c_g, gb_g, gc_g = narrow
            group_body(pl.program_id(0), pl.program_id(1) * gp + gi, xs_g, b_g, c_g, dtx_g, cs_g, stacked[0], alx_g, gy_g, stacked[1],
                       dsk_g, gxs_g, gb_g, gc_g, gdt_g, ga_g, gh_scr)

    def group_body(cc, g, xs_ref, b_ref, c_ref, dtx_ref, cs_ref, cst_ref, alx_ref, gy_ref, st_ref, dsk_ref,
                   gxs_ref, gb_ref, gc_ref, gdt_ref, ga_ref, gh_scr):
        @pl.when(cc == 0)
        def _():
            gh_scr[g] = jnp.zeros((gw, SSM_STATE), F32)

        xs, dtx = xs_ref[...].astype(F32), dtx_ref[...]
        q = _ssd_common(xs, dtx, cs_ref[...], cst_ref[...])
        cs, cs_t, e, w, xd = q["cs"], q["cs_t"], q["e"], q["w"], q["xd"]
        bb, cb = b_ref[...].astype(BF16), c_ref[...].astype(BF16)
        gy16 = gy_ref[...]
        gy = gy16.astype(F32)
        xd16 = xd.astype(BF16)
        h = st_ref[...]
        h16 = h.astype(BF16)
        ghn = gh_scr[g]
        ghn16 = ghn.astype(BF16)
        seg = _block_ones(gw, SSM_HEAD_DIM)
        c_both = lax.dot_general(cb, jnp.concatenate([bb, h16], axis=0), NT_DIMS, preferred_element_type=F32)
        cbm, chm = c_both[:, :SSM_STATE], c_both[:, SSM_STATE:]

        gye16 = (gy * e).astype(BF16)
        g_c = jnp.dot(gye16, h16, preferred_element_type=F32)
        gh_off = lax.dot_general(gye16, cb, TN_DIMS, preferred_element_type=F32)
        bgs = lax.dot_general(bb, ghn16, NT_DIMS, preferred_element_type=F32)
        g_xd = w * bgs
        head_sums = _dot_split(jnp.concatenate([gy * chm, xd * bgs], axis=0), seg, 0)
        g_e, g_w = head_sums[:ch], head_sums[ch:]
        g_b = jnp.dot((xd * w).astype(BF16), ghn16, preferred_element_type=F32)
        decay = _decay_col(cs_t, hg)
        gh_scr[g] = decay * ghn + gh_off
        rsum = jnp.sum(ghn * h, axis=1, keepdims=True)
        lane_head = lax.broadcasted_iota(jnp.int32, (ch, gw), 1) // SSM_HEAD_DIM
        lane_head1 = lax.broadcasted_iota(jnp.int32, (1, gw), 1) // SSM_HEAD_DIM
        g_el = jnp.zeros((1, gw), F32)
        g_cs = g_e * e - g_w * w
        upper = q["row"] <= q["col"]
        lms, gys = [], []
        for j in range(hg):
            g_el = jnp.where(lane_head1 == j, jnp.sum(rsum[j * SSM_HEAD_DIM:(j + 1) * SSM_HEAD_DIM, :], axis=0, keepdims=True), g_el)
            csc = cs[:, j * SSM_HEAD_DIM:j * SSM_HEAD_DIM + 1]
            csr = cs_t[j:j + 1, :]
            lms.append(jnp.exp(jnp.where(q["tril"], csc - csr, NEG)))
            gys.append(jnp.where(lane_head == j, gy16, jnp.zeros_like(gy16)))
        lm_st, gy_st = jnp.concatenate(lms, axis=0), jnp.concatenate(gys, axis=0)
        cbm_st = jnp.concatenate([cbm] * hg, axis=0)
        gcb_st = lax.dot_general(gy_st, xd16, NT_DIMS, preferred_element_type=F32) * lm_st
        gcb_sum = gcb_st[0:ch]
        for j in range(1, hg):
            gcb_sum = gcb_sum + gcb_st[j * ch:(j + 1) * ch]
        gcb16 = gcb_sum.astype(BF16)
        g_c = g_c + jnp.dot(gcb16, bb, preferred_element_type=F32)
        g_b = g_b + lax.dot_general(gcb16, cb, TN_DIMS, preferred_element_type=F32)
        g_xd = g_xd + lax.dot_general((cbm_st * lm_st).astype(BF16), gy_st, TN_DIMS, preferred_element_type=F32)
        m_st = gcb_st * cbm_st
        for j in range(hg):
            m_ls = m_st[j * ch:(j + 1) * ch]
            d_cs = jnp.sum(m_ls, axis=1, keepdims=True) - jnp.sum(m_ls.T, axis=1, keepdims=True)
            g_cs = g_cs + jnp.where(lane_head == j, d_cs, 0.0)
        extra = _colsum(g_w * w) + g_el * jnp.exp(q["cs_last"])
        g_cs = g_cs + jnp.where(lax.broadcasted_iota(jnp.int32, (ch, gw), 0) == ch - 1, extra, 0.0)
        g_la = _dot_split(upper, g_cs, 1)
        a_x = -jnp.exp(alx_ref[...])
        gdt_ref[...] = g_xd * xs + g_la * a_x * (1.0 / SSM_HEAD_DIM)
        ga_row = _colsum(g_la * (dtx * a_x)) * (1.0 / SSM_HEAD_DIM)
        ga_ref[...] = jnp.where(lax.broadcasted_iota(jnp.int32, (8, gw), 0) == 0, ga_row, 0.0)
        gxs_ref[...] = (g_xd * dtx + gy * dsk_ref[...]).astype(BF16)
        gb_ref[...] = g_b.astype(BF16)
        gc_ref[...] = g_c.astype(BF16)

    wide = pl.BlockSpec((ch, gp * gw), lambda cc, g: (rc(cc), g))
    narrow = pl.BlockSpec((ch, gp * SSM_STATE), lambda cc, g: (rc(cc), g))
    row = pl.BlockSpec((1, gp * gw), lambda cc, g: (0, g))
    return pl.pallas_call(
        body, grid=(nch, SSM_GROUPS // gp),
        in_specs=[wide,
                  pl.BlockSpec((ch, gp * SSM_STATE), lambda cc, g: (rc(cc), nb0 // gp + g)),
                  pl.BlockSpec((ch, gp * SSM_STATE), lambda cc, g: (rc(cc), (nb0 + SSM_GROUPS) // gp + g)),
                  wide, wide,
                  pl.BlockSpec((gp, 8, ch), lambda cc, g: (g, 0, rc(cc))),
                  row, wide,
                  pl.BlockSpec((None, gp, gw, SSM_STATE), lambda cc, g: (rc(cc), g, 0, 0)),
                  row],
        out_specs=[wide, narrow, narrow, wide, pl.BlockSpec((8, gp * gw), lambda cc, g: (rc(cc), g))],
        out_shape=[S((t, SSM_INNER), BF16), S((t, SSM_GROUPS * SSM_STATE), BF16), S((t, SSM_GROUPS * SSM_STATE), BF16),
                   S((t, SSM_INNER), F32), S((nch * 8, SSM_INNER), F32)],
        scratch_shapes=[pltpu.VMEM((SSM_GROUPS, gw, SSM_STATE), F32)],
        name="ssd_bwd", compiler_params=_params(("arbitrary", "arbitrary")),
    )(xa, xa, xa, dtx, csx, cst_g, alog_x, g_y, states, dskip_x)


def _local_step(x, target, w_pre, w_in_r, b_gate, conv_w, conv_b, dt_bias, a_log, d_skip, ssm_norm_w,
                late_weights, w_post, w_fpre, w_fpost, on_mid_grads, on_in_proj_grads):
    t = x.shape[0]
    mm = functools.partial(_matmul, tm=512)
    slopes = _slope_rows()
    hg = SSM_HEADS // SSM_GROUPS
    dt_bias_pad = jnp.pad(dt_bias, ((0, 0), (0, LANE - SSM_HEADS)))
    alog_x = jnp.repeat(a_log, SSM_HEAD_DIM, axis=1)
    alog_pad = jnp.pad(a_log, ((0, 0), (0, LANE - SSM_HEADS)))
    dskip_x = jnp.repeat(d_skip, SSM_HEAD_DIM, axis=1)

    u = _pre_norm(x, w_pre)
    pa = _matmul(u, w_in_r, mode="nn", out_dtype=BF16, name="in_proj_zgx", tm=1024, tn=2048, tk=D_MODEL, b_cols=(0, PA_W))
    pb, pb16 = _matmul(u, w_in_r[:, PA_W:], mode="nn", out_dtype=F32, name="in_proj_qkvdt", tm=1024, tn=PB_W // 2, tk=D_MODEL,
                       epilogue="also_bf16")
    dils = [dil for _, dil in DILATED_PATTERNS]
    wide = [_att_fwd((pb, pb16), dil, slopes) for dil in dils[1:]]
    att, lse = _att_fwd((pb, pb16), dils[0], slopes, others=wide)
    xa, xc = _conv_fwd(pa, conv_w, conv_b)
    dtx, csx, cst = _dt_fwd(pb, dt_bias_pad, alog_pad)
    cst_g = jnp.pad(cst[:SSM_HEADS].reshape(SSM_GROUPS, hg, t), ((0, 0), (0, 8 - hg), (0, 0)))
    y_ssd, states = _ssd_fwd(xa, dtx, csx, cst_g)
    y4 = _gate_norm_fwd(y_ssd, xa, pa, dskip_x, ssm_norm_w)
    w_att, w_ssm, w_out, w_up, w_down = late_weights(y4)
    att_p = mm(att, w_att, mode="nn", out_dtype=BF16, name="att_proj", tn=D_MODEL, tk=ATT_WIDTH)
    ssm_p = mm(y4, w_ssm, mode="nn", out_dtype=BF16, name="ssm_proj", tn=D_MODEL, tk=SSM_INNER)
    mixin = _gating_fwd(pa, b_gate, att_p, ssm_p)
    mixed = mm(mixin, w_out, mode="nn", out_dtype=F32, name="out_proj", tn=D_MODEL, tk=D_MODEL)
    h1, f = _mix_post_ffn_pre(x, mixed, w_post, w_fpre)
    act, up = _matmul(f, w_up, mode="nn", out_dtype=BF16, name="ffn_up", tm=2048, tn=FFN_HIDDEN // N_DEV, tk=D_MODEL, epilogue="relu2", stacked=True)
    dn = mm(act, w_down, mode="nn", out_dtype=F32, name="ffn_down", tn=D_MODEL, tk=FFN_HIDDEN)
    loss, g_h2, g_dn, gw_fpost = _loss_and_ffn_post_bwd(h1, dn, w_fpost, target)

    g_up = _matmul(g_dn, w_down, mode="nt", out_dtype=BF16, name="ffn_down_bwd_x", tm=1024, tn=2048, tk=D_MODEL, epilogue="relu2_bwd",
                   extra=up)
    gw_down = _matmul(act, g_dn, mode="tn", out_dtype=BF16, name="ffn_down_bwd_w", tm=1024, tn=D_MODEL, tk=2048)
    w_up_rows = jnp.moveaxis(w_up, 0, 1).reshape(D_MODEL, FFN_HIDDEN)
    g_f = mm(g_up, w_up_rows, mode="nt", out_dtype=F32, name="ffn_up_bwd_x", tn=D_MODEL, tk=FFN_HIDDEN)
    gw_up = _matmul(f, g_up, mode="tn", out_dtype=BF16, name="ffn_up_bwd_w", tm=D_MODEL, tn=FFN_HIDDEN // N_DEV, tk=2048, stacked=True)
    g_h1, g_mixed, gw_fpre, gw_post = _ffn_pre_mix_post_bwd(g_h2, g_f, h1, w_fpre, mixed, w_post)
    g_mixin = mm(g_mixed, w_out, mode="nt", out_dtype=BF16, name="out_proj_bwd_x", tn=D_MODEL, tk=D_MODEL)
    gw_out = _matmul(mixin, g_mixed, mode="tn", out_dtype=BF16, name="out_proj_bwd_w", tm=D_MODEL, tn=D_MODEL, tk=2048)
    g_proj = lax.empty((t, PROJ_W), BF16)
    g_att_p, g_ssm_p, g_b_gate, g_proj = _gating_bwd(g_mixin, pa, b_gate, att_p, ssm_p, g_proj)
    g_att = mm(g_att_p, w_att, mode="nt", out_dtype=F32, name="att_proj_bwd_x", tn=ATT_WIDTH, tk=D_MODEL)
    gw_att = _matmul(att, g_att_p, mode="tn", out_dtype=BF16, name="att_proj_bwd_w", tm=ATT_WIDTH, tn=D_MODEL, tk=2048)
    g_y4 = mm(g_ssm_p, w_ssm, mode="nt", out_dtype=BF16, name="ssm_proj_bwd_x", tn=SSM_INNER, tk=D_MODEL)
    gw_ssm = _matmul(y4, g_ssm_p, mode="tn", out_dtype=BF16, name="ssm_proj_bwd_w", tm=1024, tn=D_MODEL, tk=2048)
    token = on_mid_grads(dict(w_att_proj=gw_att, w_ssm_proj=gw_ssm, w_out=gw_out, w_up=gw_up, w_down=gw_down))
    if token is not None:
        ssm_norm_w = ssm_norm_w + jnp.tile(token[0:1, :], (1, SSM_INNER // LANE))
    g_y2, g_norm_w, _, g_d_skip, g_proj = _gate_norm_bwd(g_y4, y_ssd, xa, pa, dskip_x, ssm_norm_w, g_proj)
    g_xs, g_bm, g_cm, g_dtx, ga_rows = _ssd_bwd(xa, dtx, csx, cst_g, alog_x, g_y2, states, dskip_x)
    g_dt_bias, g_a_log, g_proj = _dt_bwd(g_dtx, ga_rows, pb, dt_bias_pad, g_proj)
    g_conv_b, gcw0, gcw1, gcw2, gcw3, g_proj = _conv_bwd(g_xs, g_bm, g_cm, xc, pa, conv_w, g_proj)
    delta = _att_delta(g_att, att)
    dqs, dks, dvs = [], [], []
    for _, dil in DILATED_PATTERNS:
        dq, dk, dv = _att_bwd((pb, pb16), g_att, lse, delta, dil, slopes)
        dqs.append(dq)
        dks.append(dk)
        dvs.append(dv)
    g_proj = _att_grad_sum(dqs, dks, dvs, g_proj)
    gw_in_r = _matmul(u, g_proj, mode="tn", out_dtype=BF16, name="in_proj_bwd_w", tm=D_MODEL, tn=1792, tk=2048)
    token = on_in_proj_grads(gw_in_r, jnp.concatenate([gcw0, gcw1, gcw2, gcw3], axis=0))
    g_u = _matmul(g_proj, w_in_r, mode="nt", out_dtype=F32, name="in_proj_bwd_x", tm=1024, tn=D_MODEL, tk=3584, after=token)
    g_x, gw_pre = _pre_norm_bwd(g_h1, g_u, x, w_pre)

    grads = dict(
        norm_mix_pre_w=gw_pre, b_gate=g_b_gate, conv_b=g_conv_b, dt_bias=g_dt_bias[:, :SSM_HEADS], a_log=g_a_log[:, :SSM_HEADS],
        d_skip=g_d_skip[:, :SSM_HEADS], ssm_norm_w=g_norm_w, norm_mix_post_w=gw_post, norm_ffn_pre_w=gw_fpre, norm_ffn_post_w=gw_fpost)
    return loss, g_x, grads


def _mesh_pos():
    return lax.axis_index("x"), lax.axis_index("y"), lax.axis_index("c")


def _all_gather(shards):
    n = len(shards)

    def body(*refs):
        x_refs, o_refs = refs[:n], refs[n:2 * n]
        send_sems, recv_sems, local_sems = refs[2 * n:]
        x, y, c = _mesh_pos()
        me, sibling = (x, y, c), (x, y, 1 - c)
        xn, yn, dg = (1 - x, y), (x, 1 - y), (1 - x, 1 - y)
        north = c == 1
        via = (jnp.where(north, 1 - x, x), jnp.where(north, y, 1 - y))
        onto = (jnp.where(north, x, 1 - x), jnp.where(north, 1 - y, y))

        def copy(a, k, block, to, src=None):
            dst = o_refs[a].at[4 * block[0] + 2 * block[1] + block[2]]
            return pltpu.make_async_remote_copy(
                src_ref=dst if src is None else src, dst_ref=dst, send_sem=send_sems.at[7 * a + k], recv_sem=recv_sems.at[7 * a + k],
                device_id=to, device_id_type=pl.DeviceIdType.MESH)

        mine = [pltpu.make_async_copy(x_refs[a], o_refs[a].at[4 * x + 2 * y + c], local_sems.at[a]) for a in range(n)]
        for cp in mine:
            cp.start()
        sent = []
        for a in range(n):
            sent += [copy(a, 0, me, sibling, src=x_refs[a]), copy(a, 1, me, (*xn, c), src=x_refs[a]), copy(a, 2, me, (*yn, c), src=x_refs[a])]
        for cp in sent:
            cp.start()
        later = []
        for a in range(n):
            copy(a, 1, (*xn, c), me).wait_recv()
            copy(a, 2, (*yn, c), me).wait_recv()
            later += [copy(a, 3, (*via, c), (*onto, c)), copy(a, 4, (*xn, c), sibling), copy(a, 5, (*yn, c), sibling)]
        for cp in later:
            cp.start()
        for a in range(n):
            copy(a, 3, (*dg, c), me).wait_recv()
            later.append(copy(a, 6, (*dg, c), sibling))
            later[-1].start()
        for a in range(n):
            copy(a, 0, sibling, me).wait_recv()
            for k, chip in ((4, xn), (5, yn), (6, dg)):
                copy(a, k, (*chip, 1 - c), me).wait_recv()
        for cp in sent + later:
            cp.wait_send()
        for cp in mine:
            cp.wait()

    hbm = pl.BlockSpec(memory_space=pltpu.HBM)
    return pl.pallas_call(
        body, out_shape=[S((N_DEV,) + s.shape, s.dtype) for s in shards],
        in_specs=[hbm] * n, out_specs=[hbm] * n,
        scratch_shapes=[pltpu.SemaphoreType.DMA((7 * n,)), pltpu.SemaphoreType.DMA((7 * n,)), pltpu.SemaphoreType.DMA((n,))],
        name="weights_all_gather",
    )(*shards)


def _exchange_grads(slab_arrays, small):
    n = len(slab_arrays)
    r_small = small.shape[0]

    def body(*refs):
        slab_refs, small_ref = refs[:n], refs[n]
        recv_refs, gsm_ref = refs[n + 1:2 * n + 1], refs[2 * n + 1]
        send_sems, recv_sems, local_sems = refs[2 * n + 2:]
        x, y, c = _mesh_pos()
        me = 4 * x + 2 * y + c

        def peer(k):
            px = 1 - x if k & 4 else x
            py = 1 - y if k & 2 else y
            pc = 1 - c if k & 1 else c
            return (px, py, pc), 4 * px + 2 * py + pc

        def copy(a, k, sending):
            to, lin = peer(k)
            sem = 7 * a + k - 1
            if a == n:
                src, dst = small_ref, gsm_ref.at[me if sending else lin]
            else:
                src, dst = slab_refs[a].at[lin], recv_refs[a].at[me if sending else lin]
            return pltpu.make_async_remote_copy(src_ref=src, dst_ref=dst, send_sem=send_sems.at[sem], recv_sem=recv_sems.at[sem],
                                                device_id=to, device_id_type=pl.DeviceIdType.MESH)

        own = [pltpu.make_async_copy(slab_refs[a].at[me], recv_refs[a].at[me], local_sems.at[a]) for a in range(n)]
        own.append(pltpu.make_async_copy(small_ref, gsm_ref.at[me], local_sems.at[n]))
        for cp in own:
            cp.start()
        order = [n] + list(range(n))
        sends = [copy(a, k, True) for a in order for k in range(1, N_DEV)]
        for cp in sends:
            cp.start()
        for a in order:
            for k in range(1, N_DEV):
                copy(a, k, False).wait_recv()
        for cp in sends:
            cp.wait_send()
        for cp in own:
            cp.wait()

    hbm = pl.BlockSpec(memory_space=pltpu.HBM)
    n_sem = 7 * (n + 1)
    res = pl.pallas_call(
        body, out_shape=[S(a.shape, a.dtype) for a in slab_arrays] + [S((N_DEV, r_small, LANE), small.dtype)],
        in_specs=[hbm] * (n + 1), out_specs=[hbm] * (n + 1),
        scratch_shapes=[pltpu.SemaphoreType.DMA((n_sem,)), pltpu.SemaphoreType.DMA((n_sem,)), pltpu.SemaphoreType.DMA((n + 1,))],
        name="grad_exchange",
    )(*slab_arrays, small)
    return res[:n], res[n]


def _peer_of(k, x, y, c):
    px = 1 - x if k & 4 else x
    py = 1 - y if k & 2 else y
    pc = 1 - c if k & 1 else c
    return (px, py, pc), 4 * px + 2 * py + pc


def _split_copies(src_refs, land_refs, send_sems, recv_sems, per_peer):
    x, y, c = _mesh_pos()
    me = 4 * x + 2 * y + c
    sends, recvs = [], []
    for a, (src, land) in enumerate(zip(src_refs, land_refs)):
        for k in range(1, N_DEV):
            to, lin = _peer_of(k, x, y, c)
            sem = 7 * a + k - 1
            piece = src.at[lin] if per_peer else src
            for slot, out in ((me, sends), (lin, recvs)):
                out.append(pltpu.make_async_remote_copy(
                    src_ref=piece, dst_ref=land.at[slot], send_sem=send_sems.at[sem], recv_sem=recv_sems.at[sem],
                    device_id=to, device_id_type=pl.DeviceIdType.MESH))
    return sends, recvs


def _remote_start(srcs, per_peer, name):
    n = len(srcs)
    lands = [lax.empty((N_DEV,) + (s.shape[1:] if per_peer else s.shape), s.dtype) for s in srcs]

    def body(*refs):
        src_refs, land_refs = refs[:n], refs[n:2 * n]
        send_sems, recv_sems = refs[2 * n], refs[2 * n + 1]
        token = refs[-1]
        sends, _ = _split_copies(src_refs, land_refs, send_sems, recv_sems, per_peer)
        for cp in sends:
            cp.start()
        token[...] = jnp.zeros_like(token)

    hbm = pl.BlockSpec(memory_space=pltpu.HBM)
    sem = pl.BlockSpec(memory_space=pltpu.SEMAPHORE)
    res = pl.pallas_call(
        body, name=name,
        out_shape=(pltpu.SemaphoreType.DMA((7 * n,)), pltpu.SemaphoreType.DMA((7 * n,)),
                   *[pltpu.HBM(a.shape, a.dtype) for a in srcs + lands], S((8, LANE), F32)),
        in_specs=[hbm] * (2 * n), out_specs=(sem, sem, *[hbm] * (2 * n), pl.BlockSpec(memory_space=pltpu.VMEM)),
        input_output_aliases={i: 2 + i for i in range(2 * n)},
        compiler_params=pltpu.CompilerParams(has_side_effects=pltpu.SideEffectType.DATAFLOW_SIDE_EFFECTING),
    )(*[pltpu.with_memory_space_constraint(a, pltpu.HBM) for a in srcs + lands])
    return dict(sems=res[:2], srcs=list(res[2:2 + n]), lands=list(res[2 + n:2 + 2 * n]), per_peer=per_peer), res[-1]


def _remote_wait(handle, after, name):
    n = len(handle["srcs"])
    per_peer = handle["per_peer"]

    def body(*refs):
        src_refs, land_refs = refs[:n], refs[n:2 * n]
        send_sems, recv_sems = refs[2 * n], refs[2 * n + 1]
        sends, recvs = _split_copies(src_refs, land_refs, send_sems, recv_sems, per_peer)
        for cp in sends:
            cp.wait_send()
        for cp in recvs:
            cp.wait_recv()

    hbm = pl.BlockSpec(memory_space=pltpu.HBM)
    sem = pl.BlockSpec(memory_space=pltpu.SEMAPHORE)
    arrays = handle["srcs"] + handle["lands"]
    res = pl.pallas_call(
        body, name=name, out_shape=tuple(pltpu.HBM(a.shape, a.dtype) for a in arrays),
        in_specs=[hbm] * (2 * n) + [sem, sem, pl.BlockSpec(memory_space=pl.ANY)], out_specs=tuple([hbm] * (2 * n)),
        input_output_aliases={i: i for i in range(2 * n)},
        compiler_params=pltpu.CompilerParams(has_side_effects=pltpu.SideEffectType.DATAFLOW_SIDE_EFFECTING),
    )(*arrays, *handle["sems"], after)
    return list(res[n:])


def _with_own(lands, own, me):
    return [lax.dynamic_update_index_in_dim(land, o.astype(land.dtype), me, 0) for land, o in zip(lands, own)]


def _adamw(w, m, v, slabs, name, tr):
    r, cols = w.shape
    c1 = 1.0 - ADAM_B1 ** ADAM_STEP
    c2 = 1.0 - ADAM_B2 ** ADAM_STEP

    def body(w_ref, m_ref, v_ref, s_ref, g_ref, d_ref, nm_ref, nv_ref):
        g = s_ref[0].astype(F32)
        for d in range(1, N_DEV):
            g = g + s_ref[d].astype(F32)
        nm = ADAM_B1 * m_ref[...] + (1.0 - ADAM_B1) * g
        nv = ADAM_B2 * v_ref[...] + (1.0 - ADAM_B2) * (g * g)
        g_ref[...] = g
        nm_ref[...] = nm
        nv_ref[...] = nv
        d_ref[...] = -ADAM_LR * ((nm / c1) / (jnp.sqrt(nv / c2) + ADAM_EPS) + ADAM_WD * w_ref[...])

    assert r % tr == 0, name
    blk = pl.BlockSpec((tr, cols), lambda i: (i, 0))
    return pl.pallas_call(
        body, grid=(r // tr,), in_specs=[blk, blk, blk, pl.BlockSpec((N_DEV, tr, cols), lambda i: (0, i, 0))],
        out_specs=[blk] * 4, out_shape=[S((r, cols), F32)] * 4, name=name, compiler_params=_params(("parallel",)),
    )(w, m, v, slabs)


BIG = ("w_in", "w_att_proj", "w_up", "w_ssm_proj", "w_out", "w_down", "conv_w")
ADAMW_ROWS = dict(w_in=256, w_att_proj=768, w_up=512, w_ssm_proj=256, w_out=128, w_down=256, conv_w=4)
SMALL = ("norm_mix_pre_w", "b_gate", "conv_b", "dt_bias", "a_log", "d_skip", "ssm_norm_w", "norm_mix_post_w",
         "norm_ffn_pre_w", "norm_ffn_post_w")
ORDER = ("norm_mix_pre_w", "w_in", "b_gate", "conv_w", "conv_b", "dt_bias", "a_log", "d_skip", "ssm_norm_w", "w_att_proj",
         "w_ssm_proj", "w_out", "norm_mix_post_w", "norm_ffn_pre_w", "w_up", "w_down", "norm_ffn_post_w")
ROW_SHARDED = ("w_ssm_proj", "w_out", "w_down")
LATE = ("w_att_proj", "w_ssm_proj", "w_out", "w_up", "w_down")
IN_PROJ_W = 10528
IN_SHARD_W = IN_PROJ_W // N_DEV
IN_SEGMENTS = ((2304, 4352), (8480, 10528), (4352, 8448), (0, 2304), (8448, 8480))


def _pack(parts, rows_multiple):
    flat = jnp.concatenate([p.reshape(-1) for p in parts])
    pad = (-flat.shape[0]) % (rows_multiple * LANE)
    return jnp.pad(flat, (0, pad)).reshape(-1, LANE)


def _unpack(flat2d, shapes):
    flat, out, off = flat2d.reshape(-1), [], 0
    for sh in shapes:
        n = int(np.prod(sh))
        out.append(flat[off:off + n].reshape(sh))
        off += n
    return out


def _restore_in_proj(wr):
    return jnp.concatenate([wr[:, OFF_QKV:OFF_QKV + 2304], wr[:, OFF_Z:OFF_Z + 2048], wr[:, OFF_XBC:OFF_XBC + 4096],
                            wr[:, OFF_DT:OFF_DT + 32], wr[:, OFF_GL:OFF_GL + 2048]], axis=1)


def _assemble_in_proj(g):
    pieces = []
    for lo, hi in IN_SEGMENTS:
        while lo < hi:
            d = lo // IN_SHARD_W
            end = min(hi, (d + 1) * IN_SHARD_W)
            pieces.append(g[d][:, lo - d * IN_SHARD_W:end - d * IN_SHARD_W])
            lo = end
    pieces.append(jnp.zeros((g.shape[1], PROJ_W - IN_PROJ_W), g.dtype))
    return jnp.concatenate(pieces, axis=1)


def _in_proj_slabs(wr):
    orig = _restore_in_proj(wr)
    return jnp.stack([orig[:, d * IN_SHARD_W:(d + 1) * IN_SHARD_W] for d in range(N_DEV)])


def kernel(x, norm_mix_pre_w, w_in, b_gate, conv_w, conv_b, dt_bias, a_log, d_skip, ssm_norm_w, w_att_proj, w_ssm_proj, w_out, norm_mix_post_w, norm_ffn_pre_w, w_up, w_down, norm_ffn_post_w, loss_target, m_norm_mix_pre_w, m_w_in, m_b_gate, m_conv_w, m_conv_b, m_dt_bias, m_a_log, m_d_skip, m_ssm_norm_w, m_w_att_proj, m_w_ssm_proj, m_w_out, m_norm_mix_post_w, m_norm_ffn_pre_w, m_w_up, m_w_down, m_norm_ffn_post_w, v_norm_mix_pre_w, v_w_in, v_b_gate, v_conv_w, v_conv_b, v_dt_bias, v_a_log, v_d_skip, v_ssm_norm_w, v_w_att_proj, v_w_ssm_proj, v_w_out, v_norm_mix_post_w, v_norm_ffn_pre_w, v_w_up, v_w_down, v_norm_ffn_post_w):
    w = dict(norm_mix_pre_w=norm_mix_pre_w, w_in=w_in, b_gate=b_gate, conv_w=conv_w, conv_b=conv_b, dt_bias=dt_bias, a_log=a_log,
             d_skip=d_skip, ssm_norm_w=ssm_norm_w, w_att_proj=w_att_proj, w_ssm_proj=w_ssm_proj, w_out=w_out,
             norm_mix_post_w=norm_mix_post_w, norm_ffn_pre_w=norm_ffn_pre_w, w_up=w_up, w_down=w_down, norm_ffn_post_w=norm_ffn_post_w)
    m = dict(norm_mix_pre_w=m_norm_mix_pre_w, w_in=m_w_in, b_gate=m_b_gate, conv_w=m_conv_w, conv_b=m_conv_b, dt_bias=m_dt_bias,
             a_log=m_a_log, d_skip=m_d_skip, ssm_norm_w=m_ssm_norm_w, w_att_proj=m_w_att_proj, w_ssm_proj=m_w_ssm_proj, w_out=m_w_out,
             norm_mix_post_w=m_norm_mix_post_w, norm_ffn_pre_w=m_norm_ffn_pre_w, w_up=m_w_up, w_down=m_w_down, norm_ffn_post_w=m_norm_ffn_post_w)
    v = dict(norm_mix_pre_w=v_norm_mix_pre_w, w_in=v_w_in, b_gate=v_b_gate, conv_w=v_conv_w, conv_b=v_conv_b, dt_bias=v_dt_bias,
             a_log=v_a_log, d_skip=v_d_skip, ssm_norm_w=v_ssm_norm_w, w_att_proj=v_w_att_proj, w_ssm_proj=v_w_ssm_proj, w_out=v_w_out,
             norm_mix_post_w=v_norm_mix_post_w, norm_ffn_pre_w=v_norm_ffn_pre_w, w_up=v_w_up, w_down=v_w_down, norm_ffn_post_w=v_norm_ffn_post_w)
    shard_shapes = {n: w[n].shape[1:] for n in ORDER}

    mx, my, mc = _mesh_pos()
    me = 4 * mx + 2 * my + mc

    g_in, g_conv = _all_gather([w["w_in"][0].astype(BF16), w["conv_w"][0]])
    conv_full = jnp.moveaxis(g_conv, 0, 1).reshape(SSM_CONV, CONV_DIM)
    late_shards = [w[n][0].astype(BF16) for n in LATE]
    late_handle, token = _remote_start(late_shards, False, "late_weights_start")
    w_pre = w["norm_mix_pre_w"] + jnp.tile(token[0:1, :], (1, D_MODEL // LANE))

    def late_weights(after):
        full = dict(zip(LATE, _with_own(_remote_wait(late_handle, after, "late_weights_wait"), late_shards, me)))
        for n in ROW_SHARDED:
            full[n] = full[n].reshape(-1, full[n].shape[2])
        w_att = jnp.moveaxis(full["w_att_proj"], 0, 1).reshape(ATT_WIDTH, D_MODEL)
        return w_att, full["w_ssm_proj"], full["w_out"], full["w_up"], full["w_down"]

    started = {}

    def start_exchange(tag, slabs):
        own = [lax.dynamic_index_in_dim(s, me, 0, keepdims=False) for s in slabs]
        handle, tok = _remote_start(slabs, True, tag + "_grads_start")
        started[tag] = (handle, own)
        return tok

    def on_mid_grads(g):
        slabs = dict(w_up=g["w_up"], w_att_proj=jnp.moveaxis(g["w_att_proj"].reshape(ATT_WIDTH, N_DEV, -1), 1, 0))
        for n in ROW_SHARDED:
            slabs[n] = g[n].reshape(N_DEV, -1, g[n].shape[1])
        return start_exchange("mid", [slabs[n] for n in LATE])

    def on_in_proj_grads(gw_in_r, g_conv_w):
        return start_exchange("in_proj", [_in_proj_slabs(gw_in_r), jnp.moveaxis(g_conv_w.reshape(SSM_CONV, N_DEV, -1), 1, 0)])

    loss, g_x, grads = _local_step(
        x[0], loss_target[0], w_pre, _assemble_in_proj(g_in), w["b_gate"], conv_full, w["conv_b"], w["dt_bias"], w["a_log"],
        w["d_skip"], w["ssm_norm_w"], late_weights, w["norm_mix_post_w"], w["norm_ffn_pre_w"], w["norm_ffn_post_w"],
        on_mid_grads, on_in_proj_grads)

    recv = {}
    for tag, names in (("mid", LATE), ("in_proj", ("w_in", "conv_w"))):
        handle, own = started[tag]
        recv.update(zip(names, _with_own(_remote_wait(handle, g_x, tag + "_grads_wait"), own, me)))
    small = _pack([grads[n].astype(F32) for n in SMALL], 8)
    _, small_all = _exchange_grads([], small)

    small_shapes = [shard_shapes[n] for n in SMALL]
    small_out = _adamw(*[_pack([d_[n][0] for n in SMALL], 8) for d_ in (w, m, v)], small_all, "adamw_replicated", small_all.shape[1])
    big_out = {n: _adamw(w[n][0], m[n][0], v[n][0], recv[n], "adamw_" + n, ADAMW_ROWS[n]) for n in BIG}
    res = []
    for which, small_flat in enumerate(small_out):
        vals = {n: big_out[n][which] for n in BIG}
        vals.update(zip(SMALL, _unpack(small_flat, small_shapes)))
        res.append([vals[n][None] for n in ORDER])
    g_out, d_out, m_out, v_out = res
    total = lax.psum(loss[0, 0], ("x", "y", "c"))
    return (total, g_x[None], *g_out, *d_out, *m_out, *v_out)
```
